```python
import jax, jax.numpy as jnp
from jax import lax
import numpy as np

D_MODEL = 1024
BATCH = 4
SEQ = 4096
DEPTH = 1
DEC_BATCH = 32
DEC_SEQ = 1
PAST_LEN = 16384
PAGE_SIZE = 128

CONV_WIDTH = D_MODEL // 2
ATTN_WIDTH = D_MODEL - CONV_WIDTH
HEAD_DIM = 64
N_HEADS = ATTN_WIDTH // HEAD_DIM
N_KV_HEADS = 2
Q_PER_KV = N_HEADS // N_KV_HEADS
KV_WIDTH = N_KV_HEADS * HEAD_DIM
CONV_K = 3
CMP_BLOCK = 32
CMP_STRIDE = 16
CMP_CHUNKS = CMP_BLOCK // CMP_STRIDE
CMP_HIDDEN = 2 * HEAD_DIM
SEL_BLOCK = 64
N_SEL = 16
WINDOW = 512
Q_BLOCK = 128
ROPE_THETA = 10000.0
N_GROUPS = 4
EXPERTS_PER_GROUP = 8
N_EXPERTS = N_GROUPS * EXPERTS_PER_GROUP
D_EXPERT = D_MODEL // 2
TOP_K_IN_GROUP = 2
MOE_BLOCK = 128
NORM_EPS = 1e-6
NEG_INF = -1e30
FORCE_SCORE = 1e4
IN_SPLITS = (CONV_WIDTH, CONV_WIDTH, CONV_WIDTH, ATTN_WIDTH, 2 * KV_WIDTH, 2 * KV_WIDTH, 2 * KV_WIDTH, 3 * N_HEADS)
D_IN = sum(IN_SPLITS)

kernel_name = 'hymba_conv_nsa_hmoe_step'


def rms_norm(x, g):
    xf = x.astype(jnp.float32)
    y = xf * lax.rsqrt(jnp.mean(xf * xf, axis=-1, keepdims=True) + NORM_EPS)
    return (y * g.astype(jnp.float32)).astype(x.dtype)


def rope(x, pos):
    half = HEAD_DIM // 2
    inv = ROPE_THETA ** (-jnp.arange(half, dtype=jnp.float32) / half)
    ang = pos.astype(jnp.float32)[:, None] * inv[None, :]
    cos = jnp.cos(ang)[:, None, :]
    sin = jnp.sin(ang)[:, None, :]
    xf = x.astype(jnp.float32)
    x1, x2 = xf[..., :half], xf[..., half:]
    return jnp.concatenate([x1 * cos - x2 * sin, x1 * sin + x2 * cos], axis=-1).astype(x.dtype)


def rope_kv(kv, pos):
    return jnp.stack([rope(kv[:, :, 0], pos), kv[:, :, 1]], axis=2)


def adaln(c, w_ada, b_ada):
    m = jax.nn.silu(c) @ w_ada + b_ada
    return jnp.split(m[:, None, :], 6, axis=-1)


def project(h, w_in):
    B, T, _ = h.shape
    z = h @ w_in
    cuts = np.cumsum(IN_SPLITS)[:-1].tolist()
    b_g, c_g, u, q, kvc, kvs, kvw, gl = jnp.split(z, cuts, axis=-1)
    kv = lambda t: t.reshape(B, T, 2, N_KV_HEADS, HEAD_DIM)
    gates = jax.nn.sigmoid(gl.astype(jnp.float32)).reshape(B, T, 3, N_KV_HEADS, Q_PER_KV)
    return b_g, c_g, u, q.reshape(B, T, N_HEADS, HEAD_DIM), kv(kvc), kv(kvs), kv(kvw), gates


def short_conv(b_g, c_g, u, prev, w_conv):
    v = c_g * u
    ext = jnp.concatenate([prev, v], axis=1)
    T = v.shape[1]
    y = sum(w_conv[j] * ext[:, j:j + T] for j in range(CONV_K))
    return b_g * y, ext[:, -(CONV_K - 1):]


def compress_blocks(kv_raw, cmp_pos, cmp_w1, cmp_b1, cmp_w2, cmp_b2):
    B, L = kv_raw.shape[:2]
    n_chunk = L // CMP_STRIDE
    n_cmp = n_chunk - CMP_CHUNKS + 1
    chunks = kv_raw[:, :n_chunk * CMP_STRIDE].reshape(B, n_chunk, CMP_STRIDE, 2, N_KV_HEADS, HEAD_DIM)
    w1 = cmp_w1.reshape(2, CMP_CHUNKS, CMP_STRIDE, HEAD_DIM, CMP_HIDDEN)
    proj = jnp.einsum('bnsckd,crsdh->bnrckh', chunks, w1)
    hid = sum(proj[:, m:m + n_cmp, m] for m in range(CMP_CHUNKS))
    bias = jnp.einsum('cld,cldh->ch', cmp_pos, cmp_w1) + cmp_b1
    hid = jax.nn.gelu(hid + bias[:, None, :])
    return jnp.einsum('bnckh,chd->bnckd', hid, cmp_w2) + cmp_b2[:, None, :]


def nsa_keys(kv_cmp_raw, kv_sel, cmp_params):
    B, L = kv_sel.shape[:2]
    comp = compress_blocks(kv_cmp_raw, *cmp_params)
    n_cmp = comp.shape[1]
    cmp_end = (jnp.arange(n_cmp, dtype=jnp.int32) + CMP_CHUNKS) * CMP_STRIDE - 1
    ck = rope(comp[:, :, 0], cmp_end)
    cv = comp[:, :, 1]
    n_sel = -(-L // SEL_BLOCK)
    sel = jnp.pad(kv_sel, ((0, 0), (0, n_sel * SEL_BLOCK - L), (0, 0), (0, 0), (0, 0)))
    sel = sel.reshape(B, n_sel, SEL_BLOCK, 2, N_KV_HEADS, HEAD_DIM).transpose(0, 4, 1, 2, 3, 5)
    return ck, cv, cmp_end, sel


def cmp_to_sel(imp, n_sel):
    r = SEL_BLOCK // CMP_STRIDE
    f = CMP_CHUNKS - 1
    n_cmp = imp.shape[-1]
    pad_back = r * (n_sel + 1) - f - n_cmp
    p = jnp.pad(imp, [(0, 0)] * (imp.ndim - 1) + [(f, pad_back)])
    p = p.reshape(imp.shape[:-1] + (n_sel + 1, r))
    return p[..., :n_sel, :].sum(-1) + p[..., 1:, :f].sum(-1)


def nsa_attend(q, q_pos, gates, ck, cv, cmp_end, sel_blocks, win_kv, win_pos):
    B, Tq = q.shape[:2]
    scale = HEAD_DIM ** -0.5
    qg = q.reshape(B, Tq, N_KV_HEADS, Q_PER_KV, HEAD_DIM)
    s_c = jnp.einsum('btgqd,bngd->btgqn', qg, ck).astype(jnp.float32) * scale
    valid_c = (cmp_end[None, :] <= q_pos[:, None])[None, :, None, None, :]
    p_c = jax.nn.softmax(jnp.where(valid_c, s_c, NEG_INF), axis=-1) * valid_c
    o_c = jnp.einsum('btgqn,bngd->btgqd', p_c.astype(cv.dtype), cv)
    n_sel = sel_blocks.shape[2]
    imp = cmp_to_sel(p_c.sum(axis=3), n_sel)
    blk = jnp.arange(n_sel, dtype=jnp.int32)[None, :]
    cur = (q_pos // SEL_BLOCK)[:, None]
    causal_blk = blk * SEL_BLOCK <= q_pos[:, None]
    forced = causal_blk & ((blk == 0) | (blk == cur) | (blk == cur - 1))
    score = jnp.where(forced[None, :, None, :], FORCE_SCORE,
                      jnp.where(causal_blk[None, :, None, :], imp, -1.0))
    _, idx = lax.top_k(score, min(N_SEL, n_sel))
    gathered = jax.vmap(jax.vmap(lambda blocks, ix: blocks[ix]))(sel_blocks, idx.transpose(0, 2, 1, 3))
    s_s = jnp.einsum('btgqd,bgtnkd->btgqnk', qg, gathered[..., 0, :]).astype(jnp.float32) * scale
    tok_pos = idx[..., None] * SEL_BLOCK + jnp.arange(SEL_BLOCK, dtype=jnp.int32)
    valid_s = (tok_pos <= q_pos[None, :, None, None, None])[:, :, :, None]
    s_s = jnp.where(valid_s, s_s, NEG_INF)
    p_s = jax.nn.softmax(s_s.reshape(s_s.shape[:4] + (-1,)), axis=-1).reshape(s_s.shape)
    o_s = jnp.einsum('btgqnk,bgtnkd->btgqd', p_s.astype(gathered.dtype), gathered[..., 1, :])
    s_w = jnp.einsum('btgqd,bsgd->btgqs', qg, win_kv[:, :, 0]).astype(jnp.float32) * scale
    dist = q_pos[:, None] - win_pos[None, :]
    valid_w = ((dist >= 0) & (dist <= WINDOW) & (win_pos[None, :] >= 0))[None, :, None, None, :]
    p_w = jax.nn.softmax(jnp.where(valid_w, s_w, NEG_INF), axis=-1)
    o_w = jnp.einsum('btgqs,bsgd->btgqd', p_w.astype(win_kv.dtype), win_kv[:, :, 1])
    g = gates.astype(q.dtype)[..., None]
    o = g[:, :, 0] * o_c + g[:, :, 1] * o_s + g[:, :, 2] * o_w
    return o.reshape(B, Tq, ATTN_WIDTH)


def merge_groups(conv_out, attn_out, g_out_conv, g_out_attn, w_out):
    return jnp.concatenate([rms_norm(conv_out, g_out_conv), rms_norm(attn_out, g_out_attn)], axis=-1) @ w_out


def mixer_prompt(h, w_in, w_conv, cmp_params, g_out_conv, g_out_attn, w_out):
    B, T, _ = h.shape
    b_g, c_g, u, q, kvc, kvs, kvw, gates = project(h, w_in)
    conv_out, conv_state = short_conv(b_g, c_g, u, jnp.zeros((B, CONV_K - 1, CONV_WIDTH), h.dtype), w_conv)
    pos = jnp.arange(T, dtype=jnp.int32)
    q = rope(q, pos)
    kvs = rope_kv(kvs, pos)
    kvw = rope_kv(kvw, pos)
    ck, cv, cmp_end, sel_blocks = nsa_keys(kvc, kvs, cmp_params)
    win_pad = jnp.pad(kvw, ((0, 0), (WINDOW, 0), (0, 0), (0, 0), (0, 0)))

    def query_block(i):
        start = i * Q_BLOCK
        sl = lambda t: lax.dynamic_slice_in_dim(t, start, Q_BLOCK, axis=1)
        win = lax.dynamic_slice_in_dim(win_pad, start, WINDOW + Q_BLOCK, axis=1)
        q_pos = start + jnp.arange(Q_BLOCK, dtype=jnp.int32)
        win_pos = start - WINDOW + jnp.arange(WINDOW + Q_BLOCK, dtype=jnp.int32)
        return nsa_attend(sl(q), q_pos, sl(gates), ck, cv, cmp_end, sel_blocks, win, win_pos)

    attn = lax.map(query_block, jnp.arange(T // Q_BLOCK, dtype=jnp.int32))
    attn = attn.transpose(1, 0, 2, 3).reshape(B, T, ATTN_WIDTH)
    y = merge_groups(conv_out, attn, g_out_conv, g_out_attn, w_out)
    return y, kvc, kvs, kvw[:, -min(WINDOW, T):], conv_state


def mixer_sample(h, cache_cmp, cache_sel, cache_win, conv_prev, page_table, w_in, w_conv, cmp_params,
                 g_out_conv, g_out_attn, w_out):
    B, T, _ = h.shape
    past = page_table.shape[1] * PAGE_SIZE
    b_g, c_g, u, q, kvc, kvs, kvw, gates = project(h, w_in)
    conv_out, conv_state = short_conv(b_g, c_g, u, conv_prev, w_conv)
    pos = past + jnp.arange(T, dtype=jnp.int32)
    q = rope(q, pos)
    kvs = rope_kv(kvs, pos)
    kvw = rope_kv(kvw, pos)
    gather_past = lambda pool: pool[page_table].reshape((B, past) + pool.shape[2:])
    full_cmp = jnp.concatenate([gather_past(cache_cmp), kvc], axis=1)
    full_sel = jnp.concatenate([gather_past(cache_sel), kvs], axis=1)
    ck, cv, cmp_end, sel_blocks = nsa_keys(full_cmp, full_sel, cmp_params)
    wb = cache_win.shape[1]
    win = jnp.concatenate([cache_win, kvw], axis=1)
    win_pos = past - wb + jnp.arange(wb + T, dtype=jnp.int32)
    attn = nsa_attend(q, pos, gates, ck, cv, cmp_end, sel_blocks, win, win_pos)
    y = merge_groups(conv_out, attn, g_out_conv, g_out_attn, w_out)
    return y, kvc, kvs, win[:, -min(WINDOW, wb + T):], conv_state


def grouped_experts(x, expert, weights, w_gate, w_up, w_down):
    T, D = x.shape
    K = expert.shape[1]
    M = T * K
    flat_e = expert.reshape(-1)
    order = jnp.argsort(flat_e)
    se = flat_e[order]
    tok = order // K
    sw = weights.reshape(-1)[order]
    counts = jnp.bincount(flat_e, length=N_EXPERTS)
    padded = (counts + MOE_BLOCK - 1) // MOE_BLOCK * MOE_BLOCK
    pad_end = jnp.cumsum(padded)
    pad_start = pad_end - padded
    start = jnp.cumsum(counts) - counts
    dest = pad_start[se] + jnp.arange(M, dtype=jnp.int32) - start[se]
    n_blocks = -(-(M + N_EXPERTS * (MOE_BLOCK - 1)) // MOE_BLOCK)
    slot_tok = jnp.full((n_blocks * MOE_BLOCK,), T, jnp.int32).at[dest].set(tok)
    xb = jnp.concatenate([x, jnp.zeros((1, D), x.dtype)], axis=0)[slot_tok].reshape(n_blocks, MOE_BLOCK, D)
    blk_e = jnp.minimum(jnp.searchsorted(pad_end, jnp.arange(n_blocks) * MOE_BLOCK, side='right'), N_EXPERTS - 1)

    def expert_block(args):
        xs, e = args
        return (jax.nn.silu(xs @ w_gate[e]) * (xs @ w_up[e])) @ w_down[e]

    yb = lax.map(expert_block, (xb, blk_e)).reshape(n_blocks * MOE_BLOCK, D)
    y = yb[dest] * sw[:, None].astype(yb.dtype)
    return jax.ops.segment_sum(y, tok, num_segments=T)


def hier_moe(x, w_rg, b_rg, w_re, b_re, w_gate, w_up, w_down):
    T = x.shape[0]
    rows = jnp.arange(T)
    lg = (x @ w_rg).astype(jnp.float32) + b_rg
    grp = jnp.argmax(lg, axis=-1)
    p_grp = jax.nn.softmax(lg, axis=-1)[rows, grp]
    le = ((x @ w_re).astype(jnp.float32) + b_re).reshape(T, N_GROUPS, EXPERTS_PER_GROUP)[rows, grp]
    top_v, top_i = lax.top_k(le, TOP_K_IN_GROUP)
    weights = p_grp[:, None] * jax.nn.softmax(top_v, axis=-1)
    expert = grp[:, None].astype(jnp.int32) * EXPERTS_PER_GROUP + top_i
    return grouped_experts(x, expert, weights, w_gate, w_up, w_down)


def setup_inputs(seed: int = 0) -> dict:
    key = jax.random.key(seed)
    ks = iter(jax.random.split(key, 40))
    nrm = lambda shape, s: jax.random.normal(next(ks), shape, jnp.float32) * s
    n_pages = PAST_LEN // PAGE_SIZE
    n_used = DEC_BATCH * n_pages
    n_pool = n_used + max(1, n_used // 4)
    win_buf = min(WINDOW, PAST_LEN)
    kv_row = (2, N_KV_HEADS, HEAD_DIM)
    page_table = jax.random.permutation(next(ks), n_pool)[:n_used].reshape(DEC_BATCH, n_pages).astype(jnp.int32)
    return {
        'x_prompt': nrm((BATCH, SEQ, D_MODEL), 1.0),
        'x_sample': nrm((DEC_BATCH, DEC_SEQ, D_MODEL), 1.0),
        'c_prompt': nrm((BATCH, D_MODEL), 1.0),
        'c_sample': nrm((DEC_BATCH, D_MODEL), 1.0),
        'cache_cmp_kv': nrm((DEPTH, n_pool, PAGE_SIZE) + kv_row, 1.0),
        'cache_sel_kv': nrm((DEPTH, n_pool, PAGE_SIZE) + kv_row, 1.0),
        'cache_win_kv': nrm((DEPTH, DEC_BATCH, win_buf) + kv_row, 1.0),
        'state_conv': nrm((DEPTH, DEC_BATCH, CONV_K - 1, CONV_WIDTH), 1.0),
        'page_table': page_table,
        'ln1_g': 1.0 + nrm((DEPTH, D_MODEL), 0.05),
        'ln2_g': 1.0 + nrm((DEPTH, D_MODEL), 0.05),
        'w_ada': nrm((DEPTH, D_MODEL, 6 * D_MODEL), 0.5 * D_MODEL ** -0.5),
        'b_ada': nrm((DEPTH, 6 * D_MODEL), 0.02),
        'w_in': nrm((DEPTH, D_MODEL, D_IN), D_MODEL ** -0.5),
        'w_conv': nrm((DEPTH, CONV_K, CONV_WIDTH), CONV_K ** -0.5),
        'cmp_pos': nrm((DEPTH, 2, CMP_BLOCK, HEAD_DIM), 0.1),
        'cmp_w1': nrm((DEPTH, 2, CMP_BLOCK, HEAD_DIM, CMP_HIDDEN), (CMP_BLOCK * HEAD_DIM) ** -0.5),
        'cmp_b1': nrm((DEPTH, 2, CMP_HIDDEN), 0.02),
        'cmp_w2': nrm((DEPTH, 2, CMP_HIDDEN, HEAD_DIM), CMP_HIDDEN ** -0.5),
        'cmp_b2': nrm((DEPTH, 2, HEAD_DIM), 0.02),
        'g_out_conv': 1.0 + nrm((DEPTH, CONV_WIDTH), 0.05),
        'g_out_attn': 1.0 + nrm((DEPTH, ATTN_WIDTH), 0.05),
        'w_out': nrm((DEPTH, D_MODEL, D_MODEL), D_MODEL ** -0.5),
        'w_route_group': nrm((DEPTH, D_MODEL, N_GROUPS), D_MODEL ** -0.5),
        'b_route_group': nrm((DEPTH, N_GROUPS), 0.01),
        'w_route_expert': nrm((DEPTH, D_MODEL, N_EXPERTS), D_MODEL ** -0.5),
        'b_route_expert': nrm((DEPTH, N_EXPERTS), 0.01),
        'w_gate': nrm((DEPTH, N_EXPERTS, D_MODEL, D_EXPERT), D_MODEL ** -0.5),
        'w_up': nrm((DEPTH, N_EXPERTS, D_MODEL, D_EXPERT), D_MODEL ** -0.5),
        'w_down': nrm((DEPTH, N_EXPERTS, D_EXPERT, D_MODEL), D_EXPERT ** -0.5),
        'final_g': 1.0 + nrm((D_MODEL,), 0.05),
    }


def reference(x_prompt, x_sample, c_prompt, c_sample, cache_cmp_kv, cache_sel_kv, cache_win_kv, state_conv,
              page_table, ln1_g, ln2_g, w_ada, b_ada, w_in, w_conv, cmp_pos, cmp_w1, cmp_b1, cmp_w2, cmp_b2,
              g_out_conv, g_out_attn, w_out, w_route_group, b_route_group, w_route_expert, b_route_expert,
              w_gate, w_up, w_down, final_g):
    xp, xs = x_prompt, x_sample
    cmp_p, cmp_s, sel_p, sel_s, win_p, win_s, conv_p, conv_s = [], [], [], [], [], [], [], []
    for l in range(DEPTH):
        sh1p, sc1p, ga1p, sh2p, sc2p, ga2p = adaln(c_prompt, w_ada[l], b_ada[l])
        sh1s, sc1s, ga1s, sh2s, sc2s, ga2s = adaln(c_sample, w_ada[l], b_ada[l])
        cmp_params = (cmp_pos[l], cmp_w1[l], cmp_b1[l], cmp_w2[l], cmp_b2[l])
        hp = rms_norm(xp, ln1_g[l]) * (1 + sc1p) + sh1p
        yp, kvc_p, kvs_p, kvw_p, cs_p = mixer_prompt(hp, w_in[l], w_conv[l], cmp_params,
                                                     g_out_conv[l], g_out_attn[l], w_out[l])
        hs = rms_norm(xs, ln1_g[l]) * (1 + sc1s) + sh1s
        ys, kvc_s, kvs_s, kvw_s, cs_s = mixer_sample(hs, cache_cmp_kv[l], cache_sel_kv[l], cache_win_kv[l],
                                                     state_conv[l], page_table, w_in[l], w_conv[l], cmp_params,
                                                     g_out_conv[l], g_out_attn[l], w_out[l])
        xp = xp + ga1p * yp
        xs = xs + ga1s * ys
        hp2 = rms_norm(xp, ln2_g[l]) * (1 + sc2p) + sh2p
        hs2 = rms_norm(xs, ln2_g[l]) * (1 + sc2s) + sh2s
        n_p = hp2.shape[0] * hp2.shape[1]
        f = hier_moe(jnp.concatenate([hp2.reshape(-1, D_MODEL), hs2.reshape(-1, D_MODEL)], axis=0),
                     w_route_group[l], b_route_group[l], w_route_expert[l], b_route_expert[l],
                     w_gate[l], w_up[l], w_down[l])
        xp = xp + ga2p * f[:n_p].reshape(xp.shape)
        xs = xs + ga2s * f[n_p:].reshape(xs.shape)
        cmp_p.append(kvc_p); cmp_s.append(kvc_s)
        sel_p.append(kvs_p); sel_s.append(kvs_s)
        win_p.append(kvw_p); win_s.append(kvw_s)
        conv_p.append(cs_p); conv_s.append(cs_s)
    y_prompt = rms_norm(xp, final_g)
    y_sample = rms_norm(xs, final_g)
    new_cmp_prompt = jnp.stack(cmp_p)
    new_cmp_sample = jnp.stack(cmp_s)
    new_sel_prompt = jnp.stack(sel_p)
    new_sel_sample = jnp.stack(sel_s)
    new_win_prompt = jnp.stack(win_p)
    new_win_sample = jnp.stack(win_s)
    new_conv_prompt = jnp.stack(conv_p)
    new_conv_sample = jnp.stack(conv_s)
    return (y_prompt, y_sample, new_cmp_prompt, new_cmp_sample, new_sel_prompt, new_sel_sample,
            new_win_prompt, new_win_sample, new_conv_prompt, new_conv_sample)
```

```python
import functools

import numpy as np
import jax
import jax.numpy as jnp
from jax import lax
from jax.experimental import pallas as pl
from jax.experimental.pallas import tpu as pltpu

F32 = jnp.float32
BF16 = jnp.bfloat16
I32 = jnp.int32

D_MODEL = 1024
CONV_W = 512
ATTN_W = 512
HD = 64
HALF = HD // 2
N_HEADS = 8
N_KV = 2
QPK = 4
KV_W = N_KV * HD
CONV_K = 3
PAGE = 128
CMP_STRIDE = 16
CMP_HID = 128
SEL_BLOCK = 64
N_SEL = 16
WINDOW = 512
Q_BLOCK = 128
ROPE_THETA = 10000.0
N_GROUPS = 4
EPG = 8
N_EXPERTS = 32
D_EXPERT = 512
MOE_BLOCK = 128
NORM_EPS = 1e-6
NEG_INF = -1e30
FORCE_SCORE = 1e4
LANES = 128
SUBLANES = 8
CHUNK_ROW = CMP_STRIDE * 2 * KV_W
VMEM_LIMIT = 56 * 1024 * 1024

_NT = (((1,), (1,)), ((), ()))


def _params(n_axes):
    return pltpu.CompilerParams(dimension_semantics=("arbitrary",) * n_axes,
                                vmem_limit_bytes=VMEM_LIMIT)


def _rms(x, g):
    return x * lax.rsqrt(jnp.mean(x * x, axis=-1, keepdims=True) + NORM_EPS) * g


def _rope128(x, cos, sin_signed, first_half):
    xr = jnp.where(first_half, pltpu.roll(x, LANES - HALF, 1), pltpu.roll(x, HALF, 1))
    return x * cos + xr * sin_signed


def _first_half_mask(rows):
    lane = lax.broadcasted_iota(I32, (rows, LANES), 1)
    return (lane % HD) < HALF


def _ada_kernel(c_ref, w_ref, b_ref, o_ref):
    c = c_ref[...]
    s = c * jax.nn.sigmoid(c)
    o_ref[...] = jnp.dot(s.astype(BF16), w_ref[...].astype(BF16), preferred_element_type=F32) + b_ref[...]


def _ada(c_all, w_ada, b_ada):
    m, d = c_all.shape
    n = w_ada.shape[1]
    tn = 1024
    return pl.pallas_call(
        _ada_kernel,
        grid=(n // tn,),
        in_specs=[pl.BlockSpec((m, d), lambda j: (0, 0)),
                  pl.BlockSpec((d, tn), lambda j: (0, j)),
                  pl.BlockSpec((1, tn), lambda j: (0, j))],
        out_specs=pl.BlockSpec((m, tn), lambda j: (0, j)),
        out_shape=jax.ShapeDtypeStruct((m, n), F32),
        compiler_params=_params(1),
        name="ada",
    )(c_all, w_ada, b_ada.reshape(1, n))


_C_B, _C_C, _C_U, _C_Q, _C_KVC, _C_KVS, _C_KVW, _C_G, _C_END = 0, 512, 1024, 1536, 2048, 2304, 2560, 2816, 3072


def _proj_kernel(*refs, tm, tpb, sample):
    if sample:
        (x_ref, g1_ref, sc_ref, sh_ref, w_ref, wc_ref, cos_ref, sin_ref, p0_ref, p1_ref,
         conv_ref, cst_ref, q_ref, kvc_ref, kvs_ref, kvw_ref, ks_ref, vs_ref, kw_ref, vw_ref, gate_ref) = refs
        vbuf = None
    else:
        (x_ref, g1_ref, sc_ref, sh_ref, w_ref, wc_ref, cos_ref, sin_ref,
         conv_ref, cst_ref, q_ref, kvc_ref, kvs_ref, kvw_ref, ks_ref, vs_ref, kw_ref, vw_ref, gate_ref, vbuf) = refs
    i = pl.program_id(0)
    x = x_ref[...]
    h = _rms(x, g1_ref[...]) * (1.0 + sc_ref[0]) + sh_ref[0]
    hb = h.astype(BF16)

    zc = jnp.dot(hb, w_ref[:, _C_B:_C_Q], preferred_element_type=F32)
    b_g = zc[:, 0:CONV_W]
    v = zc[:, CONV_W:2 * CONV_W] * zc[:, 2 * CONV_W:3 * CONV_W]
    wc = wc_ref[...]
    if sample:
        y = wc[0:1] * p0_ref[...] + wc[1:2] * p1_ref[...] + wc[2:3] * v
        cst_ref[...] = v
    else:
        @pl.when(i % tpb == 0)
        def _():
            vbuf[0:SUBLANES, :] = jnp.zeros((SUBLANES, CONV_W), F32)
        vbuf[SUBLANES:SUBLANES + tm, :] = v
        y = wc[0:1] * vbuf[pl.ds(SUBLANES - 2, tm), :] + wc[1:2] * vbuf[pl.ds(SUBLANES - 1, tm), :] + wc[2:3] * v
        tail = vbuf[tm:tm + SUBLANES, :]
        cst_ref[0] = tail
        vbuf[0:SUBLANES, :] = tail
    conv_ref[...] = b_g * y

    cos = cos_ref[...]
    sin_s = sin_ref[...]
    first = _first_half_mask(tm)

    zq = jnp.dot(hb, w_ref[:, _C_Q:_C_KVC], preferred_element_type=F32)
    for gq in range(ATTN_W // LANES):
        qr = _rope128(zq[:, gq * LANES:(gq + 1) * LANES], cos, sin_s, first) * (HD ** -0.5)
        q_ref[0, 2 * gq] = qr[:, 0:HD].astype(BF16)
        q_ref[0, 2 * gq + 1] = qr[:, HD:LANES].astype(BF16)

    zkv = jnp.dot(hb, w_ref[:, _C_KVC:_C_G], preferred_element_type=F32)
    kvc_ref[...] = zkv[:, 0:2 * KV_W]
    for base, kv_ref, kh_ref, vh_ref in ((2 * KV_W, kvs_ref, ks_ref, vs_ref), (4 * KV_W, kvw_ref, kw_ref, vw_ref)):
        kr = _rope128(zkv[:, base:base + KV_W], cos, sin_s, first)
        vv = zkv[:, base + KV_W:base + 2 * KV_W]
        kv_ref[:, 0:KV_W] = kr
        kv_ref[:, KV_W:2 * KV_W] = vv
        for k in range(N_KV):
            kh_ref[0, k] = kr[:, k * HD:(k + 1) * HD].astype(BF16)
            vh_ref[0, k] = vv[:, k * HD:(k + 1) * HD].astype(BF16)

    zg = jnp.dot(hb, w_ref[:, _C_G:_C_END], preferred_element_type=F32)
    gate_ref[...] = jax.nn.sigmoid(zg)


def _proj(x2d, g1, sc, sh, w_pack, w_conv, cos_t, sin_t, *, nb, t, sample, prev=None):
    rows = nb * t
    tm = min(512, rows) if not sample else rows
    tpb = (t // tm) if not sample else 1
    n_tiles = rows // tm
    f = lambda a: jax.ShapeDtypeStruct(a, F32)
    b = lambda a: jax.ShapeDtypeStruct(a, BF16)
    if sample:
        mod_spec = pl.BlockSpec((1, tm, D_MODEL), lambda i: (0, 0, 0))
        tab_spec = pl.BlockSpec((1, LANES), lambda i: (0, 0))
        cst_shape, cst_spec = f((rows, CONV_W)), pl.BlockSpec((tm, CONV_W), lambda i: (0, 0))
        hm = lambda i: (0, 0, i, 0)
        hb_, ht_ = 1, rows
    else:
        mod_spec = pl.BlockSpec((1, 1, D_MODEL), lambda i: (i // tpb, 0, 0))
        tab_spec = pl.BlockSpec((tm, LANES), lambda i: (i % tpb, 0))
        cst_shape, cst_spec = f((nb, SUBLANES, CONV_W)), pl.BlockSpec((1, SUBLANES, CONV_W), lambda i: (i // tpb, 0, 0))
        hm = lambda i: (i // tpb, 0, i % tpb, 0)
        hb_, ht_ = nb, t
    row = lambda w: pl.BlockSpec((tm, w), lambda i: (i, 0))
    in_specs = [row(D_MODEL), pl.BlockSpec((1, D_MODEL), lambda i: (0, 0)), mod_spec, mod_spec,
                pl.BlockSpec((D_MODEL, _C_END), lambda i: (0, 0)),
                pl.BlockSpec((SUBLANES, CONV_W), lambda i: (0, 0)), tab_spec, tab_spec]
    args = [x2d, g1.reshape(1, D_MODEL), sc, sh, w_pack, w_conv, cos_t, sin_t]
    scratch = []
    if sample:
        in_specs += [row(CONV_W), row(CONV_W)]
        args += [prev[0], prev[1]]
    else:
        scratch = [pltpu.VMEM((tm + SUBLANES, CONV_W), F32)]
    out_shape = [f((rows, CONV_W)), cst_shape, b((hb_, N_HEADS, ht_, HD)),
                 f((rows, 2 * KV_W)), f((rows, 2 * KV_W)), f((rows, 2 * KV_W)),
                 b((hb_, N_KV, ht_, HD)), b((hb_, N_KV, ht_, HD)), b((hb_, N_KV, ht_, HD)), b((hb_, N_KV, ht_, HD)),
                 f((rows, 2 * LANES))]
    out_specs = [row(CONV_W), cst_spec, pl.BlockSpec((1, N_HEADS, tm, HD), hm),
                 row(2 * KV_W), row(2 * KV_W), row(2 * KV_W),
                 pl.BlockSpec((1, N_KV, tm, HD), hm), pl.BlockSpec((1, N_KV, tm, HD), hm),
                 pl.BlockSpec((1, N_KV, tm, HD), hm), pl.BlockSpec((1, N_KV, tm, HD), hm),
                 row(2 * LANES)]
    return pl.pallas_call(
        functools.partial(_proj_kernel, tm=tm, tpb=tpb, sample=sample),
        grid=(n_tiles,), in_specs=in_specs, out_specs=out_specs, out_shape=out_shape,
        scratch_shapes=scratch, compiler_params=_params(1),
        name="proj_sample" if sample else "proj_prompt",
    )(*args)


def _cmpbias_kernel(pos_ref, w_ref, b1_ref, o_ref):
    for c in range(2):
        o_ref[c:c + 1, :] = jnp.sum(pos_ref[c] * w_ref[c], axis=0, keepdims=True) + b1_ref[c:c + 1, :]


def _cmpbias(cmp_pos, cmp_w1, cmp_b1):
    n = cmp_pos.shape[1] * cmp_pos.shape[2]
    return pl.pallas_call(
        _cmpbias_kernel,
        out_shape=jax.ShapeDtypeStruct((2, CMP_HID), F32),
        compiler_params=pltpu.CompilerParams(vmem_limit_bytes=VMEM_LIMIT),
        name="cmpbias",
    )(cmp_pos.reshape(2, n, 1), cmp_w1.reshape(2, n, CMP_HID), cmp_b1)


def _cmp_kernel(pt_ref, *refs, ppt):
    pages = refs[:ppt]
    nxt = refs[ppt]
    w1_ref, b1_ref, w2_ref, b2_ref, cos_ref, sin_ref, ck_ref, cv_ref, lhs, pbuf = refs[ppt + 1:]
    r = ppt * SUBLANES
    first = _first_half_mask(r)
    for c in range(2):
        for s in range(CMP_STRIDE):
            src = slice(s * 2 * KV_W + c * KV_W, s * 2 * KV_W + (c + 1) * KV_W)
            dst = slice(s * KV_W, (s + 1) * KV_W)
            for j in range(ppt):
                lhs[j * SUBLANES:(j + 1) * SUBLANES, dst] = pages[j][0, :, src]
            lhs[r:r + SUBLANES, dst] = nxt[0, :, src]
        p = jnp.dot(lhs[...].astype(BF16), w1_ref[c], preferred_element_type=F32)
        pbuf[...] = p[:, 2 * CMP_HID:4 * CMP_HID]
        hid = p[0:r, 0:2 * CMP_HID] + pbuf[pl.ds(1, r), :] + b1_ref[c]
        act = jax.nn.gelu(hid)
        comp = jnp.dot(act.astype(BF16), w2_ref[c], preferred_element_type=F32) + b2_ref[c]
        if c == 0:
            comp = _rope128(comp, cos_ref[...], sin_ref[...], first)
            out = ck_ref
        else:
            out = cv_ref
        for k in range(N_KV):
            out[0, k] = comp[:, k * HD:(k + 1) * HD]


def _cmp(pages3, pt_flat, nb, n_pages, w1p, b1p, w2p, b2p, cos_c, sin_c, name):
    ppt = min(32, n_pages)
    n_tiles = n_pages // ppt
    r = ppt * SUBLANES
    n_chunk = n_pages * SUBLANES

    def page_map(j):
        return lambda b, t, pt: (pt[b * n_pages + t * ppt + j], 0, 0)

    def next_map(b, t, pt):
        return (pt[b * n_pages + jnp.minimum(t * ppt + ppt, n_pages - 1)], 0, 0)

    in_specs = [pl.BlockSpec((1, SUBLANES, CHUNK_ROW), page_map(j)) for j in range(ppt)]
    in_specs.append(pl.BlockSpec((1, SUBLANES, CHUNK_ROW), next_map))
    const = lambda shp: pl.BlockSpec(shp, lambda b, t, pt: (0,) * len(shp))
    in_specs += [const(w1p.shape), const(b1p.shape), const(w2p.shape), const(b2p.shape),
                 pl.BlockSpec((r, LANES), lambda b, t, pt: (t, 0)), pl.BlockSpec((r, LANES), lambda b, t, pt: (t, 0))]
    hm = pl.BlockSpec((1, N_KV, r, HD), lambda b, t, pt: (b, 0, t, 0))
    grid_spec = pltpu.PrefetchScalarGridSpec(
        num_scalar_prefetch=1, grid=(nb, n_tiles), in_specs=in_specs, out_specs=[hm, hm],
        scratch_shapes=[pltpu.VMEM((r + SUBLANES, CMP_STRIDE * KV_W), F32), pltpu.VMEM((r + SUBLANES, 2 * CMP_HID), F32)])
    return pl.pallas_call(
        functools.partial(_cmp_kernel, ppt=ppt),
        grid_spec=grid_spec,
        out_shape=[jax.ShapeDtypeStruct((nb, N_KV, n_chunk, HD), F32)] * 2,
        compiler_params=_params(2), name=name,
    )(pt_flat, *([pages3] * (ppt + 1)), w1p, b1p, w2p, b2p, cos_c, sin_c)


def _softmax_rows(s, valid):
    s = jnp.where(valid, s, NEG_INF)
    m = jnp.max(s, axis=-1, keepdims=True)
    e = jnp.exp(s - m)
    return e / jnp.sum(e, axis=-1, keepdims=True)


def _attn_p_kernel(q_ref, ck_ref, cv_ref, ks_ref, vs_ref, kw_ref, vw_ref, gate_ref, band_ref, exp_ref, o_ref,
                   *, n_cmp_pad, n_blk, kc, t):
    qb = pl.program_id(2)
    start = qb * Q_BLOCK
    rows = QPK * Q_BLOCK
    q = q_ref[0].reshape(rows, HD)
    tpos = start + lax.broadcasted_iota(I32, (Q_BLOCK, 1), 0)
    qpos = start + lax.broadcasted_iota(I32, (rows, 1), 0) % Q_BLOCK

    s_c = lax.dot_general(q, ck_ref[0, 0].astype(BF16), _NT, preferred_element_type=F32)
    cmp_end = (lax.broadcasted_iota(I32, (1, n_cmp_pad), 1) + 2) * CMP_STRIDE - 1
    valid_c = cmp_end <= qpos
    p_c = _softmax_rows(s_c, valid_c) * valid_c.astype(F32)
    o_c = jnp.dot(p_c.astype(BF16), cv_ref[0, 0].astype(BF16), preferred_element_type=F32)

    pcs = p_c[0:Q_BLOCK] + p_c[Q_BLOCK:2 * Q_BLOCK] + p_c[2 * Q_BLOCK:3 * Q_BLOCK] + p_c[3 * Q_BLOCK:4 * Q_BLOCK]
    imp = jnp.dot(pcs, band_ref[...], preferred_element_type=F32, precision=lax.Precision.HIGHEST)
    blk = lax.broadcasted_iota(I32, (Q_BLOCK, n_blk), 1)
    cur = tpos // SEL_BLOCK
    causal = blk * SEL_BLOCK <= tpos
    forced = causal & ((blk == 0) | (blk == cur) | (blk == cur - 1))
    score = jnp.where(forced, FORCE_SCORE, jnp.where(causal, imp, -1.0))
    rank = jnp.zeros((Q_BLOCK, n_blk), F32)
    for bp in range(n_blk):
        col = score[:, bp:bp + 1]
        beats = (col > score) | ((col == score) & (bp < blk))
        rank = rank + beats.astype(F32)
    sel = (rank < float(min(N_SEL, n_blk))).astype(BF16)

    n_chunks = (start + Q_BLOCK + kc - 1) // kc

    def body(j, carry):
        m_i, l_i, acc = carry
        off = pl.multiple_of(j * kc, kc)
        kj = ks_ref[0, 0, pl.ds(off, kc), :]
        vj = vs_ref[0, 0, pl.ds(off, kc), :]
        s = lax.dot_general(q, kj, _NT, preferred_element_type=F32)
        mexp = jnp.dot(sel, exp_ref[j], preferred_element_type=F32)
        keypos = off + lax.broadcasted_iota(I32, (1, kc), 1)
        ok = ((mexp > 0.5) & (keypos <= tpos))[None]
        s = jnp.where(ok, s.reshape(QPK, Q_BLOCK, kc), NEG_INF).reshape(rows, kc)
        m_new = jnp.maximum(m_i, jnp.max(s, axis=-1, keepdims=True))
        alpha = jnp.exp(m_i - m_new)
        p = jnp.where(ok, jnp.exp(s - m_new).reshape(QPK, Q_BLOCK, kc), 0.0).reshape(rows, kc)
        l_new = alpha * l_i + jnp.sum(p, axis=-1, keepdims=True)
        acc_new = alpha * acc + jnp.dot(p.astype(BF16), vj, preferred_element_type=F32)
        return m_new, l_new, acc_new

    m0 = jnp.full((rows, 1), NEG_INF, F32)
    l0 = jnp.zeros((rows, 1), F32)
    a0 = jnp.zeros((rows, HD), F32)
    _, l_s, acc_s = lax.fori_loop(0, n_chunks, body, (m0, l0, a0))
    o_s = acc_s / l_s

    wlen = WINDOW + Q_BLOCK
    s0 = pl.multiple_of(jnp.maximum(start - WINDOW, 0), Q_BLOCK)
    kwin = kw_ref[0, 0, pl.ds(s0, wlen), :]
    vwin = vw_ref[0, 0, pl.ds(s0, wlen), :]
    s_w = lax.dot_general(q, kwin, _NT, preferred_element_type=F32)
    dist = qpos - (s0 + lax.broadcasted_iota(I32, (1, wlen), 1))
    p_w = _softmax_rows(s_w, (dist >= 0) & (dist <= WINDOW))
    o_w = jnp.dot(p_w.astype(BF16), vwin, preferred_element_type=F32)

    g = gate_ref[...]
    for hq in range(QPK):
        rs = slice(hq * Q_BLOCK, (hq + 1) * Q_BLOCK)
        o = (g[:, hq:hq + 1] * o_c[rs] + g[:, QPK + hq:QPK + hq + 1] * o_s[rs]
             + g[:, 2 * QPK + hq:2 * QPK + hq + 1] * o_w[rs])
        o_ref[0, :, hq * HD:(hq + 1) * HD] = o


def _attn_prompt(q_hm, ck, cv, ks, vs, kw, vw, gates, band, expand, *, nb, t):
    n_qb = t // Q_BLOCK
    n_cmp_pad = ck.shape[2]
    n_blk = band.shape[1]
    kc = expand.shape[2]
    kv_spec = lambda n: pl.BlockSpec((1, 1, n, HD), lambda b, k, i: (b, k, 0, 0))
    return pl.pallas_call(
        functools.partial(_attn_p_kernel, n_cmp_pad=n_cmp_pad, n_blk=n_blk, kc=kc, t=t),
        grid=(nb, N_KV, n_qb),
        in_specs=[pl.BlockSpec((1, QPK, Q_BLOCK, HD), lambda b, k, i: (b, k, i, 0)),
                  kv_spec(n_cmp_pad), kv_spec(n_cmp_pad), kv_spec(t), kv_spec(t), kv_spec(t), kv_spec(t),
                  pl.BlockSpec((Q_BLOCK, LANES), lambda b, k, i: (b * n_qb + i, k)),
                  pl.BlockSpec(band.shape, lambda b, k, i: (0, 0)),
                  pl.BlockSpec(expand.shape, lambda b, k, i: (0, 0, 0))],
        out_specs=pl.BlockSpec((1, Q_BLOCK, QPK * HD), lambda b, k, i: (b, i, k)),
        out_shape=jax.ShapeDtypeStruct((nb, t, ATTN_W), F32),
        compiler_params=_params(3), name="attn_prompt",
    )(q_hm, ck, cv, ks, vs, kw, vw, gates, band, expand)


def _attn_s1_kernel(q_ref, ck_ref, cv_ref, band_ref, oc_ref, imp_ref, *, n_chunk, past):
    q = q_ref[0]
    q16 = jnp.concatenate([q, jnp.zeros_like(q)], axis=0).astype(BF16)
    cmp_end = (lax.broadcasted_iota(I32, (1, n_chunk), 1) + 2) * CMP_STRIDE - 1
    valid = cmp_end <= past
    head = lax.broadcasted_iota(I32, (2 * N_HEADS, 1), 0)
    oc = jnp.zeros((2 * N_HEADS, HD), F32)
    imps = []
    for k in range(N_KV):
        s = lax.dot_general(q16, ck_ref[0, k].astype(BF16), _NT, preferred_element_type=F32)
        p = _softmax_rows(s, valid) * valid.astype(F32)
        in_grp = (head >= k * QPK) & (head < (k + 1) * QPK)
        p = jnp.where(in_grp, p, 0.0)
        oc = oc + jnp.dot(p.astype(BF16), cv_ref[0, k].astype(BF16), preferred_element_type=F32)
        pcs = jnp.sum(p, axis=0, keepdims=True)
        pcs8 = jnp.broadcast_to(pcs, (SUBLANES, n_chunk))
        imps.append(jnp.dot(pcs8, band_ref[...], preferred_element_type=F32,
                            precision=lax.Precision.HIGHEST)[0:1])
    oc_ref[0] = oc[0:N_HEADS]
    imp_ref[0] = jnp.concatenate(imps + [jnp.zeros((SUBLANES - N_KV, imps[0].shape[1]), F32)], axis=0)


def _topk_s_kernel(imp_ref, idx_ref, *, n_sel_blocks, past):
    imp = imp_ref[...]
    rows, nbp = imp.shape
    blk = lax.broadcasted_iota(I32, (rows, nbp), 1)
    cur = past // SEL_BLOCK
    causal = blk * SEL_BLOCK <= past
    forced = causal & ((blk == 0) | (blk == cur) | (blk == cur - 1))
    score = jnp.where(forced, FORCE_SCORE, jnp.where(causal, imp, -1.0))
    score = jnp.where(blk < n_sel_blocks, score, -2.0)
    lane = lax.broadcasted_iota(I32, (rows, LANES), 1)
    out = jnp.zeros((rows, LANES), I32)
    for r in range(min(N_SEL, n_sel_blocks)):
        m = jnp.max(score, axis=-1, keepdims=True)
        pick = jnp.min(jnp.where(score == m, blk, nbp), axis=-1, keepdims=True)
        out = jnp.where(lane == r, pick, out)
        score = jnp.where(blk == pick, -3.0, score)
    idx_ref[...] = out


def _attn_s2_kernel(pt_ref, idx_ref, *refs, n_pages, past, n_sel_blocks):
    blocks = refs[:N_KV * N_SEL]
    q_ref, oc_ref, kvs_ref, win_ref, kvw_ref, gate_ref, o_ref, kbuf, vbuf, kwb, vwb = refs[N_KV * N_SEL:]
    b = pl.program_id(0)
    q = q_ref[0]
    q16 = jnp.concatenate([q, jnp.zeros_like(q)], axis=0).astype(BF16)
    nk = N_SEL * SEL_BLOCK
    wb = win_ref.shape[1]
    kbuf[nk:nk + SUBLANES, :] = jnp.zeros((SUBLANES, LANES), F32)
    vbuf[nk:nk + SUBLANES, :] = jnp.zeros((SUBLANES, LANES), F32)
    kwb[wb:wb + SUBLANES, :] = jnp.zeros((SUBLANES, LANES), F32)
    vwb[wb:wb + SUBLANES, :] = jnp.zeros((SUBLANES, LANES), F32)
    kvs_new = kvs_ref[0]
    kvw_new = kvw_ref[0]
    win = win_ref[0]
    head = lax.broadcasted_iota(I32, (2 * N_HEADS, 1), 0)
    o_s = jnp.zeros((2 * N_HEADS, HD), F32)
    o_w = jnp.zeros((2 * N_HEADS, HD), F32)
    keyslot = lax.broadcasted_iota(I32, (1, nk + SUBLANES), 1)
    for k in range(N_KV):
        in_grp = (head >= k * QPK) & (head < (k + 1) * QPK)
        valid = keyslot == nk
        for j in range(N_SEL):
            blkref = blocks[k * N_SEL + j]
            kbuf[j * SEL_BLOCK:(j + 1) * SEL_BLOCK, 0:HD] = blkref[0, :, k * HD:(k + 1) * HD]
            vbuf[j * SEL_BLOCK:(j + 1) * SEL_BLOCK, 0:HD] = blkref[0, :, KV_W + k * HD:KV_W + (k + 1) * HD]
            bidx = idx_ref[(b * N_KV + k) * LANES + j]
            in_cache = bidx < (n_sel_blocks - 1)
            tok = bidx * SEL_BLOCK + (keyslot - j * SEL_BLOCK)
            valid = valid | ((keyslot >= j * SEL_BLOCK) & (keyslot < (j + 1) * SEL_BLOCK) & in_cache & (tok <= past))
        kbuf[nk:nk + 1, 0:HD] = kvs_new[:, k * HD:(k + 1) * HD]
        vbuf[nk:nk + 1, 0:HD] = kvs_new[:, KV_W + k * HD:KV_W + (k + 1) * HD]
        ks = kbuf[:, 0:HD].astype(BF16)
        vs = vbuf[:, 0:HD].astype(BF16)
        s = lax.dot_general(q16, ks, _NT, preferred_element_type=F32)
        p = jnp.where(in_grp, _softmax_rows(s, valid), 0.0)
        o_s = o_s + jnp.dot(p.astype(BF16), vs, preferred_element_type=F32)
        kwb[0:wb, 0:HD] = win[:, k * HD:(k + 1) * HD]
        vwb[0:wb, 0:HD] = win[:, KV_W + k * HD:KV_W + (k + 1) * HD]
        kwb[wb:wb + 1, 0:HD] = kvw_new[:, k * HD:(k + 1) * HD]
        vwb[wb:wb + 1, 0:HD] = kvw_new[:, KV_W + k * HD:KV_W + (k + 1) * HD]
        wslot = lax.broadcasted_iota(I32, (1, wb + SUBLANES), 1)
        wpos = past - wb + wslot
        dist = past - wpos
        valid_w = (wslot <= wb) & (dist >= 0) & (dist <= WINDOW) & (wpos >= 0)
        sw = lax.dot_general(q16, kwb[:, 0:HD].astype(BF16), _NT, preferred_element_type=F32)
        pw = jnp.where(in_grp, _softmax_rows(sw, valid_w), 0.0)
        o_w = o_w + jnp.dot(pw.astype(BF16), vwb[:, 0:HD].astype(BF16), preferred_element_type=F32)
    g = gate_ref[0]
    o_ref[0] = g[:, 0:1] * oc_ref[0] + g[:, 1:2] * o_s[0:N_HEADS] + g[:, 2:3] * o_w[0:N_HEADS]


def _attn_sample(q3, ck, cv, band_s, sel_blocks3, pt_flat, kvs_rows, cache_win2, kvw_rows, gates_hm,
                 *, nb, n_pages, past, n_sel_blocks):
    n_chunk = ck.shape[2]
    nbp = band_s.shape[1]
    oc, imp = pl.pallas_call(
        functools.partial(_attn_s1_kernel, n_chunk=n_chunk, past=past),
        grid=(nb,),
        in_specs=[pl.BlockSpec((1, N_HEADS, HD), lambda b: (b, 0, 0)),
                  pl.BlockSpec((1, N_KV, n_chunk, HD), lambda b: (b, 0, 0, 0)),
                  pl.BlockSpec((1, N_KV, n_chunk, HD), lambda b: (b, 0, 0, 0)),
                  pl.BlockSpec(band_s.shape, lambda b: (0, 0))],
        out_specs=[pl.BlockSpec((1, N_HEADS, HD), lambda b: (b, 0, 0)),
                   pl.BlockSpec((1, SUBLANES, nbp), lambda b: (b, 0, 0))],
        out_shape=[jax.ShapeDtypeStruct((nb, N_HEADS, HD), F32), jax.ShapeDtypeStruct((nb, SUBLANES, nbp), F32)],
        compiler_params=_params(1), name="attn_sample_cmp",
    )(q3, ck, cv, band_s)
    imp2 = imp[:, 0:N_KV, :].reshape(nb * N_KV, nbp)
    idx = pl.pallas_call(
        functools.partial(_topk_s_kernel, n_sel_blocks=n_sel_blocks, past=past),
        out_shape=jax.ShapeDtypeStruct((nb * N_KV, LANES), I32),
        compiler_params=pltpu.CompilerParams(vmem_limit_bytes=VMEM_LIMIT), name="topk_sample",
    )(imp2)
    idx_flat = idx.reshape(-1)

    def blk_map(k, j):
        def f(b, pt, ix):
            bidx = ix[(b * N_KV + k) * LANES + j]
            page = pt[b * n_pages + jnp.minimum(bidx // 2, n_pages - 1)]
            return (page * 2 + bidx % 2, 0, 0)
        return f

    in_specs = [pl.BlockSpec((1, SEL_BLOCK, 2 * KV_W), blk_map(k, j)) for k in range(N_KV) for j in range(N_SEL)]
    wb = cache_win2.shape[1]
    in_specs += [pl.BlockSpec((1, N_HEADS, HD), lambda b, pt, ix: (b, 0, 0)),
                 pl.BlockSpec((1, N_HEADS, HD), lambda b, pt, ix: (b, 0, 0)),
                 pl.BlockSpec((1, 1, 2 * KV_W), lambda b, pt, ix: (b, 0, 0)),
                 pl.BlockSpec((1, wb, 2 * KV_W), lambda b, pt, ix: (b, 0, 0)),
                 pl.BlockSpec((1, 1, 2 * KV_W), lambda b, pt, ix: (b, 0, 0)),
                 pl.BlockSpec((1, N_HEADS, LANES), lambda b, pt, ix: (b, 0, 0))]
    nk = N_SEL * SEL_BLOCK
    grid_spec = pltpu.PrefetchScalarGridSpec(
        num_scalar_prefetch=2, grid=(nb,), in_specs=in_specs,
        out_specs=pl.BlockSpec((1, N_HEADS, HD), lambda b, pt, ix: (b, 0, 0)),
        scratch_shapes=[pltpu.VMEM((nk + SUBLANES, LANES), F32), pltpu.VMEM((nk + SUBLANES, LANES), F32),
                        pltpu.VMEM((wb + SUBLANES, LANES), F32), pltpu.VMEM((wb + SUBLANES, LANES), F32)])
    return pl.pallas_call(
        functools.partial(_attn_s2_kernel, n_pages=n_pages, past=past, n_sel_blocks=n_sel_blocks),
        grid_spec=grid_spec,
        out_shape=jax.ShapeDtypeStruct((nb, N_HEADS, HD), F32),
        compiler_params=_params(1), name="attn_sample_sel",
    )(pt_flat, idx_flat, *([sel_blocks3] * (N_KV * N_SEL)), q3, oc, kvs_rows, cache_win2, kvw_rows, gates_hm)


def _outp_kernel(xp_ref, convp_ref, attnp_ref, ga1p_ref, sc2p_ref, sh2p_ref,
                 xs_ref, convs_ref, attns_ref, ga1s_ref, sc2s_ref, sh2s_ref,
                 gc_ref, ga_ref, w_ref, g2_ref, wr_ref, x1_ref, hp_ref, lg_ref, *, n_prompt_tiles):
    is_p = pl.program_id(0) < n_prompt_tiles
    pick = lambda a, b: jnp.where(is_p, a, b)
    cn = _rms(pick(convp_ref[...], convs_ref[...]), gc_ref[...])
    an = _rms(pick(attnp_ref[...], attns_ref[...]), ga_ref[...])
    cat = jnp.concatenate([cn, an], axis=1).astype(BF16)
    y = jnp.dot(cat, w_ref[...], preferred_element_type=F32)
    x1 = pick(xp_ref[...], xs_ref[...]) + pick(ga1p_ref[0], ga1s_ref[0]) * y
    x1_ref[...] = x1
    hp = _rms(x1, g2_ref[...]) * (1.0 + pick(sc2p_ref[0], sc2s_ref[0])) + pick(sh2p_ref[0], sh2s_ref[0])
    hp_ref[...] = hp
    lg_ref[...] = jnp.dot(hp, wr_ref[...], preferred_element_type=F32, precision=lax.Precision.HIGHEST)


TOKEN_TILE = 512


def _outp(prompt, sample, g_conv, g_attn, w_out_b, g2, w_route, *, tpb):
    tm = TOKEN_TILE
    n_p = prompt[0].shape[0] // tm
    total = (n_p + 1) * tm
    last = n_p - 1
    prow = lambda w: pl.BlockSpec((tm, w), lambda i: (jnp.minimum(i, last), 0))
    srow = lambda w: pl.BlockSpec((tm, w), lambda i: (0, 0))
    pmod = pl.BlockSpec((1, 1, D_MODEL), lambda i: (jnp.minimum(i, last) // tpb, 0, 0))
    smod = pl.BlockSpec((1, tm, D_MODEL), lambda i: (0, 0, 0))
    vec = lambda w: pl.BlockSpec((1, w), lambda i: (0, 0))
    row = lambda w: pl.BlockSpec((tm, w), lambda i: (i, 0))
    in_specs = [prow(D_MODEL), prow(CONV_W), prow(ATTN_W), pmod, pmod, pmod,
                srow(D_MODEL), srow(CONV_W), srow(ATTN_W), smod, smod, smod,
                vec(CONV_W), vec(ATTN_W), pl.BlockSpec((D_MODEL, D_MODEL), lambda i: (0, 0)), vec(D_MODEL),
                pl.BlockSpec((D_MODEL, LANES), lambda i: (0, 0))]
    return pl.pallas_call(
        functools.partial(_outp_kernel, n_prompt_tiles=n_p),
        grid=(n_p + 1,), in_specs=in_specs,
        out_specs=[row(D_MODEL), row(D_MODEL), row(LANES)],
        out_shape=[jax.ShapeDtypeStruct((total, D_MODEL), F32), jax.ShapeDtypeStruct((total, D_MODEL), F32),
                   jax.ShapeDtypeStruct((total, LANES), F32)],
        compiler_params=_params(1), name="outp",
    )(*prompt, *sample, g_conv.reshape(1, -1), g_attn.reshape(1, -1), w_out_b, g2.reshape(1, -1), w_route)


def _route_kernel(lg_ref, bias_ref, tri_ref, o_ref, cnt_ref, carry, *, tm, n_valid):
    i = pl.program_id(0)

    @pl.when(i == 0)
    def _():
        carry[...] = jnp.zeros_like(carry)

    lane = lax.broadcasted_iota(I32, (tm, LANES), 1)
    rowid = i * tm + lax.broadcasted_iota(I32, (tm, 1), 0)
    live = rowid < n_valid
    lg = lg_ref[...] + bias_ref[...]
    is_g = lane < N_GROUPS
    lgg = jnp.where(is_g, lg, NEG_INF)
    gmax = jnp.max(lgg, axis=-1, keepdims=True)
    grp = jnp.min(jnp.where(is_g & (lgg == gmax), lane, LANES), axis=-1, keepdims=True)
    p_grp = 1.0 / jnp.sum(jnp.where(is_g, jnp.exp(lgg - gmax), 0.0), axis=-1, keepdims=True)
    eid = lane - N_GROUPS
    in_grp = (eid >= grp * EPG) & (eid < (grp + 1) * EPG)
    le = jnp.where(in_grp, lg, NEG_INF)
    v1 = jnp.max(le, axis=-1, keepdims=True)
    e1 = jnp.min(jnp.where(in_grp & (le == v1), eid, LANES), axis=-1, keepdims=True)
    le2 = jnp.where(eid == e1, NEG_INF, le)
    v2 = jnp.max(le2, axis=-1, keepdims=True)
    e2 = jnp.min(jnp.where(in_grp & (eid != e1) & (le2 == v2), eid, LANES), axis=-1, keepdims=True)
    ex2 = jnp.exp(v2 - v1)
    w1 = p_grp * (1.0 / (1.0 + ex2))
    w2 = p_grp * (ex2 / (1.0 + ex2))
    oh1 = ((lane == e1) & live).astype(F32)
    oh2 = ((lane == e2) & live).astype(F32)
    both = oh1 + oh2
    before = jnp.dot(tri_ref[...], both.astype(BF16), preferred_element_type=F32) + carry[0:1, :]
    r1 = jnp.sum(oh1 * before, axis=-1, keepdims=True)
    r2 = jnp.sum(oh2 * before, axis=-1, keepdims=True)
    carry[0:1, :] = carry[0:1, :] + jnp.sum(both, axis=0, keepdims=True)
    out = jnp.where(lane == 0, e1.astype(F32), 0.0)
    out = jnp.where(lane == 1, e2.astype(F32), out)
    out = jnp.where(lane == 2, w1, out)
    out = jnp.where(lane == 3, w2, out)
    out = jnp.where(lane == 4, r1, out)
    out = jnp.where(lane == 5, r2, out)
    o_ref[...] = out
    cnt_ref[...] = carry[...]


def _route(logits, bias_row, n_valid):
    total = logits.shape[0]
    tm = TOKEN_TILE
    n_tiles = total // tm
    tri =(np.arange(tm)[:, None] > np.arange(tm)[None, :]).astype(np.float32)
    return pl.pallas_call(
        functools.partial(_route_kernel, tm=tm, n_valid=n_valid),
        grid=(n_tiles,),
        in_specs=[pl.BlockSpec((tm, LANES), lambda i: (i, 0)), pl.BlockSpec((1, LANES), lambda i: (0, 0)),
                  pl.BlockSpec((tm, tm), lambda i: (0, 0))],
        out_specs=[pl.BlockSpec((tm, LANES), lambda i: (i, 0)), pl.BlockSpec((SUBLANES, LANES), lambda i: (0, 0))],
        out_shape=[jax.ShapeDtypeStruct((total, LANES), F32), jax.ShapeDtypeStruct((SUBLANES, LANES), F32)],
        scratch_shapes=[pltpu.VMEM((SUBLANES, LANES), F32)],
        compiler_params=_params(1), name="route",
    )(logits, bias_row, jnp.asarray(tri, BF16))


def _row_copy(src_hbm, row, dst, slot, r, sem):
    return pltpu.make_async_copy(src_hbm.at[pl.ds(row, 1), :], dst.at[slot, pl.ds(r, 1), :], sem.at[slot])


def _experts_kernel(blk_e_ref, tok_ref, x_hbm, wg_ref, wu_ref, wd_ref, o_ref, xbuf, sem, wg_b, wu_b, wd_b, *, n_blocks):
    i = pl.program_id(0)
    slot = i % 2

    def issue(blk, s):
        for r in range(MOE_BLOCK):
            _row_copy(x_hbm, tok_ref[blk * MOE_BLOCK + r], xbuf, s, r, sem).start()

    @pl.when(i == 0)
    def _():
        issue(0, 0)

    @pl.when(i + 1 < n_blocks)
    def _():
        issue(i + 1, 1 - slot)

    changed = jnp.logical_or(i == 0, blk_e_ref[i] != blk_e_ref[jnp.maximum(i - 1, 0)])

    @pl.when(changed)
    def _():
        wg_b[...] = wg_ref[0].astype(BF16)
        wu_b[...] = wu_ref[0].astype(BF16)
        wd_b[...] = wd_ref[0].astype(BF16)

    for r in range(MOE_BLOCK):
        _row_copy(x_hbm, 0, xbuf, slot, r, sem).wait()
    x = xbuf[slot].astype(BF16)
    g = jnp.dot(x, wg_b[...], preferred_element_type=F32)
    u = jnp.dot(x, wu_b[...], preferred_element_type=F32)
    h = (g * jax.nn.sigmoid(g)) * u
    o_ref[...] = jnp.dot(h.astype(BF16), wd_b[...], preferred_element_type=F32)


def _experts(blk_e, slot_tok, hp_all, w_gate, w_up, w_down, n_blocks):
    grid_spec = pltpu.PrefetchScalarGridSpec(
        num_scalar_prefetch=2, grid=(n_blocks,),
        in_specs=[pl.BlockSpec(memory_space=pl.ANY),
                  pl.BlockSpec((1, D_MODEL, D_EXPERT), lambda i, be, st: (be[i], 0, 0)),
                  pl.BlockSpec((1, D_MODEL, D_EXPERT), lambda i, be, st: (be[i], 0, 0)),
                  pl.BlockSpec((1, D_EXPERT, D_MODEL), lambda i, be, st: (be[i], 0, 0))],
        out_specs=pl.BlockSpec((MOE_BLOCK, D_MODEL), lambda i, be, st: (i, 0)),
        scratch_shapes=[pltpu.VMEM((2, MOE_BLOCK, D_MODEL), F32), pltpu.SemaphoreType.DMA((2,)),
                        pltpu.VMEM((D_MODEL, D_EXPERT), BF16), pltpu.VMEM((D_MODEL, D_EXPERT), BF16),
                        pltpu.VMEM((D_EXPERT, D_MODEL), BF16)])
    return pl.pallas_call(
        functools.partial(_experts_kernel, n_blocks=n_blocks),
        grid_spec=grid_spec,
        out_shape=jax.ShapeDtypeStruct((n_blocks * MOE_BLOCK, D_MODEL), F32),
        compiler_params=_params(1), name="experts",
    )(blk_e, slot_tok, hp_all, w_gate, w_up, w_down)


def _final_kernel(dest_ref, yb_hbm, x1_ref, wt_ref, gate2_ref, gf_ref, o_ref, ybuf, sem, *, tm, n_tiles, row0):
    i = pl.program_id(0)
    slot = i % 2

    def copy(tile, s, r, k):
        d = dest_ref[(row0 + tile * tm + r) * 2 + k]
        return pltpu.make_async_copy(yb_hbm.at[pl.ds(d, 1), :], ybuf.at[s, k, pl.ds(r, 1), :], sem.at[s])

    def issue(tile, s):
        for r in range(tm):
            for k in range(2):
                copy(tile, s, r, k).start()

    @pl.when(i == 0)
    def _():
        issue(0, 0)

    @pl.when(i + 1 < n_tiles)
    def _():
        issue(i + 1, 1 - slot)

    for r in range(tm):
        for k in range(2):
            pltpu.make_async_copy(yb_hbm.at[pl.ds(0, 1), :], ybuf.at[slot, k, pl.ds(r, 1), :], sem.at[slot]).wait()
    wt = wt_ref[...]
    f = wt[:, 2:3] * ybuf[slot, 0] + wt[:, 3:4] * ybuf[slot, 1]
    x2 = x1_ref[...] + gate2_ref[0] * f
    o_ref[...] = _rms(x2, gf_ref[...])


def _final(dest_flat, yb, x1_all, route_rows, gate2, final_g, *, rows, tpb, per_row, row0):
    tm = min(128, rows)
    n_tiles = rows // tm
    blk0 = row0 // tm
    mod = (pl.BlockSpec((1, tm, D_MODEL), lambda i, d: (0, i, 0)) if per_row
           else pl.BlockSpec((1, 1, D_MODEL), lambda i, d: (i // tpb, 0, 0)))
    grid_spec = pltpu.PrefetchScalarGridSpec(
        num_scalar_prefetch=1, grid=(n_tiles,),
        in_specs=[pl.BlockSpec(memory_space=pl.ANY),
                  pl.BlockSpec((tm, D_MODEL), lambda i, d: (blk0 + i, 0)),
                  pl.BlockSpec((tm, LANES), lambda i, d: (blk0 + i, 0)),
                  mod, pl.BlockSpec((1, D_MODEL), lambda i, d: (0, 0))],
        out_specs=pl.BlockSpec((tm, D_MODEL), lambda i, d: (i, 0)),
        scratch_shapes=[pltpu.VMEM((2, 2, tm, D_MODEL), F32), pltpu.SemaphoreType.DMA((2,))])
    return pl.pallas_call(
        functools.partial(_final_kernel, tm=tm, n_tiles=n_tiles, row0=row0),
        grid_spec=grid_spec,
        out_shape=jax.ShapeDtypeStruct((rows, D_MODEL), F32),
        compiler_params=_params(1), name="final_sample" if per_row else "final_prompt",
    )(dest_flat, yb, x1_all, route_rows, gate2, final_g.reshape(1, -1))


def _rope_tables(pos):
    inv = ROPE_THETA ** (-jnp.arange(HALF, dtype=F32) / HALF)
    ang = pos.astype(F32)[:, None] * inv[None, :]
    cos = jnp.tile(jnp.cos(ang), (1, LANES // HALF))
    sin = jnp.sin(ang)
    sin_s = jnp.tile(jnp.concatenate([-sin, sin], axis=1), (1, LANES // HD))
    return cos, sin_s


def _pack_w_in(w_in):
    gl = w_in[:, _C_G:_C_G + 3 * N_HEADS].reshape(D_MODEL, 3, N_KV, QPK)
    gcols = []
    for k in range(N_KV):
        gk = gl[:, :, k, :].reshape(D_MODEL, 3 * QPK)
        gcols.append(jnp.pad(gk, ((0, 0), (0, LANES - 3 * QPK))))
    return jnp.concatenate([w_in[:, :_C_G]] + gcols, axis=1).astype(BF16)


def _pack_cmp_weights(cmp_w1, cmp_w2, bias, cmp_b2):
    w1 = cmp_w1.reshape(2, 2, CMP_STRIDE, HD, CMP_HID)
    eye = jnp.eye(N_KV, dtype=F32)
    w1p = jnp.einsum('crsdh,pk->cspdrkh', w1, eye).reshape(2, CMP_STRIDE * KV_W, 2 * N_KV * CMP_HID)
    w2p = jnp.einsum('chd,pk->cphkd', cmp_w2, eye).reshape(2, N_KV * CMP_HID, KV_W)
    b1p = jnp.tile(bias, (1, N_KV)).reshape(2, 1, N_KV * CMP_HID)
    b2p = jnp.tile(cmp_b2, (1, N_KV)).reshape(2, 1, KV_W)
    return w1p.astype(BF16), b1p, w2p.astype(BF16), b2p


def _band(n_cmp_pad, n_cmp, n_blk_pad, n_blk):
    n = np.arange(n_cmp_pad)[:, None]
    b = np.arange(n_blk_pad)[None, :]
    r = SEL_BLOCK // CMP_STRIDE
    m = (n >= r * b - 1) & (n <= r * b + r - 1) & (n < n_cmp) & (b < n_blk)
    return jnp.asarray(m.astype(np.float32))


def _expand(t, kc):
    n_chunks = t // kc
    key = np.arange(t).reshape(n_chunks, 1, kc)
    blk = np.arange(t // SEL_BLOCK).reshape(1, -1, 1)
    return jnp.asarray((key // SEL_BLOCK == blk).astype(np.float32), BF16)


def kernel(x_prompt, x_sample, c_prompt, c_sample, cache_cmp_kv, cache_sel_kv, cache_win_kv, state_conv, page_table,
           ln1_g, ln2_g, w_ada, b_ada, w_in, w_conv, cmp_pos, cmp_w1, cmp_b1, cmp_w2, cmp_b2, g_out_conv, g_out_attn,
           w_out, w_route_group, b_route_group, w_route_expert, b_route_expert, w_gate, w_up, w_down, final_g):
    depth = w_in.shape[0]
    assert depth == 1, "single-layer step"
    nb, t, _ = x_prompt.shape
    ns, ts, _ = x_sample.shape
    assert ts == 1 and t % 512 == 0 and t >= WINDOW + Q_BLOCK
    n_pool = cache_cmp_kv.shape[1]
    n_pages = page_table.shape[1]
    past = n_pages * PAGE
    wb = cache_win_kv.shape[2]
    assert wb == WINDOW
    l = 0

    n_c = nb + ns
    c_all = jnp.pad(jnp.concatenate([c_prompt, c_sample], axis=0), ((0, (-n_c) % SUBLANES), (0, 0)))
    mods = _ada(c_all, w_ada[l], b_ada[l])
    sh1, sc1, ga1, sh2, sc2, ga2 = [mods[:, j * D_MODEL:(j + 1) * D_MODEL] for j in range(6)]
    pr = lambda a: a[0:nb].reshape(nb, 1, D_MODEL)
    sr = lambda a: a[nb:nb + ns].reshape(1, ns, D_MODEL)

    w_pack = _pack_w_in(w_in[l])
    wconv8 = jnp.pad(w_conv[l], ((0, SUBLANES - CONV_K), (0, 0)))
    cos_p, sin_p = _rope_tables(jnp.arange(t, dtype=I32))
    cos_s, sin_s = _rope_tables(jnp.full((1,), past, I32))
    xp2 = x_prompt.reshape(nb * t, D_MODEL)
    xs2 = x_sample.reshape(ns, D_MODEL)
    (conv_p, cst_p, q_p, kvc_p, kvs_p, kvw_p, ks_p, vs_p, kw_p, vw_p, gates_p) = _proj(
        xp2, ln1_g[l], pr(sc1), pr(sh1), w_pack, wconv8, cos_p, sin_p, nb=nb, t=t, sample=False)
    (conv_s, cst_s, q_s, kvc_s, kvs_s, kvw_s, _, _, _, _, gates_s) = _proj(
        xs2, ln1_g[l], sr(sc1), sr(sh1), w_pack, wconv8, cos_s, sin_s, nb=ns, t=1, sample=True,
        prev=(state_conv[l][:, 0], state_conv[l][:, 1]))

    bias = _cmpbias(cmp_pos[l], cmp_w1[l], cmp_b1[l])
    w1p, b1p, w2p, b2p = _pack_cmp_weights(cmp_w1[l], cmp_w2[l], bias, cmp_b2[l])
    pp = t // PAGE
    cos_cp, sin_cp = _rope_tables((jnp.arange(t // CMP_STRIDE, dtype=I32) + 2) * CMP_STRIDE - 1)
    ck_p, cv_p = _cmp(kvc_p.reshape(nb * pp, SUBLANES, CHUNK_ROW), jnp.arange(nb * pp, dtype=I32), nb, pp,
                      w1p, b1p, w2p, b2p, cos_cp, sin_cp, "cmp_prompt")
    pt_flat = page_table.reshape(-1).astype(I32)
    cos_cs, sin_cs = _rope_tables((jnp.arange(past // CMP_STRIDE, dtype=I32) + 2) * CMP_STRIDE - 1)
    ck_s, cv_s = _cmp(cache_cmp_kv[l].reshape(n_pool, SUBLANES, CHUNK_ROW), pt_flat, ns, n_pages,
                      w1p, b1p, w2p, b2p, cos_cs, sin_cs, "cmp_sample")

    n_chunk_p = t // CMP_STRIDE
    n_blk_p = t // SEL_BLOCK
    band_p = _band(n_chunk_p, n_chunk_p - 1, n_blk_p, n_blk_p)
    attn_p = _attn_prompt(q_p, ck_p, cv_p, ks_p, vs_p, kw_p, vw_p, gates_p, band_p, _expand(t, 512), nb=nb, t=t)

    n_chunk_s = past // CMP_STRIDE
    n_sel_s = -(-(past + 1) // SEL_BLOCK)
    nbp = -(-n_sel_s // LANES) * LANES
    band_s = _band(n_chunk_s, (past + 1) // CMP_STRIDE - 1, nbp, n_sel_s)
    q3 = q_s.reshape(N_HEADS, ns, HD).transpose(1, 0, 2).astype(F32)
    gs = gates_s.reshape(ns, N_KV, LANES)[:, :, :3 * QPK].reshape(ns, N_KV, 3, QPK)
    gates_hm = jnp.pad(gs.transpose(0, 1, 3, 2).reshape(ns, N_HEADS, 3), ((0, 0), (0, 0), (0, LANES - 3)))
    attn_s = _attn_sample(q3, ck_s, cv_s, band_s, cache_sel_kv[l].reshape(n_pool * 2, SEL_BLOCK, 2 * KV_W), pt_flat,
                          kvs_s.reshape(ns, 1, 2 * KV_W), cache_win_kv[l].reshape(ns, wb, 2 * KV_W),
                          kvw_s.reshape(ns, 1, 2 * KV_W), gates_hm,
                          nb=ns, n_pages=n_pages, past=past, n_sel_blocks=n_sel_s).reshape(ns, ATTN_W)

    total = nb * t + ns
    w_out_b = w_out[l].astype(BF16)
    w_route = jnp.pad(jnp.concatenate([w_route_group[l], w_route_expert[l]], axis=1),
                      ((0, 0), (0, LANES - N_GROUPS - N_EXPERTS)))
    b_route = jnp.pad(jnp.concatenate([b_route_group[l], b_route_expert[l]]), (0, LANES - N_GROUPS - N_EXPERTS))
    tile_pad = lambda a: jnp.pad(a, ((0, TOKEN_TILE - ns), (0, 0)))
    smod = lambda a: tile_pad(a[nb:nb + ns]).reshape(1, TOKEN_TILE, D_MODEL)
    x1_all, hp_all, lg_all = _outp(
        (xp2, conv_p, attn_p.reshape(nb * t, ATTN_W), pr(ga1), pr(sc2), pr(sh2)),
        (tile_pad(xs2), tile_pad(conv_s), tile_pad(attn_s), smod(ga1), smod(sc2), smod(sh2)),
        g_out_conv[l], g_out_attn[l], w_out_b, ln2_g[l], w_route, tpb=t // TOKEN_TILE)

    route, counts = _route(lg_all, b_route.reshape(1, LANES), total)
    e = route[:total, 0:2].astype(I32)
    rank = route[:total, 4:6].astype(I32)
    cnt = counts[0, :N_EXPERTS].astype(I32)
    padded = (cnt + MOE_BLOCK - 1) // MOE_BLOCK * MOE_BLOCK
    pad_end = jnp.cumsum(padded)
    pad_start = pad_end - padded
    dest = pad_start[e] + rank
    m_slots = total * 2
    n_blocks = -(-(m_slots + N_EXPERTS * (MOE_BLOCK - 1)) // MOE_BLOCK)
    tok = jnp.broadcast_to(jnp.arange(total, dtype=I32)[:, None], (total, 2))
    slot_tok = jnp.zeros((n_blocks * MOE_BLOCK,), I32).at[dest.reshape(-1)].set(tok.reshape(-1))
    blk_e = jnp.minimum(jnp.searchsorted(pad_end, jnp.arange(n_blocks, dtype=I32) * MOE_BLOCK, side='right'),
                        N_EXPERTS - 1).astype(I32)

    yb = _experts(blk_e, slot_tok, hp_all, w_gate[l], w_up[l], w_down[l], n_blocks)
    dest_flat = dest.reshape(-1)
    y_p = _final(dest_flat, yb, x1_all, route, pr(ga2), final_g, rows=nb * t, tpb=t // 128, per_row=False, row0=0)
    y_s = _final(dest_flat, yb, x1_all, route, sr(ga2), final_g, rows=ns, tpb=1, per_row=True, row0=nb * t)

    kv_shape = (2, N_KV, HD)
    y_prompt = y_p.reshape(nb, t, D_MODEL)
    y_sample = y_s.reshape(ns, 1, D_MODEL)
    new_cmp_prompt = kvc_p.reshape((1, nb, t) + kv_shape)
    new_cmp_sample = kvc_s.reshape((1, ns, 1) + kv_shape)
    new_sel_prompt = kvs_p.reshape((1, nb, t) + kv_shape)
    new_sel_sample = kvs_s.reshape((1, ns, 1) + kv_shape)
    new_win_prompt = kvw_p.reshape(nb, t, 2 * KV_W)[:, t - WINDOW:].reshape((1, nb, WINDOW) + kv_shape)
    new_win_sample = jnp.concatenate([cache_win_kv[l][:, 1:], kvw_s.reshape((ns, 1) + kv_shape)], axis=1)[None]
    new_conv_prompt = cst_p[:, SUBLANES - (CONV_K - 1):][None]
    new_conv_sample = jnp.stack([state_conv[l][:, 1], cst_s], axis=1)[None]
    return (y_prompt, y_sample, new_cmp_prompt, new_cmp_sample, new_sel_prompt, new_sel_sample,
            new_win_prompt, new_win_sample, new_conv_prompt, new_conv_sample)
```

```python
import functools

import numpy as np
import jax
import jax.numpy as jnp
from jax import lax
from jax.experimental import pallas as pl
from jax.experimental.pallas import tpu as pltpu

F32 = jnp.float32
BF16 = jnp.bfloat16
I32 = jnp.int32

D_MODEL = 1024
CONV_W = 512
ATTN_W = 512
HD = 64
HALF = HD // 2
N_HEADS = 8
N_KV = 2
QPK = 4
KV_W = N_KV * HD
CONV_K = 3
PAGE = 128
CMP_STRIDE = 16
CMP_HID = 128
SEL_BLOCK = 64
N_SEL = 16
WINDOW = 512
Q_BLOCK = 128
ROPE_THETA = 10000.0
N_GROUPS = 4
EPG = 8
N_EXPERTS = 32
D_EXPERT = 512
MOE_BLOCK = 128
NORM_EPS = 1e-6
NEG_INF = -1e30
FORCE_SCORE = 1e4
LANES = 128
SUBLANES = 8
CHUNK_ROW = CMP_STRIDE * 2 * KV_W
VMEM_LIMIT = 56 * 1024 * 1024

_NT = (((1,), (1,)), ((), ()))


def _params(n_axes):
    return pltpu.CompilerParams(dimension_semantics=("arbitrary",) * n_axes,
                                vmem_limit_bytes=VMEM_LIMIT)


def _rms(x, g):
    return x * lax.rsqrt(jnp.mean(x * x, axis=-1, keepdims=True) + NORM_EPS) * g


def _rope128(x, cos, sin_signed, first_half):
    xr = jnp.where(first_half, pltpu.roll(x, LANES - HALF, 1), pltpu.roll(x, HALF, 1))
    return x * cos + xr * sin_signed


def _first_half_mask(rows):
    lane = lax.broadcasted_iota(I32, (rows, LANES), 1)
    return (lane % HD) < HALF


def _ada_kernel(c_ref, w_ref, b_ref, o_ref):
    c = c_ref[...]
    s = c * jax.nn.sigmoid(c)
    o_ref[...] = jnp.dot(s.astype(BF16), w_ref[...].astype(BF16), preferred_element_type=F32) + b_ref[...]


def _ada(c_all, w_ada, b_ada):
    m, d = c_all.shape
    n = w_ada.shape[1]
    tn = 1024
    return pl.pallas_call(
        _ada_kernel,
        grid=(n // tn,),
        in_specs=[pl.BlockSpec((m, d), lambda j: (0, 0)),
                  pl.BlockSpec((d, tn), lambda j: (0, j)),
                  pl.BlockSpec((1, tn), lambda j: (0, j))],
        out_specs=pl.BlockSpec((m, tn), lambda j: (0, j)),
        out_shape=jax.ShapeDtypeStruct((m, n), F32),
        compiler_params=_params(1),
        name="ada",
    )(c_all, w_ada, b_ada.reshape(1, n))


_C_B, _C_C, _C_U, _C_Q, _C_KVC, _C_KVS, _C_KVW, _C_G, _C_END = 0, 512, 1024, 1536, 2048, 2304, 2560, 2816, 3072


def _proj_kernel(*refs, tm, tpb, sample):
    if sample:
        (x_ref, g1_ref, sc_ref, sh_ref, w_ref, wc_ref, cos_ref, sin_ref, p0_ref, p1_ref,
         conv_ref, cst_ref, q_ref, kvc_ref, kvc_il_ref, kvs_ref, kvw_ref, ks_ref, vs_ref, kw_ref, vw_ref, gate_ref,
         ilbuf) = refs
        vbuf = None
    else:
        (x_ref, g1_ref, sc_ref, sh_ref, w_ref, wc_ref, cos_ref, sin_ref,
         conv_ref, cst_ref, q_ref, kvc_ref, kvc_il_ref, kvs_ref, kvw_ref, ks_ref, vs_ref, kw_ref, vw_ref, gate_ref,
         ilbuf, vbuf) = refs
    i = pl.program_id(0)
    x = x_ref[...]
    h = _rms(x, g1_ref[...]) * (1.0 + sc_ref[0]) + sh_ref[0]
    hb = h.astype(BF16)

    zc = jnp.dot(hb, w_ref[:, _C_B:_C_Q], preferred_element_type=F32)
    b_g = zc[:, 0:CONV_W]
    v = zc[:, CONV_W:2 * CONV_W] * zc[:, 2 * CONV_W:3 * CONV_W]
    wc = wc_ref[...]
    if sample:
        y = wc[0:1] * p0_ref[...] + wc[1:2] * p1_ref[...] + wc[2:3] * v
        cst_ref[...] = v
    else:
        @pl.when(i % tpb == 0)
        def _():
            vbuf[0:SUBLANES, :] = jnp.zeros((SUBLANES, CONV_W), F32)
        vbuf[SUBLANES:SUBLANES + tm, :] = v
        y = wc[0:1] * vbuf[pl.ds(SUBLANES - 2, tm), :] + wc[1:2] * vbuf[pl.ds(SUBLANES - 1, tm), :] + wc[2:3] * v
        tail = vbuf[tm:tm + SUBLANES, :]
        cst_ref[0] = tail
        vbuf[0:SUBLANES, :] = tail
    conv_ref[...] = b_g * y

    cos = cos_ref[...]
    sin_s = sin_ref[...]
    first = _first_half_mask(tm)

    zq = jnp.dot(hb, w_ref[:, _C_Q:_C_KVC], preferred_element_type=F32)
    for gq in range(ATTN_W // LANES):
        qr = _rope128(zq[:, gq * LANES:(gq + 1) * LANES], cos, sin_s, first) * (HD ** -0.5)
        q_ref[0, 2 * gq] = qr[:, 0:HD].astype(BF16)
        q_ref[0, 2 * gq + 1] = qr[:, HD:LANES].astype(BF16)

    def store_rows(out_ref, halves):
        for j in range(2 * N_KV):
            piece = halves[j // N_KV]
            if j % N_KV == 1:
                piece = pltpu.roll(piece, HD, 1)
            ilbuf[pl.ds(j, tm, stride=2 * N_KV), :] = piece
        out_ref[...] = ilbuf[:, 0:HD]

    zkv = jnp.dot(hb, w_ref[:, _C_KVC:_C_G], preferred_element_type=F32)
    kvc_ref[...] = zkv[:, 0:2 * KV_W]
    store_rows(kvc_il_ref, (zkv[:, 0:KV_W], zkv[:, KV_W:2 * KV_W]))
    for base, kv_ref, kh_ref, vh_ref in ((2 * KV_W, kvs_ref, ks_ref, vs_ref), (4 * KV_W, kvw_ref, kw_ref, vw_ref)):
        kr = _rope128(zkv[:, base:base + KV_W], cos, sin_s, first)
        vv = zkv[:, base + KV_W:base + 2 * KV_W]
        store_rows(kv_ref, (kr, vv))
        for k in range(N_KV):
            kh_ref[0, k] = kr[:, k * HD:(k + 1) * HD].astype(BF16)
            vh_ref[0, k] = vv[:, k * HD:(k + 1) * HD].astype(BF16)

    zg = jnp.dot(hb, w_ref[:, _C_G:_C_END], preferred_element_type=F32)
    gate_ref[...] = jax.nn.sigmoid(zg)


def _proj(x2d, g1, sc, sh, w_pack, w_conv, cos_t, sin_t, *, nb, t, sample, prev=None):
    rows = nb * t
    tm = min(512, rows) if not sample else rows
    tpb = (t // tm) if not sample else 1
    n_tiles = rows // tm
    f = lambda a: jax.ShapeDtypeStruct(a, F32)
    b = lambda a: jax.ShapeDtypeStruct(a, BF16)
    if sample:
        mod_spec = pl.BlockSpec((1, tm, D_MODEL), lambda i: (0, 0, 0))
        tab_spec = pl.BlockSpec((1, LANES), lambda i: (0, 0))
        cst_shape, cst_spec = f((rows, CONV_W)), pl.BlockSpec((tm, CONV_W), lambda i: (0, 0))
        hm = lambda i: (0, 0, i, 0)
        hb_, ht_ = 1, rows
    else:
        mod_spec = pl.BlockSpec((1, 1, D_MODEL), lambda i: (i // tpb, 0, 0))
        tab_spec = pl.BlockSpec((tm, LANES), lambda i: (i % tpb, 0))
        cst_shape, cst_spec = f((nb, SUBLANES, CONV_W)), pl.BlockSpec((1, SUBLANES, CONV_W), lambda i: (i // tpb, 0, 0))
        hm = lambda i: (i // tpb, 0, i % tpb, 0)
        hb_, ht_ = nb, t
    row = lambda w: pl.BlockSpec((tm, w), lambda i: (i, 0))
    in_specs = [row(D_MODEL), pl.BlockSpec((1, D_MODEL), lambda i: (0, 0)), mod_spec, mod_spec,
                pl.BlockSpec((D_MODEL, _C_END), lambda i: (0, 0)),
                pl.BlockSpec((SUBLANES, CONV_W), lambda i: (0, 0)), tab_spec, tab_spec]
    args = [x2d, g1.reshape(1, D_MODEL), sc, sh, w_pack, w_conv, cos_t, sin_t]
    scratch = [pltpu.VMEM((2 * N_KV * tm, LANES), F32)]
    if sample:
        in_specs += [row(CONV_W), row(CONV_W)]
        args += [prev[0], prev[1]]
    else:
        scratch.append(pltpu.VMEM((tm + SUBLANES, CONV_W), F32))
    il_rows = 2 * N_KV * rows
    il = pl.BlockSpec((2 * N_KV * tm, HD), lambda i: (i, 0))
    out_shape = [f((rows, CONV_W)), cst_shape, b((hb_, N_HEADS, ht_, HD)),
                 f((rows, 2 * KV_W)), f((il_rows, HD)), f((il_rows, HD)), f((il_rows, HD)),
                 b((hb_, N_KV, ht_, HD)), b((hb_, N_KV, ht_, HD)), b((hb_, N_KV, ht_, HD)), b((hb_, N_KV, ht_, HD)),
                 f((rows, 2 * LANES))]
    out_specs = [row(CONV_W), cst_spec, pl.BlockSpec((1, N_HEADS, tm, HD), hm),
                 row(2 * KV_W), il, il, il,
                 pl.BlockSpec((1, N_KV, tm, HD), hm), pl.BlockSpec((1, N_KV, tm, HD), hm),
                 pl.BlockSpec((1, N_KV, tm, HD), hm), pl.BlockSpec((1, N_KV, tm, HD), hm),
                 row(2 * LANES)]
    return pl.pallas_call(
        functools.partial(_proj_kernel, tm=tm, tpb=tpb, sample=sample),
        grid=(n_tiles,), in_specs=in_specs, out_specs=out_specs, out_shape=out_shape,
        scratch_shapes=scratch, compiler_params=_params(1),
        name="proj_sample" if sample else "proj_prompt",
    )(*args)


def _cmpbias_kernel(pos_ref, w_ref, b1_ref, o_ref):
    for c in range(2):
        o_ref[c:c + 1, :] = jnp.sum(pos_ref[c] * w_ref[c], axis=0, keepdims=True) + b1_ref[c:c + 1, :]


def _cmpbias(cmp_pos, cmp_w1, cmp_b1):
    n = cmp_pos.shape[1] * cmp_pos.shape[2]
    return pl.pallas_call(
        _cmpbias_kernel,
        out_shape=jax.ShapeDtypeStruct((2, CMP_HID), F32),
        compiler_params=pltpu.CompilerParams(vmem_limit_bytes=VMEM_LIMIT),
        name="cmpbias",
    )(cmp_pos.reshape(2, n, 1), cmp_w1.reshape(2, n, CMP_HID), cmp_b1)


def _cmp_kernel(pt_ref, *refs, ppt):
    pages = refs[:ppt]
    nxt = refs[ppt]
    w1_ref, b1_ref, w2_ref, b2_ref, cos_ref, sin_ref, ck_ref, cv_ref, lhs, pbuf = refs[ppt + 1:]
    r = ppt * SUBLANES
    first = _first_half_mask(r)
    for c in range(2):
        for s in range(CMP_STRIDE):
            src = slice(s * 2 * KV_W + c * KV_W, s * 2 * KV_W + (c + 1) * KV_W)
            dst = slice(s * KV_W, (s + 1) * KV_W)
            for j in range(ppt):
                lhs[j * SUBLANES:(j + 1) * SUBLANES, dst] = pages[j][0, :, src]
            lhs[r:r + SUBLANES, dst] = nxt[0, :, src]
        p = jnp.dot(lhs[...].astype(BF16), w1_ref[c], preferred_element_type=F32)
        pbuf[...] = p[:, 2 * CMP_HID:4 * CMP_HID]
        hid = p[0:r, 0:2 * CMP_HID] + pbuf[pl.ds(1, r), :] + b1_ref[c]
        act = jax.nn.gelu(hid)
        comp = jnp.dot(act.astype(BF16), w2_ref[c], preferred_element_type=F32) + b2_ref[c]
        if c == 0:
            comp = _rope128(comp, cos_ref[...], sin_ref[...], first)
            out = ck_ref
        else:
            out = cv_ref
        for k in range(N_KV):
            out[0, k] = comp[:, k * HD:(k + 1) * HD]


def _cmp(pages3, pt_flat, nb, n_pages, w1p, b1p, w2p, b2p, cos_c, sin_c, name):
    ppt = min(32, n_pages)
    n_tiles = n_pages // ppt
    r = ppt * SUBLANES
    n_chunk = n_pages * SUBLANES

    def page_map(j):
        return lambda b, t, pt: (pt[b * n_pages + t * ppt + j], 0, 0)

    def next_map(b, t, pt):
        return (pt[b * n_pages + jnp.minimum(t * ppt + ppt, n_pages - 1)], 0, 0)

    in_specs = [pl.BlockSpec((1, SUBLANES, CHUNK_ROW), page_map(j)) for j in range(ppt)]
    in_specs.append(pl.BlockSpec((1, SUBLANES, CHUNK_ROW), next_map))
    const = lambda shp: pl.BlockSpec(shp, lambda b, t, pt: (0,) * len(shp))
    in_specs += [const(w1p.shape), const(b1p.shape), const(w2p.shape), const(b2p.shape),
                 pl.BlockSpec((r, LANES), lambda b, t, pt: (t, 0)), pl.BlockSpec((r, LANES), lambda b, t, pt: (t, 0))]
    hm = pl.BlockSpec((1, N_KV, r, HD), lambda b, t, pt: (b, 0, t, 0))
    grid_spec = pltpu.PrefetchScalarGridSpec(
        num_scalar_prefetch=1, grid=(nb, n_tiles), in_specs=in_specs, out_specs=[hm, hm],
        scratch_shapes=[pltpu.VMEM((r + SUBLANES, CMP_STRIDE * KV_W), F32), pltpu.VMEM((r + SUBLANES, 2 * CMP_HID), F32)])
    return pl.pallas_call(
        functools.partial(_cmp_kernel, ppt=ppt),
        grid_spec=grid_spec,
        out_shape=[jax.ShapeDtypeStruct((nb, N_KV, n_chunk, HD), F32)] * 2,
        compiler_params=_params(2), name=name,
    )(pt_flat, *([pages3] * (ppt + 1)), w1p, b1p, w2p, b2p, cos_c, sin_c)


def _softmax_rows(s, valid):
    s = jnp.where(valid, s, NEG_INF)
    m = jnp.max(s, axis=-1, keepdims=True)
    e = jnp.exp(s - m)
    return e / jnp.sum(e, axis=-1, keepdims=True)


def _attn_p_kernel(q_ref, ck_ref, cv_ref, ks_ref, vs_ref, kw_ref, vw_ref, gate_ref, band_ref, exp_ref, o_ref,
                   *, n_cmp_pad, n_blk, kc, t):
    qb = pl.program_id(2)
    start = qb * Q_BLOCK
    rows = QPK * Q_BLOCK
    q = q_ref[0].reshape(rows, HD)
    tpos = start + lax.broadcasted_iota(I32, (Q_BLOCK, 1), 0)
    qpos = start + lax.broadcasted_iota(I32, (rows, 1), 0) % Q_BLOCK

    s_c = lax.dot_general(q, ck_ref[0, 0].astype(BF16), _NT, preferred_element_type=F32)
    cmp_end = (lax.broadcasted_iota(I32, (1, n_cmp_pad), 1) + 2) * CMP_STRIDE - 1
    valid_c = cmp_end <= qpos
    p_c = _softmax_rows(s_c, valid_c) * valid_c.astype(F32)
    o_c = jnp.dot(p_c.astype(BF16), cv_ref[0, 0].astype(BF16), preferred_element_type=F32)

    pcs = p_c[0:Q_BLOCK] + p_c[Q_BLOCK:2 * Q_BLOCK] + p_c[2 * Q_BLOCK:3 * Q_BLOCK] + p_c[3 * Q_BLOCK:4 * Q_BLOCK]
    imp = jnp.dot(pcs, band_ref[...], preferred_element_type=F32, precision=lax.Precision.HIGHEST)
    blk = lax.broadcasted_iota(I32, (Q_BLOCK, n_blk), 1)
    cur = tpos // SEL_BLOCK
    causal = blk * SEL_BLOCK <= tpos
    forced = causal & ((blk == 0) | (blk == cur) | (blk == cur - 1))
    score = jnp.where(forced, FORCE_SCORE, jnp.where(causal, imp, -1.0))
    rank = jnp.zeros((Q_BLOCK, n_blk), F32)
    for bp in range(n_blk):
        col = score[:, bp:bp + 1]
        beats = (col > score) | ((col == score) & (bp < blk))
        rank = rank + beats.astype(F32)
    sel = (rank < float(min(N_SEL, n_blk))).astype(BF16)

    n_chunks = (start + Q_BLOCK + kc - 1) // kc

    def body(j, carry):
        m_i, l_i, acc = carry
        off = pl.multiple_of(j * kc, kc)
        kj = ks_ref[0, 0, pl.ds(off, kc), :]
        vj = vs_ref[0, 0, pl.ds(off, kc), :]
        s = lax.dot_general(q, kj, _NT, preferred_element_type=F32)
        mexp = jnp.dot(sel, exp_ref[j], preferred_element_type=F32)
        keypos = off + lax.broadcasted_iota(I32, (1, kc), 1)
        ok = ((mexp > 0.5) & (keypos <= tpos))[None]
        s = jnp.where(ok, s.reshape(QPK, Q_BLOCK, kc), NEG_INF).reshape(rows, kc)
        m_new = jnp.maximum(m_i, jnp.max(s, axis=-1, keepdims=True))
        alpha = jnp.exp(m_i - m_new)
        p = jnp.where(ok, jnp.exp(s - m_new).reshape(QPK, Q_BLOCK, kc), 0.0).reshape(rows, kc)
        l_new = alpha * l_i + jnp.sum(p, axis=-1, keepdims=True)
        acc_new = alpha * acc + jnp.dot(p.astype(BF16), vj, preferred_element_type=F32)
        return m_new, l_new, acc_new

    m0 = jnp.full((rows, 1), NEG_INF, F32)
    l0 = jnp.zeros((rows, 1), F32)
    a0 = jnp.zeros((rows, HD), F32)
    _, l_s, acc_s = lax.fori_loop(0, n_chunks, body, (m0, l0, a0))
    o_s = acc_s / l_s

    wlen = WINDOW + Q_BLOCK
    s0 = pl.multiple_of(jnp.maximum(start - WINDOW, 0), Q_BLOCK)
    kwin = kw_ref[0, 0, pl.ds(s0, wlen), :]
    vwin = vw_ref[0, 0, pl.ds(s0, wlen), :]
    s_w = lax.dot_general(q, kwin, _NT, preferred_element_type=F32)
    dist = qpos - (s0 + lax.broadcasted_iota(I32, (1, wlen), 1))
    p_w = _softmax_rows(s_w, (dist >= 0) & (dist <= WINDOW))
    o_w = jnp.dot(p_w.astype(BF16), vwin, preferred_element_type=F32)

    g = gate_ref[...]
    for hq in range(QPK):
        rs = slice(hq * Q_BLOCK, (hq + 1) * Q_BLOCK)
        o = (g[:, hq:hq + 1] * o_c[rs] + g[:, QPK + hq:QPK + hq + 1] * o_s[rs]
             + g[:, 2 * QPK + hq:2 * QPK + hq + 1] * o_w[rs])
        o_ref[0, :, hq * HD:(hq + 1) * HD] = o


def _attn_prompt(q_hm, ck, cv, ks, vs, kw, vw, gates, band, expand, *, nb, t):
    n_qb = t // Q_BLOCK
    n_cmp_pad = ck.shape[2]
    n_blk = band.shape[1]
    kc = expand.shape[2]
    kv_spec = lambda n: pl.BlockSpec((1, 1, n, HD), lambda b, k, i: (b, k, 0, 0))
    return pl.pallas_call(
        functools.partial(_attn_p_kernel, n_cmp_pad=n_cmp_pad, n_blk=n_blk, kc=kc, t=t),
        grid=(nb, N_KV, n_qb),
        in_specs=[pl.BlockSpec((1, QPK, Q_BLOCK, HD), lambda b, k, i: (b, k, i, 0)),
                  kv_spec(n_cmp_pad), kv_spec(n_cmp_pad), kv_spec(t), kv_spec(t), kv_spec(t), kv_spec(t),
                  pl.BlockSpec((Q_BLOCK, LANES), lambda b, k, i: (b * n_qb + i, k)),
                  pl.BlockSpec(band.shape, lambda b, k, i: (0, 0)),
                  pl.BlockSpec(expand.shape, lambda b, k, i: (0, 0, 0))],
        out_specs=pl.BlockSpec((1, Q_BLOCK, QPK * HD), lambda b, k, i: (b, i, k)),
        out_shape=jax.ShapeDtypeStruct((nb, t, ATTN_W), F32),
        compiler_params=_params(3), name="attn_prompt",
    )(q_hm, ck, cv, ks, vs, kw, vw, gates, band, expand)


def _attn_s1_kernel(q_ref, ck_ref, cv_ref, band_ref, oc_ref, imp_ref, *, n_chunk, past):
    q = q_ref[0]
    q16 = jnp.concatenate([q, jnp.zeros_like(q)], axis=0).astype(BF16)
    cmp_end = (lax.broadcasted_iota(I32, (1, n_chunk), 1) + 2) * CMP_STRIDE - 1
    valid = cmp_end <= past
    head = lax.broadcasted_iota(I32, (2 * N_HEADS, 1), 0)
    oc = jnp.zeros((2 * N_HEADS, HD), F32)
    imps = []
    for k in range(N_KV):
        s = lax.dot_general(q16, ck_ref[0, k].astype(BF16), _NT, preferred_element_type=F32)
        p = _softmax_rows(s, valid) * valid.astype(F32)
        in_grp = (head >= k * QPK) & (head < (k + 1) * QPK)
        p = jnp.where(in_grp, p, 0.0)
        oc = oc + jnp.dot(p.astype(BF16), cv_ref[0, k].astype(BF16), preferred_element_type=F32)
        pcs = jnp.sum(p, axis=0, keepdims=True)
        pcs8 = jnp.broadcast_to(pcs, (SUBLANES, n_chunk))
        imps.append(jnp.dot(pcs8, band_ref[...], preferred_element_type=F32,
                            precision=lax.Precision.HIGHEST)[0:1])
    oc_ref[0] = oc[0:N_HEADS]
    imp_ref[0] = jnp.concatenate(imps + [jnp.zeros((SUBLANES - N_KV, imps[0].shape[1]), F32)], axis=0)


def _topk_s_kernel(imp_ref, idx_ref, *, n_sel_blocks, past):
    imp = imp_ref[...]
    rows, nbp = imp.shape
    blk = lax.broadcasted_iota(I32, (rows, nbp), 1)
    cur = past // SEL_BLOCK
    causal = blk * SEL_BLOCK <= past
    forced = causal & ((blk == 0) | (blk == cur) | (blk == cur - 1))
    score = jnp.where(forced, FORCE_SCORE, jnp.where(causal, imp, -1.0))
    score = jnp.where(blk < n_sel_blocks, score, -2.0)
    lane = lax.broadcasted_iota(I32, (rows, LANES), 1)
    out = jnp.zeros((rows, LANES), I32)
    for r in range(min(N_SEL, n_sel_blocks)):
        m = jnp.max(score, axis=-1, keepdims=True)
        pick = jnp.min(jnp.where(score == m, blk, nbp), axis=-1, keepdims=True)
        out = jnp.where(lane == r, pick, out)
        score = jnp.where(blk == pick, -3.0, score)
    idx_ref[...] = out


def _attn_s2_kernel(pt_ref, idx_ref, *refs, n_pages, past, n_sel_blocks):
    blocks = refs[:N_KV * N_SEL]
    q_ref, oc_ref, kvs_ref, win_ref, kvw_ref, gate_ref, o_ref, kbuf, wbuf = refs[N_KV * N_SEL:]
    b = pl.program_id(0)
    q = q_ref[0]
    q16 = jnp.concatenate([q, jnp.zeros_like(q)], axis=0).astype(BF16)
    rpt = 2 * N_KV
    brows = SEL_BLOCK * rpt
    nk = N_SEL * brows
    wrows = win_ref.shape[1]
    wb = wrows // rpt
    kbuf[nk + SUBLANES:nk + LANES, :] = jnp.zeros((LANES - SUBLANES, HD), F32)
    wbuf[wrows + SUBLANES:wrows + LANES, :] = jnp.zeros((LANES - SUBLANES, HD), F32)
    wbuf[0:wrows, :] = win_ref[0]
    wbuf[wrows:wrows + SUBLANES, :] = kvw_ref[0]
    kbuf[nk:nk + SUBLANES, :] = kvs_ref[0]
    head = lax.broadcasted_iota(I32, (2 * N_HEADS, 1), 0)
    o_s = jnp.zeros((2 * N_HEADS, HD), F32)
    o_w = jnp.zeros((2 * N_HEADS, HD), F32)
    krow = lax.broadcasted_iota(I32, (1, nk + LANES), 1)
    kslot = krow // brows
    ktok = (krow % brows) // rpt
    wrow = lax.broadcasted_iota(I32, (1, wrows + LANES), 1)
    wpos = past - wb + wrow // rpt
    wdist = past - wpos
    new_blk = n_sel_blocks - 1
    for k in range(N_KV):
        in_grp = (head >= k * QPK) & (head < (k + 1) * QPK)
        bvec = jnp.full((1, nk + LANES), new_blk, I32)
        for j in range(N_SEL):
            kbuf[j * brows:(j + 1) * brows, :] = blocks[k * N_SEL + j][...]
            bvec = jnp.where(kslot == j, idx_ref[(b * N_KV + k) * LANES + j], bvec)
        cached = (kslot < N_SEL) & (bvec < new_blk)
        valid = ((krow % rpt) == k) & (cached | (kslot == N_SEL)) & (bvec * SEL_BLOCK + ktok <= past)
        kall = kbuf[...].astype(BF16)
        s = lax.dot_general(q16, kall, _NT, preferred_element_type=F32)
        p = jnp.where(in_grp, _softmax_rows(s, valid), 0.0)
        o_s = o_s + jnp.dot(pltpu.roll(p, N_KV, 1).astype(BF16), kall, preferred_element_type=F32)
        valid_w = ((wrow % rpt) == k) & (wdist >= 0) & (wdist <= WINDOW) & (wpos >= 0)
        wall = wbuf[...].astype(BF16)
        sw = lax.dot_general(q16, wall, _NT, preferred_element_type=F32)
        pw = jnp.where(in_grp, _softmax_rows(sw, valid_w), 0.0)
        o_w = o_w + jnp.dot(pltpu.roll(pw, N_KV, 1).astype(BF16), wall, preferred_element_type=F32)
    g = gate_ref[0]
    o_ref[0] = g[:, 0:1] * oc_ref[0] + g[:, 1:2] * o_s[0:N_HEADS] + g[:, 2:3] * o_w[0:N_HEADS]


def _attn_sample(q3, ck, cv, band_s, sel_blocks3, pt_flat, kvs_rows, cache_win2, kvw_rows, gates_hm,
                 *, nb, n_pages, past, n_sel_blocks):
    n_chunk = ck.shape[2]
    nbp = band_s.shape[1]
    oc, imp = pl.pallas_call(
        functools.partial(_attn_s1_kernel, n_chunk=n_chunk, past=past),
        grid=(nb,),
        in_specs=[pl.BlockSpec((1, N_HEADS, HD), lambda b: (b, 0, 0)),
                  pl.BlockSpec((1, N_KV, n_chunk, HD), lambda b: (b, 0, 0, 0)),
                  pl.BlockSpec((1, N_KV, n_chunk, HD), lambda b: (b, 0, 0, 0)),
                  pl.BlockSpec(band_s.shape, lambda b: (0, 0))],
        out_specs=[pl.BlockSpec((1, N_HEADS, HD), lambda b: (b, 0, 0)),
                   pl.BlockSpec((1, SUBLANES, nbp), lambda b: (b, 0, 0))],
        out_shape=[jax.ShapeDtypeStruct((nb, N_HEADS, HD), F32), jax.ShapeDtypeStruct((nb, SUBLANES, nbp), F32)],
        compiler_params=_params(1), name="attn_sample_cmp",
    )(q3, ck, cv, band_s)
    imp2 = imp[:, 0:N_KV, :].reshape(nb * N_KV, nbp)
    idx = pl.pallas_call(
        functools.partial(_topk_s_kernel, n_sel_blocks=n_sel_blocks, past=past),
        out_shape=jax.ShapeDtypeStruct((nb * N_KV, LANES), I32),
        compiler_params=pltpu.CompilerParams(vmem_limit_bytes=VMEM_LIMIT), name="topk_sample",
    )(imp2)
    idx_flat = idx.reshape(-1)

    def blk_map(k, j):
        def f(b, pt, ix):
            bidx = ix[(b * N_KV + k) * LANES + j]
            page = pt[b * n_pages + jnp.minimum(bidx // 2, n_pages - 1)]
            return (page * 2 + bidx % 2, 0)
        return f

    brows = SEL_BLOCK * 2 * N_KV
    in_specs = [pl.BlockSpec((brows, HD), blk_map(k, j)) for k in range(N_KV) for j in range(N_SEL)]
    wrows = cache_win2.shape[1]
    in_specs += [pl.BlockSpec((1, N_HEADS, HD), lambda b, pt, ix: (b, 0, 0)),
                 pl.BlockSpec((1, N_HEADS, HD), lambda b, pt, ix: (b, 0, 0)),
                 pl.BlockSpec((1, SUBLANES, HD), lambda b, pt, ix: (b, 0, 0)),
                 pl.BlockSpec((1, wrows, HD), lambda b, pt, ix: (b, 0, 0)),
                 pl.BlockSpec((1, SUBLANES, HD), lambda b, pt, ix: (b, 0, 0)),
                 pl.BlockSpec((1, N_HEADS, LANES), lambda b, pt, ix: (b, 0, 0))]
    grid_spec = pltpu.PrefetchScalarGridSpec(
        num_scalar_prefetch=2, grid=(nb,), in_specs=in_specs,
        out_specs=pl.BlockSpec((1, N_HEADS, HD), lambda b, pt, ix: (b, 0, 0)),
        scratch_shapes=[pltpu.VMEM((N_SEL * brows + LANES, HD), F32), pltpu.VMEM((wrows + LANES, HD), F32)])
    return pl.pallas_call(
        functools.partial(_attn_s2_kernel, n_pages=n_pages, past=past, n_sel_blocks=n_sel_blocks),
        grid_spec=grid_spec,
        out_shape=jax.ShapeDtypeStruct((nb, N_HEADS, HD), F32),
        compiler_params=_params(1), name="attn_sample_sel",
    )(pt_flat, idx_flat, *([sel_blocks3] * (N_KV * N_SEL)), q3, oc, kvs_rows, cache_win2, kvw_rows, gates_hm)


def _outp_kernel(xp_ref, convp_ref, attnp_ref, ga1p_ref, sc2p_ref, sh2p_ref,
                 xs_ref, convs_ref, attns_ref, ga1s_ref, sc2s_ref, sh2s_ref,
                 gc_ref, ga_ref, w_ref, g2_ref, wr_ref, x1_ref, hp_ref, lg_ref, *, n_prompt_tiles):
    is_p = pl.program_id(0) < n_prompt_tiles
    pick = lambda a, b: jnp.where(is_p, a, b)
    cn = _rms(pick(convp_ref[...], convs_ref[...]), gc_ref[...])
    an = _rms(pick(attnp_ref[...], attns_ref[...]), ga_ref[...])
    cat = jnp.concatenate([cn, an], axis=1).astype(BF16)
    y = jnp.dot(cat, w_ref[...], preferred_element_type=F32)
    x1 = pick(xp_ref[...], xs_ref[...]) + pick(ga1p_ref[0], ga1s_ref[0]) * y
    x1_ref[...] = x1
    hp = _rms(x1, g2_ref[...]) * (1.0 + pick(sc2p_ref[0], sc2s_ref[0])) + pick(sh2p_ref[0], sh2s_ref[0])
    hp_ref[...] = hp
    lg_ref[...] = jnp.dot(hp, wr_ref[...], preferred_element_type=F32, precision=lax.Precision.HIGHEST)


TOKEN_TILE = 512


def _outp(prompt, sample, g_conv, g_attn, w_out_b, g2, w_route, *, tpb):
    tm = TOKEN_TILE
    n_p = prompt[0].shape[0] // tm
    total = (n_p + 1) * tm
    last = n_p - 1
    prow = lambda w: pl.BlockSpec((tm, w), lambda i: (jnp.minimum(i, last), 0))
    srow = lambda w: pl.BlockSpec((tm, w), lambda i: (0, 0))
    pmod = pl.BlockSpec((1, 1, D_MODEL), lambda i: (jnp.minimum(i, last) // tpb, 0, 0))
    smod = pl.BlockSpec((1, tm, D_MODEL), lambda i: (0, 0, 0))
    vec = lambda w: pl.BlockSpec((1, w), lambda i: (0, 0))
    row = lambda w: pl.BlockSpec((tm, w), lambda i: (i, 0))
    in_specs = [prow(D_MODEL), prow(CONV_W), prow(ATTN_W), pmod, pmod, pmod,
                srow(D_MODEL), srow(CONV_W), srow(ATTN_W), smod, smod, smod,
                vec(CONV_W), vec(ATTN_W), pl.BlockSpec((D_MODEL, D_MODEL), lambda i: (0, 0)), vec(D_MODEL),
                pl.BlockSpec((D_MODEL, LANES), lambda i: (0, 0))]
    return pl.pallas_call(
        functools.partial(_outp_kernel, n_prompt_tiles=n_p),
        grid=(n_p + 1,), in_specs=in_specs,
        out_specs=[row(D_MODEL), row(D_MODEL), row(LANES)],
        out_shape=[jax.ShapeDtypeStruct((total, D_MODEL), F32), jax.ShapeDtypeStruct((total, D_MODEL), F32),
                   jax.ShapeDtypeStruct((total, LANES), F32)],
        compiler_params=_params(1), name="outp",
    )(*prompt, *sample, g_conv.reshape(1, -1), g_attn.reshape(1, -1), w_out_b, g2.reshape(1, -1), w_route)


def _route_kernel(lg_ref, bias_ref, tri_ref, o_ref, cnt_ref, carry, *, tm, n_valid):
    i = pl.program_id(0)

    @pl.when(i == 0)
    def _():
        carry[...] = jnp.zeros_like(carry)

    lane = lax.broadcasted_iota(I32, (tm, LANES), 1)
    rowid = i * tm + lax.broadcasted_iota(I32, (tm, 1), 0)
    live = rowid < n_valid
    lg = lg_ref[...] + bias_ref[...]
    is_g = lane < N_GROUPS
    lgg = jnp.where(is_g, lg, NEG_INF)
    gmax = jnp.max(lgg, axis=-1, keepdims=True)
    grp = jnp.min(jnp.where(is_g & (lgg == gmax), lane, LANES), axis=-1, keepdims=True)
    p_grp = 1.0 / jnp.sum(jnp.where(is_g, jnp.exp(lgg - gmax), 0.0), axis=-1, keepdims=True)
    eid = lane - N_GROUPS
    in_grp = (eid >= grp * EPG) & (eid < (grp + 1) * EPG)
    le = jnp.where(in_grp, lg, NEG_INF)
    v1 = jnp.max(le, axis=-1, keepdims=True)
    e1 = jnp.min(jnp.where(in_grp & (le == v1), eid, LANES), axis=-1, keepdims=True)
    le2 = jnp.where(eid == e1, NEG_INF, le)
    v2 = jnp.max(le2, axis=-1, keepdims=True)
    e2 = jnp.min(jnp.where(in_grp & (eid != e1) & (le2 == v2), eid, LANES), axis=-1, keepdims=True)
    ex2 = jnp.exp(v2 - v1)
    w1 = p_grp * (1.0 / (1.0 + ex2))
    w2 = p_grp * (ex2 / (1.0 + ex2))
    oh1 = ((lane == e1) & live).astype(F32)
    oh2 = ((lane == e2) & live).astype(F32)
    both = oh1 + oh2
    before = jnp.dot(tri_ref[...], both.astype(BF16), preferred_element_type=F32) + carry[0:1, :]
    r1 = jnp.sum(oh1 * before, axis=-1, keepdims=True)
    r2 = jnp.sum(oh2 * before, axis=-1, keepdims=True)
    carry[0:1, :] = carry[0:1, :] + jnp.sum(both, axis=0, keepdims=True)
    out = jnp.where(lane == 0, e1.astype(F32), 0.0)
    out = jnp.where(lane == 1, e2.astype(F32), out)
    out = jnp.where(lane == 2, w1, out)
    out = jnp.where(lane == 3, w2, out)
    out = jnp.where(lane == 4, r1, out)
    out = jnp.where(lane == 5, r2, out)
    o_ref[...] = out
    cnt_ref[...] = carry[...]


def _route(logits, bias_row, n_valid):
    total = logits.shape[0]
    tm = TOKEN_TILE
    n_tiles = total // tm
    tri =(np.arange(tm)[:, None] > np.arange(tm)[None, :]).astype(np.float32)
    return pl.pallas_call(
        functools.partial(_route_kernel, tm=tm, n_valid=n_valid),
        grid=(n_tiles,),
        in_specs=[pl.BlockSpec((tm, LANES), lambda i: (i, 0)), pl.BlockSpec((1, LANES), lambda i: (0, 0)),
                  pl.BlockSpec((tm, tm), lambda i: (0, 0))],
        out_specs=[pl.BlockSpec((tm, LANES), lambda i: (i, 0)), pl.BlockSpec((SUBLANES, LANES), lambda i: (0, 0))],
        out_shape=[jax.ShapeDtypeStruct((total, LANES), F32), jax.ShapeDtypeStruct((SUBLANES, LANES), F32)],
        scratch_shapes=[pltpu.VMEM((SUBLANES, LANES), F32)],
        compiler_params=_params(1), name="route",
    )(logits, bias_row, jnp.asarray(tri, BF16))


def _row_copy(src_hbm, row, dst, slot, r, sem):
    return pltpu.make_async_copy(src_hbm.at[pl.ds(row, 1), :], dst.at[slot, pl.ds(r, 1), :], sem.at[slot])


def _experts_kernel(blk_e_ref, tok_ref, x_hbm, wg_ref, wu_ref, wd_ref, o_ref, xbuf, sem, wg_b, wu_b, wd_b, *, n_blocks):
    i = pl.program_id(0)
    slot = i % 2

    def issue(blk, s):
        for r in range(MOE_BLOCK):
            _row_copy(x_hbm, tok_ref[blk * MOE_BLOCK + r], xbuf, s, r, sem).start()

    @pl.when(i == 0)
    def _():
        issue(0, 0)

    @pl.when(i + 1 < n_blocks)
    def _():
        issue(i + 1, 1 - slot)

    changed = jnp.logical_or(i == 0, blk_e_ref[i] != blk_e_ref[jnp.maximum(i - 1, 0)])

    @pl.when(changed)
    def _():
        wg_b[...] = wg_ref[0].astype(BF16)
        wu_b[...] = wu_ref[0].astype(BF16)
        wd_b[...] = wd_ref[0].astype(BF16)

    for r in range(MOE_BLOCK):
        _row_copy(x_hbm, 0, xbuf, slot, r, sem).wait()
    x = xbuf[slot].astype(BF16)
    g = jnp.dot(x, wg_b[...], preferred_element_type=F32)
    u = jnp.dot(x, wu_b[...], preferred_element_type=F32)
    h = (g * jax.nn.sigmoid(g)) * u
    o_ref[...] = jnp.dot(h.astype(BF16), wd_b[...], preferred_element_type=F32)


def _experts(blk_e, slot_tok, hp_all, w_gate, w_up, w_down, n_blocks):
    grid_spec = pltpu.PrefetchScalarGridSpec(
        num_scalar_prefetch=2, grid=(n_blocks,),
        in_specs=[pl.BlockSpec(memory_space=pl.ANY),
                  pl.BlockSpec((1, D_MODEL, D_EXPERT), lambda i, be, st: (be[i], 0, 0)),
                  pl.BlockSpec((1, D_MODEL, D_EXPERT), lambda i, be, st: (be[i], 0, 0)),
                  pl.BlockSpec((1, D_EXPERT, D_MODEL), lambda i, be, st: (be[i], 0, 0))],
        out_specs=pl.BlockSpec((MOE_BLOCK, D_MODEL), lambda i, be, st: (i, 0)),
        scratch_shapes=[pltpu.VMEM((2, MOE_BLOCK, D_MODEL), F32), pltpu.SemaphoreType.DMA((2,)),
                        pltpu.VMEM((D_MODEL, D_EXPERT), BF16), pltpu.VMEM((D_MODEL, D_EXPERT), BF16),
                        pltpu.VMEM((D_EXPERT, D_MODEL), BF16)])
    return pl.pallas_call(
        functools.partial(_experts_kernel, n_blocks=n_blocks),
        grid_spec=grid_spec,
        out_shape=jax.ShapeDtypeStruct((n_blocks * MOE_BLOCK, D_MODEL), F32),
        compiler_params=_params(1), name="experts",
    )(blk_e, slot_tok, hp_all, w_gate, w_up, w_down)


def _final_kernel(dest_ref, yb_hbm, x1_ref, wt_ref, gate2_ref, gf_ref, o_ref, ybuf, sem, *, tm, n_tiles, row0):
    i = pl.program_id(0)
    slot = i % 2

    def copy(tile, s, r, k):
        d = dest_ref[(row0 + tile * tm + r) * 2 + k]
        return pltpu.make_async_copy(yb_hbm.at[pl.ds(d, 1), :], ybuf.at[s, k, pl.ds(r, 1), :], sem.at[s])

    def issue(tile, s):
        for r in range(tm):
            for k in range(2):
                copy(tile, s, r, k).start()

    @pl.when(i == 0)
    def _():
        issue(0, 0)

    @pl.when(i + 1 < n_tiles)
    def _():
        issue(i + 1, 1 - slot)

    for r in range(tm):
        for k in range(2):
            pltpu.make_async_copy(yb_hbm.at[pl.ds(0, 1), :], ybuf.at[slot, k, pl.ds(r, 1), :], sem.at[slot]).wait()
    wt = wt_ref[...]
    f = wt[:, 2:3] * ybuf[slot, 0] + wt[:, 3:4] * ybuf[slot, 1]
    x2 = x1_ref[...] + gate2_ref[0] * f
    o_ref[...] = _rms(x2, gf_ref[...])


def _final(dest_flat, yb, x1_all, route_rows, gate2, final_g, *, rows, tpb, per_row, row0):
    tm = min(128, rows)
    n_tiles = rows // tm
    blk0 = row0 // tm
    mod = (pl.BlockSpec((1, tm, D_MODEL), lambda i, d: (0, i, 0)) if per_row
           else pl.BlockSpec((1, 1, D_MODEL), lambda i, d: (i // tpb, 0, 0)))
    grid_spec = pltpu.PrefetchScalarGridSpec(
        num_scalar_prefetch=1, grid=(n_tiles,),
        in_specs=[pl.BlockSpec(memory_space=pl.ANY),
                  pl.BlockSpec((tm, D_MODEL), lambda i, d: (blk0 + i, 0)),
                  pl.BlockSpec((tm, LANES), lambda i, d: (blk0 + i, 0)),
                  mod, pl.BlockSpec((1, D_MODEL), lambda i, d: (0, 0))],
        out_specs=pl.BlockSpec((tm, D_MODEL), lambda i, d: (i, 0)),
        scratch_shapes=[pltpu.VMEM((2, 2, tm, D_MODEL), F32), pltpu.SemaphoreType.DMA((2,))])
    return pl.pallas_call(
        functools.partial(_final_kernel, tm=tm, n_tiles=n_tiles, row0=row0),
        grid_spec=grid_spec,
        out_shape=jax.ShapeDtypeStruct((rows, D_MODEL), F32),
        compiler_params=_params(1), name="final_sample" if per_row else "final_prompt",
    )(dest_flat, yb, x1_all, route_rows, gate2, final_g.reshape(1, -1))


def _rope_tables(pos):
    inv = ROPE_THETA ** (-jnp.arange(HALF, dtype=F32) / HALF)
    ang = pos.astype(F32)[:, None] * inv[None, :]
    cos = jnp.tile(jnp.cos(ang), (1, LANES // HALF))
    sin = jnp.sin(ang)
    sin_s = jnp.tile(jnp.concatenate([-sin, sin], axis=1), (1, LANES // HD))
    return cos, sin_s


def _pack_w_in(w_in):
    gl = w_in[:, _C_G:_C_G + 3 * N_HEADS].reshape(D_MODEL, 3, N_KV, QPK)
    gcols = []
    for k in range(N_KV):
        gk = gl[:, :, k, :].reshape(D_MODEL, 3 * QPK)
        gcols.append(jnp.pad(gk, ((0, 0), (0, LANES - 3 * QPK))))
    return jnp.concatenate([w_in[:, :_C_G]] + gcols, axis=1).astype(BF16)


def _pack_cmp_weights(cmp_w1, cmp_w2, bias, cmp_b2):
    w1 = cmp_w1.reshape(2, 2, CMP_STRIDE, HD, CMP_HID)
    eye = jnp.eye(N_KV, dtype=F32)
    w1p = jnp.einsum('crsdh,pk->cspdrkh', w1, eye).reshape(2, CMP_STRIDE * KV_W, 2 * N_KV * CMP_HID)
    w2p = jnp.einsum('chd,pk->cphkd', cmp_w2, eye).reshape(2, N_KV * CMP_HID, KV_W)
    b1p = jnp.tile(bias, (1, N_KV)).reshape(2, 1, N_KV * CMP_HID)
    b2p = jnp.tile(cmp_b2, (1, N_KV)).reshape(2, 1, KV_W)
    return w1p.astype(BF16), b1p, w2p.astype(BF16), b2p


def _band(n_cmp_pad, n_cmp, n_blk_pad, n_blk):
    n = np.arange(n_cmp_pad)[:, None]
    b = np.arange(n_blk_pad)[None, :]
    r = SEL_BLOCK // CMP_STRIDE
    m = (n >= r * b - 1) & (n <= r * b + r - 1) & (n < n_cmp) & (b < n_blk)
    return jnp.asarray(m.astype(np.float32))


def _expand(t, kc):
    n_chunks = t // kc
    key = np.arange(t).reshape(n_chunks, 1, kc)
    blk = np.arange(t // SEL_BLOCK).reshape(1, -1, 1)
    return jnp.asarray((key // SEL_BLOCK == blk).astype(np.float32), BF16)


def kernel(x_prompt, x_sample, c_prompt, c_sample, cache_cmp_kv, cache_sel_kv, cache_win_kv, state_conv, page_table,
           ln1_g, ln2_g, w_ada, b_ada, w_in, w_conv, cmp_pos, cmp_w1, cmp_b1, cmp_w2, cmp_b2, g_out_conv, g_out_attn,
           w_out, w_route_group, b_route_group, w_route_expert, b_route_expert, w_gate, w_up, w_down, final_g):
    depth = w_in.shape[0]
    assert depth == 1, "single-layer step"
    nb, t, _ = x_prompt.shape
    ns, ts, _ = x_sample.shape
    assert ts == 1 and t % 512 == 0 and t >= WINDOW + Q_BLOCK
    n_pool = cache_cmp_kv.shape[1]
    n_pages = page_table.shape[1]
    past = n_pages * PAGE
    wb = cache_win_kv.shape[2]
    assert wb == WINDOW
    l = 0

    n_c = nb + ns
    c_all = jnp.pad(jnp.concatenate([c_prompt, c_sample], axis=0), ((0, (-n_c) % SUBLANES), (0, 0)))
    mods = _ada(c_all, w_ada[l], b_ada[l])
    sh1, sc1, ga1, sh2, sc2, ga2 = [mods[:, j * D_MODEL:(j + 1) * D_MODEL] for j in range(6)]
    pr = lambda a: a[0:nb].reshape(nb, 1, D_MODEL)
    sr = lambda a: a[nb:nb + ns].reshape(1, ns, D_MODEL)

    w_pack = _pack_w_in(w_in[l])
    wconv8 = jnp.pad(w_conv[l], ((0, SUBLANES - CONV_K), (0, 0)))
    cos_p, sin_p = _rope_tables(jnp.arange(t, dtype=I32))
    cos_s, sin_s = _rope_tables(jnp.full((1,), past, I32))
    xp2 = x_prompt.reshape(nb * t, D_MODEL)
    xs2 = x_sample.reshape(ns, D_MODEL)
    (conv_p, cst_p, q_p, kvc_p, kvc_rows_p, kvs_rows_p, kvw_rows_p, ks_p, vs_p, kw_p, vw_p, gates_p) = _proj(
        xp2, ln1_g[l], pr(sc1), pr(sh1), w_pack, wconv8, cos_p, sin_p, nb=nb, t=t, sample=False)
    (conv_s, cst_s, q_s, _, kvc_rows_s, kvs_rows_s, kvw_rows_s, _, _, _, _, gates_s) = _proj(
        xs2, ln1_g[l], sr(sc1), sr(sh1), w_pack, wconv8, cos_s, sin_s, nb=ns, t=1, sample=True,
        prev=(state_conv[l][:, 0], state_conv[l][:, 1]))

    bias = _cmpbias(cmp_pos[l], cmp_w1[l], cmp_b1[l])
    w1p, b1p, w2p, b2p = _pack_cmp_weights(cmp_w1[l], cmp_w2[l], bias, cmp_b2[l])
    pp = t // PAGE
    cos_cp, sin_cp = _rope_tables((jnp.arange(t // CMP_STRIDE, dtype=I32) + 2) * CMP_STRIDE - 1)
    ck_p, cv_p = _cmp(kvc_p.reshape(nb * pp, SUBLANES, CHUNK_ROW), jnp.arange(nb * pp, dtype=I32), nb, pp,
                      w1p, b1p, w2p, b2p, cos_cp, sin_cp, "cmp_prompt")
    pt_flat = page_table.reshape(-1).astype(I32)
    cos_cs, sin_cs = _rope_tables((jnp.arange(past // CMP_STRIDE, dtype=I32) + 2) * CMP_STRIDE - 1)
    ck_s, cv_s = _cmp(cache_cmp_kv[l].reshape(n_pool, SUBLANES, CHUNK_ROW), pt_flat, ns, n_pages,
                      w1p, b1p, w2p, b2p, cos_cs, sin_cs, "cmp_sample")

    n_chunk_p = t // CMP_STRIDE
    n_blk_p = t // SEL_BLOCK
    band_p = _band(n_chunk_p, n_chunk_p - 1, n_blk_p, n_blk_p)
    attn_p = _attn_prompt(q_p, ck_p, cv_p, ks_p, vs_p, kw_p, vw_p, gates_p, band_p, _expand(t, 512), nb=nb, t=t)

    n_chunk_s = past // CMP_STRIDE
    n_sel_s = -(-(past + 1) // SEL_BLOCK)
    nbp = -(-n_sel_s // LANES) * LANES
    band_s = _band(n_chunk_s, (past + 1) // CMP_STRIDE - 1, nbp, n_sel_s)
    q3 = q_s.reshape(N_HEADS, ns, HD).transpose(1, 0, 2).astype(F32)
    gs = gates_s.reshape(ns, N_KV, LANES)[:, :, :3 * QPK].reshape(ns, N_KV, 3, QPK)
    gates_hm = jnp.pad(gs.transpose(0, 1, 3, 2).reshape(ns, N_HEADS, 3), ((0, 0), (0, 0), (0, LANES - 3)))
    rpt = 2 * N_KV
    new_rows = lambda a: jnp.pad(a.reshape(ns, rpt, HD), ((0, 0), (0, SUBLANES - rpt), (0, 0)))
    attn_s = _attn_sample(q3, ck_s, cv_s, band_s, cache_sel_kv[l].reshape(n_pool * PAGE * rpt, HD), pt_flat,
                          new_rows(kvs_rows_s), cache_win_kv[l].reshape(ns, wb * rpt, HD), new_rows(kvw_rows_s),
                          gates_hm, nb=ns, n_pages=n_pages, past=past, n_sel_blocks=n_sel_s).reshape(ns, ATTN_W)

    total = nb * t + ns
    w_out_b = w_out[l].astype(BF16)
    w_route = jnp.pad(jnp.concatenate([w_route_group[l], w_route_expert[l]], axis=1),
                      ((0, 0), (0, LANES - N_GROUPS - N_EXPERTS)))
    b_route = jnp.pad(jnp.concatenate([b_route_group[l], b_route_expert[l]]), (0, LANES - N_GROUPS - N_EXPERTS))
    tile_pad = lambda a: jnp.pad(a, ((0, TOKEN_TILE - ns), (0, 0)))
    smod = lambda a: tile_pad(a[nb:nb + ns]).reshape(1, TOKEN_TILE, D_MODEL)
    x1_all, hp_all, lg_all = _outp(
        (xp2, conv_p, attn_p.reshape(nb * t, ATTN_W), pr(ga1), pr(sc2), pr(sh2)),
        (tile_pad(xs2), tile_pad(conv_s), tile_pad(attn_s), smod(ga1), smod(sc2), smod(sh2)),
        g_out_conv[l], g_out_attn[l], w_out_b, ln2_g[l], w_route, tpb=t // TOKEN_TILE)

    route, counts = _route(lg_all, b_route.reshape(1, LANES), total)
    e = route[:total, 0:2].astype(I32)
    rank = route[:total, 4:6].astype(I32)
    cnt = counts[0, :N_EXPERTS].astype(I32)
    padded = (cnt + MOE_BLOCK - 1) // MOE_BLOCK * MOE_BLOCK
    pad_end = jnp.cumsum(padded)
    pad_start = pad_end - padded
    dest = pad_start[e] + rank
    m_slots = total * 2
    n_blocks = -(-(m_slots + N_EXPERTS * (MOE_BLOCK - 1)) // MOE_BLOCK)
    tok = jnp.broadcast_to(jnp.arange(total, dtype=I32)[:, None], (total, 2))
    slot_tok = jnp.zeros((n_blocks * MOE_BLOCK,), I32).at[dest.reshape(-1)].set(tok.reshape(-1))
    blk_start = jnp.arange(n_blocks, dtype=I32) * MOE_BLOCK
    blk_e = jnp.minimum(jnp.sum((pad_end[None, :] <= blk_start[:, None]).astype(I32), axis=1), N_EXPERTS - 1)

    yb = _experts(blk_e, slot_tok, hp_all, w_gate[l], w_up[l], w_down[l], n_blocks)
    dest_flat = dest.reshape(-1)
    y_p = _final(dest_flat, yb, x1_all, route, pr(ga2), final_g, rows=nb * t, tpb=t // 128, per_row=False, row0=0)
    y_s = _final(dest_flat, yb, x1_all, route, sr(ga2), final_g, rows=ns, tpb=1, per_row=True, row0=nb * t)

    kv_shape = (2, N_KV, HD)
    y_prompt = y_p.reshape(nb, t, D_MODEL)
    y_sample = y_s.reshape(ns, 1, D_MODEL)
    new_cmp_prompt = kvc_rows_p.reshape((1, nb, t) + kv_shape)
    new_cmp_sample = kvc_rows_s.reshape((1, ns, 1) + kv_shape)
    new_sel_prompt = kvs_rows_p.reshape((1, nb, t) + kv_shape)
    new_sel_sample = kvs_rows_s.reshape((1, ns, 1) + kv_shape)
    new_win_prompt = kvw_rows_p.reshape((nb, t) + kv_shape)[:, t - WINDOW:][None]
    new_win_sample = jnp.concatenate([cache_win_kv[l][:, 1:], kvw_rows_s.reshape((ns, 1) + kv_shape)], axis=1)[None]
    new_conv_prompt = cst_p[:, SUBLANES - (CONV_K - 1):][None]
    new_conv_sample = jnp.stack([state_conv[l][:, 1], cst_s], axis=1)[None]
    return (y_prompt, y_sample, new_cmp_prompt, new_cmp_sample, new_sel_prompt, new_sel_sample,
            new_win_prompt, new_win_sample, new_conv_prompt, new_conv_sample)
```

```python
import functools

import numpy as np
import jax
import jax.numpy as jnp
from jax import lax
from jax.experimental import pallas as pl
from jax.experimental.pallas import tpu as pltpu

F32 = jnp.float32
BF16 = jnp.bfloat16
I32 = jnp.int32

D_MODEL = 1024
CONV_W = 512
ATTN_W = 512
HD = 64
HALF = HD // 2
N_HEADS = 8
N_KV = 2
QPK = 4
KV_W = N_KV * HD
CONV_K = 3
PAGE = 128
CMP_STRIDE = 16
CMP_HID = 128
SEL_BLOCK = 64
N_SEL = 16
WINDOW = 512
Q_BLOCK = 128
ROPE_THETA = 10000.0
N_GROUPS = 4
EPG = 8
N_EXPERTS = 32
D_EXPERT = 512
MOE_BLOCK = 128
NORM_EPS = 1e-6
NEG_INF = -1e30
FORCE_SCORE = 1e4
LANES = 128
SUBLANES = 8
CHUNK_ROW = CMP_STRIDE * 2 * KV_W
VMEM_LIMIT = 56 * 1024 * 1024

_NT = (((1,), (1,)), ((), ()))


def _params(n_axes):
    return pltpu.CompilerParams(dimension_semantics=("arbitrary",) * n_axes,
                                vmem_limit_bytes=VMEM_LIMIT)


def _rms(x, g):
    return x * lax.rsqrt(jnp.mean(x * x, axis=-1, keepdims=True) + NORM_EPS) * g


def _rope128(x, cos, sin_signed, first_half):
    xr = jnp.where(first_half, pltpu.roll(x, LANES - HALF, 1), pltpu.roll(x, HALF, 1))
    return x * cos + xr * sin_signed


def _first_half_mask(rows):
    lane = lax.broadcasted_iota(I32, (rows, LANES), 1)
    return (lane % HD) < HALF


def _ada_kernel(c_ref, w_ref, b_ref, o_ref):
    c = c_ref[...]
    s = c * jax.nn.sigmoid(c)
    o_ref[...] = jnp.dot(s.astype(BF16), w_ref[...].astype(BF16), preferred_element_type=F32) + b_ref[...]


def _ada(c_all, w_ada, b_ada):
    m, d = c_all.shape
    n = w_ada.shape[1]
    tn = 1024
    return pl.pallas_call(
        _ada_kernel,
        grid=(n // tn,),
        in_specs=[pl.BlockSpec((m, d), lambda j: (0, 0)),
                  pl.BlockSpec((d, tn), lambda j: (0, j)),
                  pl.BlockSpec((1, tn), lambda j: (0, j))],
        out_specs=pl.BlockSpec((m, tn), lambda j: (0, j)),
        out_shape=jax.ShapeDtypeStruct((m, n), F32),
        compiler_params=_params(1),
        name="ada",
    )(c_all, w_ada, b_ada.reshape(1, n))


_C_B, _C_C, _C_U, _C_Q, _C_KVC, _C_KVS, _C_KVW, _C_G, _C_END = 0, 512, 1024, 1536, 2048, 2304, 2560, 2816, 3072


def _proj_kernel(*refs, tm, tpb, sample):
    if sample:
        (x_ref, g1_ref, sc_ref, sh_ref, w_ref, wc_ref, cos_ref, sin_ref, p0_ref, p1_ref,
         conv_ref, cst_ref, q_ref, kvc_ref, kvc_il_ref, kvs_ref, kvw_ref, ks_ref, vs_ref, kw_ref, vw_ref, gate_ref,
         ilbuf) = refs
        vbuf = None
    else:
        (x_ref, g1_ref, sc_ref, sh_ref, w_ref, wc_ref, cos_ref, sin_ref,
         conv_ref, cst_ref, q_ref, kvc_ref, kvc_il_ref, kvs_ref, kvw_ref, ks_ref, vs_ref, kw_ref, vw_ref, gate_ref,
         ilbuf, vbuf) = refs
    i = pl.program_id(0)
    x = x_ref[...]
    h = _rms(x, g1_ref[...]) * (1.0 + sc_ref[0]) + sh_ref[0]
    hb = h.astype(BF16)

    zc = jnp.dot(hb, w_ref[:, _C_B:_C_Q], preferred_element_type=F32)
    b_g = zc[:, 0:CONV_W]
    v = zc[:, CONV_W:2 * CONV_W] * zc[:, 2 * CONV_W:3 * CONV_W]
    wc = wc_ref[...]
    if sample:
        y = wc[0:1] * p0_ref[...] + wc[1:2] * p1_ref[...] + wc[2:3] * v
        cst_ref[...] = v
    else:
        @pl.when(i % tpb == 0)
        def _():
            vbuf[0:SUBLANES, :] = jnp.zeros((SUBLANES, CONV_W), F32)
        vbuf[SUBLANES:SUBLANES + tm, :] = v
        y = wc[0:1] * vbuf[pl.ds(SUBLANES - 2, tm), :] + wc[1:2] * vbuf[pl.ds(SUBLANES - 1, tm), :] + wc[2:3] * v
        tail = vbuf[tm:tm + SUBLANES, :]
        cst_ref[0] = tail
        vbuf[0:SUBLANES, :] = tail
    conv_ref[...] = b_g * y

    cos = cos_ref[...]
    sin_s = sin_ref[...]
    first = _first_half_mask(tm)

    zq = jnp.dot(hb, w_ref[:, _C_Q:_C_KVC], preferred_element_type=F32)
    for gq in range(ATTN_W // LANES):
        qr = _rope128(zq[:, gq * LANES:(gq + 1) * LANES], cos, sin_s, first) * (HD ** -0.5)
        q_ref[0, 2 * gq] = qr[:, 0:HD].astype(BF16)
        q_ref[0, 2 * gq + 1] = qr[:, HD:LANES].astype(BF16)

    def store_rows(out_ref, halves):
        for j in range(2 * N_KV):
            piece = halves[j // N_KV]
            if j % N_KV == 1:
                piece = pltpu.roll(piece, HD, 1)
            ilbuf[pl.ds(j, tm, stride=2 * N_KV), :] = piece
        out_ref[...] = ilbuf[:, 0:HD]

    zkv = jnp.dot(hb, w_ref[:, _C_KVC:_C_G], preferred_element_type=F32)
    kvc_ref[...] = zkv[:, 0:2 * KV_W]
    store_rows(kvc_il_ref, (zkv[:, 0:KV_W], zkv[:, KV_W:2 * KV_W]))
    for base, kv_ref, kh_ref, vh_ref in ((2 * KV_W, kvs_ref, ks_ref, vs_ref), (4 * KV_W, kvw_ref, kw_ref, vw_ref)):
        kr = _rope128(zkv[:, base:base + KV_W], cos, sin_s, first)
        vv = zkv[:, base + KV_W:base + 2 * KV_W]
        store_rows(kv_ref, (kr, vv))
        for k in range(N_KV):
            kh_ref[0, k] = kr[:, k * HD:(k + 1) * HD].astype(BF16)
            vh_ref[0, k] = vv[:, k * HD:(k + 1) * HD].astype(BF16)

    zg = jnp.dot(hb, w_ref[:, _C_G:_C_END], preferred_element_type=F32)
    gate_ref[...] = jax.nn.sigmoid(zg)


def _proj(x2d, g1, sc, sh, w_pack, w_conv, cos_t, sin_t, *, nb, t, sample, prev=None):
    rows = nb * t
    tm = min(512, rows) if not sample else rows
    tpb = (t // tm) if not sample else 1
    n_tiles = rows // tm
    f = lambda a: jax.ShapeDtypeStruct(a, F32)
    b = lambda a: jax.ShapeDtypeStruct(a, BF16)
    if sample:
        mod_spec = pl.BlockSpec((1, tm, D_MODEL), lambda i: (0, 0, 0))
        tab_spec = pl.BlockSpec((1, LANES), lambda i: (0, 0))
        cst_shape, cst_spec = f((rows, CONV_W)), pl.BlockSpec((tm, CONV_W), lambda i: (0, 0))
        hm = lambda i: (0, 0, i, 0)
        hb_, ht_ = 1, rows
    else:
        mod_spec = pl.BlockSpec((1, 1, D_MODEL), lambda i: (i // tpb, 0, 0))
        tab_spec = pl.BlockSpec((tm, LANES), lambda i: (i % tpb, 0))
        cst_shape, cst_spec = f((nb, SUBLANES, CONV_W)), pl.BlockSpec((1, SUBLANES, CONV_W), lambda i: (i // tpb, 0, 0))
        hm = lambda i: (i // tpb, 0, i % tpb, 0)
        hb_, ht_ = nb, t
    row = lambda w: pl.BlockSpec((tm, w), lambda i: (i, 0))
    in_specs = [row(D_MODEL), pl.BlockSpec((1, D_MODEL), lambda i: (0, 0)), mod_spec, mod_spec,
                pl.BlockSpec((D_MODEL, _C_END), lambda i: (0, 0)),
                pl.BlockSpec((SUBLANES, CONV_W), lambda i: (0, 0)), tab_spec, tab_spec]
    args = [x2d, g1.reshape(1, D_MODEL), sc, sh, w_pack, w_conv, cos_t, sin_t]
    scratch = [pltpu.VMEM((2 * N_KV * tm, LANES), F32)]
    if sample:
        in_specs += [row(CONV_W), row(CONV_W)]
        args += [prev[0], prev[1]]
    else:
        scratch.append(pltpu.VMEM((tm + SUBLANES, CONV_W), F32))
    il_rows = 2 * N_KV * rows
    il = pl.BlockSpec((2 * N_KV * tm, HD), lambda i: (i, 0))
    out_shape = [f((rows, CONV_W)), cst_shape, b((hb_, N_HEADS, ht_, HD)),
                 f((rows, 2 * KV_W)), f((il_rows, HD)), f((il_rows, HD)), f((il_rows, HD)),
                 b((hb_, N_KV, ht_, HD)), b((hb_, N_KV, ht_, HD)), b((hb_, N_KV, ht_, HD)), b((hb_, N_KV, ht_, HD)),
                 f((rows, 2 * LANES))]
    out_specs = [row(CONV_W), cst_spec, pl.BlockSpec((1, N_HEADS, tm, HD), hm),
                 row(2 * KV_W), il, il, il,
                 pl.BlockSpec((1, N_KV, tm, HD), hm), pl.BlockSpec((1, N_KV, tm, HD), hm),
                 pl.BlockSpec((1, N_KV, tm, HD), hm), pl.BlockSpec((1, N_KV, tm, HD), hm),
                 row(2 * LANES)]
    return pl.pallas_call(
        functools.partial(_proj_kernel, tm=tm, tpb=tpb, sample=sample),
        grid=(n_tiles,), in_specs=in_specs, out_specs=out_specs, out_shape=out_shape,
        scratch_shapes=scratch, compiler_params=_params(1),
        name="proj_sample" if sample else "proj_prompt",
    )(*args)


def _cmpbias_kernel(pos_ref, w_ref, b1_ref, o_ref):
    for c in range(2):
        o_ref[c:c + 1, :] = jnp.sum(pos_ref[c] * w_ref[c], axis=0, keepdims=True) + b1_ref[c:c + 1, :]


def _cmpbias(cmp_pos, cmp_w1, cmp_b1):
    n = cmp_pos.shape[1] * cmp_pos.shape[2]
    return pl.pallas_call(
        _cmpbias_kernel,
        out_shape=jax.ShapeDtypeStruct((2, CMP_HID), F32),
        compiler_params=pltpu.CompilerParams(vmem_limit_bytes=VMEM_LIMIT),
        name="cmpbias",
    )(cmp_pos.reshape(2, n, 1), cmp_w1.reshape(2, n, CMP_HID), cmp_b1)


def _cmp_kernel(pt_ref, *refs, ppt):
    pages = refs[:ppt]
    nxt = refs[ppt]
    w1_ref, b1_ref, w2_ref, b2_ref, cos_ref, sin_ref, ck_ref, cv_ref, lhs, pbuf = refs[ppt + 1:]
    r = ppt * SUBLANES
    first = _first_half_mask(r)
    for c in range(2):
        for s in range(CMP_STRIDE):
            src = slice(s * 2 * KV_W + c * KV_W, s * 2 * KV_W + (c + 1) * KV_W)
            dst = slice(s * KV_W, (s + 1) * KV_W)
            for j in range(ppt):
                lhs[j * SUBLANES:(j + 1) * SUBLANES, dst] = pages[j][0, :, src]
            lhs[r:r + SUBLANES, dst] = nxt[0, :, src]
        p = jnp.dot(lhs[...].astype(BF16), w1_ref[c], preferred_element_type=F32)
        pbuf[...] = p[:, 2 * CMP_HID:4 * CMP_HID]
        hid = p[0:r, 0:2 * CMP_HID] + pbuf[pl.ds(1, r), :] + b1_ref[c]
        act = jax.nn.gelu(hid)
        comp = jnp.dot(act.astype(BF16), w2_ref[c], preferred_element_type=F32) + b2_ref[c]
        if c == 0:
            comp = _rope128(comp, cos_ref[...], sin_ref[...], first)
            out = ck_ref
        else:
            out = cv_ref
        for k in range(N_KV):
            out[0, k] = comp[:, k * HD:(k + 1) * HD]


def _cmp(pages3, pt_flat, nb, n_pages, w1p, b1p, w2p, b2p, cos_c, sin_c, name):
    ppt = min(32, n_pages)
    n_tiles = n_pages // ppt
    r = ppt * SUBLANES
    n_chunk = n_pages * SUBLANES

    def page_map(j):
        return lambda b, t, pt: (pt[b * n_pages + t * ppt + j], 0, 0)

    def next_map(b, t, pt):
        return (pt[b * n_pages + jnp.minimum(t * ppt + ppt, n_pages - 1)], 0, 0)

    in_specs = [pl.BlockSpec((1, SUBLANES, CHUNK_ROW), page_map(j)) for j in range(ppt)]
    in_specs.append(pl.BlockSpec((1, SUBLANES, CHUNK_ROW), next_map))
    const = lambda shp: pl.BlockSpec(shp, lambda b, t, pt: (0,) * len(shp))
    in_specs += [const(w1p.shape), const(b1p.shape), const(w2p.shape), const(b2p.shape),
                 pl.BlockSpec((r, LANES), lambda b, t, pt: (t, 0)), pl.BlockSpec((r, LANES), lambda b, t, pt: (t, 0))]
    hm = pl.BlockSpec((1, N_KV, r, HD), lambda b, t, pt: (b, 0, t, 0))
    grid_spec = pltpu.PrefetchScalarGridSpec(
        num_scalar_prefetch=1, grid=(nb, n_tiles), in_specs=in_specs, out_specs=[hm, hm],
        scratch_shapes=[pltpu.VMEM((r + SUBLANES, CMP_STRIDE * KV_W), F32), pltpu.VMEM((r + SUBLANES, 2 * CMP_HID), F32)])
    return pl.pallas_call(
        functools.partial(_cmp_kernel, ppt=ppt),
        grid_spec=grid_spec,
        out_shape=[jax.ShapeDtypeStruct((nb, N_KV, n_chunk, HD), F32)] * 2,
        compiler_params=_params(2), name=name,
    )(pt_flat, *([pages3] * (ppt + 1)), w1p, b1p, w2p, b2p, cos_c, sin_c)


def _softmax_rows(s, valid):
    s = jnp.where(valid, s, NEG_INF)
    m = jnp.max(s, axis=-1, keepdims=True)
    e = jnp.exp(s - m)
    return e / jnp.sum(e, axis=-1, keepdims=True)


def _attn_p_kernel(q_ref, ck_ref, cv_ref, ks_ref, vs_ref, kw_ref, vw_ref, gate_ref, band_ref, exp_ref, o_ref,
                   *, n_cmp_pad, n_blk, kc, t):
    qb = pl.program_id(2)
    start = qb * Q_BLOCK
    rows = QPK * Q_BLOCK
    q = q_ref[0].reshape(rows, HD)
    tpos = start + lax.broadcasted_iota(I32, (Q_BLOCK, 1), 0)
    qpos = start + lax.broadcasted_iota(I32, (rows, 1), 0) % Q_BLOCK

    def biased(s, bias):
        width = s.shape[-1]
        return (s.reshape(QPK, Q_BLOCK, width) + bias[None]).reshape(rows, width)

    s_c = lax.dot_general(q, ck_ref[0, 0].astype(BF16), _NT, preferred_element_type=F32)
    cmp_end = (lax.broadcasted_iota(I32, (1, n_cmp_pad), 1) + 2) * CMP_STRIDE - 1
    s_c = biased(s_c, jnp.where(cmp_end <= tpos, 0.0, NEG_INF))
    m_c = jnp.maximum(jnp.max(s_c, axis=-1, keepdims=True), 0.5 * NEG_INF)
    e_c = jnp.exp(s_c - m_c)
    l_c = jnp.sum(e_c, axis=-1, keepdims=True)
    p_c = e_c * (1.0 / jnp.where(l_c > 0.0, l_c, 1.0))
    o_c = jnp.dot(p_c.astype(BF16), cv_ref[0, 0].astype(BF16), preferred_element_type=F32)

    pcs = p_c[0:Q_BLOCK] + p_c[Q_BLOCK:2 * Q_BLOCK] + p_c[2 * Q_BLOCK:3 * Q_BLOCK] + p_c[3 * Q_BLOCK:4 * Q_BLOCK]
    imp = lax.dot_general(band_ref[...], pcs, _NT, preferred_element_type=F32,
                          precision=lax.Precision.HIGHEST)
    blk = lax.broadcasted_iota(I32, (n_blk, Q_BLOCK), 0)
    tlane = start + lax.broadcasted_iota(I32, (1, Q_BLOCK), 1)
    cur = tlane // SEL_BLOCK
    causal = blk * SEL_BLOCK <= tlane
    forced = causal & ((blk == 0) | (blk == cur) | (blk == cur - 1))
    score = jnp.where(forced, FORCE_SCORE, jnp.where(causal, imp, -1.0))
    rank = jnp.zeros((n_blk, Q_BLOCK), F32)
    for bp in range(n_blk):
        other = score[bp:bp + 1, :]
        beats = (other > score) | ((other == score) & (bp < blk))
        rank = rank + beats.astype(F32)
    sel_t = (rank < float(min(N_SEL, n_blk))).astype(BF16)
    eye = (lax.broadcasted_iota(I32, (Q_BLOCK, Q_BLOCK), 0)
           == lax.broadcasted_iota(I32, (Q_BLOCK, Q_BLOCK), 1)).astype(BF16)
    sel = lax.dot_general(eye, sel_t, _NT, preferred_element_type=F32).astype(BF16)

    n_chunks = (start + Q_BLOCK + kc - 1) // kc

    def step(j, carry, causal_chunk):
        m_i, l_i, acc = carry
        off = pl.multiple_of(j * kc, kc)
        kj = ks_ref[0, 0, pl.ds(off, kc), :]
        vj = vs_ref[0, 0, pl.ds(off, kc), :]
        s = lax.dot_general(q, kj, _NT, preferred_element_type=F32)
        mexp = jnp.dot(sel, exp_ref[j], preferred_element_type=F32)
        bias = mexp * (-NEG_INF) + NEG_INF
        if causal_chunk:
            keypos = off + lax.broadcasted_iota(I32, (1, kc), 1)
            bias = jnp.where(keypos <= tpos, bias, NEG_INF)
        s = biased(s, bias)
        m_new = jnp.maximum(m_i, jnp.max(s, axis=-1, keepdims=True))
        alpha = jnp.exp(m_i - m_new)
        p = jnp.exp(s - m_new)
        l_new = alpha * l_i + jnp.sum(p, axis=-1, keepdims=True)
        acc_new = alpha * acc + jnp.dot(p.astype(BF16), vj, preferred_element_type=F32)
        return m_new, l_new, acc_new

    m0 = jnp.full((rows, 1), NEG_INF, F32)
    l0 = jnp.zeros((rows, 1), F32)
    a0 = jnp.zeros((rows, HD), F32)
    carry = lax.fori_loop(0, n_chunks - 1, lambda j, c: step(j, c, False), (m0, l0, a0))
    _, l_s, acc_s = step(n_chunks - 1, carry, True)
    o_s = acc_s * (1.0 / l_s)

    wlen = WINDOW + Q_BLOCK
    s0 = pl.multiple_of(jnp.maximum(start - WINDOW, 0), Q_BLOCK)
    kwin = kw_ref[0, 0, pl.ds(s0, wlen), :]
    vwin = vw_ref[0, 0, pl.ds(s0, wlen), :]
    s_w = lax.dot_general(q, kwin, _NT, preferred_element_type=F32)
    dist = tpos - (s0 + lax.broadcasted_iota(I32, (1, wlen), 1))
    s_w = biased(s_w, jnp.where((dist >= 0) & (dist <= WINDOW), 0.0, NEG_INF))
    e_w = jnp.exp(s_w - jnp.max(s_w, axis=-1, keepdims=True))
    o_w = jnp.dot(e_w.astype(BF16), vwin, preferred_element_type=F32) * (1.0 / jnp.sum(e_w, axis=-1, keepdims=True))

    g = gate_ref[...]
    for hq in range(QPK):
        rs = slice(hq * Q_BLOCK, (hq + 1) * Q_BLOCK)
        o = (g[:, hq:hq + 1] * o_c[rs] + g[:, QPK + hq:QPK + hq + 1] * o_s[rs]
             + g[:, 2 * QPK + hq:2 * QPK + hq + 1] * o_w[rs])
        o_ref[0, :, hq * HD:(hq + 1) * HD] = o


def _attn_prompt(q_hm, ck, cv, ks, vs, kw, vw, gates, band, expand, *, nb, t):
    n_qb = t // Q_BLOCK
    n_cmp_pad = ck.shape[2]
    n_blk = band.shape[0]
    kc = expand.shape[2]
    kv_spec = lambda n: pl.BlockSpec((1, 1, n, HD), lambda b, k, i: (b, k, 0, 0))
    return pl.pallas_call(
        functools.partial(_attn_p_kernel, n_cmp_pad=n_cmp_pad, n_blk=n_blk, kc=kc, t=t),
        grid=(nb, N_KV, n_qb),
        in_specs=[pl.BlockSpec((1, QPK, Q_BLOCK, HD), lambda b, k, i: (b, k, i, 0)),
                  kv_spec(n_cmp_pad), kv_spec(n_cmp_pad), kv_spec(t), kv_spec(t), kv_spec(t), kv_spec(t),
                  pl.BlockSpec((Q_BLOCK, LANES), lambda b, k, i: (b * n_qb + i, k)),
                  pl.BlockSpec(band.shape, lambda b, k, i: (0, 0)),
                  pl.BlockSpec(expand.shape, lambda b, k, i: (0, 0, 0))],
        out_specs=pl.BlockSpec((1, Q_BLOCK, QPK * HD), lambda b, k, i: (b, i, k)),
        out_shape=jax.ShapeDtypeStruct((nb, t, ATTN_W), F32),
        compiler_params=_params(3), name="attn_prompt",
    )(q_hm, ck, cv, ks, vs, kw, vw, gates, band, expand)


def _attn_s1_kernel(q_ref, ck_ref, cv_ref, band_ref, oc_ref, imp_ref, *, n_chunk, past):
    q = q_ref[0]
    q16 = jnp.concatenate([q, jnp.zeros_like(q)], axis=0).astype(BF16)
    cmp_end = (lax.broadcasted_iota(I32, (1, n_chunk), 1) + 2) * CMP_STRIDE - 1
    valid = cmp_end <= past
    head = lax.broadcasted_iota(I32, (2 * N_HEADS, 1), 0)
    oc = jnp.zeros((2 * N_HEADS, HD), F32)
    imps = []
    for k in range(N_KV):
        s = lax.dot_general(q16, ck_ref[0, k].astype(BF16), _NT, preferred_element_type=F32)
        p = _softmax_rows(s, valid) * valid.astype(F32)
        in_grp = (head >= k * QPK) & (head < (k + 1) * QPK)
        p = jnp.where(in_grp, p, 0.0)
        oc = oc + jnp.dot(p.astype(BF16), cv_ref[0, k].astype(BF16), preferred_element_type=F32)
        pcs = jnp.sum(p, axis=0, keepdims=True)
        pcs8 = jnp.broadcast_to(pcs, (SUBLANES, n_chunk))
        imps.append(jnp.dot(pcs8, band_ref[...], preferred_element_type=F32,
                            precision=lax.Precision.HIGHEST)[0:1])
    oc_ref[0] = oc[0:N_HEADS]
    imp_ref[0] = jnp.concatenate(imps + [jnp.zeros((SUBLANES - N_KV, imps[0].shape[1]), F32)], axis=0)


def _topk_s_kernel(imp_ref, idx_ref, *, n_sel_blocks, past):
    imp = imp_ref[...]
    rows, nbp = imp.shape
    blk = lax.broadcasted_iota(I32, (rows, nbp), 1)
    cur = past // SEL_BLOCK
    causal = blk * SEL_BLOCK <= past
    forced = causal & ((blk == 0) | (blk == cur) | (blk == cur - 1))
    score = jnp.where(forced, FORCE_SCORE, jnp.where(causal, imp, -1.0))
    score = jnp.where(blk < n_sel_blocks, score, -2.0)
    lane = lax.broadcasted_iota(I32, (rows, LANES), 1)
    out = jnp.zeros((rows, LANES), I32)
    for r in range(min(N_SEL, n_sel_blocks)):
        m = jnp.max(score, axis=-1, keepdims=True)
        pick = jnp.min(jnp.where(score == m, blk, nbp), axis=-1, keepdims=True)
        out = jnp.where(lane == r, pick, out)
        score = jnp.where(blk == pick, -3.0, score)
    idx_ref[...] = out


def _attn_s2_kernel(pt_ref, idx_ref, *refs, n_pages, past, n_sel_blocks):
    ktiles, vtiles = refs[:N_SEL], refs[N_SEL:2 * N_SEL]
    q_ref, oc_ref, kvs_ref, wk_ref, wv_ref, kvw_ref, gate_ref, o_ref, kbuf, vbuf = refs[2 * N_SEL:]
    b = pl.program_id(0)
    k = pl.program_id(1)
    q = q_ref[0]
    q16f = jnp.concatenate([q, jnp.zeros_like(q)], axis=0)
    q16 = q16f.astype(BF16)
    head = lax.broadcasted_iota(I32, (N_HEADS, 1), 0)
    nk = N_SEL * PAGE
    lane = lax.broadcasted_iota(I32, (1, nk), 1)
    slot = lane // PAGE
    new_blk = n_sel_blocks - 1
    wb = wk_ref.shape[-1]
    wpos = past - wb + lax.broadcasted_iota(I32, (1, wb), 1)
    wdist = past - wpos
    valid_w = (wdist >= 0) & (wdist <= WINDOW) & (wpos >= 0)

    def attend(s, valid, v_t, k_new, v_new):
        s_new = jnp.sum(q16f * k_new, axis=-1, keepdims=True)
        s = jnp.where(valid, s, NEG_INF)
        m = jnp.maximum(jnp.max(s, axis=-1, keepdims=True), s_new)
        e = jnp.exp(s - m)
        e_new = jnp.exp(s_new - m)
        den = jnp.sum(e, axis=-1, keepdims=True) + e_new
        acc = lax.dot_general(e.astype(BF16), v_t, _NT, preferred_element_type=F32) + e_new * v_new
        return acc / den

    in_grp = (head >= k * QPK) & (head < (k + 1) * QPK)
    bvec = jnp.zeros((1, nk), I32)
    for j in range(N_SEL):
        kbuf[:, j * PAGE:(j + 1) * PAGE] = ktiles[j][...].astype(BF16)
        vbuf[:, j * PAGE:(j + 1) * PAGE] = vtiles[j][...].astype(BF16)
        bvec = jnp.where(slot == j, idx_ref[(b * N_KV + k) * LANES + j], bvec)
    tok = (bvec // 2) * PAGE + lane % PAGE
    valid = (tok // SEL_BLOCK == bvec) & (bvec < new_blk) & (tok <= past)
    s = jnp.dot(q16, kbuf[...], preferred_element_type=F32)
    o_s = attend(s, valid, vbuf[...], kvs_ref[0, pl.ds(k, 1), :], kvs_ref[0, pl.ds(N_KV + k, 1), :])
    sw = jnp.dot(q16, wk_ref[...].astype(BF16), preferred_element_type=F32)
    o_w = attend(sw, valid_w, wv_ref[...].astype(BF16), kvw_ref[0, pl.ds(k, 1), :], kvw_ref[0, pl.ds(N_KV + k, 1), :])
    g = gate_ref[0]
    part = jnp.where(in_grp, g[:, 1:2] * o_s[0:N_HEADS] + g[:, 2:3] * o_w[0:N_HEADS], 0.0)

    @pl.when(k == 0)
    def _():
        o_ref[0] = g[:, 0:1] * oc_ref[0] + part

    @pl.when(k > 0)
    def _():
        o_ref[0] = o_ref[0] + part


def _attn_sample(q3, ck, cv, band_s, sel_t, pt_flat, kvs_rows, win_t, kvw_rows, gates_hm,
                 *, nb, n_pages, past, n_sel_blocks):
    n_chunk = ck.shape[2]
    nbp = band_s.shape[1]
    oc, imp = pl.pallas_call(
        functools.partial(_attn_s1_kernel, n_chunk=n_chunk, past=past),
        grid=(nb,),
        in_specs=[pl.BlockSpec((1, N_HEADS, HD), lambda b: (b, 0, 0)),
                  pl.BlockSpec((1, N_KV, n_chunk, HD), lambda b: (b, 0, 0, 0)),
                  pl.BlockSpec((1, N_KV, n_chunk, HD), lambda b: (b, 0, 0, 0)),
                  pl.BlockSpec(band_s.shape, lambda b: (0, 0))],
        out_specs=[pl.BlockSpec((1, N_HEADS, HD), lambda b: (b, 0, 0)),
                   pl.BlockSpec((1, SUBLANES, nbp), lambda b: (b, 0, 0))],
        out_shape=[jax.ShapeDtypeStruct((nb, N_HEADS, HD), F32), jax.ShapeDtypeStruct((nb, SUBLANES, nbp), F32)],
        compiler_params=_params(1), name="attn_sample_cmp",
    )(q3, ck, cv, band_s)
    imp2 = imp[:, 0:N_KV, :].reshape(nb * N_KV, nbp)
    idx = pl.pallas_call(
        functools.partial(_topk_s_kernel, n_sel_blocks=n_sel_blocks, past=past),
        out_shape=jax.ShapeDtypeStruct((nb * N_KV, LANES), I32),
        compiler_params=pltpu.CompilerParams(vmem_limit_bytes=VMEM_LIMIT), name="topk_sample",
    )(imp2)
    idx_flat = idx.reshape(-1)

    def tile_map(c, j):
        def f(b, k, pt, ix):
            bidx = ix[(b * N_KV + k) * LANES + j]
            return (pt[b * n_pages + jnp.minimum(bidx // 2, n_pages - 1)], c, k, 0, 0)
        return f

    tile = lambda c, j: pl.BlockSpec((None, None, None, HD, PAGE), tile_map(c, j))
    in_specs = [tile(0, j) for j in range(N_SEL)] + [tile(1, j) for j in range(N_SEL)]
    wb = win_t.shape[-1]
    per_b = lambda shp: pl.BlockSpec(shp, lambda b, k, pt, ix: (b, 0, 0))
    in_specs += [per_b((1, N_HEADS, HD)), per_b((1, N_HEADS, HD)), per_b((1, SUBLANES, HD)),
                 pl.BlockSpec((None, None, None, HD, wb), lambda b, k, pt, ix: (b, 0, k, 0, 0)),
                 pl.BlockSpec((None, None, None, HD, wb), lambda b, k, pt, ix: (b, 1, k, 0, 0)),
                 per_b((1, SUBLANES, HD)), per_b((1, N_HEADS, LANES))]
    grid_spec = pltpu.PrefetchScalarGridSpec(
        num_scalar_prefetch=2, grid=(nb, N_KV), in_specs=in_specs,
        out_specs=per_b((1, N_HEADS, HD)),
        scratch_shapes=[pltpu.VMEM((HD, N_SEL * PAGE), BF16), pltpu.VMEM((HD, N_SEL * PAGE), BF16)])
    return pl.pallas_call(
        functools.partial(_attn_s2_kernel, n_pages=n_pages, past=past, n_sel_blocks=n_sel_blocks),
        grid_spec=grid_spec,
        out_shape=jax.ShapeDtypeStruct((nb, N_HEADS, HD), F32),
        compiler_params=_params(2), name="attn_sample_sel",
    )(pt_flat, idx_flat, *([sel_t] * (2 * N_SEL)), q3, oc, kvs_rows, win_t, win_t, kvw_rows, gates_hm)


def _outp_kernel(xp_ref, convp_ref, attnp_ref, ga1p_ref, sc2p_ref, sh2p_ref,
                 xs_ref, convs_ref, attns_ref, ga1s_ref, sc2s_ref, sh2s_ref,
                 gc_ref, ga_ref, w_ref, g2_ref, wr_ref, x1_ref, hp_ref, lg_ref, *, n_prompt_tiles):
    is_p = pl.program_id(0) < n_prompt_tiles
    pick = lambda a, b: jnp.where(is_p, a, b)
    cn = _rms(pick(convp_ref[...], convs_ref[...]), gc_ref[...])
    an = _rms(pick(attnp_ref[...], attns_ref[...]), ga_ref[...])
    cat = jnp.concatenate([cn, an], axis=1).astype(BF16)
    y = jnp.dot(cat, w_ref[...], preferred_element_type=F32)
    x1 = pick(xp_ref[...], xs_ref[...]) + pick(ga1p_ref[0], ga1s_ref[0]) * y
    x1_ref[...] = x1
    hp = _rms(x1, g2_ref[...]) * (1.0 + pick(sc2p_ref[0], sc2s_ref[0])) + pick(sh2p_ref[0], sh2s_ref[0])
    hp_ref[...] = hp
    lg_ref[...] = jnp.dot(hp, wr_ref[...], preferred_element_type=F32, precision=lax.Precision.HIGHEST)


TOKEN_TILE = 512


def _outp(prompt, sample, g_conv, g_attn, w_out_b, g2, w_route, *, tpb):
    tm = TOKEN_TILE
    n_p = prompt[0].shape[0] // tm
    total = (n_p + 1) * tm
    last = n_p - 1
    prow = lambda w: pl.BlockSpec((tm, w), lambda i: (jnp.minimum(i, last), 0))
    srow = lambda w: pl.BlockSpec((tm, w), lambda i: (0, 0))
    pmod = pl.BlockSpec((1, 1, D_MODEL), lambda i: (jnp.minimum(i, last) // tpb, 0, 0))
    smod = pl.BlockSpec((1, tm, D_MODEL), lambda i: (0, 0, 0))
    vec = lambda w: pl.BlockSpec((1, w), lambda i: (0, 0))
    row = lambda w: pl.BlockSpec((tm, w), lambda i: (i, 0))
    in_specs = [prow(D_MODEL), prow(CONV_W), prow(ATTN_W), pmod, pmod, pmod,
                srow(D_MODEL), srow(CONV_W), srow(ATTN_W), smod, smod, smod,
                vec(CONV_W), vec(ATTN_W), pl.BlockSpec((D_MODEL, D_MODEL), lambda i: (0, 0)), vec(D_MODEL),
                pl.BlockSpec((D_MODEL, LANES), lambda i: (0, 0))]
    return pl.pallas_call(
        functools.partial(_outp_kernel, n_prompt_tiles=n_p),
        grid=(n_p + 1,), in_specs=in_specs,
        out_specs=[row(D_MODEL), row(D_MODEL), row(LANES)],
        out_shape=[jax.ShapeDtypeStruct((total, D_MODEL), F32), jax.ShapeDtypeStruct((total, D_MODEL), F32),
                   jax.ShapeDtypeStruct((total, LANES), F32)],
        compiler_params=_params(1), name="outp",
    )(*prompt, *sample, g_conv.reshape(1, -1), g_attn.reshape(1, -1), w_out_b, g2.reshape(1, -1), w_route)


def _route_kernel(lg_ref, bias_ref, tri_ref, o_ref, cnt_ref, carry, *, tm, n_valid):
    i = pl.program_id(0)

    @pl.when(i == 0)
    def _():
        carry[...] = jnp.zeros_like(carry)

    lane = lax.broadcasted_iota(I32, (tm, LANES), 1)
    rowid = i * tm + lax.broadcasted_iota(I32, (tm, 1), 0)
    live = rowid < n_valid
    lg = lg_ref[...] + bias_ref[...]
    is_g = lane < N_GROUPS
    lgg = jnp.where(is_g, lg, NEG_INF)
    gmax = jnp.max(lgg, axis=-1, keepdims=True)
    grp = jnp.min(jnp.where(is_g & (lgg == gmax), lane, LANES), axis=-1, keepdims=True)
    p_grp = 1.0 / jnp.sum(jnp.where(is_g, jnp.exp(lgg - gmax), 0.0), axis=-1, keepdims=True)
    eid = lane - N_GROUPS
    in_grp = (eid >= grp * EPG) & (eid < (grp + 1) * EPG)
    le = jnp.where(in_grp, lg, NEG_INF)
    v1 = jnp.max(le, axis=-1, keepdims=True)
    e1 = jnp.min(jnp.where(in_grp & (le == v1), eid, LANES), axis=-1, keepdims=True)
    le2 = jnp.where(eid == e1, NEG_INF, le)
    v2 = jnp.max(le2, axis=-1, keepdims=True)
    e2 = jnp.min(jnp.where(in_grp & (eid != e1) & (le2 == v2), eid, LANES), axis=-1, keepdims=True)
    ex2 = jnp.exp(v2 - v1)
    w1 = p_grp * (1.0 / (1.0 + ex2))
    w2 = p_grp * (ex2 / (1.0 + ex2))
    oh1 = ((lane == e1) & live).astype(F32)
    oh2 = ((lane == e2) & live).astype(F32)
    both = oh1 + oh2
    before = jnp.dot(tri_ref[...], both.astype(BF16), preferred_element_type=F32) + carry[0:1, :]
    r1 = jnp.sum(oh1 * before, axis=-1, keepdims=True)
    r2 = jnp.sum(oh2 * before, axis=-1, keepdims=True)
    carry[0:1, :] = carry[0:1, :] + jnp.sum(both, axis=0, keepdims=True)
    out = jnp.where(lane == 0, e1.astype(F32), 0.0)
    out = jnp.where(lane == 1, e2.astype(F32), out)
    out = jnp.where(lane == 2, w1, out)
    out = jnp.where(lane == 3, w2, out)
    out = jnp.where(lane == 4, r1, out)
    out = jnp.where(lane == 5, r2, out)
    o_ref[...] = out
    cnt_ref[...] = carry[...]


def _route(logits, bias_row, n_valid):
    total = logits.shape[0]
    tm = TOKEN_TILE
    n_tiles = total // tm
    tri =(np.arange(tm)[:, None] > np.arange(tm)[None, :]).astype(np.float32)
    return pl.pallas_call(
        functools.partial(_route_kernel, tm=tm, n_valid=n_valid),
        grid=(n_tiles,),
        in_specs=[pl.BlockSpec((tm, LANES), lambda i: (i, 0)), pl.BlockSpec((1, LANES), lambda i: (0, 0)),
                  pl.BlockSpec((tm, tm), lambda i: (0, 0))],
        out_specs=[pl.BlockSpec((tm, LANES), lambda i: (i, 0)), pl.BlockSpec((SUBLANES, LANES), lambda i: (0, 0))],
        out_shape=[jax.ShapeDtypeStruct((total, LANES), F32), jax.ShapeDtypeStruct((SUBLANES, LANES), F32)],
        scratch_shapes=[pltpu.VMEM((SUBLANES, LANES), F32)],
        compiler_params=_params(1), name="route",
    )(logits, bias_row, jnp.asarray(tri, BF16))


def _row_copy(src_hbm, row, dst, slot, r, sem):
    return pltpu.make_async_copy(src_hbm.at[pl.ds(row, 1), :], dst.at[slot, pl.ds(r, 1), :], sem.at[slot])


def _experts_kernel(blk_e_ref, tok_ref, x_hbm, wg_ref, wu_ref, wd_ref, o_ref, xbuf, sem, wg_b, wu_b, wd_b, *, n_blocks):
    i = pl.program_id(0)
    slot = i % 2

    def issue(blk, s):
        for r in range(MOE_BLOCK):
            _row_copy(x_hbm, tok_ref[blk * MOE_BLOCK + r], xbuf, s, r, sem).start()

    @pl.when(i == 0)
    def _():
        issue(0, 0)

    @pl.when(i + 1 < n_blocks)
    def _():
        issue(i + 1, 1 - slot)

    changed = jnp.logical_or(i == 0, blk_e_ref[i] != blk_e_ref[jnp.maximum(i - 1, 0)])

    @pl.when(changed)
    def _():
        wg_b[...] = wg_ref[0].astype(BF16)
        wu_b[...] = wu_ref[0].astype(BF16)
        wd_b[...] = wd_ref[0].astype(BF16)

    for r in range(MOE_BLOCK):
        _row_copy(x_hbm, 0, xbuf, slot, r, sem).wait()
    x = xbuf[slot].astype(BF16)
    g = jnp.dot(x, wg_b[...], preferred_element_type=F32)
    u = jnp.dot(x, wu_b[...], preferred_element_type=F32)
    h = (g * jax.nn.sigmoid(g)) * u
    o_ref[...] = jnp.dot(h.astype(BF16), wd_b[...], preferred_element_type=F32)


def _experts(blk_e, slot_tok, hp_all, w_gate, w_up, w_down, n_blocks):
    grid_spec = pltpu.PrefetchScalarGridSpec(
        num_scalar_prefetch=2, grid=(n_blocks,),
        in_specs=[pl.BlockSpec(memory_space=pl.ANY),
                  pl.BlockSpec((1, D_MODEL, D_EXPERT), lambda i, be, st: (be[i], 0, 0)),
                  pl.BlockSpec((1, D_MODEL, D_EXPERT), lambda i, be, st: (be[i], 0, 0)),
                  pl.BlockSpec((1, D_EXPERT, D_MODEL), lambda i, be, st: (be[i], 0, 0))],
        out_specs=pl.BlockSpec((MOE_BLOCK, D_MODEL), lambda i, be, st: (i, 0)),
        scratch_shapes=[pltpu.VMEM((2, MOE_BLOCK, D_MODEL), F32), pltpu.SemaphoreType.DMA((2,)),
                        pltpu.VMEM((D_MODEL, D_EXPERT), BF16), pltpu.VMEM((D_MODEL, D_EXPERT), BF16),
                        pltpu.VMEM((D_EXPERT, D_MODEL), BF16)])
    return pl.pallas_call(
        functools.partial(_experts_kernel, n_blocks=n_blocks),
        grid_spec=grid_spec,
        out_shape=jax.ShapeDtypeStruct((n_blocks * MOE_BLOCK, D_MODEL), F32),
        compiler_params=_params(1), name="experts",
    )(blk_e, slot_tok, hp_all, w_gate, w_up, w_down)


def _final_kernel(dest_ref, yb_hbm, x1_ref, wt_ref, gate2_ref, gf_ref, o_ref, ybuf, sem, *, tm, n_tiles, row0):
    i = pl.program_id(0)
    slot = i % 2

    def copy(tile, s, r, k):
        d = dest_ref[(row0 + tile * tm + r) * 2 + k]
        return pltpu.make_async_copy(yb_hbm.at[pl.ds(d, 1), :], ybuf.at[s, k, pl.ds(r, 1), :], sem.at[s])

    def issue(tile, s):
        for r in range(tm):
            for k in range(2):
                copy(tile, s, r, k).start()

    @pl.when(i == 0)
    def _():
        issue(0, 0)

    @pl.when(i + 1 < n_tiles)
    def _():
        issue(i + 1, 1 - slot)

    for r in range(tm):
        for k in range(2):
            pltpu.make_async_copy(yb_hbm.at[pl.ds(0, 1), :], ybuf.at[slot, k, pl.ds(r, 1), :], sem.at[slot]).wait()
    wt = wt_ref[...]
    f = wt[:, 2:3] * ybuf[slot, 0] + wt[:, 3:4] * ybuf[slot, 1]
    x2 = x1_ref[...] + gate2_ref[0] * f
    o_ref[...] = _rms(x2, gf_ref[...])


def _final(dest_flat, yb, x1_all, route_rows, gate2, final_g, *, rows, tpb, per_row, row0):
    tm = min(128, rows)
    n_tiles = rows // tm
    blk0 = row0 // tm
    mod = (pl.BlockSpec((1, tm, D_MODEL), lambda i, d: (0, i, 0)) if per_row
           else pl.BlockSpec((1, 1, D_MODEL), lambda i, d: (i // tpb, 0, 0)))
    grid_spec = pltpu.PrefetchScalarGridSpec(
        num_scalar_prefetch=1, grid=(n_tiles,),
        in_specs=[pl.BlockSpec(memory_space=pl.ANY),
                  pl.BlockSpec((tm, D_MODEL), lambda i, d: (blk0 + i, 0)),
                  pl.BlockSpec((tm, LANES), lambda i, d: (blk0 + i, 0)),
                  mod, pl.BlockSpec((1, D_MODEL), lambda i, d: (0, 0))],
        out_specs=pl.BlockSpec((tm, D_MODEL), lambda i, d: (i, 0)),
        scratch_shapes=[pltpu.VMEM((2, 2, tm, D_MODEL), F32), pltpu.SemaphoreType.DMA((2,))])
    return pl.pallas_call(
        functools.partial(_final_kernel, tm=tm, n_tiles=n_tiles, row0=row0),
        grid_spec=grid_spec,
        out_shape=jax.ShapeDtypeStruct((rows, D_MODEL), F32),
        compiler_params=_params(1), name="final_sample" if per_row else "final_prompt",
    )(dest_flat, yb, x1_all, route_rows, gate2, final_g.reshape(1, -1))


def _rope_tables(pos):
    inv = ROPE_THETA ** (-jnp.arange(HALF, dtype=F32) / HALF)
    ang = pos.astype(F32)[:, None] * inv[None, :]
    cos = jnp.tile(jnp.cos(ang), (1, LANES // HALF))
    sin = jnp.sin(ang)
    sin_s = jnp.tile(jnp.concatenate([-sin, sin], axis=1), (1, LANES // HD))
    return cos, sin_s


def _pack_w_in(w_in):
    gl = w_in[:, _C_G:_C_G + 3 * N_HEADS].reshape(D_MODEL, 3, N_KV, QPK)
    gcols = []
    for k in range(N_KV):
        gk = gl[:, :, k, :].reshape(D_MODEL, 3 * QPK)
        gcols.append(jnp.pad(gk, ((0, 0), (0, LANES - 3 * QPK))))
    return jnp.concatenate([w_in[:, :_C_G]] + gcols, axis=1).astype(BF16)


def _pack_cmp_weights(cmp_w1, cmp_w2, bias, cmp_b2):
    w1 = cmp_w1.reshape(2, 2, CMP_STRIDE, HD, CMP_HID)
    eye = jnp.eye(N_KV, dtype=F32)
    w1p = jnp.einsum('crsdh,pk->cspdrkh', w1, eye).reshape(2, CMP_STRIDE * KV_W, 2 * N_KV * CMP_HID)
    w2p = jnp.einsum('chd,pk->cphkd', cmp_w2, eye).reshape(2, N_KV * CMP_HID, KV_W)
    b1p = jnp.tile(bias, (1, N_KV)).reshape(2, 1, N_KV * CMP_HID)
    b2p = jnp.tile(cmp_b2, (1, N_KV)).reshape(2, 1, KV_W)
    return w1p.astype(BF16), b1p, w2p.astype(BF16), b2p


def _band(n_cmp_pad, n_cmp, n_blk_pad, n_blk):
    n = np.arange(n_cmp_pad)[:, None]
    b = np.arange(n_blk_pad)[None, :]
    r = SEL_BLOCK // CMP_STRIDE
    m = (n >= r * b - 1) & (n <= r * b + r - 1) & (n < n_cmp) & (b < n_blk)
    return jnp.asarray(m.astype(np.float32))


def _expand(t, kc):
    n_chunks = t // kc
    key = np.arange(t).reshape(n_chunks, 1, kc)
    blk = np.arange(t // SEL_BLOCK).reshape(1, -1, 1)
    return jnp.asarray((key // SEL_BLOCK == blk).astype(np.float32), BF16)


def kernel(x_prompt, x_sample, c_prompt, c_sample, cache_cmp_kv, cache_sel_kv, cache_win_kv, state_conv, page_table,
           ln1_g, ln2_g, w_ada, b_ada, w_in, w_conv, cmp_pos, cmp_w1, cmp_b1, cmp_w2, cmp_b2, g_out_conv, g_out_attn,
           w_out, w_route_group, b_route_group, w_route_expert, b_route_expert, w_gate, w_up, w_down, final_g):
    depth = w_in.shape[0]
    assert depth == 1, "single-layer step"
    nb, t, _ = x_prompt.shape
    ns, ts, _ = x_sample.shape
    assert ts == 1 and t % 512 == 0 and t >= WINDOW + Q_BLOCK
    n_pool = cache_cmp_kv.shape[1]
    n_pages = page_table.shape[1]
    past = n_pages * PAGE
    wb = cache_win_kv.shape[2]
    assert wb == WINDOW
    l = 0

    n_c = nb + ns
    c_all = jnp.pad(jnp.concatenate([c_prompt, c_sample], axis=0), ((0, (-n_c) % SUBLANES), (0, 0)))
    mods = _ada(c_all, w_ada[l], b_ada[l])
    sh1, sc1, ga1, sh2, sc2, ga2 = [mods[:, j * D_MODEL:(j + 1) * D_MODEL] for j in range(6)]
    pr = lambda a: a[0:nb].reshape(nb, 1, D_MODEL)
    sr = lambda a: a[nb:nb + ns].reshape(1, ns, D_MODEL)

    w_pack = _pack_w_in(w_in[l])
    wconv8 = jnp.pad(w_conv[l], ((0, SUBLANES - CONV_K), (0, 0)))
    cos_p, sin_p = _rope_tables(jnp.arange(t, dtype=I32))
    cos_s, sin_s = _rope_tables(jnp.full((1,), past, I32))
    xp2 = x_prompt.reshape(nb * t, D_MODEL)
    xs2 = x_sample.reshape(ns, D_MODEL)
    (conv_p, cst_p, q_p, kvc_p, kvc_rows_p, kvs_rows_p, kvw_rows_p, ks_p, vs_p, kw_p, vw_p, gates_p) = _proj(
        xp2, ln1_g[l], pr(sc1), pr(sh1), w_pack, wconv8, cos_p, sin_p, nb=nb, t=t, sample=False)
    (conv_s, cst_s, q_s, _, kvc_rows_s, kvs_rows_s, kvw_rows_s, _, _, _, _, gates_s) = _proj(
        xs2, ln1_g[l], sr(sc1), sr(sh1), w_pack, wconv8, cos_s, sin_s, nb=ns, t=1, sample=True,
        prev=(state_conv[l][:, 0], state_conv[l][:, 1]))

    bias = _cmpbias(cmp_pos[l], cmp_w1[l], cmp_b1[l])
    w1p, b1p, w2p, b2p = _pack_cmp_weights(cmp_w1[l], cmp_w2[l], bias, cmp_b2[l])
    pp = t // PAGE
    cos_cp, sin_cp = _rope_tables((jnp.arange(t // CMP_STRIDE, dtype=I32) + 2) * CMP_STRIDE - 1)
    ck_p, cv_p = _cmp(kvc_p.reshape(nb * pp, SUBLANES, CHUNK_ROW), jnp.arange(nb * pp, dtype=I32), nb, pp,
                      w1p, b1p, w2p, b2p, cos_cp, sin_cp, "cmp_prompt")
    pt_flat = page_table.reshape(-1).astype(I32)
    cos_cs, sin_cs = _rope_tables((jnp.arange(past // CMP_STRIDE, dtype=I32) + 2) * CMP_STRIDE - 1)
    ck_s, cv_s = _cmp(cache_cmp_kv[l].reshape(n_pool, SUBLANES, CHUNK_ROW), pt_flat, ns, n_pages,
                      w1p, b1p, w2p, b2p, cos_cs, sin_cs, "cmp_sample")

    n_chunk_p = t // CMP_STRIDE
    n_blk_p = t // SEL_BLOCK
    band_p = _band(n_chunk_p, n_chunk_p - 1, n_blk_p, n_blk_p)
    attn_p = _attn_prompt(q_p, ck_p, cv_p, ks_p, vs_p, kw_p, vw_p, gates_p, band_p.T, _expand(t, 512), nb=nb, t=t)

    n_chunk_s = past // CMP_STRIDE
    n_sel_s = -(-(past + 1) // SEL_BLOCK)
    nbp = -(-n_sel_s // LANES) * LANES
    band_s = _band(n_chunk_s, (past + 1) // CMP_STRIDE - 1, nbp, n_sel_s)
    q3 = q_s.reshape(N_HEADS, ns, HD).transpose(1, 0, 2).astype(F32)
    gs = gates_s.reshape(ns, N_KV, LANES)[:, :, :3 * QPK].reshape(ns, N_KV, 3, QPK)
    gates_hm = jnp.pad(gs.transpose(0, 1, 3, 2).reshape(ns, N_HEADS, 3), ((0, 0), (0, 0), (0, LANES - 3)))
    rpt = 2 * N_KV
    new_rows = lambda a: jnp.pad(a.reshape(ns, rpt, HD), ((0, 0), (0, SUBLANES - rpt), (0, 0)))
    to_tiles = lambda a: a.transpose(0, 2, 3, 4, 1)
    attn_s = _attn_sample(q3, ck_s, cv_s, band_s, to_tiles(cache_sel_kv[l]), pt_flat,
                          new_rows(kvs_rows_s), to_tiles(cache_win_kv[l]), new_rows(kvw_rows_s),
                          gates_hm, nb=ns, n_pages=n_pages, past=past, n_sel_blocks=n_sel_s).reshape(ns, ATTN_W)

    total = nb * t + ns
    w_out_b = w_out[l].astype(BF16)
    w_route = jnp.pad(jnp.concatenate([w_route_group[l], w_route_expert[l]], axis=1),
                      ((0, 0), (0, LANES - N_GROUPS - N_EXPERTS)))
    b_route = jnp.pad(jnp.concatenate([b_route_group[l], b_route_expert[l]]), (0, LANES - N_GROUPS - N_EXPERTS))
    tile_pad = lambda a: jnp.pad(a, ((0, TOKEN_TILE - ns), (0, 0)))
    smod = lambda a: tile_pad(a[nb:nb + ns]).reshape(1, TOKEN_TILE, D_MODEL)
    x1_all, hp_all, lg_all = _outp(
        (xp2, conv_p, attn_p.reshape(nb * t, ATTN_W), pr(ga1), pr(sc2), pr(sh2)),
        (tile_pad(xs2), tile_pad(conv_s), tile_pad(attn_s), smod(ga1), smod(sc2), smod(sh2)),
        g_out_conv[l], g_out_attn[l], w_out_b, ln2_g[l], w_route, tpb=t // TOKEN_TILE)

    route, counts = _route(lg_all, b_route.reshape(1, LANES), total)
    e = route[:total, 0:2].astype(I32)
    rank = route[:total, 4:6].astype(I32)
    cnt = counts[0, :N_EXPERTS].astype(I32)
    padded = (cnt + MOE_BLOCK - 1) // MOE_BLOCK * MOE_BLOCK
    pad_end = jnp.cumsum(padded)
    pad_start = pad_end - padded
    m_slots = total * 2
    n_blocks = -(-(m_slots + N_EXPERTS * (MOE_BLOCK - 1)) // MOE_BLOCK)
    dest = jnp.clip(pad_start[e] + rank, 0, n_blocks * MOE_BLOCK - 1)
    tok = jnp.broadcast_to(jnp.arange(total, dtype=I32)[:, None], (total, 2))
    slot_tok = jnp.zeros((n_blocks * MOE_BLOCK,), I32).at[dest.reshape(-1)].set(tok.reshape(-1))
    blk_start = jnp.arange(n_blocks, dtype=I32) * MOE_BLOCK
    blk_e = jnp.minimum(jnp.sum((pad_end[None, :] <= blk_start[:, None]).astype(I32), axis=1), N_EXPERTS - 1)

    yb = _experts(blk_e, slot_tok, hp_all, w_gate[l], w_up[l], w_down[l], n_blocks)
    dest_flat = dest.reshape(-1)
    y_p = _final(dest_flat, yb, x1_all, route, pr(ga2), final_g, rows=nb * t, tpb=t // 128, per_row=False, row0=0)
    y_s = _final(dest_flat, yb, x1_all, route, sr(ga2), final_g, rows=ns, tpb=1, per_row=True, row0=nb * t)

    kv_shape = (2, N_KV, HD)
    y_prompt = y_p.reshape(nb, t, D_MODEL)
    y_sample = y_s.reshape(ns, 1, D_MODEL)
    new_cmp_prompt = kvc_rows_p.reshape((1, nb, t) + kv_shape)
    new_cmp_sample = kvc_rows_s.reshape((1, ns, 1) + kv_shape)
    new_sel_prompt = kvs_rows_p.reshape((1, nb, t) + kv_shape)
    new_sel_sample = kvs_rows_s.reshape((1, ns, 1) + kv_shape)
    new_win_prompt = kvw_rows_p.reshape((nb, t) + kv_shape)[:, t - WINDOW:][None]
    new_win_sample = jnp.concatenate([cache_win_kv[l][:, 1:], kvw_rows_s.reshape((ns, 1) + kv_shape)], axis=1)[None]
    new_conv_prompt = cst_p[:, SUBLANES - (CONV_K - 1):][None]
    new_conv_sample = jnp.stack([state_conv[l][:, 1], cst_s], axis=1)[None]
    return (y_prompt, y_sample, new_cmp_prompt, new_cmp_sample, new_sel_prompt, new_sel_sample,
            new_win_prompt, new_win_sample, new_conv_prompt, new_conv_sample)
```

```python
import functools

import numpy as np
import jax
import jax.numpy as jnp
from jax import lax
from jax.experimental import pallas as pl
from jax.experimental.pallas import tpu as pltpu

F32 = jnp.float32
BF16 = jnp.bfloat16
I32 = jnp.int32

D_MODEL = 1024
CONV_W = 512
ATTN_W = 512
HD = 64
HALF = HD // 2
N_HEADS = 8
N_KV = 2
QPK = 4
KV_W = N_KV * HD
CONV_K = 3
PAGE = 128
CMP_STRIDE = 16
CMP_HID = 128
SEL_BLOCK = 64
N_SEL = 16
WINDOW = 512
Q_BLOCK = 128
ROPE_THETA = 10000.0
N_GROUPS = 4
EPG = 8
N_EXPERTS = 32
D_EXPERT = 512
MOE_BLOCK = 128
NORM_EPS = 1e-6
NEG_INF = -1e30
FORCE_SCORE = 1e4
LANES = 128
SUBLANES = 8
CHUNK_ROW = CMP_STRIDE * 2 * KV_W
VMEM_LIMIT = 56 * 1024 * 1024

_NT = (((1,), (1,)), ((), ()))


def _params(n_axes):
    return pltpu.CompilerParams(dimension_semantics=("arbitrary",) * n_axes,
                                vmem_limit_bytes=VMEM_LIMIT)


def _rms(x, g):
    return x * lax.rsqrt(jnp.mean(x * x, axis=-1, keepdims=True) + NORM_EPS) * g


def _rope128(x, cos, sin_signed, first_half):
    xr = jnp.where(first_half, pltpu.roll(x, LANES - HALF, 1), pltpu.roll(x, HALF, 1))
    return x * cos + xr * sin_signed


def _first_half_mask(rows):
    lane = lax.broadcasted_iota(I32, (rows, LANES), 1)
    return (lane % HD) < HALF


def _ada_kernel(c_ref, w_ref, b_ref, o_ref):
    c = c_ref[...]
    s = c * jax.nn.sigmoid(c)
    o_ref[...] = jnp.dot(s.astype(BF16), w_ref[...].astype(BF16), preferred_element_type=F32) + b_ref[...]


def _ada(c_all, w_ada, b_ada):
    m, d = c_all.shape
    n = w_ada.shape[1]
    tn = 1024
    return pl.pallas_call(
        _ada_kernel,
        grid=(n // tn,),
        in_specs=[pl.BlockSpec((m, d), lambda j: (0, 0)),
                  pl.BlockSpec((d, tn), lambda j: (0, j)),
                  pl.BlockSpec((1, tn), lambda j: (0, j))],
        out_specs=pl.BlockSpec((m, tn), lambda j: (0, j)),
        out_shape=jax.ShapeDtypeStruct((m, n), F32),
        compiler_params=_params(1),
        name="ada",
    )(c_all, w_ada, b_ada.reshape(1, n))


_C_B, _C_C, _C_U, _C_Q, _C_KVC, _C_KVS, _C_KVW, _C_G, _C_END = 0, 512, 1024, 1536, 2048, 2304, 2560, 2816, 3072


def _proj_kernel(*refs, tm, tpb, sample):
    if sample:
        (x_ref, g1_ref, sc_ref, sh_ref, w_ref, wc_ref, cos_ref, sin_ref, p0_ref, p1_ref,
         conv_ref, cst_ref, q_ref, kvc_ref, kvc_il_ref, kvs_ref, kvw_ref, ks_ref, vs_ref, kw_ref, vw_ref, gate_ref,
         ilbuf) = refs
        vbuf = None
    else:
        (x_ref, g1_ref, sc_ref, sh_ref, w_ref, wc_ref, cos_ref, sin_ref,
         conv_ref, cst_ref, q_ref, kvc_ref, kvc_il_ref, kvs_ref, kvw_ref, ks_ref, vs_ref, kw_ref, vw_ref, gate_ref,
         ilbuf, vbuf) = refs
    i = pl.program_id(0)
    x = x_ref[...]
    h = _rms(x, g1_ref[...]) * (1.0 + sc_ref[0]) + sh_ref[0]
    hb = h.astype(BF16)

    zc = jnp.dot(hb, w_ref[:, _C_B:_C_Q], preferred_element_type=F32)
    b_g = zc[:, 0:CONV_W]
    v = zc[:, CONV_W:2 * CONV_W] * zc[:, 2 * CONV_W:3 * CONV_W]
    wc = wc_ref[...]
    if sample:
        y = wc[0:1] * p0_ref[...] + wc[1:2] * p1_ref[...] + wc[2:3] * v
        cst_ref[...] = v
    else:
        @pl.when(i % tpb == 0)
        def _():
            vbuf[0:SUBLANES, :] = jnp.zeros((SUBLANES, CONV_W), F32)
        vbuf[SUBLANES:SUBLANES + tm, :] = v
        y = wc[0:1] * vbuf[pl.ds(SUBLANES - 2, tm), :] + wc[1:2] * vbuf[pl.ds(SUBLANES - 1, tm), :] + wc[2:3] * v
        tail = vbuf[tm:tm + SUBLANES, :]
        cst_ref[0] = tail
        vbuf[0:SUBLANES, :] = tail
    conv_ref[...] = b_g * y

    cos = cos_ref[...]
    sin_s = sin_ref[...]
    first = _first_half_mask(tm)

    zq = jnp.dot(hb, w_ref[:, _C_Q:_C_KVC], preferred_element_type=F32)
    for gq in range(ATTN_W // LANES):
        qr = _rope128(zq[:, gq * LANES:(gq + 1) * LANES], cos, sin_s, first) * (HD ** -0.5)
        q_ref[0, 2 * gq] = qr[:, 0:HD].astype(BF16)
        q_ref[0, 2 * gq + 1] = qr[:, HD:LANES].astype(BF16)

    def store_rows(out_ref, halves):
        for j in range(2 * N_KV):
            piece = halves[j // N_KV]
            if j % N_KV == 1:
                piece = pltpu.roll(piece, HD, 1)
            ilbuf[pl.ds(j, tm, stride=2 * N_KV), :] = piece
        out_ref[...] = ilbuf[:, 0:HD]

    zkv = jnp.dot(hb, w_ref[:, _C_KVC:_C_G], preferred_element_type=F32)
    kvc_ref[...] = zkv[:, 0:2 * KV_W]
    store_rows(kvc_il_ref, (zkv[:, 0:KV_W], zkv[:, KV_W:2 * KV_W]))
    for base, kv_ref, kh_ref, vh_ref in ((2 * KV_W, kvs_ref, ks_ref, vs_ref), (4 * KV_W, kvw_ref, kw_ref, vw_ref)):
        kr = _rope128(zkv[:, base:base + KV_W], cos, sin_s, first)
        vv = zkv[:, base + KV_W:base + 2 * KV_W]
        store_rows(kv_ref, (kr, vv))
        for k in range(N_KV):
            kh_ref[0, k] = kr[:, k * HD:(k + 1) * HD].astype(BF16)
            vh_ref[0, k] = vv[:, k * HD:(k + 1) * HD].astype(BF16)

    zg = jnp.dot(hb, w_ref[:, _C_G:_C_END], preferred_element_type=F32)
    gate_ref[...] = jax.nn.sigmoid(zg)


def _proj(x2d, g1, sc, sh, w_pack, w_conv, cos_t, sin_t, *, nb, t, sample, prev=None):
    rows = nb * t
    tm = min(512, rows) if not sample else rows
    tpb = (t // tm) if not sample else 1
    n_tiles = rows // tm
    f = lambda a: jax.ShapeDtypeStruct(a, F32)
    b = lambda a: jax.ShapeDtypeStruct(a, BF16)
    if sample:
        mod_spec = pl.BlockSpec((1, tm, D_MODEL), lambda i: (0, 0, 0))
        tab_spec = pl.BlockSpec((1, LANES), lambda i: (0, 0))
        cst_shape, cst_spec = f((rows, CONV_W)), pl.BlockSpec((tm, CONV_W), lambda i: (0, 0))
        hm = lambda i: (0, 0, i, 0)
        hb_, ht_ = 1, rows
    else:
        mod_spec = pl.BlockSpec((1, 1, D_MODEL), lambda i: (i // tpb, 0, 0))
        tab_spec = pl.BlockSpec((tm, LANES), lambda i: (i % tpb, 0))
        cst_shape, cst_spec = f((nb, SUBLANES, CONV_W)), pl.BlockSpec((1, SUBLANES, CONV_W), lambda i: (i // tpb, 0, 0))
        hm = lambda i: (i // tpb, 0, i % tpb, 0)
        hb_, ht_ = nb, t
    row = lambda w: pl.BlockSpec((tm, w), lambda i: (i, 0))
    in_specs = [row(D_MODEL), pl.BlockSpec((1, D_MODEL), lambda i: (0, 0)), mod_spec, mod_spec,
                pl.BlockSpec((D_MODEL, _C_END), lambda i: (0, 0)),
                pl.BlockSpec((SUBLANES, CONV_W), lambda i: (0, 0)), tab_spec, tab_spec]
    args = [x2d, g1.reshape(1, D_MODEL), sc, sh, w_pack, w_conv, cos_t, sin_t]
    scratch = [pltpu.VMEM((2 * N_KV * tm, LANES), F32)]
    if sample:
        in_specs += [row(CONV_W), row(CONV_W)]
        args += [prev[0], prev[1]]
    else:
        scratch.append(pltpu.VMEM((tm + SUBLANES, CONV_W), F32))
    il_rows = 2 * N_KV * rows
    il = pl.BlockSpec((2 * N_KV * tm, HD), lambda i: (i, 0))
    out_shape = [f((rows, CONV_W)), cst_shape, b((hb_, N_HEADS, ht_, HD)),
                 f((rows, 2 * KV_W)), f((il_rows, HD)), f((il_rows, HD)), f((il_rows, HD)),
                 b((hb_, N_KV, ht_, HD)), b((hb_, N_KV, ht_, HD)), b((hb_, N_KV, ht_, HD)), b((hb_, N_KV, ht_, HD)),
                 f((rows, 2 * LANES))]
    out_specs = [row(CONV_W), cst_spec, pl.BlockSpec((1, N_HEADS, tm, HD), hm),
                 row(2 * KV_W), il, il, il,
                 pl.BlockSpec((1, N_KV, tm, HD), hm), pl.BlockSpec((1, N_KV, tm, HD), hm),
                 pl.BlockSpec((1, N_KV, tm, HD), hm), pl.BlockSpec((1, N_KV, tm, HD), hm),
                 row(2 * LANES)]
    return pl.pallas_call(
        functools.partial(_proj_kernel, tm=tm, tpb=tpb, sample=sample),
        grid=(n_tiles,), in_specs=in_specs, out_specs=out_specs, out_shape=out_shape,
        scratch_shapes=scratch, compiler_params=_params(1),
        name="proj_sample" if sample else "proj_prompt",
    )(*args)


def _cmpbias_kernel(pos_ref, w_ref, b1_ref, o_ref):
    for c in range(2):
        o_ref[c:c + 1, :] = jnp.sum(pos_ref[c] * w_ref[c], axis=0, keepdims=True) + b1_ref[c:c + 1, :]


def _cmpbias(cmp_pos, cmp_w1, cmp_b1):
    n = cmp_pos.shape[1] * cmp_pos.shape[2]
    return pl.pallas_call(
        _cmpbias_kernel,
        out_shape=jax.ShapeDtypeStruct((2, CMP_HID), F32),
        compiler_params=pltpu.CompilerParams(vmem_limit_bytes=VMEM_LIMIT),
        name="cmpbias",
    )(cmp_pos.reshape(2, n, 1), cmp_w1.reshape(2, n, CMP_HID), cmp_b1)


def _cmp_kernel(pt_ref, *refs, ppt, tiles):
    pages = refs[:ppt + 1]
    if tiles:
        unfold_ref = refs[ppt + 1]
        refs = refs[1:]
    w1_ref, b1_ref, w2_ref, b2_ref, cos_ref, sin_ref, ck_ref, cv_ref, lhs, pbuf = refs[ppt + 1:]
    r = ppt * SUBLANES
    first = _first_half_mask(r)
    for j in range(ppt + 1):
        rows = slice(j * SUBLANES, (j + 1) * SUBLANES)
        if tiles:
            a = pages[j][...].reshape(2 * KV_W, PAGE).astype(BF16)
            y = lax.dot_general(unfold_ref[...], a, _NT, preferred_element_type=F32)
            for c in range(2):
                for s in range(CMP_STRIDE):
                    lhs[c, rows, s * KV_W:(s + 1) * KV_W] = y[s * SUBLANES:(s + 1) * SUBLANES, c * KV_W:(c + 1) * KV_W]
        else:
            for c in range(2):
                for s in range(CMP_STRIDE):
                    src = slice(s * 2 * KV_W + c * KV_W, s * 2 * KV_W + (c + 1) * KV_W)
                    lhs[c, rows, s * KV_W:(s + 1) * KV_W] = pages[j][0, :, src]
    for c in range(2):
        p = jnp.dot(lhs[c].astype(BF16), w1_ref[c], preferred_element_type=F32)
        pbuf[...] = p[:, 2 * CMP_HID:4 * CMP_HID]
        hid = p[0:r, 0:2 * CMP_HID] + pbuf[pl.ds(1, r), :] + b1_ref[c]
        act = jax.nn.gelu(hid)
        comp = jnp.dot(act.astype(BF16), w2_ref[c], preferred_element_type=F32) + b2_ref[c]
        if c == 0:
            comp = _rope128(comp, cos_ref[...], sin_ref[...], first)
            out = ck_ref
        else:
            out = cv_ref
        for k in range(N_KV):
            out[0, k] = comp[:, k * HD:(k + 1) * HD]


def _cmp(pages, pt_flat, nb, n_pages, w1p, b1p, w2p, b2p, cos_c, sin_c, name, tiles):
    ppt = min(32, n_pages)
    n_tiles = n_pages // ppt
    r = ppt * SUBLANES
    n_chunk = n_pages * SUBLANES
    zeros = (0,) * (pages.ndim - 1)

    def page_map(j):
        return lambda b, t, pt: (pt[b * n_pages + t * ppt + j],) + zeros

    def next_map(b, t, pt):
        return (pt[b * n_pages + jnp.minimum(t * ppt + ppt, n_pages - 1)],) + zeros

    page_blk = (None, 2, N_KV, HD, PAGE) if tiles else (1, SUBLANES, CHUNK_ROW)
    in_specs = [pl.BlockSpec(page_blk, page_map(j)) for j in range(ppt)]
    in_specs.append(pl.BlockSpec(page_blk, next_map))
    const = lambda shp: pl.BlockSpec(shp, lambda b, t, pt: (0,) * len(shp))
    extra = []
    if tiles:
        row = np.arange(PAGE)
        tok = (row % SUBLANES) * CMP_STRIDE + row // SUBLANES
        extra = [jnp.asarray(tok[:, None] == np.arange(PAGE)[None, :], BF16)]
        in_specs.append(const((PAGE, PAGE)))
    in_specs += [const(w1p.shape), const(b1p.shape), const(w2p.shape), const(b2p.shape),
                 pl.BlockSpec((r, LANES), lambda b, t, pt: (t, 0)), pl.BlockSpec((r, LANES), lambda b, t, pt: (t, 0))]
    hm = pl.BlockSpec((1, N_KV, r, HD), lambda b, t, pt: (b, 0, t, 0))
    grid_spec = pltpu.PrefetchScalarGridSpec(
        num_scalar_prefetch=1, grid=(nb, n_tiles), in_specs=in_specs, out_specs=[hm, hm],
        scratch_shapes=[pltpu.VMEM((2, r + SUBLANES, CMP_STRIDE * KV_W), F32),
                        pltpu.VMEM((r + SUBLANES, 2 * CMP_HID), F32)])
    return pl.pallas_call(
        functools.partial(_cmp_kernel, ppt=ppt, tiles=tiles),
        grid_spec=grid_spec,
        out_shape=[jax.ShapeDtypeStruct((nb, N_KV, n_chunk, HD), F32)] * 2,
        compiler_params=_params(2), name=name,
    )(pt_flat, *([pages] * (ppt + 1)), *extra, w1p, b1p, w2p, b2p, cos_c, sin_c)


def _softmax_rows(s, valid):
    s = jnp.where(valid, s, NEG_INF)
    m = jnp.max(s, axis=-1, keepdims=True)
    e = jnp.exp(s - m)
    return e / jnp.sum(e, axis=-1, keepdims=True)


def _attn_p_kernel(q_ref, ck_ref, cv_ref, ks_ref, vs_ref, kw_ref, vw_ref, gate_ref, band_ref, exp_ref, o_ref,
                   *, n_cmp_pad, n_blk, kc, t):
    qb = pl.program_id(2)
    start = qb * Q_BLOCK
    rows = QPK * Q_BLOCK
    q = q_ref[0].reshape(rows, HD)
    tpos = start + lax.broadcasted_iota(I32, (Q_BLOCK, 1), 0)
    qpos = start + lax.broadcasted_iota(I32, (rows, 1), 0) % Q_BLOCK

    def biased(s, bias):
        width = s.shape[-1]
        return (s.reshape(QPK, Q_BLOCK, width) + bias[None]).reshape(rows, width)

    s_c = lax.dot_general(q, ck_ref[0, 0].astype(BF16), _NT, preferred_element_type=F32)
    cmp_end = (lax.broadcasted_iota(I32, (1, n_cmp_pad), 1) + 2) * CMP_STRIDE - 1
    s_c = biased(s_c, jnp.where(cmp_end <= tpos, 0.0, NEG_INF))
    m_c = jnp.maximum(jnp.max(s_c, axis=-1, keepdims=True), 0.5 * NEG_INF)
    e_c = jnp.exp(s_c - m_c)
    l_c = jnp.sum(e_c, axis=-1, keepdims=True)
    p_c = e_c * (1.0 / jnp.where(l_c > 0.0, l_c, 1.0))
    o_c = jnp.dot(p_c.astype(BF16), cv_ref[0, 0].astype(BF16), preferred_element_type=F32)

    pcs = p_c[0:Q_BLOCK] + p_c[Q_BLOCK:2 * Q_BLOCK] + p_c[2 * Q_BLOCK:3 * Q_BLOCK] + p_c[3 * Q_BLOCK:4 * Q_BLOCK]
    imp = lax.dot_general(band_ref[...], pcs, _NT, preferred_element_type=F32,
                          precision=lax.Precision.HIGHEST)
    blk = lax.broadcasted_iota(I32, (n_blk, Q_BLOCK), 0)
    tlane = start + lax.broadcasted_iota(I32, (1, Q_BLOCK), 1)
    cur = tlane // SEL_BLOCK
    causal = blk * SEL_BLOCK <= tlane
    forced = causal & ((blk == 0) | (blk == cur) | (blk == cur - 1))
    score = jnp.where(forced, FORCE_SCORE, jnp.where(causal, imp, -1.0))
    rank = jnp.zeros((n_blk, Q_BLOCK), F32)
    for bp in range(n_blk):
        other = score[bp:bp + 1, :]
        beats = (other > score) | ((other == score) & (bp < blk))
        rank = rank + beats.astype(F32)
    sel_t = (rank < float(min(N_SEL, n_blk))).astype(BF16)
    eye = (lax.broadcasted_iota(I32, (Q_BLOCK, Q_BLOCK), 0)
           == lax.broadcasted_iota(I32, (Q_BLOCK, Q_BLOCK), 1)).astype(BF16)
    sel = lax.dot_general(eye, sel_t, _NT, preferred_element_type=F32).astype(BF16)

    n_chunks = (start + Q_BLOCK + kc - 1) // kc

    def step(j, carry, causal_chunk):
        m_i, l_i, acc = carry
        off = pl.multiple_of(j * kc, kc)
        kj = ks_ref[0, 0, pl.ds(off, kc), :]
        vj = vs_ref[0, 0, pl.ds(off, kc), :]
        s = lax.dot_general(q, kj, _NT, preferred_element_type=F32)
        mexp = jnp.dot(sel, exp_ref[j], preferred_element_type=F32)
        bias = mexp * (-NEG_INF) + NEG_INF
        if causal_chunk:
            keypos = off + lax.broadcasted_iota(I32, (1, kc), 1)
            bias = jnp.where(keypos <= tpos, bias, NEG_INF)
        s = biased(s, bias)
        m_new = jnp.maximum(m_i, jnp.max(s, axis=-1, keepdims=True))
        alpha = jnp.exp(m_i - m_new)
        p = jnp.exp(s - m_new)
        l_new = alpha * l_i + jnp.sum(p, axis=-1, keepdims=True)
        acc_new = alpha * acc + jnp.dot(p.astype(BF16), vj, preferred_element_type=F32)
        return m_new, l_new, acc_new

    m0 = jnp.full((rows, 1), NEG_INF, F32)
    l0 = jnp.zeros((rows, 1), F32)
    a0 = jnp.zeros((rows, HD), F32)
    carry = lax.fori_loop(0, n_chunks - 1, lambda j, c: step(j, c, False), (m0, l0, a0))
    _, l_s, acc_s = step(n_chunks - 1, carry, True)
    o_s = acc_s * (1.0 / l_s)

    wlen = WINDOW + Q_BLOCK
    s0 = pl.multiple_of(jnp.maximum(start - WINDOW, 0), Q_BLOCK)
    kwin = kw_ref[0, 0, pl.ds(s0, wlen), :]
    vwin = vw_ref[0, 0, pl.ds(s0, wlen), :]
    s_w = lax.dot_general(q, kwin, _NT, preferred_element_type=F32)
    dist = tpos - (s0 + lax.broadcasted_iota(I32, (1, wlen), 1))
    s_w = biased(s_w, jnp.where((dist >= 0) & (dist <= WINDOW), 0.0, NEG_INF))
    e_w = jnp.exp(s_w - jnp.max(s_w, axis=-1, keepdims=True))
    o_w = jnp.dot(e_w.astype(BF16), vwin, preferred_element_type=F32) * (1.0 / jnp.sum(e_w, axis=-1, keepdims=True))

    g = gate_ref[...]
    for hq in range(QPK):
        rs = slice(hq * Q_BLOCK, (hq + 1) * Q_BLOCK)
        o = (g[:, hq:hq + 1] * o_c[rs] + g[:, QPK + hq:QPK + hq + 1] * o_s[rs]
             + g[:, 2 * QPK + hq:2 * QPK + hq + 1] * o_w[rs])
        o_ref[0, :, hq * HD:(hq + 1) * HD] = o


def _attn_prompt(q_hm, ck, cv, ks, vs, kw, vw, gates, band, expand, *, nb, t):
    n_qb = t // Q_BLOCK
    n_cmp_pad = ck.shape[2]
    n_blk = band.shape[0]
    kc = expand.shape[2]
    kv_spec = lambda n: pl.BlockSpec((1, 1, n, HD), lambda b, k, i: (b, k, 0, 0))
    return pl.pallas_call(
        functools.partial(_attn_p_kernel, n_cmp_pad=n_cmp_pad, n_blk=n_blk, kc=kc, t=t),
        grid=(nb, N_KV, n_qb),
        in_specs=[pl.BlockSpec((1, QPK, Q_BLOCK, HD), lambda b, k, i: (b, k, i, 0)),
                  kv_spec(n_cmp_pad), kv_spec(n_cmp_pad), kv_spec(t), kv_spec(t), kv_spec(t), kv_spec(t),
                  pl.BlockSpec((Q_BLOCK, LANES), lambda b, k, i: (b * n_qb + i, k)),
                  pl.BlockSpec(band.shape, lambda b, k, i: (0, 0)),
                  pl.BlockSpec(expand.shape, lambda b, k, i: (0, 0, 0))],
        out_specs=pl.BlockSpec((1, Q_BLOCK, QPK * HD), lambda b, k, i: (b, i, k)),
        out_shape=jax.ShapeDtypeStruct((nb, t, ATTN_W), F32),
        compiler_params=_params(3), name="attn_prompt",
    )(q_hm, ck, cv, ks, vs, kw, vw, gates, band, expand)


def _attn_s1_kernel(q_ref, ck_ref, cv_ref, band_ref, oc_ref, imp_ref, *, n_chunk, past):
    q = q_ref[0]
    q16 = jnp.concatenate([q, jnp.zeros_like(q)], axis=0).astype(BF16)
    cmp_end = (lax.broadcasted_iota(I32, (1, n_chunk), 1) + 2) * CMP_STRIDE - 1
    valid = cmp_end <= past
    head = lax.broadcasted_iota(I32, (2 * N_HEADS, 1), 0)
    oc = jnp.zeros((2 * N_HEADS, HD), F32)
    imps = []
    for k in range(N_KV):
        s = lax.dot_general(q16, ck_ref[0, k].astype(BF16), _NT, preferred_element_type=F32)
        p = _softmax_rows(s, valid) * valid.astype(F32)
        in_grp = (head >= k * QPK) & (head < (k + 1) * QPK)
        p = jnp.where(in_grp, p, 0.0)
        oc = oc + jnp.dot(p.astype(BF16), cv_ref[0, k].astype(BF16), preferred_element_type=F32)
        pcs = jnp.sum(p, axis=0, keepdims=True)
        pcs8 = jnp.broadcast_to(pcs, (SUBLANES, n_chunk))
        imps.append(jnp.dot(pcs8, band_ref[...], preferred_element_type=F32,
                            precision=lax.Precision.HIGHEST)[0:1])
    oc_ref[0] = oc[0:N_HEADS]
    imp_ref[0] = jnp.concatenate(imps + [jnp.zeros((SUBLANES - N_KV, imps[0].shape[1]), F32)], axis=0)


def _topk_s_kernel(imp_ref, idx_ref, *, n_sel_blocks, past):
    imp = imp_ref[...]
    rows, nbp = imp.shape
    blk = lax.broadcasted_iota(I32, (rows, nbp), 1)
    cur = past // SEL_BLOCK
    causal = blk * SEL_BLOCK <= past
    forced = causal & ((blk == 0) | (blk == cur) | (blk == cur - 1))
    score = jnp.where(forced, FORCE_SCORE, jnp.where(causal, imp, -1.0))
    score = jnp.where(blk < n_sel_blocks, score, -2.0)
    lane = lax.broadcasted_iota(I32, (rows, LANES), 1)
    out = jnp.zeros((rows, LANES), I32)
    for r in range(min(N_SEL, n_sel_blocks)):
        m = jnp.max(score, axis=-1, keepdims=True)
        pick = jnp.min(jnp.where(score == m, blk, nbp), axis=-1, keepdims=True)
        out = jnp.where(lane == r, pick, out)
        score = jnp.where(blk == pick, -3.0, score)
    idx_ref[...] = out


def _attn_s2_kernel(pt_ref, idx_ref, *refs, n_pages, past, n_sel_blocks):
    ktiles, vtiles = refs[:N_SEL], refs[N_SEL:2 * N_SEL]
    q_ref, oc_ref, kvs_ref, wk_ref, wv_ref, kvw_ref, gate_ref, o_ref, kbuf, vbuf = refs[2 * N_SEL:]
    b = pl.program_id(0)
    k = pl.program_id(1)
    q = q_ref[0]
    q16f = jnp.concatenate([q, jnp.zeros_like(q)], axis=0)
    q16 = q16f.astype(BF16)
    head = lax.broadcasted_iota(I32, (N_HEADS, 1), 0)
    nk = N_SEL * PAGE
    lane = lax.broadcasted_iota(I32, (1, nk), 1)
    slot = lane // PAGE
    new_blk = n_sel_blocks - 1
    wb = wk_ref.shape[-1]
    wpos = past - wb + lax.broadcasted_iota(I32, (1, wb), 1)
    wdist = past - wpos
    valid_w = (wdist >= 0) & (wdist <= WINDOW) & (wpos >= 0)

    def attend(s, valid, v_t, k_new, v_new):
        s_new = jnp.sum(q16f * k_new, axis=-1, keepdims=True)
        s = jnp.where(valid, s, NEG_INF)
        m = jnp.maximum(jnp.max(s, axis=-1, keepdims=True), s_new)
        e = jnp.exp(s - m)
        e_new = jnp.exp(s_new - m)
        den = jnp.sum(e, axis=-1, keepdims=True) + e_new
        acc = lax.dot_general(e.astype(BF16), v_t, _NT, preferred_element_type=F32) + e_new * v_new
        return acc / den

    in_grp = (head >= k * QPK) & (head < (k + 1) * QPK)
    bvec = jnp.zeros((1, nk), I32)
    for j in range(N_SEL):
        kbuf[:, j * PAGE:(j + 1) * PAGE] = ktiles[j][...].astype(BF16)
        vbuf[:, j * PAGE:(j + 1) * PAGE] = vtiles[j][...].astype(BF16)
        bvec = jnp.where(slot == j, idx_ref[(b * N_KV + k) * LANES + j], bvec)
    tok = (bvec // 2) * PAGE + lane % PAGE
    valid = (tok // SEL_BLOCK == bvec) & (bvec < new_blk) & (tok <= past)
    s = jnp.dot(q16, kbuf[...], preferred_element_type=F32)
    o_s = attend(s, valid, vbuf[...], kvs_ref[0, pl.ds(k, 1), :], kvs_ref[0, pl.ds(N_KV + k, 1), :])
    sw = jnp.dot(q16, wk_ref[...].astype(BF16), preferred_element_type=F32)
    o_w = attend(sw, valid_w, wv_ref[...].astype(BF16), kvw_ref[0, pl.ds(k, 1), :], kvw_ref[0, pl.ds(N_KV + k, 1), :])
    g = gate_ref[0]
    part = jnp.where(in_grp, g[:, 1:2] * o_s[0:N_HEADS] + g[:, 2:3] * o_w[0:N_HEADS], 0.0)

    @pl.when(k == 0)
    def _():
        o_ref[0] = g[:, 0:1] * oc_ref[0] + part

    @pl.when(k > 0)
    def _():
        o_ref[0] = o_ref[0] + part


def _attn_sample(q3, ck, cv, band_s, sel_t, pt_flat, kvs_rows, win_t, kvw_rows, gates_hm,
                 *, nb, n_pages, past, n_sel_blocks):
    n_chunk = ck.shape[2]
    nbp = band_s.shape[1]
    oc, imp = pl.pallas_call(
        functools.partial(_attn_s1_kernel, n_chunk=n_chunk, past=past),
        grid=(nb,),
        in_specs=[pl.BlockSpec((1, N_HEADS, HD), lambda b: (b, 0, 0)),
                  pl.BlockSpec((1, N_KV, n_chunk, HD), lambda b: (b, 0, 0, 0)),
                  pl.BlockSpec((1, N_KV, n_chunk, HD), lambda b: (b, 0, 0, 0)),
                  pl.BlockSpec(band_s.shape, lambda b: (0, 0))],
        out_specs=[pl.BlockSpec((1, N_HEADS, HD), lambda b: (b, 0, 0)),
                   pl.BlockSpec((1, SUBLANES, nbp), lambda b: (b, 0, 0))],
        out_shape=[jax.ShapeDtypeStruct((nb, N_HEADS, HD), F32), jax.ShapeDtypeStruct((nb, SUBLANES, nbp), F32)],
        compiler_params=_params(1), name="attn_sample_cmp",
    )(q3, ck, cv, band_s)
    imp2 = imp[:, 0:N_KV, :].reshape(nb * N_KV, nbp)
    idx = pl.pallas_call(
        functools.partial(_topk_s_kernel, n_sel_blocks=n_sel_blocks, past=past),
        out_shape=jax.ShapeDtypeStruct((nb * N_KV, LANES), I32),
        compiler_params=pltpu.CompilerParams(vmem_limit_bytes=VMEM_LIMIT), name="topk_sample",
    )(imp2)
    idx_flat = idx.reshape(-1)

    def tile_map(c, j):
        def f(b, k, pt, ix):
            bidx = ix[(b * N_KV + k) * LANES + j]
            return (pt[b * n_pages + jnp.minimum(bidx // 2, n_pages - 1)], c, k, 0, 0)
        return f

    tile = lambda c, j: pl.BlockSpec((None, None, None, HD, PAGE), tile_map(c, j))
    in_specs = [tile(0, j) for j in range(N_SEL)] + [tile(1, j) for j in range(N_SEL)]
    wb = win_t.shape[-1]
    per_b = lambda shp: pl.BlockSpec(shp, lambda b, k, pt, ix: (b, 0, 0))
    in_specs += [per_b((1, N_HEADS, HD)), per_b((1, N_HEADS, HD)), per_b((1, SUBLANES, HD)),
                 pl.BlockSpec((None, None, None, HD, wb), lambda b, k, pt, ix: (b, 0, k, 0, 0)),
                 pl.BlockSpec((None, None, None, HD, wb), lambda b, k, pt, ix: (b, 1, k, 0, 0)),
                 per_b((1, SUBLANES, HD)), per_b((1, N_HEADS, LANES))]
    grid_spec = pltpu.PrefetchScalarGridSpec(
        num_scalar_prefetch=2, grid=(nb, N_KV), in_specs=in_specs,
        out_specs=per_b((1, N_HEADS, HD)),
        scratch_shapes=[pltpu.VMEM((HD, N_SEL * PAGE), BF16), pltpu.VMEM((HD, N_SEL * PAGE), BF16)])
    return pl.pallas_call(
        functools.partial(_attn_s2_kernel, n_pages=n_pages, past=past, n_sel_blocks=n_sel_blocks),
        grid_spec=grid_spec,
        out_shape=jax.ShapeDtypeStruct((nb, N_HEADS, HD), F32),
        compiler_params=_params(2), name="attn_sample_sel",
    )(pt_flat, idx_flat, *([sel_t] * (2 * N_SEL)), q3, oc, kvs_rows, win_t, win_t, kvw_rows, gates_hm)


def _outp_kernel(xp_ref, convp_ref, attnp_ref, ga1p_ref, sc2p_ref, sh2p_ref,
                 xs_ref, convs_ref, attns_ref, ga1s_ref, sc2s_ref, sh2s_ref,
                 gc_ref, ga_ref, w_ref, g2_ref, wr_ref, x1_ref, hp_ref, lg_ref, *, n_prompt_tiles):
    is_p = pl.program_id(0) < n_prompt_tiles
    pick = lambda a, b: jnp.where(is_p, a, b)
    cn = _rms(pick(convp_ref[...], convs_ref[...]), gc_ref[...])
    an = _rms(pick(attnp_ref[...], attns_ref[...]), ga_ref[...])
    cat = jnp.concatenate([cn, an], axis=1).astype(BF16)
    y = jnp.dot(cat, w_ref[...], preferred_element_type=F32)
    x1 = pick(xp_ref[...], xs_ref[...]) + pick(ga1p_ref[0], ga1s_ref[0]) * y
    x1_ref[...] = x1
    hp = _rms(x1, g2_ref[...]) * (1.0 + pick(sc2p_ref[0], sc2s_ref[0])) + pick(sh2p_ref[0], sh2s_ref[0])
    hp_ref[...] = hp
    lg_ref[...] = jnp.dot(hp, wr_ref[...], preferred_element_type=F32, precision=lax.Precision.HIGHEST)


TOKEN_TILE = 512


def _outp(prompt, sample, g_conv, g_attn, w_out_b, g2, w_route, *, tpb):
    tm = TOKEN_TILE
    n_p = prompt[0].shape[0] // tm
    total = (n_p + 1) * tm
    last = n_p - 1
    prow = lambda w: pl.BlockSpec((tm, w), lambda i: (jnp.minimum(i, last), 0))
    srow = lambda w: pl.BlockSpec((tm, w), lambda i: (0, 0))
    pmod = pl.BlockSpec((1, 1, D_MODEL), lambda i: (jnp.minimum(i, last) // tpb, 0, 0))
    smod = pl.BlockSpec((1, tm, D_MODEL), lambda i: (0, 0, 0))
    vec = lambda w: pl.BlockSpec((1, w), lambda i: (0, 0))
    row = lambda w: pl.BlockSpec((tm, w), lambda i: (i, 0))
    in_specs = [prow(D_MODEL), prow(CONV_W), prow(ATTN_W), pmod, pmod, pmod,
                srow(D_MODEL), srow(CONV_W), srow(ATTN_W), smod, smod, smod,
                vec(CONV_W), vec(ATTN_W), pl.BlockSpec((D_MODEL, D_MODEL), lambda i: (0, 0)), vec(D_MODEL),
                pl.BlockSpec((D_MODEL, LANES), lambda i: (0, 0))]
    return pl.pallas_call(
        functools.partial(_outp_kernel, n_prompt_tiles=n_p),
        grid=(n_p + 1,), in_specs=in_specs,
        out_specs=[row(D_MODEL), row(D_MODEL), row(LANES)],
        out_shape=[jax.ShapeDtypeStruct((total, D_MODEL), F32), jax.ShapeDtypeStruct((total, D_MODEL), F32),
                   jax.ShapeDtypeStruct((total, LANES), F32)],
        compiler_params=_params(1), name="outp",
    )(*prompt, *sample, g_conv.reshape(1, -1), g_attn.reshape(1, -1), w_out_b, g2.reshape(1, -1), w_route)


def _route_kernel(lg_ref, bias_ref, tri_ref, o_ref, cnt_ref, carry, *, tm, n_valid):
    i = pl.program_id(0)

    @pl.when(i == 0)
    def _():
        carry[...] = jnp.zeros_like(carry)

    lane = lax.broadcasted_iota(I32, (tm, LANES), 1)
    rowid = i * tm + lax.broadcasted_iota(I32, (tm, 1), 0)
    live = rowid < n_valid
    lg = lg_ref[...] + bias_ref[...]
    is_g = lane < N_GROUPS
    lgg = jnp.where(is_g, lg, NEG_INF)
    gmax = jnp.max(lgg, axis=-1, keepdims=True)
    grp = jnp.min(jnp.where(is_g & (lgg == gmax), lane, LANES), axis=-1, keepdims=True)
    p_grp = 1.0 / jnp.sum(jnp.where(is_g, jnp.exp(lgg - gmax), 0.0), axis=-1, keepdims=True)
    eid = lane - N_GROUPS
    in_grp = (eid >= grp * EPG) & (eid < (grp + 1) * EPG)
    le = jnp.where(in_grp, lg, NEG_INF)
    v1 = jnp.max(le, axis=-1, keepdims=True)
    e1 = jnp.min(jnp.where(in_grp & (le == v1), eid, LANES), axis=-1, keepdims=True)
    le2 = jnp.where(eid == e1, NEG_INF, le)
    v2 = jnp.max(le2, axis=-1, keepdims=True)
    e2 = jnp.min(jnp.where(in_grp & (eid != e1) & (le2 == v2), eid, LANES), axis=-1, keepdims=True)
    ex2 = jnp.exp(v2 - v1)
    w1 = p_grp * (1.0 / (1.0 + ex2))
    w2 = p_grp * (ex2 / (1.0 + ex2))
    oh1 = ((lane == e1) & live).astype(F32)
    oh2 = ((lane == e2) & live).astype(F32)
    both = oh1 + oh2
    before = jnp.dot(tri_ref[...], both.astype(BF16), preferred_element_type=F32) + carry[0:1, :]
    r1 = jnp.sum(oh1 * before, axis=-1, keepdims=True)
    r2 = jnp.sum(oh2 * before, axis=-1, keepdims=True)
    carry[0:1, :] = carry[0:1, :] + jnp.sum(both, axis=0, keepdims=True)
    out = jnp.where(lane == 0, e1.astype(F32), 0.0)
    out = jnp.where(lane == 1, e2.astype(F32), out)
    out = jnp.where(lane == 2, w1, out)
    out = jnp.where(lane == 3, w2, out)
    out = jnp.where(lane == 4, r1, out)
    out = jnp.where(lane == 5, r2, out)
    o_ref[...] = out
    cnt_ref[...] = carry[...]


def _route(logits, bias_row, n_valid):
    total = logits.shape[0]
    tm = TOKEN_TILE
    n_tiles = total // tm
    tri =(np.arange(tm)[:, None] > np.arange(tm)[None, :]).astype(np.float32)
    return pl.pallas_call(
        functools.partial(_route_kernel, tm=tm, n_valid=n_valid),
        grid=(n_tiles,),
        in_specs=[pl.BlockSpec((tm, LANES), lambda i: (i, 0)), pl.BlockSpec((1, LANES), lambda i: (0, 0)),
                  pl.BlockSpec((tm, tm), lambda i: (0, 0))],
        out_specs=[pl.BlockSpec((tm, LANES), lambda i: (i, 0)), pl.BlockSpec((SUBLANES, LANES), lambda i: (0, 0))],
        out_shape=[jax.ShapeDtypeStruct((total, LANES), F32), jax.ShapeDtypeStruct((SUBLANES, LANES), F32)],
        scratch_shapes=[pltpu.VMEM((SUBLANES, LANES), F32)],
        compiler_params=_params(1), name="route",
    )(logits, bias_row, jnp.asarray(tri, BF16))


def _row_copy(src_hbm, row, dst, slot, r, sem):
    return pltpu.make_async_copy(src_hbm.at[pl.ds(row, 1), :], dst.at[slot, pl.ds(r, 1), :], sem.at[slot])


EXPERT_ROWS = 256


def _experts_kernel(blk_e_ref, tok_ref, x_hbm, wg_ref, wu_ref, wd_ref, o_ref, xbuf, sem, wg_b, wu_b, wd_b, *, n_blocks):
    i = pl.program_id(0)
    slot = i % 2

    def issue(blk, s):
        for r in range(EXPERT_ROWS):
            _row_copy(x_hbm, tok_ref[blk * EXPERT_ROWS + r], xbuf, s, r, sem).start(priority=r % 2)

    @pl.when(i == 0)
    def _():
        issue(0, 0)

    @pl.when(i + 1 < n_blocks)
    def _():
        issue(i + 1, 1 - slot)

    changed = jnp.logical_or(i == 0, blk_e_ref[i] != blk_e_ref[jnp.maximum(i - 1, 0)])

    @pl.when(changed)
    def _():
        wg_b[...] = wg_ref[0].astype(BF16)
        wu_b[...] = wu_ref[0].astype(BF16)
        wd_b[...] = wd_ref[0].astype(BF16)

    for r in range(EXPERT_ROWS):
        _row_copy(x_hbm, 0, xbuf, slot, r, sem).wait()
    x = xbuf[slot].astype(BF16)
    g = jnp.dot(x, wg_b[...], preferred_element_type=F32)
    u = jnp.dot(x, wu_b[...], preferred_element_type=F32)
    h = (g * jax.nn.sigmoid(g)) * u
    o_ref[...] = jnp.dot(h.astype(BF16), wd_b[...], preferred_element_type=F32)


def _experts(blk_e, slot_tok, hp_all, w_gate, w_up, w_down, n_blocks):
    grid_spec = pltpu.PrefetchScalarGridSpec(
        num_scalar_prefetch=2, grid=(n_blocks,),
        in_specs=[pl.BlockSpec(memory_space=pl.ANY),
                  pl.BlockSpec((1, D_MODEL, D_EXPERT), lambda i, be, st: (be[i], 0, 0)),
                  pl.BlockSpec((1, D_MODEL, D_EXPERT), lambda i, be, st: (be[i], 0, 0)),
                  pl.BlockSpec((1, D_EXPERT, D_MODEL), lambda i, be, st: (be[i], 0, 0))],
        out_specs=pl.BlockSpec((EXPERT_ROWS, D_MODEL), lambda i, be, st: (i, 0)),
        scratch_shapes=[pltpu.VMEM((2, EXPERT_ROWS, D_MODEL), F32), pltpu.SemaphoreType.DMA((2,)),
                        pltpu.VMEM((D_MODEL, D_EXPERT), BF16), pltpu.VMEM((D_MODEL, D_EXPERT), BF16),
                        pltpu.VMEM((D_EXPERT, D_MODEL), BF16)])
    return pl.pallas_call(
        functools.partial(_experts_kernel, n_blocks=n_blocks),
        grid_spec=grid_spec,
        out_shape=jax.ShapeDtypeStruct((n_blocks * EXPERT_ROWS, D_MODEL), F32),
        compiler_params=_params(1), name="experts",
    )(blk_e, slot_tok, hp_all, w_gate, w_up, w_down)


def _final_kernel(dest_ref, yb_hbm, x1_ref, wt_ref, gate2_ref, gf_ref, o_ref, ybuf, sem, *, tm, n_tiles, row0):
    i = pl.program_id(0)
    slot = i % 2

    def copy(tile, s, r, k):
        d = dest_ref[(row0 + tile * tm + r) * 2 + k]
        return pltpu.make_async_copy(yb_hbm.at[pl.ds(d, 1), :], ybuf.at[s, k, pl.ds(r, 1), :], sem.at[s])

    def issue(tile, s):
        for r in range(tm):
            for k in range(2):
                copy(tile, s, r, k).start(priority=k)

    @pl.when(i == 0)
    def _():
        issue(0, 0)

    @pl.when(i + 1 < n_tiles)
    def _():
        issue(i + 1, 1 - slot)

    for r in range(tm):
        for k in range(2):
            pltpu.make_async_copy(yb_hbm.at[pl.ds(0, 1), :], ybuf.at[slot, k, pl.ds(r, 1), :], sem.at[slot]).wait()
    wt = wt_ref[...]
    f = wt[:, 2:3] * ybuf[slot, 0] + wt[:, 3:4] * ybuf[slot, 1]
    x2 = x1_ref[...] + gate2_ref[0] * f
    o_ref[...] = _rms(x2, gf_ref[...])


def _final(dest_flat, yb, x1_all, route_rows, gate2, final_g, *, rows, tpb, per_row, row0):
    tm = min(256, rows)
    n_tiles = rows // tm
    blk0 = row0 // tm
    mod = (pl.BlockSpec((1, tm, D_MODEL), lambda i, d: (0, i, 0)) if per_row
           else pl.BlockSpec((1, 1, D_MODEL), lambda i, d: (i // tpb, 0, 0)))
    grid_spec = pltpu.PrefetchScalarGridSpec(
        num_scalar_prefetch=1, grid=(n_tiles,),
        in_specs=[pl.BlockSpec(memory_space=pl.ANY),
                  pl.BlockSpec((tm, D_MODEL), lambda i, d: (blk0 + i, 0)),
                  pl.BlockSpec((tm, LANES), lambda i, d: (blk0 + i, 0)),
                  mod, pl.BlockSpec((1, D_MODEL), lambda i, d: (0, 0))],
        out_specs=pl.BlockSpec((tm, D_MODEL), lambda i, d: (i, 0)),
        scratch_shapes=[pltpu.VMEM((2, 2, tm, D_MODEL), F32), pltpu.SemaphoreType.DMA((2,))])
    return pl.pallas_call(
        functools.partial(_final_kernel, tm=tm, n_tiles=n_tiles, row0=row0),
        grid_spec=grid_spec,
        out_shape=jax.ShapeDtypeStruct((rows, D_MODEL), F32),
        compiler_params=_params(1), name="final_sample" if per_row else "final_prompt",
    )(dest_flat, yb, x1_all, route_rows, gate2, final_g.reshape(1, -1))


def _rope_tables(pos):
    inv = ROPE_THETA ** (-jnp.arange(HALF, dtype=F32) / HALF)
    ang = pos.astype(F32)[:, None] * inv[None, :]
    cos = jnp.tile(jnp.cos(ang), (1, LANES // HALF))
    sin = jnp.sin(ang)
    sin_s = jnp.tile(jnp.concatenate([-sin, sin], axis=1), (1, LANES // HD))
    return cos, sin_s


def _pack_w_in(w_in):
    gl = w_in[:, _C_G:_C_G + 3 * N_HEADS].reshape(D_MODEL, 3, N_KV, QPK)
    gcols = []
    for k in range(N_KV):
        gk = gl[:, :, k, :].reshape(D_MODEL, 3 * QPK)
        gcols.append(jnp.pad(gk, ((0, 0), (0, LANES - 3 * QPK))))
    return jnp.concatenate([w_in[:, :_C_G]] + gcols, axis=1).astype(BF16)


def _pack_cmp_weights(cmp_w1, cmp_w2, bias, cmp_b2):
    w1 = cmp_w1.reshape(2, 2, CMP_STRIDE, HD, CMP_HID)
    eye = jnp.eye(N_KV, dtype=F32)
    w1p = jnp.einsum('crsdh,pk->cspdrkh', w1, eye).reshape(2, CMP_STRIDE * KV_W, 2 * N_KV * CMP_HID)
    w2p = jnp.einsum('chd,pk->cphkd', cmp_w2, eye).reshape(2, N_KV * CMP_HID, KV_W)
    b1p = jnp.tile(bias, (1, N_KV)).reshape(2, 1, N_KV * CMP_HID)
    b2p = jnp.tile(cmp_b2, (1, N_KV)).reshape(2, 1, KV_W)
    return w1p.astype(BF16), b1p, w2p.astype(BF16), b2p


def _band(n_cmp_pad, n_cmp, n_blk_pad, n_blk):
    n = np.arange(n_cmp_pad)[:, None]
    b = np.arange(n_blk_pad)[None, :]
    r = SEL_BLOCK // CMP_STRIDE
    m = (n >= r * b - 1) & (n <= r * b + r - 1) & (n < n_cmp) & (b < n_blk)
    return jnp.asarray(m.astype(np.float32))


def _expand(t, kc):
    n_chunks = t // kc
    key = np.arange(t).reshape(n_chunks, 1, kc)
    blk = np.arange(t // SEL_BLOCK).reshape(1, -1, 1)
    return jnp.asarray((key // SEL_BLOCK == blk).astype(np.float32), BF16)


def kernel(x_prompt, x_sample, c_prompt, c_sample, cache_cmp_kv, cache_sel_kv, cache_win_kv, state_conv, page_table,
           ln1_g, ln2_g, w_ada, b_ada, w_in, w_conv, cmp_pos, cmp_w1, cmp_b1, cmp_w2, cmp_b2, g_out_conv, g_out_attn,
           w_out, w_route_group, b_route_group, w_route_expert, b_route_expert, w_gate, w_up, w_down, final_g):
    depth = w_in.shape[0]
    assert depth == 1, "single-layer step"
    nb, t, _ = x_prompt.shape
    ns, ts, _ = x_sample.shape
    assert ts == 1 and t % 512 == 0 and t >= WINDOW + Q_BLOCK
    n_pool = cache_cmp_kv.shape[1]
    n_pages = page_table.shape[1]
    past = n_pages * PAGE
    wb = cache_win_kv.shape[2]
    assert wb == WINDOW
    l = 0

    n_c = nb + ns
    c_all = jnp.pad(jnp.concatenate([c_prompt, c_sample], axis=0), ((0, (-n_c) % SUBLANES), (0, 0)))
    mods = _ada(c_all, w_ada[l], b_ada[l])
    sh1, sc1, ga1, sh2, sc2, ga2 = [mods[:, j * D_MODEL:(j + 1) * D_MODEL] for j in range(6)]
    pr = lambda a: a[0:nb].reshape(nb, 1, D_MODEL)
    sr = lambda a: a[nb:nb + ns].reshape(1, ns, D_MODEL)

    w_pack = _pack_w_in(w_in[l])
    wconv8 = jnp.pad(w_conv[l], ((0, SUBLANES - CONV_K), (0, 0)))
    cos_p, sin_p = _rope_tables(jnp.arange(t, dtype=I32))
    cos_s, sin_s = _rope_tables(jnp.full((1,), past, I32))
    xp2 = x_prompt.reshape(nb * t, D_MODEL)
    xs2 = x_sample.reshape(ns, D_MODEL)
    (conv_p, cst_p, q_p, kvc_p, kvc_rows_p, kvs_rows_p, kvw_rows_p, ks_p, vs_p, kw_p, vw_p, gates_p) = _proj(
        xp2, ln1_g[l], pr(sc1), pr(sh1), w_pack, wconv8, cos_p, sin_p, nb=nb, t=t, sample=False)
    (conv_s, cst_s, q_s, _, kvc_rows_s, kvs_rows_s, kvw_rows_s, _, _, _, _, gates_s) = _proj(
        xs2, ln1_g[l], sr(sc1), sr(sh1), w_pack, wconv8, cos_s, sin_s, nb=ns, t=1, sample=True,
        prev=(state_conv[l][:, 0], state_conv[l][:, 1]))

    bias = _cmpbias(cmp_pos[l], cmp_w1[l], cmp_b1[l])
    w1p, b1p, w2p, b2p = _pack_cmp_weights(cmp_w1[l], cmp_w2[l], bias, cmp_b2[l])
    pp = t // PAGE
    cos_cp, sin_cp = _rope_tables((jnp.arange(t // CMP_STRIDE, dtype=I32) + 2) * CMP_STRIDE - 1)
    ck_p, cv_p = _cmp(kvc_p.reshape(nb * pp, SUBLANES, CHUNK_ROW), jnp.arange(nb * pp, dtype=I32), nb, pp,
                      w1p, b1p, w2p, b2p, cos_cp, sin_cp, "cmp_prompt", tiles=False)
    pt_flat = page_table.reshape(-1).astype(I32)
    cos_cs, sin_cs = _rope_tables((jnp.arange(past // CMP_STRIDE, dtype=I32) + 2) * CMP_STRIDE - 1)
    to_tiles = lambda a: a.transpose(0, 2, 3, 4, 1)
    ck_s, cv_s = _cmp(to_tiles(cache_cmp_kv[l]), pt_flat, ns, n_pages,
                      w1p, b1p, w2p, b2p, cos_cs, sin_cs, "cmp_sample", tiles=True)

    n_chunk_p = t // CMP_STRIDE
    n_blk_p = t // SEL_BLOCK
    band_p = _band(n_chunk_p, n_chunk_p - 1, n_blk_p, n_blk_p)
    attn_p = _attn_prompt(q_p, ck_p, cv_p, ks_p, vs_p, kw_p, vw_p, gates_p, band_p.T, _expand(t, 512), nb=nb, t=t)

    n_chunk_s = past // CMP_STRIDE
    n_sel_s = -(-(past + 1) // SEL_BLOCK)
    nbp = -(-n_sel_s // LANES) * LANES
    band_s = _band(n_chunk_s, (past + 1) // CMP_STRIDE - 1, nbp, n_sel_s)
    q3 = q_s.reshape(N_HEADS, ns, HD).transpose(1, 0, 2).astype(F32)
    gs = gates_s.reshape(ns, N_KV, LANES)[:, :, :3 * QPK].reshape(ns, N_KV, 3, QPK)
    gates_hm = jnp.pad(gs.transpose(0, 1, 3, 2).reshape(ns, N_HEADS, 3), ((0, 0), (0, 0), (0, LANES - 3)))
    rpt = 2 * N_KV
    new_rows = lambda a: jnp.pad(a.reshape(ns, rpt, HD), ((0, 0), (0, SUBLANES - rpt), (0, 0)))
    attn_s = _attn_sample(q3, ck_s, cv_s, band_s, to_tiles(cache_sel_kv[l]), pt_flat,
                          new_rows(kvs_rows_s), to_tiles(cache_win_kv[l]), new_rows(kvw_rows_s),
                          gates_hm, nb=ns, n_pages=n_pages, past=past, n_sel_blocks=n_sel_s).reshape(ns, ATTN_W)

    total = nb * t + ns
    w_out_b = w_out[l].astype(BF16)
    w_route = jnp.pad(jnp.concatenate([w_route_group[l], w_route_expert[l]], axis=1),
                      ((0, 0), (0, LANES - N_GROUPS - N_EXPERTS)))
    b_route = jnp.pad(jnp.concatenate([b_route_group[l], b_route_expert[l]]), (0, LANES - N_GROUPS - N_EXPERTS))
    tile_pad = lambda a: jnp.pad(a, ((0, TOKEN_TILE - ns), (0, 0)))
    smod = lambda a: tile_pad(a[nb:nb + ns]).reshape(1, TOKEN_TILE, D_MODEL)
    x1_all, hp_all, lg_all = _outp(
        (xp2, conv_p, attn_p.reshape(nb * t, ATTN_W), pr(ga1), pr(sc2), pr(sh2)),
        (tile_pad(xs2), tile_pad(conv_s), tile_pad(attn_s), smod(ga1), smod(sc2), smod(sh2)),
        g_out_conv[l], g_out_attn[l], w_out_b, ln2_g[l], w_route, tpb=t // TOKEN_TILE)

    route, counts = _route(lg_all, b_route.reshape(1, LANES), total)
    e = route[:total, 0:2].astype(I32)
    rank = route[:total, 4:6].astype(I32)
    cnt = counts[0, :N_EXPERTS].astype(I32)
    padded = (cnt + EXPERT_ROWS - 1) // EXPERT_ROWS * EXPERT_ROWS
    pad_end = jnp.cumsum(padded)
    pad_start = pad_end - padded
    m_slots = total * 2
    n_blocks = -(-(m_slots + N_EXPERTS * (EXPERT_ROWS - 1)) // EXPERT_ROWS)
    dest = jnp.clip(pad_start[e] + rank, 0, n_blocks * EXPERT_ROWS - 1)
    tok = jnp.broadcast_to(jnp.arange(total, dtype=I32)[:, None], (total, 2))
    slot_tok = jnp.zeros((n_blocks * EXPERT_ROWS,), I32).at[dest.reshape(-1)].set(
        tok.reshape(-1), unique_indices=True)
    blk_start = jnp.arange(n_blocks, dtype=I32) * EXPERT_ROWS
    blk_e = jnp.minimum(jnp.sum((pad_end[None, :] <= blk_start[:, None]).astype(I32), axis=1), N_EXPERTS - 1)

    yb = _experts(blk_e, slot_tok, hp_all, w_gate[l], w_up[l], w_down[l], n_blocks)
    dest_flat = dest.reshape(-1)
    y_p = _final(dest_flat, yb, x1_all, route, pr(ga2), final_g, rows=nb * t, tpb=t // 256, per_row=False, row0=0)
    y_s = _final(dest_flat, yb, x1_all, route, sr(ga2), final_g, rows=ns, tpb=1, per_row=True, row0=nb * t)

    kv_shape = (2, N_KV, HD)
    y_prompt = y_p.reshape(nb, t, D_MODEL)
    y_sample = y_s.reshape(ns, 1, D_MODEL)
    new_cmp_prompt = kvc_rows_p.reshape((1, nb, t) + kv_shape)
    new_cmp_sample = kvc_rows_s.reshape((1, ns, 1) + kv_shape)
    new_sel_prompt = kvs_rows_p.reshape((1, nb, t) + kv_shape)
    new_sel_sample = kvs_rows_s.reshape((1, ns, 1) + kv_shape)
    new_win_prompt = kvw_rows_p.reshape((nb, t) + kv_shape)[:, t - WINDOW:][None]
    new_win_sample = jnp.concatenate([cache_win_kv[l][:, 1:], kvw_rows_s.reshape((ns, 1) + kv_shape)], axis=1)[None]
    new_conv_prompt = cst_p[:, SUBLANES - (CONV_K - 1):][None]
    new_conv_sample = jnp.stack([state_conv[l][:, 1], cst_s], axis=1)[None]
    return (y_prompt, y_sample, new_cmp_prompt, new_cmp_sample, new_sel_prompt, new_sel_sample,
            new_win_prompt, new_win_sample, new_conv_prompt, new_conv_sample)
```

```python
import functools

import numpy as np
import jax
import jax.numpy as jnp
from jax import lax
from jax.experimental import pallas as pl
from jax.experimental.pallas import tpu as pltpu

F32 = jnp.float32
BF16 = jnp.bfloat16
I32 = jnp.int32

D_MODEL = 1024
CONV_W = 512
ATTN_W = 512
HD = 64
HALF = HD // 2
N_HEADS = 8
N_KV = 2
QPK = 4
KV_W = N_KV * HD
CONV_K = 3
PAGE = 128
CMP_STRIDE = 16
CMP_HID = 128
SEL_BLOCK = 64
N_SEL = 16
WINDOW = 512
Q_BLOCK = 128
ROPE_THETA = 10000.0
N_GROUPS = 4
EPG = 8
N_EXPERTS = 32
D_EXPERT = 512
MOE_BLOCK = 128
NORM_EPS = 1e-6
NEG_INF = -1e30
FORCE_SCORE = 1e4
LANES = 128
SUBLANES = 8
CHUNK_ROW = CMP_STRIDE * 2 * KV_W
VMEM_LIMIT = 56 * 1024 * 1024

_NT = (((1,), (1,)), ((), ()))
Q_SCALE = HD ** -0.5 * 1.4426950408889634


def _params(n_axes):
    return pltpu.CompilerParams(dimension_semantics=("arbitrary",) * n_axes,
                                vmem_limit_bytes=VMEM_LIMIT)


def _rms(x, g):
    return x * lax.rsqrt(jnp.mean(x * x, axis=-1, keepdims=True) + NORM_EPS) * g


def _rope128(x, cos, sin_signed, first_half):
    xr = jnp.where(first_half, pltpu.roll(x, LANES - HALF, 1), pltpu.roll(x, HALF, 1))
    return x * cos + xr * sin_signed


def _first_half_mask(rows):
    lane = lax.broadcasted_iota(I32, (rows, LANES), 1)
    return (lane % HD) < HALF


def _ada_kernel(c_ref, w_ref, b_ref, o_ref):
    c = c_ref[...]
    s = c * jax.nn.sigmoid(c)
    o_ref[...] = jnp.dot(s.astype(BF16), w_ref[...].astype(BF16), preferred_element_type=F32) + b_ref[...]


def _ada(c_all, w_ada, b_ada):
    m, d = c_all.shape
    n = w_ada.shape[1]
    tn = 1024
    return pl.pallas_call(
        _ada_kernel,
        grid=(n // tn,),
        in_specs=[pl.BlockSpec((m, d), lambda j: (0, 0)),
                  pl.BlockSpec((d, tn), lambda j: (0, j)),
                  pl.BlockSpec((1, tn), lambda j: (0, j))],
        out_specs=pl.BlockSpec((m, tn), lambda j: (0, j)),
        out_shape=jax.ShapeDtypeStruct((m, n), F32),
        compiler_params=_params(1),
        name="ada",
    )(c_all, w_ada, b_ada.reshape(1, n))


_C_B, _C_C, _C_U, _C_Q, _C_KVC, _C_KVS, _C_KVW, _C_G, _C_END = 0, 512, 1024, 1536, 2048, 2304, 2560, 2816, 3072


def _proj_kernel(*refs, tm, tpb, sample):
    if sample:
        (x_ref, g1_ref, sc_ref, sh_ref, w_ref, wc_ref, cos_ref, sin_ref, p0_ref, p1_ref,
         conv_ref, cst_ref, q_ref, kvc_ref, kvc_il_ref, kvs_ref, kvw_ref, ks_ref, vs_ref, kw_ref, vw_ref, gate_ref,
         ilbuf) = refs
        vbuf = None
    else:
        (x_ref, g1_ref, sc_ref, sh_ref, w_ref, wc_ref, cos_ref, sin_ref,
         conv_ref, cst_ref, q_ref, kvc_ref, kvc_il_ref, kvs_ref, kvw_ref, ks_ref, vs_ref, kw_ref, vw_ref, gate_ref,
         ilbuf, vbuf) = refs
    i = pl.program_id(0)
    x = x_ref[...]
    h = _rms(x, g1_ref[...]) * (1.0 + sc_ref[0]) + sh_ref[0]
    hb = h.astype(BF16)

    zc = jnp.dot(hb, w_ref[:, _C_B:_C_Q], preferred_element_type=F32)
    b_g = zc[:, 0:CONV_W]
    v = zc[:, CONV_W:2 * CONV_W] * zc[:, 2 * CONV_W:3 * CONV_W]
    wc = wc_ref[...]
    if sample:
        y = wc[0:1] * p0_ref[...] + wc[1:2] * p1_ref[...] + wc[2:3] * v
        cst_ref[...] = v
    else:
        @pl.when(i % tpb == 0)
        def _():
            vbuf[0:SUBLANES, :] = jnp.zeros((SUBLANES, CONV_W), F32)
        vbuf[SUBLANES:SUBLANES + tm, :] = v
        y = wc[0:1] * vbuf[pl.ds(SUBLANES - 2, tm), :] + wc[1:2] * vbuf[pl.ds(SUBLANES - 1, tm), :] + wc[2:3] * v
        tail = vbuf[tm:tm + SUBLANES, :]
        cst_ref[0] = tail
        vbuf[0:SUBLANES, :] = tail
    conv_ref[...] = b_g * y

    cos = cos_ref[...]
    sin_s = sin_ref[...]
    first = _first_half_mask(tm)

    zq = jnp.dot(hb, w_ref[:, _C_Q:_C_KVC], preferred_element_type=F32)
    for gq in range(ATTN_W // LANES):
        qr = _rope128(zq[:, gq * LANES:(gq + 1) * LANES], cos, sin_s, first) * Q_SCALE
        q_ref[0, 2 * gq] = qr[:, 0:HD].astype(BF16)
        q_ref[0, 2 * gq + 1] = qr[:, HD:LANES].astype(BF16)

    def store_rows(out_ref, halves):
        for j in range(2 * N_KV):
            piece = halves[j // N_KV]
            if j % N_KV == 1:
                piece = pltpu.roll(piece, HD, 1)
            ilbuf[pl.ds(j, tm, stride=2 * N_KV), :] = piece
        out_ref[...] = ilbuf[:, 0:HD]

    zkv = jnp.dot(hb, w_ref[:, _C_KVC:_C_G], preferred_element_type=F32)
    kvc_ref[...] = zkv[:, 0:2 * KV_W]
    store_rows(kvc_il_ref, (zkv[:, 0:KV_W], zkv[:, KV_W:2 * KV_W]))
    for base, kv_ref, kh_ref, vh_ref in ((2 * KV_W, kvs_ref, ks_ref, vs_ref), (4 * KV_W, kvw_ref, kw_ref, vw_ref)):
        kr = _rope128(zkv[:, base:base + KV_W], cos, sin_s, first)
        vv = zkv[:, base + KV_W:base + 2 * KV_W]
        store_rows(kv_ref, (kr, vv))
        lane = lax.broadcasted_iota(I32, (tm, LANES), 1)
        for k in range(N_KV):
            kh_ref[0, k] = kr[:, k * HD:(k + 1) * HD].astype(BF16)
            vk = vv if k == 0 else pltpu.roll(vv, HD, 1)
            vh_ref[0, k] = jnp.where(lane < HD, vk, jnp.where(lane == HD, 1.0, 0.0)).astype(BF16)

    zg = jnp.dot(hb, w_ref[:, _C_G:_C_END], preferred_element_type=F32)
    gate_ref[...] = jax.nn.sigmoid(zg)


def _proj(x2d, g1, sc, sh, w_pack, w_conv, cos_t, sin_t, *, nb, t, sample, prev=None):
    rows = nb * t
    tm = min(512, rows) if not sample else rows
    tpb = (t // tm) if not sample else 1
    n_tiles = rows // tm
    f = lambda a: jax.ShapeDtypeStruct(a, F32)
    b = lambda a: jax.ShapeDtypeStruct(a, BF16)
    if sample:
        mod_spec = pl.BlockSpec((1, tm, D_MODEL), lambda i: (0, 0, 0))
        tab_spec = pl.BlockSpec((1, LANES), lambda i: (0, 0))
        cst_shape, cst_spec = f((rows, CONV_W)), pl.BlockSpec((tm, CONV_W), lambda i: (0, 0))
        hm = lambda i: (0, 0, i, 0)
        hb_, ht_ = 1, rows
    else:
        mod_spec = pl.BlockSpec((1, 1, D_MODEL), lambda i: (i // tpb, 0, 0))
        tab_spec = pl.BlockSpec((tm, LANES), lambda i: (i % tpb, 0))
        cst_shape, cst_spec = f((nb, SUBLANES, CONV_W)), pl.BlockSpec((1, SUBLANES, CONV_W), lambda i: (i // tpb, 0, 0))
        hm = lambda i: (i // tpb, 0, i % tpb, 0)
        hb_, ht_ = nb, t
    row = lambda w: pl.BlockSpec((tm, w), lambda i: (i, 0))
    in_specs = [row(D_MODEL), pl.BlockSpec((1, D_MODEL), lambda i: (0, 0)), mod_spec, mod_spec,
                pl.BlockSpec((D_MODEL, _C_END), lambda i: (0, 0)),
                pl.BlockSpec((SUBLANES, CONV_W), lambda i: (0, 0)), tab_spec, tab_spec]
    args = [x2d, g1.reshape(1, D_MODEL), sc, sh, w_pack, w_conv, cos_t, sin_t]
    scratch = [pltpu.VMEM((2 * N_KV * tm, LANES), F32)]
    if sample:
        in_specs += [row(CONV_W), row(CONV_W)]
        args += [prev[0], prev[1]]
    else:
        scratch.append(pltpu.VMEM((tm + SUBLANES, CONV_W), F32))
    il_rows = 2 * N_KV * rows
    il = pl.BlockSpec((2 * N_KV * tm, HD), lambda i: (i, 0))
    out_shape = [f((rows, CONV_W)), cst_shape, b((hb_, N_HEADS, ht_, HD)),
                 f((rows, 2 * KV_W)), f((il_rows, HD)), f((il_rows, HD)), f((il_rows, HD)),
                 b((hb_, N_KV, ht_, HD)), b((hb_, N_KV, ht_, LANES)), b((hb_, N_KV, ht_, HD)), b((hb_, N_KV, ht_, LANES)),
                 f((rows, 2 * LANES))]
    out_specs = [row(CONV_W), cst_spec, pl.BlockSpec((1, N_HEADS, tm, HD), hm),
                 row(2 * KV_W), il, il, il,
                 pl.BlockSpec((1, N_KV, tm, HD), hm), pl.BlockSpec((1, N_KV, tm, LANES), hm),
                 pl.BlockSpec((1, N_KV, tm, HD), hm), pl.BlockSpec((1, N_KV, tm, LANES), hm),
                 row(2 * LANES)]
    return pl.pallas_call(
        functools.partial(_proj_kernel, tm=tm, tpb=tpb, sample=sample),
        grid=(n_tiles,), in_specs=in_specs, out_specs=out_specs, out_shape=out_shape,
        scratch_shapes=scratch, compiler_params=_params(1),
        name="proj_sample" if sample else "proj_prompt",
    )(*args)


def _cmpbias_kernel(pos_ref, w_ref, b1_ref, o_ref):
    for c in range(2):
        o_ref[c:c + 1, :] = jnp.sum(pos_ref[c] * w_ref[c], axis=0, keepdims=True) + b1_ref[c:c + 1, :]


def _cmpbias(cmp_pos, cmp_w1, cmp_b1):
    n = cmp_pos.shape[1] * cmp_pos.shape[2]
    return pl.pallas_call(
        _cmpbias_kernel,
        out_shape=jax.ShapeDtypeStruct((2, CMP_HID), F32),
        compiler_params=pltpu.CompilerParams(vmem_limit_bytes=VMEM_LIMIT),
        name="cmpbias",
    )(cmp_pos.reshape(2, n, 1), cmp_w1.reshape(2, n, CMP_HID), cmp_b1)


def _cmp_kernel(pt_ref, *refs, ppt, tiles):
    pages = refs[:ppt + 1]
    if tiles:
        unfold_ref = refs[ppt + 1]
        refs = refs[1:]
    w1_ref, b1_ref, w2_ref, b2_ref, cos_ref, sin_ref, ck_ref, cv_ref, lhs, pbuf = refs[ppt + 1:]
    r = ppt * SUBLANES
    first = _first_half_mask(r)
    for j in range(ppt + 1):
        rows = slice(j * SUBLANES, (j + 1) * SUBLANES)
        if tiles:
            a = pages[j][...].reshape(2 * KV_W, PAGE).astype(BF16)
            y = lax.dot_general(unfold_ref[...], a, _NT, preferred_element_type=F32)
            for c in range(2):
                for s in range(CMP_STRIDE):
                    lhs[c, rows, s * KV_W:(s + 1) * KV_W] = y[s * SUBLANES:(s + 1) * SUBLANES, c * KV_W:(c + 1) * KV_W]
        else:
            for c in range(2):
                for s in range(CMP_STRIDE):
                    src = slice(s * 2 * KV_W + c * KV_W, s * 2 * KV_W + (c + 1) * KV_W)
                    lhs[c, rows, s * KV_W:(s + 1) * KV_W] = pages[j][0, :, src]
    for c in range(2):
        p = jnp.dot(lhs[c].astype(BF16), w1_ref[c], preferred_element_type=F32)
        pbuf[...] = p[:, 2 * CMP_HID:4 * CMP_HID]
        hid = p[0:r, 0:2 * CMP_HID] + pbuf[pl.ds(1, r), :] + b1_ref[c]
        act = jax.nn.gelu(hid)
        comp = jnp.dot(act.astype(BF16), w2_ref[c], preferred_element_type=F32) + b2_ref[c]
        if c == 0:
            comp = _rope128(comp, cos_ref[...], sin_ref[...], first)
            out = ck_ref
        else:
            out = cv_ref
        for k in range(N_KV):
            out[0, k] = comp[:, k * HD:(k + 1) * HD]


def _cmp(pages, pt_flat, nb, n_pages, w1p, b1p, w2p, b2p, cos_c, sin_c, name, tiles):
    ppt = min(32, n_pages)
    n_tiles = n_pages // ppt
    r = ppt * SUBLANES
    n_chunk = n_pages * SUBLANES
    zeros = (0,) * (pages.ndim - 1)

    def page_map(j):
        return lambda b, t, pt: (pt[b * n_pages + t * ppt + j],) + zeros

    def next_map(b, t, pt):
        return (pt[b * n_pages + jnp.minimum(t * ppt + ppt, n_pages - 1)],) + zeros

    page_blk = (None, 2, N_KV, HD, PAGE) if tiles else (1, SUBLANES, CHUNK_ROW)
    in_specs = [pl.BlockSpec(page_blk, page_map(j)) for j in range(ppt)]
    in_specs.append(pl.BlockSpec(page_blk, next_map))
    const = lambda shp: pl.BlockSpec(shp, lambda b, t, pt: (0,) * len(shp))
    extra = []
    if tiles:
        row = np.arange(PAGE)
        tok = (row % SUBLANES) * CMP_STRIDE + row // SUBLANES
        extra = [jnp.asarray(tok[:, None] == np.arange(PAGE)[None, :], BF16)]
        in_specs.append(const((PAGE, PAGE)))
    in_specs += [const(w1p.shape), const(b1p.shape), const(w2p.shape), const(b2p.shape),
                 pl.BlockSpec((r, LANES), lambda b, t, pt: (t, 0)), pl.BlockSpec((r, LANES), lambda b, t, pt: (t, 0))]
    hm = pl.BlockSpec((1, N_KV, r, HD), lambda b, t, pt: (b, 0, t, 0))
    grid_spec = pltpu.PrefetchScalarGridSpec(
        num_scalar_prefetch=1, grid=(nb, n_tiles), in_specs=in_specs, out_specs=[hm, hm],
        scratch_shapes=[pltpu.VMEM((2, r + SUBLANES, CMP_STRIDE * KV_W), F32),
                        pltpu.VMEM((r + SUBLANES, 2 * CMP_HID), F32)])
    return pl.pallas_call(
        functools.partial(_cmp_kernel, ppt=ppt, tiles=tiles),
        grid_spec=grid_spec,
        out_shape=[jax.ShapeDtypeStruct((nb, N_KV, n_chunk, HD), F32)] * 2,
        compiler_params=_params(2), name=name,
    )(pt_flat, *([pages] * (ppt + 1)), *extra, w1p, b1p, w2p, b2p, cos_c, sin_c)


def _softmax_rows(s, valid):
    s = jnp.where(valid, s, NEG_INF)
    m = jnp.max(s, axis=-1, keepdims=True)
    e = jnp.exp2(s - m)
    return e / jnp.sum(e, axis=-1, keepdims=True)


def _attn_p_kernel(q_ref, ck_ref, cv_ref, ks_ref, vs_ref, kw_ref, vw_ref, gate_ref, band_ref, exp_ref, o_ref,
                   *, n_cmp_pad, n_blk, kc, hg, wc):
    qb = pl.program_id(2)
    start = qb * Q_BLOCK
    tpos = start + lax.broadcasted_iota(I32, (Q_BLOCK, 1), 0)
    groups = range(QPK // hg)
    rows = hg * Q_BLOCK

    def q_of(g):
        return q_ref[0, g * hg:(g + 1) * hg].reshape(rows, HD)

    def biased(s, bias):
        width = s.shape[-1]
        return (s.reshape(hg, Q_BLOCK, width) + bias[None]).reshape(rows, width)

    ck = ck_ref[0, 0].astype(BF16)
    cv = cv_ref[0, 0].astype(BF16)
    cmp_end = (lax.broadcasted_iota(I32, (1, n_cmp_pad), 1) + 2) * CMP_STRIDE - 1
    bias_c = jnp.where(cmp_end <= tpos, 0.0, NEG_INF)
    o_c = []
    pcs = jnp.zeros((Q_BLOCK, n_cmp_pad), F32)
    for g in groups:
        s_c = biased(lax.dot_general(q_of(g), ck, _NT, preferred_element_type=F32), bias_c)
        m_c = jnp.maximum(jnp.max(s_c, axis=-1, keepdims=True), 0.5 * NEG_INF)
        e_c = jnp.exp2(s_c - m_c)
        l_c = jnp.sum(e_c, axis=-1, keepdims=True)
        p_c = e_c * (1.0 / jnp.where(l_c > 0.0, l_c, 1.0))
        o_c.append(jnp.dot(p_c.astype(BF16), cv, preferred_element_type=F32))
        for h in range(hg):
            pcs = pcs + p_c[h * Q_BLOCK:(h + 1) * Q_BLOCK]

    imp =lax.dot_general(band_ref[...], pcs, _NT, preferred_element_type=F32,
                          precision=lax.Precision.HIGHEST)
    blk = lax.broadcasted_iota(I32, (n_blk, Q_BLOCK), 0)
    tlane = start + lax.broadcasted_iota(I32, (1, Q_BLOCK), 1)
    cur = tlane // SEL_BLOCK
    causal = blk * SEL_BLOCK <= tlane
    forced = causal & ((blk == 0) | (blk == cur) | (blk == cur - 1))
    score = jnp.where(forced, FORCE_SCORE, jnp.where(causal, imp, -1.0))
    rank = jnp.zeros((n_blk, Q_BLOCK), F32)
    for bp in range(n_blk):
        other = score[bp:bp + 1, :]
        beats = (other > score) | ((other == score) & (bp < blk))
        rank = rank + beats.astype(F32)
    sel_t = (rank < float(min(N_SEL, n_blk))).astype(BF16)
    eye = (lax.broadcasted_iota(I32, (Q_BLOCK, Q_BLOCK), 0)
           == lax.broadcasted_iota(I32, (Q_BLOCK, Q_BLOCK), 1)).astype(BF16)
    sel = lax.dot_general(eye, sel_t, _NT, preferred_element_type=F32).astype(BF16)

    n_chunks = (start + Q_BLOCK + kc - 1) // kc

    def online(state, kj, vj, bias):
        out = []
        for g in groups:
            m_i, acc = state[g]
            s = biased(lax.dot_general(q_of(g), kj, _NT, preferred_element_type=F32), bias)
            m_new = jnp.maximum(m_i, jnp.max(s, axis=-1, keepdims=True))
            p = jnp.exp2(s - m_new).astype(BF16)
            out.append((m_new, jnp.exp2(m_i - m_new) * acc + jnp.dot(p, vj, preferred_element_type=F32)))
        return tuple(out)

    def step(j, state, causal_chunk):
        off = pl.multiple_of(j * kc, kc)
        mexp = jnp.dot(sel, exp_ref[j], preferred_element_type=F32)
        bias = mexp * (-NEG_INF) + NEG_INF
        if causal_chunk:
            keypos = off + lax.broadcasted_iota(I32, (1, kc), 1)
            bias = jnp.where(keypos <= tpos, bias, NEG_INF)
        return online(state, ks_ref[0, 0, pl.ds(off, kc), :], vs_ref[0, 0, pl.ds(off, kc), :], bias)

    init = tuple((jnp.full((rows, 1), NEG_INF, F32), jnp.zeros((rows, LANES), F32)) for _ in groups)
    state = lax.fori_loop(0, n_chunks - 1, lambda j, c: step(j, c, False), init)
    sel_state = step(n_chunks - 1, state, True)

    s0 = jnp.maximum(start - WINDOW, 0)
    win_state = init
    for c in range((WINDOW + Q_BLOCK) // wc):
        off = pl.multiple_of(s0 + c * wc, Q_BLOCK)
        dist = tpos - (off + lax.broadcasted_iota(I32, (1, wc), 1))
        bias_w = jnp.where((dist >= 0) & (dist <= WINDOW), 0.0, NEG_INF)
        win_state = online(win_state, kw_ref[0, 0, pl.ds(off, wc), :], vw_ref[0, 0, pl.ds(off, wc), :], bias_w)

    gt = gate_ref[...]
    for g in groups:
        acc_s = sel_state[g][1]
        acc_w = win_state[g][1]
        o_s = acc_s[:, 0:HD] * (1.0 / acc_s[:, HD:HD + 1])
        o_w = acc_w[:, 0:HD] * (1.0 / acc_w[:, HD:HD + 1])
        for hh in range(hg):
            h = g * hg + hh
            rs = slice(hh * Q_BLOCK, (hh + 1) * Q_BLOCK)
            o = (gt[:, h:h + 1] * o_c[g][rs] + gt[:, QPK + h:QPK + h + 1] * o_s[rs]
                 + gt[:, 2 * QPK + h:2 * QPK + h + 1] * o_w[rs])
            o_ref[0, :, h * HD:(h + 1) * HD] = o


ATTN_HEAD_GROUP = 4
ATTN_KEY_CHUNK = 512
ATTN_WIN_CHUNK = 640


def _attn_prompt(q_hm, ck, cv, ks, vs, kw, vw, gates, band, expand, *, nb, t):
    n_qb = t // Q_BLOCK
    n_cmp_pad = ck.shape[2]
    n_blk = band.shape[0]
    kc = expand.shape[2]
    kv_spec = lambda n, w=HD: pl.BlockSpec((1, 1, n, w), lambda b, k, i: (b, k, 0, 0))
    return pl.pallas_call(
        functools.partial(_attn_p_kernel, n_cmp_pad=n_cmp_pad, n_blk=n_blk, kc=kc, hg=ATTN_HEAD_GROUP, wc=ATTN_WIN_CHUNK),
        grid=(nb, N_KV, n_qb),
        in_specs=[pl.BlockSpec((1, QPK, Q_BLOCK, HD), lambda b, k, i: (b, k, i, 0)),
                  kv_spec(n_cmp_pad), kv_spec(n_cmp_pad), kv_spec(t), kv_spec(t, LANES), kv_spec(t), kv_spec(t, LANES),
                  pl.BlockSpec((Q_BLOCK, LANES), lambda b, k, i: (b * n_qb + i, k)),
                  pl.BlockSpec(band.shape, lambda b, k, i: (0, 0)),
                  pl.BlockSpec(expand.shape, lambda b, k, i: (0, 0, 0))],
        out_specs=pl.BlockSpec((1, Q_BLOCK, QPK * HD), lambda b, k, i: (b, i, k)),
        out_shape=jax.ShapeDtypeStruct((nb, t, ATTN_W), F32),
        compiler_params=_params(3), name="attn_prompt",
    )(q_hm, ck, cv, ks, vs, kw, vw, gates, band, expand)


def _attn_s1_kernel(q_ref, ck_ref, cv_ref, band_ref, oc_ref, imp_ref, *, n_chunk, past):
    q = q_ref[0]
    q16 = jnp.concatenate([q, jnp.zeros_like(q)], axis=0).astype(BF16)
    cmp_end = (lax.broadcasted_iota(I32, (1, n_chunk), 1) + 2) * CMP_STRIDE - 1
    valid = cmp_end <= past
    head = lax.broadcasted_iota(I32, (2 * N_HEADS, 1), 0)
    oc = jnp.zeros((2 * N_HEADS, HD), F32)
    imps = []
    for k in range(N_KV):
        s = lax.dot_general(q16, ck_ref[0, k].astype(BF16), _NT, preferred_element_type=F32)
        p = _softmax_rows(s, valid) * valid.astype(F32)
        in_grp = (head >= k * QPK) & (head < (k + 1) * QPK)
        p = jnp.where(in_grp, p, 0.0)
        oc = oc + jnp.dot(p.astype(BF16), cv_ref[0, k].astype(BF16), preferred_element_type=F32)
        pcs = jnp.sum(p, axis=0, keepdims=True)
        pcs8 = jnp.broadcast_to(pcs, (SUBLANES, n_chunk))
        imps.append(jnp.dot(pcs8, band_ref[...], preferred_element_type=F32,
                            precision=lax.Precision.HIGHEST)[0:1])
    oc_ref[0] = oc[0:N_HEADS]
    imp_ref[0] = jnp.concatenate(imps + [jnp.zeros((SUBLANES - N_KV, imps[0].shape[1]), F32)], axis=0)


def _topk_s_kernel(imp_ref, idx_ref, *, n_sel_blocks, past):
    imp = imp_ref[...]
    rows, nbp = imp.shape
    blk = lax.broadcasted_iota(I32, (rows, nbp), 1)
    cur = past // SEL_BLOCK
    causal = blk * SEL_BLOCK <= past
    forced = causal & ((blk == 0) | (blk == cur) | (blk == cur - 1))
    score = jnp.where(forced, FORCE_SCORE, jnp.where(causal, imp, -1.0))
    score = jnp.where(blk < n_sel_blocks, score, -2.0)
    lane = lax.broadcasted_iota(I32, (rows, LANES), 1)
    out = jnp.zeros((rows, LANES), I32)
    for r in range(min(N_SEL, n_sel_blocks)):
        m = jnp.max(score, axis=-1, keepdims=True)
        pick = jnp.min(jnp.where(score == m, blk, nbp), axis=-1, keepdims=True)
        out = jnp.where(lane == r, pick, out)
        score = jnp.where(blk == pick, -3.0, score)
    idx_ref[...] = out


def _attn_s2_kernel(pt_ref, idx_ref, *refs, n_pages, past, n_sel_blocks):
    ktiles, vtiles = refs[:N_SEL], refs[N_SEL:2 * N_SEL]
    q_ref, oc_ref, kvs_ref, wk_ref, wv_ref, kvw_ref, gate_ref, o_ref, kbuf, vbuf = refs[2 * N_SEL:]
    b = pl.program_id(0)
    k = pl.program_id(1)
    q = q_ref[0]
    q16f = jnp.concatenate([q, jnp.zeros_like(q)], axis=0)
    q16 = q16f.astype(BF16)
    head = lax.broadcasted_iota(I32, (N_HEADS, 1), 0)
    nk = N_SEL * PAGE
    lane = lax.broadcasted_iota(I32, (1, nk), 1)
    slot = lane // PAGE
    new_blk = n_sel_blocks - 1
    wb = wk_ref.shape[-1]
    wpos = past - wb + lax.broadcasted_iota(I32, (1, wb), 1)
    wdist = past - wpos
    valid_w = (wdist >= 0) & (wdist <= WINDOW) & (wpos >= 0)

    def attend(s, valid, v_t, k_new, v_new):
        s_new = jnp.sum(q16f * k_new, axis=-1, keepdims=True)
        s = jnp.where(valid, s, NEG_INF)
        m = jnp.maximum(jnp.max(s, axis=-1, keepdims=True), s_new)
        e = jnp.exp2(s - m)
        e_new = jnp.exp2(s_new - m)
        den = jnp.sum(e, axis=-1, keepdims=True) + e_new
        acc = lax.dot_general(e.astype(BF16), v_t, _NT, preferred_element_type=F32) + e_new * v_new
        return acc / den

    in_grp = (head >= k * QPK) & (head < (k + 1) * QPK)
    bvec = jnp.zeros((1, nk), I32)
    for j in range(N_SEL):
        kbuf[:, j * PAGE:(j + 1) * PAGE] = ktiles[j][...].astype(BF16)
        vbuf[:, j * PAGE:(j + 1) * PAGE] = vtiles[j][...].astype(BF16)
        bvec = jnp.where(slot == j, idx_ref[(b * N_KV + k) * LANES + j], bvec)
    tok = (bvec // 2) * PAGE + lane % PAGE
    valid = (tok // SEL_BLOCK == bvec) & (bvec < new_blk) & (tok <= past)
    s = jnp.dot(q16, kbuf[...], preferred_element_type=F32)
    o_s = attend(s, valid, vbuf[...], kvs_ref[0, pl.ds(k, 1), :], kvs_ref[0, pl.ds(N_KV + k, 1), :])
    sw = jnp.dot(q16, wk_ref[...].astype(BF16), preferred_element_type=F32)
    o_w = attend(sw, valid_w, wv_ref[...].astype(BF16), kvw_ref[0, pl.ds(k, 1), :], kvw_ref[0, pl.ds(N_KV + k, 1), :])
    g = gate_ref[0]
    part = jnp.where(in_grp, g[:, 1:2] * o_s[0:N_HEADS] + g[:, 2:3] * o_w[0:N_HEADS], 0.0)

    @pl.when(k == 0)
    def _():
        o_ref[0] = g[:, 0:1] * oc_ref[0] + part

    @pl.when(k > 0)
    def _():
        o_ref[0] = o_ref[0] + part


def _attn_sample(q3, ck, cv, band_s, sel_t, pt_flat, kvs_rows, win_t, kvw_rows, gates_hm,
                 *, nb, n_pages, past, n_sel_blocks):
    n_chunk = ck.shape[2]
    nbp = band_s.shape[1]
    oc, imp = pl.pallas_call(
        functools.partial(_attn_s1_kernel, n_chunk=n_chunk, past=past),
        grid=(nb,),
        in_specs=[pl.BlockSpec((1, N_HEADS, HD), lambda b: (b, 0, 0)),
                  pl.BlockSpec((1, N_KV, n_chunk, HD), lambda b: (b, 0, 0, 0)),
                  pl.BlockSpec((1, N_KV, n_chunk, HD), lambda b: (b, 0, 0, 0)),
                  pl.BlockSpec(band_s.shape, lambda b: (0, 0))],
        out_specs=[pl.BlockSpec((1, N_HEADS, HD), lambda b: (b, 0, 0)),
                   pl.BlockSpec((1, SUBLANES, nbp), lambda b: (b, 0, 0))],
        out_shape=[jax.ShapeDtypeStruct((nb, N_HEADS, HD), F32), jax.ShapeDtypeStruct((nb, SUBLANES, nbp), F32)],
        compiler_params=_params(1), name="attn_sample_cmp",
    )(q3, ck, cv, band_s)
    imp2 = imp[:, 0:N_KV, :].reshape(nb * N_KV, nbp)
    idx = pl.pallas_call(
        functools.partial(_topk_s_kernel, n_sel_blocks=n_sel_blocks, past=past),
        out_shape=jax.ShapeDtypeStruct((nb * N_KV, LANES), I32),
        compiler_params=pltpu.CompilerParams(vmem_limit_bytes=VMEM_LIMIT), name="topk_sample",
    )(imp2)
    idx_flat = idx.reshape(-1)

    def tile_map(c, j):
        def f(b, k, pt, ix):
            bidx = ix[(b * N_KV + k) * LANES + j]
            return (pt[b * n_pages + jnp.minimum(bidx // 2, n_pages - 1)], c, k, 0, 0)
        return f

    tile = lambda c, j: pl.BlockSpec((None, None, None, HD, PAGE), tile_map(c, j))
    in_specs = [tile(0, j) for j in range(N_SEL)] + [tile(1, j) for j in range(N_SEL)]
    wb = win_t.shape[-1]
    per_b = lambda shp: pl.BlockSpec(shp, lambda b, k, pt, ix: (b, 0, 0))
    in_specs += [per_b((1, N_HEADS, HD)), per_b((1, N_HEADS, HD)), per_b((1, SUBLANES, HD)),
                 pl.BlockSpec((None, None, None, HD, wb), lambda b, k, pt, ix: (b, 0, k, 0, 0)),
                 pl.BlockSpec((None, None, None, HD, wb), lambda b, k, pt, ix: (b, 1, k, 0, 0)),
                 per_b((1, SUBLANES, HD)), per_b((1, N_HEADS, LANES))]
    grid_spec = pltpu.PrefetchScalarGridSpec(
        num_scalar_prefetch=2, grid=(nb, N_KV), in_specs=in_specs,
        out_specs=per_b((1, N_HEADS, HD)),
        scratch_shapes=[pltpu.VMEM((HD, N_SEL * PAGE), BF16), pltpu.VMEM((HD, N_SEL * PAGE), BF16)])
    return pl.pallas_call(
        functools.partial(_attn_s2_kernel, n_pages=n_pages, past=past, n_sel_blocks=n_sel_blocks),
        grid_spec=grid_spec,
        out_shape=jax.ShapeDtypeStruct((nb, N_HEADS, HD), F32),
        compiler_params=_params(2), name="attn_sample_sel",
    )(pt_flat, idx_flat, *([sel_t] * (2 * N_SEL)), q3, oc, kvs_rows, win_t, win_t, kvw_rows, gates_hm)


TOK_ROWS = D_MODEL // LANES


def _store_token_tiles(ref, x):
    n = x.shape[0]
    for j in range(TOK_ROWS):
        ref[pl.ds(j, n, stride=TOK_ROWS), :] = x[:, j * LANES:(j + 1) * LANES]


def _load_token_tiles(ref, lead, n):
    return jnp.concatenate([ref[lead + (pl.ds(j, n, stride=TOK_ROWS), slice(None))] for j in range(TOK_ROWS)], axis=1)


def _outp_kernel(xp_ref, convp_ref, attnp_ref, ga1p_ref, sc2p_ref, sh2p_ref,
                 xs_ref, convs_ref, attns_ref, ga1s_ref, sc2s_ref, sh2s_ref,
                 gc_ref, ga_ref, w_ref, g2_ref, wr_ref, x1_ref, hp_ref, lg_ref, *, n_prompt_tiles):
    is_p = pl.program_id(0) < n_prompt_tiles
    pick = lambda a, b: jnp.where(is_p, a, b)
    cn = _rms(pick(convp_ref[...], convs_ref[...]), gc_ref[...])
    an = _rms(pick(attnp_ref[...], attns_ref[...]), ga_ref[...])
    cat = jnp.concatenate([cn, an], axis=1).astype(BF16)
    y = jnp.dot(cat, w_ref[...], preferred_element_type=F32)
    x1 = pick(xp_ref[...], xs_ref[...]) + pick(ga1p_ref[0], ga1s_ref[0]) * y
    x1_ref[...] = x1
    hp = _rms(x1, g2_ref[...]) * (1.0 + pick(sc2p_ref[0], sc2s_ref[0])) + pick(sh2p_ref[0], sh2s_ref[0])
    _store_token_tiles(hp_ref, hp)
    lg_ref[...] = jnp.dot(hp, wr_ref[...], preferred_element_type=F32, precision=lax.Precision.HIGHEST)


TOKEN_TILE = 512


def _outp(prompt, sample, g_conv, g_attn, w_out_b, g2, w_route, *, tpb):
    tm = TOKEN_TILE
    n_p = prompt[0].shape[0] // tm
    total = (n_p + 1) * tm
    last = n_p - 1
    prow = lambda w: pl.BlockSpec((tm, w), lambda i: (jnp.minimum(i, last), 0))
    srow = lambda w: pl.BlockSpec((tm, w), lambda i: (0, 0))
    pmod = pl.BlockSpec((1, 1, D_MODEL), lambda i: (jnp.minimum(i, last) // tpb, 0, 0))
    smod = pl.BlockSpec((1, tm, D_MODEL), lambda i: (0, 0, 0))
    vec = lambda w: pl.BlockSpec((1, w), lambda i: (0, 0))
    row = lambda w: pl.BlockSpec((tm, w), lambda i: (i, 0))
    in_specs = [prow(D_MODEL), prow(CONV_W), prow(ATTN_W), pmod, pmod, pmod,
                srow(D_MODEL), srow(CONV_W), srow(ATTN_W), smod, smod, smod,
                vec(CONV_W), vec(ATTN_W), pl.BlockSpec((D_MODEL, D_MODEL), lambda i: (0, 0)), vec(D_MODEL),
                pl.BlockSpec((D_MODEL, LANES), lambda i: (0, 0))]
    return pl.pallas_call(
        functools.partial(_outp_kernel, n_prompt_tiles=n_p),
        grid=(n_p + 1,), in_specs=in_specs,
        out_specs=[row(D_MODEL), pl.BlockSpec((tm * TOK_ROWS, LANES), lambda i: (i, 0)), row(LANES)],
        out_shape=[jax.ShapeDtypeStruct((total, D_MODEL), F32), jax.ShapeDtypeStruct((total * TOK_ROWS, LANES), F32),
                   jax.ShapeDtypeStruct((total, LANES), F32)],
        compiler_params=_params(1), name="outp",
    )(*prompt, *sample, g_conv.reshape(1, -1), g_attn.reshape(1, -1), w_out_b, g2.reshape(1, -1), w_route)


def _route_kernel(lg_ref, bias_ref, tri_ref, o_ref, cnt_ref, carry, *, tm, n_valid):
    i = pl.program_id(0)

    @pl.when(i == 0)
    def _():
        carry[...] = jnp.zeros_like(carry)

    lane = lax.broadcasted_iota(I32, (tm, LANES), 1)
    rowid = i * tm + lax.broadcasted_iota(I32, (tm, 1), 0)
    live = rowid < n_valid
    lg = lg_ref[...] + bias_ref[...]
    is_g = lane < N_GROUPS
    lgg = jnp.where(is_g, lg, NEG_INF)
    gmax = jnp.max(lgg, axis=-1, keepdims=True)
    grp = jnp.min(jnp.where(is_g & (lgg == gmax), lane, LANES), axis=-1, keepdims=True)
    p_grp = 1.0 / jnp.sum(jnp.where(is_g, jnp.exp(lgg - gmax), 0.0), axis=-1, keepdims=True)
    eid = lane - N_GROUPS
    in_grp = (eid >= grp * EPG) & (eid < (grp + 1) * EPG)
    le = jnp.where(in_grp, lg, NEG_INF)
    v1 = jnp.max(le, axis=-1, keepdims=True)
    e1 = jnp.min(jnp.where(in_grp & (le == v1), eid, LANES), axis=-1, keepdims=True)
    le2 = jnp.where(eid == e1, NEG_INF, le)
    v2 = jnp.max(le2, axis=-1, keepdims=True)
    e2 = jnp.min(jnp.where(in_grp & (eid != e1) & (le2 == v2), eid, LANES), axis=-1, keepdims=True)
    ex2 = jnp.exp(v2 - v1)
    w1 = p_grp * (1.0 / (1.0 + ex2))
    w2 = p_grp * (ex2 / (1.0 + ex2))
    oh1 = ((lane == e1) & live).astype(F32)
    oh2 = ((lane == e2) & live).astype(F32)
    both = oh1 + oh2
    before = jnp.dot(tri_ref[...], both.astype(BF16), preferred_element_type=F32) + carry[0:1, :]
    r1 = jnp.sum(oh1 * before, axis=-1, keepdims=True)
    r2 = jnp.sum(oh2 * before, axis=-1, keepdims=True)
    carry[0:1, :] = carry[0:1, :] + jnp.sum(both, axis=0, keepdims=True)
    out = jnp.where(lane == 0, e1.astype(F32), 0.0)
    out = jnp.where(lane == 1, e2.astype(F32), out)
    out = jnp.where(lane == 2, w1, out)
    out = jnp.where(lane == 3, w2, out)
    out = jnp.where(lane == 4, r1, out)
    out = jnp.where(lane == 5, r2, out)
    o_ref[...] = out
    cnt_ref[...] = carry[...]


def _route(logits, bias_row, n_valid):
    total = logits.shape[0]
    tm = TOKEN_TILE
    n_tiles = total // tm
    tri =(np.arange(tm)[:, None] > np.arange(tm)[None, :]).astype(np.float32)
    return pl.pallas_call(
        functools.partial(_route_kernel, tm=tm, n_valid=n_valid),
        grid=(n_tiles,),
        in_specs=[pl.BlockSpec((tm, LANES), lambda i: (i, 0)), pl.BlockSpec((1, LANES), lambda i: (0, 0)),
                  pl.BlockSpec((tm, tm), lambda i: (0, 0))],
        out_specs=[pl.BlockSpec((tm, LANES), lambda i: (i, 0)), pl.BlockSpec((SUBLANES, LANES), lambda i: (0, 0))],
        out_shape=[jax.ShapeDtypeStruct((total, LANES), F32), jax.ShapeDtypeStruct((SUBLANES, LANES), F32)],
        scratch_shapes=[pltpu.VMEM((SUBLANES, LANES), F32)],
        compiler_params=_params(1), name="route",
    )(logits, bias_row, jnp.asarray(tri, BF16))


def _row_copy(src_hbm, row, dst, slot, r, sem):
    src = src_hbm.at[pl.ds(pl.multiple_of(row * TOK_ROWS, TOK_ROWS), TOK_ROWS), :]
    return pltpu.make_async_copy(src, dst.at[slot, pl.ds(r * TOK_ROWS, TOK_ROWS), :], sem.at[slot])


EXPERT_ROWS = 256


def _experts_kernel(blk_e_ref, tok_ref, x_hbm, wg_ref, wu_ref, wd_ref, o_ref, xbuf, sem, wg_b, wu_b, wd_b, *, n_blocks):
    i = pl.program_id(0)
    slot = i % 2

    def issue(blk, s):
        for r in range(EXPERT_ROWS):
            _row_copy(x_hbm, tok_ref[blk * EXPERT_ROWS + r], xbuf, s, r, sem).start()

    @pl.when(i == 0)
    def _():
        issue(0, 0)

    @pl.when(i + 1 < n_blocks)
    def _():
        issue(i + 1, 1 - slot)

    changed = jnp.logical_or(i == 0, blk_e_ref[i] != blk_e_ref[jnp.maximum(i - 1, 0)])

    @pl.when(changed)
    def _():
        wg_b[...] = wg_ref[0].astype(BF16)
        wu_b[...] = wu_ref[0].astype(BF16)
        wd_b[...] = wd_ref[0].astype(BF16)

    for r in range(EXPERT_ROWS):
        _row_copy(x_hbm, 0, xbuf, slot, r, sem).wait()
    x = _load_token_tiles(xbuf, (slot,), EXPERT_ROWS).astype(BF16)
    g = jnp.dot(x, wg_b[...], preferred_element_type=F32)
    u = jnp.dot(x, wu_b[...], preferred_element_type=F32)
    h = (g * jax.nn.sigmoid(g)) * u
    _store_token_tiles(o_ref, jnp.dot(h.astype(BF16), wd_b[...], preferred_element_type=F32))


def _experts(blk_e, slot_tok, hp_all, w_gate, w_up, w_down, n_blocks):
    grid_spec = pltpu.PrefetchScalarGridSpec(
        num_scalar_prefetch=2, grid=(n_blocks,),
        in_specs=[pl.BlockSpec(memory_space=pl.ANY),
                  pl.BlockSpec((1, D_MODEL, D_EXPERT), lambda i, be, st: (be[i], 0, 0)),
                  pl.BlockSpec((1, D_MODEL, D_EXPERT), lambda i, be, st: (be[i], 0, 0)),
                  pl.BlockSpec((1, D_EXPERT, D_MODEL), lambda i, be, st: (be[i], 0, 0))],
        out_specs=pl.BlockSpec((EXPERT_ROWS * TOK_ROWS, LANES), lambda i, be, st: (i, 0)),
        scratch_shapes=[pltpu.VMEM((2, EXPERT_ROWS * TOK_ROWS, LANES), F32), pltpu.SemaphoreType.DMA((2,)),
                        pltpu.VMEM((D_MODEL, D_EXPERT), BF16), pltpu.VMEM((D_MODEL, D_EXPERT), BF16),
                        pltpu.VMEM((D_EXPERT, D_MODEL), BF16)])
    return pl.pallas_call(
        functools.partial(_experts_kernel, n_blocks=n_blocks),
        grid_spec=grid_spec,
        out_shape=jax.ShapeDtypeStruct((n_blocks * EXPERT_ROWS * TOK_ROWS, LANES), F32),
        compiler_params=_params(1), name="experts",
    )(blk_e, slot_tok, hp_all, w_gate, w_up, w_down)


def _final_kernel(dest_ref, yb_hbm, x1_ref, wt_ref, gate2_ref, gf_ref, o_ref, ybuf, sem, *, tm, n_tiles, row0):
    i = pl.program_id(0)
    slot = i % 2

    def copy(tile, s, r, k):
        d = dest_ref[(row0 + tile * tm + r) * 2 + k]
        src = yb_hbm.at[pl.ds(pl.multiple_of(d * TOK_ROWS, TOK_ROWS), TOK_ROWS), :]
        return pltpu.make_async_copy(src, ybuf.at[s, k, pl.ds(r * TOK_ROWS, TOK_ROWS), :], sem.at[s])

    def issue(tile, s):
        for r in range(tm):
            for k in range(2):
                copy(tile, s, r, k).start()

    @pl.when(i == 0)
    def _():
        issue(0, 0)

    @pl.when(i + 1 < n_tiles)
    def _():
        issue(i + 1, 1 - slot)

    for r in range(tm):
        for k in range(2):
            pltpu.make_async_copy(yb_hbm.at[pl.ds(0, TOK_ROWS), :], ybuf.at[slot, k, pl.ds(r * TOK_ROWS, TOK_ROWS), :],
                                  sem.at[slot]).wait()
    wt = wt_ref[...]
    f = wt[:, 2:3] * _load_token_tiles(ybuf, (slot, 0), tm) + wt[:, 3:4] * _load_token_tiles(ybuf, (slot, 1), tm)
    x2 = x1_ref[...] + gate2_ref[0] * f
    o_ref[...] = _rms(x2, gf_ref[...])


def _final(dest_flat, yb, x1_all, route_rows, gate2, final_g, *, rows, tpb, per_row, row0):
    tm = min(256, rows)
    n_tiles = rows // tm
    blk0 = row0 // tm
    mod = (pl.BlockSpec((1, tm, D_MODEL), lambda i, d: (0, i, 0)) if per_row
           else pl.BlockSpec((1, 1, D_MODEL), lambda i, d: (i // tpb, 0, 0)))
    grid_spec = pltpu.PrefetchScalarGridSpec(
        num_scalar_prefetch=1, grid=(n_tiles,),
        in_specs=[pl.BlockSpec(memory_space=pl.ANY),
                  pl.BlockSpec((tm, D_MODEL), lambda i, d: (blk0 + i, 0)),
                  pl.BlockSpec((tm, LANES), lambda i, d: (blk0 + i, 0)),
                  mod, pl.BlockSpec((1, D_MODEL), lambda i, d: (0, 0))],
        out_specs=pl.BlockSpec((tm, D_MODEL), lambda i, d: (i, 0)),
        scratch_shapes=[pltpu.VMEM((2, 2, tm * TOK_ROWS, LANES), F32), pltpu.SemaphoreType.DMA((2,))])
    return pl.pallas_call(
        functools.partial(_final_kernel, tm=tm, n_tiles=n_tiles, row0=row0),
        grid_spec=grid_spec,
        out_shape=jax.ShapeDtypeStruct((rows, D_MODEL), F32),
        compiler_params=_params(1), name="final_sample" if per_row else "final_prompt",
    )(dest_flat, yb, x1_all, route_rows, gate2, final_g.reshape(1, -1))


def _rope_tables(pos):
    inv = ROPE_THETA ** (-jnp.arange(HALF, dtype=F32) / HALF)
    ang = pos.astype(F32)[:, None] * inv[None, :]
    cos = jnp.tile(jnp.cos(ang), (1, LANES // HALF))
    sin = jnp.sin(ang)
    sin_s = jnp.tile(jnp.concatenate([-sin, sin], axis=1), (1, LANES // HD))
    return cos, sin_s


def _pack_w_in(w_in):
    gl = w_in[:, _C_G:_C_G + 3 * N_HEADS].reshape(D_MODEL, 3, N_KV, QPK)
    gcols = []
    for k in range(N_KV):
        gk = gl[:, :, k, :].reshape(D_MODEL, 3 * QPK)
        gcols.append(jnp.pad(gk, ((0, 0), (0, LANES - 3 * QPK))))
    return jnp.concatenate([w_in[:, :_C_G]] + gcols, axis=1).astype(BF16)


def _pack_cmp_weights(cmp_w1, cmp_w2, bias, cmp_b2):
    w1 = cmp_w1.reshape(2, 2, CMP_STRIDE, HD, CMP_HID)
    eye = jnp.eye(N_KV, dtype=F32)
    w1p = jnp.einsum('crsdh,pk->cspdrkh', w1, eye).reshape(2, CMP_STRIDE * KV_W, 2 * N_KV * CMP_HID)
    w2p = jnp.einsum('chd,pk->cphkd', cmp_w2, eye).reshape(2, N_KV * CMP_HID, KV_W)
    b1p = jnp.tile(bias, (1, N_KV)).reshape(2, 1, N_KV * CMP_HID)
    b2p = jnp.tile(cmp_b2, (1, N_KV)).reshape(2, 1, KV_W)
    return w1p.astype(BF16), b1p, w2p.astype(BF16), b2p


def _band(n_cmp_pad, n_cmp, n_blk_pad, n_blk):
    n = np.arange(n_cmp_pad)[:, None]
    b = np.arange(n_blk_pad)[None, :]
    r = SEL_BLOCK // CMP_STRIDE
    m = (n >= r * b - 1) & (n <= r * b + r - 1) & (n < n_cmp) & (b < n_blk)
    return jnp.asarray(m.astype(np.float32))


def _expand(t, kc):
    n_chunks = t // kc
    key = np.arange(t).reshape(n_chunks, 1, kc)
    blk = np.arange(t // SEL_BLOCK).reshape(1, -1, 1)
    return jnp.asarray((key // SEL_BLOCK == blk).astype(np.float32), BF16)


def kernel(x_prompt, x_sample, c_prompt, c_sample, cache_cmp_kv, cache_sel_kv, cache_win_kv, state_conv, page_table,
           ln1_g, ln2_g, w_ada, b_ada, w_in, w_conv, cmp_pos, cmp_w1, cmp_b1, cmp_w2, cmp_b2, g_out_conv, g_out_attn,
           w_out, w_route_group, b_route_group, w_route_expert, b_route_expert, w_gate, w_up, w_down, final_g):
    depth = w_in.shape[0]
    assert depth == 1, "single-layer step"
    nb, t, _ = x_prompt.shape
    ns, ts, _ = x_sample.shape
    assert ts == 1 and t % 512 == 0 and t >= WINDOW + Q_BLOCK
    n_pool = cache_cmp_kv.shape[1]
    n_pages = page_table.shape[1]
    past = n_pages * PAGE
    wb = cache_win_kv.shape[2]
    assert wb == WINDOW
    l = 0

    n_c = nb + ns
    c_all = jnp.pad(jnp.concatenate([c_prompt, c_sample], axis=0), ((0, (-n_c) % SUBLANES), (0, 0)))
    mods = _ada(c_all, w_ada[l], b_ada[l])
    sh1, sc1, ga1, sh2, sc2, ga2 = [mods[:, j * D_MODEL:(j + 1) * D_MODEL] for j in range(6)]
    pr = lambda a: a[0:nb].reshape(nb, 1, D_MODEL)
    sr = lambda a: a[nb:nb + ns].reshape(1, ns, D_MODEL)

    w_pack = _pack_w_in(w_in[l])
    wconv8 = jnp.pad(w_conv[l], ((0, SUBLANES - CONV_K), (0, 0)))
    cos_p, sin_p = _rope_tables(jnp.arange(t, dtype=I32))
    cos_s, sin_s = _rope_tables(jnp.full((1,), past, I32))
    xp2 = x_prompt.reshape(nb * t, D_MODEL)
    xs2 = x_sample.reshape(ns, D_MODEL)
    (conv_p, cst_p, q_p, kvc_p, kvc_rows_p, kvs_rows_p, kvw_rows_p, ks_p, vs_p, kw_p, vw_p, gates_p) = _proj(
        xp2, ln1_g[l], pr(sc1), pr(sh1), w_pack, wconv8, cos_p, sin_p, nb=nb, t=t, sample=False)
    (conv_s, cst_s, q_s, _, kvc_rows_s, kvs_rows_s, kvw_rows_s, _, _, _, _, gates_s) = _proj(
        xs2, ln1_g[l], sr(sc1), sr(sh1), w_pack, wconv8, cos_s, sin_s, nb=ns, t=1, sample=True,
        prev=(state_conv[l][:, 0], state_conv[l][:, 1]))

    bias = _cmpbias(cmp_pos[l], cmp_w1[l], cmp_b1[l])
    w1p, b1p, w2p, b2p = _pack_cmp_weights(cmp_w1[l], cmp_w2[l], bias, cmp_b2[l])
    pp = t // PAGE
    cos_cp, sin_cp = _rope_tables((jnp.arange(t // CMP_STRIDE, dtype=I32) + 2) * CMP_STRIDE - 1)
    ck_p, cv_p = _cmp(kvc_p.reshape(nb * pp, SUBLANES, CHUNK_ROW), jnp.arange(nb * pp, dtype=I32), nb, pp,
                      w1p, b1p, w2p, b2p, cos_cp, sin_cp, "cmp_prompt", tiles=False)
    pt_flat = page_table.reshape(-1).astype(I32)
    cos_cs, sin_cs = _rope_tables((jnp.arange(past // CMP_STRIDE, dtype=I32) + 2) * CMP_STRIDE - 1)
    to_tiles = lambda a: a.transpose(0, 2, 3, 4, 1)
    ck_s, cv_s = _cmp(to_tiles(cache_cmp_kv[l]), pt_flat, ns, n_pages,
                      w1p, b1p, w2p, b2p, cos_cs, sin_cs, "cmp_sample", tiles=True)

    n_chunk_p = t // CMP_STRIDE
    n_blk_p = t // SEL_BLOCK
    band_p = _band(n_chunk_p, n_chunk_p - 1, n_blk_p, n_blk_p)
    attn_p = _attn_prompt(q_p, ck_p, cv_p, ks_p, vs_p, kw_p, vw_p, gates_p, band_p.T, _expand(t, ATTN_KEY_CHUNK), nb=nb, t=t)

    n_chunk_s = past // CMP_STRIDE
    n_sel_s = -(-(past + 1) // SEL_BLOCK)
    nbp = -(-n_sel_s // LANES) * LANES
    band_s = _band(n_chunk_s, (past + 1) // CMP_STRIDE - 1, nbp, n_sel_s)
    q3 = q_s.reshape(N_HEADS, ns, HD).transpose(1, 0, 2).astype(F32)
    gs = gates_s.reshape(ns, N_KV, LANES)[:, :, :3 * QPK].reshape(ns, N_KV, 3, QPK)
    gates_hm = jnp.pad(gs.transpose(0, 1, 3, 2).reshape(ns, N_HEADS, 3), ((0, 0), (0, 0), (0, LANES - 3)))
    rpt = 2 * N_KV
    new_rows = lambda a: jnp.pad(a.reshape(ns, rpt, HD), ((0, 0), (0, SUBLANES - rpt), (0, 0)))
    attn_s = _attn_sample(q3, ck_s, cv_s, band_s, to_tiles(cache_sel_kv[l]), pt_flat,
                          new_rows(kvs_rows_s), to_tiles(cache_win_kv[l]), new_rows(kvw_rows_s),
                          gates_hm, nb=ns, n_pages=n_pages, past=past, n_sel_blocks=n_sel_s).reshape(ns, ATTN_W)

    total = nb * t + ns
    w_out_b = w_out[l].astype(BF16)
    w_route = jnp.pad(jnp.concatenate([w_route_group[l], w_route_expert[l]], axis=1),
                      ((0, 0), (0, LANES - N_GROUPS - N_EXPERTS)))
    b_route = jnp.pad(jnp.concatenate([b_route_group[l], b_route_expert[l]]), (0, LANES - N_GROUPS - N_EXPERTS))
    tile_pad = lambda a: jnp.pad(a, ((0, TOKEN_TILE - ns), (0, 0)))
    smod = lambda a: tile_pad(a[nb:nb + ns]).reshape(1, TOKEN_TILE, D_MODEL)
    x1_all, hp_all, lg_all = _outp(
        (xp2, conv_p, attn_p.reshape(nb * t, ATTN_W), pr(ga1), pr(sc2), pr(sh2)),
        (tile_pad(xs2), tile_pad(conv_s), tile_pad(attn_s), smod(ga1), smod(sc2), smod(sh2)),
        g_out_conv[l], g_out_attn[l], w_out_b, ln2_g[l], w_route, tpb=t // TOKEN_TILE)

    route, counts = _route(lg_all, b_route.reshape(1, LANES), total)
    e = route[:total, 0:2].astype(I32)
    rank = route[:total, 4:6].astype(I32)
    cnt = counts[0, :N_EXPERTS].astype(I32)
    padded = (cnt + EXPERT_ROWS - 1) // EXPERT_ROWS * EXPERT_ROWS
    pad_end = jnp.cumsum(padded)
    pad_start = pad_end - padded
    m_slots = total * 2
    n_blocks = -(-(m_slots + N_EXPERTS * (EXPERT_ROWS - 1)) // EXPERT_ROWS)
    dest = jnp.clip(pad_start[e] + rank, 0, n_blocks * EXPERT_ROWS - 1)
    tok = jnp.broadcast_to(jnp.arange(total, dtype=I32)[:, None], (total, 2))
    slot_tok = jnp.zeros((n_blocks * EXPERT_ROWS,), I32).at[dest.reshape(-1)].set(
        tok.reshape(-1), unique_indices=True)
    blk_start = jnp.arange(n_blocks, dtype=I32) * EXPERT_ROWS
    blk_e = jnp.minimum(jnp.sum((pad_end[None, :] <= blk_start[:, None]).astype(I32), axis=1), N_EXPERTS - 1)

    yb = _experts(blk_e, slot_tok, hp_all, w_gate[l], w_up[l], w_down[l], n_blocks)
    dest_flat = dest.reshape(-1)
    y_p = _final(dest_flat, yb, x1_all, route, pr(ga2), final_g, rows=nb * t, tpb=t // 256, per_row=False, row0=0)
    y_s = _final(dest_flat, yb, x1_all, route, sr(ga2), final_g, rows=ns, tpb=1, per_row=True, row0=nb * t)

    kv_shape = (2, N_KV, HD)
    y_prompt = y_p.reshape(nb, t, D_MODEL)
    y_sample = y_s.reshape(ns, 1, D_MODEL)
    new_cmp_prompt = kvc_rows_p.reshape((1, nb, t) + kv_shape)
    new_cmp_sample = kvc_rows_s.reshape((1, ns, 1) + kv_shape)
    new_sel_prompt = kvs_rows_p.reshape((1, nb, t) + kv_shape)
    new_sel_sample = kvs_rows_s.reshape((1, ns, 1) + kv_shape)
    new_win_prompt = kvw_rows_p.reshape((nb, t) + kv_shape)[:, t - WINDOW:][None]
    new_win_sample = jnp.concatenate([cache_win_kv[l][:, 1:], kvw_rows_s.reshape((ns, 1) + kv_shape)], axis=1)[None]
    new_conv_prompt = cst_p[:, SUBLANES - (CONV_K - 1):][None]
    new_conv_sample = jnp.stack([state_conv[l][:, 1], cst_s], axis=1)[None]
    return (y_prompt, y_sample, new_cmp_prompt, new_cmp_sample, new_sel_prompt, new_sel_sample,
            new_win_prompt, new_win_sample, new_conv_prompt, new_conv_sample)
```

```python
import functools

import numpy as np
import jax
import jax.numpy as jnp
from jax import lax
from jax.experimental import pallas as pl
from jax.experimental.pallas import tpu as pltpu

F32 = jnp.float32
BF16 = jnp.bfloat16
I32 = jnp.int32

D_MODEL = 1024
CONV_W = 512
ATTN_W = 512
HD = 64
HALF = HD // 2
N_HEADS = 8
N_KV = 2
QPK = 4
KV_W = N_KV * HD
CONV_K = 3
PAGE = 128
CMP_STRIDE = 16
CMP_HID = 128
SEL_BLOCK = 64
N_SEL = 16
WINDOW = 512
Q_BLOCK = 128
ROPE_THETA = 10000.0
N_GROUPS = 4
EPG = 8
N_EXPERTS = 32
D_EXPERT = 512
MOE_BLOCK = 128
NORM_EPS = 1e-6
NEG_INF = -1e30
FORCE_SCORE = 1e4
LANES = 128
SUBLANES = 8
CHUNK_ROW = CMP_STRIDE * 2 * KV_W
VMEM_LIMIT = 56 * 1024 * 1024

_NT = (((1,), (1,)), ((), ()))
Q_SCALE = HD ** -0.5 * 1.4426950408889634


def _params(n_axes):
    return pltpu.CompilerParams(dimension_semantics=("arbitrary",) * n_axes,
                                vmem_limit_bytes=VMEM_LIMIT)


def _rms(x, g):
    return x * lax.rsqrt(jnp.mean(x * x, axis=-1, keepdims=True) + NORM_EPS) * g


def _rope128(x, cos, sin_signed, first_half):
    xr = jnp.where(first_half, pltpu.roll(x, LANES - HALF, 1), pltpu.roll(x, HALF, 1))
    return x * cos + xr * sin_signed


def _first_half_mask(rows):
    lane = lax.broadcasted_iota(I32, (rows, LANES), 1)
    return (lane % HD) < HALF


def _ada_kernel(c_ref, w_ref, b_ref, o_ref):
    c = c_ref[...]
    s = c * jax.nn.sigmoid(c)
    o_ref[...] = jnp.dot(s.astype(BF16), w_ref[...].astype(BF16), preferred_element_type=F32) + b_ref[...]


def _ada(c_all, w_ada, b_ada):
    m, d = c_all.shape
    n = w_ada.shape[1]
    tn = 1024
    return pl.pallas_call(
        _ada_kernel,
        grid=(n // tn,),
        in_specs=[pl.BlockSpec((m, d), lambda j: (0, 0)),
                  pl.BlockSpec((d, tn), lambda j: (0, j)),
                  pl.BlockSpec((1, tn), lambda j: (0, j))],
        out_specs=pl.BlockSpec((m, tn), lambda j: (0, j)),
        out_shape=jax.ShapeDtypeStruct((m, n), F32),
        compiler_params=_params(1),
        name="ada",
    )(c_all, w_ada, b_ada.reshape(1, n))


_C_B, _C_C, _C_U, _C_Q, _C_KVC, _C_KVS, _C_KVW, _C_G, _C_END = 0, 512, 1024, 1536, 2048, 2304, 2560, 2816, 3072


def _proj_kernel(*refs, tm, tpb, sample):
    if sample:
        (x_ref, g1_ref, sc_ref, sh_ref, w_ref, wc_ref, cos_ref, sin_ref, p0_ref, p1_ref,
         conv_ref, cst_ref, q_ref, kvc_ref, kvc_il_ref, kvs_ref, kvw_ref, ks_ref, vs_ref, kw_ref, vw_ref, gate_ref,
         ilbuf) = refs
        vbuf = None
    else:
        (x_ref, g1_ref, sc_ref, sh_ref, w_ref, wc_ref, cos_ref, sin_ref,
         conv_ref, cst_ref, q_ref, kvc_ref, kvc_il_ref, kvs_ref, kvw_ref, ks_ref, vs_ref, kw_ref, vw_ref, gate_ref,
         ilbuf, vbuf) = refs
    i = pl.program_id(0)
    x = x_ref[...]
    h = _rms(x, g1_ref[...]) * (1.0 + sc_ref[0]) + sh_ref[0]
    hb = h.astype(BF16)

    zc = jnp.dot(hb, w_ref[:, _C_B:_C_Q], preferred_element_type=F32)
    b_g = zc[:, 0:CONV_W]
    v = zc[:, CONV_W:2 * CONV_W] * zc[:, 2 * CONV_W:3 * CONV_W]
    wc = wc_ref[...]
    if sample:
        y = wc[0:1] * p0_ref[...] + wc[1:2] * p1_ref[...] + wc[2:3] * v
        cst_ref[...] = v
    else:
        @pl.when(i % tpb == 0)
        def _():
            vbuf[0:SUBLANES, :] = jnp.zeros((SUBLANES, CONV_W), F32)
        vbuf[SUBLANES:SUBLANES + tm, :] = v
        y = wc[0:1] * vbuf[pl.ds(SUBLANES - 2, tm), :] + wc[1:2] * vbuf[pl.ds(SUBLANES - 1, tm), :] + wc[2:3] * v
        tail = vbuf[tm:tm + SUBLANES, :]
        cst_ref[0] = tail
        vbuf[0:SUBLANES, :] = tail
    conv_ref[...] = b_g * y

    cos = cos_ref[...]
    sin_s = sin_ref[...]
    first = _first_half_mask(tm)

    zq = jnp.dot(hb, w_ref[:, _C_Q:_C_KVC], preferred_element_type=F32)
    for gq in range(ATTN_W // LANES):
        qr = _rope128(zq[:, gq * LANES:(gq + 1) * LANES], cos, sin_s, first) * Q_SCALE
        q_ref[0, 2 * gq] = qr[:, 0:HD].astype(BF16)
        q_ref[0, 2 * gq + 1] = qr[:, HD:LANES].astype(BF16)

    def store_rows(out_ref, halves):
        for j in range(2 * N_KV):
            piece = halves[j // N_KV]
            if j % N_KV == 1:
                piece = pltpu.roll(piece, HD, 1)
            ilbuf[pl.ds(j, tm, stride=2 * N_KV), :] = piece
        out_ref[...] = ilbuf[:, 0:HD]

    zkv = jnp.dot(hb, w_ref[:, _C_KVC:_C_G], preferred_element_type=F32)
    kvc_ref[...] = zkv[:, 0:2 * KV_W]
    store_rows(kvc_il_ref, (zkv[:, 0:KV_W], zkv[:, KV_W:2 * KV_W]))
    for base, kv_ref, kh_ref, vh_ref in ((2 * KV_W, kvs_ref, ks_ref, vs_ref), (4 * KV_W, kvw_ref, kw_ref, vw_ref)):
        kr = _rope128(zkv[:, base:base + KV_W], cos, sin_s, first)
        vv = zkv[:, base + KV_W:base + 2 * KV_W]
        store_rows(kv_ref, (kr, vv))
        lane = lax.broadcasted_iota(I32, (tm, LANES), 1)
        for k in range(N_KV):
            kh_ref[0, k] = kr[:, k * HD:(k + 1) * HD].astype(BF16)
            vk = vv if k == 0 else pltpu.roll(vv, HD, 1)
            vh_ref[0, k] = jnp.where(lane < HD, vk, jnp.where(lane == HD, 1.0, 0.0)).astype(BF16)

    zg = jnp.dot(hb, w_ref[:, _C_G:_C_END], preferred_element_type=F32)
    gate_ref[...] = jax.nn.sigmoid(zg)


def _proj(x2d, g1, sc, sh, w_pack, w_conv, cos_t, sin_t, *, nb, t, sample, prev=None):
    rows = nb * t
    tm = min(512, rows) if not sample else rows
    tpb = (t // tm) if not sample else 1
    n_tiles = rows // tm
    f = lambda a: jax.ShapeDtypeStruct(a, F32)
    b = lambda a: jax.ShapeDtypeStruct(a, BF16)
    if sample:
        mod_spec = pl.BlockSpec((1, tm, D_MODEL), lambda i: (0, 0, 0))
        tab_spec = pl.BlockSpec((1, LANES), lambda i: (0, 0))
        cst_shape, cst_spec = f((rows, CONV_W)), pl.BlockSpec((tm, CONV_W), lambda i: (0, 0))
        hm = lambda i: (0, 0, i, 0)
        hb_, ht_ = 1, rows
    else:
        mod_spec = pl.BlockSpec((1, 1, D_MODEL), lambda i: (i // tpb, 0, 0))
        tab_spec = pl.BlockSpec((tm, LANES), lambda i: (i % tpb, 0))
        cst_shape, cst_spec = f((nb, SUBLANES, CONV_W)), pl.BlockSpec((1, SUBLANES, CONV_W), lambda i: (i // tpb, 0, 0))
        hm = lambda i: (i // tpb, 0, i % tpb, 0)
        hb_, ht_ = nb, t
    row = lambda w: pl.BlockSpec((tm, w), lambda i: (i, 0))
    in_specs = [row(D_MODEL), pl.BlockSpec((1, D_MODEL), lambda i: (0, 0)), mod_spec, mod_spec,
                pl.BlockSpec((D_MODEL, _C_END), lambda i: (0, 0)),
                pl.BlockSpec((SUBLANES, CONV_W), lambda i: (0, 0)), tab_spec, tab_spec]
    args = [x2d, g1.reshape(1, D_MODEL), sc, sh, w_pack, w_conv, cos_t, sin_t]
    scratch = [pltpu.VMEM((2 * N_KV * tm, LANES), F32)]
    if sample:
        in_specs += [row(CONV_W), row(CONV_W)]
        args += [prev[0], prev[1]]
    else:
        scratch.append(pltpu.VMEM((tm + SUBLANES, CONV_W), F32))
    il_rows = 2 * N_KV * rows
    il = pl.BlockSpec((2 * N_KV * tm, HD), lambda i: (i, 0))
    out_shape = [f((rows, CONV_W)), cst_shape, b((hb_, N_HEADS, ht_, HD)),
                 f((rows, 2 * KV_W)), f((il_rows, HD)), f((il_rows, HD)), f((il_rows, HD)),
                 b((hb_, N_KV, ht_, HD)), b((hb_, N_KV, ht_, LANES)), b((hb_, N_KV, ht_, HD)), b((hb_, N_KV, ht_, LANES)),
                 f((rows, 2 * LANES))]
    out_specs = [row(CONV_W), cst_spec, pl.BlockSpec((1, N_HEADS, tm, HD), hm),
                 row(2 * KV_W), il, il, il,
                 pl.BlockSpec((1, N_KV, tm, HD), hm), pl.BlockSpec((1, N_KV, tm, LANES), hm),
                 pl.BlockSpec((1, N_KV, tm, HD), hm), pl.BlockSpec((1, N_KV, tm, LANES), hm),
                 row(2 * LANES)]
    return pl.pallas_call(
        functools.partial(_proj_kernel, tm=tm, tpb=tpb, sample=sample),
        grid=(n_tiles,), in_specs=in_specs, out_specs=out_specs, out_shape=out_shape,
        scratch_shapes=scratch, compiler_params=_params(1),
        name="proj_sample" if sample else "proj_prompt",
    )(*args)


def _cmpbias_kernel(pos_ref, w_ref, b1_ref, o_ref):
    for c in range(2):
        o_ref[c:c + 1, :] = jnp.sum(pos_ref[c] * w_ref[c], axis=0, keepdims=True) + b1_ref[c:c + 1, :]


def _cmpbias(cmp_pos, cmp_w1, cmp_b1):
    n = cmp_pos.shape[1] * cmp_pos.shape[2]
    return pl.pallas_call(
        _cmpbias_kernel,
        out_shape=jax.ShapeDtypeStruct((2, CMP_HID), F32),
        compiler_params=pltpu.CompilerParams(vmem_limit_bytes=VMEM_LIMIT),
        name="cmpbias",
    )(cmp_pos.reshape(2, n, 1), cmp_w1.reshape(2, n, CMP_HID), cmp_b1)


def _cmp_kernel(pt_ref, *refs, ppt, tiles):
    pages = refs[:ppt + 1]
    if tiles:
        unfold_ref = refs[ppt + 1]
        refs = refs[1:]
    w1_ref, b1_ref, w2_ref, b2_ref, cos_ref, sin_ref, ck_ref, cv_ref, lhs, pbuf = refs[ppt + 1:]
    r = ppt * SUBLANES
    first = _first_half_mask(r)
    if tiles:
        for j in range(ppt + 1):
            a = pages[j][...].reshape(2 * KV_W, PAGE).astype(BF16)
            y = lax.dot_general(unfold_ref[...], a, _NT, preferred_element_type=F32)
            for c in range(2):
                for s in range(CMP_STRIDE):
                    lhs[c, j * SUBLANES:(j + 1) * SUBLANES, s * KV_W:(s + 1) * KV_W] = (
                        y[s * SUBLANES:(s + 1) * SUBLANES, c * KV_W:(c + 1) * KV_W])
    else:
        for j in range(ppt + 1):
            for c in range(2):
                for s in range(CMP_STRIDE):
                    src = slice(s * 2 * KV_W + c * KV_W, s * 2 * KV_W + (c + 1) * KV_W)
                    lhs[c, j * SUBLANES:(j + 1) * SUBLANES, s * KV_W:(s + 1) * KV_W] = pages[j][0, :, src]
    for c in range(2):
        p = jnp.dot(lhs[c].astype(BF16), w1_ref[c], preferred_element_type=F32)
        pbuf[...] = p[:, 2 * CMP_HID:4 * CMP_HID]
        hid = p[0:r, 0:2 * CMP_HID] + pbuf[pl.ds(1, r), :] + b1_ref[c]
        act = jax.nn.gelu(hid)
        comp = jnp.dot(act.astype(BF16), w2_ref[c], preferred_element_type=F32) + b2_ref[c]
        if c == 0:
            comp = _rope128(comp, cos_ref[...], sin_ref[...], first)
            out = ck_ref
        else:
            out = cv_ref
        for k in range(N_KV):
            out[0, k] = comp[:, k * HD:(k + 1) * HD]


def _cmp(pages, pt_flat, nb, n_pages, w1p, b1p, w2p, b2p, cos_c, sin_c, name, tiles):
    ppt = min(32, n_pages)
    n_tiles = n_pages // ppt
    r = ppt * SUBLANES
    n_chunk = n_pages * SUBLANES
    zeros = (0,) * (pages.ndim - 1)

    def page_map(j):
        return lambda b, t, pt: (pt[b * n_pages + t * ppt + j],) + zeros

    def next_map(b, t, pt):
        return (pt[b * n_pages + jnp.minimum(t * ppt + ppt, n_pages - 1)],) + zeros

    page_blk = (None, 2, N_KV, HD, PAGE) if tiles else (1, SUBLANES, CHUNK_ROW)
    in_specs = [pl.BlockSpec(page_blk, page_map(j)) for j in range(ppt)]
    in_specs.append(pl.BlockSpec(page_blk, next_map))
    const = lambda shp: pl.BlockSpec(shp, lambda b, t, pt: (0,) * len(shp))
    extra = []
    if tiles:
        row = np.arange(PAGE)
        tok = (row % SUBLANES) * CMP_STRIDE + row // SUBLANES
        extra = [jnp.asarray(tok[:, None] == np.arange(PAGE)[None, :], BF16)]
        in_specs.append(const((PAGE, PAGE)))
    in_specs += [const(w1p.shape), const(b1p.shape), const(w2p.shape), const(b2p.shape),
                 pl.BlockSpec((r, LANES), lambda b, t, pt: (t, 0)), pl.BlockSpec((r, LANES), lambda b, t, pt: (t, 0))]
    hm = pl.BlockSpec((1, N_KV, r, HD), lambda b, t, pt: (b, 0, t, 0))
    grid_spec = pltpu.PrefetchScalarGridSpec(
        num_scalar_prefetch=1, grid=(nb, n_tiles), in_specs=in_specs, out_specs=[hm, hm],
        scratch_shapes=[pltpu.VMEM((2, r + SUBLANES, CMP_STRIDE * KV_W), F32),
                        pltpu.VMEM((r + SUBLANES, 2 * CMP_HID), F32)])
    return pl.pallas_call(
        functools.partial(_cmp_kernel, ppt=ppt, tiles=tiles),
        grid_spec=grid_spec,
        out_shape=[jax.ShapeDtypeStruct((nb, N_KV, n_chunk, HD), F32)] * 2,
        compiler_params=_params(2), name=name,
    )(pt_flat, *([pages] * (ppt + 1)), *extra, w1p, b1p, w2p, b2p, cos_c, sin_c)


def _softmax_rows(s, valid):
    s = jnp.where(valid, s, NEG_INF)
    m = jnp.max(s, axis=-1, keepdims=True)
    e = jnp.exp2(s - m)
    return e / jnp.sum(e, axis=-1, keepdims=True)


def _attn_p_kernel(q_ref, ck_ref, cv_ref, ks_ref, vs_ref, kw_ref, vw_ref, gate_ref, band_ref, exp_ref, o_ref,
                   *, n_cmp_pad, n_blk, kc, hg, wc):
    qb = pl.program_id(2)
    start = qb * Q_BLOCK
    tpos = start + lax.broadcasted_iota(I32, (Q_BLOCK, 1), 0)
    groups = range(QPK // hg)
    rows = hg * Q_BLOCK

    def q_of(g):
        return q_ref[0, g * hg:(g + 1) * hg].reshape(rows, HD)

    def biased(s, bias):
        width = s.shape[-1]
        return (s.reshape(hg, Q_BLOCK, width) + bias[None]).reshape(rows, width)

    ck = ck_ref[0, 0].astype(BF16)
    cv = cv_ref[0, 0].astype(BF16)
    cmp_end = (lax.broadcasted_iota(I32, (1, n_cmp_pad), 1) + 2) * CMP_STRIDE - 1
    bias_c = jnp.where(cmp_end <= tpos, 0.0, NEG_INF)
    o_c = []
    pcs = jnp.zeros((Q_BLOCK, n_cmp_pad), F32)
    for g in groups:
        s_c = biased(lax.dot_general(q_of(g), ck, _NT, preferred_element_type=F32), bias_c)
        m_c = jnp.maximum(jnp.max(s_c, axis=-1, keepdims=True), 0.5 * NEG_INF)
        e_c = jnp.exp2(s_c - m_c)
        l_c = jnp.sum(e_c, axis=-1, keepdims=True)
        p_c = e_c * (1.0 / jnp.where(l_c > 0.0, l_c, 1.0))
        o_c.append(jnp.dot(p_c.astype(BF16), cv, preferred_element_type=F32))
        for h in range(hg):
            pcs = pcs + p_c[h * Q_BLOCK:(h + 1) * Q_BLOCK]

    imp =lax.dot_general(band_ref[...], pcs, _NT, preferred_element_type=F32,
                          precision=lax.Precision.HIGHEST)
    blk = lax.broadcasted_iota(I32, (n_blk, Q_BLOCK), 0)
    tlane = start + lax.broadcasted_iota(I32, (1, Q_BLOCK), 1)
    cur = tlane // SEL_BLOCK
    causal = blk * SEL_BLOCK <= tlane
    forced = causal & ((blk == 0) | (blk == cur) | (blk == cur - 1))
    score = jnp.where(forced, FORCE_SCORE, jnp.where(causal, imp, -1.0))
    rank = jnp.zeros((n_blk, Q_BLOCK), F32)
    for bp in range(n_blk):
        other = score[bp:bp + 1, :]
        beats = (other > score) | ((other == score) & (bp < blk))
        rank = rank + beats.astype(F32)
    sel_t = (rank < float(min(N_SEL, n_blk))).astype(BF16)
    eye = (lax.broadcasted_iota(I32, (Q_BLOCK, Q_BLOCK), 0)
           == lax.broadcasted_iota(I32, (Q_BLOCK, Q_BLOCK), 1)).astype(BF16)
    sel = lax.dot_general(eye, sel_t, _NT, preferred_element_type=F32).astype(BF16)

    n_chunks = (start + Q_BLOCK + kc - 1) // kc

    def online(state, kj, vj, bias):
        out = []
        for g in groups:
            m_i, acc = state[g]
            s = biased(lax.dot_general(q_of(g), kj, _NT, preferred_element_type=F32), bias)
            m_new = jnp.maximum(m_i, jnp.max(s, axis=-1, keepdims=True))
            p = jnp.exp2(s - m_new).astype(BF16)
            out.append((m_new, jnp.exp2(m_i - m_new) * acc + jnp.dot(p, vj, preferred_element_type=F32)))
        return tuple(out)

    def step(j, state, causal_chunk):
        off = pl.multiple_of(j * kc, kc)
        mexp = jnp.dot(sel, exp_ref[j], preferred_element_type=F32)
        bias = mexp * (-NEG_INF) + NEG_INF
        if causal_chunk:
            keypos = off + lax.broadcasted_iota(I32, (1, kc), 1)
            bias = jnp.where(keypos <= tpos, bias, NEG_INF)
        return online(state, ks_ref[0, 0, pl.ds(off, kc), :], vs_ref[0, 0, pl.ds(off, kc), :], bias)

    init = tuple((jnp.full((rows, 1), NEG_INF, F32), jnp.zeros((rows, LANES), F32)) for _ in groups)
    state = lax.fori_loop(0, n_chunks - 1, lambda j, c: step(j, c, False), init)
    sel_state = step(n_chunks - 1, state, True)

    s0 = jnp.maximum(start - WINDOW, 0)
    win_state = init
    for c in range((WINDOW + Q_BLOCK) // wc):
        off = pl.multiple_of(s0 + c * wc, Q_BLOCK)
        dist = tpos - (off + lax.broadcasted_iota(I32, (1, wc), 1))
        bias_w = jnp.where((dist >= 0) & (dist <= WINDOW), 0.0, NEG_INF)
        win_state = online(win_state, kw_ref[0, 0, pl.ds(off, wc), :], vw_ref[0, 0, pl.ds(off, wc), :], bias_w)

    gt = gate_ref[...]
    for g in groups:
        acc_s = sel_state[g][1]
        acc_w = win_state[g][1]
        o_s = acc_s[:, 0:HD] * (1.0 / acc_s[:, HD:HD + 1])
        o_w = acc_w[:, 0:HD] * (1.0 / acc_w[:, HD:HD + 1])
        for hh in range(hg):
            h = g * hg + hh
            rs = slice(hh * Q_BLOCK, (hh + 1) * Q_BLOCK)
            o = (gt[:, h:h + 1] * o_c[g][rs] + gt[:, QPK + h:QPK + h + 1] * o_s[rs]
                 + gt[:, 2 * QPK + h:2 * QPK + h + 1] * o_w[rs])
            o_ref[0, :, h * HD:(h + 1) * HD] = o


ATTN_HEAD_GROUP = 4
ATTN_KEY_CHUNK = 512
ATTN_WIN_CHUNK = 640


def _attn_prompt(q_hm, ck, cv, ks, vs, kw, vw, gates, band, expand, *, nb, t):
    n_qb = t // Q_BLOCK
    n_cmp_pad = ck.shape[2]
    n_blk = band.shape[0]
    kc = expand.shape[2]
    kv_spec = lambda n, w=HD: pl.BlockSpec((1, 1, n, w), lambda b, k, i: (b, k, 0, 0))
    return pl.pallas_call(
        functools.partial(_attn_p_kernel, n_cmp_pad=n_cmp_pad, n_blk=n_blk, kc=kc, hg=ATTN_HEAD_GROUP, wc=ATTN_WIN_CHUNK),
        grid=(nb, N_KV, n_qb),
        in_specs=[pl.BlockSpec((1, QPK, Q_BLOCK, HD), lambda b, k, i: (b, k, i, 0)),
                  kv_spec(n_cmp_pad), kv_spec(n_cmp_pad), kv_spec(t), kv_spec(t, LANES), kv_spec(t), kv_spec(t, LANES),
                  pl.BlockSpec((Q_BLOCK, LANES), lambda b, k, i: (b * n_qb + i, k)),
                  pl.BlockSpec(band.shape, lambda b, k, i: (0, 0)),
                  pl.BlockSpec(expand.shape, lambda b, k, i: (0, 0, 0))],
        out_specs=pl.BlockSpec((1, Q_BLOCK, QPK * HD), lambda b, k, i: (b, i, k)),
        out_shape=jax.ShapeDtypeStruct((nb, t, ATTN_W), F32),
        compiler_params=_params(3), name="attn_prompt",
    )(q_hm, ck, cv, ks, vs, kw, vw, gates, band, expand)


def _attn_s1_kernel(q_ref, ck_ref, cv_ref, band_ref, oc_ref, imp_ref, *, n_chunk, past):
    q = q_ref[0]
    q16 = jnp.concatenate([q, jnp.zeros_like(q)], axis=0).astype(BF16)
    cmp_end = (lax.broadcasted_iota(I32, (1, n_chunk), 1) + 2) * CMP_STRIDE - 1
    valid = cmp_end <= past
    head = lax.broadcasted_iota(I32, (2 * N_HEADS, 1), 0)
    oc = jnp.zeros((2 * N_HEADS, HD), F32)
    imps = []
    for k in range(N_KV):
        s = lax.dot_general(q16, ck_ref[0, k].astype(BF16), _NT, preferred_element_type=F32)
        p = _softmax_rows(s, valid) * valid.astype(F32)
        in_grp = (head >= k * QPK) & (head < (k + 1) * QPK)
        p = jnp.where(in_grp, p, 0.0)
        oc = oc + jnp.dot(p.astype(BF16), cv_ref[0, k].astype(BF16), preferred_element_type=F32)
        pcs = jnp.sum(p, axis=0, keepdims=True)
        pcs8 = jnp.broadcast_to(pcs, (SUBLANES, n_chunk))
        imps.append(jnp.dot(pcs8, band_ref[...], preferred_element_type=F32,
                            precision=lax.Precision.HIGHEST)[0:1])
    oc_ref[0] = oc[0:N_HEADS]
    imp_ref[0] = jnp.concatenate(imps + [jnp.zeros((SUBLANES - N_KV, imps[0].shape[1]), F32)], axis=0)


def _topk_s_kernel(imp_ref, idx_ref, *, n_sel_blocks, past):
    imp = imp_ref[...]
    rows, nbp = imp.shape
    blk = lax.broadcasted_iota(I32, (rows, nbp), 1)
    cur = past // SEL_BLOCK
    causal = blk * SEL_BLOCK <= past
    forced = causal & ((blk == 0) | (blk == cur) | (blk == cur - 1))
    score = jnp.where(forced, FORCE_SCORE, jnp.where(causal, imp, -1.0))
    score = jnp.where(blk < n_sel_blocks, score, -2.0)
    lane = lax.broadcasted_iota(I32, (rows, LANES), 1)
    out = jnp.zeros((rows, LANES), I32)
    for r in range(min(N_SEL, n_sel_blocks)):
        m = jnp.max(score, axis=-1, keepdims=True)
        pick = jnp.min(jnp.where(score == m, blk, nbp), axis=-1, keepdims=True)
        out = jnp.where(lane == r, pick, out)
        score = jnp.where(blk == pick, -3.0, score)
    idx_ref[...] = out


def _attn_s2_kernel(pt_ref, idx_ref, *refs, n_pages, past, n_sel_blocks):
    ktiles, vtiles = refs[:N_SEL], refs[N_SEL:2 * N_SEL]
    q_ref, oc_ref, kvs_ref, wk_ref, wv_ref, kvw_ref, gate_ref, o_ref, kbuf, vbuf = refs[2 * N_SEL:]
    b = pl.program_id(0)
    k = pl.program_id(1)
    q = q_ref[0]
    q16f = jnp.concatenate([q, jnp.zeros_like(q)], axis=0)
    q16 = q16f.astype(BF16)
    head = lax.broadcasted_iota(I32, (N_HEADS, 1), 0)
    nk = N_SEL * PAGE
    lane = lax.broadcasted_iota(I32, (1, nk), 1)
    slot = lane // PAGE
    new_blk = n_sel_blocks - 1
    wb = wk_ref.shape[-1]
    wpos = past - wb + lax.broadcasted_iota(I32, (1, wb), 1)
    wdist = past - wpos
    valid_w = (wdist >= 0) & (wdist <= WINDOW) & (wpos >= 0)

    def attend(s, valid, v_t, k_new, v_new):
        s_new = jnp.sum(q16f * k_new, axis=-1, keepdims=True)
        s = jnp.where(valid, s, NEG_INF)
        m = jnp.maximum(jnp.max(s, axis=-1, keepdims=True), s_new)
        e = jnp.exp2(s - m)
        e_new = jnp.exp2(s_new - m)
        den = jnp.sum(e, axis=-1, keepdims=True) + e_new
        acc = lax.dot_general(e.astype(BF16), v_t, _NT, preferred_element_type=F32) + e_new * v_new
        return acc / den

    in_grp = (head >= k * QPK) & (head < (k + 1) * QPK)
    bvec = jnp.zeros((1, nk), I32)
    for j in range(N_SEL):
        kbuf[:, j * PAGE:(j + 1) * PAGE] = ktiles[j][...].astype(BF16)
        vbuf[:, j * PAGE:(j + 1) * PAGE] = vtiles[j][...].astype(BF16)
        bvec = jnp.where(slot == j, idx_ref[(b * N_KV + k) * LANES + j], bvec)
    tok = (bvec // 2) * PAGE + lane % PAGE
    valid = (tok // SEL_BLOCK == bvec) & (bvec < new_blk) & (tok <= past)
    s = jnp.dot(q16, kbuf[...], preferred_element_type=F32)
    o_s = attend(s, valid, vbuf[...], kvs_ref[0, pl.ds(k, 1), :], kvs_ref[0, pl.ds(N_KV + k, 1), :])
    sw = jnp.dot(q16, wk_ref[...].astype(BF16), preferred_element_type=F32)
    o_w = attend(sw, valid_w, wv_ref[...].astype(BF16), kvw_ref[0, pl.ds(k, 1), :], kvw_ref[0, pl.ds(N_KV + k, 1), :])
    g = gate_ref[0]
    part = jnp.where(in_grp, g[:, 1:2] * o_s[0:N_HEADS] + g[:, 2:3] * o_w[0:N_HEADS], 0.0)

    @pl.when(k == 0)
    def _():
        o_ref[0] = g[:, 0:1] * oc_ref[0] + part

    @pl.when(k > 0)
    def _():
        o_ref[0] = o_ref[0] + part


def _attn_sample(q3, ck, cv, band_s, sel_t, pt_flat, kvs_rows, win_t, kvw_rows, gates_hm,
                 *, nb, n_pages, past, n_sel_blocks):
    n_chunk = ck.shape[2]
    nbp = band_s.shape[1]
    oc, imp = pl.pallas_call(
        functools.partial(_attn_s1_kernel, n_chunk=n_chunk, past=past),
        grid=(nb,),
        in_specs=[pl.BlockSpec((1, N_HEADS, HD), lambda b: (b, 0, 0)),
                  pl.BlockSpec((1, N_KV, n_chunk, HD), lambda b: (b, 0, 0, 0)),
                  pl.BlockSpec((1, N_KV, n_chunk, HD), lambda b: (b, 0, 0, 0)),
                  pl.BlockSpec(band_s.shape, lambda b: (0, 0))],
        out_specs=[pl.BlockSpec((1, N_HEADS, HD), lambda b: (b, 0, 0)),
                   pl.BlockSpec((1, SUBLANES, nbp), lambda b: (b, 0, 0))],
        out_shape=[jax.ShapeDtypeStruct((nb, N_HEADS, HD), F32), jax.ShapeDtypeStruct((nb, SUBLANES, nbp), F32)],
        compiler_params=_params(1), name="attn_sample_cmp",
    )(q3, ck, cv, band_s)
    imp2 = imp[:, 0:N_KV, :].reshape(nb * N_KV, nbp)
    idx = pl.pallas_call(
        functools.partial(_topk_s_kernel, n_sel_blocks=n_sel_blocks, past=past),
        out_shape=jax.ShapeDtypeStruct((nb * N_KV, LANES), I32),
        compiler_params=pltpu.CompilerParams(vmem_limit_bytes=VMEM_LIMIT), name="topk_sample",
    )(imp2)
    idx_flat = idx.reshape(-1)

    def tile_map(c, j):
        def f(b, k, pt, ix):
            bidx = ix[(b * N_KV + k) * LANES + j]
            return (pt[b * n_pages + jnp.minimum(bidx // 2, n_pages - 1)], c, k, 0, 0)
        return f

    tile = lambda c, j: pl.BlockSpec((None, None, None, HD, PAGE), tile_map(c, j))
    in_specs = [tile(0, j) for j in range(N_SEL)] + [tile(1, j) for j in range(N_SEL)]
    wb = win_t.shape[-1]
    per_b = lambda shp: pl.BlockSpec(shp, lambda b, k, pt, ix: (b, 0, 0))
    in_specs += [per_b((1, N_HEADS, HD)), per_b((1, N_HEADS, HD)), per_b((1, SUBLANES, HD)),
                 pl.BlockSpec((None, None, None, HD, wb), lambda b, k, pt, ix: (b, 0, k, 0, 0)),
                 pl.BlockSpec((None, None, None, HD, wb), lambda b, k, pt, ix: (b, 1, k, 0, 0)),
                 per_b((1, SUBLANES, HD)), per_b((1, N_HEADS, LANES))]
    grid_spec = pltpu.PrefetchScalarGridSpec(
        num_scalar_prefetch=2, grid=(nb, N_KV), in_specs=in_specs,
        out_specs=per_b((1, N_HEADS, HD)),
        scratch_shapes=[pltpu.VMEM((HD, N_SEL * PAGE), BF16), pltpu.VMEM((HD, N_SEL * PAGE), BF16)])
    return pl.pallas_call(
        functools.partial(_attn_s2_kernel, n_pages=n_pages, past=past, n_sel_blocks=n_sel_blocks),
        grid_spec=grid_spec,
        out_shape=jax.ShapeDtypeStruct((nb, N_HEADS, HD), F32),
        compiler_params=_params(2), name="attn_sample_sel",
    )(pt_flat, idx_flat, *([sel_t] * (2 * N_SEL)), q3, oc, kvs_rows, win_t, win_t, kvw_rows, gates_hm)


TOK_ROWS = D_MODEL // LANES


def _store_token_tiles(ref, x):
    n = x.shape[0]
    for j in range(TOK_ROWS):
        ref[pl.ds(j, n, stride=TOK_ROWS), :] = x[:, j * LANES:(j + 1) * LANES]


def _load_token_tiles(ref, lead, n):
    return jnp.concatenate([ref[lead + (pl.ds(j, n, stride=TOK_ROWS), slice(None))] for j in range(TOK_ROWS)], axis=1)


def _outp_kernel(xp_ref, convp_ref, attnp_ref, ga1p_ref, sc2p_ref, sh2p_ref,
                 xs_ref, convs_ref, attns_ref, ga1s_ref, sc2s_ref, sh2s_ref,
                 gc_ref, ga_ref, w_ref, g2_ref, wr_ref, x1_ref, hp_ref, lg_ref, *, n_prompt_tiles):
    is_p = pl.program_id(0) < n_prompt_tiles
    pick = lambda a, b: jnp.where(is_p, a, b)
    cn = _rms(pick(convp_ref[...], convs_ref[...]), gc_ref[...])
    an = _rms(pick(attnp_ref[...], attns_ref[...]), ga_ref[...])
    cat = jnp.concatenate([cn, an], axis=1).astype(BF16)
    y = jnp.dot(cat, w_ref[...], preferred_element_type=F32)
    x1 = pick(xp_ref[...], xs_ref[...]) + pick(ga1p_ref[0], ga1s_ref[0]) * y
    x1_ref[...] = x1
    hp = _rms(x1, g2_ref[...]) * (1.0 + pick(sc2p_ref[0], sc2s_ref[0])) + pick(sh2p_ref[0], sh2s_ref[0])
    _store_token_tiles(hp_ref, hp)
    lg_ref[...] = jnp.dot(hp, wr_ref[...], preferred_element_type=F32, precision=lax.Precision.HIGHEST)


TOKEN_TILE = 512


def _outp(prompt, sample, g_conv, g_attn, w_out_b, g2, w_route, *, tpb):
    tm = TOKEN_TILE
    n_p = prompt[0].shape[0] // tm
    total = (n_p + 1) * tm
    last = n_p - 1
    prow = lambda w: pl.BlockSpec((tm, w), lambda i: (jnp.minimum(i, last), 0))
    srow = lambda w: pl.BlockSpec((tm, w), lambda i: (0, 0))
    pmod = pl.BlockSpec((1, 1, D_MODEL), lambda i: (jnp.minimum(i, last) // tpb, 0, 0))
    smod = pl.BlockSpec((1, tm, D_MODEL), lambda i: (0, 0, 0))
    vec = lambda w: pl.BlockSpec((1, w), lambda i: (0, 0))
    row = lambda w: pl.BlockSpec((tm, w), lambda i: (i, 0))
    in_specs = [prow(D_MODEL), prow(CONV_W), prow(ATTN_W), pmod, pmod, pmod,
                srow(D_MODEL), srow(CONV_W), srow(ATTN_W), smod, smod, smod,
                vec(CONV_W), vec(ATTN_W), pl.BlockSpec((D_MODEL, D_MODEL), lambda i: (0, 0)), vec(D_MODEL),
                pl.BlockSpec((D_MODEL, LANES), lambda i: (0, 0))]
    return pl.pallas_call(
        functools.partial(_outp_kernel, n_prompt_tiles=n_p),
        grid=(n_p + 1,), in_specs=in_specs,
        out_specs=[row(D_MODEL), pl.BlockSpec((tm * TOK_ROWS, LANES), lambda i: (i, 0)), row(LANES)],
        out_shape=[jax.ShapeDtypeStruct((total, D_MODEL), F32), jax.ShapeDtypeStruct((total * TOK_ROWS, LANES), F32),
                   jax.ShapeDtypeStruct((total, LANES), F32)],
        compiler_params=_params(1), name="outp",
    )(*prompt, *sample, g_conv.reshape(1, -1), g_attn.reshape(1, -1), w_out_b, g2.reshape(1, -1), w_route)


def _route_kernel(lg_ref, bias_ref, tri_ref, o_ref, cnt_ref, carry, *, tm, n_valid):
    i = pl.program_id(0)

    @pl.when(i == 0)
    def _():
        carry[...] = jnp.zeros_like(carry)

    lane = lax.broadcasted_iota(I32, (tm, LANES), 1)
    rowid = i * tm + lax.broadcasted_iota(I32, (tm, 1), 0)
    live = rowid < n_valid
    lg = lg_ref[...] + bias_ref[...]
    is_g = lane < N_GROUPS
    lgg = jnp.where(is_g, lg, NEG_INF)
    gmax = jnp.max(lgg, axis=-1, keepdims=True)
    grp = jnp.min(jnp.where(is_g & (lgg == gmax), lane, LANES), axis=-1, keepdims=True)
    p_grp = 1.0 / jnp.sum(jnp.where(is_g, jnp.exp(lgg - gmax), 0.0), axis=-1, keepdims=True)
    eid = lane - N_GROUPS
    in_grp = (eid >= grp * EPG) & (eid < (grp + 1) * EPG)
    le = jnp.where(in_grp, lg, NEG_INF)
    v1 = jnp.max(le, axis=-1, keepdims=True)
    e1 = jnp.min(jnp.where(in_grp & (le == v1), eid, LANES), axis=-1, keepdims=True)
    le2 = jnp.where(eid == e1, NEG_INF, le)
    v2 = jnp.max(le2, axis=-1, keepdims=True)
    e2 = jnp.min(jnp.where(in_grp & (eid != e1) & (le2 == v2), eid, LANES), axis=-1, keepdims=True)
    ex2 = jnp.exp(v2 - v1)
    w1 = p_grp * (1.0 / (1.0 + ex2))
    w2 = p_grp * (ex2 / (1.0 + ex2))
    oh1 = ((lane == e1) & live).astype(F32)
    oh2 = ((lane == e2) & live).astype(F32)
    both = oh1 + oh2
    before = jnp.dot(tri_ref[...], both.astype(BF16), preferred_element_type=F32) + carry[0:1, :]
    r1 = jnp.sum(oh1 * before, axis=-1, keepdims=True)
    r2 = jnp.sum(oh2 * before, axis=-1, keepdims=True)
    carry[0:1, :] = carry[0:1, :] + jnp.sum(both, axis=0, keepdims=True)
    out = jnp.where(lane == 0, e1.astype(F32), 0.0)
    out = jnp.where(lane == 1, e2.astype(F32), out)
    out = jnp.where(lane == 2, w1, out)
    out = jnp.where(lane == 3, w2, out)
    out = jnp.where(lane == 4, r1, out)
    out = jnp.where(lane == 5, r2, out)
    o_ref[...] = out
    cnt_ref[...] = carry[...]


def _route(logits, bias_row, n_valid):
    total = logits.shape[0]
    tm = TOKEN_TILE
    n_tiles = total // tm
    tri =(np.arange(tm)[:, None] > np.arange(tm)[None, :]).astype(np.float32)
    return pl.pallas_call(
        functools.partial(_route_kernel, tm=tm, n_valid=n_valid),
        grid=(n_tiles,),
        in_specs=[pl.BlockSpec((tm, LANES), lambda i: (i, 0)), pl.BlockSpec((1, LANES), lambda i: (0, 0)),
                  pl.BlockSpec((tm, tm), lambda i: (0, 0))],
        out_specs=[pl.BlockSpec((tm, LANES), lambda i: (i, 0)), pl.BlockSpec((SUBLANES, LANES), lambda i: (0, 0))],
        out_shape=[jax.ShapeDtypeStruct((total, LANES), F32), jax.ShapeDtypeStruct((SUBLANES, LANES), F32)],
        scratch_shapes=[pltpu.VMEM((SUBLANES, LANES), F32)],
        compiler_params=_params(1), name="route",
    )(logits, bias_row, jnp.asarray(tri, BF16))


def _row_copy(src_hbm, row, dst, slot, r, sem):
    src = src_hbm.at[pl.ds(pl.multiple_of(row * TOK_ROWS, TOK_ROWS), TOK_ROWS), :]
    return pltpu.make_async_copy(src, dst.at[slot, pl.ds(r * TOK_ROWS, TOK_ROWS), :], sem.at[slot])


EXPERT_ROWS = 256


def _experts_kernel(blk_e_ref, tok_ref, x_hbm, wg_ref, wu_ref, wd_ref, o_ref, xbuf, sem, wg_b, wu_b, wd_b, *, n_blocks):
    i = pl.program_id(0)
    slot = i % 2

    def issue(blk, s):
        for r in range(EXPERT_ROWS):
            _row_copy(x_hbm, tok_ref[blk * EXPERT_ROWS + r], xbuf, s, r, sem).start(priority=1)

    @pl.when(i == 0)
    def _():
        issue(0, 0)

    @pl.when(i + 1 < n_blocks)
    def _():
        issue(i + 1, 1 - slot)

    changed = jnp.logical_or(i == 0, blk_e_ref[i] != blk_e_ref[jnp.maximum(i - 1, 0)])

    @pl.when(changed)
    def _():
        wg_b[...] = wg_ref[0].astype(BF16)
        wu_b[...] = wu_ref[0].astype(BF16)
        wd_b[...] = wd_ref[0].astype(BF16)

    for r in range(EXPERT_ROWS):
        _row_copy(x_hbm, 0, xbuf, slot, r, sem).wait()
    x = _load_token_tiles(xbuf, (slot,), EXPERT_ROWS).astype(BF16)
    g = jnp.dot(x, wg_b[...], preferred_element_type=F32)
    u = jnp.dot(x, wu_b[...], preferred_element_type=F32)
    h = (g * jax.nn.sigmoid(g)) * u
    _store_token_tiles(o_ref, jnp.dot(h.astype(BF16), wd_b[...], preferred_element_type=F32))


def _experts(blk_e, slot_tok, hp_all, w_gate, w_up, w_down, n_blocks):
    grid_spec = pltpu.PrefetchScalarGridSpec(
        num_scalar_prefetch=2, grid=(n_blocks,),
        in_specs=[pl.BlockSpec(memory_space=pl.ANY),
                  pl.BlockSpec((1, D_MODEL, D_EXPERT), lambda i, be, st: (be[i], 0, 0)),
                  pl.BlockSpec((1, D_MODEL, D_EXPERT), lambda i, be, st: (be[i], 0, 0)),
                  pl.BlockSpec((1, D_EXPERT, D_MODEL), lambda i, be, st: (be[i], 0, 0))],
        out_specs=pl.BlockSpec((EXPERT_ROWS * TOK_ROWS, LANES), lambda i, be, st: (i, 0)),
        scratch_shapes=[pltpu.VMEM((2, EXPERT_ROWS * TOK_ROWS, LANES), F32), pltpu.SemaphoreType.DMA((2,)),
                        pltpu.VMEM((D_MODEL, D_EXPERT), BF16), pltpu.VMEM((D_MODEL, D_EXPERT), BF16),
                        pltpu.VMEM((D_EXPERT, D_MODEL), BF16)])
    return pl.pallas_call(
        functools.partial(_experts_kernel, n_blocks=n_blocks),
        grid_spec=grid_spec,
        out_shape=jax.ShapeDtypeStruct((n_blocks * EXPERT_ROWS * TOK_ROWS, LANES), F32),
        compiler_params=_params(1), name="experts",
    )(blk_e, slot_tok, hp_all, w_gate, w_up, w_down)


def _final_kernel(dest_ref, yb_hbm, x1_ref, wt_ref, gate2_ref, gf_ref, o_ref, ybuf, sem, *, tm, n_tiles, row0):
    i = pl.program_id(0)
    slot = i % 2

    def copy(tile, s, r, k):
        d = dest_ref[(row0 + tile * tm + r) * 2 + k]
        src = yb_hbm.at[pl.ds(pl.multiple_of(d * TOK_ROWS, TOK_ROWS), TOK_ROWS), :]
        return pltpu.make_async_copy(src, ybuf.at[s, k, pl.ds(r * TOK_ROWS, TOK_ROWS), :], sem.at[s])

    def issue(tile, s):
        for r in range(tm):
            for k in range(2):
                copy(tile, s, r, k).start()

    @pl.when(i == 0)
    def _():
        issue(0, 0)

    @pl.when(i + 1 < n_tiles)
    def _():
        issue(i + 1, 1 - slot)

    for r in range(tm):
        for k in range(2):
            pltpu.make_async_copy(yb_hbm.at[pl.ds(0, TOK_ROWS), :], ybuf.at[slot, k, pl.ds(r * TOK_ROWS, TOK_ROWS), :],
                                  sem.at[slot]).wait()
    wt = wt_ref[...]
    f = wt[:, 2:3] * _load_token_tiles(ybuf, (slot, 0), tm) + wt[:, 3:4] * _load_token_tiles(ybuf, (slot, 1), tm)
    x2 = x1_ref[...] + gate2_ref[0] * f
    o_ref[...] = _rms(x2, gf_ref[...])


def _final(dest_flat, yb, x1_all, route_rows, gate2, final_g, *, rows, tpb, per_row, row0):
    tm = min(256, rows)
    n_tiles = rows // tm
    blk0 = row0 // tm
    mod = (pl.BlockSpec((1, tm, D_MODEL), lambda i, d: (0, i, 0)) if per_row
           else pl.BlockSpec((1, 1, D_MODEL), lambda i, d: (i // tpb, 0, 0)))
    grid_spec = pltpu.PrefetchScalarGridSpec(
        num_scalar_prefetch=1, grid=(n_tiles,),
        in_specs=[pl.BlockSpec(memory_space=pl.ANY),
                  pl.BlockSpec((tm, D_MODEL), lambda i, d: (blk0 + i, 0)),
                  pl.BlockSpec((tm, LANES), lambda i, d: (blk0 + i, 0)),
                  mod, pl.BlockSpec((1, D_MODEL), lambda i, d: (0, 0))],
        out_specs=pl.BlockSpec((tm, D_MODEL), lambda i, d: (i, 0)),
        scratch_shapes=[pltpu.VMEM((2, 2, tm * TOK_ROWS, LANES), F32), pltpu.SemaphoreType.DMA((2,))])
    return pl.pallas_call(
        functools.partial(_final_kernel, tm=tm, n_tiles=n_tiles, row0=row0),
        grid_spec=grid_spec,
        out_shape=jax.ShapeDtypeStruct((rows, D_MODEL), F32),
        compiler_params=_params(1), name="final_sample" if per_row else "final_prompt",
    )(dest_flat, yb, x1_all, route_rows, gate2, final_g.reshape(1, -1))


def _rope_tables(pos):
    inv = ROPE_THETA ** (-jnp.arange(HALF, dtype=F32) / HALF)
    ang = pos.astype(F32)[:, None] * inv[None, :]
    cos = jnp.tile(jnp.cos(ang), (1, LANES // HALF))
    sin = jnp.sin(ang)
    sin_s = jnp.tile(jnp.concatenate([-sin, sin], axis=1), (1, LANES // HD))
    return cos, sin_s


def _pack_w_in(w_in):
    gl = w_in[:, _C_G:_C_G + 3 * N_HEADS].reshape(D_MODEL, 3, N_KV, QPK)
    gcols = []
    for k in range(N_KV):
        gk = gl[:, :, k, :].reshape(D_MODEL, 3 * QPK)
        gcols.append(jnp.pad(gk, ((0, 0), (0, LANES - 3 * QPK))))
    return jnp.concatenate([w_in[:, :_C_G]] + gcols, axis=1).astype(BF16)


def _pack_cmp_weights(cmp_w1, cmp_w2, bias, cmp_b2):
    w1 = cmp_w1.reshape(2, 2, CMP_STRIDE, HD, CMP_HID)
    eye = jnp.eye(N_KV, dtype=F32)
    w1p = jnp.einsum('crsdh,pk->cspdrkh', w1, eye).reshape(2, CMP_STRIDE * KV_W, 2 * N_KV * CMP_HID)
    w2p = jnp.einsum('chd,pk->cphkd', cmp_w2, eye).reshape(2, N_KV * CMP_HID, KV_W)
    b1p = jnp.tile(bias, (1, N_KV)).reshape(2, 1, N_KV * CMP_HID)
    b2p = jnp.tile(cmp_b2, (1, N_KV)).reshape(2, 1, KV_W)
    return w1p.astype(BF16), b1p, w2p.astype(BF16), b2p


def _band(n_cmp_pad, n_cmp, n_blk_pad, n_blk):
    n = np.arange(n_cmp_pad)[:, None]
    b = np.arange(n_blk_pad)[None, :]
    r = SEL_BLOCK // CMP_STRIDE
    m = (n >= r * b - 1) & (n <= r * b + r - 1) & (n < n_cmp) & (b < n_blk)
    return jnp.asarray(m.astype(np.float32))


def _expand(t, kc):
    n_chunks = t // kc
    key = np.arange(t).reshape(n_chunks, 1, kc)
    blk = np.arange(t // SEL_BLOCK).reshape(1, -1, 1)
    return jnp.asarray((key // SEL_BLOCK == blk).astype(np.float32), BF16)


def kernel(x_prompt, x_sample, c_prompt, c_sample, cache_cmp_kv, cache_sel_kv, cache_win_kv, state_conv, page_table,
           ln1_g, ln2_g, w_ada, b_ada, w_in, w_conv, cmp_pos, cmp_w1, cmp_b1, cmp_w2, cmp_b2, g_out_conv, g_out_attn,
           w_out, w_route_group, b_route_group, w_route_expert, b_route_expert, w_gate, w_up, w_down, final_g):
    depth = w_in.shape[0]
    assert depth == 1, "single-layer step"
    nb, t, _ = x_prompt.shape
    ns, ts, _ = x_sample.shape
    assert ts == 1 and t % 512 == 0 and t >= WINDOW + Q_BLOCK
    n_pool = cache_cmp_kv.shape[1]
    n_pages = page_table.shape[1]
    past = n_pages * PAGE
    wb = cache_win_kv.shape[2]
    assert wb == WINDOW
    l = 0

    n_c = nb + ns
    c_all = jnp.pad(jnp.concatenate([c_prompt, c_sample], axis=0), ((0, (-n_c) % SUBLANES), (0, 0)))
    mods = _ada(c_all, w_ada[l], b_ada[l])
    sh1, sc1, ga1, sh2, sc2, ga2 = [mods[:, j * D_MODEL:(j + 1) * D_MODEL] for j in range(6)]
    pr = lambda a: a[0:nb].reshape(nb, 1, D_MODEL)
    sr = lambda a: a[nb:nb + ns].reshape(1, ns, D_MODEL)

    w_pack = _pack_w_in(w_in[l])
    wconv8 = jnp.pad(w_conv[l], ((0, SUBLANES - CONV_K), (0, 0)))
    cos_p, sin_p = _rope_tables(jnp.arange(t, dtype=I32))
    cos_s, sin_s = _rope_tables(jnp.full((1,), past, I32))
    xp2 = x_prompt.reshape(nb * t, D_MODEL)
    xs2 = x_sample.reshape(ns, D_MODEL)
    (conv_p, cst_p, q_p, kvc_p, kvc_rows_p, kvs_rows_p, kvw_rows_p, ks_p, vs_p, kw_p, vw_p, gates_p) = _proj(
        xp2, ln1_g[l], pr(sc1), pr(sh1), w_pack, wconv8, cos_p, sin_p, nb=nb, t=t, sample=False)
    (conv_s, cst_s, q_s, _, kvc_rows_s, kvs_rows_s, kvw_rows_s, _, _, _, _, gates_s) = _proj(
        xs2, ln1_g[l], sr(sc1), sr(sh1), w_pack, wconv8, cos_s, sin_s, nb=ns, t=1, sample=True,
        prev=(state_conv[l][:, 0], state_conv[l][:, 1]))

    bias = _cmpbias(cmp_pos[l], cmp_w1[l], cmp_b1[l])
    w1p, b1p, w2p, b2p = _pack_cmp_weights(cmp_w1[l], cmp_w2[l], bias, cmp_b2[l])
    pp = t // PAGE
    cos_cp, sin_cp = _rope_tables((jnp.arange(t // CMP_STRIDE, dtype=I32) + 2) * CMP_STRIDE - 1)
    ck_p, cv_p = _cmp(kvc_p.reshape(nb * pp, SUBLANES, CHUNK_ROW), jnp.arange(nb * pp, dtype=I32), nb, pp,
                      w1p, b1p, w2p, b2p, cos_cp, sin_cp, "cmp_prompt", tiles=False)
    pt_flat = page_table.reshape(-1).astype(I32)
    cos_cs, sin_cs = _rope_tables((jnp.arange(past // CMP_STRIDE, dtype=I32) + 2) * CMP_STRIDE - 1)
    to_tiles = lambda a: a.transpose(0, 2, 3, 4, 1)
    ck_s, cv_s = _cmp(to_tiles(cache_cmp_kv[l]), pt_flat, ns, n_pages,
                      w1p, b1p, w2p, b2p, cos_cs, sin_cs, "cmp_sample", tiles=True)

    n_chunk_p = t // CMP_STRIDE
    n_blk_p = t // SEL_BLOCK
    band_p = _band(n_chunk_p, n_chunk_p - 1, n_blk_p, n_blk_p)
    attn_p = _attn_prompt(q_p, ck_p, cv_p, ks_p, vs_p, kw_p, vw_p, gates_p, band_p.T, _expand(t, ATTN_KEY_CHUNK), nb=nb, t=t)

    n_chunk_s = past // CMP_STRIDE
    n_sel_s = -(-(past + 1) // SEL_BLOCK)
    nbp = -(-n_sel_s // LANES) * LANES
    band_s = _band(n_chunk_s, (past + 1) // CMP_STRIDE - 1, nbp, n_sel_s)
    q3 = q_s.reshape(N_HEADS, ns, HD).transpose(1, 0, 2).astype(F32)
    gs = gates_s.reshape(ns, N_KV, LANES)[:, :, :3 * QPK].reshape(ns, N_KV, 3, QPK)
    gates_hm = jnp.pad(gs.transpose(0, 1, 3, 2).reshape(ns, N_HEADS, 3), ((0, 0), (0, 0), (0, LANES - 3)))
    rpt = 2 * N_KV
    new_rows = lambda a: jnp.pad(a.reshape(ns, rpt, HD), ((0, 0), (0, SUBLANES - rpt), (0, 0)))
    attn_s = _attn_sample(q3, ck_s, cv_s, band_s, to_tiles(cache_sel_kv[l]), pt_flat,
                          new_rows(kvs_rows_s), to_tiles(cache_win_kv[l]), new_rows(kvw_rows_s),
                          gates_hm, nb=ns, n_pages=n_pages, past=past, n_sel_blocks=n_sel_s).reshape(ns, ATTN_W)

    total = nb * t + ns
    w_out_b = w_out[l].astype(BF16)
    w_route = jnp.pad(jnp.concatenate([w_route_group[l], w_route_expert[l]], axis=1),
                      ((0, 0), (0, LANES - N_GROUPS - N_EXPERTS)))
    b_route = jnp.pad(jnp.concatenate([b_route_group[l], b_route_expert[l]]), (0, LANES - N_GROUPS - N_EXPERTS))
    tile_pad = lambda a: jnp.pad(a, ((0, TOKEN_TILE - ns), (0, 0)))
    smod = lambda a: tile_pad(a[nb:nb + ns]).reshape(1, TOKEN_TILE, D_MODEL)
    x1_all, hp_all, lg_all = _outp(
        (xp2, conv_p, attn_p.reshape(nb * t, ATTN_W), pr(ga1), pr(sc2), pr(sh2)),
        (tile_pad(xs2), tile_pad(conv_s), tile_pad(attn_s), smod(ga1), smod(sc2), smod(sh2)),
        g_out_conv[l], g_out_attn[l], w_out_b, ln2_g[l], w_route, tpb=t // TOKEN_TILE)

    route, counts = _route(lg_all, b_route.reshape(1, LANES), total)
    e = route[:total, 0:2].astype(I32)
    rank = route[:total, 4:6].astype(I32)
    cnt = counts[0, :N_EXPERTS].astype(I32)
    padded = (cnt + EXPERT_ROWS - 1) // EXPERT_ROWS * EXPERT_ROWS
    pad_end = jnp.cumsum(padded)
    pad_start = pad_end - padded
    m_slots = total * 2
    n_blocks = -(-(m_slots + N_EXPERTS * (EXPERT_ROWS - 1)) // EXPERT_ROWS)
    dest = jnp.clip(pad_start[e] + rank, 0, n_blocks * EXPERT_ROWS - 1)
    tok = jnp.broadcast_to(jnp.arange(total, dtype=I32)[:, None], (total, 2))
    slot_tok = jnp.zeros((n_blocks * EXPERT_ROWS,), I32).at[dest.reshape(-1)].set(
        tok.reshape(-1), unique_indices=True)
    blk_start = jnp.arange(n_blocks, dtype=I32) * EXPERT_ROWS
    blk_e = jnp.minimum(jnp.sum((pad_end[None, :] <= blk_start[:, None]).astype(I32), axis=1), N_EXPERTS - 1)

    yb = _experts(blk_e, slot_tok, hp_all, w_gate[l], w_up[l], w_down[l], n_blocks)
    dest_flat = dest.reshape(-1)
    y_p = _final(dest_flat, yb, x1_all, route, pr(ga2), final_g, rows=nb * t, tpb=t // 256, per_row=False, row0=0)
    y_s = _final(dest_flat, yb, x1_all, route, sr(ga2), final_g, rows=ns, tpb=1, per_row=True, row0=nb * t)

    kv_shape = (2, N_KV, HD)
    y_prompt = y_p.reshape(nb, t, D_MODEL)
    y_sample = y_s.reshape(ns, 1, D_MODEL)
    new_cmp_prompt = kvc_rows_p.reshape((1, nb, t) + kv_shape)
    new_cmp_sample = kvc_rows_s.reshape((1, ns, 1) + kv_shape)
    new_sel_prompt = kvs_rows_p.reshape((1, nb, t) + kv_shape)
    new_sel_sample = kvs_rows_s.reshape((1, ns, 1) + kv_shape)
    new_win_prompt = kvw_rows_p.reshape((nb, t) + kv_shape)[:, t - WINDOW:][None]
    new_win_sample = jnp.concatenate([cache_win_kv[l][:, 1:], kvw_rows_s.reshape((ns, 1) + kv_shape)], axis=1)[None]
    new_conv_prompt = cst_p[:, SUBLANES - (CONV_K - 1):][None]
    new_conv_sample = jnp.stack([state_conv[l][:, 1], cst_s], axis=1)[None]
    return (y_prompt, y_sample, new_cmp_prompt, new_cmp_sample, new_sel_prompt, new_sel_sample,
            new_win_prompt, new_win_sample, new_conv_prompt, new_conv_sample)
```

```python
import functools

import numpy as np
import jax
import jax.numpy as jnp
from jax import lax
from jax.experimental import pallas as pl
from jax.experimental.pallas import tpu as pltpu

F32 = jnp.float32
BF16 = jnp.bfloat16
I32 = jnp.int32

D_MODEL = 1024
CONV_W = 512
ATTN_W = 512
HD = 64
HALF = HD // 2
N_HEADS = 8
N_KV = 2
QPK = 4
KV_W = N_KV * HD
CONV_K = 3
PAGE = 128
CMP_STRIDE = 16
CMP_HID = 128
SEL_BLOCK = 64
N_SEL = 16
WINDOW = 512
Q_BLOCK = 128
ROPE_THETA = 10000.0
N_GROUPS = 4
EPG = 8
N_EXPERTS = 32
D_EXPERT = 512
MOE_BLOCK = 128
NORM_EPS = 1e-6
NEG_INF = -1e30
FORCE_SCORE = 1e4
LANES = 128
SUBLANES = 8
CHUNK_ROW = CMP_STRIDE * 2 * KV_W
VMEM_LIMIT = 56 * 1024 * 1024

_NT = (((1,), (1,)), ((), ()))
Q_SCALE = HD ** -0.5 * 1.4426950408889634


def _params(n_axes):
    return pltpu.CompilerParams(dimension_semantics=("arbitrary",) * n_axes,
                                vmem_limit_bytes=VMEM_LIMIT)


def _rms(x, g):
    return x * lax.rsqrt(jnp.mean(x * x, axis=-1, keepdims=True) + NORM_EPS) * g


def _rope128(x, cos, sin_signed, first_half):
    xr = jnp.where(first_half, pltpu.roll(x, LANES - HALF, 1), pltpu.roll(x, HALF, 1))
    return x * cos + xr * sin_signed


def _first_half_mask(rows):
    lane = lax.broadcasted_iota(I32, (rows, LANES), 1)
    return (lane % HD) < HALF


def _ada_kernel(c_ref, w_ref, b_ref, o_ref):
    c = c_ref[...]
    s = c * jax.nn.sigmoid(c)
    o_ref[...] = jnp.dot(s.astype(BF16), w_ref[...].astype(BF16), preferred_element_type=F32) + b_ref[...]


def _ada(c_all, w_ada, b_ada):
    m, d = c_all.shape
    n = w_ada.shape[1]
    tn = 1024
    return pl.pallas_call(
        _ada_kernel,
        grid=(n // tn,),
        in_specs=[pl.BlockSpec((m, d), lambda j: (0, 0)),
                  pl.BlockSpec((d, tn), lambda j: (0, j)),
                  pl.BlockSpec((1, tn), lambda j: (0, j))],
        out_specs=pl.BlockSpec((m, tn), lambda j: (0, j)),
        out_shape=jax.ShapeDtypeStruct((m, n), F32),
        compiler_params=_params(1),
        name="ada",
    )(c_all, w_ada, b_ada.reshape(1, n))


_C_B, _C_C, _C_U, _C_Q, _C_KVC, _C_KVS, _C_KVW, _C_G, _C_END = 0, 512, 1024, 1536, 2048, 2304, 2560, 2816, 3072


def _proj_kernel(*refs, tm, tpb, sample):
    if sample:
        (x_ref, g1_ref, sc_ref, sh_ref, w_ref, wc_ref, cos_ref, sin_ref, p0_ref, p1_ref,
         conv_ref, cst_ref, q_ref, kvc_ref, kvc_il_ref, kvs_ref, kvw_ref, ks_ref, vs_ref, kw_ref, vw_ref, gate_ref,
         ilbuf) = refs
        vbuf = None
    else:
        (x_ref, g1_ref, sc_ref, sh_ref, w_ref, wc_ref, cos_ref, sin_ref,
         conv_ref, cst_ref, q_ref, kvc_ref, kvc_il_ref, kvs_ref, kvw_ref, ks_ref, vs_ref, kw_ref, vw_ref, gate_ref,
         ilbuf, vbuf) = refs
    i = pl.program_id(0)
    x = x_ref[...]
    h = _rms(x, g1_ref[...]) * (1.0 + sc_ref[0]) + sh_ref[0]
    hb = h.astype(BF16)

    zc = jnp.dot(hb, w_ref[:, _C_B:_C_Q], preferred_element_type=F32)
    b_g = zc[:, 0:CONV_W]
    v = zc[:, CONV_W:2 * CONV_W] * zc[:, 2 * CONV_W:3 * CONV_W]
    wc = wc_ref[...]
    if sample:
        y = wc[0:1] * p0_ref[...] + wc[1:2] * p1_ref[...] + wc[2:3] * v
        cst_ref[...] = v
    else:
        @pl.when(i % tpb == 0)
        def _():
            vbuf[0:SUBLANES, :] = jnp.zeros((SUBLANES, CONV_W), F32)
        vbuf[SUBLANES:SUBLANES + tm, :] = v
        y = wc[0:1] * vbuf[pl.ds(SUBLANES - 2, tm), :] + wc[1:2] * vbuf[pl.ds(SUBLANES - 1, tm), :] + wc[2:3] * v
        tail = vbuf[tm:tm + SUBLANES, :]
        cst_ref[0] = tail
        vbuf[0:SUBLANES, :] = tail
    conv_ref[...] = b_g * y

    cos = cos_ref[...]
    sin_s = sin_ref[...]
    first = _first_half_mask(tm)

    zq = jnp.dot(hb, w_ref[:, _C_Q:_C_KVC], preferred_element_type=F32)
    for gq in range(ATTN_W // LANES):
        qr = _rope128(zq[:, gq * LANES:(gq + 1) * LANES], cos, sin_s, first) * Q_SCALE
        q_ref[0, 2 * gq] = qr[:, 0:HD].astype(BF16)
        q_ref[0, 2 * gq + 1] = qr[:, HD:LANES].astype(BF16)

    def store_rows(out_ref, halves):
        for j in range(2 * N_KV):
            piece = halves[j // N_KV]
            if j % N_KV == 1:
                piece = pltpu.roll(piece, HD, 1)
            ilbuf[pl.ds(j, tm, stride=2 * N_KV), :] = piece
        out_ref[...] = ilbuf[:, 0:HD]

    zkv = jnp.dot(hb, w_ref[:, _C_KVC:_C_G], preferred_element_type=F32)
    kvc_ref[...] = zkv[:, 0:2 * KV_W]
    store_rows(kvc_il_ref, (zkv[:, 0:KV_W], zkv[:, KV_W:2 * KV_W]))
    for base, kv_ref, kh_ref, vh_ref in ((2 * KV_W, kvs_ref, ks_ref, vs_ref), (4 * KV_W, kvw_ref, kw_ref, vw_ref)):
        kr = _rope128(zkv[:, base:base + KV_W], cos, sin_s, first)
        vv = zkv[:, base + KV_W:base + 2 * KV_W]
        store_rows(kv_ref, (kr, vv))
        lane = lax.broadcasted_iota(I32, (tm, LANES), 1)
        for k in range(N_KV):
            kh_ref[0, k] = kr[:, k * HD:(k + 1) * HD].astype(BF16)
            vk = vv if k == 0 else pltpu.roll(vv, HD, 1)
            vh_ref[0, k] = jnp.where(lane < HD, vk, jnp.where(lane == HD, 1.0, 0.0)).astype(BF16)

    zg = jnp.dot(hb, w_ref[:, _C_G:_C_END], preferred_element_type=F32)
    gate_ref[...] = jax.nn.sigmoid(zg)


def _proj(x2d, g1, sc, sh, w_pack, w_conv, cos_t, sin_t, *, nb, t, sample, prev=None):
    rows = nb * t
    tm = min(512, rows) if not sample else rows
    tpb = (t // tm) if not sample else 1
    n_tiles = rows // tm
    f = lambda a: jax.ShapeDtypeStruct(a, F32)
    b = lambda a: jax.ShapeDtypeStruct(a, BF16)
    if sample:
        mod_spec = pl.BlockSpec((1, tm, D_MODEL), lambda i: (0, 0, 0))
        tab_spec = pl.BlockSpec((1, LANES), lambda i: (0, 0))
        cst_shape, cst_spec = f((rows, CONV_W)), pl.BlockSpec((tm, CONV_W), lambda i: (0, 0))
        hm = lambda i: (0, 0, i, 0)
        hb_, ht_ = 1, rows
    else:
        mod_spec = pl.BlockSpec((1, 1, D_MODEL), lambda i: (i // tpb, 0, 0))
        tab_spec = pl.BlockSpec((tm, LANES), lambda i: (i % tpb, 0))
        cst_shape, cst_spec = f((nb, SUBLANES, CONV_W)), pl.BlockSpec((1, SUBLANES, CONV_W), lambda i: (i // tpb, 0, 0))
        hm = lambda i: (i // tpb, 0, i % tpb, 0)
        hb_, ht_ = nb, t
    row = lambda w: pl.BlockSpec((tm, w), lambda i: (i, 0))
    in_specs = [row(D_MODEL), pl.BlockSpec((1, D_MODEL), lambda i: (0, 0)), mod_spec, mod_spec,
                pl.BlockSpec((D_MODEL, _C_END), lambda i: (0, 0)),
                pl.BlockSpec((SUBLANES, CONV_W), lambda i: (0, 0)), tab_spec, tab_spec]
    args = [x2d, g1.reshape(1, D_MODEL), sc, sh, w_pack, w_conv, cos_t, sin_t]
    scratch = [pltpu.VMEM((2 * N_KV * tm, LANES), F32)]
    if sample:
        in_specs += [row(CONV_W), row(CONV_W)]
        args += [prev[0], prev[1]]
    else:
        scratch.append(pltpu.VMEM((tm + SUBLANES, CONV_W), F32))
    il_rows = 2 * N_KV * rows
    il = pl.BlockSpec((2 * N_KV * tm, HD), lambda i: (i, 0))
    out_shape = [f((rows, CONV_W)), cst_shape, b((hb_, N_HEADS, ht_, HD)),
                 f((rows, 2 * KV_W)), f((il_rows, HD)), f((il_rows, HD)), f((il_rows, HD)),
                 b((hb_, N_KV, ht_, HD)), b((hb_, N_KV, ht_, LANES)), b((hb_, N_KV, ht_, HD)), b((hb_, N_KV, ht_, LANES)),
                 f((rows, 2 * LANES))]
    out_specs = [row(CONV_W), cst_spec, pl.BlockSpec((1, N_HEADS, tm, HD), hm),
                 row(2 * KV_W), il, il, il,
                 pl.BlockSpec((1, N_KV, tm, HD), hm), pl.BlockSpec((1, N_KV, tm, LANES), hm),
                 pl.BlockSpec((1, N_KV, tm, HD), hm), pl.BlockSpec((1, N_KV, tm, LANES), hm),
                 row(2 * LANES)]
    return pl.pallas_call(
        functools.partial(_proj_kernel, tm=tm, tpb=tpb, sample=sample),
        grid=(n_tiles,), in_specs=in_specs, out_specs=out_specs, out_shape=out_shape,
        scratch_shapes=scratch, compiler_params=_params(1),
        name="proj_sample" if sample else "proj_prompt",
    )(*args)


def _cmpbias_kernel(pos_ref, w_ref, b1_ref, o_ref):
    for c in range(2):
        o_ref[c:c + 1, :] = jnp.sum(pos_ref[c] * w_ref[c], axis=0, keepdims=True) + b1_ref[c:c + 1, :]


def _cmpbias(cmp_pos, cmp_w1, cmp_b1):
    n = cmp_pos.shape[1] * cmp_pos.shape[2]
    return pl.pallas_call(
        _cmpbias_kernel,
        out_shape=jax.ShapeDtypeStruct((2, CMP_HID), F32),
        compiler_params=pltpu.CompilerParams(vmem_limit_bytes=VMEM_LIMIT),
        name="cmpbias",
    )(cmp_pos.reshape(2, n, 1), cmp_w1.reshape(2, n, CMP_HID), cmp_b1)


def _cmp_kernel(pt_ref, *refs, ppt, tiles):
    pages = refs[:ppt + 1]
    if tiles:
        unfold_ref = refs[ppt + 1]
        refs = refs[1:]
    w1_ref, b1_ref, w2_ref, b2_ref, cos_ref, sin_ref, ck_ref, cv_ref, lhs, pbuf = refs[ppt + 1:]
    r = ppt * SUBLANES
    first = _first_half_mask(r)
    if tiles:
        for j in range(ppt + 1):
            a = pages[j][...].reshape(2 * KV_W, PAGE).astype(BF16)
            y = lax.dot_general(unfold_ref[...], a, _NT, preferred_element_type=F32)
            for c in range(2):
                for s in range(CMP_STRIDE):
                    lhs[c, j * SUBLANES:(j + 1) * SUBLANES, s * KV_W:(s + 1) * KV_W] = (
                        y[s * SUBLANES:(s + 1) * SUBLANES, c * KV_W:(c + 1) * KV_W])
    else:
        for j in range(ppt + 1):
            for c in range(2):
                for s in range(CMP_STRIDE):
                    src = slice(s * 2 * KV_W + c * KV_W, s * 2 * KV_W + (c + 1) * KV_W)
                    lhs[c, j * SUBLANES:(j + 1) * SUBLANES, s * KV_W:(s + 1) * KV_W] = pages[j][0, :, src]
    for c in range(2):
        p = jnp.dot(lhs[c].astype(BF16), w1_ref[c], preferred_element_type=F32)
        pbuf[...] = p[:, 2 * CMP_HID:4 * CMP_HID]
        hid = p[0:r, 0:2 * CMP_HID] + pbuf[pl.ds(1, r), :] + b1_ref[c]
        act = jax.nn.gelu(hid)
        comp = jnp.dot(act.astype(BF16), w2_ref[c], preferred_element_type=F32) + b2_ref[c]
        if c == 0:
            comp = _rope128(comp, cos_ref[...], sin_ref[...], first)
            out = ck_ref
        else:
            out = cv_ref
        for k in range(N_KV):
            out[0, k] = comp[:, k * HD:(k + 1) * HD]


def _cmp(pages, pt_flat, nb, n_pages, w1p, b1p, w2p, b2p, cos_c, sin_c, name, tiles):
    ppt = min(32, n_pages)
    n_tiles = n_pages // ppt
    r = ppt * SUBLANES
    n_chunk = n_pages * SUBLANES
    zeros = (0,) * (pages.ndim - 1)

    def page_map(j):
        return lambda b, t, pt: (pt[b * n_pages + t * ppt + j],) + zeros

    def next_map(b, t, pt):
        return (pt[b * n_pages + jnp.minimum(t * ppt + ppt, n_pages - 1)],) + zeros

    page_blk = (None, 2, N_KV, HD, PAGE) if tiles else (1, SUBLANES, CHUNK_ROW)
    in_specs = [pl.BlockSpec(page_blk, page_map(j)) for j in range(ppt)]
    in_specs.append(pl.BlockSpec(page_blk, next_map))
    const = lambda shp: pl.BlockSpec(shp, lambda b, t, pt: (0,) * len(shp))
    extra = []
    if tiles:
        row = np.arange(PAGE)
        tok = (row % SUBLANES) * CMP_STRIDE + row // SUBLANES
        extra = [jnp.asarray(tok[:, None] == np.arange(PAGE)[None, :], BF16)]
        in_specs.append(const((PAGE, PAGE)))
    in_specs += [const(w1p.shape), const(b1p.shape), const(w2p.shape), const(b2p.shape),
                 pl.BlockSpec((r, LANES), lambda b, t, pt: (t, 0)), pl.BlockSpec((r, LANES), lambda b, t, pt: (t, 0))]
    hm = pl.BlockSpec((1, N_KV, r, HD), lambda b, t, pt: (b, 0, t, 0))
    grid_spec = pltpu.PrefetchScalarGridSpec(
        num_scalar_prefetch=1, grid=(nb, n_tiles), in_specs=in_specs, out_specs=[hm, hm],
        scratch_shapes=[pltpu.VMEM((2, r + SUBLANES, CMP_STRIDE * KV_W), F32),
                        pltpu.VMEM((r + SUBLANES, 2 * CMP_HID), F32)])
    return pl.pallas_call(
        functools.partial(_cmp_kernel, ppt=ppt, tiles=tiles),
        grid_spec=grid_spec,
        out_shape=[jax.ShapeDtypeStruct((nb, N_KV, n_chunk, HD), F32)] * 2,
        compiler_params=_params(2), name=name,
    )(pt_flat, *([pages] * (ppt + 1)), *extra, w1p, b1p, w2p, b2p, cos_c, sin_c)


def _softmax_rows(s, valid):
    s = jnp.where(valid, s, NEG_INF)
    m = jnp.max(s, axis=-1, keepdims=True)
    e = jnp.exp2(s - m)
    return e / jnp.sum(e, axis=-1, keepdims=True)


def _attn_p_kernel(q_ref, ck_ref, cv_ref, ks_ref, vs_ref, kw_ref, vw_ref, gate_ref, band_ref, exp_ref, o_ref,
                   *, n_cmp_pad, n_blk, kc, hg, wc):
    qb = pl.program_id(2)
    start = qb * Q_BLOCK
    tpos = start + lax.broadcasted_iota(I32, (Q_BLOCK, 1), 0)
    groups = range(QPK // hg)
    rows = hg * Q_BLOCK

    def q_of(g):
        return q_ref[0, g * hg:(g + 1) * hg].reshape(rows, HD)

    def biased(s, bias):
        width = s.shape[-1]
        return (s.reshape(hg, Q_BLOCK, width) + bias[None]).reshape(rows, width)

    ck = ck_ref[0, 0].astype(BF16)
    cv = cv_ref[0, 0].astype(BF16)
    cmp_end = (lax.broadcasted_iota(I32, (1, n_cmp_pad), 1) + 2) * CMP_STRIDE - 1
    bias_c = jnp.where(cmp_end <= tpos, 0.0, NEG_INF)
    o_c = []
    pcs = jnp.zeros((Q_BLOCK, n_cmp_pad), F32)
    for g in groups:
        s_c = biased(lax.dot_general(q_of(g), ck, _NT, preferred_element_type=F32), bias_c)
        m_c = jnp.maximum(jnp.max(s_c, axis=-1, keepdims=True), 0.5 * NEG_INF)
        e_c = jnp.exp2(s_c - m_c)
        l_c = jnp.sum(e_c, axis=-1, keepdims=True)
        p_c = e_c * (1.0 / jnp.where(l_c > 0.0, l_c, 1.0))
        o_c.append(jnp.dot(p_c.astype(BF16), cv, preferred_element_type=F32))
        for h in range(hg):
            pcs = pcs + p_c[h * Q_BLOCK:(h + 1) * Q_BLOCK]

    imp =lax.dot_general(band_ref[...], pcs, _NT, preferred_element_type=F32,
                          precision=lax.Precision.HIGHEST)
    blk = lax.broadcasted_iota(I32, (n_blk, Q_BLOCK), 0)
    tlane = start + lax.broadcasted_iota(I32, (1, Q_BLOCK), 1)
    cur = tlane // SEL_BLOCK
    causal = blk * SEL_BLOCK <= tlane
    forced = causal & ((blk == 0) | (blk == cur) | (blk == cur - 1))
    score = jnp.where(forced, FORCE_SCORE, jnp.where(causal, imp, -1.0))
    rank = jnp.zeros((n_blk, Q_BLOCK), F32)
    for bp in range(n_blk):
        other = score[bp:bp + 1, :]
        beats = (other > score) | ((other == score) & (bp < blk))
        rank = rank + beats.astype(F32)
    sel_t = (rank < float(min(N_SEL, n_blk))).astype(BF16)
    eye = (lax.broadcasted_iota(I32, (Q_BLOCK, Q_BLOCK), 0)
           == lax.broadcasted_iota(I32, (Q_BLOCK, Q_BLOCK), 1)).astype(BF16)
    sel = lax.dot_general(eye, sel_t, _NT, preferred_element_type=F32).astype(BF16)

    n_chunks = (start + Q_BLOCK + kc - 1) // kc

    def online(state, kj, vj, bias):
        out = []
        for g in groups:
            m_i, acc = state[g]
            s = biased(lax.dot_general(q_of(g), kj, _NT, preferred_element_type=F32), bias)
            m_new = jnp.maximum(m_i, jnp.max(s, axis=-1, keepdims=True))
            p = jnp.exp2(s - m_new).astype(BF16)
            out.append((m_new, jnp.exp2(m_i - m_new) * acc + jnp.dot(p, vj, preferred_element_type=F32)))
        return tuple(out)

    def step(j, state, causal_chunk):
        off = pl.multiple_of(j * kc, kc)
        mexp = jnp.dot(sel, exp_ref[j], preferred_element_type=F32)
        bias = mexp * (-NEG_INF) + NEG_INF
        if causal_chunk:
            keypos = off + lax.broadcasted_iota(I32, (1, kc), 1)
            bias = jnp.where(keypos <= tpos, bias, NEG_INF)
        return online(state, ks_ref[0, 0, pl.ds(off, kc), :], vs_ref[0, 0, pl.ds(off, kc), :], bias)

    init = tuple((jnp.full((rows, 1), NEG_INF, F32), jnp.zeros((rows, LANES), F32)) for _ in groups)
    state = lax.fori_loop(0, n_chunks - 1, lambda j, c: step(j, c, False), init)
    sel_state = step(n_chunks - 1, state, True)

    s0 = jnp.maximum(start - WINDOW, 0)
    win_state = init
    for c in range((WINDOW + Q_BLOCK) // wc):
        off = pl.multiple_of(s0 + c * wc, Q_BLOCK)
        dist = tpos - (off + lax.broadcasted_iota(I32, (1, wc), 1))
        bias_w = jnp.where((dist >= 0) & (dist <= WINDOW), 0.0, NEG_INF)
        win_state = online(win_state, kw_ref[0, 0, pl.ds(off, wc), :], vw_ref[0, 0, pl.ds(off, wc), :], bias_w)

    gt = gate_ref[...]
    for g in groups:
        acc_s = sel_state[g][1]
        acc_w = win_state[g][1]
        o_s = acc_s[:, 0:HD] * (1.0 / acc_s[:, HD:HD + 1])
        o_w = acc_w[:, 0:HD] * (1.0 / acc_w[:, HD:HD + 1])
        for hh in range(hg):
            h = g * hg + hh
            rs = slice(hh * Q_BLOCK, (hh + 1) * Q_BLOCK)
            o = (gt[:, h:h + 1] * o_c[g][rs] + gt[:, QPK + h:QPK + h + 1] * o_s[rs]
                 + gt[:, 2 * QPK + h:2 * QPK + h + 1] * o_w[rs])
            o_ref[0, :, h * HD:(h + 1) * HD] = o


ATTN_HEAD_GROUP = 4
ATTN_KEY_CHUNK = 512
ATTN_WIN_CHUNK = 640


def _attn_prompt(q_hm, ck, cv, ks, vs, kw, vw, gates, band, expand, *, nb, t):
    n_qb = t // Q_BLOCK
    n_cmp_pad = ck.shape[2]
    n_blk = band.shape[0]
    kc = expand.shape[2]
    kv_spec = lambda n, w=HD: pl.BlockSpec((1, 1, n, w), lambda b, k, i: (b, k, 0, 0))
    return pl.pallas_call(
        functools.partial(_attn_p_kernel, n_cmp_pad=n_cmp_pad, n_blk=n_blk, kc=kc, hg=ATTN_HEAD_GROUP, wc=ATTN_WIN_CHUNK),
        grid=(nb, N_KV, n_qb),
        in_specs=[pl.BlockSpec((1, QPK, Q_BLOCK, HD), lambda b, k, i: (b, k, i, 0)),
                  kv_spec(n_cmp_pad), kv_spec(n_cmp_pad), kv_spec(t), kv_spec(t, LANES), kv_spec(t), kv_spec(t, LANES),
                  pl.BlockSpec((Q_BLOCK, LANES), lambda b, k, i: (b * n_qb + i, k)),
                  pl.BlockSpec(band.shape, lambda b, k, i: (0, 0)),
                  pl.BlockSpec(expand.shape, lambda b, k, i: (0, 0, 0))],
        out_specs=pl.BlockSpec((1, Q_BLOCK, QPK * HD), lambda b, k, i: (b, i, k)),
        out_shape=jax.ShapeDtypeStruct((nb, t, ATTN_W), F32),
        compiler_params=_params(3), name="attn_prompt",
    )(q_hm, ck, cv, ks, vs, kw, vw, gates, band, expand)


def _attn_s1_kernel(q_ref, ck_ref, cv_ref, band_ref, oc_ref, imp_ref, *, n_chunk, past):
    q = q_ref[0]
    q16 = jnp.concatenate([q, jnp.zeros_like(q)], axis=0).astype(BF16)
    cmp_end = (lax.broadcasted_iota(I32, (1, n_chunk), 1) + 2) * CMP_STRIDE - 1
    valid = cmp_end <= past
    head = lax.broadcasted_iota(I32, (2 * N_HEADS, 1), 0)
    oc = jnp.zeros((2 * N_HEADS, HD), F32)
    imps = []
    for k in range(N_KV):
        s = lax.dot_general(q16, ck_ref[0, k].astype(BF16), _NT, preferred_element_type=F32)
        p = _softmax_rows(s, valid) * valid.astype(F32)
        in_grp = (head >= k * QPK) & (head < (k + 1) * QPK)
        p = jnp.where(in_grp, p, 0.0)
        oc = oc + jnp.dot(p.astype(BF16), cv_ref[0, k].astype(BF16), preferred_element_type=F32)
        pcs = jnp.sum(p, axis=0, keepdims=True)
        pcs8 = jnp.broadcast_to(pcs, (SUBLANES, n_chunk))
        imps.append(jnp.dot(pcs8, band_ref[...], preferred_element_type=F32,
                            precision=lax.Precision.HIGHEST)[0:1])
    oc_ref[0] = oc[0:N_HEADS]
    imp_ref[0] = jnp.concatenate(imps + [jnp.zeros((SUBLANES - N_KV, imps[0].shape[1]), F32)], axis=0)


def _topk_s_kernel(imp_ref, idx_ref, *, n_sel_blocks, past):
    imp = imp_ref[...]
    rows, nbp = imp.shape
    blk = lax.broadcasted_iota(I32, (rows, nbp), 1)
    cur = past // SEL_BLOCK
    causal = blk * SEL_BLOCK <= past
    forced = causal & ((blk == 0) | (blk == cur) | (blk == cur - 1))
    score = jnp.where(forced, FORCE_SCORE, jnp.where(causal, imp, -1.0))
    score = jnp.where(blk < n_sel_blocks, score, -2.0)
    lane = lax.broadcasted_iota(I32, (rows, LANES), 1)
    out = jnp.zeros((rows, LANES), I32)
    for r in range(min(N_SEL, n_sel_blocks)):
        m = jnp.max(score, axis=-1, keepdims=True)
        pick = jnp.min(jnp.where(score == m, blk, nbp), axis=-1, keepdims=True)
        out = jnp.where(lane == r, pick, out)
        score = jnp.where(blk == pick, -3.0, score)
    idx_ref[...] = out


def _attn_s2_kernel(pt_ref, idx_ref, *refs, n_pages, past, n_sel_blocks):
    ktiles, vtiles = refs[:N_SEL], refs[N_SEL:2 * N_SEL]
    q_ref, oc_ref, kvs_ref, wk_ref, wv_ref, kvw_ref, gate_ref, o_ref, kbuf, vbuf = refs[2 * N_SEL:]
    b = pl.program_id(0)
    k = pl.program_id(1)
    q = q_ref[0]
    q16f = jnp.concatenate([q, jnp.zeros_like(q)], axis=0)
    q16 = q16f.astype(BF16)
    head = lax.broadcasted_iota(I32, (N_HEADS, 1), 0)
    nk = N_SEL * PAGE
    lane = lax.broadcasted_iota(I32, (1, nk), 1)
    slot = lane // PAGE
    new_blk = n_sel_blocks - 1
    wb = wk_ref.shape[-1]
    wpos = past - wb + lax.broadcasted_iota(I32, (1, wb), 1)
    wdist = past - wpos
    valid_w = (wdist >= 0) & (wdist <= WINDOW) & (wpos >= 0)

    def attend(s, valid, v_t, k_new, v_new):
        s_new = jnp.sum(q16f * k_new, axis=-1, keepdims=True)
        s = jnp.where(valid, s, NEG_INF)
        m = jnp.maximum(jnp.max(s, axis=-1, keepdims=True), s_new)
        e = jnp.exp2(s - m)
        e_new = jnp.exp2(s_new - m)
        den = jnp.sum(e, axis=-1, keepdims=True) + e_new
        acc = lax.dot_general(e.astype(BF16), v_t, _NT, preferred_element_type=F32) + e_new * v_new
        return acc / den

    in_grp = (head >= k * QPK) & (head < (k + 1) * QPK)
    bvec = jnp.zeros((1, nk), I32)
    for j in range(N_SEL):
        kbuf[:, j * PAGE:(j + 1) * PAGE] = ktiles[j][...].astype(BF16)
        vbuf[:, j * PAGE:(j + 1) * PAGE] = vtiles[j][...].astype(BF16)
        bvec = jnp.where(slot == j, idx_ref[(b * N_KV + k) * LANES + j], bvec)
    tok = (bvec // 2) * PAGE + lane % PAGE
    valid = (tok // SEL_BLOCK == bvec) & (bvec < new_blk) & (tok <= past)
    s = jnp.dot(q16, kbuf[...], preferred_element_type=F32)
    o_s = attend(s, valid, vbuf[...], kvs_ref[0, pl.ds(k, 1), :], kvs_ref[0, pl.ds(N_KV + k, 1), :])
    sw = jnp.dot(q16, wk_ref[...].astype(BF16), preferred_element_type=F32)
    o_w = attend(sw, valid_w, wv_ref[...].astype(BF16), kvw_ref[0, pl.ds(k, 1), :], kvw_ref[0, pl.ds(N_KV + k, 1), :])
    g = gate_ref[0]
    part = jnp.where(in_grp, g[:, 1:2] * o_s[0:N_HEADS] + g[:, 2:3] * o_w[0:N_HEADS], 0.0)

    @pl.when(k == 0)
    def _():
        o_ref[0] = g[:, 0:1] * oc_ref[0] + part

    @pl.when(k > 0)
    def _():
        o_ref[0] = o_ref[0] + part


def _attn_sample(q3, ck, cv, band_s, sel_t, pt_flat, kvs_rows, win_t, kvw_rows, gates_hm,
                 *, nb, n_pages, past, n_sel_blocks):
    n_chunk = ck.shape[2]
    nbp = band_s.shape[1]
    oc, imp = pl.pallas_call(
        functools.partial(_attn_s1_kernel, n_chunk=n_chunk, past=past),
        grid=(nb,),
        in_specs=[pl.BlockSpec((1, N_HEADS, HD), lambda b: (b, 0, 0)),
                  pl.BlockSpec((1, N_KV, n_chunk, HD), lambda b: (b, 0, 0, 0)),
                  pl.BlockSpec((1, N_KV, n_chunk, HD), lambda b: (b, 0, 0, 0)),
                  pl.BlockSpec(band_s.shape, lambda b: (0, 0))],
        out_specs=[pl.BlockSpec((1, N_HEADS, HD), lambda b: (b, 0, 0)),
                   pl.BlockSpec((1, SUBLANES, nbp), lambda b: (b, 0, 0))],
        out_shape=[jax.ShapeDtypeStruct((nb, N_HEADS, HD), F32), jax.ShapeDtypeStruct((nb, SUBLANES, nbp), F32)],
        compiler_params=_params(1), name="attn_sample_cmp",
    )(q3, ck, cv, band_s)
    imp2 = imp[:, 0:N_KV, :].reshape(nb * N_KV, nbp)
    idx = pl.pallas_call(
        functools.partial(_topk_s_kernel, n_sel_blocks=n_sel_blocks, past=past),
        out_shape=jax.ShapeDtypeStruct((nb * N_KV, LANES), I32),
        compiler_params=pltpu.CompilerParams(vmem_limit_bytes=VMEM_LIMIT), name="topk_sample",
    )(imp2)
    idx_flat = idx.reshape(-1)

    def tile_map(c, j):
        def f(b, k, pt, ix):
            bidx = ix[(b * N_KV + k) * LANES + j]
            return (pt[b * n_pages + jnp.minimum(bidx // 2, n_pages - 1)], c, k, 0, 0)
        return f

    tile = lambda c, j: pl.BlockSpec((None, None, None, HD, PAGE), tile_map(c, j))
    in_specs = [tile(0, j) for j in range(N_SEL)] + [tile(1, j) for j in range(N_SEL)]
    wb = win_t.shape[-1]
    per_b = lambda shp: pl.BlockSpec(shp, lambda b, k, pt, ix: (b, 0, 0))
    in_specs += [per_b((1, N_HEADS, HD)), per_b((1, N_HEADS, HD)), per_b((1, SUBLANES, HD)),
                 pl.BlockSpec((None, None, None, HD, wb), lambda b, k, pt, ix: (b, 0, k, 0, 0)),
                 pl.BlockSpec((None, None, None, HD, wb), lambda b, k, pt, ix: (b, 1, k, 0, 0)),
                 per_b((1, SUBLANES, HD)), per_b((1, N_HEADS, LANES))]
    grid_spec = pltpu.PrefetchScalarGridSpec(
        num_scalar_prefetch=2, grid=(nb, N_KV), in_specs=in_specs,
        out_specs=per_b((1, N_HEADS, HD)),
        scratch_shapes=[pltpu.VMEM((HD, N_SEL * PAGE), BF16), pltpu.VMEM((HD, N_SEL * PAGE), BF16)])
    return pl.pallas_call(
        functools.partial(_attn_s2_kernel, n_pages=n_pages, past=past, n_sel_blocks=n_sel_blocks),
        grid_spec=grid_spec,
        out_shape=jax.ShapeDtypeStruct((nb, N_HEADS, HD), F32),
        compiler_params=_params(2), name="attn_sample_sel",
    )(pt_flat, idx_flat, *([sel_t] * (2 * N_SEL)), q3, oc, kvs_rows, win_t, win_t, kvw_rows, gates_hm)


TOK_ROWS = D_MODEL // LANES


def _store_token_tiles(ref, x):
    n = x.shape[0]
    for j in range(TOK_ROWS):
        ref[pl.ds(j, n, stride=TOK_ROWS), :] = x[:, j * LANES:(j + 1) * LANES]


def _load_token_tiles(ref, lead, n):
    return jnp.concatenate([ref[lead + (pl.ds(j, n, stride=TOK_ROWS), slice(None))] for j in range(TOK_ROWS)], axis=1)


def _outp_kernel(xp_ref, convp_ref, attnp_ref, ga1p_ref, sc2p_ref, sh2p_ref,
                 xs_ref, convs_ref, attns_ref, ga1s_ref, sc2s_ref, sh2s_ref,
                 gc_ref, ga_ref, w_ref, g2_ref, wr_ref, x1_ref, hp_ref, lg_ref, *, n_prompt_tiles):
    is_p = pl.program_id(0) < n_prompt_tiles
    pick = lambda a, b: jnp.where(is_p, a, b)
    cn = _rms(pick(convp_ref[...], convs_ref[...]), gc_ref[...])
    an = _rms(pick(attnp_ref[...], attns_ref[...]), ga_ref[...])
    cat = jnp.concatenate([cn, an], axis=1).astype(BF16)
    y = jnp.dot(cat, w_ref[...], preferred_element_type=F32)
    x1 = pick(xp_ref[...], xs_ref[...]) + pick(ga1p_ref[0], ga1s_ref[0]) * y
    x1_ref[...] = x1
    hp = _rms(x1, g2_ref[...]) * (1.0 + pick(sc2p_ref[0], sc2s_ref[0])) + pick(sh2p_ref[0], sh2s_ref[0])
    _store_token_tiles(hp_ref, hp)
    lg_ref[...] = jnp.dot(hp, wr_ref[...], preferred_element_type=F32, precision=lax.Precision.HIGHEST)


TOKEN_TILE = 512


def _outp(prompt, sample, g_conv, g_attn, w_out_b, g2, w_route, *, tpb):
    tm = TOKEN_TILE
    n_p = prompt[0].shape[0] // tm
    total = (n_p + 1) * tm
    last = n_p - 1
    prow = lambda w: pl.BlockSpec((tm, w), lambda i: (jnp.minimum(i, last), 0))
    srow = lambda w: pl.BlockSpec((tm, w), lambda i: (0, 0))
    pmod = pl.BlockSpec((1, 1, D_MODEL), lambda i: (jnp.minimum(i, last) // tpb, 0, 0))
    smod = pl.BlockSpec((1, tm, D_MODEL), lambda i: (0, 0, 0))
    vec = lambda w: pl.BlockSpec((1, w), lambda i: (0, 0))
    row = lambda w: pl.BlockSpec((tm, w), lambda i: (i, 0))
    in_specs = [prow(D_MODEL), prow(CONV_W), prow(ATTN_W), pmod, pmod, pmod,
                srow(D_MODEL), srow(CONV_W), srow(ATTN_W), smod, smod, smod,
                vec(CONV_W), vec(ATTN_W), pl.BlockSpec((D_MODEL, D_MODEL), lambda i: (0, 0)), vec(D_MODEL),
                pl.BlockSpec((D_MODEL, LANES), lambda i: (0, 0))]
    return pl.pallas_call(
        functools.partial(_outp_kernel, n_prompt_tiles=n_p),
        grid=(n_p + 1,), in_specs=in_specs,
        out_specs=[row(D_MODEL), pl.BlockSpec((tm * TOK_ROWS, LANES), lambda i: (i, 0)), row(LANES)],
        out_shape=[jax.ShapeDtypeStruct((total, D_MODEL), F32), jax.ShapeDtypeStruct((total * TOK_ROWS, LANES), F32),
                   jax.ShapeDtypeStruct((total, LANES), F32)],
        compiler_params=_params(1), name="outp",
    )(*prompt, *sample, g_conv.reshape(1, -1), g_attn.reshape(1, -1), w_out_b, g2.reshape(1, -1), w_route)


def _route_kernel(lg_ref, bias_ref, tri_ref, o_ref, cnt_ref, carry, *, tm, n_valid):
    i = pl.program_id(0)

    @pl.when(i == 0)
    def _():
        carry[...] = jnp.zeros_like(carry)

    lane = lax.broadcasted_iota(I32, (tm, LANES), 1)
    rowid = i * tm + lax.broadcasted_iota(I32, (tm, 1), 0)
    live = rowid < n_valid
    lg = lg_ref[...] + bias_ref[...]
    is_g = lane < N_GROUPS
    lgg = jnp.where(is_g, lg, NEG_INF)
    gmax = jnp.max(lgg, axis=-1, keepdims=True)
    grp = jnp.min(jnp.where(is_g & (lgg == gmax), lane, LANES), axis=-1, keepdims=True)
    p_grp = 1.0 / jnp.sum(jnp.where(is_g, jnp.exp(lgg - gmax), 0.0), axis=-1, keepdims=True)
    eid = lane - N_GROUPS
    in_grp = (eid >= grp * EPG) & (eid < (grp + 1) * EPG)
    le = jnp.where(in_grp, lg, NEG_INF)
    v1 = jnp.max(le, axis=-1, keepdims=True)
    e1 = jnp.min(jnp.where(in_grp & (le == v1), eid, LANES), axis=-1, keepdims=True)
    le2 = jnp.where(eid == e1, NEG_INF, le)
    v2 = jnp.max(le2, axis=-1, keepdims=True)
    e2 = jnp.min(jnp.where(in_grp & (eid != e1) & (le2 == v2), eid, LANES), axis=-1, keepdims=True)
    ex2 = jnp.exp(v2 - v1)
    w1 = p_grp * (1.0 / (1.0 + ex2))
    w2 = p_grp * (ex2 / (1.0 + ex2))
    oh1 = ((lane == e1) & live).astype(F32)
    oh2 = ((lane == e2) & live).astype(F32)
    both = oh1 + oh2
    before = jnp.dot(tri_ref[...], both.astype(BF16), preferred_element_type=F32) + carry[0:1, :]
    r1 = jnp.sum(oh1 * before, axis=-1, keepdims=True)
    r2 = jnp.sum(oh2 * before, axis=-1, keepdims=True)
    carry[0:1, :] = carry[0:1, :] + jnp.sum(both, axis=0, keepdims=True)
    out = jnp.where(lane == 0, e1.astype(F32), 0.0)
    out = jnp.where(lane == 1, e2.astype(F32), out)
    out = jnp.where(lane == 2, w1, out)
    out = jnp.where(lane == 3, w2, out)
    out = jnp.where(lane == 4, r1, out)
    out = jnp.where(lane == 5, r2, out)
    o_ref[...] = out
    cnt_ref[...] = carry[...]


def _route(logits, bias_row, n_valid):
    total = logits.shape[0]
    tm = TOKEN_TILE
    n_tiles = total // tm
    tri =(np.arange(tm)[:, None] > np.arange(tm)[None, :]).astype(np.float32)
    return pl.pallas_call(
        functools.partial(_route_kernel, tm=tm, n_valid=n_valid),
        grid=(n_tiles,),
        in_specs=[pl.BlockSpec((tm, LANES), lambda i: (i, 0)), pl.BlockSpec((1, LANES), lambda i: (0, 0)),
                  pl.BlockSpec((tm, tm), lambda i: (0, 0))],
        out_specs=[pl.BlockSpec((tm, LANES), lambda i: (i, 0)), pl.BlockSpec((SUBLANES, LANES), lambda i: (0, 0))],
        out_shape=[jax.ShapeDtypeStruct((total, LANES), F32), jax.ShapeDtypeStruct((SUBLANES, LANES), F32)],
        scratch_shapes=[pltpu.VMEM((SUBLANES, LANES), F32)],
        compiler_params=_params(1), name="route",
    )(logits, bias_row, jnp.asarray(tri, BF16))


def _row_copy(src_hbm, row, dst, slot, r, sem):
    src = src_hbm.at[pl.ds(pl.multiple_of(row * TOK_ROWS, TOK_ROWS), TOK_ROWS), :]
    return pltpu.make_async_copy(src, dst.at[slot, pl.ds(r * TOK_ROWS, TOK_ROWS), :], sem.at[slot])


EXPERT_ROWS = 256


def _experts_kernel(blk_e_ref, tok_first_ref, tok_next_ref, x_hbm, wg_ref, wu_ref, wd_ref, o_ref,
                    xbuf, sem, wg_b, wu_b, wd_b, *, n_blocks):
    i = pl.program_id(0)
    slot = i % 2

    def issue(tok_ref, s):
        for r in range(EXPERT_ROWS):
            _row_copy(x_hbm, tok_ref[0, 0, r], xbuf, s, r, sem).start()

    @pl.when(i == 0)
    def _():
        issue(tok_first_ref, 0)

    @pl.when(i + 1 < n_blocks)
    def _():
        issue(tok_next_ref, 1 - slot)

    changed = jnp.logical_or(i == 0, blk_e_ref[i] != blk_e_ref[jnp.maximum(i - 1, 0)])

    @pl.when(changed)
    def _():
        wg_b[...] = wg_ref[0].astype(BF16)
        wu_b[...] = wu_ref[0].astype(BF16)
        wd_b[...] = wd_ref[0].astype(BF16)

    for r in range(EXPERT_ROWS):
        _row_copy(x_hbm, 0, xbuf, slot, r, sem).wait()
    x = _load_token_tiles(xbuf, (slot,), EXPERT_ROWS).astype(BF16)
    g = jnp.dot(x, wg_b[...], preferred_element_type=F32)
    u = jnp.dot(x, wu_b[...], preferred_element_type=F32)
    h = (g * jax.nn.sigmoid(g)) * u
    _store_token_tiles(o_ref, jnp.dot(h.astype(BF16), wd_b[...], preferred_element_type=F32))


def _experts(blk_e, slot_tok, hp_all, w_gate, w_up, w_down, n_blocks):
    tok3 = slot_tok.reshape(n_blocks, 1, EXPERT_ROWS)
    idx_blk = lambda f: pl.BlockSpec((1, 1, EXPERT_ROWS), f, memory_space=pltpu.SMEM)
    grid_spec = pltpu.PrefetchScalarGridSpec(
        num_scalar_prefetch=1, grid=(n_blocks,),
        in_specs=[idx_blk(lambda i, be: (0, 0, 0)),
                  idx_blk(lambda i, be: (jnp.minimum(i + 1, n_blocks - 1), 0, 0)),
                  pl.BlockSpec(memory_space=pl.ANY),
                  pl.BlockSpec((1, D_MODEL, D_EXPERT), lambda i, be: (be[i], 0, 0)),
                  pl.BlockSpec((1, D_MODEL, D_EXPERT), lambda i, be: (be[i], 0, 0)),
                  pl.BlockSpec((1, D_EXPERT, D_MODEL), lambda i, be: (be[i], 0, 0))],
        out_specs=pl.BlockSpec((EXPERT_ROWS * TOK_ROWS, LANES), lambda i, be: (i, 0)),
        scratch_shapes=[pltpu.VMEM((2, EXPERT_ROWS * TOK_ROWS, LANES), F32), pltpu.SemaphoreType.DMA((2,)),
                        pltpu.VMEM((D_MODEL, D_EXPERT), BF16), pltpu.VMEM((D_MODEL, D_EXPERT), BF16),
                        pltpu.VMEM((D_EXPERT, D_MODEL), BF16)])
    return pl.pallas_call(
        functools.partial(_experts_kernel, n_blocks=n_blocks),
        grid_spec=grid_spec,
        out_shape=jax.ShapeDtypeStruct((n_blocks * EXPERT_ROWS * TOK_ROWS, LANES), F32),
        compiler_params=_params(1), name="experts",
    )(blk_e, tok3, tok3, hp_all, w_gate, w_up, w_down)


def _final_kernel(dest_first_ref, dest_next_ref, yb_hbm, x1_ref, wt_ref, gate2_ref, gf_ref, o_ref, ybuf, sem,
                  *, tm, n_tiles):
    i = pl.program_id(0)
    slot = i % 2

    def issue(dest_ref, s):
        for r in range(tm):
            for k in range(2):
                d = dest_ref[0, 0, 2 * r + k]
                src = yb_hbm.at[pl.ds(pl.multiple_of(d * TOK_ROWS, TOK_ROWS), TOK_ROWS), :]
                pltpu.make_async_copy(src, ybuf.at[s, k, pl.ds(r * TOK_ROWS, TOK_ROWS), :], sem.at[s]).start()

    @pl.when(i == 0)
    def _():
        issue(dest_first_ref, 0)

    @pl.when(i + 1 < n_tiles)
    def _():
        issue(dest_next_ref, 1 - slot)

    for r in range(tm):
        for k in range(2):
            pltpu.make_async_copy(yb_hbm.at[pl.ds(0, TOK_ROWS), :], ybuf.at[slot, k, pl.ds(r * TOK_ROWS, TOK_ROWS), :],
                                  sem.at[slot]).wait()
    wt = wt_ref[...]
    f = wt[:, 2:3] * _load_token_tiles(ybuf, (slot, 0), tm) + wt[:, 3:4] * _load_token_tiles(ybuf, (slot, 1), tm)
    x2 = x1_ref[...] + gate2_ref[0] * f
    o_ref[...] = _rms(x2, gf_ref[...])


def _final(dest_pad, yb, x1_all, route_rows, gate2, final_g, *, rows, tpb, per_row, row0):
    tm = min(256, rows)
    n_tiles = rows // tm
    blk0 = row0 // tm
    dest3 = dest_pad.reshape(-1, 1, 2 * tm)
    idx_blk = lambda f: pl.BlockSpec((1, 1, 2 * tm), f, memory_space=pltpu.SMEM)
    mod = (pl.BlockSpec((1, tm, D_MODEL), lambda i: (0, i, 0)) if per_row
           else pl.BlockSpec((1, 1, D_MODEL), lambda i: (i // tpb, 0, 0)))
    return pl.pallas_call(
        functools.partial(_final_kernel, tm=tm, n_tiles=n_tiles),
        grid=(n_tiles,),
        in_specs=[idx_blk(lambda i: (blk0, 0, 0)),
                  idx_blk(lambda i: (blk0 + jnp.minimum(i + 1, n_tiles - 1), 0, 0)),
                  pl.BlockSpec(memory_space=pl.ANY),
                  pl.BlockSpec((tm, D_MODEL), lambda i: (blk0 + i, 0)),
                  pl.BlockSpec((tm, LANES), lambda i: (blk0 + i, 0)),
                  mod, pl.BlockSpec((1, D_MODEL), lambda i: (0, 0))],
        out_specs=pl.BlockSpec((tm, D_MODEL), lambda i: (i, 0)),
        scratch_shapes=[pltpu.VMEM((2, 2, tm * TOK_ROWS, LANES), F32), pltpu.SemaphoreType.DMA((2,))],
        out_shape=jax.ShapeDtypeStruct((rows, D_MODEL), F32),
        compiler_params=_params(1), name="final_sample" if per_row else "final_prompt",
    )(dest3, dest3, yb, x1_all, route_rows, gate2, final_g.reshape(1, -1))


def _rope_tables(pos):
    inv = ROPE_THETA ** (-jnp.arange(HALF, dtype=F32) / HALF)
    ang = pos.astype(F32)[:, None] * inv[None, :]
    cos = jnp.tile(jnp.cos(ang), (1, LANES // HALF))
    sin = jnp.sin(ang)
    sin_s = jnp.tile(jnp.concatenate([-sin, sin], axis=1), (1, LANES // HD))
    return cos, sin_s


def _pack_w_in(w_in):
    gl = w_in[:, _C_G:_C_G + 3 * N_HEADS].reshape(D_MODEL, 3, N_KV, QPK)
    gcols = []
    for k in range(N_KV):
        gk = gl[:, :, k, :].reshape(D_MODEL, 3 * QPK)
        gcols.append(jnp.pad(gk, ((0, 0), (0, LANES - 3 * QPK))))
    return jnp.concatenate([w_in[:, :_C_G]] + gcols, axis=1).astype(BF16)


def _pack_cmp_weights(cmp_w1, cmp_w2, bias, cmp_b2):
    w1 = cmp_w1.reshape(2, 2, CMP_STRIDE, HD, CMP_HID)
    eye = jnp.eye(N_KV, dtype=F32)
    w1p = jnp.einsum('crsdh,pk->cspdrkh', w1, eye).reshape(2, CMP_STRIDE * KV_W, 2 * N_KV * CMP_HID)
    w2p = jnp.einsum('chd,pk->cphkd', cmp_w2, eye).reshape(2, N_KV * CMP_HID, KV_W)
    b1p = jnp.tile(bias, (1, N_KV)).reshape(2, 1, N_KV * CMP_HID)
    b2p = jnp.tile(cmp_b2, (1, N_KV)).reshape(2, 1, KV_W)
    return w1p.astype(BF16), b1p, w2p.astype(BF16), b2p


def _band(n_cmp_pad, n_cmp, n_blk_pad, n_blk):
    n = np.arange(n_cmp_pad)[:, None]
    b = np.arange(n_blk_pad)[None, :]
    r = SEL_BLOCK // CMP_STRIDE
    m = (n >= r * b - 1) & (n <= r * b + r - 1) & (n < n_cmp) & (b < n_blk)
    return jnp.asarray(m.astype(np.float32))


def _expand(t, kc):
    n_chunks = t // kc
    key = np.arange(t).reshape(n_chunks, 1, kc)
    blk = np.arange(t // SEL_BLOCK).reshape(1, -1, 1)
    return jnp.asarray((key // SEL_BLOCK == blk).astype(np.float32), BF16)


def kernel(x_prompt, x_sample, c_prompt, c_sample, cache_cmp_kv, cache_sel_kv, cache_win_kv, state_conv, page_table,
           ln1_g, ln2_g, w_ada, b_ada, w_in, w_conv, cmp_pos, cmp_w1, cmp_b1, cmp_w2, cmp_b2, g_out_conv, g_out_attn,
           w_out, w_route_group, b_route_group, w_route_expert, b_route_expert, w_gate, w_up, w_down, final_g):
    depth = w_in.shape[0]
    assert depth == 1, "single-layer step"
    nb, t, _ = x_prompt.shape
    ns, ts, _ = x_sample.shape
    assert ts == 1 and t % 512 == 0 and t >= WINDOW + Q_BLOCK
    n_pool = cache_cmp_kv.shape[1]
    n_pages = page_table.shape[1]
    past = n_pages * PAGE
    wb = cache_win_kv.shape[2]
    assert wb == WINDOW
    l = 0

    n_c = nb + ns
    c_all = jnp.pad(jnp.concatenate([c_prompt, c_sample], axis=0), ((0, (-n_c) % SUBLANES), (0, 0)))
    mods = _ada(c_all, w_ada[l], b_ada[l])
    sh1, sc1, ga1, sh2, sc2, ga2 = [mods[:, j * D_MODEL:(j + 1) * D_MODEL] for j in range(6)]
    pr = lambda a: a[0:nb].reshape(nb, 1, D_MODEL)
    sr = lambda a: a[nb:nb + ns].reshape(1, ns, D_MODEL)

    w_pack = _pack_w_in(w_in[l])
    wconv8 = jnp.pad(w_conv[l], ((0, SUBLANES - CONV_K), (0, 0)))
    cos_p, sin_p = _rope_tables(jnp.arange(t, dtype=I32))
    cos_s, sin_s = _rope_tables(jnp.full((1,), past, I32))
    xp2 = x_prompt.reshape(nb * t, D_MODEL)
    xs2 = x_sample.reshape(ns, D_MODEL)
    (conv_p, cst_p, q_p, kvc_p, kvc_rows_p, kvs_rows_p, kvw_rows_p, ks_p, vs_p, kw_p, vw_p, gates_p) = _proj(
        xp2, ln1_g[l], pr(sc1), pr(sh1), w_pack, wconv8, cos_p, sin_p, nb=nb, t=t, sample=False)
    (conv_s, cst_s, q_s, _, kvc_rows_s, kvs_rows_s, kvw_rows_s, _, _, _, _, gates_s) = _proj(
        xs2, ln1_g[l], sr(sc1), sr(sh1), w_pack, wconv8, cos_s, sin_s, nb=ns, t=1, sample=True,
        prev=(state_conv[l][:, 0], state_conv[l][:, 1]))

    bias = _cmpbias(cmp_pos[l], cmp_w1[l], cmp_b1[l])
    w1p, b1p, w2p, b2p = _pack_cmp_weights(cmp_w1[l], cmp_w2[l], bias, cmp_b2[l])
    pp = t // PAGE
    cos_cp, sin_cp = _rope_tables((jnp.arange(t // CMP_STRIDE, dtype=I32) + 2) * CMP_STRIDE - 1)
    ck_p, cv_p = _cmp(kvc_p.reshape(nb * pp, SUBLANES, CHUNK_ROW), jnp.arange(nb * pp, dtype=I32), nb, pp,
                      w1p, b1p, w2p, b2p, cos_cp, sin_cp, "cmp_prompt", tiles=False)
    pt_flat = page_table.reshape(-1).astype(I32)
    cos_cs, sin_cs = _rope_tables((jnp.arange(past // CMP_STRIDE, dtype=I32) + 2) * CMP_STRIDE - 1)
    to_tiles = lambda a: a.transpose(0, 2, 3, 4, 1)
    ck_s, cv_s = _cmp(to_tiles(cache_cmp_kv[l]), pt_flat, ns, n_pages,
                      w1p, b1p, w2p, b2p, cos_cs, sin_cs, "cmp_sample", tiles=True)

    n_chunk_p = t // CMP_STRIDE
    n_blk_p = t // SEL_BLOCK
    band_p = _band(n_chunk_p, n_chunk_p - 1, n_blk_p, n_blk_p)
    attn_p = _attn_prompt(q_p, ck_p, cv_p, ks_p, vs_p, kw_p, vw_p, gates_p, band_p.T, _expand(t, ATTN_KEY_CHUNK), nb=nb, t=t)

    n_chunk_s = past // CMP_STRIDE
    n_sel_s = -(-(past + 1) // SEL_BLOCK)
    nbp = -(-n_sel_s // LANES) * LANES
    band_s = _band(n_chunk_s, (past + 1) // CMP_STRIDE - 1, nbp, n_sel_s)
    q3 = q_s.reshape(N_HEADS, ns, HD).transpose(1, 0, 2).astype(F32)
    gs = gates_s.reshape(ns, N_KV, LANES)[:, :, :3 * QPK].reshape(ns, N_KV, 3, QPK)
    gates_hm = jnp.pad(gs.transpose(0, 1, 3, 2).reshape(ns, N_HEADS, 3), ((0, 0), (0, 0), (0, LANES - 3)))
    rpt = 2 * N_KV
    new_rows = lambda a: jnp.pad(a.reshape(ns, rpt, HD), ((0, 0), (0, SUBLANES - rpt), (0, 0)))
    attn_s = _attn_sample(q3, ck_s, cv_s, band_s, to_tiles(cache_sel_kv[l]), pt_flat,
                          new_rows(kvs_rows_s), to_tiles(cache_win_kv[l]), new_rows(kvw_rows_s),
                          gates_hm, nb=ns, n_pages=n_pages, past=past, n_sel_blocks=n_sel_s).reshape(ns, ATTN_W)

    total = nb * t + ns
    w_out_b = w_out[l].astype(BF16)
    w_route = jnp.pad(jnp.concatenate([w_route_group[l], w_route_expert[l]], axis=1),
                      ((0, 0), (0, LANES - N_GROUPS - N_EXPERTS)))
    b_route = jnp.pad(jnp.concatenate([b_route_group[l], b_route_expert[l]]), (0, LANES - N_GROUPS - N_EXPERTS))
    tile_pad = lambda a: jnp.pad(a, ((0, TOKEN_TILE - ns), (0, 0)))
    smod = lambda a: tile_pad(a[nb:nb + ns]).reshape(1, TOKEN_TILE, D_MODEL)
    x1_all, hp_all, lg_all = _outp(
        (xp2, conv_p, attn_p.reshape(nb * t, ATTN_W), pr(ga1), pr(sc2), pr(sh2)),
        (tile_pad(xs2), tile_pad(conv_s), tile_pad(attn_s), smod(ga1), smod(sc2), smod(sh2)),
        g_out_conv[l], g_out_attn[l], w_out_b, ln2_g[l], w_route, tpb=t // TOKEN_TILE)

    route, counts = _route(lg_all, b_route.reshape(1, LANES), total)
    e = route[:total, 0:2].astype(I32)
    rank = route[:total, 4:6].astype(I32)
    cnt = counts[0, :N_EXPERTS].astype(I32)
    padded = (cnt + EXPERT_ROWS - 1) // EXPERT_ROWS * EXPERT_ROWS
    pad_end = jnp.cumsum(padded)
    pad_start = pad_end - padded
    m_slots = total * 2
    n_blocks = -(-(m_slots + N_EXPERTS * (EXPERT_ROWS - 1)) // EXPERT_ROWS)
    dest = jnp.clip(pad_start[e] + rank, 0, n_blocks * EXPERT_ROWS - 1)
    tok = jnp.broadcast_to(jnp.arange(total, dtype=I32)[:, None], (total, 2))
    slot_tok = jnp.zeros((n_blocks * EXPERT_ROWS,), I32).at[dest.reshape(-1)].set(
        tok.reshape(-1), unique_indices=True)
    blk_start = jnp.arange(n_blocks, dtype=I32) * EXPERT_ROWS
    blk_e = jnp.minimum(jnp.sum((pad_end[None, :] <= blk_start[:, None]).astype(I32), axis=1), N_EXPERTS - 1)

    yb = _experts(blk_e, slot_tok, hp_all, w_gate[l], w_up[l], w_down[l], n_blocks)
    dest_pad = jnp.pad(dest, ((0, x1_all.shape[0] - total), (0, 0)))
    y_p = _final(dest_pad, yb, x1_all, route, pr(ga2), final_g, rows=nb * t, tpb=t // 256, per_row=False, row0=0)
    y_s = _final(dest_pad, yb, x1_all, route, sr(ga2), final_g, rows=ns, tpb=1, per_row=True, row0=nb * t)

    kv_shape = (2, N_KV, HD)
    y_prompt = y_p.reshape(nb, t, D_MODEL)
    y_sample = y_s.reshape(ns, 1, D_MODEL)
    new_cmp_prompt = kvc_rows_p.reshape((1, nb, t) + kv_shape)
    new_cmp_sample = kvc_rows_s.reshape((1, ns, 1) + kv_shape)
    new_sel_prompt = kvs_rows_p.reshape((1, nb, t) + kv_shape)
    new_sel_sample = kvs_rows_s.reshape((1, ns, 1) + kv_shape)
    new_win_prompt = kvw_rows_p.reshape((nb, t) + kv_shape)[:, t - WINDOW:][None]
    new_win_sample = jnp.concatenate([cache_win_kv[l][:, 1:], kvw_rows_s.reshape((ns, 1) + kv_shape)], axis=1)[None]
    new_conv_prompt = cst_p[:, SUBLANES - (CONV_K - 1):][None]
    new_conv_sample = jnp.stack([state_conv[l][:, 1], cst_s], axis=1)[None]
    return (y_prompt, y_sample, new_cmp_prompt, new_cmp_sample, new_sel_prompt, new_sel_sample,
            new_win_prompt, new_win_sample, new_conv_prompt, new_conv_sample)
```

```python
import functools

import numpy as np
import jax
import jax.numpy as jnp
from jax import lax
from jax.experimental import pallas as pl
from jax.experimental.pallas import tpu as pltpu

F32 = jnp.float32
BF16 = jnp.bfloat16
I32 = jnp.int32

D_MODEL = 1024
CONV_W = 512
ATTN_W = 512
HD = 64
HALF = HD // 2
N_HEADS = 8
N_KV = 2
QPK = 4
KV_W = N_KV * HD
CONV_K = 3
PAGE = 128
CMP_STRIDE = 16
CMP_HID = 128
SEL_BLOCK = 64
N_SEL = 16
WINDOW = 512
Q_BLOCK = 128
ROPE_THETA = 10000.0
N_GROUPS = 4
EPG = 8
N_EXPERTS = 32
D_EXPERT = 512
NORM_EPS = 1e-6
NEG_INF = -1e30
FORCE_SCORE = 1e4
LANES = 128
SUBLANES = 8
CHUNK_ROW = CMP_STRIDE * 2 * KV_W
VMEM_LIMIT = 56 * 1024 * 1024

_NT = (((1,), (1,)), ((), ()))
Q_SCALE = HD ** -0.5 * 1.4426950408889634


def _params(n_axes):
    return pltpu.CompilerParams(dimension_semantics=("arbitrary",) * n_axes,
                                vmem_limit_bytes=VMEM_LIMIT)


def _rms(x, g):
    return x * lax.rsqrt(jnp.mean(x * x, axis=-1, keepdims=True) + NORM_EPS) * g


def _rope128(x, cos, sin_signed, first_half):
    xr = jnp.where(first_half, pltpu.roll(x, LANES - HALF, 1), pltpu.roll(x, HALF, 1))
    return x * cos + xr * sin_signed


def _first_half_mask(rows):
    lane = lax.broadcasted_iota(I32, (rows, LANES), 1)
    return (lane % HD) < HALF


def _ada_kernel(c_ref, w_ref, b_ref, o_ref):
    c = c_ref[...]
    s = c * jax.nn.sigmoid(c)
    o_ref[...] = jnp.dot(s.astype(BF16), w_ref[...].astype(BF16), preferred_element_type=F32) + b_ref[...]


def _ada(c_all, w_ada, b_ada):
    m, d = c_all.shape
    n = w_ada.shape[1]
    tn = 1024
    return pl.pallas_call(
        _ada_kernel,
        grid=(n // tn,),
        in_specs=[pl.BlockSpec((m, d), lambda j: (0, 0)),
                  pl.BlockSpec((d, tn), lambda j: (0, j)),
                  pl.BlockSpec((1, tn), lambda j: (0, j))],
        out_specs=pl.BlockSpec((m, tn), lambda j: (0, j)),
        out_shape=jax.ShapeDtypeStruct((m, n), F32),
        compiler_params=_params(1),
        name="ada",
    )(c_all, w_ada, b_ada.reshape(1, n))


_C_B, _C_C, _C_U, _C_Q, _C_KVC, _C_KVS, _C_KVW, _C_G, _C_END = 0, 512, 1024, 1536, 2048, 2304, 2560, 2816, 3072


def _proj_kernel(*refs, tm, tpb, sample):
    if sample:
        (x_ref, g1_ref, sc_ref, sh_ref, w_ref, wc_ref, cos_ref, sin_ref, p0_ref, p1_ref,
         conv_ref, cst_ref, q_ref, kvc_ref, kvc_il_ref, kvs_ref, kvw_ref, ks_ref, vs_ref, kw_ref, vw_ref, gate_ref,
         ilbuf) = refs
        vbuf = None
    else:
        (x_ref, g1_ref, sc_ref, sh_ref, w_ref, wc_ref, cos_ref, sin_ref,
         conv_ref, cst_ref, q_ref, kvc_ref, kvc_il_ref, kvs_ref, kvw_ref, ks_ref, vs_ref, kw_ref, vw_ref, gate_ref,
         ilbuf, vbuf) = refs
    i = pl.program_id(0)
    x = x_ref[...]
    h = _rms(x, g1_ref[...]) * (1.0 + sc_ref[0]) + sh_ref[0]
    hb = h.astype(BF16)

    zc = jnp.dot(hb, w_ref[:, _C_B:_C_Q], preferred_element_type=F32)
    b_g = zc[:, 0:CONV_W]
    v = zc[:, CONV_W:2 * CONV_W] * zc[:, 2 * CONV_W:3 * CONV_W]
    wc = wc_ref[...]
    if sample:
        y = wc[0:1] * p0_ref[...] + wc[1:2] * p1_ref[...] + wc[2:3] * v
        cst_ref[...] = v
    else:
        @pl.when(i % tpb == 0)
        def _():
            vbuf[0:SUBLANES, :] = jnp.zeros((SUBLANES, CONV_W), F32)
        vbuf[SUBLANES:SUBLANES + tm, :] = v
        y = wc[0:1] * vbuf[pl.ds(SUBLANES - 2, tm), :] + wc[1:2] * vbuf[pl.ds(SUBLANES - 1, tm), :] + wc[2:3] * v
        tail = vbuf[tm:tm + SUBLANES, :]
        cst_ref[0] = tail
        vbuf[0:SUBLANES, :] = tail
    conv_ref[...] = b_g * y

    cos = cos_ref[...]
    sin_s = sin_ref[...]
    first = _first_half_mask(tm)

    zq = jnp.dot(hb, w_ref[:, _C_Q:_C_KVC], preferred_element_type=F32)
    for gq in range(ATTN_W // LANES):
        qr = _rope128(zq[:, gq * LANES:(gq + 1) * LANES], cos, sin_s, first) * Q_SCALE
        q_ref[0, 2 * gq] = qr[:, 0:HD].astype(BF16)
        q_ref[0, 2 * gq + 1] = qr[:, HD:LANES].astype(BF16)

    def store_rows(out_ref, halves):
        for j in range(2 * N_KV):
            piece = halves[j // N_KV]
            if j % N_KV == 1:
                piece = pltpu.roll(piece, HD, 1)
            ilbuf[pl.ds(j, tm, stride=2 * N_KV), :] = piece
        out_ref[...] = ilbuf[:, 0:HD]

    zkv = jnp.dot(hb, w_ref[:, _C_KVC:_C_G], preferred_element_type=F32)
    kvc_ref[...] = zkv[:, 0:2 * KV_W]
    store_rows(kvc_il_ref, (zkv[:, 0:KV_W], zkv[:, KV_W:2 * KV_W]))
    for base, kv_ref, kh_ref, vh_ref in ((2 * KV_W, kvs_ref, ks_ref, vs_ref), (4 * KV_W, kvw_ref, kw_ref, vw_ref)):
        kr = _rope128(zkv[:, base:base + KV_W], cos, sin_s, first)
        vv = zkv[:, base + KV_W:base + 2 * KV_W]
        store_rows(kv_ref, (kr, vv))
        lane = lax.broadcasted_iota(I32, (tm, LANES), 1)
        for k in range(N_KV):
            kh_ref[0, k] = kr[:, k * HD:(k + 1) * HD].astype(BF16)
            vk = vv if k == 0 else pltpu.roll(vv, HD, 1)
            vh_ref[0, k] = jnp.where(lane < HD, vk, jnp.where(lane == HD, 1.0, 0.0)).astype(BF16)

    zg = jnp.dot(hb, w_ref[:, _C_G:_C_END], preferred_element_type=F32)
    gate_ref[...] = jax.nn.sigmoid(zg)


def _proj(x2d, g1, sc, sh, w_pack, w_conv, cos_t, sin_t, *, nb, t, sample, prev=None):
    rows = nb * t
    tm = min(512, rows) if not sample else rows
    tpb = (t // tm) if not sample else 1
    n_tiles = rows // tm
    f = lambda a: jax.ShapeDtypeStruct(a, F32)
    b = lambda a: jax.ShapeDtypeStruct(a, BF16)
    if sample:
        mod_spec = pl.BlockSpec((1, tm, D_MODEL), lambda i: (0, 0, 0))
        tab_spec = pl.BlockSpec((1, LANES), lambda i: (0, 0))
        cst_shape, cst_spec = f((rows, CONV_W)), pl.BlockSpec((tm, CONV_W), lambda i: (0, 0))
        hm = lambda i: (0, 0, i, 0)
        hb_, ht_ = 1, rows
    else:
        mod_spec = pl.BlockSpec((1, 1, D_MODEL), lambda i: (i // tpb, 0, 0))
        tab_spec = pl.BlockSpec((tm, LANES), lambda i: (i % tpb, 0))
        cst_shape, cst_spec = f((nb, SUBLANES, CONV_W)), pl.BlockSpec((1, SUBLANES, CONV_W), lambda i: (i // tpb, 0, 0))
        hm = lambda i: (i // tpb, 0, i % tpb, 0)
        hb_, ht_ = nb, t
    row = lambda w: pl.BlockSpec((tm, w), lambda i: (i, 0))
    in_specs = [row(D_MODEL), pl.BlockSpec((1, D_MODEL), lambda i: (0, 0)), mod_spec, mod_spec,
                pl.BlockSpec((D_MODEL, _C_END), lambda i: (0, 0)),
                pl.BlockSpec((SUBLANES, CONV_W), lambda i: (0, 0)), tab_spec, tab_spec]
    args = [x2d, g1.reshape(1, D_MODEL), sc, sh, w_pack, w_conv, cos_t, sin_t]
    scratch = [pltpu.VMEM((2 * N_KV * tm, LANES), F32)]
    if sample:
        in_specs += [row(CONV_W), row(CONV_W)]
        args += [prev[0], prev[1]]
    else:
        scratch.append(pltpu.VMEM((tm + SUBLANES, CONV_W), F32))
    il_rows = 2 * N_KV * rows
    il = pl.BlockSpec((2 * N_KV * tm, HD), lambda i: (i, 0))
    out_shape = [f((rows, CONV_W)), cst_shape, b((hb_, N_HEADS, ht_, HD)),
                 f((rows, 2 * KV_W)), f((il_rows, HD)), f((il_rows, HD)), f((il_rows, HD)),
                 b((hb_, N_KV, ht_, HD)), b((hb_, N_KV, ht_, LANES)), b((hb_, N_KV, ht_, HD)), b((hb_, N_KV, ht_, LANES)),
                 f((rows, 2 * LANES))]
    out_specs = [row(CONV_W), cst_spec, pl.BlockSpec((1, N_HEADS, tm, HD), hm),
                 row(2 * KV_W), il, il, il,
                 pl.BlockSpec((1, N_KV, tm, HD), hm), pl.BlockSpec((1, N_KV, tm, LANES), hm),
                 pl.BlockSpec((1, N_KV, tm, HD), hm), pl.BlockSpec((1, N_KV, tm, LANES), hm),
                 row(2 * LANES)]
    return pl.pallas_call(
        functools.partial(_proj_kernel, tm=tm, tpb=tpb, sample=sample),
        grid=(n_tiles,), in_specs=in_specs, out_specs=out_specs, out_shape=out_shape,
        scratch_shapes=scratch, compiler_params=_params(1),
        name="proj_sample" if sample else "proj_prompt",
    )(*args)


def _cmpbias_kernel(pos_ref, w_ref, b1_ref, o_ref):
    for c in range(2):
        o_ref[c:c + 1, :] = jnp.sum(pos_ref[c] * w_ref[c], axis=0, keepdims=True) + b1_ref[c:c + 1, :]


def _cmpbias(cmp_pos, cmp_w1, cmp_b1):
    n = cmp_pos.shape[1] * cmp_pos.shape[2]
    return pl.pallas_call(
        _cmpbias_kernel,
        out_shape=jax.ShapeDtypeStruct((2, CMP_HID), F32),
        compiler_params=pltpu.CompilerParams(vmem_limit_bytes=VMEM_LIMIT),
        name="cmpbias",
    )(cmp_pos.reshape(2, n, 1), cmp_w1.reshape(2, n, CMP_HID), cmp_b1)


def _cmp_kernel(pt_ref, *refs, ppt, tiles):
    pages = refs[:ppt + 1]
    if tiles:
        unfold_ref = refs[ppt + 1]
        refs = refs[1:]
    w1_ref, b1_ref, w2_ref, b2_ref, cos_ref, sin_ref, ck_ref, cv_ref, lhs, pbuf = refs[ppt + 1:]
    r = ppt * SUBLANES
    first = _first_half_mask(r)
    if tiles:
        for j in range(ppt + 1):
            a = pages[j][...].reshape(2 * KV_W, PAGE).astype(BF16)
            y = lax.dot_general(unfold_ref[...], a, _NT, preferred_element_type=F32)
            for c in range(2):
                for s in range(CMP_STRIDE):
                    lhs[c, j * SUBLANES:(j + 1) * SUBLANES, s * KV_W:(s + 1) * KV_W] = (
                        y[s * SUBLANES:(s + 1) * SUBLANES, c * KV_W:(c + 1) * KV_W])
    else:
        for j in range(ppt + 1):
            for c in range(2):
                for s in range(CMP_STRIDE):
                    src = slice(s * 2 * KV_W + c * KV_W, s * 2 * KV_W + (c + 1) * KV_W)
                    lhs[c, j * SUBLANES:(j + 1) * SUBLANES, s * KV_W:(s + 1) * KV_W] = pages[j][0, :, src]
    for c in range(2):
        p = jnp.dot(lhs[c].astype(BF16), w1_ref[c], preferred_element_type=F32)
        pbuf[...] = p[:, 2 * CMP_HID:4 * CMP_HID]
        hid = p[0:r, 0:2 * CMP_HID] + pbuf[pl.ds(1, r), :] + b1_ref[c]
        act = jax.nn.gelu(hid)
        comp = jnp.dot(act.astype(BF16), w2_ref[c], preferred_element_type=F32) + b2_ref[c]
        if c == 0:
            comp = _rope128(comp, cos_ref[...], sin_ref[...], first)
            out = ck_ref
        else:
            out = cv_ref
        for k in range(N_KV):
            out[0, k] = comp[:, k * HD:(k + 1) * HD]


def _cmp(pages, pt_flat, nb, n_pages, w1p, b1p, w2p, b2p, cos_c, sin_c, name, tiles):
    ppt = min(32, n_pages)
    n_tiles = n_pages // ppt
    r = ppt * SUBLANES
    n_chunk = n_pages * SUBLANES
    zeros = (0,) * (pages.ndim - 1)

    def page_map(j):
        return lambda b, t, pt: (pt[b * n_pages + t * ppt + j],) + zeros

    def next_map(b, t, pt):
        return (pt[b * n_pages + jnp.minimum(t * ppt + ppt, n_pages - 1)],) + zeros

    page_blk = (None, 2, N_KV, HD, PAGE) if tiles else (1, SUBLANES, CHUNK_ROW)
    in_specs = [pl.BlockSpec(page_blk, page_map(j)) for j in range(ppt)]
    in_specs.append(pl.BlockSpec(page_blk, next_map))
    const = lambda shp: pl.BlockSpec(shp, lambda b, t, pt: (0,) * len(shp))
    extra = []
    if tiles:
        row = np.arange(PAGE)
        tok = (row % SUBLANES) * CMP_STRIDE + row // SUBLANES
        extra = [jnp.asarray(tok[:, None] == np.arange(PAGE)[None, :], BF16)]
        in_specs.append(const((PAGE, PAGE)))
    in_specs += [const(w1p.shape), const(b1p.shape), const(w2p.shape), const(b2p.shape),
                 pl.BlockSpec((r, LANES), lambda b, t, pt: (t, 0)), pl.BlockSpec((r, LANES), lambda b, t, pt: (t, 0))]
    hm = pl.BlockSpec((1, N_KV, r, HD), lambda b, t, pt: (b, 0, t, 0))
    grid_spec = pltpu.PrefetchScalarGridSpec(
        num_scalar_prefetch=1, grid=(nb, n_tiles), in_specs=in_specs, out_specs=[hm, hm],
        scratch_shapes=[pltpu.VMEM((2, r + SUBLANES, CMP_STRIDE * KV_W), F32),
                        pltpu.VMEM((r + SUBLANES, 2 * CMP_HID), F32)])
    return pl.pallas_call(
        functools.partial(_cmp_kernel, ppt=ppt, tiles=tiles),
        grid_spec=grid_spec,
        out_shape=[jax.ShapeDtypeStruct((nb, N_KV, n_chunk, HD), F32)] * 2,
        compiler_params=_params(2), name=name,
    )(pt_flat, *([pages] * (ppt + 1)), *extra, w1p, b1p, w2p, b2p, cos_c, sin_c)


def _softmax_rows(s, valid):
    s = jnp.where(valid, s, NEG_INF)
    m = jnp.max(s, axis=-1, keepdims=True)
    e = jnp.exp2(s - m)
    return e / jnp.sum(e, axis=-1, keepdims=True)


def _attn_p_kernel(q_ref, ck_ref, cv_ref, ks_ref, vs_ref, kw_ref, vw_ref, gate_ref, band_ref, exp_ref, o_ref,
                   *, n_cmp_pad, n_blk, kc, hg, wc):
    qb = pl.program_id(2)
    start = qb * Q_BLOCK
    tpos = start + lax.broadcasted_iota(I32, (Q_BLOCK, 1), 0)
    groups = range(QPK // hg)
    rows = hg * Q_BLOCK

    def q_of(g):
        return q_ref[0, g * hg:(g + 1) * hg].reshape(rows, HD)

    def biased(s, bias):
        width = s.shape[-1]
        return (s.reshape(hg, Q_BLOCK, width) + bias[None]).reshape(rows, width)

    ck = ck_ref[0, 0].astype(BF16)
    cv = cv_ref[0, 0].astype(BF16)
    cmp_end = (lax.broadcasted_iota(I32, (1, n_cmp_pad), 1) + 2) * CMP_STRIDE - 1
    bias_c = jnp.where(cmp_end <= tpos, 0.0, NEG_INF)
    o_c = []
    pcs = jnp.zeros((Q_BLOCK, n_cmp_pad), F32)
    for g in groups:
        s_c = biased(lax.dot_general(q_of(g), ck, _NT, preferred_element_type=F32), bias_c)
        m_c = jnp.maximum(jnp.max(s_c, axis=-1, keepdims=True), 0.5 * NEG_INF)
        e_c = jnp.exp2(s_c - m_c)
        l_c = jnp.sum(e_c, axis=-1, keepdims=True)
        p_c = e_c * (1.0 / jnp.where(l_c > 0.0, l_c, 1.0))
        o_c.append(jnp.dot(p_c.astype(BF16), cv, preferred_element_type=F32))
        for h in range(hg):
            pcs = pcs + p_c[h * Q_BLOCK:(h + 1) * Q_BLOCK]

    imp =lax.dot_general(band_ref[...], pcs, _NT, preferred_element_type=F32,
                          precision=lax.Precision.HIGHEST)
    blk = lax.broadcasted_iota(I32, (n_blk, Q_BLOCK), 0)
    tlane = start + lax.broadcasted_iota(I32, (1, Q_BLOCK), 1)
    cur = tlane // SEL_BLOCK
    causal = blk * SEL_BLOCK <= tlane
    forced = causal & ((blk == 0) | (blk == cur) | (blk == cur - 1))
    score = jnp.where(forced, FORCE_SCORE, jnp.where(causal, imp, -1.0))
    rank = jnp.zeros((n_blk, Q_BLOCK), F32)
    for bp in range(n_blk):
        other = score[bp:bp + 1, :]
        beats = (other > score) | ((other == score) & (bp < blk))
        rank = rank + beats.astype(F32)
    sel_t = (rank < float(min(N_SEL, n_blk))).astype(BF16)
    eye = (lax.broadcasted_iota(I32, (Q_BLOCK, Q_BLOCK), 0)
           == lax.broadcasted_iota(I32, (Q_BLOCK, Q_BLOCK), 1)).astype(BF16)
    sel = lax.dot_general(eye, sel_t, _NT, preferred_element_type=F32).astype(BF16)

    n_chunks = (start + Q_BLOCK + kc - 1) // kc

    def online(state, kj, vj, bias):
        out = []
        for g in groups:
            m_i, acc = state[g]
            s = biased(lax.dot_general(q_of(g), kj, _NT, preferred_element_type=F32), bias)
            m_new = jnp.maximum(m_i, jnp.max(s, axis=-1, keepdims=True))
            p = jnp.exp2(s - m_new).astype(BF16)
            out.append((m_new, jnp.exp2(m_i - m_new) * acc + jnp.dot(p, vj, preferred_element_type=F32)))
        return tuple(out)

    def step(j, state, causal_chunk):
        off = pl.multiple_of(j * kc, kc)
        mexp = jnp.dot(sel, exp_ref[j], preferred_element_type=F32)
        bias = mexp * (-NEG_INF) + NEG_INF
        if causal_chunk:
            keypos = off + lax.broadcasted_iota(I32, (1, kc), 1)
            bias = jnp.where(keypos <= tpos, bias, NEG_INF)
        return online(state, ks_ref[0, 0, pl.ds(off, kc), :], vs_ref[0, 0, pl.ds(off, kc), :], bias)

    init = tuple((jnp.full((rows, 1), NEG_INF, F32), jnp.zeros((rows, LANES), F32)) for _ in groups)
    state = lax.fori_loop(0, n_chunks - 1, lambda j, c: step(j, c, False), init)
    sel_state = step(n_chunks - 1, state, True)

    s0 = jnp.maximum(start - WINDOW, 0)
    win_state = init
    for c in range((WINDOW + Q_BLOCK) // wc):
        off = pl.multiple_of(s0 + c * wc, Q_BLOCK)
        dist = tpos - (off + lax.broadcasted_iota(I32, (1, wc), 1))
        bias_w = jnp.where((dist >= 0) & (dist <= WINDOW), 0.0, NEG_INF)
        win_state = online(win_state, kw_ref[0, 0, pl.ds(off, wc), :], vw_ref[0, 0, pl.ds(off, wc), :], bias_w)

    gt = gate_ref[...]
    for g in groups:
        acc_s = sel_state[g][1]
        acc_w = win_state[g][1]
        o_s = acc_s[:, 0:HD] * (1.0 / acc_s[:, HD:HD + 1])
        o_w = acc_w[:, 0:HD] * (1.0 / acc_w[:, HD:HD + 1])
        for hh in range(hg):
            h = g * hg + hh
            rs = slice(hh * Q_BLOCK, (hh + 1) * Q_BLOCK)
            o = (gt[:, h:h + 1] * o_c[g][rs] + gt[:, QPK + h:QPK + h + 1] * o_s[rs]
                 + gt[:, 2 * QPK + h:2 * QPK + h + 1] * o_w[rs])
            o_ref[0, :, h * HD:(h + 1) * HD] = o


ATTN_HEAD_GROUP = 4
ATTN_KEY_CHUNK = 512
ATTN_WIN_CHUNK = 640


def _attn_prompt(q_hm, ck, cv, ks, vs, kw, vw, gates, band, expand, *, nb, t):
    n_qb = t // Q_BLOCK
    n_cmp_pad = ck.shape[2]
    n_blk = band.shape[0]
    kc = expand.shape[2]
    kv_spec = lambda n, w=HD: pl.BlockSpec((1, 1, n, w), lambda b, k, i: (b, k, 0, 0))
    return pl.pallas_call(
        functools.partial(_attn_p_kernel, n_cmp_pad=n_cmp_pad, n_blk=n_blk, kc=kc, hg=ATTN_HEAD_GROUP, wc=ATTN_WIN_CHUNK),
        grid=(nb, N_KV, n_qb),
        in_specs=[pl.BlockSpec((1, QPK, Q_BLOCK, HD), lambda b, k, i: (b, k, i, 0)),
                  kv_spec(n_cmp_pad), kv_spec(n_cmp_pad), kv_spec(t), kv_spec(t, LANES), kv_spec(t), kv_spec(t, LANES),
                  pl.BlockSpec((Q_BLOCK, LANES), lambda b, k, i: (b * n_qb + i, k)),
                  pl.BlockSpec(band.shape, lambda b, k, i: (0, 0)),
                  pl.BlockSpec(expand.shape, lambda b, k, i: (0, 0, 0))],
        out_specs=pl.BlockSpec((1, Q_BLOCK, QPK * HD), lambda b, k, i: (b, i, k)),
        out_shape=jax.ShapeDtypeStruct((nb, t, ATTN_W), F32),
        compiler_params=_params(3), name="attn_prompt",
    )(q_hm, ck, cv, ks, vs, kw, vw, gates, band, expand)


def _attn_s1_kernel(q_ref, ck_ref, cv_ref, band_ref, oc_ref, imp_ref, *, n_chunk, past):
    q = q_ref[0]
    q16 = jnp.concatenate([q, jnp.zeros_like(q)], axis=0).astype(BF16)
    cmp_end = (lax.broadcasted_iota(I32, (1, n_chunk), 1) + 2) * CMP_STRIDE - 1
    valid = cmp_end <= past
    head = lax.broadcasted_iota(I32, (2 * N_HEADS, 1), 0)
    oc = jnp.zeros((2 * N_HEADS, HD), F32)
    imps = []
    for k in range(N_KV):
        s = lax.dot_general(q16, ck_ref[0, k].astype(BF16), _NT, preferred_element_type=F32)
        p = _softmax_rows(s, valid) * valid.astype(F32)
        in_grp = (head >= k * QPK) & (head < (k + 1) * QPK)
        p = jnp.where(in_grp, p, 0.0)
        oc = oc + jnp.dot(p.astype(BF16), cv_ref[0, k].astype(BF16), preferred_element_type=F32)
        pcs = jnp.sum(p, axis=0, keepdims=True)
        pcs8 = jnp.broadcast_to(pcs, (SUBLANES, n_chunk))
        imps.append(jnp.dot(pcs8, band_ref[...], preferred_element_type=F32,
                            precision=lax.Precision.HIGHEST)[0:1])
    oc_ref[0] = oc[0:N_HEADS]
    imp_ref[0] = jnp.concatenate(imps + [jnp.zeros((SUBLANES - N_KV, imps[0].shape[1]), F32)], axis=0)


def _topk_s_kernel(imp_ref, idx_ref, *, n_sel_blocks, past):
    imp = imp_ref[...]
    rows, nbp = imp.shape
    blk = lax.broadcasted_iota(I32, (rows, nbp), 1)
    cur = past // SEL_BLOCK
    causal = blk * SEL_BLOCK <= past
    forced = causal & ((blk == 0) | (blk == cur) | (blk == cur - 1))
    score = jnp.where(forced, FORCE_SCORE, jnp.where(causal, imp, -1.0))
    score = jnp.where(blk < n_sel_blocks, score, -2.0)
    lane = lax.broadcasted_iota(I32, (rows, LANES), 1)
    out = jnp.zeros((rows, LANES), I32)
    for r in range(min(N_SEL, n_sel_blocks)):
        m = jnp.max(score, axis=-1, keepdims=True)
        pick = jnp.min(jnp.where(score == m, blk, nbp), axis=-1, keepdims=True)
        out = jnp.where(lane == r, pick, out)
        score = jnp.where(blk == pick, -3.0, score)
    idx_ref[...] = out


def _attn_s2_kernel(pt_ref, idx_ref, *refs, n_pages, past, n_sel_blocks):
    ktiles, vtiles = refs[:N_SEL], refs[N_SEL:2 * N_SEL]
    q_ref, oc_ref, kvs_ref, wk_ref, wv_ref, kvw_ref, gate_ref, o_ref, kbuf, vbuf = refs[2 * N_SEL:]
    b = pl.program_id(0)
    k = pl.program_id(1)
    q = q_ref[0]
    q16f = jnp.concatenate([q, jnp.zeros_like(q)], axis=0)
    q16 = q16f.astype(BF16)
    head = lax.broadcasted_iota(I32, (N_HEADS, 1), 0)
    nk = N_SEL * PAGE
    lane = lax.broadcasted_iota(I32, (1, nk), 1)
    slot = lane // PAGE
    new_blk = n_sel_blocks - 1
    wb = wk_ref.shape[-1]
    wpos = past - wb + lax.broadcasted_iota(I32, (1, wb), 1)
    wdist = past - wpos
    valid_w = (wdist >= 0) & (wdist <= WINDOW) & (wpos >= 0)

    def attend(s, valid, v_t, k_new, v_new):
        s_new = jnp.sum(q16f * k_new, axis=-1, keepdims=True)
        s = jnp.where(valid, s, NEG_INF)
        m = jnp.maximum(jnp.max(s, axis=-1, keepdims=True), s_new)
        e = jnp.exp2(s - m)
        e_new = jnp.exp2(s_new - m)
        den = jnp.sum(e, axis=-1, keepdims=True) + e_new
        acc = lax.dot_general(e.astype(BF16), v_t, _NT, preferred_element_type=F32) + e_new * v_new
        return acc / den

    in_grp = (head >= k * QPK) & (head < (k + 1) * QPK)
    bvec = jnp.zeros((1, nk), I32)
    for j in range(N_SEL):
        kbuf[:, j * PAGE:(j + 1) * PAGE] = ktiles[j][...].astype(BF16)
        vbuf[:, j * PAGE:(j + 1) * PAGE] = vtiles[j][...].astype(BF16)
        bvec = jnp.where(slot == j, idx_ref[(b * N_KV + k) * LANES + j], bvec)
    tok = (bvec // 2) * PAGE + lane % PAGE
    valid = (tok // SEL_BLOCK == bvec) & (bvec < new_blk) & (tok <= past)
    s = jnp.dot(q16, kbuf[...], preferred_element_type=F32)
    o_s = attend(s, valid, vbuf[...], kvs_ref[0, pl.ds(k, 1), :], kvs_ref[0, pl.ds(N_KV + k, 1), :])
    sw = jnp.dot(q16, wk_ref[...].astype(BF16), preferred_element_type=F32)
    o_w = attend(sw, valid_w, wv_ref[...].astype(BF16), kvw_ref[0, pl.ds(k, 1), :], kvw_ref[0, pl.ds(N_KV + k, 1), :])
    g = gate_ref[0]
    part = jnp.where(in_grp, g[:, 1:2] * o_s[0:N_HEADS] + g[:, 2:3] * o_w[0:N_HEADS], 0.0)

    @pl.when(k == 0)
    def _():
        o_ref[0] = g[:, 0:1] * oc_ref[0] + part

    @pl.when(k > 0)
    def _():
        o_ref[0] = o_ref[0] + part


def _attn_sample(q3, ck, cv, band_s, sel_t, pt_flat, kvs_rows, win_t, kvw_rows, gates_hm,
                 *, nb, n_pages, past, n_sel_blocks):
    n_chunk = ck.shape[2]
    nbp = band_s.shape[1]
    oc, imp = pl.pallas_call(
        functools.partial(_attn_s1_kernel, n_chunk=n_chunk, past=past),
        grid=(nb,),
        in_specs=[pl.BlockSpec((1, N_HEADS, HD), lambda b: (b, 0, 0)),
                  pl.BlockSpec((1, N_KV, n_chunk, HD), lambda b: (b, 0, 0, 0)),
                  pl.BlockSpec((1, N_KV, n_chunk, HD), lambda b: (b, 0, 0, 0)),
                  pl.BlockSpec(band_s.shape, lambda b: (0, 0))],
        out_specs=[pl.BlockSpec((1, N_HEADS, HD), lambda b: (b, 0, 0)),
                   pl.BlockSpec((1, SUBLANES, nbp), lambda b: (b, 0, 0))],
        out_shape=[jax.ShapeDtypeStruct((nb, N_HEADS, HD), F32), jax.ShapeDtypeStruct((nb, SUBLANES, nbp), F32)],
        compiler_params=_params(1), name="attn_sample_cmp",
    )(q3, ck, cv, band_s)
    imp2 = imp[:, 0:N_KV, :].reshape(nb * N_KV, nbp)
    idx = pl.pallas_call(
        functools.partial(_topk_s_kernel, n_sel_blocks=n_sel_blocks, past=past),
        out_shape=jax.ShapeDtypeStruct((nb * N_KV, LANES), I32),
        compiler_params=pltpu.CompilerParams(vmem_limit_bytes=VMEM_LIMIT), name="topk_sample",
    )(imp2)
    idx_flat = idx.reshape(-1)

    def tile_map(c, j):
        def f(b, k, pt, ix):
            bidx = ix[(b * N_KV + k) * LANES + j]
            return (pt[b * n_pages + jnp.minimum(bidx // 2, n_pages - 1)], c, k, 0, 0)
        return f

    tile = lambda c, j: pl.BlockSpec((None, None, None, HD, PAGE), tile_map(c, j))
    in_specs = [tile(0, j) for j in range(N_SEL)] + [tile(1, j) for j in range(N_SEL)]
    wb = win_t.shape[-1]
    per_b = lambda shp: pl.BlockSpec(shp, lambda b, k, pt, ix: (b, 0, 0))
    in_specs += [per_b((1, N_HEADS, HD)), per_b((1, N_HEADS, HD)), per_b((1, SUBLANES, HD)),
                 pl.BlockSpec((None, None, None, HD, wb), lambda b, k, pt, ix: (b, 0, k, 0, 0)),
                 pl.BlockSpec((None, None, None, HD, wb), lambda b, k, pt, ix: (b, 1, k, 0, 0)),
                 per_b((1, SUBLANES, HD)), per_b((1, N_HEADS, LANES))]
    grid_spec = pltpu.PrefetchScalarGridSpec(
        num_scalar_prefetch=2, grid=(nb, N_KV), in_specs=in_specs,
        out_specs=per_b((1, N_HEADS, HD)),
        scratch_shapes=[pltpu.VMEM((HD, N_SEL * PAGE), BF16), pltpu.VMEM((HD, N_SEL * PAGE), BF16)])
    return pl.pallas_call(
        functools.partial(_attn_s2_kernel, n_pages=n_pages, past=past, n_sel_blocks=n_sel_blocks),
        grid_spec=grid_spec,
        out_shape=jax.ShapeDtypeStruct((nb, N_HEADS, HD), F32),
        compiler_params=_params(2), name="attn_sample_sel",
    )(pt_flat, idx_flat, *([sel_t] * (2 * N_SEL)), q3, oc, kvs_rows, win_t, win_t, kvw_rows, gates_hm)


TOK_ROWS = D_MODEL // LANES


def _store_token_tiles(ref, x):
    n = x.shape[0]
    for j in range(TOK_ROWS):
        ref[pl.ds(j, n, stride=TOK_ROWS), :] = x[:, j * LANES:(j + 1) * LANES]


def _load_token_tiles(ref, lead, n):
    return jnp.concatenate([ref[lead + (pl.ds(j, n, stride=TOK_ROWS), slice(None))] for j in range(TOK_ROWS)], axis=1)


def _outp_kernel(xp_ref, convp_ref, attnp_ref, ga1p_ref, sc2p_ref, sh2p_ref,
                 xs_ref, convs_ref, attns_ref, ga1s_ref, sc2s_ref, sh2s_ref,
                 gc_ref, ga_ref, w_ref, g2_ref, wr_ref, x1_ref, hp_ref, lg_ref, *, n_prompt_tiles):
    is_p = pl.program_id(0) < n_prompt_tiles
    pick = lambda a, b: jnp.where(is_p, a, b)
    cn = _rms(pick(convp_ref[...], convs_ref[...]), gc_ref[...])
    an = _rms(pick(attnp_ref[...], attns_ref[...]), ga_ref[...])
    cat = jnp.concatenate([cn, an], axis=1).astype(BF16)
    y = jnp.dot(cat, w_ref[...], preferred_element_type=F32)
    x1 = pick(xp_ref[...], xs_ref[...]) + pick(ga1p_ref[0], ga1s_ref[0]) * y
    x1_ref[...] = x1
    hp = _rms(x1, g2_ref[...]) * (1.0 + pick(sc2p_ref[0], sc2s_ref[0])) + pick(sh2p_ref[0], sh2s_ref[0])
    _store_token_tiles(hp_ref, hp)
    lg_ref[...] = jnp.dot(hp, wr_ref[...], preferred_element_type=F32, precision=lax.Precision.HIGHEST)


TOKEN_TILE = 512


def _outp(prompt, sample, g_conv, g_attn, w_out_b, g2, w_route, *, tpb):
    tm = TOKEN_TILE
    n_p = prompt[0].shape[0] // tm
    total = (n_p + 1) * tm
    last = n_p - 1
    prow = lambda w: pl.BlockSpec((tm, w), lambda i: (jnp.minimum(i, last), 0))
    srow = lambda w: pl.BlockSpec((tm, w), lambda i: (0, 0))
    pmod = pl.BlockSpec((1, 1, D_MODEL), lambda i: (jnp.minimum(i, last) // tpb, 0, 0))
    smod = pl.BlockSpec((1, tm, D_MODEL), lambda i: (0, 0, 0))
    vec = lambda w: pl.BlockSpec((1, w), lambda i: (0, 0))
    row = lambda w: pl.BlockSpec((tm, w), lambda i: (i, 0))
    in_specs = [prow(D_MODEL), prow(CONV_W), prow(ATTN_W), pmod, pmod, pmod,
                srow(D_MODEL), srow(CONV_W), srow(ATTN_W), smod, smod, smod,
                vec(CONV_W), vec(ATTN_W), pl.BlockSpec((D_MODEL, D_MODEL), lambda i: (0, 0)), vec(D_MODEL),
                pl.BlockSpec((D_MODEL, LANES), lambda i: (0, 0))]
    return pl.pallas_call(
        functools.partial(_outp_kernel, n_prompt_tiles=n_p),
        grid=(n_p + 1,), in_specs=in_specs,
        out_specs=[row(D_MODEL), pl.BlockSpec((tm * TOK_ROWS, LANES), lambda i: (i, 0)), row(LANES)],
        out_shape=[jax.ShapeDtypeStruct((total, D_MODEL), F32), jax.ShapeDtypeStruct((total * TOK_ROWS, LANES), F32),
                   jax.ShapeDtypeStruct((total, LANES), F32)],
        compiler_params=_params(1), name="outp",
    )(*prompt, *sample, g_conv.reshape(1, -1), g_attn.reshape(1, -1), w_out_b, g2.reshape(1, -1), w_route)


def _route_kernel(lg_ref, bias_ref, tri_ref, o_ref, cnt_ref, carry, *, tm, n_valid):
    i = pl.program_id(0)

    @pl.when(i == 0)
    def _():
        carry[...] = jnp.zeros_like(carry)

    lane = lax.broadcasted_iota(I32, (tm, LANES), 1)
    rowid = i * tm + lax.broadcasted_iota(I32, (tm, 1), 0)
    live = rowid < n_valid
    lg = lg_ref[...] + bias_ref[...]
    is_g = lane < N_GROUPS
    lgg = jnp.where(is_g, lg, NEG_INF)
    gmax = jnp.max(lgg, axis=-1, keepdims=True)
    grp = jnp.min(jnp.where(is_g & (lgg == gmax), lane, LANES), axis=-1, keepdims=True)
    p_grp = 1.0 / jnp.sum(jnp.where(is_g, jnp.exp(lgg - gmax), 0.0), axis=-1, keepdims=True)
    eid = lane - N_GROUPS
    in_grp = (eid >= grp * EPG) & (eid < (grp + 1) * EPG)
    le = jnp.where(in_grp, lg, NEG_INF)
    v1 = jnp.max(le, axis=-1, keepdims=True)
    e1 = jnp.min(jnp.where(in_grp & (le == v1), eid, LANES), axis=-1, keepdims=True)
    le2 = jnp.where(eid == e1, NEG_INF, le)
    v2 = jnp.max(le2, axis=-1, keepdims=True)
    e2 = jnp.min(jnp.where(in_grp & (eid != e1) & (le2 == v2), eid, LANES), axis=-1, keepdims=True)
    ex2 = jnp.exp(v2 - v1)
    w1 = p_grp * (1.0 / (1.0 + ex2))
    w2 = p_grp * (ex2 / (1.0 + ex2))
    oh1 = ((lane == e1) & live).astype(F32)
    oh2 = ((lane == e2) & live).astype(F32)
    both = oh1 + oh2
    before = jnp.dot(tri_ref[...], both.astype(BF16), preferred_element_type=F32) + carry[0:1, :]
    r1 = jnp.sum(oh1 * before, axis=-1, keepdims=True)
    r2 = jnp.sum(oh2 * before, axis=-1, keepdims=True)
    carry[0:1, :] = carry[0:1, :] + jnp.sum(both, axis=0, keepdims=True)
    out = jnp.where(lane == 0, e1.astype(F32), 0.0)
    out = jnp.where(lane == 1, e2.astype(F32), out)
    out = jnp.where(lane == 2, w1, out)
    out = jnp.where(lane == 3, w2, out)
    out = jnp.where(lane == 4, r1, out)
    out = jnp.where(lane == 5, r2, out)
    o_ref[...] = out
    cnt_ref[...] = carry[...]


def _route(logits, bias_row, n_valid):
    total = logits.shape[0]
    tm = TOKEN_TILE
    n_tiles = total // tm
    tri =(np.arange(tm)[:, None] > np.arange(tm)[None, :]).astype(np.float32)
    return pl.pallas_call(
        functools.partial(_route_kernel, tm=tm, n_valid=n_valid),
        grid=(n_tiles,),
        in_specs=[pl.BlockSpec((tm, LANES), lambda i: (i, 0)), pl.BlockSpec((1, LANES), lambda i: (0, 0)),
                  pl.BlockSpec((tm, tm), lambda i: (0, 0))],
        out_specs=[pl.BlockSpec((tm, LANES), lambda i: (i, 0)), pl.BlockSpec((SUBLANES, LANES), lambda i: (0, 0))],
        out_shape=[jax.ShapeDtypeStruct((total, LANES), F32), jax.ShapeDtypeStruct((SUBLANES, LANES), F32)],
        scratch_shapes=[pltpu.VMEM((SUBLANES, LANES), F32)],
        compiler_params=_params(1), name="route",
    )(logits, bias_row, jnp.asarray(tri, BF16))


EXPERT_ROWS = 256
DISPATCH_TILE = 256


def _tile_copy(src, src_row, dst, dst_row, sem):
    return pltpu.make_async_copy(src.at[pl.ds(pl.multiple_of(src_row * TOK_ROWS, TOK_ROWS), TOK_ROWS), :],
                                 dst.at[pl.ds(pl.multiple_of(dst_row * TOK_ROWS, TOK_ROWS), TOK_ROWS), :], sem)


def _dispatch_kernel(zstart_ref, zcnt_ref, dest_ref, x_hbm, xb_hbm, zeros, sem, zsem, *, n_tiles):
    i = pl.program_id(0)
    tm = DISPATCH_TILE

    @pl.when(i == 0)
    def _():
        zeros[...] = jnp.zeros_like(zeros)
        for e in range(N_EXPERTS):
            def fill(r, c, e=e):
                _tile_copy(zeros, 0, xb_hbm, zstart_ref[e] + r, zsem).start()
                return c
            lax.fori_loop(0, zcnt_ref[e], fill, 0)
        for e in range(N_EXPERTS):
            def drain(r, c):
                _tile_copy(zeros, 0, xb_hbm, 0, zsem).wait()
                return c
            lax.fori_loop(0, zcnt_ref[e], drain, 0)

    def wait_tile(s):
        for _ in range(2 * tm):
            _tile_copy(x_hbm, 0, xb_hbm, 0, sem.at[s]).wait()

    for r in range(tm):
        for k in range(2):
            _tile_copy(x_hbm, i * tm + r, xb_hbm, dest_ref[0, 0, 2 * r + k], sem.at[i % 2]).start()

    @pl.when(i > 0)
    def _():
        wait_tile((i + 1) % 2)

    @pl.when(i == n_tiles - 1)
    def _():
        wait_tile(i % 2)


def _dispatch(dest_pad, zstart, zcnt, hp_all, n_rows):
    tm = DISPATCH_TILE
    n_tiles = dest_pad.shape[0] // tm
    grid_spec = pltpu.PrefetchScalarGridSpec(
        num_scalar_prefetch=2, grid=(n_tiles,),
        in_specs=[pl.BlockSpec((1, 1, 2 * tm), lambda i, zs, zc: (i, 0, 0), memory_space=pltpu.SMEM),
                  pl.BlockSpec(memory_space=pl.ANY)],
        out_specs=pl.BlockSpec(memory_space=pl.ANY),
        scratch_shapes=[pltpu.VMEM((TOK_ROWS, LANES), F32), pltpu.SemaphoreType.DMA((2,)), pltpu.SemaphoreType.DMA])
    return pl.pallas_call(
        functools.partial(_dispatch_kernel, n_tiles=n_tiles),
        grid_spec=grid_spec,
        out_shape=jax.ShapeDtypeStruct((n_rows * TOK_ROWS, LANES), F32),
        compiler_params=_params(1), name="dispatch",
    )(zstart, zcnt, dest_pad.reshape(n_tiles, 1, 2 * tm), hp_all)


def _experts_kernel(blk_e_ref, x_ref, wg_ref, wu_ref, wd_ref, o_ref, wg_b, wu_b, wd_b):
    i = pl.program_id(0)
    changed = jnp.logical_or(i == 0, blk_e_ref[i] != blk_e_ref[jnp.maximum(i - 1, 0)])

    @pl.when(changed)
    def _():
        wg_b[...] = wg_ref[0].astype(BF16)
        wu_b[...] = wu_ref[0].astype(BF16)
        wd_b[...] = wd_ref[0].astype(BF16)

    x = _load_token_tiles(x_ref, (), EXPERT_ROWS).astype(BF16)
    g = jnp.dot(x, wg_b[...], preferred_element_type=F32)
    u = jnp.dot(x, wu_b[...], preferred_element_type=F32)
    h = (g * jax.nn.sigmoid(g)) * u
    _store_token_tiles(o_ref, jnp.dot(h.astype(BF16), wd_b[...], preferred_element_type=F32))


def _experts(blk_e, xb, w_gate, w_up, w_down, n_blocks):
    blk = pl.BlockSpec((EXPERT_ROWS * TOK_ROWS, LANES), lambda i, be: (i, 0))
    grid_spec = pltpu.PrefetchScalarGridSpec(
        num_scalar_prefetch=1, grid=(n_blocks,),
        in_specs=[blk,
                  pl.BlockSpec((1, D_MODEL, D_EXPERT), lambda i, be: (be[i], 0, 0)),
                  pl.BlockSpec((1, D_MODEL, D_EXPERT), lambda i, be: (be[i], 0, 0)),
                  pl.BlockSpec((1, D_EXPERT, D_MODEL), lambda i, be: (be[i], 0, 0))],
        out_specs=blk,
        scratch_shapes=[pltpu.VMEM((D_MODEL, D_EXPERT), BF16), pltpu.VMEM((D_MODEL, D_EXPERT), BF16),
                        pltpu.VMEM((D_EXPERT, D_MODEL), BF16)])
    return pl.pallas_call(
        _experts_kernel,
        grid_spec=grid_spec,
        out_shape=jax.ShapeDtypeStruct((n_blocks * EXPERT_ROWS * TOK_ROWS, LANES), F32),
        compiler_params=_params(1), name="experts",
    )(blk_e, xb, w_gate, w_up, w_down)


def _final_kernel(dest_first_ref, dest_next_ref, yb_hbm, x1_ref, wt_ref, gate2_ref, gf_ref, o_ref, ybuf, sem,
                  *, tm, n_tiles):
    i = pl.program_id(0)
    slot = i % 2

    def issue(dest_ref, s):
        for r in range(tm):
            for k in range(2):
                d = dest_ref[0, 0, 2 * r + k]
                src = yb_hbm.at[pl.ds(pl.multiple_of(d * TOK_ROWS, TOK_ROWS), TOK_ROWS), :]
                pltpu.make_async_copy(src, ybuf.at[s, k, pl.ds(r * TOK_ROWS, TOK_ROWS), :], sem.at[s]).start()

    @pl.when(i == 0)
    def _():
        issue(dest_first_ref, 0)

    @pl.when(i + 1 < n_tiles)
    def _():
        issue(dest_next_ref, 1 - slot)

    for r in range(tm):
        for k in range(2):
            pltpu.make_async_copy(yb_hbm.at[pl.ds(0, TOK_ROWS), :], ybuf.at[slot, k, pl.ds(r * TOK_ROWS, TOK_ROWS), :],
                                  sem.at[slot]).wait()
    wt = wt_ref[...]
    f = wt[:, 2:3] * _load_token_tiles(ybuf, (slot, 0), tm) + wt[:, 3:4] * _load_token_tiles(ybuf, (slot, 1), tm)
    x2 = x1_ref[...] + gate2_ref[0] * f
    o_ref[...] = _rms(x2, gf_ref[...])


def _final(dest_pad, yb, x1_all, route_rows, gate2, final_g, *, rows, tpb, per_row, row0):
    tm = min(256, rows)
    n_tiles = rows // tm
    blk0 = row0 // tm
    dest3 = dest_pad.reshape(-1, 1, 2 * tm)
    idx_blk = lambda f: pl.BlockSpec((1, 1, 2 * tm), f, memory_space=pltpu.SMEM)
    mod = (pl.BlockSpec((1, tm, D_MODEL), lambda i: (0, i, 0)) if per_row
           else pl.BlockSpec((1, 1, D_MODEL), lambda i: (i // tpb, 0, 0)))
    return pl.pallas_call(
        functools.partial(_final_kernel, tm=tm, n_tiles=n_tiles),
        grid=(n_tiles,),
        in_specs=[idx_blk(lambda i: (blk0, 0, 0)),
                  idx_blk(lambda i: (blk0 + jnp.minimum(i + 1, n_tiles - 1), 0, 0)),
                  pl.BlockSpec(memory_space=pl.ANY),
                  pl.BlockSpec((tm, D_MODEL), lambda i: (blk0 + i, 0)),
                  pl.BlockSpec((tm, LANES), lambda i: (blk0 + i, 0)),
                  mod, pl.BlockSpec((1, D_MODEL), lambda i: (0, 0))],
        out_specs=pl.BlockSpec((tm, D_MODEL), lambda i: (i, 0)),
        scratch_shapes=[pltpu.VMEM((2, 2, tm * TOK_ROWS, LANES), F32), pltpu.SemaphoreType.DMA((2,))],
        out_shape=jax.ShapeDtypeStruct((rows, D_MODEL), F32),
        compiler_params=_params(1), name="final_sample" if per_row else "final_prompt",
    )(dest3, dest3, yb, x1_all, route_rows, gate2, final_g.reshape(1, -1))


def _rope_tables(pos):
    inv = ROPE_THETA ** (-jnp.arange(HALF, dtype=F32) / HALF)
    ang = pos.astype(F32)[:, None] * inv[None, :]
    cos = jnp.tile(jnp.cos(ang), (1, LANES // HALF))
    sin = jnp.sin(ang)
    sin_s = jnp.tile(jnp.concatenate([-sin, sin], axis=1), (1, LANES // HD))
    return cos, sin_s


def _pack_w_in(w_in):
    gl = w_in[:, _C_G:_C_G + 3 * N_HEADS].reshape(D_MODEL, 3, N_KV, QPK)
    gcols = []
    for k in range(N_KV):
        gk = gl[:, :, k, :].reshape(D_MODEL, 3 * QPK)
        gcols.append(jnp.pad(gk, ((0, 0), (0, LANES - 3 * QPK))))
    return jnp.concatenate([w_in[:, :_C_G]] + gcols, axis=1).astype(BF16)


def _pack_cmp_weights(cmp_w1, cmp_w2, bias, cmp_b2):
    w1 = cmp_w1.reshape(2, 2, CMP_STRIDE, HD, CMP_HID)
    eye = jnp.eye(N_KV, dtype=F32)
    w1p = jnp.einsum('crsdh,pk->cspdrkh', w1, eye).reshape(2, CMP_STRIDE * KV_W, 2 * N_KV * CMP_HID)
    w2p = jnp.einsum('chd,pk->cphkd', cmp_w2, eye).reshape(2, N_KV * CMP_HID, KV_W)
    b1p = jnp.tile(bias, (1, N_KV)).reshape(2, 1, N_KV * CMP_HID)
    b2p = jnp.tile(cmp_b2, (1, N_KV)).reshape(2, 1, KV_W)
    return w1p.astype(BF16), b1p, w2p.astype(BF16), b2p


def _band(n_cmp_pad, n_cmp, n_blk_pad, n_blk):
    n = np.arange(n_cmp_pad)[:, None]
    b = np.arange(n_blk_pad)[None, :]
    r = SEL_BLOCK // CMP_STRIDE
    m = (n >= r * b - 1) & (n <= r * b + r - 1) & (n < n_cmp) & (b < n_blk)
    return jnp.asarray(m.astype(np.float32))


def _expand(t, kc):
    n_chunks = t // kc
    key = np.arange(t).reshape(n_chunks, 1, kc)
    blk = np.arange(t // SEL_BLOCK).reshape(1, -1, 1)
    return jnp.asarray((key // SEL_BLOCK == blk).astype(np.float32), BF16)


def kernel(x_prompt, x_sample, c_prompt, c_sample, cache_cmp_kv, cache_sel_kv, cache_win_kv, state_conv, page_table,
           ln1_g, ln2_g, w_ada, b_ada, w_in, w_conv, cmp_pos, cmp_w1, cmp_b1, cmp_w2, cmp_b2, g_out_conv, g_out_attn,
           w_out, w_route_group, b_route_group, w_route_expert, b_route_expert, w_gate, w_up, w_down, final_g):
    depth = w_in.shape[0]
    assert depth == 1, "single-layer step"
    nb, t, _ = x_prompt.shape
    ns, ts, _ = x_sample.shape
    assert ts == 1 and t % 512 == 0 and t >= WINDOW + Q_BLOCK
    n_pool = cache_cmp_kv.shape[1]
    n_pages = page_table.shape[1]
    past = n_pages * PAGE
    wb = cache_win_kv.shape[2]
    assert wb == WINDOW
    l = 0

    n_c = nb + ns
    c_all = jnp.pad(jnp.concatenate([c_prompt, c_sample], axis=0), ((0, (-n_c) % SUBLANES), (0, 0)))
    mods = _ada(c_all, w_ada[l], b_ada[l])
    sh1, sc1, ga1, sh2, sc2, ga2 = [mods[:, j * D_MODEL:(j + 1) * D_MODEL] for j in range(6)]
    pr = lambda a: a[0:nb].reshape(nb, 1, D_MODEL)
    sr = lambda a: a[nb:nb + ns].reshape(1, ns, D_MODEL)

    w_pack = _pack_w_in(w_in[l])
    wconv8 = jnp.pad(w_conv[l], ((0, SUBLANES - CONV_K), (0, 0)))
    cos_p, sin_p = _rope_tables(jnp.arange(t, dtype=I32))
    cos_s, sin_s = _rope_tables(jnp.full((1,), past, I32))
    xp2 = x_prompt.reshape(nb * t, D_MODEL)
    xs2 = x_sample.reshape(ns, D_MODEL)
    (conv_p, cst_p, q_p, kvc_p, kvc_rows_p, kvs_rows_p, kvw_rows_p, ks_p, vs_p, kw_p, vw_p, gates_p) = _proj(
        xp2, ln1_g[l], pr(sc1), pr(sh1), w_pack, wconv8, cos_p, sin_p, nb=nb, t=t, sample=False)
    (conv_s, cst_s, q_s, _, kvc_rows_s, kvs_rows_s, kvw_rows_s, _, _, _, _, gates_s) = _proj(
        xs2, ln1_g[l], sr(sc1), sr(sh1), w_pack, wconv8, cos_s, sin_s, nb=ns, t=1, sample=True,
        prev=(state_conv[l][:, 0], state_conv[l][:, 1]))

    bias = _cmpbias(cmp_pos[l], cmp_w1[l], cmp_b1[l])
    w1p, b1p, w2p, b2p = _pack_cmp_weights(cmp_w1[l], cmp_w2[l], bias, cmp_b2[l])
    pp = t // PAGE
    cos_cp, sin_cp = _rope_tables((jnp.arange(t // CMP_STRIDE, dtype=I32) + 2) * CMP_STRIDE - 1)
    ck_p, cv_p = _cmp(kvc_p.reshape(nb * pp, SUBLANES, CHUNK_ROW), jnp.arange(nb * pp, dtype=I32), nb, pp,
                      w1p, b1p, w2p, b2p, cos_cp, sin_cp, "cmp_prompt", tiles=False)
    pt_flat = page_table.reshape(-1).astype(I32)
    cos_cs, sin_cs = _rope_tables((jnp.arange(past // CMP_STRIDE, dtype=I32) + 2) * CMP_STRIDE - 1)
    to_tiles = lambda a: a.transpose(0, 2, 3, 4, 1)
    ck_s, cv_s = _cmp(to_tiles(cache_cmp_kv[l]), pt_flat, ns, n_pages,
                      w1p, b1p, w2p, b2p, cos_cs, sin_cs, "cmp_sample", tiles=True)

    n_chunk_p = t // CMP_STRIDE
    n_blk_p = t // SEL_BLOCK
    band_p = _band(n_chunk_p, n_chunk_p - 1, n_blk_p, n_blk_p)
    attn_p = _attn_prompt(q_p, ck_p, cv_p, ks_p, vs_p, kw_p, vw_p, gates_p, band_p.T, _expand(t, ATTN_KEY_CHUNK), nb=nb, t=t)

    n_chunk_s = past // CMP_STRIDE
    n_sel_s = -(-(past + 1) // SEL_BLOCK)
    nbp = -(-n_sel_s // LANES) * LANES
    band_s = _band(n_chunk_s, (past + 1) // CMP_STRIDE - 1, nbp, n_sel_s)
    q3 = q_s.reshape(N_HEADS, ns, HD).transpose(1, 0, 2).astype(F32)
    gs = gates_s.reshape(ns, N_KV, LANES)[:, :, :3 * QPK].reshape(ns, N_KV, 3, QPK)
    gates_hm = jnp.pad(gs.transpose(0, 1, 3, 2).reshape(ns, N_HEADS, 3), ((0, 0), (0, 0), (0, LANES - 3)))
    rpt = 2 * N_KV
    new_rows = lambda a: jnp.pad(a.reshape(ns, rpt, HD), ((0, 0), (0, SUBLANES - rpt), (0, 0)))
    attn_s = _attn_sample(q3, ck_s, cv_s, band_s, to_tiles(cache_sel_kv[l]), pt_flat,
                          new_rows(kvs_rows_s), to_tiles(cache_win_kv[l]), new_rows(kvw_rows_s),
                          gates_hm, nb=ns, n_pages=n_pages, past=past, n_sel_blocks=n_sel_s).reshape(ns, ATTN_W)

    total = nb * t + ns
    w_out_b = w_out[l].astype(BF16)
    w_route = jnp.pad(jnp.concatenate([w_route_group[l], w_route_expert[l]], axis=1),
                      ((0, 0), (0, LANES - N_GROUPS - N_EXPERTS)))
    b_route = jnp.pad(jnp.concatenate([b_route_group[l], b_route_expert[l]]), (0, LANES - N_GROUPS - N_EXPERTS))
    tile_pad = lambda a: jnp.pad(a, ((0, TOKEN_TILE - ns), (0, 0)))
    smod = lambda a: tile_pad(a[nb:nb + ns]).reshape(1, TOKEN_TILE, D_MODEL)
    x1_all, hp_all, lg_all = _outp(
        (xp2, conv_p, attn_p.reshape(nb * t, ATTN_W), pr(ga1), pr(sc2), pr(sh2)),
        (tile_pad(xs2), tile_pad(conv_s), tile_pad(attn_s), smod(ga1), smod(sc2), smod(sh2)),
        g_out_conv[l], g_out_attn[l], w_out_b, ln2_g[l], w_route, tpb=t // TOKEN_TILE)

    route, counts = _route(lg_all, b_route.reshape(1, LANES), total)
    e = route[:total, 0:2].astype(I32)
    rank = route[:total, 4:6].astype(I32)
    cnt = counts[0, :N_EXPERTS].astype(I32)
    padded = (cnt + EXPERT_ROWS - 1) // EXPERT_ROWS * EXPERT_ROWS
    pad_end = jnp.cumsum(padded)
    pad_start = pad_end - padded
    m_slots = total * 2
    n_blocks = -(-(m_slots + N_EXPERTS * (EXPERT_ROWS - 1)) // EXPERT_ROWS)
    n_slots = n_blocks * EXPERT_ROWS
    dest = jnp.clip(pad_start[e] + rank, 0, n_slots - 1)
    blk_start = jnp.arange(n_blocks, dtype=I32) * EXPERT_ROWS
    blk_e = jnp.minimum(jnp.sum((pad_end[None, :] <= blk_start[:, None]).astype(I32), axis=1), N_EXPERTS - 1)
    n_dump = 2 * (x1_all.shape[0] - total)
    dest_pad = jnp.concatenate([dest, n_slots + jnp.arange(n_dump, dtype=I32).reshape(-1, 2)], axis=0)

    xb = _dispatch(dest_pad, pad_start + cnt, padded - cnt, hp_all, n_slots + n_dump)
    yb = _experts(blk_e, xb, w_gate[l], w_up[l], w_down[l], n_blocks)
    y_p = _final(dest_pad, yb, x1_all, route, pr(ga2), final_g, rows=nb * t, tpb=t // 256, per_row=False, row0=0)
    y_s = _final(dest_pad, yb, x1_all, route, sr(ga2), final_g, rows=ns, tpb=1, per_row=True, row0=nb * t)

    kv_shape = (2, N_KV, HD)
    y_prompt = y_p.reshape(nb, t, D_MODEL)
    y_sample = y_s.reshape(ns, 1, D_MODEL)
    new_cmp_prompt = kvc_rows_p.reshape((1, nb, t) + kv_shape)
    new_cmp_sample = kvc_rows_s.reshape((1, ns, 1) + kv_shape)
    new_sel_prompt = kvs_rows_p.reshape((1, nb, t) + kv_shape)
    new_sel_sample = kvs_rows_s.reshape((1, ns, 1) + kv_shape)
    new_win_prompt = kvw_rows_p.reshape((nb, t) + kv_shape)[:, t - WINDOW:][None]
    new_win_sample = jnp.concatenate([cache_win_kv[l][:, 1:], kvw_rows_s.reshape((ns, 1) + kv_shape)], axis=1)[None]
    new_conv_prompt = cst_p[:, SUBLANES - (CONV_K - 1):][None]
    new_conv_sample = jnp.stack([state_conv[l][:, 1], cst_s], axis=1)[None]
    return (y_prompt, y_sample, new_cmp_prompt, new_cmp_sample, new_sel_prompt, new_sel_sample,
            new_win_prompt, new_win_sample, new_conv_prompt, new_conv_sample)
```

```python
import functools

import numpy as np
import jax
import jax.numpy as jnp
from jax import lax
from jax.experimental import pallas as pl
from jax.experimental.pallas import tpu as pltpu

F32 = jnp.float32
BF16 = jnp.bfloat16
I32 = jnp.int32

D_MODEL = 1024
CONV_W = 512
ATTN_W = 512
HD = 64
HALF = HD // 2
N_HEADS = 8
N_KV = 2
QPK = 4
KV_W = N_KV * HD
CONV_K = 3
PAGE = 128
CMP_STRIDE = 16
CMP_HID = 128
SEL_BLOCK = 64
N_SEL = 16
WINDOW = 512
Q_BLOCK = 128
ROPE_THETA = 10000.0
N_GROUPS = 4
EPG = 8
N_EXPERTS = 32
D_EXPERT = 512
NORM_EPS = 1e-6
NEG_INF = -1e30
FORCE_SCORE = 1e4
LANES = 128
SUBLANES = 8
CHUNK_ROW = CMP_STRIDE * 2 * KV_W
VMEM_LIMIT = 56 * 1024 * 1024

_NT = (((1,), (1,)), ((), ()))
Q_SCALE = HD ** -0.5 * 1.4426950408889634


def _params(n_axes):
    return pltpu.CompilerParams(dimension_semantics=("arbitrary",) * n_axes,
                                vmem_limit_bytes=VMEM_LIMIT)


def _rms(x, g):
    return x * lax.rsqrt(jnp.mean(x * x, axis=-1, keepdims=True) + NORM_EPS) * g


def _rope128(x, cos, sin_signed, first_half):
    xr = jnp.where(first_half, pltpu.roll(x, LANES - HALF, 1), pltpu.roll(x, HALF, 1))
    return x * cos + xr * sin_signed


def _first_half_mask(rows):
    lane = lax.broadcasted_iota(I32, (rows, LANES), 1)
    return (lane % HD) < HALF


def _ada_kernel(c_ref, w_ref, b_ref, o_ref):
    c = c_ref[...]
    s = c * jax.nn.sigmoid(c)
    o_ref[...] = jnp.dot(s.astype(BF16), w_ref[...].astype(BF16), preferred_element_type=F32) + b_ref[...]


def _ada(c_all, w_ada, b_ada):
    m, d = c_all.shape
    n = w_ada.shape[1]
    tn = 1024
    return pl.pallas_call(
        _ada_kernel,
        grid=(n // tn,),
        in_specs=[pl.BlockSpec((m, d), lambda j: (0, 0)),
                  pl.BlockSpec((d, tn), lambda j: (0, j)),
                  pl.BlockSpec((1, tn), lambda j: (0, j))],
        out_specs=pl.BlockSpec((m, tn), lambda j: (0, j)),
        out_shape=jax.ShapeDtypeStruct((m, n), F32),
        compiler_params=_params(1),
        name="ada",
    )(c_all, w_ada, b_ada.reshape(1, n))


_C_B, _C_C, _C_U, _C_Q, _C_KVC, _C_KVS, _C_KVW, _C_G, _C_END = 0, 512, 1024, 1536, 2048, 2304, 2560, 2816, 3072


def _proj_kernel(*refs, tm, tpb, sample):
    if sample:
        (x_ref, g1_ref, sc_ref, sh_ref, w_ref, wc_ref, cos_ref, sin_ref, p0_ref, p1_ref,
         conv_ref, cst_ref, q_ref, kvc_ref, kvc_il_ref, kvs_ref, kvw_ref, ks_ref, vs_ref, kw_ref, vw_ref, gate_ref,
         ilbuf) = refs
        vbuf = None
    else:
        (x_ref, g1_ref, sc_ref, sh_ref, w_ref, wc_ref, cos_ref, sin_ref,
         conv_ref, cst_ref, q_ref, kvc_ref, kvc_il_ref, kvs_ref, kvw_ref, ks_ref, vs_ref, kw_ref, vw_ref, gate_ref,
         ilbuf, vbuf) = refs
    i = pl.program_id(0)
    x = x_ref[...]
    h = _rms(x, g1_ref[...]) * (1.0 + sc_ref[0]) + sh_ref[0]
    hb = h.astype(BF16)

    zc = jnp.dot(hb, w_ref[:, _C_B:_C_Q], preferred_element_type=F32)
    b_g = zc[:, 0:CONV_W]
    v = zc[:, CONV_W:2 * CONV_W] * zc[:, 2 * CONV_W:3 * CONV_W]
    wc = wc_ref[...]
    if sample:
        y = wc[0:1] * p0_ref[...] + wc[1:2] * p1_ref[...] + wc[2:3] * v
        cst_ref[...] = v
    else:
        @pl.when(i % tpb == 0)
        def _():
            vbuf[0:SUBLANES, :] = jnp.zeros((SUBLANES, CONV_W), F32)
        vbuf[SUBLANES:SUBLANES + tm, :] = v
        y = wc[0:1] * vbuf[pl.ds(SUBLANES - 2, tm), :] + wc[1:2] * vbuf[pl.ds(SUBLANES - 1, tm), :] + wc[2:3] * v
        tail = vbuf[tm:tm + SUBLANES, :]
        cst_ref[0] = tail
        vbuf[0:SUBLANES, :] = tail
    conv_ref[...] = b_g * y

    cos = cos_ref[...]
    sin_s = sin_ref[...]
    first = _first_half_mask(tm)

    zq = jnp.dot(hb, w_ref[:, _C_Q:_C_KVC], preferred_element_type=F32)
    for gq in range(ATTN_W // LANES):
        qr = _rope128(zq[:, gq * LANES:(gq + 1) * LANES], cos, sin_s, first) * Q_SCALE
        q_ref[0, 2 * gq] = qr[:, 0:HD].astype(BF16)
        q_ref[0, 2 * gq + 1] = qr[:, HD:LANES].astype(BF16)

    def store_rows(out_ref, halves):
        for j in range(2 * N_KV):
            piece = halves[j // N_KV]
            if j % N_KV == 1:
                piece = pltpu.roll(piece, HD, 1)
            ilbuf[pl.ds(j, tm, stride=2 * N_KV), :] = piece
        out_ref[...] = ilbuf[:, 0:HD]

    zkv = jnp.dot(hb, w_ref[:, _C_KVC:_C_G], preferred_element_type=F32)
    kvc_ref[...] = zkv[:, 0:2 * KV_W]
    store_rows(kvc_il_ref, (zkv[:, 0:KV_W], zkv[:, KV_W:2 * KV_W]))
    for base, kv_ref, kh_ref, vh_ref in ((2 * KV_W, kvs_ref, ks_ref, vs_ref), (4 * KV_W, kvw_ref, kw_ref, vw_ref)):
        kr = _rope128(zkv[:, base:base + KV_W], cos, sin_s, first)
        vv = zkv[:, base + KV_W:base + 2 * KV_W]
        store_rows(kv_ref, (kr, vv))
        lane = lax.broadcasted_iota(I32, (tm, LANES), 1)
        for k in range(N_KV):
            kh_ref[0, k] = kr[:, k * HD:(k + 1) * HD].astype(BF16)
            vk = vv if k == 0 else pltpu.roll(vv, HD, 1)
            vh_ref[0, k] = jnp.where(lane < HD, vk, jnp.where(lane == HD, 1.0, 0.0)).astype(BF16)

    zg = jnp.dot(hb, w_ref[:, _C_G:_C_END], preferred_element_type=F32)
    gate_ref[...] = jax.nn.sigmoid(zg)


def _proj(x2d, g1, sc, sh, w_pack, w_conv, cos_t, sin_t, *, nb, t, sample, prev=None):
    rows = nb * t
    tm = min(512, rows) if not sample else rows
    tpb = (t // tm) if not sample else 1
    n_tiles = rows // tm
    f = lambda a: jax.ShapeDtypeStruct(a, F32)
    b = lambda a: jax.ShapeDtypeStruct(a, BF16)
    if sample:
        mod_spec = pl.BlockSpec((1, tm, D_MODEL), lambda i: (0, 0, 0))
        tab_spec = pl.BlockSpec((1, LANES), lambda i: (0, 0))
        cst_shape, cst_spec = f((rows, CONV_W)), pl.BlockSpec((tm, CONV_W), lambda i: (0, 0))
        hm = lambda i: (0, 0, i, 0)
        hb_, ht_ = 1, rows
    else:
        mod_spec = pl.BlockSpec((1, 1, D_MODEL), lambda i: (i // tpb, 0, 0))
        tab_spec = pl.BlockSpec((tm, LANES), lambda i: (i % tpb, 0))
        cst_shape, cst_spec = f((nb, SUBLANES, CONV_W)), pl.BlockSpec((1, SUBLANES, CONV_W), lambda i: (i // tpb, 0, 0))
        hm = lambda i: (i // tpb, 0, i % tpb, 0)
        hb_, ht_ = nb, t
    row = lambda w: pl.BlockSpec((tm, w), lambda i: (i, 0))
    in_specs = [row(D_MODEL), pl.BlockSpec((1, D_MODEL), lambda i: (0, 0)), mod_spec, mod_spec,
                pl.BlockSpec((D_MODEL, _C_END), lambda i: (0, 0)),
                pl.BlockSpec((SUBLANES, CONV_W), lambda i: (0, 0)), tab_spec, tab_spec]
    args = [x2d, g1.reshape(1, D_MODEL), sc, sh, w_pack, w_conv, cos_t, sin_t]
    scratch = [pltpu.VMEM((2 * N_KV * tm, LANES), F32)]
    if sample:
        in_specs += [row(CONV_W), row(CONV_W)]
        args += [prev[0], prev[1]]
    else:
        scratch.append(pltpu.VMEM((tm + SUBLANES, CONV_W), F32))
    il_rows = 2 * N_KV * rows
    il = pl.BlockSpec((2 * N_KV * tm, HD), lambda i: (i, 0))
    out_shape = [f((rows, CONV_W)), cst_shape, b((hb_, N_HEADS, ht_, HD)),
                 f((rows, 2 * KV_W)), f((il_rows, HD)), f((il_rows, HD)), f((il_rows, HD)),
                 b((hb_, N_KV, ht_, HD)), b((hb_, N_KV, ht_, LANES)), b((hb_, N_KV, ht_, HD)), b((hb_, N_KV, ht_, LANES)),
                 f((rows, 2 * LANES))]
    out_specs = [row(CONV_W), cst_spec, pl.BlockSpec((1, N_HEADS, tm, HD), hm),
                 row(2 * KV_W), il, il, il,
                 pl.BlockSpec((1, N_KV, tm, HD), hm), pl.BlockSpec((1, N_KV, tm, LANES), hm),
                 pl.BlockSpec((1, N_KV, tm, HD), hm), pl.BlockSpec((1, N_KV, tm, LANES), hm),
                 row(2 * LANES)]
    return pl.pallas_call(
        functools.partial(_proj_kernel, tm=tm, tpb=tpb, sample=sample),
        grid=(n_tiles,), in_specs=in_specs, out_specs=out_specs, out_shape=out_shape,
        scratch_shapes=scratch, compiler_params=_params(1),
        name="proj_sample" if sample else "proj_prompt",
    )(*args)


def _cmpbias_kernel(pos_ref, w_ref, b1_ref, o_ref):
    for c in range(2):
        o_ref[c:c + 1, :] = jnp.sum(pos_ref[c] * w_ref[c], axis=0, keepdims=True) + b1_ref[c:c + 1, :]


def _cmpbias(cmp_pos, cmp_w1, cmp_b1):
    n = cmp_pos.shape[1] * cmp_pos.shape[2]
    return pl.pallas_call(
        _cmpbias_kernel,
        out_shape=jax.ShapeDtypeStruct((2, CMP_HID), F32),
        compiler_params=pltpu.CompilerParams(vmem_limit_bytes=VMEM_LIMIT),
        name="cmpbias",
    )(cmp_pos.reshape(2, n, 1), cmp_w1.reshape(2, n, CMP_HID), cmp_b1)


def _cmp_kernel(pt_ref, *refs, ppt, tiles):
    pages = refs[:ppt + 1]
    if tiles:
        unfold_ref = refs[ppt + 1]
        refs = refs[1:]
    w1_ref, b1_ref, w2_ref, b2_ref, cos_ref, sin_ref, ck_ref, cv_ref, lhs, pbuf = refs[ppt + 1:]
    r = ppt * SUBLANES
    first = _first_half_mask(r)
    if tiles:
        for j in range(ppt + 1):
            a = pages[j][...].reshape(2 * KV_W, PAGE).astype(BF16)
            y = lax.dot_general(unfold_ref[...], a, _NT, preferred_element_type=F32)
            for c in range(2):
                for s in range(CMP_STRIDE):
                    lhs[c, j * SUBLANES:(j + 1) * SUBLANES, s * KV_W:(s + 1) * KV_W] = (
                        y[s * SUBLANES:(s + 1) * SUBLANES, c * KV_W:(c + 1) * KV_W])
    else:
        for j in range(ppt + 1):
            for c in range(2):
                for s in range(CMP_STRIDE):
                    src = slice(s * 2 * KV_W + c * KV_W, s * 2 * KV_W + (c + 1) * KV_W)
                    lhs[c, j * SUBLANES:(j + 1) * SUBLANES, s * KV_W:(s + 1) * KV_W] = pages[j][0, :, src]
    for c in range(2):
        p = jnp.dot(lhs[c].astype(BF16), w1_ref[c], preferred_element_type=F32)
        pbuf[...] = p[:, 2 * CMP_HID:4 * CMP_HID]
        hid = p[0:r, 0:2 * CMP_HID] + pbuf[pl.ds(1, r), :] + b1_ref[c]
        act = jax.nn.gelu(hid)
        comp = jnp.dot(act.astype(BF16), w2_ref[c], preferred_element_type=F32) + b2_ref[c]
        if c == 0:
            comp = _rope128(comp, cos_ref[...], sin_ref[...], first)
            out = ck_ref
        else:
            out = cv_ref
        for k in range(N_KV):
            out[0, k] = comp[:, k * HD:(k + 1) * HD]


def _cmp(pages, pt_flat, nb, n_pages, w1p, b1p, w2p, b2p, cos_c, sin_c, name, tiles):
    ppt = min(32, n_pages)
    n_tiles = n_pages // ppt
    r = ppt * SUBLANES
    n_chunk = n_pages * SUBLANES
    zeros = (0,) * (pages.ndim - 1)

    def page_map(j):
        return lambda b, t, pt: (pt[b * n_pages + t * ppt + j],) + zeros

    def next_map(b, t, pt):
        return (pt[b * n_pages + jnp.minimum(t * ppt + ppt, n_pages - 1)],) + zeros

    page_blk = (None, 2, N_KV, HD, PAGE) if tiles else (1, SUBLANES, CHUNK_ROW)
    in_specs = [pl.BlockSpec(page_blk, page_map(j)) for j in range(ppt)]
    in_specs.append(pl.BlockSpec(page_blk, next_map))
    const = lambda shp: pl.BlockSpec(shp, lambda b, t, pt: (0,) * len(shp))
    extra = []
    if tiles:
        row = np.arange(PAGE)
        tok = (row % SUBLANES) * CMP_STRIDE + row // SUBLANES
        extra = [jnp.asarray(tok[:, None] == np.arange(PAGE)[None, :], BF16)]
        in_specs.append(const((PAGE, PAGE)))
    in_specs += [const(w1p.shape), const(b1p.shape), const(w2p.shape), const(b2p.shape),
                 pl.BlockSpec((r, LANES), lambda b, t, pt: (t, 0)), pl.BlockSpec((r, LANES), lambda b, t, pt: (t, 0))]
    hm = pl.BlockSpec((1, N_KV, r, HD), lambda b, t, pt: (b, 0, t, 0))
    grid_spec = pltpu.PrefetchScalarGridSpec(
        num_scalar_prefetch=1, grid=(nb, n_tiles), in_specs=in_specs, out_specs=[hm, hm],
        scratch_shapes=[pltpu.VMEM((2, r + SUBLANES, CMP_STRIDE * KV_W), F32),
                        pltpu.VMEM((r + SUBLANES, 2 * CMP_HID), F32)])
    return pl.pallas_call(
        functools.partial(_cmp_kernel, ppt=ppt, tiles=tiles),
        grid_spec=grid_spec,
        out_shape=[jax.ShapeDtypeStruct((nb, N_KV, n_chunk, HD), F32)] * 2,
        compiler_params=_params(2), name=name,
    )(pt_flat, *([pages] * (ppt + 1)), *extra, w1p, b1p, w2p, b2p, cos_c, sin_c)


def _softmax_rows(s, valid):
    s = jnp.where(valid, s, NEG_INF)
    m = jnp.max(s, axis=-1, keepdims=True)
    e = jnp.exp2(s - m)
    return e / jnp.sum(e, axis=-1, keepdims=True)


def _attn_p_kernel(q_ref, ck_ref, cv_ref, ks_ref, vs_ref, kw_ref, vw_ref, gate_ref, band_ref, exp_ref, o_ref,
                   *, n_cmp_pad, n_blk, kc, hg, wc):
    qb = pl.program_id(2)
    start = qb * Q_BLOCK
    tpos = start + lax.broadcasted_iota(I32, (Q_BLOCK, 1), 0)
    groups = range(QPK // hg)
    rows = hg * Q_BLOCK

    def q_of(g):
        return q_ref[0, g * hg:(g + 1) * hg].reshape(rows, HD)

    def biased(s, bias):
        width = s.shape[-1]
        return (s.reshape(hg, Q_BLOCK, width) + bias[None]).reshape(rows, width)

    ck = ck_ref[0, 0].astype(BF16)
    cv = cv_ref[0, 0].astype(BF16)
    cmp_end = (lax.broadcasted_iota(I32, (1, n_cmp_pad), 1) + 2) * CMP_STRIDE - 1
    bias_c = jnp.where(cmp_end <= tpos, 0.0, NEG_INF)
    o_c = []
    pcs = jnp.zeros((Q_BLOCK, n_cmp_pad), F32)
    for g in groups:
        s_c = biased(lax.dot_general(q_of(g), ck, _NT, preferred_element_type=F32), bias_c)
        m_c = jnp.maximum(jnp.max(s_c, axis=-1, keepdims=True), 0.5 * NEG_INF)
        e_c = jnp.exp2(s_c - m_c)
        l_c = jnp.sum(e_c, axis=-1, keepdims=True)
        p_c = e_c * (1.0 / jnp.where(l_c > 0.0, l_c, 1.0))
        o_c.append(jnp.dot(p_c.astype(BF16), cv, preferred_element_type=F32))
        for h in range(hg):
            pcs = pcs + p_c[h * Q_BLOCK:(h + 1) * Q_BLOCK]

    imp =lax.dot_general(band_ref[...], pcs, _NT, preferred_element_type=F32,
                          precision=lax.Precision.HIGHEST)
    blk = lax.broadcasted_iota(I32, (n_blk, Q_BLOCK), 0)
    tlane = start + lax.broadcasted_iota(I32, (1, Q_BLOCK), 1)
    cur = tlane // SEL_BLOCK
    causal = blk * SEL_BLOCK <= tlane
    forced = causal & ((blk == 0) | (blk == cur) | (blk == cur - 1))
    score = jnp.where(forced, FORCE_SCORE, jnp.where(causal, imp, -1.0))
    rank = jnp.zeros((n_blk, Q_BLOCK), F32)
    for bp in range(n_blk):
        other = score[bp:bp + 1, :]
        beats = (other > score) | ((other == score) & (bp < blk))
        rank = rank + beats.astype(F32)
    sel_t = (rank < float(min(N_SEL, n_blk))).astype(BF16)
    eye = (lax.broadcasted_iota(I32, (Q_BLOCK, Q_BLOCK), 0)
           == lax.broadcasted_iota(I32, (Q_BLOCK, Q_BLOCK), 1)).astype(BF16)
    sel = lax.dot_general(eye, sel_t, _NT, preferred_element_type=F32).astype(BF16)

    n_chunks = (start + Q_BLOCK + kc - 1) // kc

    def online(state, kj, vj, bias):
        out = []
        for g in groups:
            m_i, acc = state[g]
            s = biased(lax.dot_general(q_of(g), kj, _NT, preferred_element_type=F32), bias)
            m_new = jnp.maximum(m_i, jnp.max(s, axis=-1, keepdims=True))
            p = jnp.exp2(s - m_new).astype(BF16)
            out.append((m_new, jnp.exp2(m_i - m_new) * acc + jnp.dot(p, vj, preferred_element_type=F32)))
        return tuple(out)

    def step(j, state, causal_chunk):
        off = pl.multiple_of(j * kc, kc)
        mexp = jnp.dot(sel, exp_ref[j], preferred_element_type=F32)
        bias = mexp * (-NEG_INF) + NEG_INF
        if causal_chunk:
            keypos = off + lax.broadcasted_iota(I32, (1, kc), 1)
            bias = jnp.where(keypos <= tpos, bias, NEG_INF)
        return online(state, ks_ref[0, 0, pl.ds(off, kc), :], vs_ref[0, 0, pl.ds(off, kc), :], bias)

    init = tuple((jnp.full((rows, 1), NEG_INF, F32), jnp.zeros((rows, LANES), F32)) for _ in groups)
    state = lax.fori_loop(0, n_chunks - 1, lambda j, c: step(j, c, False), init)
    sel_state = step(n_chunks - 1, state, True)

    s0 = jnp.maximum(start - WINDOW, 0)
    win_state = init
    for c in range((WINDOW + Q_BLOCK) // wc):
        off = pl.multiple_of(s0 + c * wc, Q_BLOCK)
        dist = tpos - (off + lax.broadcasted_iota(I32, (1, wc), 1))
        bias_w = jnp.where((dist >= 0) & (dist <= WINDOW), 0.0, NEG_INF)
        win_state = online(win_state, kw_ref[0, 0, pl.ds(off, wc), :], vw_ref[0, 0, pl.ds(off, wc), :], bias_w)

    gt = gate_ref[...]
    for g in groups:
        acc_s = sel_state[g][1]
        acc_w = win_state[g][1]
        o_s = acc_s[:, 0:HD] * (1.0 / acc_s[:, HD:HD + 1])
        o_w = acc_w[:, 0:HD] * (1.0 / acc_w[:, HD:HD + 1])
        for hh in range(hg):
            h = g * hg + hh
            rs = slice(hh * Q_BLOCK, (hh + 1) * Q_BLOCK)
            o = (gt[:, h:h + 1] * o_c[g][rs] + gt[:, QPK + h:QPK + h + 1] * o_s[rs]
                 + gt[:, 2 * QPK + h:2 * QPK + h + 1] * o_w[rs])
            o_ref[0, :, h * HD:(h + 1) * HD] = o


ATTN_HEAD_GROUP = 4
ATTN_KEY_CHUNK = 512
ATTN_WIN_CHUNK = 640


def _attn_prompt(q_hm, ck, cv, ks, vs, kw, vw, gates, band, expand, *, nb, t):
    n_qb = t // Q_BLOCK
    n_cmp_pad = ck.shape[2]
    n_blk = band.shape[0]
    kc = expand.shape[2]
    kv_spec = lambda n, w=HD: pl.BlockSpec((1, 1, n, w), lambda b, k, i: (b, k, 0, 0))
    return pl.pallas_call(
        functools.partial(_attn_p_kernel, n_cmp_pad=n_cmp_pad, n_blk=n_blk, kc=kc, hg=ATTN_HEAD_GROUP, wc=ATTN_WIN_CHUNK),
        grid=(nb, N_KV, n_qb),
        in_specs=[pl.BlockSpec((1, QPK, Q_BLOCK, HD), lambda b, k, i: (b, k, i, 0)),
                  kv_spec(n_cmp_pad), kv_spec(n_cmp_pad), kv_spec(t), kv_spec(t, LANES), kv_spec(t), kv_spec(t, LANES),
                  pl.BlockSpec((Q_BLOCK, LANES), lambda b, k, i: (b * n_qb + i, k)),
                  pl.BlockSpec(band.shape, lambda b, k, i: (0, 0)),
                  pl.BlockSpec(expand.shape, lambda b, k, i: (0, 0, 0))],
        out_specs=pl.BlockSpec((1, Q_BLOCK, QPK * HD), lambda b, k, i: (b, i, k)),
        out_shape=jax.ShapeDtypeStruct((nb, t, ATTN_W), F32),
        compiler_params=_params(3), name="attn_prompt",
    )(q_hm, ck, cv, ks, vs, kw, vw, gates, band, expand)


def _attn_s1_kernel(q_ref, ck_ref, cv_ref, band_ref, oc_ref, imp_ref, *, n_chunk, past):
    q = q_ref[0]
    q16 = jnp.concatenate([q, jnp.zeros_like(q)], axis=0).astype(BF16)
    cmp_end = (lax.broadcasted_iota(I32, (1, n_chunk), 1) + 2) * CMP_STRIDE - 1
    valid = cmp_end <= past
    head = lax.broadcasted_iota(I32, (2 * N_HEADS, 1), 0)
    oc = jnp.zeros((2 * N_HEADS, HD), F32)
    imps = []
    for k in range(N_KV):
        s = lax.dot_general(q16, ck_ref[0, k].astype(BF16), _NT, preferred_element_type=F32)
        p = _softmax_rows(s, valid) * valid.astype(F32)
        in_grp = (head >= k * QPK) & (head < (k + 1) * QPK)
        p = jnp.where(in_grp, p, 0.0)
        oc = oc + jnp.dot(p.astype(BF16), cv_ref[0, k].astype(BF16), preferred_element_type=F32)
        pcs = jnp.sum(p, axis=0, keepdims=True)
        pcs8 = jnp.broadcast_to(pcs, (SUBLANES, n_chunk))
        imps.append(jnp.dot(pcs8, band_ref[...], preferred_element_type=F32,
                            precision=lax.Precision.HIGHEST)[0:1])
    oc_ref[0] = oc[0:N_HEADS]
    imp_ref[0] = jnp.concatenate(imps + [jnp.zeros((SUBLANES - N_KV, imps[0].shape[1]), F32)], axis=0)


def _topk_s_kernel(imp_ref, idx_ref, *, n_sel_blocks, past):
    imp = imp_ref[...]
    rows, nbp = imp.shape
    blk = lax.broadcasted_iota(I32, (rows, nbp), 1)
    cur = past // SEL_BLOCK
    causal = blk * SEL_BLOCK <= past
    forced = causal & ((blk == 0) | (blk == cur) | (blk == cur - 1))
    score = jnp.where(forced, FORCE_SCORE, jnp.where(causal, imp, -1.0))
    score = jnp.where(blk < n_sel_blocks, score, -2.0)
    lane = lax.broadcasted_iota(I32, (rows, LANES), 1)
    out = jnp.zeros((rows, LANES), I32)
    for r in range(min(N_SEL, n_sel_blocks)):
        m = jnp.max(score, axis=-1, keepdims=True)
        pick = jnp.min(jnp.where(score == m, blk, nbp), axis=-1, keepdims=True)
        out = jnp.where(lane == r, pick, out)
        score = jnp.where(blk == pick, -3.0, score)
    idx_ref[...] = out


def _attn_s2_kernel(pt_ref, idx_ref, *refs, n_pages, past, n_sel_blocks):
    ktiles, vtiles = refs[:N_SEL], refs[N_SEL:2 * N_SEL]
    q_ref, oc_ref, kvs_ref, wk_ref, wv_ref, kvw_ref, gate_ref, o_ref, kbuf, vbuf = refs[2 * N_SEL:]
    b = pl.program_id(0)
    k = pl.program_id(1)
    q = q_ref[0]
    q16f = jnp.concatenate([q, jnp.zeros_like(q)], axis=0)
    q16 = q16f.astype(BF16)
    head = lax.broadcasted_iota(I32, (N_HEADS, 1), 0)
    nk = N_SEL * PAGE
    lane = lax.broadcasted_iota(I32, (1, nk), 1)
    slot = lane // PAGE
    new_blk = n_sel_blocks - 1
    wb = wk_ref.shape[-1]
    wpos = past - wb + lax.broadcasted_iota(I32, (1, wb), 1)
    wdist = past - wpos
    valid_w = (wdist >= 0) & (wdist <= WINDOW) & (wpos >= 0)

    def attend(s, valid, v_t, k_new, v_new):
        s_new = jnp.sum(q16f * k_new, axis=-1, keepdims=True)
        s = jnp.where(valid, s, NEG_INF)
        m = jnp.maximum(jnp.max(s, axis=-1, keepdims=True), s_new)
        e = jnp.exp2(s - m)
        e_new = jnp.exp2(s_new - m)
        den = jnp.sum(e, axis=-1, keepdims=True) + e_new
        acc = lax.dot_general(e.astype(BF16), v_t, _NT, preferred_element_type=F32) + e_new * v_new
        return acc / den

    in_grp = (head >= k * QPK) & (head < (k + 1) * QPK)
    bvec = jnp.zeros((1, nk), I32)
    for j in range(N_SEL):
        kbuf[:, j * PAGE:(j + 1) * PAGE] = ktiles[j][...].astype(BF16)
        vbuf[:, j * PAGE:(j + 1) * PAGE] = vtiles[j][...].astype(BF16)
        bvec = jnp.where(slot == j, idx_ref[(b * N_KV + k) * LANES + j], bvec)
    tok = (bvec // 2) * PAGE + lane % PAGE
    valid = (tok // SEL_BLOCK == bvec) & (bvec < new_blk) & (tok <= past)
    s = jnp.dot(q16, kbuf[...], preferred_element_type=F32)
    o_s = attend(s, valid, vbuf[...], kvs_ref[0, pl.ds(k, 1), :], kvs_ref[0, pl.ds(N_KV + k, 1), :])
    sw = jnp.dot(q16, wk_ref[...].astype(BF16), preferred_element_type=F32)
    o_w = attend(sw, valid_w, wv_ref[...].astype(BF16), kvw_ref[0, pl.ds(k, 1), :], kvw_ref[0, pl.ds(N_KV + k, 1), :])
    g = gate_ref[0]
    part = jnp.where(in_grp, g[:, 1:2] * o_s[0:N_HEADS] + g[:, 2:3] * o_w[0:N_HEADS], 0.0)

    @pl.when(k == 0)
    def _():
        o_ref[0] = g[:, 0:1] * oc_ref[0] + part

    @pl.when(k > 0)
    def _():
        o_ref[0] = o_ref[0] + part


def _attn_sample(q3, ck, cv, band_s, sel_t, pt_flat, kvs_rows, win_t, kvw_rows, gates_hm,
                 *, nb, n_pages, past, n_sel_blocks):
    n_chunk = ck.shape[2]
    nbp = band_s.shape[1]
    oc, imp = pl.pallas_call(
        functools.partial(_attn_s1_kernel, n_chunk=n_chunk, past=past),
        grid=(nb,),
        in_specs=[pl.BlockSpec((1, N_HEADS, HD), lambda b: (b, 0, 0)),
                  pl.BlockSpec((1, N_KV, n_chunk, HD), lambda b: (b, 0, 0, 0)),
                  pl.BlockSpec((1, N_KV, n_chunk, HD), lambda b: (b, 0, 0, 0)),
                  pl.BlockSpec(band_s.shape, lambda b: (0, 0))],
        out_specs=[pl.BlockSpec((1, N_HEADS, HD), lambda b: (b, 0, 0)),
                   pl.BlockSpec((1, SUBLANES, nbp), lambda b: (b, 0, 0))],
        out_shape=[jax.ShapeDtypeStruct((nb, N_HEADS, HD), F32), jax.ShapeDtypeStruct((nb, SUBLANES, nbp), F32)],
        compiler_params=_params(1), name="attn_sample_cmp",
    )(q3, ck, cv, band_s)
    imp2 = imp[:, 0:N_KV, :].reshape(nb * N_KV, nbp)
    idx = pl.pallas_call(
        functools.partial(_topk_s_kernel, n_sel_blocks=n_sel_blocks, past=past),
        out_shape=jax.ShapeDtypeStruct((nb * N_KV, LANES), I32),
        compiler_params=pltpu.CompilerParams(vmem_limit_bytes=VMEM_LIMIT), name="topk_sample",
    )(imp2)
    idx_flat = idx.reshape(-1)

    def tile_map(c, j):
        def f(b, k, pt, ix):
            bidx = ix[(b * N_KV + k) * LANES + j]
            return (pt[b * n_pages + jnp.minimum(bidx // 2, n_pages - 1)], c, k, 0, 0)
        return f

    tile = lambda c, j: pl.BlockSpec((None, None, None, HD, PAGE), tile_map(c, j))
    in_specs = [tile(0, j) for j in range(N_SEL)] + [tile(1, j) for j in range(N_SEL)]
    wb = win_t.shape[-1]
    per_b = lambda shp: pl.BlockSpec(shp, lambda b, k, pt, ix: (b, 0, 0))
    in_specs += [per_b((1, N_HEADS, HD)), per_b((1, N_HEADS, HD)), per_b((1, SUBLANES, HD)),
                 pl.BlockSpec((None, None, None, HD, wb), lambda b, k, pt, ix: (b, 0, k, 0, 0)),
                 pl.BlockSpec((None, None, None, HD, wb), lambda b, k, pt, ix: (b, 1, k, 0, 0)),
                 per_b((1, SUBLANES, HD)), per_b((1, N_HEADS, LANES))]
    grid_spec = pltpu.PrefetchScalarGridSpec(
        num_scalar_prefetch=2, grid=(nb, N_KV), in_specs=in_specs,
        out_specs=per_b((1, N_HEADS, HD)),
        scratch_shapes=[pltpu.VMEM((HD, N_SEL * PAGE), BF16), pltpu.VMEM((HD, N_SEL * PAGE), BF16)])
    return pl.pallas_call(
        functools.partial(_attn_s2_kernel, n_pages=n_pages, past=past, n_sel_blocks=n_sel_blocks),
        grid_spec=grid_spec,
        out_shape=jax.ShapeDtypeStruct((nb, N_HEADS, HD), F32),
        compiler_params=_params(2), name="attn_sample_sel",
    )(pt_flat, idx_flat, *([sel_t] * (2 * N_SEL)), q3, oc, kvs_rows, win_t, win_t, kvw_rows, gates_hm)


TOK_ROWS = D_MODEL // LANES


def _store_token_tiles(ref, x):
    n = x.shape[0]
    for j in range(TOK_ROWS):
        ref[pl.ds(j, n, stride=TOK_ROWS), :] = x[:, j * LANES:(j + 1) * LANES]


def _load_token_tiles(ref, lead, n):
    return jnp.concatenate([ref[lead + (pl.ds(j, n, stride=TOK_ROWS), slice(None))] for j in range(TOK_ROWS)], axis=1)


def _outp_kernel(xp_ref, convp_ref, attnp_ref, ga1p_ref, sc2p_ref, sh2p_ref,
                 xs_ref, convs_ref, attns_ref, ga1s_ref, sc2s_ref, sh2s_ref,
                 gc_ref, ga_ref, w_ref, g2_ref, wr_ref, x1_ref, hp_ref, lg_ref, *, n_prompt_tiles):
    is_p = pl.program_id(0) < n_prompt_tiles
    pick = lambda a, b: jnp.where(is_p, a, b)
    cn = _rms(pick(convp_ref[...], convs_ref[...]), gc_ref[...])
    an = _rms(pick(attnp_ref[...], attns_ref[...]), ga_ref[...])
    cat = jnp.concatenate([cn, an], axis=1).astype(BF16)
    y = jnp.dot(cat, w_ref[...], preferred_element_type=F32)
    x1 = pick(xp_ref[...], xs_ref[...]) + pick(ga1p_ref[0], ga1s_ref[0]) * y
    x1_ref[...] = x1
    hp = _rms(x1, g2_ref[...]) * (1.0 + pick(sc2p_ref[0], sc2s_ref[0])) + pick(sh2p_ref[0], sh2s_ref[0])
    _store_token_tiles(hp_ref, hp)
    lg_ref[...] = jnp.dot(hp, wr_ref[...], preferred_element_type=F32, precision=lax.Precision.HIGHEST)


TOKEN_TILE = 512


def _outp(prompt, sample, g_conv, g_attn, w_out_b, g2, w_route, *, tpb):
    tm = TOKEN_TILE
    n_p = prompt[0].shape[0] // tm
    total = (n_p + 1) * tm
    last = n_p - 1
    prow = lambda w: pl.BlockSpec((tm, w), lambda i: (jnp.minimum(i, last), 0))
    srow = lambda w: pl.BlockSpec((tm, w), lambda i: (0, 0))
    pmod = pl.BlockSpec((1, 1, D_MODEL), lambda i: (jnp.minimum(i, last) // tpb, 0, 0))
    smod = pl.BlockSpec((1, tm, D_MODEL), lambda i: (0, 0, 0))
    vec = lambda w: pl.BlockSpec((1, w), lambda i: (0, 0))
    row = lambda w: pl.BlockSpec((tm, w), lambda i: (i, 0))
    in_specs = [prow(D_MODEL), prow(CONV_W), prow(ATTN_W), pmod, pmod, pmod,
                srow(D_MODEL), srow(CONV_W), srow(ATTN_W), smod, smod, smod,
                vec(CONV_W), vec(ATTN_W), pl.BlockSpec((D_MODEL, D_MODEL), lambda i: (0, 0)), vec(D_MODEL),
                pl.BlockSpec((D_MODEL, LANES), lambda i: (0, 0))]
    return pl.pallas_call(
        functools.partial(_outp_kernel, n_prompt_tiles=n_p),
        grid=(n_p + 1,), in_specs=in_specs,
        out_specs=[row(D_MODEL), pl.BlockSpec((tm * TOK_ROWS, LANES), lambda i: (i, 0)), row(LANES)],
        out_shape=[jax.ShapeDtypeStruct((total, D_MODEL), F32), jax.ShapeDtypeStruct((total * TOK_ROWS, LANES), F32),
                   jax.ShapeDtypeStruct((total, LANES), F32)],
        compiler_params=_params(1), name="outp",
    )(*prompt, *sample, g_conv.reshape(1, -1), g_attn.reshape(1, -1), w_out_b, g2.reshape(1, -1), w_route)


def _route_kernel(lg_ref, bias_ref, tri_ref, o_ref, cnt_ref, carry, *, tm, n_valid):
    i = pl.program_id(0)

    @pl.when(i == 0)
    def _():
        carry[...] = jnp.zeros_like(carry)

    lane = lax.broadcasted_iota(I32, (tm, LANES), 1)
    rowid = i * tm + lax.broadcasted_iota(I32, (tm, 1), 0)
    live = rowid < n_valid
    lg = lg_ref[...] + bias_ref[...]
    is_g = lane < N_GROUPS
    lgg = jnp.where(is_g, lg, NEG_INF)
    gmax = jnp.max(lgg, axis=-1, keepdims=True)
    grp = jnp.min(jnp.where(is_g & (lgg == gmax), lane, LANES), axis=-1, keepdims=True)
    p_grp = 1.0 / jnp.sum(jnp.where(is_g, jnp.exp(lgg - gmax), 0.0), axis=-1, keepdims=True)
    eid = lane - N_GROUPS
    in_grp = (eid >= grp * EPG) & (eid < (grp + 1) * EPG)
    le = jnp.where(in_grp, lg, NEG_INF)
    v1 = jnp.max(le, axis=-1, keepdims=True)
    e1 = jnp.min(jnp.where(in_grp & (le == v1), eid, LANES), axis=-1, keepdims=True)
    le2 = jnp.where(eid == e1, NEG_INF, le)
    v2 = jnp.max(le2, axis=-1, keepdims=True)
    e2 = jnp.min(jnp.where(in_grp & (eid != e1) & (le2 == v2), eid, LANES), axis=-1, keepdims=True)
    ex2 = jnp.exp(v2 - v1)
    w1 = p_grp * (1.0 / (1.0 + ex2))
    w2 = p_grp * (ex2 / (1.0 + ex2))
    oh1 = ((lane == e1) & live).astype(F32)
    oh2 = ((lane == e2) & live).astype(F32)
    both = oh1 + oh2
    before = jnp.dot(tri_ref[...], both.astype(BF16), preferred_element_type=F32) + carry[0:1, :]
    r1 = jnp.sum(oh1 * before, axis=-1, keepdims=True)
    r2 = jnp.sum(oh2 * before, axis=-1, keepdims=True)
    carry[0:1, :] = carry[0:1, :] + jnp.sum(both, axis=0, keepdims=True)
    out = jnp.where(lane == 0, e1.astype(F32), 0.0)
    out = jnp.where(lane == 1, e2.astype(F32), out)
    out = jnp.where(lane == 2, w1, out)
    out = jnp.where(lane == 3, w2, out)
    out = jnp.where(lane == 4, r1, out)
    out = jnp.where(lane == 5, r2, out)
    o_ref[...] = out
    cnt_ref[...] = carry[...]


def _route(logits, bias_row, n_valid):
    total = logits.shape[0]
    tm = TOKEN_TILE
    n_tiles = total // tm
    tri =(np.arange(tm)[:, None] > np.arange(tm)[None, :]).astype(np.float32)
    return pl.pallas_call(
        functools.partial(_route_kernel, tm=tm, n_valid=n_valid),
        grid=(n_tiles,),
        in_specs=[pl.BlockSpec((tm, LANES), lambda i: (i, 0)), pl.BlockSpec((1, LANES), lambda i: (0, 0)),
                  pl.BlockSpec((tm, tm), lambda i: (0, 0))],
        out_specs=[pl.BlockSpec((tm, LANES), lambda i: (i, 0)), pl.BlockSpec((SUBLANES, LANES), lambda i: (0, 0))],
        out_shape=[jax.ShapeDtypeStruct((total, LANES), F32), jax.ShapeDtypeStruct((SUBLANES, LANES), F32)],
        scratch_shapes=[pltpu.VMEM((SUBLANES, LANES), F32)],
        compiler_params=_params(1), name="route",
    )(logits, bias_row, jnp.asarray(tri, BF16))


EXPERT_ROWS = 256
DISPATCH_TILE = 256


def _tile_copy(src, src_row, dst, dst_row, sem):
    return pltpu.make_async_copy(src.at[pl.ds(pl.multiple_of(src_row * TOK_ROWS, TOK_ROWS), TOK_ROWS), :],
                                 dst.at[pl.ds(pl.multiple_of(dst_row * TOK_ROWS, TOK_ROWS), TOK_ROWS), :], sem)


def _dispatch_kernel(zstart_ref, zcnt_ref, dest_ref, x_ref, xb_hbm, stage, zeros, sem, zsem, *, n_tiles, n_blocks):
    i = pl.program_id(0)
    tm = DISPATCH_TILE
    slot = i % 2
    blk_rows = EXPERT_ROWS * TOK_ROWS

    def tail_copy(b):
        off = pl.multiple_of(b * blk_rows, blk_rows)
        return pltpu.make_async_copy(zeros, xb_hbm.at[pl.ds(off, blk_rows), :], zsem.at[1])

    @pl.when(i == 0)
    def _():
        zeros[...] = jnp.zeros_like(zeros)
        first_tail = zstart_ref[N_EXPERTS] // EXPERT_ROWS
        for e in range(N_EXPERTS):
            def fill(r, c, e=e):
                _tile_copy(zeros, 0, xb_hbm, zstart_ref[e] + r, zsem.at[0]).start()
                return c
            lax.fori_loop(0, zcnt_ref[e], fill, 0)
        lax.fori_loop(first_tail, n_blocks, lambda b, c: (tail_copy(b).start(), c)[1], 0)
        for e in range(N_EXPERTS):
            def drain(r, c):
                _tile_copy(zeros, 0, xb_hbm, 0, zsem.at[0]).wait()
                return c
            lax.fori_loop(0, zcnt_ref[e], drain, 0)
        lax.fori_loop(first_tail, n_blocks, lambda b, c: (tail_copy(b).wait(), c)[1], 0)

    def wait_rows(s):
        for _ in range(2 * tm):
            _tile_copy(stage.at[s], 0, xb_hbm, 0, sem.at[s]).wait()

    @pl.when(i >= 2)
    def _():
        wait_rows(slot)

    stage[slot] = x_ref[...]
    for r in range(tm):
        for k in range(2):
            _tile_copy(stage.at[slot], r, xb_hbm, dest_ref[0, 0, 2 * r + k], sem.at[slot]).start()

    @pl.when(i == n_tiles - 1)
    def _():
        wait_rows(slot)
        if n_tiles > 1:
            wait_rows(1 - slot)


def _dispatch(dest_pad, zstart, zcnt, hp_all, n_rows, n_blocks):
    tm = DISPATCH_TILE
    n_tiles = dest_pad.shape[0] // tm
    grid_spec = pltpu.PrefetchScalarGridSpec(
        num_scalar_prefetch=2, grid=(n_tiles,),
        in_specs=[pl.BlockSpec((1, 1, 2 * tm), lambda i, zs, zc: (i, 0, 0), memory_space=pltpu.SMEM),
                  pl.BlockSpec((tm * TOK_ROWS, LANES), lambda i, zs, zc: (i, 0))],
        out_specs=pl.BlockSpec(memory_space=pl.ANY),
        scratch_shapes=[pltpu.VMEM((2, tm * TOK_ROWS, LANES), F32), pltpu.VMEM((EXPERT_ROWS * TOK_ROWS, LANES), F32),
                        pltpu.SemaphoreType.DMA((2,)), pltpu.SemaphoreType.DMA((2,))])
    return pl.pallas_call(
        functools.partial(_dispatch_kernel, n_tiles=n_tiles, n_blocks=n_blocks),
        grid_spec=grid_spec,
        out_shape=jax.ShapeDtypeStruct((n_rows * TOK_ROWS, LANES), F32),
        compiler_params=_params(1), name="dispatch",
    )(zstart, zcnt, dest_pad.reshape(n_tiles, 1, 2 * tm), hp_all)


def _experts_kernel(blk_e_ref, x_ref, wg_ref, wu_ref, wd_ref, o_ref, wg_b, wu_b, wd_b):
    i = pl.program_id(0)
    changed = jnp.logical_or(i == 0, blk_e_ref[i] != blk_e_ref[jnp.maximum(i - 1, 0)])

    @pl.when(changed)
    def _():
        wg_b[...] = wg_ref[0].astype(BF16)
        wu_b[...] = wu_ref[0].astype(BF16)
        wd_b[...] = wd_ref[0].astype(BF16)

    x = _load_token_tiles(x_ref, (), EXPERT_ROWS).astype(BF16)
    g = jnp.dot(x, wg_b[...], preferred_element_type=F32)
    u = jnp.dot(x, wu_b[...], preferred_element_type=F32)
    h = (g * jax.nn.sigmoid(g)) * u
    _store_token_tiles(o_ref, jnp.dot(h.astype(BF16), wd_b[...], preferred_element_type=F32))


def _experts(blk_e, xb, w_gate, w_up, w_down, n_blocks):
    blk = pl.BlockSpec((EXPERT_ROWS * TOK_ROWS, LANES), lambda i, be: (i, 0))
    grid_spec = pltpu.PrefetchScalarGridSpec(
        num_scalar_prefetch=1, grid=(n_blocks,),
        in_specs=[blk,
                  pl.BlockSpec((1, D_MODEL, D_EXPERT), lambda i, be: (be[i], 0, 0)),
                  pl.BlockSpec((1, D_MODEL, D_EXPERT), lambda i, be: (be[i], 0, 0)),
                  pl.BlockSpec((1, D_EXPERT, D_MODEL), lambda i, be: (be[i], 0, 0))],
        out_specs=blk,
        scratch_shapes=[pltpu.VMEM((D_MODEL, D_EXPERT), BF16), pltpu.VMEM((D_MODEL, D_EXPERT), BF16),
                        pltpu.VMEM((D_EXPERT, D_MODEL), BF16)])
    return pl.pallas_call(
        _experts_kernel,
        grid_spec=grid_spec,
        out_shape=jax.ShapeDtypeStruct((n_blocks * EXPERT_ROWS * TOK_ROWS, LANES), F32),
        compiler_params=_params(1), name="experts",
    )(blk_e, xb, w_gate, w_up, w_down)


def _final_kernel(dest_first_ref, dest_next_ref, yb_hbm, x1_ref, wt_ref, gate2_ref, gf_ref, o_ref, ybuf, sem,
                  *, tm, n_tiles):
    i = pl.program_id(0)
    slot = i % 2

    def issue(dest_ref, s):
        for r in range(tm):
            for k in range(2):
                d = dest_ref[0, 0, 2 * r + k]
                src = yb_hbm.at[pl.ds(pl.multiple_of(d * TOK_ROWS, TOK_ROWS), TOK_ROWS), :]
                pltpu.make_async_copy(src, ybuf.at[s, k, pl.ds(r * TOK_ROWS, TOK_ROWS), :], sem.at[s]).start()

    @pl.when(i == 0)
    def _():
        issue(dest_first_ref, 0)

    @pl.when(i + 1 < n_tiles)
    def _():
        issue(dest_next_ref, 1 - slot)

    for r in range(tm):
        for k in range(2):
            pltpu.make_async_copy(yb_hbm.at[pl.ds(0, TOK_ROWS), :], ybuf.at[slot, k, pl.ds(r * TOK_ROWS, TOK_ROWS), :],
                                  sem.at[slot]).wait()
    wt = wt_ref[...]
    f = wt[:, 2:3] * _load_token_tiles(ybuf, (slot, 0), tm) + wt[:, 3:4] * _load_token_tiles(ybuf, (slot, 1), tm)
    x2 = x1_ref[...] + gate2_ref[0] * f
    o_ref[...] = _rms(x2, gf_ref[...])


def _final(dest_pad, yb, x1_all, route_rows, gate2, final_g, *, rows, tpb, per_row, row0):
    tm = min(256, rows)
    n_tiles = rows // tm
    blk0 = row0 // tm
    dest3 = dest_pad.reshape(-1, 1, 2 * tm)
    idx_blk = lambda f: pl.BlockSpec((1, 1, 2 * tm), f, memory_space=pltpu.SMEM)
    mod = (pl.BlockSpec((1, tm, D_MODEL), lambda i: (0, i, 0)) if per_row
           else pl.BlockSpec((1, 1, D_MODEL), lambda i: (i // tpb, 0, 0)))
    return pl.pallas_call(
        functools.partial(_final_kernel, tm=tm, n_tiles=n_tiles),
        grid=(n_tiles,),
        in_specs=[idx_blk(lambda i: (blk0, 0, 0)),
                  idx_blk(lambda i: (blk0 + jnp.minimum(i + 1, n_tiles - 1), 0, 0)),
                  pl.BlockSpec(memory_space=pl.ANY),
                  pl.BlockSpec((tm, D_MODEL), lambda i: (blk0 + i, 0)),
                  pl.BlockSpec((tm, LANES), lambda i: (blk0 + i, 0)),
                  mod, pl.BlockSpec((1, D_MODEL), lambda i: (0, 0))],
        out_specs=pl.BlockSpec((tm, D_MODEL), lambda i: (i, 0)),
        scratch_shapes=[pltpu.VMEM((2, 2, tm * TOK_ROWS, LANES), F32), pltpu.SemaphoreType.DMA((2,))],
        out_shape=jax.ShapeDtypeStruct((rows, D_MODEL), F32),
        compiler_params=_params(1), name="final_sample" if per_row else "final_prompt",
    )(dest3, dest3, yb, x1_all, route_rows, gate2, final_g.reshape(1, -1))


def _rope_tables(pos):
    inv = ROPE_THETA ** (-jnp.arange(HALF, dtype=F32) / HALF)
    ang = pos.astype(F32)[:, None] * inv[None, :]
    cos = jnp.tile(jnp.cos(ang), (1, LANES // HALF))
    sin = jnp.sin(ang)
    sin_s = jnp.tile(jnp.concatenate([-sin, sin], axis=1), (1, LANES // HD))
    return cos, sin_s


def _pack_w_in(w_in):
    gl = w_in[:, _C_G:_C_G + 3 * N_HEADS].reshape(D_MODEL, 3, N_KV, QPK)
    gcols = []
    for k in range(N_KV):
        gk = gl[:, :, k, :].reshape(D_MODEL, 3 * QPK)
        gcols.append(jnp.pad(gk, ((0, 0), (0, LANES - 3 * QPK))))
    return jnp.concatenate([w_in[:, :_C_G]] + gcols, axis=1).astype(BF16)


def _pack_cmp_weights(cmp_w1, cmp_w2, bias, cmp_b2):
    w1 = cmp_w1.reshape(2, 2, CMP_STRIDE, HD, CMP_HID)
    eye = jnp.eye(N_KV, dtype=F32)
    w1p = jnp.einsum('crsdh,pk->cspdrkh', w1, eye).reshape(2, CMP_STRIDE * KV_W, 2 * N_KV * CMP_HID)
    w2p = jnp.einsum('chd,pk->cphkd', cmp_w2, eye).reshape(2, N_KV * CMP_HID, KV_W)
    b1p = jnp.tile(bias, (1, N_KV)).reshape(2, 1, N_KV * CMP_HID)
    b2p = jnp.tile(cmp_b2, (1, N_KV)).reshape(2, 1, KV_W)
    return w1p.astype(BF16), b1p, w2p.astype(BF16), b2p


def _band(n_cmp_pad, n_cmp, n_blk_pad, n_blk):
    n = np.arange(n_cmp_pad)[:, None]
    b = np.arange(n_blk_pad)[None, :]
    r = SEL_BLOCK // CMP_STRIDE
    m = (n >= r * b - 1) & (n <= r * b + r - 1) & (n < n_cmp) & (b < n_blk)
    return jnp.asarray(m.astype(np.float32))


def _expand(t, kc):
    n_chunks = t // kc
    key = np.arange(t).reshape(n_chunks, 1, kc)
    blk = np.arange(t // SEL_BLOCK).reshape(1, -1, 1)
    return jnp.asarray((key // SEL_BLOCK == blk).astype(np.float32), BF16)


def kernel(x_prompt, x_sample, c_prompt, c_sample, cache_cmp_kv, cache_sel_kv, cache_win_kv, state_conv, page_table,
           ln1_g, ln2_g, w_ada, b_ada, w_in, w_conv, cmp_pos, cmp_w1, cmp_b1, cmp_w2, cmp_b2, g_out_conv, g_out_attn,
           w_out, w_route_group, b_route_group, w_route_expert, b_route_expert, w_gate, w_up, w_down, final_g):
    depth = w_in.shape[0]
    assert depth == 1, "single-layer step"
    nb, t, _ = x_prompt.shape
    ns, ts, _ = x_sample.shape
    assert ts == 1 and t % 512 == 0 and t >= WINDOW + Q_BLOCK
    n_pool = cache_cmp_kv.shape[1]
    n_pages = page_table.shape[1]
    past = n_pages * PAGE
    wb = cache_win_kv.shape[2]
    assert wb == WINDOW
    l = 0

    n_c = nb + ns
    c_all = jnp.pad(jnp.concatenate([c_prompt, c_sample], axis=0), ((0, (-n_c) % SUBLANES), (0, 0)))
    mods = _ada(c_all, w_ada[l], b_ada[l])
    sh1, sc1, ga1, sh2, sc2, ga2 = [mods[:, j * D_MODEL:(j + 1) * D_MODEL] for j in range(6)]
    pr = lambda a: a[0:nb].reshape(nb, 1, D_MODEL)
    sr = lambda a: a[nb:nb + ns].reshape(1, ns, D_MODEL)

    w_pack = _pack_w_in(w_in[l])
    wconv8 = jnp.pad(w_conv[l], ((0, SUBLANES - CONV_K), (0, 0)))
    cos_p, sin_p = _rope_tables(jnp.arange(t, dtype=I32))
    cos_s, sin_s = _rope_tables(jnp.full((1,), past, I32))
    xp2 = x_prompt.reshape(nb * t, D_MODEL)
    xs2 = x_sample.reshape(ns, D_MODEL)
    (conv_p, cst_p, q_p, kvc_p, kvc_rows_p, kvs_rows_p, kvw_rows_p, ks_p, vs_p, kw_p, vw_p, gates_p) = _proj(
        xp2, ln1_g[l], pr(sc1), pr(sh1), w_pack, wconv8, cos_p, sin_p, nb=nb, t=t, sample=False)
    (conv_s, cst_s, q_s, _, kvc_rows_s, kvs_rows_s, kvw_rows_s, _, _, _, _, gates_s) = _proj(
        xs2, ln1_g[l], sr(sc1), sr(sh1), w_pack, wconv8, cos_s, sin_s, nb=ns, t=1, sample=True,
        prev=(state_conv[l][:, 0], state_conv[l][:, 1]))

    bias = _cmpbias(cmp_pos[l], cmp_w1[l], cmp_b1[l])
    w1p, b1p, w2p, b2p = _pack_cmp_weights(cmp_w1[l], cmp_w2[l], bias, cmp_b2[l])
    pp = t // PAGE
    cos_cp, sin_cp = _rope_tables((jnp.arange(t // CMP_STRIDE, dtype=I32) + 2) * CMP_STRIDE - 1)
    ck_p, cv_p = _cmp(kvc_p.reshape(nb * pp, SUBLANES, CHUNK_ROW), jnp.arange(nb * pp, dtype=I32), nb, pp,
                      w1p, b1p, w2p, b2p, cos_cp, sin_cp, "cmp_prompt", tiles=False)
    pt_flat = page_table.reshape(-1).astype(I32)
    cos_cs, sin_cs = _rope_tables((jnp.arange(past // CMP_STRIDE, dtype=I32) + 2) * CMP_STRIDE - 1)
    to_tiles = lambda a: a.transpose(0, 2, 3, 4, 1)
    ck_s, cv_s = _cmp(to_tiles(cache_cmp_kv[l]), pt_flat, ns, n_pages,
                      w1p, b1p, w2p, b2p, cos_cs, sin_cs, "cmp_sample", tiles=True)

    n_chunk_p = t // CMP_STRIDE
    n_blk_p = t // SEL_BLOCK
    band_p = _band(n_chunk_p, n_chunk_p - 1, n_blk_p, n_blk_p)
    attn_p = _attn_prompt(q_p, ck_p, cv_p, ks_p, vs_p, kw_p, vw_p, gates_p, band_p.T, _expand(t, ATTN_KEY_CHUNK), nb=nb, t=t)

    n_chunk_s = past // CMP_STRIDE
    n_sel_s = -(-(past + 1) // SEL_BLOCK)
    nbp = -(-n_sel_s // LANES) * LANES
    band_s = _band(n_chunk_s, (past + 1) // CMP_STRIDE - 1, nbp, n_sel_s)
    q3 = q_s.reshape(N_HEADS, ns, HD).transpose(1, 0, 2).astype(F32)
    gs = gates_s.reshape(ns, N_KV, LANES)[:, :, :3 * QPK].reshape(ns, N_KV, 3, QPK)
    gates_hm = jnp.pad(gs.transpose(0, 1, 3, 2).reshape(ns, N_HEADS, 3), ((0, 0), (0, 0), (0, LANES - 3)))
    rpt = 2 * N_KV
    new_rows = lambda a: jnp.pad(a.reshape(ns, rpt, HD), ((0, 0), (0, SUBLANES - rpt), (0, 0)))
    attn_s = _attn_sample(q3, ck_s, cv_s, band_s, to_tiles(cache_sel_kv[l]), pt_flat,
                          new_rows(kvs_rows_s), to_tiles(cache_win_kv[l]), new_rows(kvw_rows_s),
                          gates_hm, nb=ns, n_pages=n_pages, past=past, n_sel_blocks=n_sel_s).reshape(ns, ATTN_W)

    total = nb * t + ns
    w_out_b = w_out[l].astype(BF16)
    w_route = jnp.pad(jnp.concatenate([w_route_group[l], w_route_expert[l]], axis=1),
                      ((0, 0), (0, LANES - N_GROUPS - N_EXPERTS)))
    b_route = jnp.pad(jnp.concatenate([b_route_group[l], b_route_expert[l]]), (0, LANES - N_GROUPS - N_EXPERTS))
    tile_pad = lambda a: jnp.pad(a, ((0, TOKEN_TILE - ns), (0, 0)))
    smod = lambda a: tile_pad(a[nb:nb + ns]).reshape(1, TOKEN_TILE, D_MODEL)
    x1_all, hp_all, lg_all = _outp(
        (xp2, conv_p, attn_p.reshape(nb * t, ATTN_W), pr(ga1), pr(sc2), pr(sh2)),
        (tile_pad(xs2), tile_pad(conv_s), tile_pad(attn_s), smod(ga1), smod(sc2), smod(sh2)),
        g_out_conv[l], g_out_attn[l], w_out_b, ln2_g[l], w_route, tpb=t // TOKEN_TILE)

    route, counts = _route(lg_all, b_route.reshape(1, LANES), total)
    e = route[:total, 0:2].astype(I32)
    rank = route[:total, 4:6].astype(I32)
    cnt = counts[0, :N_EXPERTS].astype(I32)
    padded = (cnt + EXPERT_ROWS - 1) // EXPERT_ROWS * EXPERT_ROWS
    pad_end = jnp.cumsum(padded)
    pad_start = pad_end - padded
    m_slots = total * 2
    n_blocks = -(-(m_slots + N_EXPERTS * (EXPERT_ROWS - 1)) // EXPERT_ROWS)
    n_slots = n_blocks * EXPERT_ROWS
    dest = jnp.clip(pad_start[e] + rank, 0, n_slots - 1)
    blk_start = jnp.arange(n_blocks, dtype=I32) * EXPERT_ROWS
    blk_e = jnp.minimum(jnp.sum((pad_end[None, :] <= blk_start[:, None]).astype(I32), axis=1), N_EXPERTS - 1)
    n_dump = 2 * (x1_all.shape[0] - total)
    dest_pad = jnp.concatenate([dest, n_slots + jnp.arange(n_dump, dtype=I32).reshape(-1, 2)], axis=0)

    zstart = jnp.concatenate([pad_start + cnt, pad_end[-1:]])
    zcnt = jnp.concatenate([padded - cnt, jnp.zeros((1,), I32)])
    xb = _dispatch(dest_pad, zstart, zcnt, hp_all, n_slots + n_dump, n_blocks)
    yb = _experts(blk_e, xb, w_gate[l], w_up[l], w_down[l], n_blocks)
    y_p = _final(dest_pad, yb, x1_all, route, pr(ga2), final_g, rows=nb * t, tpb=t // 256, per_row=False, row0=0)
    y_s = _final(dest_pad, yb, x1_all, route, sr(ga2), final_g, rows=ns, tpb=1, per_row=True, row0=nb * t)

    kv_shape = (2, N_KV, HD)
    y_prompt = y_p.reshape(nb, t, D_MODEL)
    y_sample = y_s.reshape(ns, 1, D_MODEL)
    new_cmp_prompt = kvc_rows_p.reshape((1, nb, t) + kv_shape)
    new_cmp_sample = kvc_rows_s.reshape((1, ns, 1) + kv_shape)
    new_sel_prompt = kvs_rows_p.reshape((1, nb, t) + kv_shape)
    new_sel_sample = kvs_rows_s.reshape((1, ns, 1) + kv_shape)
    new_win_prompt = kvw_rows_p.reshape((nb, t) + kv_shape)[:, t - WINDOW:][None]
    new_win_sample = jnp.concatenate([cache_win_kv[l][:, 1:], kvw_rows_s.reshape((ns, 1) + kv_shape)], axis=1)[None]
    new_conv_prompt = cst_p[:, SUBLANES - (CONV_K - 1):][None]
    new_conv_sample = jnp.stack([state_conv[l][:, 1], cst_s], axis=1)[None]
    return (y_prompt, y_sample, new_cmp_prompt, new_cmp_sample, new_sel_prompt, new_sel_sample,
            new_win_prompt, new_win_sample, new_conv_prompt, new_conv_sample)
```

```python
import functools

import numpy as np
import jax
import jax.numpy as jnp
from jax import lax
from jax.experimental import pallas as pl
from jax.experimental.pallas import tpu as pltpu

F32 = jnp.float32
BF16 = jnp.bfloat16
I32 = jnp.int32

D_MODEL = 1024
CONV_W = 512
ATTN_W = 512
HD = 64
HALF = HD // 2
N_HEADS = 8
N_KV = 2
QPK = 4
KV_W = N_KV * HD
CONV_K = 3
PAGE = 128
CMP_STRIDE = 16
CMP_HID = 128
SEL_BLOCK = 64
N_SEL = 16
WINDOW = 512
Q_BLOCK = 128
ROPE_THETA = 10000.0
N_GROUPS = 4
EPG = 8
N_EXPERTS = 32
D_EXPERT = 512
NORM_EPS = 1e-6
NEG_INF = -1e30
FORCE_SCORE = 1e4
LANES = 128
SUBLANES = 8
CHUNK_ROW = CMP_STRIDE * 2 * KV_W
VMEM_LIMIT = 56 * 1024 * 1024

_NT = (((1,), (1,)), ((), ()))
Q_SCALE = HD ** -0.5 * 1.4426950408889634


def _params(n_axes):
    return pltpu.CompilerParams(dimension_semantics=("arbitrary",) * n_axes,
                                vmem_limit_bytes=VMEM_LIMIT)


def _rms(x, g):
    return x * lax.rsqrt(jnp.mean(x * x, axis=-1, keepdims=True) + NORM_EPS) * g


def _rope128(x, cos, sin_signed, first_half):
    xr = jnp.where(first_half, pltpu.roll(x, LANES - HALF, 1), pltpu.roll(x, HALF, 1))
    return x * cos + xr * sin_signed


def _first_half_mask(rows):
    lane = lax.broadcasted_iota(I32, (rows, LANES), 1)
    return (lane % HD) < HALF


def _ada_kernel(c_ref, w_ref, b_ref, o_ref):
    c = c_ref[...]
    s = c * jax.nn.sigmoid(c)
    o_ref[...] = jnp.dot(s.astype(BF16), w_ref[...].astype(BF16), preferred_element_type=F32) + b_ref[...]


def _ada(c_all, w_ada, b_ada):
    m, d = c_all.shape
    n = w_ada.shape[1]
    tn = 1024
    return pl.pallas_call(
        _ada_kernel,
        grid=(n // tn,),
        in_specs=[pl.BlockSpec((m, d), lambda j: (0, 0)),
                  pl.BlockSpec((d, tn), lambda j: (0, j)),
                  pl.BlockSpec((1, tn), lambda j: (0, j))],
        out_specs=pl.BlockSpec((m, tn), lambda j: (0, j)),
        out_shape=jax.ShapeDtypeStruct((m, n), F32),
        compiler_params=_params(1),
        name="ada",
    )(c_all, w_ada, b_ada.reshape(1, n))


_C_B, _C_C, _C_U, _C_Q, _C_KVC, _C_KVS, _C_KVW, _C_G, _C_END = 0, 512, 1024, 1536, 2048, 2304, 2560, 2816, 3072


def _proj_kernel(*refs, tm, tpb, sample):
    if sample:
        (x_ref, g1_ref, sc_ref, sh_ref, w_ref, wc_ref, cos_ref, sin_ref, p0_ref, p1_ref,
         conv_ref, cst_ref, q_ref, kvc_ref, kvc_il_ref, kvs_ref, kvw_ref, ks_ref, vs_ref, kw_ref, vw_ref, gate_ref,
         ilbuf) = refs
        vbuf = None
    else:
        (x_ref, g1_ref, sc_ref, sh_ref, w_ref, wc_ref, cos_ref, sin_ref,
         conv_ref, cst_ref, q_ref, kvc_ref, kvc_il_ref, kvs_ref, kvw_ref, ks_ref, vs_ref, kw_ref, vw_ref, gate_ref,
         ilbuf, vbuf) = refs
    i = pl.program_id(0)
    x = x_ref[...]
    h = _rms(x, g1_ref[...]) * (1.0 + sc_ref[0]) + sh_ref[0]
    hb = h.astype(BF16)

    zc = jnp.dot(hb, w_ref[:, _C_B:_C_Q], preferred_element_type=F32)
    b_g = zc[:, 0:CONV_W]
    v = zc[:, CONV_W:2 * CONV_W] * zc[:, 2 * CONV_W:3 * CONV_W]
    wc = wc_ref[...]
    if sample:
        y = wc[0:1] * p0_ref[...] + wc[1:2] * p1_ref[...] + wc[2:3] * v
        cst_ref[...] = v
    else:
        @pl.when(i % tpb == 0)
        def _():
            vbuf[0:SUBLANES, :] = jnp.zeros((SUBLANES, CONV_W), F32)
        vbuf[SUBLANES:SUBLANES + tm, :] = v
        y = wc[0:1] * vbuf[pl.ds(SUBLANES - 2, tm), :] + wc[1:2] * vbuf[pl.ds(SUBLANES - 1, tm), :] + wc[2:3] * v
        tail = vbuf[tm:tm + SUBLANES, :]
        cst_ref[0] = tail
        vbuf[0:SUBLANES, :] = tail
    conv_ref[...] = b_g * y

    cos = cos_ref[...]
    sin_s = sin_ref[...]
    first = _first_half_mask(tm)

    zq = jnp.dot(hb, w_ref[:, _C_Q:_C_KVC], preferred_element_type=F32)
    for gq in range(ATTN_W // LANES):
        qr = _rope128(zq[:, gq * LANES:(gq + 1) * LANES], cos, sin_s, first) * Q_SCALE
        q_ref[0, 2 * gq] = qr[:, 0:HD].astype(BF16)
        q_ref[0, 2 * gq + 1] = qr[:, HD:LANES].astype(BF16)

    def store_rows(out_ref, halves):
        for j in range(2 * N_KV):
            piece = halves[j // N_KV]
            if j % N_KV == 1:
                piece = pltpu.roll(piece, HD, 1)
            ilbuf[pl.ds(j, tm, stride=2 * N_KV), :] = piece
        out_ref[...] = ilbuf[:, 0:HD]

    zkv = jnp.dot(hb, w_ref[:, _C_KVC:_C_G], preferred_element_type=F32)
    kvc_ref[...] = zkv[:, 0:2 * KV_W]
    store_rows(kvc_il_ref, (zkv[:, 0:KV_W], zkv[:, KV_W:2 * KV_W]))
    for base, kv_ref, kh_ref, vh_ref in ((2 * KV_W, kvs_ref, ks_ref, vs_ref), (4 * KV_W, kvw_ref, kw_ref, vw_ref)):
        kr = _rope128(zkv[:, base:base + KV_W], cos, sin_s, first)
        vv = zkv[:, base + KV_W:base + 2 * KV_W]
        store_rows(kv_ref, (kr, vv))
        lane = lax.broadcasted_iota(I32, (tm, LANES), 1)
        for k in range(N_KV):
            kh_ref[0, k] = kr[:, k * HD:(k + 1) * HD].astype(BF16)
            vk = vv if k == 0 else pltpu.roll(vv, HD, 1)
            vh_ref[0, k] = jnp.where(lane < HD, vk, jnp.where(lane == HD, 1.0, 0.0)).astype(BF16)

    zg = jnp.dot(hb, w_ref[:, _C_G:_C_END], preferred_element_type=F32)
    gate_ref[...] = jax.nn.sigmoid(zg)


def _proj(x2d, g1, sc, sh, w_pack, w_conv, cos_t, sin_t, *, nb, t, sample, prev=None):
    rows = nb * t
    tm = min(512, rows) if not sample else rows
    tpb = (t // tm) if not sample else 1
    n_tiles = rows // tm
    f = lambda a: jax.ShapeDtypeStruct(a, F32)
    b = lambda a: jax.ShapeDtypeStruct(a, BF16)
    if sample:
        mod_spec = pl.BlockSpec((1, tm, D_MODEL), lambda i: (0, 0, 0))
        tab_spec = pl.BlockSpec((1, LANES), lambda i: (0, 0))
        cst_shape, cst_spec = f((rows, CONV_W)), pl.BlockSpec((tm, CONV_W), lambda i: (0, 0))
        hm = lambda i: (0, 0, i, 0)
        hb_, ht_ = 1, rows
    else:
        mod_spec = pl.BlockSpec((1, 1, D_MODEL), lambda i: (i // tpb, 0, 0))
        tab_spec = pl.BlockSpec((tm, LANES), lambda i: (i % tpb, 0))
        cst_shape, cst_spec = f((nb, SUBLANES, CONV_W)), pl.BlockSpec((1, SUBLANES, CONV_W), lambda i: (i // tpb, 0, 0))
        hm = lambda i: (i // tpb, 0, i % tpb, 0)
        hb_, ht_ = nb, t
    row = lambda w: pl.BlockSpec((tm, w), lambda i: (i, 0))
    in_specs = [row(D_MODEL), pl.BlockSpec((1, D_MODEL), lambda i: (0, 0)), mod_spec, mod_spec,
                pl.BlockSpec((D_MODEL, _C_END), lambda i: (0, 0)),
                pl.BlockSpec((SUBLANES, CONV_W), lambda i: (0, 0)), tab_spec, tab_spec]
    args = [x2d, g1.reshape(1, D_MODEL), sc, sh, w_pack, w_conv, cos_t, sin_t]
    scratch = [pltpu.VMEM((2 * N_KV * tm, LANES), F32)]
    if sample:
        in_specs += [row(CONV_W), row(CONV_W)]
        args += [prev[0], prev[1]]
    else:
        scratch.append(pltpu.VMEM((tm + SUBLANES, CONV_W), F32))
    il_rows = 2 * N_KV * rows
    il = pl.BlockSpec((2 * N_KV * tm, HD), lambda i: (i, 0))
    out_shape = [f((rows, CONV_W)), cst_shape, b((hb_, N_HEADS, ht_, HD)),
                 f((rows, 2 * KV_W)), f((il_rows, HD)), f((il_rows, HD)), f((il_rows, HD)),
                 b((hb_, N_KV, ht_, HD)), b((hb_, N_KV, ht_, LANES)), b((hb_, N_KV, ht_, HD)), b((hb_, N_KV, ht_, LANES)),
                 f((rows, 2 * LANES))]
    out_specs = [row(CONV_W), cst_spec, pl.BlockSpec((1, N_HEADS, tm, HD), hm),
                 row(2 * KV_W), il, il, il,
                 pl.BlockSpec((1, N_KV, tm, HD), hm), pl.BlockSpec((1, N_KV, tm, LANES), hm),
                 pl.BlockSpec((1, N_KV, tm, HD), hm), pl.BlockSpec((1, N_KV, tm, LANES), hm),
                 row(2 * LANES)]
    return pl.pallas_call(
        functools.partial(_proj_kernel, tm=tm, tpb=tpb, sample=sample),
        grid=(n_tiles,), in_specs=in_specs, out_specs=out_specs, out_shape=out_shape,
        scratch_shapes=scratch, compiler_params=_params(1),
        name="proj_sample" if sample else "proj_prompt",
    )(*args)


def _cmpbias_kernel(pos_ref, w_ref, b1_ref, o_ref):
    for c in range(2):
        o_ref[c:c + 1, :] = jnp.sum(pos_ref[c] * w_ref[c], axis=0, keepdims=True) + b1_ref[c:c + 1, :]


def _cmpbias(cmp_pos, cmp_w1, cmp_b1):
    n = cmp_pos.shape[1] * cmp_pos.shape[2]
    return pl.pallas_call(
        _cmpbias_kernel,
        out_shape=jax.ShapeDtypeStruct((2, CMP_HID), F32),
        compiler_params=pltpu.CompilerParams(vmem_limit_bytes=VMEM_LIMIT),
        name="cmpbias",
    )(cmp_pos.reshape(2, n, 1), cmp_w1.reshape(2, n, CMP_HID), cmp_b1)


def _cmp_kernel(pt_ref, *refs, ppt, tiles):
    pages = refs[:ppt + 1]
    if tiles:
        unfold_ref = refs[ppt + 1]
        refs = refs[1:]
    w1_ref, b1_ref, w2_ref, b2_ref, cos_ref, sin_ref, ck_ref, cv_ref, lhs, pbuf = refs[ppt + 1:]
    r = ppt * SUBLANES
    rk = r + SUBLANES
    first = _first_half_mask(r)
    low = lax.broadcasted_iota(I32, (SUBLANES, LANES), 1) < HD

    def tap_tile(j, c, s, y):
        if tiles:
            return y[s * SUBLANES:(s + 1) * SUBLANES, c * KV_W:(c + 1) * KV_W]
        return pages[j][0, :, s * 2 * KV_W + c * KV_W:s * 2 * KV_W + (c + 1) * KV_W]

    for j in range(ppt + 1):
        y = None
        if tiles:
            a = pages[j][...].reshape(2 * KV_W, PAGE).astype(BF16)
            y = lax.dot_general(unfold_ref[...], a, _NT, preferred_element_type=F32)
        for c in range(2):
            for sp in range(CMP_STRIDE // 2):
                t0 = tap_tile(j, c, 2 * sp, y)
                t1 = tap_tile(j, c, 2 * sp + 1, y)
                lhs[c, j * SUBLANES:(j + 1) * SUBLANES, sp * LANES:(sp + 1) * LANES] = (
                    jnp.where(low, t0, pltpu.roll(t1, HD, 1)))
                lhs[c, rk + j * SUBLANES:rk + (j + 1) * SUBLANES, sp * LANES:(sp + 1) * LANES] = (
                    jnp.where(low, pltpu.roll(t0, HD, 1), t1))
    for c in range(2):
        p = jnp.dot(lhs[c].astype(BF16), w1_ref[c], preferred_element_type=F32)
        hids = []
        for k in range(N_KV):
            pbuf[...] = p[k * rk:(k + 1) * rk, CMP_HID:2 * CMP_HID]
            hids.append(p[k * rk:k * rk + r, 0:CMP_HID] + pbuf[pl.ds(1, r), :])
        hid = jnp.concatenate(hids, axis=1) + b1_ref[c]
        act = jax.nn.gelu(hid)
        comp = jnp.dot(act.astype(BF16), w2_ref[c], preferred_element_type=F32) + b2_ref[c]
        if c == 0:
            comp = _rope128(comp, cos_ref[...], sin_ref[...], first)
            out = ck_ref
        else:
            out = cv_ref
        for k in range(N_KV):
            out[0, k] = comp[:, k * HD:(k + 1) * HD]


def _cmp(pages, pt_flat, nb, n_pages, w1p, b1p, w2p, b2p, cos_c, sin_c, name, tiles):
    ppt = min(32, n_pages)
    n_tiles = n_pages // ppt
    r = ppt * SUBLANES
    n_chunk = n_pages * SUBLANES
    zeros = (0,) * (pages.ndim - 1)

    def page_map(j):
        return lambda b, t, pt: (pt[b * n_pages + t * ppt + j],) + zeros

    def next_map(b, t, pt):
        return (pt[b * n_pages + jnp.minimum(t * ppt + ppt, n_pages - 1)],) + zeros

    page_blk = (None, 2, N_KV, HD, PAGE) if tiles else (1, SUBLANES, CHUNK_ROW)
    in_specs = [pl.BlockSpec(page_blk, page_map(j)) for j in range(ppt)]
    in_specs.append(pl.BlockSpec(page_blk, next_map))
    const = lambda shp: pl.BlockSpec(shp, lambda b, t, pt: (0,) * len(shp))
    extra = []
    if tiles:
        row = np.arange(PAGE)
        tok = (row % SUBLANES) * CMP_STRIDE + row // SUBLANES
        extra = [jnp.asarray(tok[:, None] == np.arange(PAGE)[None, :], BF16)]
        in_specs.append(const((PAGE, PAGE)))
    in_specs += [const(w1p.shape), const(b1p.shape), const(w2p.shape), const(b2p.shape),
                 pl.BlockSpec((r, LANES), lambda b, t, pt: (t, 0)), pl.BlockSpec((r, LANES), lambda b, t, pt: (t, 0))]
    hm = pl.BlockSpec((1, N_KV, r, HD), lambda b, t, pt: (b, 0, t, 0))
    grid_spec = pltpu.PrefetchScalarGridSpec(
        num_scalar_prefetch=1, grid=(nb, n_tiles), in_specs=in_specs, out_specs=[hm, hm],
        scratch_shapes=[pltpu.VMEM((2, N_KV * (r + SUBLANES), CMP_STRIDE * HD), F32),
                        pltpu.VMEM((r + SUBLANES, CMP_HID), F32)])
    return pl.pallas_call(
        functools.partial(_cmp_kernel, ppt=ppt, tiles=tiles),
        grid_spec=grid_spec,
        out_shape=[jax.ShapeDtypeStruct((nb, N_KV, n_chunk, HD), F32)] * 2,
        compiler_params=_params(2), name=name,
    )(pt_flat, *([pages] * (ppt + 1)), *extra, w1p, b1p, w2p, b2p, cos_c, sin_c)


def _softmax_rows(s, valid):
    s = jnp.where(valid, s, NEG_INF)
    m = jnp.max(s, axis=-1, keepdims=True)
    e = jnp.exp2(s - m)
    return e / jnp.sum(e, axis=-1, keepdims=True)


def _attn_p_kernel(q_ref, ck_ref, cv_ref, ks_ref, vs_ref, kw_ref, vw_ref, gate_ref, band_ref, exp_ref, o_ref,
                   *, n_cmp_pad, n_blk, kc, hg, wc):
    qb = pl.program_id(2)
    start = qb * Q_BLOCK
    tpos = start + lax.broadcasted_iota(I32, (Q_BLOCK, 1), 0)
    groups = range(QPK // hg)
    rows = hg * Q_BLOCK

    def q_of(g):
        return q_ref[0, g * hg:(g + 1) * hg].reshape(rows, HD)

    def biased(s, bias):
        width = s.shape[-1]
        return (s.reshape(hg, Q_BLOCK, width) + bias[None]).reshape(rows, width)

    ck = ck_ref[0, 0].astype(BF16)
    cv = cv_ref[0, 0].astype(BF16)
    cmp_end = (lax.broadcasted_iota(I32, (1, n_cmp_pad), 1) + 2) * CMP_STRIDE - 1
    bias_c = jnp.where(cmp_end <= tpos, 0.0, NEG_INF)
    o_c = []
    pcs = jnp.zeros((Q_BLOCK, n_cmp_pad), F32)
    for g in groups:
        s_c = biased(lax.dot_general(q_of(g), ck, _NT, preferred_element_type=F32), bias_c)
        m_c = jnp.maximum(jnp.max(s_c, axis=-1, keepdims=True), 0.5 * NEG_INF)
        e_c = jnp.exp2(s_c - m_c)
        l_c = jnp.sum(e_c, axis=-1, keepdims=True)
        p_c = e_c * (1.0 / jnp.where(l_c > 0.0, l_c, 1.0))
        o_c.append(jnp.dot(p_c.astype(BF16), cv, preferred_element_type=F32))
        for h in range(hg):
            pcs = pcs + p_c[h * Q_BLOCK:(h + 1) * Q_BLOCK]

    imp =lax.dot_general(band_ref[...], pcs, _NT, preferred_element_type=F32,
                          precision=lax.Precision.HIGHEST)
    blk = lax.broadcasted_iota(I32, (n_blk, Q_BLOCK), 0)
    tlane = start + lax.broadcasted_iota(I32, (1, Q_BLOCK), 1)
    cur = tlane // SEL_BLOCK
    causal = blk * SEL_BLOCK <= tlane
    forced = causal & ((blk == 0) | (blk == cur) | (blk == cur - 1))
    score = jnp.where(forced, FORCE_SCORE, jnp.where(causal, imp, -1.0))
    rank = jnp.zeros((n_blk, Q_BLOCK), F32)
    for bp in range(n_blk):
        other = score[bp:bp + 1, :]
        beats = (other > score) | ((other == score) & (bp < blk))
        rank = rank + beats.astype(F32)
    sel_t = (rank < float(min(N_SEL, n_blk))).astype(BF16)
    eye = (lax.broadcasted_iota(I32, (Q_BLOCK, Q_BLOCK), 0)
           == lax.broadcasted_iota(I32, (Q_BLOCK, Q_BLOCK), 1)).astype(BF16)
    sel = lax.dot_general(eye, sel_t, _NT, preferred_element_type=F32).astype(BF16)

    n_chunks = (start + Q_BLOCK + kc - 1) // kc

    def online(state, kj, vj, bias):
        out = []
        for g in groups:
            m_i, acc = state[g]
            s = biased(lax.dot_general(q_of(g), kj, _NT, preferred_element_type=F32), bias)
            m_new = jnp.maximum(m_i, jnp.max(s, axis=-1, keepdims=True))
            p = jnp.exp2(s - m_new).astype(BF16)
            out.append((m_new, jnp.exp2(m_i - m_new) * acc + jnp.dot(p, vj, preferred_element_type=F32)))
        return tuple(out)

    def step(j, state, causal_chunk):
        off = pl.multiple_of(j * kc, kc)
        mexp = jnp.dot(sel, exp_ref[j], preferred_element_type=F32)
        bias = mexp * (-NEG_INF) + NEG_INF
        if causal_chunk:
            keypos = off + lax.broadcasted_iota(I32, (1, kc), 1)
            bias = jnp.where(keypos <= tpos, bias, NEG_INF)
        return online(state, ks_ref[0, 0, pl.ds(off, kc), :], vs_ref[0, 0, pl.ds(off, kc), :], bias)

    init = tuple((jnp.full((rows, 1), NEG_INF, F32), jnp.zeros((rows, LANES), F32)) for _ in groups)
    state = lax.fori_loop(0, n_chunks - 1, lambda j, c: step(j, c, False), init)
    sel_state = step(n_chunks - 1, state, True)

    s0 = jnp.maximum(start - WINDOW, 0)
    win_state = init
    for c in range((WINDOW + Q_BLOCK) // wc):
        off = pl.multiple_of(s0 + c * wc, Q_BLOCK)
        dist = tpos - (off + lax.broadcasted_iota(I32, (1, wc), 1))
        bias_w = jnp.where((dist >= 0) & (dist <= WINDOW), 0.0, NEG_INF)
        win_state = online(win_state, kw_ref[0, 0, pl.ds(off, wc), :], vw_ref[0, 0, pl.ds(off, wc), :], bias_w)

    gt = gate_ref[...]
    for g in groups:
        acc_s = sel_state[g][1]
        acc_w = win_state[g][1]
        o_s = acc_s[:, 0:HD] * (1.0 / acc_s[:, HD:HD + 1])
        o_w = acc_w[:, 0:HD] * (1.0 / acc_w[:, HD:HD + 1])
        for hh in range(hg):
            h = g * hg + hh
            rs = slice(hh * Q_BLOCK, (hh + 1) * Q_BLOCK)
            o = (gt[:, h:h + 1] * o_c[g][rs] + gt[:, QPK + h:QPK + h + 1] * o_s[rs]
                 + gt[:, 2 * QPK + h:2 * QPK + h + 1] * o_w[rs])
            o_ref[0, :, h * HD:(h + 1) * HD] = o


ATTN_HEAD_GROUP = 4
ATTN_KEY_CHUNK = 512
ATTN_WIN_CHUNK = 640


def _attn_prompt(q_hm, ck, cv, ks, vs, kw, vw, gates, band, expand, *, nb, t):
    n_qb = t // Q_BLOCK
    n_cmp_pad = ck.shape[2]
    n_blk = band.shape[0]
    kc = expand.shape[2]
    kv_spec = lambda n, w=HD: pl.BlockSpec((1, 1, n, w), lambda b, k, i: (b, k, 0, 0))
    return pl.pallas_call(
        functools.partial(_attn_p_kernel, n_cmp_pad=n_cmp_pad, n_blk=n_blk, kc=kc, hg=ATTN_HEAD_GROUP, wc=ATTN_WIN_CHUNK),
        grid=(nb, N_KV, n_qb),
        in_specs=[pl.BlockSpec((1, QPK, Q_BLOCK, HD), lambda b, k, i: (b, k, i, 0)),
                  kv_spec(n_cmp_pad), kv_spec(n_cmp_pad), kv_spec(t), kv_spec(t, LANES), kv_spec(t), kv_spec(t, LANES),
                  pl.BlockSpec((Q_BLOCK, LANES), lambda b, k, i: (b * n_qb + i, k)),
                  pl.BlockSpec(band.shape, lambda b, k, i: (0, 0)),
                  pl.BlockSpec(expand.shape, lambda b, k, i: (0, 0, 0))],
        out_specs=pl.BlockSpec((1, Q_BLOCK, QPK * HD), lambda b, k, i: (b, i, k)),
        out_shape=jax.ShapeDtypeStruct((nb, t, ATTN_W), F32),
        compiler_params=_params(3), name="attn_prompt",
    )(q_hm, ck, cv, ks, vs, kw, vw, gates, band, expand)


def _attn_s1_kernel(q_ref, ck_ref, cv_ref, oc_ref, pcs_ref, *, n_chunk, past):
    q = q_ref[0]
    q16 = jnp.concatenate([q, jnp.zeros_like(q)], axis=0).astype(BF16)
    cmp_end = (lax.broadcasted_iota(I32, (1, n_chunk), 1) + 2) * CMP_STRIDE - 1
    valid = cmp_end <= past
    head = lax.broadcasted_iota(I32, (2 * N_HEADS, 1), 0)
    oc = jnp.zeros((2 * N_HEADS, HD), F32)
    pcs = []
    for k in range(N_KV):
        s = lax.dot_general(q16, ck_ref[0, k].astype(BF16), _NT, preferred_element_type=F32)
        p = _softmax_rows(s, valid) * valid.astype(F32)
        in_grp = (head >= k * QPK) & (head < (k + 1) * QPK)
        p = jnp.where(in_grp, p, 0.0)
        oc = oc + jnp.dot(p.astype(BF16), cv_ref[0, k].astype(BF16), preferred_element_type=F32)
        pcs.append(jnp.sum(p, axis=0, keepdims=True))
    oc_ref[0] = oc[0:N_HEADS]
    pcs_ref[0] = jnp.concatenate(pcs + [jnp.zeros((SUBLANES - N_KV, n_chunk), F32)], axis=0)


def _topk_s_kernel(pcs_ref, band_ref, idx_ref, *, n_sel_blocks, past):
    imp = jnp.dot(pcs_ref[...], band_ref[...], preferred_element_type=F32, precision=lax.Precision.HIGHEST)
    rows, nbp = imp.shape
    blk = lax.broadcasted_iota(I32, (rows, nbp), 1)
    cur = past // SEL_BLOCK
    causal = blk * SEL_BLOCK <= past
    forced = causal & ((blk == 0) | (blk == cur) | (blk == cur - 1))
    score = jnp.where(forced, FORCE_SCORE, jnp.where(causal, imp, -1.0))
    score = jnp.where(blk < n_sel_blocks, score, -2.0)
    lane = lax.broadcasted_iota(I32, (rows, LANES), 1)
    out = jnp.zeros((rows, LANES), I32)
    for r in range(min(N_SEL, n_sel_blocks)):
        m = jnp.max(score, axis=-1, keepdims=True)
        pick = jnp.min(jnp.where(score == m, blk, nbp), axis=-1, keepdims=True)
        out = jnp.where(lane == r, pick, out)
        score = jnp.where(blk == pick, -3.0, score)
    idx_ref[...] = out


def _attn_s2_kernel(pt_ref, idx_ref, *refs, n_pages, past, n_sel_blocks):
    ktiles, vtiles = refs[:N_SEL], refs[N_SEL:2 * N_SEL]
    q_ref, oc_ref, kvs_ref, wk_ref, wv_ref, kvw_ref, gate_ref, o_ref, kbuf, vbuf = refs[2 * N_SEL:]
    b = pl.program_id(0)
    k = pl.program_id(1)
    q = q_ref[0]
    q16f = jnp.concatenate([q, jnp.zeros_like(q)], axis=0)
    q16 = q16f.astype(BF16)
    head = lax.broadcasted_iota(I32, (N_HEADS, 1), 0)
    nk = N_SEL * PAGE
    lane = lax.broadcasted_iota(I32, (1, nk), 1)
    slot = lane // PAGE
    new_blk = n_sel_blocks - 1
    wb = wk_ref.shape[-1]
    wpos = past - wb + lax.broadcasted_iota(I32, (1, wb), 1)
    wdist = past - wpos
    valid_w = (wdist >= 0) & (wdist <= WINDOW) & (wpos >= 0)

    def attend(s, valid, v_t, k_new, v_new):
        s_new = jnp.sum(q16f * k_new, axis=-1, keepdims=True)
        s = jnp.where(valid, s, NEG_INF)
        m = jnp.maximum(jnp.max(s, axis=-1, keepdims=True), s_new)
        e = jnp.exp2(s - m)
        e_new = jnp.exp2(s_new - m)
        den = jnp.sum(e, axis=-1, keepdims=True) + e_new
        acc = lax.dot_general(e.astype(BF16), v_t, _NT, preferred_element_type=F32) + e_new * v_new
        return acc / den

    in_grp = (head >= k * QPK) & (head < (k + 1) * QPK)
    bvec = jnp.zeros((1, nk), I32)
    for j in range(N_SEL):
        kbuf[:, j * PAGE:(j + 1) * PAGE] = ktiles[j][...].astype(BF16)
        vbuf[:, j * PAGE:(j + 1) * PAGE] = vtiles[j][...].astype(BF16)
        bvec = jnp.where(slot == j, idx_ref[(b * N_KV + k) * LANES + j], bvec)
    tok = (bvec // 2) * PAGE + lane % PAGE
    valid = (tok // SEL_BLOCK == bvec) & (bvec < new_blk) & (tok <= past)
    s = jnp.dot(q16, kbuf[...], preferred_element_type=F32)
    o_s = attend(s, valid, vbuf[...], kvs_ref[0, pl.ds(k, 1), :], kvs_ref[0, pl.ds(N_KV + k, 1), :])
    sw = jnp.dot(q16, wk_ref[...].astype(BF16), preferred_element_type=F32)
    o_w = attend(sw, valid_w, wv_ref[...].astype(BF16), kvw_ref[0, pl.ds(k, 1), :], kvw_ref[0, pl.ds(N_KV + k, 1), :])
    g = gate_ref[0]
    part = jnp.where(in_grp, g[:, 1:2] * o_s[0:N_HEADS] + g[:, 2:3] * o_w[0:N_HEADS], 0.0)

    @pl.when(k == 0)
    def _():
        o_ref[0] = g[:, 0:1] * oc_ref[0] + part

    @pl.when(k > 0)
    def _():
        o_ref[0] = o_ref[0] + part


def _attn_sample(q3, ck, cv, band_s, sel_t, pt_flat, kvs_rows, win_t, kvw_rows, gates_hm,
                 *, nb, n_pages, past, n_sel_blocks):
    n_chunk = ck.shape[2]
    nbp = band_s.shape[1]
    oc, pcs = pl.pallas_call(
        functools.partial(_attn_s1_kernel, n_chunk=n_chunk, past=past),
        grid=(nb,),
        in_specs=[pl.BlockSpec((1, N_HEADS, HD), lambda b: (b, 0, 0)),
                  pl.BlockSpec((1, N_KV, n_chunk, HD), lambda b: (b, 0, 0, 0)),
                  pl.BlockSpec((1, N_KV, n_chunk, HD), lambda b: (b, 0, 0, 0))],
        out_specs=[pl.BlockSpec((1, N_HEADS, HD), lambda b: (b, 0, 0)),
                   pl.BlockSpec((1, SUBLANES, n_chunk), lambda b: (b, 0, 0))],
        out_shape=[jax.ShapeDtypeStruct((nb, N_HEADS, HD), F32), jax.ShapeDtypeStruct((nb, SUBLANES, n_chunk), F32)],
        compiler_params=_params(1), name="attn_sample_cmp",
    )(q3, ck, cv)
    idx = pl.pallas_call(
        functools.partial(_topk_s_kernel, n_sel_blocks=n_sel_blocks, past=past),
        out_shape=jax.ShapeDtypeStruct((nb * N_KV, LANES), I32),
        compiler_params=pltpu.CompilerParams(vmem_limit_bytes=VMEM_LIMIT), name="topk_sample",
    )(pcs[:, 0:N_KV, :].reshape(nb * N_KV, n_chunk), band_s)
    idx_flat = idx.reshape(-1)

    def tile_map(c, j):
        def f(b, k, pt, ix):
            bidx = ix[(b * N_KV + k) * LANES + j]
            return (pt[b * n_pages + jnp.minimum(bidx // 2, n_pages - 1)], c, k, 0, 0)
        return f

    tile = lambda c, j: pl.BlockSpec((None, None, None, HD, PAGE), tile_map(c, j))
    in_specs = [tile(0, j) for j in range(N_SEL)] + [tile(1, j) for j in range(N_SEL)]
    wb = win_t.shape[-1]
    per_b = lambda shp: pl.BlockSpec(shp, lambda b, k, pt, ix: (b, 0, 0))
    in_specs += [per_b((1, N_HEADS, HD)), per_b((1, N_HEADS, HD)), per_b((1, SUBLANES, HD)),
                 pl.BlockSpec((None, None, None, HD, wb), lambda b, k, pt, ix: (b, 0, k, 0, 0)),
                 pl.BlockSpec((None, None, None, HD, wb), lambda b, k, pt, ix: (b, 1, k, 0, 0)),
                 per_b((1, SUBLANES, HD)), per_b((1, N_HEADS, LANES))]
    grid_spec = pltpu.PrefetchScalarGridSpec(
        num_scalar_prefetch=2, grid=(nb, N_KV), in_specs=in_specs,
        out_specs=per_b((1, N_HEADS, HD)),
        scratch_shapes=[pltpu.VMEM((HD, N_SEL * PAGE), BF16), pltpu.VMEM((HD, N_SEL * PAGE), BF16)])
    return pl.pallas_call(
        functools.partial(_attn_s2_kernel, n_pages=n_pages, past=past, n_sel_blocks=n_sel_blocks),
        grid_spec=grid_spec,
        out_shape=jax.ShapeDtypeStruct((nb, N_HEADS, HD), F32),
        compiler_params=_params(2), name="attn_sample_sel",
    )(pt_flat, idx_flat, *([sel_t] * (2 * N_SEL)), q3, oc, kvs_rows, win_t, win_t, kvw_rows, gates_hm)


TOK_ROWS = D_MODEL // LANES


def _store_token_tiles(ref, x):
    n = x.shape[0]
    for j in range(TOK_ROWS):
        ref[pl.ds(j, n, stride=TOK_ROWS), :] = x[:, j * LANES:(j + 1) * LANES]


def _load_token_tiles(ref, lead, n):
    return jnp.concatenate([ref[lead + (pl.ds(j, n, stride=TOK_ROWS), slice(None))] for j in range(TOK_ROWS)], axis=1)


def _outp_kernel(xp_ref, convp_ref, attnp_ref, ga1p_ref, sc2p_ref, sh2p_ref,
                 xs_ref, convs_ref, attns_ref, ga1s_ref, sc2s_ref, sh2s_ref,
                 gc_ref, ga_ref, w_ref, g2_ref, wr_ref, x1_ref, hp_ref, lg_ref, *, n_prompt_tiles):
    is_p = pl.program_id(0) < n_prompt_tiles
    pick = lambda a, b: jnp.where(is_p, a, b)
    cn = _rms(pick(convp_ref[...], convs_ref[...]), gc_ref[...])
    an = _rms(pick(attnp_ref[...], attns_ref[...]), ga_ref[...])
    cat = jnp.concatenate([cn, an], axis=1).astype(BF16)
    y = jnp.dot(cat, w_ref[...], preferred_element_type=F32)
    x1 = pick(xp_ref[...], xs_ref[...]) + pick(ga1p_ref[0], ga1s_ref[0]) * y
    x1_ref[...] = x1
    hp = _rms(x1, g2_ref[...]) * (1.0 + pick(sc2p_ref[0], sc2s_ref[0])) + pick(sh2p_ref[0], sh2s_ref[0])
    _store_token_tiles(hp_ref, hp)
    lg_ref[...] = jnp.dot(hp, wr_ref[...], preferred_element_type=F32, precision=lax.Precision.HIGHEST)


TOKEN_TILE = 512


def _outp(prompt, sample, g_conv, g_attn, w_out_b, g2, w_route, *, tpb):
    tm = TOKEN_TILE
    n_p = prompt[0].shape[0] // tm
    total = (n_p + 1) * tm
    last = n_p - 1
    prow = lambda w: pl.BlockSpec((tm, w), lambda i: (jnp.minimum(i, last), 0))
    srow = lambda w: pl.BlockSpec((tm, w), lambda i: (0, 0))
    pmod = pl.BlockSpec((1, 1, D_MODEL), lambda i: (jnp.minimum(i, last) // tpb, 0, 0))
    smod = pl.BlockSpec((1, tm, D_MODEL), lambda i: (0, 0, 0))
    vec = lambda w: pl.BlockSpec((1, w), lambda i: (0, 0))
    row = lambda w: pl.BlockSpec((tm, w), lambda i: (i, 0))
    in_specs = [prow(D_MODEL), prow(CONV_W), prow(ATTN_W), pmod, pmod, pmod,
                srow(D_MODEL), srow(CONV_W), srow(ATTN_W), smod, smod, smod,
                vec(CONV_W), vec(ATTN_W), pl.BlockSpec((D_MODEL, D_MODEL), lambda i: (0, 0)), vec(D_MODEL),
                pl.BlockSpec((D_MODEL, LANES), lambda i: (0, 0))]
    return pl.pallas_call(
        functools.partial(_outp_kernel, n_prompt_tiles=n_p),
        grid=(n_p + 1,), in_specs=in_specs,
        out_specs=[row(D_MODEL), pl.BlockSpec((tm * TOK_ROWS, LANES), lambda i: (i, 0)), row(LANES)],
        out_shape=[jax.ShapeDtypeStruct((total, D_MODEL), F32), jax.ShapeDtypeStruct((total * TOK_ROWS, LANES), F32),
                   jax.ShapeDtypeStruct((total, LANES), F32)],
        compiler_params=_params(1), name="outp",
    )(*prompt, *sample, g_conv.reshape(1, -1), g_attn.reshape(1, -1), w_out_b, g2.reshape(1, -1), w_route)


def _route_kernel(lg_ref, bias_ref, tri_ref, o_ref, cnt_ref, carry, *, tm, n_valid):
    i = pl.program_id(0)

    @pl.when(i == 0)
    def _():
        carry[...] = jnp.zeros_like(carry)

    lane = lax.broadcasted_iota(I32, (tm, LANES), 1)
    rowid = i * tm + lax.broadcasted_iota(I32, (tm, 1), 0)
    live = rowid < n_valid
    lg = lg_ref[...] + bias_ref[...]
    is_g = lane < N_GROUPS
    lgg = jnp.where(is_g, lg, NEG_INF)
    gmax = jnp.max(lgg, axis=-1, keepdims=True)
    grp = jnp.min(jnp.where(is_g & (lgg == gmax), lane, LANES), axis=-1, keepdims=True)
    p_grp = 1.0 / jnp.sum(jnp.where(is_g, jnp.exp(lgg - gmax), 0.0), axis=-1, keepdims=True)
    eid = lane - N_GROUPS
    in_grp = (eid >= grp * EPG) & (eid < (grp + 1) * EPG)
    le = jnp.where(in_grp, lg, NEG_INF)
    v1 = jnp.max(le, axis=-1, keepdims=True)
    e1 = jnp.min(jnp.where(in_grp & (le == v1), eid, LANES), axis=-1, keepdims=True)
    le2 = jnp.where(eid == e1, NEG_INF, le)
    v2 = jnp.max(le2, axis=-1, keepdims=True)
    e2 = jnp.min(jnp.where(in_grp & (eid != e1) & (le2 == v2), eid, LANES), axis=-1, keepdims=True)
    ex2 = jnp.exp(v2 - v1)
    w1 = p_grp * (1.0 / (1.0 + ex2))
    w2 = p_grp * (ex2 / (1.0 + ex2))
    oh1 = ((lane == e1) & live).astype(F32)
    oh2 = ((lane == e2) & live).astype(F32)
    both = oh1 + oh2
    before = jnp.dot(tri_ref[...], both.astype(BF16), preferred_element_type=F32) + carry[0:1, :]
    r1 = jnp.sum(oh1 * before, axis=-1, keepdims=True)
    r2 = jnp.sum(oh2 * before, axis=-1, keepdims=True)
    carry[0:1, :] = carry[0:1, :] + jnp.sum(both, axis=0, keepdims=True)
    out = jnp.where(lane == 0, e1.astype(F32), 0.0)
    out = jnp.where(lane == 1, e2.astype(F32), out)
    out = jnp.where(lane == 2, w1, out)
    out = jnp.where(lane == 3, w2, out)
    out = jnp.where(lane == 4, r1, out)
    out = jnp.where(lane == 5, r2, out)
    o_ref[...] = out
    cnt_ref[...] = carry[...]


def _route(logits, bias_row, n_valid):
    total = logits.shape[0]
    tm = TOKEN_TILE
    n_tiles = total // tm
    tri =(np.arange(tm)[:, None] > np.arange(tm)[None, :]).astype(np.float32)
    return pl.pallas_call(
        functools.partial(_route_kernel, tm=tm, n_valid=n_valid),
        grid=(n_tiles,),
        in_specs=[pl.BlockSpec((tm, LANES), lambda i: (i, 0)), pl.BlockSpec((1, LANES), lambda i: (0, 0)),
                  pl.BlockSpec((tm, tm), lambda i: (0, 0))],
        out_specs=[pl.BlockSpec((tm, LANES), lambda i: (i, 0)), pl.BlockSpec((SUBLANES, LANES), lambda i: (0, 0))],
        out_shape=[jax.ShapeDtypeStruct((total, LANES), F32), jax.ShapeDtypeStruct((SUBLANES, LANES), F32)],
        scratch_shapes=[pltpu.VMEM((SUBLANES, LANES), F32)],
        compiler_params=_params(1), name="route",
    )(logits, bias_row, jnp.asarray(tri, BF16))


EXPERT_ROWS = 256
DISPATCH_TILE = 256


def _tile_copy(src, src_row, dst, dst_row, sem):
    return pltpu.make_async_copy(src.at[pl.ds(pl.multiple_of(src_row * TOK_ROWS, TOK_ROWS), TOK_ROWS), :],
                                 dst.at[pl.ds(pl.multiple_of(dst_row * TOK_ROWS, TOK_ROWS), TOK_ROWS), :], sem)


def _dispatch_kernel(zstart_ref, zcnt_ref, dest_ref, x_ref, xb_hbm, stage, zeros, sem, zsem, *, n_tiles, n_blocks):
    i = pl.program_id(0)
    tm = DISPATCH_TILE
    slot = i % 2
    blk_rows = EXPERT_ROWS * TOK_ROWS

    def tail_copy(b):
        off = pl.multiple_of(b * blk_rows, blk_rows)
        return pltpu.make_async_copy(zeros, xb_hbm.at[pl.ds(off, blk_rows), :], zsem.at[1])

    @pl.when(i == 0)
    def _():
        zeros[...] = jnp.zeros_like(zeros)
        first_tail = zstart_ref[N_EXPERTS] // EXPERT_ROWS
        for e in range(N_EXPERTS):
            def fill(r, c, e=e):
                _tile_copy(zeros, 0, xb_hbm, zstart_ref[e] + r, zsem.at[0]).start()
                return c
            lax.fori_loop(0, zcnt_ref[e], fill, 0)
        lax.fori_loop(first_tail, n_blocks, lambda b, c: (tail_copy(b).start(), c)[1], 0)
        for e in range(N_EXPERTS):
            def drain(r, c):
                _tile_copy(zeros, 0, xb_hbm, 0, zsem.at[0]).wait()
                return c
            lax.fori_loop(0, zcnt_ref[e], drain, 0)
        lax.fori_loop(first_tail, n_blocks, lambda b, c: (tail_copy(b).wait(), c)[1], 0)

    def wait_rows(s):
        for _ in range(2 * tm):
            _tile_copy(stage.at[s], 0, xb_hbm, 0, sem.at[s]).wait()

    @pl.when(i >= 2)
    def _():
        wait_rows(slot)

    stage[slot] = x_ref[...]
    for r in range(tm):
        for k in range(2):
            _tile_copy(stage.at[slot], r, xb_hbm, dest_ref[0, 0, 2 * r + k], sem.at[slot]).start()

    @pl.when(i == n_tiles - 1)
    def _():
        wait_rows(slot)
        if n_tiles > 1:
            wait_rows(1 - slot)


def _dispatch(dest_pad, zstart, zcnt, hp_all, n_rows, n_blocks):
    tm = DISPATCH_TILE
    n_tiles = dest_pad.shape[0] // tm
    grid_spec = pltpu.PrefetchScalarGridSpec(
        num_scalar_prefetch=2, grid=(n_tiles,),
        in_specs=[pl.BlockSpec((1, 1, 2 * tm), lambda i, zs, zc: (i, 0, 0), memory_space=pltpu.SMEM),
                  pl.BlockSpec((tm * TOK_ROWS, LANES), lambda i, zs, zc: (i, 0))],
        out_specs=pl.BlockSpec(memory_space=pl.ANY),
        scratch_shapes=[pltpu.VMEM((2, tm * TOK_ROWS, LANES), F32), pltpu.VMEM((EXPERT_ROWS * TOK_ROWS, LANES), F32),
                        pltpu.SemaphoreType.DMA((2,)), pltpu.SemaphoreType.DMA((2,))])
    return pl.pallas_call(
        functools.partial(_dispatch_kernel, n_tiles=n_tiles, n_blocks=n_blocks),
        grid_spec=grid_spec,
        out_shape=jax.ShapeDtypeStruct((n_rows * TOK_ROWS, LANES), F32),
        compiler_params=_params(1), name="dispatch",
    )(zstart, zcnt, dest_pad.reshape(n_tiles, 1, 2 * tm), hp_all)


def _experts_kernel(blk_e_ref, x_ref, wg_ref, wu_ref, wd_ref, o_ref, wg_b, wu_b, wd_b):
    i = pl.program_id(0)
    changed = jnp.logical_or(i == 0, blk_e_ref[i] != blk_e_ref[jnp.maximum(i - 1, 0)])

    @pl.when(changed)
    def _():
        wg_b[...] = wg_ref[0].astype(BF16)
        wu_b[...] = wu_ref[0].astype(BF16)
        wd_b[...] = wd_ref[0].astype(BF16)

    x = _load_token_tiles(x_ref, (), EXPERT_ROWS).astype(BF16)
    g = jnp.dot(x, wg_b[...], preferred_element_type=F32)
    u = jnp.dot(x, wu_b[...], preferred_element_type=F32)
    h = (g * jax.nn.sigmoid(g)) * u
    _store_token_tiles(o_ref, jnp.dot(h.astype(BF16), wd_b[...], preferred_element_type=F32))


def _experts(blk_e, xb, w_gate, w_up, w_down, n_blocks):
    blk = pl.BlockSpec((EXPERT_ROWS * TOK_ROWS, LANES), lambda i, be: (i, 0))
    grid_spec = pltpu.PrefetchScalarGridSpec(
        num_scalar_prefetch=1, grid=(n_blocks,),
        in_specs=[blk,
                  pl.BlockSpec((1, D_MODEL, D_EXPERT), lambda i, be: (be[i], 0, 0)),
                  pl.BlockSpec((1, D_MODEL, D_EXPERT), lambda i, be: (be[i], 0, 0)),
                  pl.BlockSpec((1, D_EXPERT, D_MODEL), lambda i, be: (be[i], 0, 0))],
        out_specs=blk,
        scratch_shapes=[pltpu.VMEM((D_MODEL, D_EXPERT), BF16), pltpu.VMEM((D_MODEL, D_EXPERT), BF16),
                        pltpu.VMEM((D_EXPERT, D_MODEL), BF16)])
    return pl.pallas_call(
        _experts_kernel,
        grid_spec=grid_spec,
        out_shape=jax.ShapeDtypeStruct((n_blocks * EXPERT_ROWS * TOK_ROWS, LANES), F32),
        compiler_params=_params(1), name="experts",
    )(blk_e, xb, w_gate, w_up, w_down)


def _final_kernel(dest_first_ref, dest_next_ref, yb_hbm, x1_ref, wt_ref, gate2_ref, gf_ref, o_ref, ybuf, sem,
                  *, tm, n_tiles):
    i = pl.program_id(0)
    slot = i % 2

    def issue(dest_ref, s):
        for r in range(tm):
            for k in range(2):
                d = dest_ref[0, 0, 2 * r + k]
                src = yb_hbm.at[pl.ds(pl.multiple_of(d * TOK_ROWS, TOK_ROWS), TOK_ROWS), :]
                pltpu.make_async_copy(src, ybuf.at[s, k, pl.ds(r * TOK_ROWS, TOK_ROWS), :], sem.at[s]).start()

    @pl.when(i == 0)
    def _():
        issue(dest_first_ref, 0)

    @pl.when(i + 1 < n_tiles)
    def _():
        issue(dest_next_ref, 1 - slot)

    for r in range(tm):
        for k in range(2):
            pltpu.make_async_copy(yb_hbm.at[pl.ds(0, TOK_ROWS), :], ybuf.at[slot, k, pl.ds(r * TOK_ROWS, TOK_ROWS), :],
                                  sem.at[slot]).wait()
    wt = wt_ref[...]
    f = wt[:, 2:3] * _load_token_tiles(ybuf, (slot, 0), tm) + wt[:, 3:4] * _load_token_tiles(ybuf, (slot, 1), tm)
    x2 = x1_ref[...] + gate2_ref[0] * f
    o_ref[...] = _rms(x2, gf_ref[...])


def _final(dest_pad, yb, x1_all, route_rows, gate2, final_g, *, rows, tpb, per_row, row0):
    tm = min(256, rows)
    n_tiles = rows // tm
    blk0 = row0 // tm
    dest3 = dest_pad.reshape(-1, 1, 2 * tm)
    idx_blk = lambda f: pl.BlockSpec((1, 1, 2 * tm), f, memory_space=pltpu.SMEM)
    mod = (pl.BlockSpec((1, tm, D_MODEL), lambda i: (0, i, 0)) if per_row
           else pl.BlockSpec((1, 1, D_MODEL), lambda i: (i // tpb, 0, 0)))
    return pl.pallas_call(
        functools.partial(_final_kernel, tm=tm, n_tiles=n_tiles),
        grid=(n_tiles,),
        in_specs=[idx_blk(lambda i: (blk0, 0, 0)),
                  idx_blk(lambda i: (blk0 + jnp.minimum(i + 1, n_tiles - 1), 0, 0)),
                  pl.BlockSpec(memory_space=pl.ANY),
                  pl.BlockSpec((tm, D_MODEL), lambda i: (blk0 + i, 0)),
                  pl.BlockSpec((tm, LANES), lambda i: (blk0 + i, 0)),
                  mod, pl.BlockSpec((1, D_MODEL), lambda i: (0, 0))],
        out_specs=pl.BlockSpec((tm, D_MODEL), lambda i: (i, 0)),
        scratch_shapes=[pltpu.VMEM((2, 2, tm * TOK_ROWS, LANES), F32), pltpu.SemaphoreType.DMA((2,))],
        out_shape=jax.ShapeDtypeStruct((rows, D_MODEL), F32),
        compiler_params=_params(1), name="final_sample" if per_row else "final_prompt",
    )(dest3, dest3, yb, x1_all, route_rows, gate2, final_g.reshape(1, -1))


def _rope_tables(pos):
    inv = ROPE_THETA ** (-jnp.arange(HALF, dtype=F32) / HALF)
    ang = pos.astype(F32)[:, None] * inv[None, :]
    cos = jnp.tile(jnp.cos(ang), (1, LANES // HALF))
    sin = jnp.sin(ang)
    sin_s = jnp.tile(jnp.concatenate([-sin, sin], axis=1), (1, LANES // HD))
    return cos, sin_s


def _pack_w_in(w_in):
    gl = w_in[:, _C_G:_C_G + 3 * N_HEADS].reshape(D_MODEL, 3, N_KV, QPK)
    gcols = []
    for k in range(N_KV):
        gk = gl[:, :, k, :].reshape(D_MODEL, 3 * QPK)
        gcols.append(jnp.pad(gk, ((0, 0), (0, LANES - 3 * QPK))))
    return jnp.concatenate([w_in[:, :_C_G]] + gcols, axis=1).astype(BF16)


def _pack_cmp_weights(cmp_w1, cmp_w2, bias, cmp_b2):
    w1 = cmp_w1.reshape(2, 2, CMP_STRIDE, HD, CMP_HID)
    eye = jnp.eye(N_KV, dtype=F32)
    w1p = w1.transpose(0, 2, 3, 1, 4).reshape(2, CMP_STRIDE * HD, 2 * CMP_HID)
    w2p =jnp.einsum('chd,pk->cphkd', cmp_w2, eye).reshape(2, N_KV * CMP_HID, KV_W)
    b1p = jnp.tile(bias, (1, N_KV)).reshape(2, 1, N_KV * CMP_HID)
    b2p = jnp.tile(cmp_b2, (1, N_KV)).reshape(2, 1, KV_W)
    return w1p.astype(BF16), b1p, w2p.astype(BF16), b2p


def _band(n_cmp_pad, n_cmp, n_blk_pad, n_blk):
    n = np.arange(n_cmp_pad)[:, None]
    b = np.arange(n_blk_pad)[None, :]
    r = SEL_BLOCK // CMP_STRIDE
    m = (n >= r * b - 1) & (n <= r * b + r - 1) & (n < n_cmp) & (b < n_blk)
    return jnp.asarray(m.astype(np.float32))


def _expand(t, kc):
    n_chunks = t // kc
    key = np.arange(t).reshape(n_chunks, 1, kc)
    blk = np.arange(t // SEL_BLOCK).reshape(1, -1, 1)
    return jnp.asarray((key // SEL_BLOCK == blk).astype(np.float32), BF16)


def kernel(x_prompt, x_sample, c_prompt, c_sample, cache_cmp_kv, cache_sel_kv, cache_win_kv, state_conv, page_table,
           ln1_g, ln2_g, w_ada, b_ada, w_in, w_conv, cmp_pos, cmp_w1, cmp_b1, cmp_w2, cmp_b2, g_out_conv, g_out_attn,
           w_out, w_route_group, b_route_group, w_route_expert, b_route_expert, w_gate, w_up, w_down, final_g):
    depth = w_in.shape[0]
    assert depth == 1, "single-layer step"
    nb, t, _ = x_prompt.shape
    ns, ts, _ = x_sample.shape
    assert ts == 1 and t % 512 == 0 and t >= WINDOW + Q_BLOCK
    n_pool = cache_cmp_kv.shape[1]
    n_pages = page_table.shape[1]
    past = n_pages * PAGE
    wb = cache_win_kv.shape[2]
    assert wb == WINDOW
    l = 0

    n_c = nb + ns
    c_all = jnp.pad(jnp.concatenate([c_prompt, c_sample], axis=0), ((0, (-n_c) % SUBLANES), (0, 0)))
    mods = _ada(c_all, w_ada[l], b_ada[l])
    sh1, sc1, ga1, sh2, sc2, ga2 = [mods[:, j * D_MODEL:(j + 1) * D_MODEL] for j in range(6)]
    pr = lambda a: a[0:nb].reshape(nb, 1, D_MODEL)
    sr = lambda a: a[nb:nb + ns].reshape(1, ns, D_MODEL)

    w_pack = _pack_w_in(w_in[l])
    wconv8 = jnp.pad(w_conv[l], ((0, SUBLANES - CONV_K), (0, 0)))
    cos_p, sin_p = _rope_tables(jnp.arange(t, dtype=I32))
    cos_s, sin_s = _rope_tables(jnp.full((1,), past, I32))
    xp2 = x_prompt.reshape(nb * t, D_MODEL)
    xs2 = x_sample.reshape(ns, D_MODEL)
    (conv_p, cst_p, q_p, kvc_p, kvc_rows_p, kvs_rows_p, kvw_rows_p, ks_p, vs_p, kw_p, vw_p, gates_p) = _proj(
        xp2, ln1_g[l], pr(sc1), pr(sh1), w_pack, wconv8, cos_p, sin_p, nb=nb, t=t, sample=False)
    (conv_s, cst_s, q_s, _, kvc_rows_s, kvs_rows_s, kvw_rows_s, _, _, _, _, gates_s) = _proj(
        xs2, ln1_g[l], sr(sc1), sr(sh1), w_pack, wconv8, cos_s, sin_s, nb=ns, t=1, sample=True,
        prev=(state_conv[l][:, 0], state_conv[l][:, 1]))

    bias = _cmpbias(cmp_pos[l], cmp_w1[l], cmp_b1[l])
    w1p, b1p, w2p, b2p = _pack_cmp_weights(cmp_w1[l], cmp_w2[l], bias, cmp_b2[l])
    pp = t // PAGE
    cos_cp, sin_cp = _rope_tables((jnp.arange(t // CMP_STRIDE, dtype=I32) + 2) * CMP_STRIDE - 1)
    ck_p, cv_p = _cmp(kvc_p.reshape(nb * pp, SUBLANES, CHUNK_ROW), jnp.arange(nb * pp, dtype=I32), nb, pp,
                      w1p, b1p, w2p, b2p, cos_cp, sin_cp, "cmp_prompt", tiles=False)
    pt_flat = page_table.reshape(-1).astype(I32)
    cos_cs, sin_cs = _rope_tables((jnp.arange(past // CMP_STRIDE, dtype=I32) + 2) * CMP_STRIDE - 1)
    to_tiles = lambda a: a.transpose(0, 2, 3, 4, 1)
    ck_s, cv_s = _cmp(to_tiles(cache_cmp_kv[l]), pt_flat, ns, n_pages,
                      w1p, b1p, w2p, b2p, cos_cs, sin_cs, "cmp_sample", tiles=True)

    n_chunk_p = t // CMP_STRIDE
    n_blk_p = t // SEL_BLOCK
    band_p = _band(n_chunk_p, n_chunk_p - 1, n_blk_p, n_blk_p)
    attn_p = _attn_prompt(q_p, ck_p, cv_p, ks_p, vs_p, kw_p, vw_p, gates_p, band_p.T, _expand(t, ATTN_KEY_CHUNK), nb=nb, t=t)

    n_chunk_s = past // CMP_STRIDE
    n_sel_s = -(-(past + 1) // SEL_BLOCK)
    nbp = -(-n_sel_s // LANES) * LANES
    band_s = _band(n_chunk_s, (past + 1) // CMP_STRIDE - 1, nbp, n_sel_s)
    q3 = q_s.reshape(N_HEADS, ns, HD).transpose(1, 0, 2).astype(F32)
    gs = gates_s.reshape(ns, N_KV, LANES)[:, :, :3 * QPK].reshape(ns, N_KV, 3, QPK)
    gates_hm = jnp.pad(gs.transpose(0, 1, 3, 2).reshape(ns, N_HEADS, 3), ((0, 0), (0, 0), (0, LANES - 3)))
    rpt = 2 * N_KV
    new_rows = lambda a: jnp.pad(a.reshape(ns, rpt, HD), ((0, 0), (0, SUBLANES - rpt), (0, 0)))
    attn_s = _attn_sample(q3, ck_s, cv_s, band_s, to_tiles(cache_sel_kv[l]), pt_flat,
                          new_rows(kvs_rows_s), to_tiles(cache_win_kv[l]), new_rows(kvw_rows_s),
                          gates_hm, nb=ns, n_pages=n_pages, past=past, n_sel_blocks=n_sel_s).reshape(ns, ATTN_W)

    total = nb * t + ns
    w_out_b = w_out[l].astype(BF16)
    w_route = jnp.pad(jnp.concatenate([w_route_group[l], w_route_expert[l]], axis=1),
                      ((0, 0), (0, LANES - N_GROUPS - N_EXPERTS)))
    b_route = jnp.pad(jnp.concatenate([b_route_group[l], b_route_expert[l]]), (0, LANES - N_GROUPS - N_EXPERTS))
    tile_pad = lambda a: jnp.pad(a, ((0, TOKEN_TILE - ns), (0, 0)))
    smod = lambda a: tile_pad(a[nb:nb + ns]).reshape(1, TOKEN_TILE, D_MODEL)
    x1_all, hp_all, lg_all = _outp(
        (xp2, conv_p, attn_p.reshape(nb * t, ATTN_W), pr(ga1), pr(sc2), pr(sh2)),
        (tile_pad(xs2), tile_pad(conv_s), tile_pad(attn_s), smod(ga1), smod(sc2), smod(sh2)),
        g_out_conv[l], g_out_attn[l], w_out_b, ln2_g[l], w_route, tpb=t // TOKEN_TILE)

    route, counts = _route(lg_all, b_route.reshape(1, LANES), total)
    e = route[:total, 0:2].astype(I32)
    rank = route[:total, 4:6].astype(I32)
    cnt = counts[0, :N_EXPERTS].astype(I32)
    padded = (cnt + EXPERT_ROWS - 1) // EXPERT_ROWS * EXPERT_ROWS
    pad_end = jnp.cumsum(padded)
    pad_start = pad_end - padded
    m_slots = total * 2
    n_blocks = -(-(m_slots + N_EXPERTS * (EXPERT_ROWS - 1)) // EXPERT_ROWS)
    n_slots = n_blocks * EXPERT_ROWS
    dest = jnp.clip(pad_start[e] + rank, 0, n_slots - 1)
    blk_start = jnp.arange(n_blocks, dtype=I32) * EXPERT_ROWS
    blk_e = jnp.minimum(jnp.sum((pad_end[None, :] <= blk_start[:, None]).astype(I32), axis=1), N_EXPERTS - 1)
    n_dump = 2 * (x1_all.shape[0] - total)
    dest_pad = jnp.concatenate([dest, n_slots + jnp.arange(n_dump, dtype=I32).reshape(-1, 2)], axis=0)

    zstart = jnp.concatenate([pad_start + cnt, pad_end[-1:]])
    zcnt = jnp.concatenate([padded - cnt, jnp.zeros((1,), I32)])
    xb = _dispatch(dest_pad, zstart, zcnt, hp_all, n_slots + n_dump, n_blocks)
    yb = _experts(blk_e, xb, w_gate[l], w_up[l], w_down[l], n_blocks)
    y_p = _final(dest_pad, yb, x1_all, route, pr(ga2), final_g, rows=nb * t, tpb=t // 256, per_row=False, row0=0)
    y_s = _final(dest_pad, yb, x1_all, route, sr(ga2), final_g, rows=ns, tpb=1, per_row=True, row0=nb * t)

    kv_shape = (2, N_KV, HD)
    y_prompt = y_p.reshape(nb, t, D_MODEL)
    y_sample = y_s.reshape(ns, 1, D_MODEL)
    new_cmp_prompt = kvc_rows_p.reshape((1, nb, t) + kv_shape)
    new_cmp_sample = kvc_rows_s.reshape((1, ns, 1) + kv_shape)
    new_sel_prompt = kvs_rows_p.reshape((1, nb, t) + kv_shape)
    new_sel_sample = kvs_rows_s.reshape((1, ns, 1) + kv_shape)
    new_win_prompt = kvw_rows_p.reshape((nb, t) + kv_shape)[:, t - WINDOW:][None]
    new_win_sample = jnp.concatenate([cache_win_kv[l][:, 1:], kvw_rows_s.reshape((ns, 1) + kv_shape)], axis=1)[None]
    new_conv_prompt = cst_p[:, SUBLANES - (CONV_K - 1):][None]
    new_conv_sample = jnp.stack([state_conv[l][:, 1], cst_s], axis=1)[None]
    return (y_prompt, y_sample, new_cmp_prompt, new_cmp_sample, new_sel_prompt, new_sel_sample,
            new_win_prompt, new_win_sample, new_conv_prompt, new_conv_sample)
```

```python
import functools

import numpy as np
import jax
import jax.numpy as jnp
from jax import lax
from jax.experimental import pallas as pl
from jax.experimental.pallas import tpu as pltpu

F32 = jnp.float32
BF16 = jnp.bfloat16
I32 = jnp.int32

D_MODEL = 1024
CONV_W = 512
ATTN_W = 512
HD = 64
HALF = HD // 2
N_HEADS = 8
N_KV = 2
QPK = 4
KV_W = N_KV * HD
CONV_K = 3
PAGE = 128
CMP_STRIDE = 16
CMP_HID = 128
SEL_BLOCK = 64
N_SEL = 16
WINDOW = 512
Q_BLOCK = 128
ROPE_THETA = 10000.0
N_GROUPS = 4
EPG = 8
N_EXPERTS = 32
D_EXPERT = 512
NORM_EPS = 1e-6
NEG_INF = -1e30
FORCE_SCORE = 1e4
LANES = 128
SUBLANES = 8
CHUNK_ROW = CMP_STRIDE * 2 * KV_W
VMEM_LIMIT = 56 * 1024 * 1024

_NT = (((1,), (1,)), ((), ()))
Q_SCALE = HD ** -0.5 * 1.4426950408889634


def _params(n_axes):
    return pltpu.CompilerParams(dimension_semantics=("arbitrary",) * n_axes,
                                vmem_limit_bytes=VMEM_LIMIT)


def _rms(x, g):
    return x * lax.rsqrt(jnp.mean(x * x, axis=-1, keepdims=True) + NORM_EPS) * g


def _rope128(x, cos, sin_signed, first_half):
    xr = jnp.where(first_half, pltpu.roll(x, LANES - HALF, 1), pltpu.roll(x, HALF, 1))
    return x * cos + xr * sin_signed


def _first_half_mask(rows):
    lane = lax.broadcasted_iota(I32, (rows, LANES), 1)
    return (lane % HD) < HALF


def _ada_kernel(c_ref, w_ref, b_ref, o_ref):
    c = c_ref[...]
    s = c * jax.nn.sigmoid(c)
    o_ref[...] = jnp.dot(s.astype(BF16), w_ref[...].astype(BF16), preferred_element_type=F32) + b_ref[...]


def _ada(c_all, w_ada, b_ada):
    m, d = c_all.shape
    n = w_ada.shape[1]
    tn = 1024
    return pl.pallas_call(
        _ada_kernel,
        grid=(n // tn,),
        in_specs=[pl.BlockSpec((m, d), lambda j: (0, 0)),
                  pl.BlockSpec((d, tn), lambda j: (0, j)),
                  pl.BlockSpec((1, tn), lambda j: (0, j))],
        out_specs=pl.BlockSpec((m, tn), lambda j: (0, j)),
        out_shape=jax.ShapeDtypeStruct((m, n), F32),
        compiler_params=_params(1),
        name="ada",
    )(c_all, w_ada, b_ada.reshape(1, n))


_C_B, _C_C, _C_U, _C_Q, _C_KVC, _C_KVS, _C_KVW, _C_G, _C_END = 0, 512, 1024, 1536, 2048, 2304, 2560, 2816, 3072


def _proj_kernel(*refs, tm, tpb, sample):
    if sample:
        (x_ref, g1_ref, sc_ref, sh_ref, w_ref, wc_ref, cos_ref, sin_ref, p0_ref, p1_ref,
         conv_ref, cst_ref, q_ref, kvc_ref, kvc_il_ref, kvs_ref, kvw_ref, ks_ref, vs_ref, kw_ref, vw_ref, gate_ref,
         ilbuf) = refs
        vbuf = None
    else:
        (x_ref, g1_ref, sc_ref, sh_ref, w_ref, wc_ref, cos_ref, sin_ref,
         conv_ref, cst_ref, q_ref, kvc_ref, kvc_il_ref, kvs_ref, kvw_ref, ks_ref, vs_ref, kw_ref, vw_ref, gate_ref,
         ilbuf, vbuf) = refs
    i = pl.program_id(0)
    x = x_ref[...]
    h = _rms(x, g1_ref[...]) * (1.0 + sc_ref[0]) + sh_ref[0]
    hb = h.astype(BF16)

    zc = jnp.dot(hb, w_ref[:, _C_B:_C_Q], preferred_element_type=F32)
    b_g = zc[:, 0:CONV_W]
    v = zc[:, CONV_W:2 * CONV_W] * zc[:, 2 * CONV_W:3 * CONV_W]
    wc = wc_ref[...]
    if sample:
        y = wc[0:1] * p0_ref[...] + wc[1:2] * p1_ref[...] + wc[2:3] * v
        cst_ref[...] = v
    else:
        @pl.when(i % tpb == 0)
        def _():
            vbuf[0:SUBLANES, :] = jnp.zeros((SUBLANES, CONV_W), F32)
        vbuf[SUBLANES:SUBLANES + tm, :] = v
        y = wc[0:1] * vbuf[pl.ds(SUBLANES - 2, tm), :] + wc[1:2] * vbuf[pl.ds(SUBLANES - 1, tm), :] + wc[2:3] * v
        tail = vbuf[tm:tm + SUBLANES, :]
        cst_ref[0] = tail
        vbuf[0:SUBLANES, :] = tail
    conv_ref[...] = b_g * y

    cos = cos_ref[...]
    sin_s = sin_ref[...]
    first = _first_half_mask(tm)

    zq = jnp.dot(hb, w_ref[:, _C_Q:_C_KVC], preferred_element_type=F32)
    for gq in range(ATTN_W // LANES):
        qr = _rope128(zq[:, gq * LANES:(gq + 1) * LANES], cos, sin_s, first) * Q_SCALE
        q_ref[0, 2 * gq] = qr[:, 0:HD].astype(BF16)
        q_ref[0, 2 * gq + 1] = qr[:, HD:LANES].astype(BF16)

    def store_rows(out_ref, halves):
        for j in range(2 * N_KV):
            piece = halves[j // N_KV]
            if j % N_KV == 1:
                piece = pltpu.roll(piece, HD, 1)
            ilbuf[pl.ds(j, tm, stride=2 * N_KV), :] = piece
        out_ref[...] = ilbuf[:, 0:HD]

    zkv = jnp.dot(hb, w_ref[:, _C_KVC:_C_G], preferred_element_type=F32)
    kvc_ref[...] = zkv[:, 0:2 * KV_W]
    store_rows(kvc_il_ref, (zkv[:, 0:KV_W], zkv[:, KV_W:2 * KV_W]))
    for base, kv_ref, kh_ref, vh_ref in ((2 * KV_W, kvs_ref, ks_ref, vs_ref), (4 * KV_W, kvw_ref, kw_ref, vw_ref)):
        kr = _rope128(zkv[:, base:base + KV_W], cos, sin_s, first)
        vv = zkv[:, base + KV_W:base + 2 * KV_W]
        store_rows(kv_ref, (kr, vv))
        lane = lax.broadcasted_iota(I32, (tm, LANES), 1)
        for k in range(N_KV):
            kh_ref[0, k] = kr[:, k * HD:(k + 1) * HD].astype(BF16)
            vk = vv if k == 0 else pltpu.roll(vv, HD, 1)
            vh_ref[0, k] = jnp.where(lane < HD, vk, jnp.where(lane == HD, 1.0, 0.0)).astype(BF16)

    zg = jnp.dot(hb, w_ref[:, _C_G:_C_END], preferred_element_type=F32)
    gate_ref[...] = jax.nn.sigmoid(zg)


def _proj(x2d, g1, sc, sh, w_pack, w_conv, cos_t, sin_t, *, nb, t, sample, prev=None):
    rows = nb * t
    tm = min(512, rows) if not sample else rows
    tpb = (t // tm) if not sample else 1
    n_tiles = rows // tm
    f = lambda a: jax.ShapeDtypeStruct(a, F32)
    b = lambda a: jax.ShapeDtypeStruct(a, BF16)
    if sample:
        mod_spec = pl.BlockSpec((1, tm, D_MODEL), lambda i: (0, 0, 0))
        tab_spec = pl.BlockSpec((1, LANES), lambda i: (0, 0))
        cst_shape, cst_spec = f((rows, CONV_W)), pl.BlockSpec((tm, CONV_W), lambda i: (0, 0))
        hm = lambda i: (0, 0, i, 0)
        hb_, ht_ = 1, rows
    else:
        mod_spec = pl.BlockSpec((1, 1, D_MODEL), lambda i: (i // tpb, 0, 0))
        tab_spec = pl.BlockSpec((tm, LANES), lambda i: (i % tpb, 0))
        cst_shape, cst_spec = f((nb, SUBLANES, CONV_W)), pl.BlockSpec((1, SUBLANES, CONV_W), lambda i: (i // tpb, 0, 0))
        hm = lambda i: (i // tpb, 0, i % tpb, 0)
        hb_, ht_ = nb, t
    row = lambda w: pl.BlockSpec((tm, w), lambda i: (i, 0))
    in_specs = [row(D_MODEL), pl.BlockSpec((1, D_MODEL), lambda i: (0, 0)), mod_spec, mod_spec,
                pl.BlockSpec((D_MODEL, _C_END), lambda i: (0, 0)),
                pl.BlockSpec((SUBLANES, CONV_W), lambda i: (0, 0)), tab_spec, tab_spec]
    args = [x2d, g1.reshape(1, D_MODEL), sc, sh, w_pack, w_conv, cos_t, sin_t]
    scratch = [pltpu.VMEM((2 * N_KV * tm, LANES), F32)]
    if sample:
        in_specs += [row(CONV_W), row(CONV_W)]
        args += [prev[0], prev[1]]
    else:
        scratch.append(pltpu.VMEM((tm + SUBLANES, CONV_W), F32))
    il_rows = 2 * N_KV * rows
    il = pl.BlockSpec((2 * N_KV * tm, HD), lambda i: (i, 0))
    out_shape = [f((rows, CONV_W)), cst_shape, b((hb_, N_HEADS, ht_, HD)),
                 f((rows, 2 * KV_W)), f((il_rows, HD)), f((il_rows, HD)), f((il_rows, HD)),
                 b((hb_, N_KV, ht_, HD)), b((hb_, N_KV, ht_, LANES)), b((hb_, N_KV, ht_, HD)), b((hb_, N_KV, ht_, LANES)),
                 f((rows, 2 * LANES))]
    out_specs = [row(CONV_W), cst_spec, pl.BlockSpec((1, N_HEADS, tm, HD), hm),
                 row(2 * KV_W), il, il, il,
                 pl.BlockSpec((1, N_KV, tm, HD), hm), pl.BlockSpec((1, N_KV, tm, LANES), hm),
                 pl.BlockSpec((1, N_KV, tm, HD), hm), pl.BlockSpec((1, N_KV, tm, LANES), hm),
                 row(2 * LANES)]
    return pl.pallas_call(
        functools.partial(_proj_kernel, tm=tm, tpb=tpb, sample=sample),
        grid=(n_tiles,), in_specs=in_specs, out_specs=out_specs, out_shape=out_shape,
        scratch_shapes=scratch, compiler_params=_params(1),
        name="proj_sample" if sample else "proj_prompt",
    )(*args)


def _cmpbias_kernel(pos_ref, w_ref, b1_ref, o_ref):
    for c in range(2):
        o_ref[c:c + 1, :] = jnp.sum(pos_ref[c] * w_ref[c], axis=0, keepdims=True) + b1_ref[c:c + 1, :]


def _cmpbias(cmp_pos, cmp_w1, cmp_b1):
    n = cmp_pos.shape[1] * cmp_pos.shape[2]
    return pl.pallas_call(
        _cmpbias_kernel,
        out_shape=jax.ShapeDtypeStruct((2, CMP_HID), F32),
        compiler_params=pltpu.CompilerParams(vmem_limit_bytes=VMEM_LIMIT),
        name="cmpbias",
    )(cmp_pos.reshape(2, n, 1), cmp_w1.reshape(2, n, CMP_HID), cmp_b1)


def _cmp_kernel(pt_ref, *refs, ppt, nsub, tiles):
    n_in = nsub * ppt + 1
    all_pages = refs[:n_in]
    if tiles:
        unfold_ref = refs[n_in]
        refs = refs[1:]
    w1_ref, b1_ref, w2_ref, b2_ref, cos_ref, sin_ref, ck_ref, cv_ref, lhs_all, pbuf = refs[n_in:]
    r = ppt * SUBLANES
    for u in range(nsub):
        _cmp_unfold(all_pages[u * ppt:u * ppt + ppt + 1], unfold_ref if tiles else None, lhs_all.at[u], ppt, tiles)
    for u in range(nsub):
        rows = slice(u * r, (u + 1) * r)
        _cmp_mlp(lhs_all.at[u], pbuf, w1_ref, b1_ref, w2_ref, b2_ref, cos_ref[rows, :], sin_ref[rows, :],
                 ck_ref, cv_ref, rows, r)


def _cmp_unfold(pages, unfold_ref, lhs, ppt, tiles):
    r = ppt * SUBLANES
    rk = r + SUBLANES
    low = lax.broadcasted_iota(I32, (SUBLANES, LANES), 1) < HD

    def tap_tile(j, c, s, y):
        if tiles:
            return y[s * SUBLANES:(s + 1) * SUBLANES, c * KV_W:(c + 1) * KV_W]
        return pages[j][0, :, s * 2 * KV_W + c * KV_W:s * 2 * KV_W + (c + 1) * KV_W]

    for j in range(ppt + 1):
        y = None
        if tiles:
            a = pages[j][...].reshape(2 * KV_W, PAGE).astype(BF16)
            y = lax.dot_general(unfold_ref[...], a, _NT, preferred_element_type=F32)
        for c in range(2):
            for sp in range(CMP_STRIDE // 2):
                t0 = tap_tile(j, c, 2 * sp, y)
                t1 = tap_tile(j, c, 2 * sp + 1, y)
                lhs[c, j * SUBLANES:(j + 1) * SUBLANES, sp * LANES:(sp + 1) * LANES] = (
                    jnp.where(low, t0, pltpu.roll(t1, HD, 1)))
                lhs[c, rk + j * SUBLANES:rk + (j + 1) * SUBLANES, sp * LANES:(sp + 1) * LANES] = (
                    jnp.where(low, pltpu.roll(t0, HD, 1), t1))


def _cmp_mlp(lhs, pbuf, w1_ref, b1_ref, w2_ref, b2_ref, cos, sin_s, ck_ref, cv_ref, rows, r):
    rk = r + SUBLANES
    first = _first_half_mask(r)
    for c in range(2):
        p = jnp.dot(lhs[c].astype(BF16), w1_ref[c], preferred_element_type=F32)
        hids = []
        for k in range(N_KV):
            pbuf[...] = p[k * rk:(k + 1) * rk, CMP_HID:2 * CMP_HID]
            hids.append(p[k * rk:k * rk + r, 0:CMP_HID] + pbuf[pl.ds(1, r), :])
        hid = jnp.concatenate(hids, axis=1) + b1_ref[c]
        act = jax.nn.gelu(hid)
        comp = jnp.dot(act.astype(BF16), w2_ref[c], preferred_element_type=F32) + b2_ref[c]
        if c == 0:
            comp = _rope128(comp, cos, sin_s, first)
            out = ck_ref
        else:
            out = cv_ref
        for k in range(N_KV):
            out[0, k, rows, :] = comp[:, k * HD:(k + 1) * HD]


def _cmp(pages, pt_flat, nb, n_pages, w1p, b1p, w2p, b2p, cos_c, sin_c, name, tiles):
    ppt = min(32, n_pages)
    nsub = 2 if n_pages % (2 * ppt) == 0 else 1
    pps = nsub * ppt
    n_tiles = n_pages // pps
    r = pps * SUBLANES
    n_chunk = n_pages * SUBLANES
    zeros = (0,) * (pages.ndim - 1)

    def page_map(j):
        return lambda b, t, pt: (pt[b * n_pages + t * pps + j],) + zeros

    def next_map(b, t, pt):
        return (pt[b * n_pages + jnp.minimum(t * pps + pps, n_pages - 1)],) + zeros

    page_blk = (None, 2, N_KV, HD, PAGE) if tiles else (1, SUBLANES, CHUNK_ROW)
    in_specs = [pl.BlockSpec(page_blk, page_map(j)) for j in range(pps)]
    in_specs.append(pl.BlockSpec(page_blk, next_map))
    const = lambda shp: pl.BlockSpec(shp, lambda b, t, pt: (0,) * len(shp))
    extra = []
    if tiles:
        row = np.arange(PAGE)
        tok = (row % SUBLANES) * CMP_STRIDE + row // SUBLANES
        extra = [jnp.asarray(tok[:, None] == np.arange(PAGE)[None, :], BF16)]
        in_specs.append(const((PAGE, PAGE)))
    in_specs += [const(w1p.shape), const(b1p.shape), const(w2p.shape), const(b2p.shape),
                 pl.BlockSpec((r, LANES), lambda b, t, pt: (t, 0)), pl.BlockSpec((r, LANES), lambda b, t, pt: (t, 0))]
    hm = pl.BlockSpec((1, N_KV, r, HD), lambda b, t, pt: (b, 0, t, 0))
    grid_spec = pltpu.PrefetchScalarGridSpec(
        num_scalar_prefetch=1, grid=(nb, n_tiles), in_specs=in_specs, out_specs=[hm, hm],
        scratch_shapes=[pltpu.VMEM((nsub, 2, N_KV * (ppt + 1) * SUBLANES, CMP_STRIDE * HD), F32),
                        pltpu.VMEM(((ppt + 1) * SUBLANES, CMP_HID), F32)])
    return pl.pallas_call(
        functools.partial(_cmp_kernel, ppt=ppt, nsub=nsub, tiles=tiles),
        grid_spec=grid_spec,
        out_shape=[jax.ShapeDtypeStruct((nb, N_KV, n_chunk, HD), F32)] * 2,
        compiler_params=_params(2), name=name,
    )(pt_flat, *([pages] * (pps + 1)), *extra, w1p, b1p, w2p, b2p, cos_c, sin_c)


def _softmax_rows(s, valid):
    s = jnp.where(valid, s, NEG_INF)
    m = jnp.max(s, axis=-1, keepdims=True)
    e = jnp.exp2(s - m)
    return e / jnp.sum(e, axis=-1, keepdims=True)


def _attn_p_kernel(q_ref, ck_ref, cv_ref, ks_ref, vs_ref, kw_ref, vw_ref, gate_ref, band_ref, exp_ref, o_ref,
                   *, n_cmp_pad, n_blk, kc, hg, wc):
    qb = pl.program_id(2)
    start = qb * Q_BLOCK
    tpos = start + lax.broadcasted_iota(I32, (Q_BLOCK, 1), 0)
    groups = range(QPK // hg)
    rows = hg * Q_BLOCK

    def q_of(g):
        return q_ref[0, g * hg:(g + 1) * hg].reshape(rows, HD)

    def biased(s, bias):
        width = s.shape[-1]
        return (s.reshape(hg, Q_BLOCK, width) + bias[None]).reshape(rows, width)

    ck = ck_ref[0, 0].astype(BF16)
    cv = cv_ref[0, 0].astype(BF16)
    cmp_end = (lax.broadcasted_iota(I32, (1, n_cmp_pad), 1) + 2) * CMP_STRIDE - 1
    bias_c = jnp.where(cmp_end <= tpos, 0.0, NEG_INF)
    o_c = []
    pcs = jnp.zeros((Q_BLOCK, n_cmp_pad), F32)
    for g in groups:
        s_c = biased(lax.dot_general(q_of(g), ck, _NT, preferred_element_type=F32), bias_c)
        m_c = jnp.maximum(jnp.max(s_c, axis=-1, keepdims=True), 0.5 * NEG_INF)
        e_c = jnp.exp2(s_c - m_c)
        l_c = jnp.sum(e_c, axis=-1, keepdims=True)
        p_c = e_c * (1.0 / jnp.where(l_c > 0.0, l_c, 1.0))
        o_c.append(jnp.dot(p_c.astype(BF16), cv, preferred_element_type=F32))
        for h in range(hg):
            pcs = pcs + p_c[h * Q_BLOCK:(h + 1) * Q_BLOCK]

    imp =lax.dot_general(band_ref[...], pcs, _NT, preferred_element_type=F32,
                          precision=lax.Precision.HIGHEST)
    blk = lax.broadcasted_iota(I32, (n_blk, Q_BLOCK), 0)
    tlane = start + lax.broadcasted_iota(I32, (1, Q_BLOCK), 1)
    cur = tlane // SEL_BLOCK
    causal = blk * SEL_BLOCK <= tlane
    forced = causal & ((blk == 0) | (blk == cur) | (blk == cur - 1))
    score = jnp.where(forced, FORCE_SCORE, jnp.where(causal, imp, -1.0))
    rank = jnp.zeros((n_blk, Q_BLOCK), F32)
    for bp in range(n_blk):
        other = score[bp:bp + 1, :]
        beats = (other > score) | ((other == score) & (bp < blk))
        rank = rank + beats.astype(F32)
    sel_t = (rank < float(min(N_SEL, n_blk))).astype(BF16)
    eye = (lax.broadcasted_iota(I32, (Q_BLOCK, Q_BLOCK), 0)
           == lax.broadcasted_iota(I32, (Q_BLOCK, Q_BLOCK), 1)).astype(BF16)
    sel = lax.dot_general(eye, sel_t, _NT, preferred_element_type=F32).astype(BF16)

    n_chunks = (start + Q_BLOCK + kc - 1) // kc

    def online(state, kj, vj, bias):
        out = []
        for g in groups:
            m_i, acc = state[g]
            s = biased(lax.dot_general(q_of(g), kj, _NT, preferred_element_type=F32), bias)
            m_new = jnp.maximum(m_i, jnp.max(s, axis=-1, keepdims=True))
            p = jnp.exp2(s - m_new).astype(BF16)
            out.append((m_new, jnp.exp2(m_i - m_new) * acc + jnp.dot(p, vj, preferred_element_type=F32)))
        return tuple(out)

    def step(j, state, causal_chunk):
        off = pl.multiple_of(j * kc, kc)
        mexp = jnp.dot(sel, exp_ref[j], preferred_element_type=F32)
        bias = mexp * (-NEG_INF) + NEG_INF
        if causal_chunk:
            keypos = off + lax.broadcasted_iota(I32, (1, kc), 1)
            bias = jnp.where(keypos <= tpos, bias, NEG_INF)
        return online(state, ks_ref[0, 0, pl.ds(off, kc), :], vs_ref[0, 0, pl.ds(off, kc), :], bias)

    init = tuple((jnp.full((rows, 1), NEG_INF, F32), jnp.zeros((rows, LANES), F32)) for _ in groups)
    state = lax.fori_loop(0, n_chunks - 1, lambda j, c: step(j, c, False), init)
    sel_state = step(n_chunks - 1, state, True)

    s0 = jnp.maximum(start - WINDOW, 0)
    win_state = init
    for c in range((WINDOW + Q_BLOCK) // wc):
        off = pl.multiple_of(s0 + c * wc, Q_BLOCK)
        dist = tpos - (off + lax.broadcasted_iota(I32, (1, wc), 1))
        bias_w = jnp.where((dist >= 0) & (dist <= WINDOW), 0.0, NEG_INF)
        win_state = online(win_state, kw_ref[0, 0, pl.ds(off, wc), :], vw_ref[0, 0, pl.ds(off, wc), :], bias_w)

    gt = gate_ref[...]
    for g in groups:
        acc_s = sel_state[g][1]
        acc_w = win_state[g][1]
        o_s = acc_s[:, 0:HD] * (1.0 / acc_s[:, HD:HD + 1])
        o_w = acc_w[:, 0:HD] * (1.0 / acc_w[:, HD:HD + 1])
        for hh in range(hg):
            h = g * hg + hh
            rs = slice(hh * Q_BLOCK, (hh + 1) * Q_BLOCK)
            o = (gt[:, h:h + 1] * o_c[g][rs] + gt[:, QPK + h:QPK + h + 1] * o_s[rs]
                 + gt[:, 2 * QPK + h:2 * QPK + h + 1] * o_w[rs])
            o_ref[0, :, h * HD:(h + 1) * HD] = o


ATTN_HEAD_GROUP = 4
ATTN_KEY_CHUNK = 512
ATTN_WIN_CHUNK = 640


def _attn_prompt(q_hm, ck, cv, ks, vs, kw, vw, gates, band, expand, *, nb, t):
    n_qb = t // Q_BLOCK
    n_cmp_pad = ck.shape[2]
    n_blk = band.shape[0]
    kc = expand.shape[2]
    kv_spec = lambda n, w=HD: pl.BlockSpec((1, 1, n, w), lambda b, k, i: (b, k, 0, 0))
    return pl.pallas_call(
        functools.partial(_attn_p_kernel, n_cmp_pad=n_cmp_pad, n_blk=n_blk, kc=kc, hg=ATTN_HEAD_GROUP, wc=ATTN_WIN_CHUNK),
        grid=(nb, N_KV, n_qb),
        in_specs=[pl.BlockSpec((1, QPK, Q_BLOCK, HD), lambda b, k, i: (b, k, i, 0)),
                  kv_spec(n_cmp_pad), kv_spec(n_cmp_pad), kv_spec(t), kv_spec(t, LANES), kv_spec(t), kv_spec(t, LANES),
                  pl.BlockSpec((Q_BLOCK, LANES), lambda b, k, i: (b * n_qb + i, k)),
                  pl.BlockSpec(band.shape, lambda b, k, i: (0, 0)),
                  pl.BlockSpec(expand.shape, lambda b, k, i: (0, 0, 0))],
        out_specs=pl.BlockSpec((1, Q_BLOCK, QPK * HD), lambda b, k, i: (b, i, k)),
        out_shape=jax.ShapeDtypeStruct((nb, t, ATTN_W), F32),
        compiler_params=_params(3), name="attn_prompt",
    )(q_hm, ck, cv, ks, vs, kw, vw, gates, band, expand)


def _attn_s1_kernel(q_ref, ck_ref, cv_ref, oc_ref, pcs_ref, *, n_chunk, past):
    q = q_ref[0]
    q16 = jnp.concatenate([q, jnp.zeros_like(q)], axis=0).astype(BF16)
    cmp_end = (lax.broadcasted_iota(I32, (1, n_chunk), 1) + 2) * CMP_STRIDE - 1
    valid = cmp_end <= past
    head = lax.broadcasted_iota(I32, (2 * N_HEADS, 1), 0)
    oc = jnp.zeros((2 * N_HEADS, HD), F32)
    pcs = []
    for k in range(N_KV):
        s = lax.dot_general(q16, ck_ref[0, k].astype(BF16), _NT, preferred_element_type=F32)
        p = _softmax_rows(s, valid) * valid.astype(F32)
        in_grp = (head >= k * QPK) & (head < (k + 1) * QPK)
        p = jnp.where(in_grp, p, 0.0)
        oc = oc + jnp.dot(p.astype(BF16), cv_ref[0, k].astype(BF16), preferred_element_type=F32)
        pcs.append(jnp.sum(p, axis=0, keepdims=True))
    oc_ref[0] = oc[0:N_HEADS]
    pcs_ref[0] = jnp.concatenate(pcs + [jnp.zeros((SUBLANES - N_KV, n_chunk), F32)], axis=0)


def _topk_s_kernel(pcs_ref, band_ref, idx_ref, *, n_sel_blocks, past):
    imp = jnp.dot(pcs_ref[...], band_ref[...], preferred_element_type=F32, precision=lax.Precision.HIGHEST)
    rows, nbp = imp.shape
    blk = lax.broadcasted_iota(I32, (rows, nbp), 1)
    cur = past // SEL_BLOCK
    causal = blk * SEL_BLOCK <= past
    forced = causal & ((blk == 0) | (blk == cur) | (blk == cur - 1))
    score = jnp.where(forced, FORCE_SCORE, jnp.where(causal, imp, -1.0))
    score = jnp.where(blk < n_sel_blocks, score, -2.0)
    lane = lax.broadcasted_iota(I32, (rows, LANES), 1)
    out = jnp.zeros((rows, LANES), I32)
    for r in range(min(N_SEL, n_sel_blocks)):
        m = jnp.max(score, axis=-1, keepdims=True)
        pick = jnp.min(jnp.where(score == m, blk, nbp), axis=-1, keepdims=True)
        out = jnp.where(lane == r, pick, out)
        score = jnp.where(blk == pick, -3.0, score)
    idx_ref[...] = out


def _attn_s2_kernel(pt_ref, idx_ref, *refs, n_pages, past, n_sel_blocks):
    ktiles, vtiles = refs[:N_SEL], refs[N_SEL:2 * N_SEL]
    q_ref, oc_ref, kvs_ref, wk_ref, wv_ref, kvw_ref, gate_ref, o_ref, kbuf, vbuf = refs[2 * N_SEL:]
    b = pl.program_id(0)
    k = pl.program_id(1)
    q = q_ref[0]
    q16f = jnp.concatenate([q, jnp.zeros_like(q)], axis=0)
    q16 = q16f.astype(BF16)
    head = lax.broadcasted_iota(I32, (N_HEADS, 1), 0)
    nk = N_SEL * PAGE
    lane = lax.broadcasted_iota(I32, (1, nk), 1)
    slot = lane // PAGE
    new_blk = n_sel_blocks - 1
    wb = wk_ref.shape[-1]
    wpos = past - wb + lax.broadcasted_iota(I32, (1, wb), 1)
    wdist = past - wpos
    valid_w = (wdist >= 0) & (wdist <= WINDOW) & (wpos >= 0)

    def attend(s, valid, v_t, k_new, v_new):
        s_new = jnp.sum(q16f * k_new, axis=-1, keepdims=True)
        s = jnp.where(valid, s, NEG_INF)
        m = jnp.maximum(jnp.max(s, axis=-1, keepdims=True), s_new)
        e = jnp.exp2(s - m)
        e_new = jnp.exp2(s_new - m)
        den = jnp.sum(e, axis=-1, keepdims=True) + e_new
        acc = lax.dot_general(e.astype(BF16), v_t, _NT, preferred_element_type=F32) + e_new * v_new
        return acc / den

    in_grp = (head >= k * QPK) & (head < (k + 1) * QPK)
    bvec = jnp.zeros((1, nk), I32)
    for j in range(N_SEL):
        kbuf[:, j * PAGE:(j + 1) * PAGE] = ktiles[j][...].astype(BF16)
        vbuf[:, j * PAGE:(j + 1) * PAGE] = vtiles[j][...].astype(BF16)
        bvec = jnp.where(slot == j, idx_ref[(b * N_KV + k) * LANES + j], bvec)
    tok = (bvec // 2) * PAGE + lane % PAGE
    valid = (tok // SEL_BLOCK == bvec) & (bvec < new_blk) & (tok <= past)
    s = jnp.dot(q16, kbuf[...], preferred_element_type=F32)
    o_s = attend(s, valid, vbuf[...], kvs_ref[0, pl.ds(k, 1), :], kvs_ref[0, pl.ds(N_KV + k, 1), :])
    sw = jnp.dot(q16, wk_ref[...].astype(BF16), preferred_element_type=F32)
    o_w = attend(sw, valid_w, wv_ref[...].astype(BF16), kvw_ref[0, pl.ds(k, 1), :], kvw_ref[0, pl.ds(N_KV + k, 1), :])
    g = gate_ref[0]
    part = jnp.where(in_grp, g[:, 1:2] * o_s[0:N_HEADS] + g[:, 2:3] * o_w[0:N_HEADS], 0.0)

    @pl.when(k == 0)
    def _():
        o_ref[0] = g[:, 0:1] * oc_ref[0] + part

    @pl.when(k > 0)
    def _():
        o_ref[0] = o_ref[0] + part


def _attn_sample(q3, ck, cv, band_s, sel_t, pt_flat, kvs_rows, win_t, kvw_rows, gates_hm,
                 *, nb, n_pages, past, n_sel_blocks):
    n_chunk = ck.shape[2]
    nbp = band_s.shape[1]
    oc, pcs = pl.pallas_call(
        functools.partial(_attn_s1_kernel, n_chunk=n_chunk, past=past),
        grid=(nb,),
        in_specs=[pl.BlockSpec((1, N_HEADS, HD), lambda b: (b, 0, 0)),
                  pl.BlockSpec((1, N_KV, n_chunk, HD), lambda b: (b, 0, 0, 0)),
                  pl.BlockSpec((1, N_KV, n_chunk, HD), lambda b: (b, 0, 0, 0))],
        out_specs=[pl.BlockSpec((1, N_HEADS, HD), lambda b: (b, 0, 0)),
                   pl.BlockSpec((1, SUBLANES, n_chunk), lambda b: (b, 0, 0))],
        out_shape=[jax.ShapeDtypeStruct((nb, N_HEADS, HD), F32), jax.ShapeDtypeStruct((nb, SUBLANES, n_chunk), F32)],
        compiler_params=_params(1), name="attn_sample_cmp",
    )(q3, ck, cv)
    idx = pl.pallas_call(
        functools.partial(_topk_s_kernel, n_sel_blocks=n_sel_blocks, past=past),
        out_shape=jax.ShapeDtypeStruct((nb * N_KV, LANES), I32),
        compiler_params=pltpu.CompilerParams(vmem_limit_bytes=VMEM_LIMIT), name="topk_sample",
    )(pcs[:, 0:N_KV, :].reshape(nb * N_KV, n_chunk), band_s)
    idx_flat = idx.reshape(-1)

    def tile_map(c, j):
        def f(b, k, pt, ix):
            bidx = ix[(b * N_KV + k) * LANES + j]
            return (pt[b * n_pages + jnp.minimum(bidx // 2, n_pages - 1)], c, k, 0, 0)
        return f

    tile = lambda c, j: pl.BlockSpec((None, None, None, HD, PAGE), tile_map(c, j))
    in_specs = [tile(0, j) for j in range(N_SEL)] + [tile(1, j) for j in range(N_SEL)]
    wb = win_t.shape[-1]
    per_b = lambda shp: pl.BlockSpec(shp, lambda b, k, pt, ix: (b, 0, 0))
    in_specs += [per_b((1, N_HEADS, HD)), per_b((1, N_HEADS, HD)), per_b((1, SUBLANES, HD)),
                 pl.BlockSpec((None, None, None, HD, wb), lambda b, k, pt, ix: (b, 0, k, 0, 0)),
                 pl.BlockSpec((None, None, None, HD, wb), lambda b, k, pt, ix: (b, 1, k, 0, 0)),
                 per_b((1, SUBLANES, HD)), per_b((1, N_HEADS, LANES))]
    grid_spec = pltpu.PrefetchScalarGridSpec(
        num_scalar_prefetch=2, grid=(nb, N_KV), in_specs=in_specs,
        out_specs=per_b((1, N_HEADS, HD)),
        scratch_shapes=[pltpu.VMEM((HD, N_SEL * PAGE), BF16), pltpu.VMEM((HD, N_SEL * PAGE), BF16)])
    return pl.pallas_call(
        functools.partial(_attn_s2_kernel, n_pages=n_pages, past=past, n_sel_blocks=n_sel_blocks),
        grid_spec=grid_spec,
        out_shape=jax.ShapeDtypeStruct((nb, N_HEADS, HD), F32),
        compiler_params=_params(2), name="attn_sample_sel",
    )(pt_flat, idx_flat, *([sel_t] * (2 * N_SEL)), q3, oc, kvs_rows, win_t, win_t, kvw_rows, gates_hm)


TOK_ROWS = D_MODEL // LANES


def _store_token_tiles(ref, x):
    n = x.shape[0]
    for j in range(TOK_ROWS):
        ref[pl.ds(j, n, stride=TOK_ROWS), :] = x[:, j * LANES:(j + 1) * LANES]


def _load_token_tiles(ref, lead, n):
    return jnp.concatenate([ref[lead + (pl.ds(j, n, stride=TOK_ROWS), slice(None))] for j in range(TOK_ROWS)], axis=1)


def _outp_kernel(xp_ref, convp_ref, attnp_ref, ga1p_ref, sc2p_ref, sh2p_ref,
                 xs_ref, convs_ref, attns_ref, ga1s_ref, sc2s_ref, sh2s_ref,
                 gc_ref, ga_ref, w_ref, g2_ref, wr_ref, x1_ref, hp_ref, lg_ref, *, n_prompt_tiles):
    is_p = pl.program_id(0) < n_prompt_tiles
    pick = lambda a, b: jnp.where(is_p, a, b)
    cn = _rms(pick(convp_ref[...], convs_ref[...]), gc_ref[...])
    an = _rms(pick(attnp_ref[...], attns_ref[...]), ga_ref[...])
    cat = jnp.concatenate([cn, an], axis=1).astype(BF16)
    y = jnp.dot(cat, w_ref[...], preferred_element_type=F32)
    x1 = pick(xp_ref[...], xs_ref[...]) + pick(ga1p_ref[0], ga1s_ref[0]) * y
    x1_ref[...] = x1
    hp = _rms(x1, g2_ref[...]) * (1.0 + pick(sc2p_ref[0], sc2s_ref[0])) + pick(sh2p_ref[0], sh2s_ref[0])
    _store_token_tiles(hp_ref, hp)
    hp_hi = hp.astype(BF16)
    hp_lo = (hp - hp_hi.astype(F32)).astype(BF16)
    lg_ref[...] = (jnp.dot(hp_hi, wr_ref[0], preferred_element_type=F32)
                   + (jnp.dot(hp_hi, wr_ref[1], preferred_element_type=F32)
                      + jnp.dot(hp_lo, wr_ref[0], preferred_element_type=F32)))


TOKEN_TILE = 512


def _outp(prompt, sample, g_conv, g_attn, w_out_b, g2, w_route, *, tpb):
    tm = TOKEN_TILE
    n_p = prompt[0].shape[0] // tm
    total = (n_p + 1) * tm
    last = n_p - 1
    prow = lambda w: pl.BlockSpec((tm, w), lambda i: (jnp.minimum(i, last), 0))
    srow = lambda w: pl.BlockSpec((tm, w), lambda i: (0, 0))
    pmod = pl.BlockSpec((1, 1, D_MODEL), lambda i: (jnp.minimum(i, last) // tpb, 0, 0))
    smod = pl.BlockSpec((1, tm, D_MODEL), lambda i: (0, 0, 0))
    vec = lambda w: pl.BlockSpec((1, w), lambda i: (0, 0))
    row = lambda w: pl.BlockSpec((tm, w), lambda i: (i, 0))
    in_specs = [prow(D_MODEL), prow(CONV_W), prow(ATTN_W), pmod, pmod, pmod,
                srow(D_MODEL), srow(CONV_W), srow(ATTN_W), smod, smod, smod,
                vec(CONV_W), vec(ATTN_W), pl.BlockSpec((D_MODEL, D_MODEL), lambda i: (0, 0)), vec(D_MODEL),
                pl.BlockSpec((2, D_MODEL, LANES), lambda i: (0, 0, 0))]
    return pl.pallas_call(
        functools.partial(_outp_kernel, n_prompt_tiles=n_p),
        grid=(n_p + 1,), in_specs=in_specs,
        out_specs=[row(D_MODEL), pl.BlockSpec((tm * TOK_ROWS, LANES), lambda i: (i, 0)), row(LANES)],
        out_shape=[jax.ShapeDtypeStruct((total, D_MODEL), F32), jax.ShapeDtypeStruct((total * TOK_ROWS, LANES), F32),
                   jax.ShapeDtypeStruct((total, LANES), F32)],
        compiler_params=_params(1), name="outp",
    )(*prompt, *sample, g_conv.reshape(1, -1), g_attn.reshape(1, -1), w_out_b, g2.reshape(1, -1), w_route)


def _route_kernel(lg_ref, bias_ref, tri_ref, o_ref, cnt_ref, carry, *, tm, n_valid):
    i = pl.program_id(0)

    @pl.when(i == 0)
    def _():
        carry[...] = jnp.zeros_like(carry)

    lane = lax.broadcasted_iota(I32, (tm, LANES), 1)
    rowid = i * tm + lax.broadcasted_iota(I32, (tm, 1), 0)
    live = rowid < n_valid
    lg = lg_ref[...] + bias_ref[...]
    is_g = lane < N_GROUPS
    lgg = jnp.where(is_g, lg, NEG_INF)
    gmax = jnp.max(lgg, axis=-1, keepdims=True)
    grp = jnp.min(jnp.where(is_g & (lgg == gmax), lane, LANES), axis=-1, keepdims=True)
    p_grp = 1.0 / jnp.sum(jnp.where(is_g, jnp.exp(lgg - gmax), 0.0), axis=-1, keepdims=True)
    eid = lane - N_GROUPS
    in_grp = (eid >= grp * EPG) & (eid < (grp + 1) * EPG)
    le = jnp.where(in_grp, lg, NEG_INF)
    v1 = jnp.max(le, axis=-1, keepdims=True)
    e1 = jnp.min(jnp.where(in_grp & (le == v1), eid, LANES), axis=-1, keepdims=True)
    le2 = jnp.where(eid == e1, NEG_INF, le)
    v2 = jnp.max(le2, axis=-1, keepdims=True)
    e2 = jnp.min(jnp.where(in_grp & (eid != e1) & (le2 == v2), eid, LANES), axis=-1, keepdims=True)
    ex2 = jnp.exp(v2 - v1)
    w1 = p_grp * (1.0 / (1.0 + ex2))
    w2 = p_grp * (ex2 / (1.0 + ex2))
    oh1 = ((lane == e1) & live).astype(F32)
    oh2 = ((lane == e2) & live).astype(F32)
    both = oh1 + oh2
    before = jnp.dot(tri_ref[...], both.astype(BF16), preferred_element_type=F32) + carry[0:1, :]
    r1 = jnp.sum(oh1 * before, axis=-1, keepdims=True)
    r2 = jnp.sum(oh2 * before, axis=-1, keepdims=True)
    carry[0:1, :] = carry[0:1, :] + jnp.sum(both, axis=0, keepdims=True)
    out = jnp.where(lane == 0, e1.astype(F32), 0.0)
    out = jnp.where(lane == 1, e2.astype(F32), out)
    out = jnp.where(lane == 2, w1, out)
    out = jnp.where(lane == 3, w2, out)
    out = jnp.where(lane == 4, r1, out)
    out = jnp.where(lane == 5, r2, out)
    o_ref[...] = out
    cnt_ref[...] = carry[...]


def _route(logits, bias_row, n_valid):
    total = logits.shape[0]
    tm = TOKEN_TILE
    n_tiles = total // tm
    tri =(np.arange(tm)[:, None] > np.arange(tm)[None, :]).astype(np.float32)
    return pl.pallas_call(
        functools.partial(_route_kernel, tm=tm, n_valid=n_valid),
        grid=(n_tiles,),
        in_specs=[pl.BlockSpec((tm, LANES), lambda i: (i, 0)), pl.BlockSpec((1, LANES), lambda i: (0, 0)),
                  pl.BlockSpec((tm, tm), lambda i: (0, 0))],
        out_specs=[pl.BlockSpec((tm, LANES), lambda i: (i, 0)), pl.BlockSpec((SUBLANES, LANES), lambda i: (0, 0))],
        out_shape=[jax.ShapeDtypeStruct((total, LANES), F32), jax.ShapeDtypeStruct((SUBLANES, LANES), F32)],
        scratch_shapes=[pltpu.VMEM((SUBLANES, LANES), F32)],
        compiler_params=_params(1), name="route",
    )(logits, bias_row, jnp.asarray(tri, BF16))


EXPERT_ROWS = 256
DISPATCH_TILE = 256


def _tile_copy(src, src_row, dst, dst_row, sem):
    return pltpu.make_async_copy(src.at[pl.ds(pl.multiple_of(src_row * TOK_ROWS, TOK_ROWS), TOK_ROWS), :],
                                 dst.at[pl.ds(pl.multiple_of(dst_row * TOK_ROWS, TOK_ROWS), TOK_ROWS), :], sem)


def _dispatch_kernel(zstart_ref, zcnt_ref, dest_ref, x_ref, xb_hbm, stage, zeros, sem, zsem, *, n_tiles, n_blocks):
    i = pl.program_id(0)
    tm = DISPATCH_TILE
    slot = i % 2
    blk_rows = EXPERT_ROWS * TOK_ROWS

    def tail_copy(b):
        off = pl.multiple_of(b * blk_rows, blk_rows)
        return pltpu.make_async_copy(zeros, xb_hbm.at[pl.ds(off, blk_rows), :], zsem.at[1])

    @pl.when(i == 0)
    def _():
        zeros[...] = jnp.zeros_like(zeros)
        first_tail = zstart_ref[N_EXPERTS] // EXPERT_ROWS
        for e in range(N_EXPERTS):
            def fill(r, c, e=e):
                _tile_copy(zeros, 0, xb_hbm, zstart_ref[e] + r, zsem.at[0]).start()
                return c
            lax.fori_loop(0, zcnt_ref[e], fill, 0)
        lax.fori_loop(first_tail, n_blocks, lambda b, c: (tail_copy(b).start(), c)[1], 0)
        for e in range(N_EXPERTS):
            def drain(r, c):
                _tile_copy(zeros, 0, xb_hbm, 0, zsem.at[0]).wait()
                return c
            lax.fori_loop(0, zcnt_ref[e], drain, 0)
        lax.fori_loop(first_tail, n_blocks, lambda b, c: (tail_copy(b).wait(), c)[1], 0)

    def wait_rows(s):
        for _ in range(2 * tm):
            _tile_copy(stage.at[s], 0, xb_hbm, 0, sem.at[s]).wait()

    @pl.when(i >= 2)
    def _():
        wait_rows(slot)

    stage[slot] = x_ref[...]
    for r in range(tm):
        for k in range(2):
            _tile_copy(stage.at[slot], r, xb_hbm, dest_ref[0, 0, 2 * r + k], sem.at[slot]).start()

    @pl.when(i == n_tiles - 1)
    def _():
        wait_rows(slot)
        if n_tiles > 1:
            wait_rows(1 - slot)


def _dispatch(dest_pad, zstart, zcnt, hp_all, n_rows, n_blocks):
    tm = DISPATCH_TILE
    n_tiles = dest_pad.shape[0] // tm
    grid_spec = pltpu.PrefetchScalarGridSpec(
        num_scalar_prefetch=2, grid=(n_tiles,),
        in_specs=[pl.BlockSpec((1, 1, 2 * tm), lambda i, zs, zc: (i, 0, 0), memory_space=pltpu.SMEM),
                  pl.BlockSpec((tm * TOK_ROWS, LANES), lambda i, zs, zc: (i, 0))],
        out_specs=pl.BlockSpec(memory_space=pl.ANY),
        scratch_shapes=[pltpu.VMEM((2, tm * TOK_ROWS, LANES), F32), pltpu.VMEM((EXPERT_ROWS * TOK_ROWS, LANES), F32),
                        pltpu.SemaphoreType.DMA((2,)), pltpu.SemaphoreType.DMA((2,))])
    return pl.pallas_call(
        functools.partial(_dispatch_kernel, n_tiles=n_tiles, n_blocks=n_blocks),
        grid_spec=grid_spec,
        out_shape=jax.ShapeDtypeStruct((n_rows * TOK_ROWS, LANES), F32),
        compiler_params=_params(1), name="dispatch",
    )(zstart, zcnt, dest_pad.reshape(n_tiles, 1, 2 * tm), hp_all)


def _experts_kernel(blk_e_ref, x_ref, wg_ref, wu_ref, wd_ref, o_ref, wg_b, wu_b, wd_b):
    i = pl.program_id(0)
    changed = jnp.logical_or(i == 0, blk_e_ref[i] != blk_e_ref[jnp.maximum(i - 1, 0)])

    @pl.when(changed)
    def _():
        wg_b[...] = wg_ref[0].astype(BF16)
        wu_b[...] = wu_ref[0].astype(BF16)
        wd_b[...] = wd_ref[0].astype(BF16)

    x = _load_token_tiles(x_ref, (), EXPERT_ROWS).astype(BF16)
    g = jnp.dot(x, wg_b[...], preferred_element_type=F32)
    u = jnp.dot(x, wu_b[...], preferred_element_type=F32)
    h = (g * jax.nn.sigmoid(g)) * u
    _store_token_tiles(o_ref, jnp.dot(h.astype(BF16), wd_b[...], preferred_element_type=F32))


def _experts(blk_e, xb, w_gate, w_up, w_down, n_blocks):
    blk = pl.BlockSpec((EXPERT_ROWS * TOK_ROWS, LANES), lambda i, be: (i, 0))
    grid_spec = pltpu.PrefetchScalarGridSpec(
        num_scalar_prefetch=1, grid=(n_blocks,),
        in_specs=[blk,
                  pl.BlockSpec((1, D_MODEL, D_EXPERT), lambda i, be: (be[i], 0, 0)),
                  pl.BlockSpec((1, D_MODEL, D_EXPERT), lambda i, be: (be[i], 0, 0)),
                  pl.BlockSpec((1, D_EXPERT, D_MODEL), lambda i, be: (be[i], 0, 0))],
        out_specs=blk,
        scratch_shapes=[pltpu.VMEM((D_MODEL, D_EXPERT), BF16), pltpu.VMEM((D_MODEL, D_EXPERT), BF16),
                        pltpu.VMEM((D_EXPERT, D_MODEL), BF16)])
    return pl.pallas_call(
        _experts_kernel,
        grid_spec=grid_spec,
        out_shape=jax.ShapeDtypeStruct((n_blocks * EXPERT_ROWS * TOK_ROWS, LANES), F32),
        compiler_params=_params(1), name="experts",
    )(blk_e, xb, w_gate, w_up, w_down)


def _final_kernel(dest_first_ref, dest_next_ref, yb_hbm, x1_ref, wt_ref, gate2_ref, gf_ref, o_ref, ybuf, sem,
                  *, tm, n_tiles):
    i = pl.program_id(0)
    slot = i % 2

    def issue(dest_ref, s):
        for r in range(tm):
            for k in range(2):
                d = dest_ref[0, 0, 2 * r + k]
                src = yb_hbm.at[pl.ds(pl.multiple_of(d * TOK_ROWS, TOK_ROWS), TOK_ROWS), :]
                pltpu.make_async_copy(src, ybuf.at[s, k, pl.ds(r * TOK_ROWS, TOK_ROWS), :], sem.at[s]).start()

    @pl.when(i == 0)
    def _():
        issue(dest_first_ref, 0)

    @pl.when(i + 1 < n_tiles)
    def _():
        issue(dest_next_ref, 1 - slot)

    for r in range(tm):
        for k in range(2):
            pltpu.make_async_copy(yb_hbm.at[pl.ds(0, TOK_ROWS), :], ybuf.at[slot, k, pl.ds(r * TOK_ROWS, TOK_ROWS), :],
                                  sem.at[slot]).wait()
    wt = wt_ref[...]
    f = wt[:, 2:3] * _load_token_tiles(ybuf, (slot, 0), tm) + wt[:, 3:4] * _load_token_tiles(ybuf, (slot, 1), tm)
    x2 = x1_ref[...] + gate2_ref[0] * f
    o_ref[...] = _rms(x2, gf_ref[...])


def _final(dest_pad, yb, x1_all, route_rows, gate2, final_g, *, rows, tpb, per_row, row0):
    tm = min(256, rows)
    n_tiles = rows // tm
    blk0 = row0 // tm
    dest3 = dest_pad.reshape(-1, 1, 2 * tm)
    idx_blk = lambda f: pl.BlockSpec((1, 1, 2 * tm), f, memory_space=pltpu.SMEM)
    mod = (pl.BlockSpec((1, tm, D_MODEL), lambda i: (0, i, 0)) if per_row
           else pl.BlockSpec((1, 1, D_MODEL), lambda i: (i // tpb, 0, 0)))
    return pl.pallas_call(
        functools.partial(_final_kernel, tm=tm, n_tiles=n_tiles),
        grid=(n_tiles,),
        in_specs=[idx_blk(lambda i: (blk0, 0, 0)),
                  idx_blk(lambda i: (blk0 + jnp.minimum(i + 1, n_tiles - 1), 0, 0)),
                  pl.BlockSpec(memory_space=pl.ANY),
                  pl.BlockSpec((tm, D_MODEL), lambda i: (blk0 + i, 0)),
                  pl.BlockSpec((tm, LANES), lambda i: (blk0 + i, 0)),
                  mod, pl.BlockSpec((1, D_MODEL), lambda i: (0, 0))],
        out_specs=pl.BlockSpec((tm, D_MODEL), lambda i: (i, 0)),
        scratch_shapes=[pltpu.VMEM((2, 2, tm * TOK_ROWS, LANES), F32), pltpu.SemaphoreType.DMA((2,))],
        out_shape=jax.ShapeDtypeStruct((rows, D_MODEL), F32),
        compiler_params=_params(1), name="final_sample" if per_row else "final_prompt",
    )(dest3, dest3, yb, x1_all, route_rows, gate2, final_g.reshape(1, -1))


def _rope_tables(pos):
    inv = ROPE_THETA ** (-jnp.arange(HALF, dtype=F32) / HALF)
    ang = pos.astype(F32)[:, None] * inv[None, :]
    cos = jnp.tile(jnp.cos(ang), (1, LANES // HALF))
    sin = jnp.sin(ang)
    sin_s = jnp.tile(jnp.concatenate([-sin, sin], axis=1), (1, LANES // HD))
    return cos, sin_s


def _pack_w_in(w_in):
    gl = w_in[:, _C_G:_C_G + 3 * N_HEADS].reshape(D_MODEL, 3, N_KV, QPK)
    gcols = []
    for k in range(N_KV):
        gk = gl[:, :, k, :].reshape(D_MODEL, 3 * QPK)
        gcols.append(jnp.pad(gk, ((0, 0), (0, LANES - 3 * QPK))))
    return jnp.concatenate([w_in[:, :_C_G]] + gcols, axis=1).astype(BF16)


def _pack_cmp_weights(cmp_w1, cmp_w2, bias, cmp_b2):
    w1 = cmp_w1.reshape(2, 2, CMP_STRIDE, HD, CMP_HID)
    eye = jnp.eye(N_KV, dtype=F32)
    w1p = w1.transpose(0, 2, 3, 1, 4).reshape(2, CMP_STRIDE * HD, 2 * CMP_HID)
    w2p =jnp.einsum('chd,pk->cphkd', cmp_w2, eye).reshape(2, N_KV * CMP_HID, KV_W)
    b1p = jnp.tile(bias, (1, N_KV)).reshape(2, 1, N_KV * CMP_HID)
    b2p = jnp.tile(cmp_b2, (1, N_KV)).reshape(2, 1, KV_W)
    return w1p.astype(BF16), b1p, w2p.astype(BF16), b2p


def _band(n_cmp_pad, n_cmp, n_blk_pad, n_blk):
    n = np.arange(n_cmp_pad)[:, None]
    b = np.arange(n_blk_pad)[None, :]
    r = SEL_BLOCK // CMP_STRIDE
    m = (n >= r * b - 1) & (n <= r * b + r - 1) & (n < n_cmp) & (b < n_blk)
    return jnp.asarray(m.astype(np.float32))


def _expand(t, kc):
    n_chunks = t // kc
    key = np.arange(t).reshape(n_chunks, 1, kc)
    blk = np.arange(t // SEL_BLOCK).reshape(1, -1, 1)
    return jnp.asarray((key // SEL_BLOCK == blk).astype(np.float32), BF16)


def kernel(x_prompt, x_sample, c_prompt, c_sample, cache_cmp_kv, cache_sel_kv, cache_win_kv, state_conv, page_table,
           ln1_g, ln2_g, w_ada, b_ada, w_in, w_conv, cmp_pos, cmp_w1, cmp_b1, cmp_w2, cmp_b2, g_out_conv, g_out_attn,
           w_out, w_route_group, b_route_group, w_route_expert, b_route_expert, w_gate, w_up, w_down, final_g):
    depth = w_in.shape[0]
    assert depth == 1, "single-layer step"
    nb, t, _ = x_prompt.shape
    ns, ts, _ = x_sample.shape
    assert ts == 1 and t % 512 == 0 and t >= WINDOW + Q_BLOCK
    n_pool = cache_cmp_kv.shape[1]
    n_pages = page_table.shape[1]
    past = n_pages * PAGE
    wb = cache_win_kv.shape[2]
    assert wb == WINDOW
    l = 0

    n_c = nb + ns
    c_all = jnp.pad(jnp.concatenate([c_prompt, c_sample], axis=0), ((0, (-n_c) % SUBLANES), (0, 0)))
    mods = _ada(c_all, w_ada[l], b_ada[l])
    sh1, sc1, ga1, sh2, sc2, ga2 = [mods[:, j * D_MODEL:(j + 1) * D_MODEL] for j in range(6)]
    pr = lambda a: a[0:nb].reshape(nb, 1, D_MODEL)
    sr = lambda a: a[nb:nb + ns].reshape(1, ns, D_MODEL)

    w_pack = _pack_w_in(w_in[l])
    wconv8 = jnp.pad(w_conv[l], ((0, SUBLANES - CONV_K), (0, 0)))
    cos_p, sin_p = _rope_tables(jnp.arange(t, dtype=I32))
    cos_s, sin_s = _rope_tables(jnp.full((1,), past, I32))
    xp2 = x_prompt.reshape(nb * t, D_MODEL)
    xs2 = x_sample.reshape(ns, D_MODEL)
    (conv_p, cst_p, q_p, kvc_p, kvc_rows_p, kvs_rows_p, kvw_rows_p, ks_p, vs_p, kw_p, vw_p, gates_p) = _proj(
        xp2, ln1_g[l], pr(sc1), pr(sh1), w_pack, wconv8, cos_p, sin_p, nb=nb, t=t, sample=False)
    (conv_s, cst_s, q_s, _, kvc_rows_s, kvs_rows_s, kvw_rows_s, _, _, _, _, gates_s) = _proj(
        xs2, ln1_g[l], sr(sc1), sr(sh1), w_pack, wconv8, cos_s, sin_s, nb=ns, t=1, sample=True,
        prev=(state_conv[l][:, 0], state_conv[l][:, 1]))

    bias = _cmpbias(cmp_pos[l], cmp_w1[l], cmp_b1[l])
    w1p, b1p, w2p, b2p = _pack_cmp_weights(cmp_w1[l], cmp_w2[l], bias, cmp_b2[l])
    pp = t // PAGE
    cos_cp, sin_cp = _rope_tables((jnp.arange(t // CMP_STRIDE, dtype=I32) + 2) * CMP_STRIDE - 1)
    ck_p, cv_p = _cmp(kvc_p.reshape(nb * pp, SUBLANES, CHUNK_ROW), jnp.arange(nb * pp, dtype=I32), nb, pp,
                      w1p, b1p, w2p, b2p, cos_cp, sin_cp, "cmp_prompt", tiles=False)
    pt_flat = page_table.reshape(-1).astype(I32)
    cos_cs, sin_cs = _rope_tables((jnp.arange(past // CMP_STRIDE, dtype=I32) + 2) * CMP_STRIDE - 1)
    to_tiles = lambda a: a.transpose(0, 2, 3, 4, 1)
    ck_s, cv_s = _cmp(to_tiles(cache_cmp_kv[l]), pt_flat, ns, n_pages,
                      w1p, b1p, w2p, b2p, cos_cs, sin_cs, "cmp_sample", tiles=True)

    n_chunk_p = t // CMP_STRIDE
    n_blk_p = t // SEL_BLOCK
    band_p = _band(n_chunk_p, n_chunk_p - 1, n_blk_p, n_blk_p)
    attn_p = _attn_prompt(q_p, ck_p, cv_p, ks_p, vs_p, kw_p, vw_p, gates_p, band_p.T, _expand(t, ATTN_KEY_CHUNK), nb=nb, t=t)

    n_chunk_s = past // CMP_STRIDE
    n_sel_s = -(-(past + 1) // SEL_BLOCK)
    nbp = -(-n_sel_s // LANES) * LANES
    band_s = _band(n_chunk_s, (past + 1) // CMP_STRIDE - 1, nbp, n_sel_s)
    q3 = q_s.reshape(N_HEADS, ns, HD).transpose(1, 0, 2).astype(F32)
    gs = gates_s.reshape(ns, N_KV, LANES)[:, :, :3 * QPK].reshape(ns, N_KV, 3, QPK)
    gates_hm = jnp.pad(gs.transpose(0, 1, 3, 2).reshape(ns, N_HEADS, 3), ((0, 0), (0, 0), (0, LANES - 3)))
    rpt = 2 * N_KV
    new_rows = lambda a: jnp.pad(a.reshape(ns, rpt, HD), ((0, 0), (0, SUBLANES - rpt), (0, 0)))
    attn_s = _attn_sample(q3, ck_s, cv_s, band_s, to_tiles(cache_sel_kv[l]), pt_flat,
                          new_rows(kvs_rows_s), to_tiles(cache_win_kv[l]), new_rows(kvw_rows_s),
                          gates_hm, nb=ns, n_pages=n_pages, past=past, n_sel_blocks=n_sel_s).reshape(ns, ATTN_W)

    total = nb * t + ns
    w_out_b = w_out[l].astype(BF16)
    w_route = jnp.pad(jnp.concatenate([w_route_group[l], w_route_expert[l]], axis=1),
                      ((0, 0), (0, LANES - N_GROUPS - N_EXPERTS)))
    w_route_hi = w_route.astype(BF16)
    w_route = jnp.stack([w_route_hi, (w_route - w_route_hi.astype(F32)).astype(BF16)])
    b_route = jnp.pad(jnp.concatenate([b_route_group[l], b_route_expert[l]]), (0, LANES - N_GROUPS - N_EXPERTS))
    tile_pad = lambda a: jnp.pad(a, ((0, TOKEN_TILE - ns), (0, 0)))
    smod = lambda a: tile_pad(a[nb:nb + ns]).reshape(1, TOKEN_TILE, D_MODEL)
    x1_all, hp_all, lg_all = _outp(
        (xp2, conv_p, attn_p.reshape(nb * t, ATTN_W), pr(ga1), pr(sc2), pr(sh2)),
        (tile_pad(xs2), tile_pad(conv_s), tile_pad(attn_s), smod(ga1), smod(sc2), smod(sh2)),
        g_out_conv[l], g_out_attn[l], w_out_b, ln2_g[l], w_route, tpb=t // TOKEN_TILE)

    route, counts = _route(lg_all, b_route.reshape(1, LANES), total)
    e = route[:total, 0:2].astype(I32)
    rank = route[:total, 4:6].astype(I32)
    cnt = counts[0, :N_EXPERTS].astype(I32)
    padded = (cnt + EXPERT_ROWS - 1) // EXPERT_ROWS * EXPERT_ROWS
    pad_end = jnp.cumsum(padded)
    pad_start = pad_end - padded
    m_slots = total * 2
    n_blocks = -(-(m_slots + N_EXPERTS * (EXPERT_ROWS - 1)) // EXPERT_ROWS)
    n_slots = n_blocks * EXPERT_ROWS
    dest = jnp.clip(pad_start[e] + rank, 0, n_slots - 1)
    blk_start = jnp.arange(n_blocks, dtype=I32) * EXPERT_ROWS
    blk_e = jnp.minimum(jnp.sum((pad_end[None, :] <= blk_start[:, None]).astype(I32), axis=1), N_EXPERTS - 1)
    n_dump = 2 * (x1_all.shape[0] - total)
    dest_pad = jnp.concatenate([dest, n_slots + jnp.arange(n_dump, dtype=I32).reshape(-1, 2)], axis=0)

    zstart = jnp.concatenate([pad_start + cnt, pad_end[-1:]])
    zcnt = jnp.concatenate([padded - cnt, jnp.zeros((1,), I32)])
    xb = _dispatch(dest_pad, zstart, zcnt, hp_all, n_slots + n_dump, n_blocks)
    yb = _experts(blk_e, xb, w_gate[l], w_up[l], w_down[l], n_blocks)
    y_p = _final(dest_pad, yb, x1_all, route, pr(ga2), final_g, rows=nb * t, tpb=t // 256, per_row=False, row0=0)
    y_s = _final(dest_pad, yb, x1_all, route, sr(ga2), final_g, rows=ns, tpb=1, per_row=True, row0=nb * t)

    kv_shape = (2, N_KV, HD)
    y_prompt = y_p.reshape(nb, t, D_MODEL)
    y_sample = y_s.reshape(ns, 1, D_MODEL)
    new_cmp_prompt = kvc_rows_p.reshape((1, nb, t) + kv_shape)
    new_cmp_sample = kvc_rows_s.reshape((1, ns, 1) + kv_shape)
    new_sel_prompt = kvs_rows_p.reshape((1, nb, t) + kv_shape)
    new_sel_sample = kvs_rows_s.reshape((1, ns, 1) + kv_shape)
    new_win_prompt = kvw_rows_p.reshape((nb, t) + kv_shape)[:, t - WINDOW:][None]
    new_win_sample = jnp.concatenate([cache_win_kv[l][:, 1:], kvw_rows_s.reshape((ns, 1) + kv_shape)], axis=1)[None]
    new_conv_prompt = cst_p[:, SUBLANES - (CONV_K - 1):][None]
    new_conv_sample = jnp.stack([state_conv[l][:, 1], cst_s], axis=1)[None]
    return (y_prompt, y_sample, new_cmp_prompt, new_cmp_sample, new_sel_prompt, new_sel_sample,
            new_win_prompt, new_win_sample, new_conv_prompt, new_conv_sample)
```

```python
import functools

import numpy as np
import jax
import jax.numpy as jnp
from jax import lax
from jax.experimental import pallas as pl
from jax.experimental.pallas import tpu as pltpu

F32 = jnp.float32
BF16 = jnp.bfloat16
I32 = jnp.int32

D_MODEL = 1024
CONV_W = 512
ATTN_W = 512
HD = 64
HALF = HD // 2
N_HEADS = 8
N_KV = 2
QPK = 4
KV_W = N_KV * HD
CONV_K = 3
PAGE = 128
CMP_STRIDE = 16
CMP_HID = 128
SEL_BLOCK = 64
N_SEL = 16
WINDOW = 512
Q_BLOCK = 128
ROPE_THETA = 10000.0
N_GROUPS = 4
EPG = 8
N_EXPERTS = 32
D_EXPERT = 512
NORM_EPS = 1e-6
NEG_INF = -1e30
FORCE_SCORE = 1e4
LANES = 128
SUBLANES = 8
CHUNK_ROW = CMP_STRIDE * 2 * KV_W
VMEM_LIMIT = 56 * 1024 * 1024

_NT = (((1,), (1,)), ((), ()))
Q_SCALE = HD ** -0.5 * 1.4426950408889634


def _params(n_axes):
    return pltpu.CompilerParams(dimension_semantics=("arbitrary",) * n_axes,
                                vmem_limit_bytes=VMEM_LIMIT)


def _rms(x, g):
    return x * lax.rsqrt(jnp.mean(x * x, axis=-1, keepdims=True) + NORM_EPS) * g


def _rope128(x, cos, sin_signed, first_half):
    xr = jnp.where(first_half, pltpu.roll(x, LANES - HALF, 1), pltpu.roll(x, HALF, 1))
    return x * cos + xr * sin_signed


def _first_half_mask(rows):
    lane = lax.broadcasted_iota(I32, (rows, LANES), 1)
    return (lane % HD) < HALF


def _ada_kernel(c_ref, w_ref, b_ref, o_ref):
    c = c_ref[...]
    s = c * jax.nn.sigmoid(c)
    o_ref[...] = jnp.dot(s.astype(BF16), w_ref[...].astype(BF16), preferred_element_type=F32) + b_ref[...]


def _ada(c_all, w_ada, b_ada):
    m, d = c_all.shape
    n = w_ada.shape[1]
    tn = 1024
    return pl.pallas_call(
        _ada_kernel,
        grid=(n // tn,),
        in_specs=[pl.BlockSpec((m, d), lambda j: (0, 0)),
                  pl.BlockSpec((d, tn), lambda j: (0, j)),
                  pl.BlockSpec((1, tn), lambda j: (0, j))],
        out_specs=pl.BlockSpec((m, tn), lambda j: (0, j)),
        out_shape=jax.ShapeDtypeStruct((m, n), F32),
        compiler_params=_params(1),
        name="ada",
    )(c_all, w_ada, b_ada.reshape(1, n))


_C_B, _C_C, _C_U, _C_Q, _C_KVC, _C_KVS, _C_KVW, _C_G, _C_END = 0, 512, 1024, 1536, 2048, 2304, 2560, 2816, 3072


def _proj_kernel(*refs, tm, tpb, sample):
    if sample:
        (x_ref, g1_ref, sc_ref, sh_ref, w_ref, wc_ref, cos_ref, sin_ref, p0_ref, p1_ref,
         conv_ref, cst_ref, q_ref, kvc_ref, kvc_il_ref, kvs_ref, kvw_ref, ks_ref, vs_ref, kw_ref, vw_ref, gate_ref,
         ilbuf) = refs
        vbuf = None
    else:
        (x_ref, g1_ref, sc_ref, sh_ref, w_ref, wc_ref, cos_ref, sin_ref,
         conv_ref, cst_ref, q_ref, kvc_ref, kvc_il_ref, kvs_ref, kvw_ref, ks_ref, vs_ref, kw_ref, vw_ref, gate_ref,
         ilbuf, vbuf) = refs
    i = pl.program_id(0)
    x = x_ref[...]
    h = _rms(x, g1_ref[...]) * (1.0 + sc_ref[0]) + sh_ref[0]
    hb = h.astype(BF16)

    zc = jnp.dot(hb, w_ref[:, _C_B:_C_Q], preferred_element_type=F32)
    b_g = zc[:, 0:CONV_W]
    v = zc[:, CONV_W:2 * CONV_W] * zc[:, 2 * CONV_W:3 * CONV_W]
    wc = wc_ref[...]
    if sample:
        y = wc[0:1] * p0_ref[...] + wc[1:2] * p1_ref[...] + wc[2:3] * v
        cst_ref[...] = v
    else:
        @pl.when(i % tpb == 0)
        def _():
            vbuf[0:SUBLANES, :] = jnp.zeros((SUBLANES, CONV_W), F32)
        vbuf[SUBLANES:SUBLANES + tm, :] = v
        y = wc[0:1] * vbuf[pl.ds(SUBLANES - 2, tm), :] + wc[1:2] * vbuf[pl.ds(SUBLANES - 1, tm), :] + wc[2:3] * v
        tail = vbuf[tm:tm + SUBLANES, :]
        cst_ref[0] = tail
        vbuf[0:SUBLANES, :] = tail
    conv_ref[...] = b_g * y

    cos = cos_ref[...]
    sin_s = sin_ref[...]
    first = _first_half_mask(tm)

    zq = jnp.dot(hb, w_ref[:, _C_Q:_C_KVC], preferred_element_type=F32)
    for gq in range(ATTN_W // LANES):
        qr = _rope128(zq[:, gq * LANES:(gq + 1) * LANES], cos, sin_s, first) * Q_SCALE
        q_ref[0, 2 * gq] = qr[:, 0:HD].astype(BF16)
        q_ref[0, 2 * gq + 1] = qr[:, HD:LANES].astype(BF16)

    def store_rows(out_ref, halves):
        for j in range(2 * N_KV):
            piece = halves[j // N_KV]
            if j % N_KV == 1:
                piece = pltpu.roll(piece, HD, 1)
            ilbuf[pl.ds(j, tm, stride=2 * N_KV), :] = piece
        out_ref[...] = ilbuf[:, 0:HD]

    zkv = jnp.dot(hb, w_ref[:, _C_KVC:_C_G], preferred_element_type=F32)
    kvc_ref[...] = zkv[:, 0:2 * KV_W]
    store_rows(kvc_il_ref, (zkv[:, 0:KV_W], zkv[:, KV_W:2 * KV_W]))
    for base, kv_ref, kh_ref, vh_ref in ((2 * KV_W, kvs_ref, ks_ref, vs_ref), (4 * KV_W, kvw_ref, kw_ref, vw_ref)):
        kr = _rope128(zkv[:, base:base + KV_W], cos, sin_s, first)
        vv = zkv[:, base + KV_W:base + 2 * KV_W]
        store_rows(kv_ref, (kr, vv))
        lane = lax.broadcasted_iota(I32, (tm, LANES), 1)
        for k in range(N_KV):
            kh_ref[0, k] = kr[:, k * HD:(k + 1) * HD].astype(BF16)
            vk = vv if k == 0 else pltpu.roll(vv, HD, 1)
            vh_ref[0, k] = jnp.where(lane < HD, vk, jnp.where(lane == HD, 1.0, 0.0)).astype(BF16)

    zg = jnp.dot(hb, w_ref[:, _C_G:_C_END], preferred_element_type=F32)
    gate_ref[...] = jax.nn.sigmoid(zg)


def _proj(x2d, g1, sc, sh, w_pack, w_conv, cos_t, sin_t, *, nb, t, sample, prev=None):
    rows = nb * t
    tm = min(512, rows) if not sample else rows
    tpb = (t // tm) if not sample else 1
    n_tiles = rows // tm
    f = lambda a: jax.ShapeDtypeStruct(a, F32)
    b = lambda a: jax.ShapeDtypeStruct(a, BF16)
    if sample:
        mod_spec = pl.BlockSpec((1, tm, D_MODEL), lambda i: (0, 0, 0))
        tab_spec = pl.BlockSpec((1, LANES), lambda i: (0, 0))
        cst_shape, cst_spec = f((rows, CONV_W)), pl.BlockSpec((tm, CONV_W), lambda i: (0, 0))
        hm = lambda i: (0, 0, i, 0)
        hb_, ht_ = 1, rows
    else:
        mod_spec = pl.BlockSpec((1, 1, D_MODEL), lambda i: (i // tpb, 0, 0))
        tab_spec = pl.BlockSpec((tm, LANES), lambda i: (i % tpb, 0))
        cst_shape, cst_spec = f((nb, SUBLANES, CONV_W)), pl.BlockSpec((1, SUBLANES, CONV_W), lambda i: (i // tpb, 0, 0))
        hm = lambda i: (i // tpb, 0, i % tpb, 0)
        hb_, ht_ = nb, t
    row = lambda w: pl.BlockSpec((tm, w), lambda i: (i, 0))
    in_specs = [row(D_MODEL), pl.BlockSpec((1, D_MODEL), lambda i: (0, 0)), mod_spec, mod_spec,
                pl.BlockSpec((D_MODEL, _C_END), lambda i: (0, 0)),
                pl.BlockSpec((SUBLANES, CONV_W), lambda i: (0, 0)), tab_spec, tab_spec]
    args = [x2d, g1.reshape(1, D_MODEL), sc, sh, w_pack, w_conv, cos_t, sin_t]
    scratch = [pltpu.VMEM((2 * N_KV * tm, LANES), F32)]
    if sample:
        in_specs += [row(CONV_W), row(CONV_W)]
        args += [prev[0], prev[1]]
    else:
        scratch.append(pltpu.VMEM((tm + SUBLANES, CONV_W), F32))
    il_rows = 2 * N_KV * rows
    il = pl.BlockSpec((2 * N_KV * tm, HD), lambda i: (i, 0))
    out_shape = [f((rows, CONV_W)), cst_shape, b((hb_, N_HEADS, ht_, HD)),
                 f((rows, 2 * KV_W)), f((il_rows, HD)), f((il_rows, HD)), f((il_rows, HD)),
                 b((hb_, N_KV, ht_, HD)), b((hb_, N_KV, ht_, LANES)), b((hb_, N_KV, ht_, HD)), b((hb_, N_KV, ht_, LANES)),
                 f((rows, 2 * LANES))]
    out_specs = [row(CONV_W), cst_spec, pl.BlockSpec((1, N_HEADS, tm, HD), hm),
                 row(2 * KV_W), il, il, il,
                 pl.BlockSpec((1, N_KV, tm, HD), hm), pl.BlockSpec((1, N_KV, tm, LANES), hm),
                 pl.BlockSpec((1, N_KV, tm, HD), hm), pl.BlockSpec((1, N_KV, tm, LANES), hm),
                 row(2 * LANES)]
    return pl.pallas_call(
        functools.partial(_proj_kernel, tm=tm, tpb=tpb, sample=sample),
        grid=(n_tiles,), in_specs=in_specs, out_specs=out_specs, out_shape=out_shape,
        scratch_shapes=scratch, compiler_params=_params(1),
        name="proj_sample" if sample else "proj_prompt",
    )(*args)


def _cmpbias_kernel(pos_ref, w_ref, b1_ref, o_ref):
    for c in range(2):
        o_ref[c:c + 1, :] = jnp.sum(pos_ref[c] * w_ref[c], axis=0, keepdims=True) + b1_ref[c:c + 1, :]


def _cmpbias(cmp_pos, cmp_w1, cmp_b1):
    n = cmp_pos.shape[1] * cmp_pos.shape[2]
    return pl.pallas_call(
        _cmpbias_kernel,
        out_shape=jax.ShapeDtypeStruct((2, CMP_HID), F32),
        compiler_params=pltpu.CompilerParams(vmem_limit_bytes=VMEM_LIMIT),
        name="cmpbias",
    )(cmp_pos.reshape(2, n, 1), cmp_w1.reshape(2, n, CMP_HID), cmp_b1)


def _cmp_kernel(pt_ref, *refs, ppt, nsub, tiles):
    n_in = nsub * ppt + 1
    all_pages = refs[:n_in]
    if tiles:
        unfold_ref = refs[n_in]
        refs = refs[1:]
    w1_ref, b1_ref, w2_ref, b2_ref, cos_ref, sin_ref, ck_ref, cv_ref, lhs_all, pbuf = refs[n_in:]
    r = ppt * SUBLANES
    for u in range(nsub):
        _cmp_unfold(all_pages[u * ppt:u * ppt + ppt + 1], unfold_ref if tiles else None, lhs_all.at[u], ppt, tiles)
    for u in range(nsub):
        rows = slice(u * r, (u + 1) * r)
        _cmp_mlp(lhs_all.at[u], pbuf, w1_ref, b1_ref, w2_ref, b2_ref, cos_ref[rows, :], sin_ref[rows, :],
                 ck_ref, cv_ref, rows, r)


def _cmp_unfold(pages, unfold_ref, lhs, ppt, tiles):
    r = ppt * SUBLANES
    rk = r + SUBLANES
    low = lax.broadcasted_iota(I32, (SUBLANES, LANES), 1) < HD

    def tap_tile(j, c, s, y):
        if tiles:
            return y[s * SUBLANES:(s + 1) * SUBLANES, c * KV_W:(c + 1) * KV_W]
        return pages[j][0, :, s * 2 * KV_W + c * KV_W:s * 2 * KV_W + (c + 1) * KV_W]

    for j in range(ppt + 1):
        y = None
        if tiles:
            a = pages[j][...].reshape(2 * KV_W, PAGE).astype(BF16)
            y = lax.dot_general(unfold_ref[...], a, _NT, preferred_element_type=F32)
        for c in range(2):
            for sp in range(CMP_STRIDE // 2):
                t0 = tap_tile(j, c, 2 * sp, y)
                t1 = tap_tile(j, c, 2 * sp + 1, y)
                lhs[c, j * SUBLANES:(j + 1) * SUBLANES, sp * LANES:(sp + 1) * LANES] = (
                    jnp.where(low, t0, pltpu.roll(t1, HD, 1)))
                lhs[c, rk + j * SUBLANES:rk + (j + 1) * SUBLANES, sp * LANES:(sp + 1) * LANES] = (
                    jnp.where(low, pltpu.roll(t0, HD, 1), t1))


def _cmp_mlp(lhs, pbuf, w1_ref, b1_ref, w2_ref, b2_ref, cos, sin_s, ck_ref, cv_ref, rows, r):
    rk = r + SUBLANES
    first = _first_half_mask(r)
    for c in range(2):
        p = jnp.dot(lhs[c].astype(BF16), w1_ref[c], preferred_element_type=F32)
        hids = []
        for k in range(N_KV):
            pbuf[...] = p[k * rk:(k + 1) * rk, CMP_HID:2 * CMP_HID]
            hids.append(p[k * rk:k * rk + r, 0:CMP_HID] + pbuf[pl.ds(1, r), :])
        hid = jnp.concatenate(hids, axis=1) + b1_ref[c]
        act = jax.nn.gelu(hid)
        comp = jnp.dot(act.astype(BF16), w2_ref[c], preferred_element_type=F32) + b2_ref[c]
        if c == 0:
            comp = _rope128(comp, cos, sin_s, first)
            out = ck_ref
        else:
            out = cv_ref
        for k in range(N_KV):
            out[0, k, rows, :] = comp[:, k * HD:(k + 1) * HD]


def _cmp(pages, pt_flat, nb, n_pages, w1p, b1p, w2p, b2p, cos_c, sin_c, name, tiles):
    ppt = min(32, n_pages)
    nsub = 2 if n_pages % (2 * ppt) == 0 else 1
    pps = nsub * ppt
    n_tiles = n_pages // pps
    r = pps * SUBLANES
    n_chunk = n_pages * SUBLANES
    zeros = (0,) * (pages.ndim - 1)

    def page_map(j):
        return lambda b, t, pt: (pt[b * n_pages + t * pps + j],) + zeros

    def next_map(b, t, pt):
        return (pt[b * n_pages + jnp.minimum(t * pps + pps, n_pages - 1)],) + zeros

    page_blk = (None, 2, N_KV, HD, PAGE) if tiles else (1, SUBLANES, CHUNK_ROW)
    in_specs = [pl.BlockSpec(page_blk, page_map(j)) for j in range(pps)]
    in_specs.append(pl.BlockSpec(page_blk, next_map))
    const = lambda shp: pl.BlockSpec(shp, lambda b, t, pt: (0,) * len(shp))
    extra = []
    if tiles:
        row = np.arange(PAGE)
        tok = (row % SUBLANES) * CMP_STRIDE + row // SUBLANES
        extra = [jnp.asarray(tok[:, None] == np.arange(PAGE)[None, :], BF16)]
        in_specs.append(const((PAGE, PAGE)))
    in_specs += [const(w1p.shape), const(b1p.shape), const(w2p.shape), const(b2p.shape),
                 pl.BlockSpec((r, LANES), lambda b, t, pt: (t, 0)), pl.BlockSpec((r, LANES), lambda b, t, pt: (t, 0))]
    hm = pl.BlockSpec((1, N_KV, r, HD), lambda b, t, pt: (b, 0, t, 0))
    grid_spec = pltpu.PrefetchScalarGridSpec(
        num_scalar_prefetch=1, grid=(nb, n_tiles), in_specs=in_specs, out_specs=[hm, hm],
        scratch_shapes=[pltpu.VMEM((nsub, 2, N_KV * (ppt + 1) * SUBLANES, CMP_STRIDE * HD), F32),
                        pltpu.VMEM(((ppt + 1) * SUBLANES, CMP_HID), F32)])
    return pl.pallas_call(
        functools.partial(_cmp_kernel, ppt=ppt, nsub=nsub, tiles=tiles),
        grid_spec=grid_spec,
        out_shape=[jax.ShapeDtypeStruct((nb, N_KV, n_chunk, HD), F32)] * 2,
        compiler_params=_params(2), name=name,
    )(pt_flat, *([pages] * (pps + 1)), *extra, w1p, b1p, w2p, b2p, cos_c, sin_c)


def _softmax_rows(s, valid):
    s = jnp.where(valid, s, NEG_INF)
    m = jnp.max(s, axis=-1, keepdims=True)
    e = jnp.exp2(s - m)
    return e / jnp.sum(e, axis=-1, keepdims=True)


def _attn_p_kernel(q_ref, ck_ref, cv_ref, ks_ref, vs_ref, kw_ref, vw_ref, gate_ref, band_ref, exp_ref, o_ref,
                   *, n_cmp_pad, n_blk, kc, hg, wc):
    qb = pl.program_id(2)
    start = qb * Q_BLOCK
    tpos = start + lax.broadcasted_iota(I32, (Q_BLOCK, 1), 0)
    groups = range(QPK // hg)
    rows = hg * Q_BLOCK

    def q_of(g):
        return q_ref[0, g * hg:(g + 1) * hg].reshape(rows, HD)

    def biased(s, bias):
        width = s.shape[-1]
        return (s.reshape(hg, Q_BLOCK, width) + bias[None]).reshape(rows, width)

    ck = ck_ref[0, 0].astype(BF16)
    cv = cv_ref[0, 0].astype(BF16)
    cmp_end = (lax.broadcasted_iota(I32, (1, n_cmp_pad), 1) + 2) * CMP_STRIDE - 1
    bias_c = jnp.where(cmp_end <= tpos, 0.0, NEG_INF)
    o_c = []
    pcs = jnp.zeros((Q_BLOCK, n_cmp_pad), F32)
    for g in groups:
        s_c = biased(lax.dot_general(q_of(g), ck, _NT, preferred_element_type=F32), bias_c)
        m_c = jnp.maximum(jnp.max(s_c, axis=-1, keepdims=True), 0.5 * NEG_INF)
        e_c = jnp.exp2(s_c - m_c)
        l_c = jnp.sum(e_c, axis=-1, keepdims=True)
        p_c = e_c * (1.0 / jnp.where(l_c > 0.0, l_c, 1.0))
        o_c.append(jnp.dot(p_c.astype(BF16), cv, preferred_element_type=F32))
        for h in range(hg):
            pcs = pcs + p_c[h * Q_BLOCK:(h + 1) * Q_BLOCK]

    imp =lax.dot_general(band_ref[...], pcs, _NT, preferred_element_type=F32,
                          precision=lax.Precision.HIGHEST)
    blk = lax.broadcasted_iota(I32, (n_blk, Q_BLOCK), 0)
    tlane = start + lax.broadcasted_iota(I32, (1, Q_BLOCK), 1)
    cur = tlane // SEL_BLOCK
    causal = blk * SEL_BLOCK <= tlane
    forced = causal & ((blk == 0) | (blk == cur) | (blk == cur - 1))
    score = jnp.where(forced, FORCE_SCORE, jnp.where(causal, imp, -1.0))
    rank = jnp.zeros((n_blk, Q_BLOCK), F32)
    for bp in range(n_blk):
        other = score[bp:bp + 1, :]
        beats = (other > score) | ((other == score) & (bp < blk))
        rank = rank + beats.astype(F32)
    sel_t = (rank < float(min(N_SEL, n_blk))).astype(BF16)
    eye = (lax.broadcasted_iota(I32, (Q_BLOCK, Q_BLOCK), 0)
           == lax.broadcasted_iota(I32, (Q_BLOCK, Q_BLOCK), 1)).astype(BF16)
    sel = lax.dot_general(eye, sel_t, _NT, preferred_element_type=F32).astype(BF16)

    n_chunks = (start + Q_BLOCK + kc - 1) // kc

    def online(state, kj, vj, bias):
        out = []
        for g in groups:
            m_i, acc = state[g]
            s = biased(lax.dot_general(q_of(g), kj, _NT, preferred_element_type=F32), bias)
            m_new = jnp.maximum(m_i, jnp.max(s, axis=-1, keepdims=True))
            p = jnp.exp2(s - m_new).astype(BF16)
            out.append((m_new, jnp.exp2(m_i - m_new) * acc + jnp.dot(p, vj, preferred_element_type=F32)))
        return tuple(out)

    def step(j, state, causal_chunk):
        off = pl.multiple_of(j * kc, kc)
        mexp = jnp.dot(sel, exp_ref[j], preferred_element_type=F32)
        bias = mexp * (-NEG_INF) + NEG_INF
        if causal_chunk:
            keypos = off + lax.broadcasted_iota(I32, (1, kc), 1)
            bias = jnp.where(keypos <= tpos, bias, NEG_INF)
        return online(state, ks_ref[0, 0, pl.ds(off, kc), :], vs_ref[0, 0, pl.ds(off, kc), :], bias)

    init = tuple((jnp.full((rows, 1), NEG_INF, F32), jnp.zeros((rows, LANES), F32)) for _ in groups)
    state = lax.fori_loop(0, n_chunks - 1, lambda j, c: step(j, c, False), init)
    sel_state = step(n_chunks - 1, state, True)

    s0 = jnp.maximum(start - WINDOW, 0)
    win_state = init
    for c in range((WINDOW + Q_BLOCK) // wc):
        off = pl.multiple_of(s0 + c * wc, Q_BLOCK)
        dist = tpos - (off + lax.broadcasted_iota(I32, (1, wc), 1))
        bias_w = jnp.where((dist >= 0) & (dist <= WINDOW), 0.0, NEG_INF)
        win_state = online(win_state, kw_ref[0, 0, pl.ds(off, wc), :], vw_ref[0, 0, pl.ds(off, wc), :], bias_w)

    gt = gate_ref[...]
    for g in groups:
        acc_s = sel_state[g][1]
        acc_w = win_state[g][1]
        o_s = acc_s[:, 0:HD] * (1.0 / acc_s[:, HD:HD + 1])
        o_w = acc_w[:, 0:HD] * (1.0 / acc_w[:, HD:HD + 1])
        for hh in range(hg):
            h = g * hg + hh
            rs = slice(hh * Q_BLOCK, (hh + 1) * Q_BLOCK)
            o = (gt[:, h:h + 1] * o_c[g][rs] + gt[:, QPK + h:QPK + h + 1] * o_s[rs]
                 + gt[:, 2 * QPK + h:2 * QPK + h + 1] * o_w[rs])
            o_ref[0, :, h * HD:(h + 1) * HD] = o


ATTN_HEAD_GROUP = 4
ATTN_KEY_CHUNK = 512
ATTN_WIN_CHUNK = 640


def _attn_prompt(q_hm, ck, cv, ks, vs, kw, vw, gates, band, expand, *, nb, t):
    n_qb = t // Q_BLOCK
    n_cmp_pad = ck.shape[2]
    n_blk = band.shape[0]
    kc = expand.shape[2]
    kv_spec = lambda n, w=HD: pl.BlockSpec((1, 1, n, w), lambda b, k, i: (b, k, 0, 0))
    return pl.pallas_call(
        functools.partial(_attn_p_kernel, n_cmp_pad=n_cmp_pad, n_blk=n_blk, kc=kc, hg=ATTN_HEAD_GROUP, wc=ATTN_WIN_CHUNK),
        grid=(nb, N_KV, n_qb),
        in_specs=[pl.BlockSpec((1, QPK, Q_BLOCK, HD), lambda b, k, i: (b, k, i, 0)),
                  kv_spec(n_cmp_pad), kv_spec(n_cmp_pad), kv_spec(t), kv_spec(t, LANES), kv_spec(t), kv_spec(t, LANES),
                  pl.BlockSpec((Q_BLOCK, LANES), lambda b, k, i: (b * n_qb + i, k)),
                  pl.BlockSpec(band.shape, lambda b, k, i: (0, 0)),
                  pl.BlockSpec(expand.shape, lambda b, k, i: (0, 0, 0))],
        out_specs=pl.BlockSpec((1, Q_BLOCK, QPK * HD), lambda b, k, i: (b, i, k)),
        out_shape=jax.ShapeDtypeStruct((nb, t, ATTN_W), F32),
        compiler_params=_params(3), name="attn_prompt",
    )(q_hm, ck, cv, ks, vs, kw, vw, gates, band, expand)


def _attn_s1_kernel(q_ref, ck_ref, cv_ref, oc_ref, pcs_ref, *, n_chunk, past):
    q = q_ref[0]
    q16 = jnp.concatenate([q, jnp.zeros_like(q)], axis=0).astype(BF16)
    cmp_end = (lax.broadcasted_iota(I32, (1, n_chunk), 1) + 2) * CMP_STRIDE - 1
    valid = cmp_end <= past
    head = lax.broadcasted_iota(I32, (2 * N_HEADS, 1), 0)
    oc = jnp.zeros((2 * N_HEADS, HD), F32)
    pcs = []
    for k in range(N_KV):
        s = lax.dot_general(q16, ck_ref[0, k].astype(BF16), _NT, preferred_element_type=F32)
        p = _softmax_rows(s, valid) * valid.astype(F32)
        in_grp = (head >= k * QPK) & (head < (k + 1) * QPK)
        p = jnp.where(in_grp, p, 0.0)
        oc = oc + jnp.dot(p.astype(BF16), cv_ref[0, k].astype(BF16), preferred_element_type=F32)
        pcs.append(jnp.sum(p, axis=0, keepdims=True))
    oc_ref[0] = oc[0:N_HEADS]
    pcs_ref[0] = jnp.concatenate(pcs + [jnp.zeros((SUBLANES - N_KV, n_chunk), F32)], axis=0)


def _topk_s_kernel(pcs_ref, band_ref, idx_ref, *, n_sel_blocks, past):
    imp = jnp.dot(pcs_ref[...], band_ref[...], preferred_element_type=F32, precision=lax.Precision.HIGHEST)
    rows, nbp = imp.shape
    blk = lax.broadcasted_iota(I32, (rows, nbp), 1)
    cur = past // SEL_BLOCK
    causal = blk * SEL_BLOCK <= past
    forced = causal & ((blk == 0) | (blk == cur) | (blk == cur - 1))
    score = jnp.where(forced, FORCE_SCORE, jnp.where(causal, imp, -1.0))
    score = jnp.where(blk < n_sel_blocks, score, -2.0)
    lane = lax.broadcasted_iota(I32, (rows, LANES), 1)
    out = jnp.zeros((rows, LANES), I32)
    for r in range(min(N_SEL, n_sel_blocks)):
        m = jnp.max(score, axis=-1, keepdims=True)
        pick = jnp.min(jnp.where(score == m, blk, nbp), axis=-1, keepdims=True)
        out = jnp.where(lane == r, pick, out)
        score = jnp.where(blk == pick, -3.0, score)
    idx_ref[...] = out


def _attn_s2_kernel(pt_ref, idx_ref, *refs, n_pages, past, n_sel_blocks):
    ktiles, vtiles = refs[:N_SEL], refs[N_SEL:2 * N_SEL]
    q_ref, oc_ref, kvs_ref, wk_ref, wv_ref, kvw_ref, gate_ref, o_ref, kbuf, vbuf = refs[2 * N_SEL:]
    b = pl.program_id(0)
    k = pl.program_id(1)
    q = q_ref[0]
    q16f = jnp.concatenate([q, jnp.zeros_like(q)], axis=0)
    q16 = q16f.astype(BF16)
    head = lax.broadcasted_iota(I32, (N_HEADS, 1), 0)
    nk = N_SEL * PAGE
    lane = lax.broadcasted_iota(I32, (1, nk), 1)
    slot = lane // PAGE
    new_blk = n_sel_blocks - 1
    wb = wk_ref.shape[-1]
    wpos = past - wb + lax.broadcasted_iota(I32, (1, wb), 1)
    wdist = past - wpos
    valid_w = (wdist >= 0) & (wdist <= WINDOW) & (wpos >= 0)

    def attend(s, valid, v_t, k_new, v_new):
        s_new = jnp.sum(q16f * k_new, axis=-1, keepdims=True)
        s = jnp.where(valid, s, NEG_INF)
        m = jnp.maximum(jnp.max(s, axis=-1, keepdims=True), s_new)
        e = jnp.exp2(s - m)
        e_new = jnp.exp2(s_new - m)
        den = jnp.sum(e, axis=-1, keepdims=True) + e_new
        acc = lax.dot_general(e.astype(BF16), v_t, _NT, preferred_element_type=F32) + e_new * v_new
        return acc / den

    in_grp = (head >= k * QPK) & (head < (k + 1) * QPK)
    bvec = jnp.zeros((1, nk), I32)
    for j in range(N_SEL):
        kbuf[:, j * PAGE:(j + 1) * PAGE] = ktiles[j][...].astype(BF16)
        vbuf[:, j * PAGE:(j + 1) * PAGE] = vtiles[j][...].astype(BF16)
        bvec = jnp.where(slot == j, idx_ref[(b * N_KV + k) * LANES + j], bvec)
    tok = (bvec // 2) * PAGE + lane % PAGE
    valid = (tok // SEL_BLOCK == bvec) & (bvec < new_blk) & (tok <= past)
    s = jnp.dot(q16, kbuf[...], preferred_element_type=F32)
    o_s = attend(s, valid, vbuf[...], kvs_ref[0, pl.ds(k, 1), :], kvs_ref[0, pl.ds(N_KV + k, 1), :])
    sw = jnp.dot(q16, wk_ref[...].astype(BF16), preferred_element_type=F32)
    o_w = attend(sw, valid_w, wv_ref[...].astype(BF16), kvw_ref[0, pl.ds(k, 1), :], kvw_ref[0, pl.ds(N_KV + k, 1), :])
    g = gate_ref[0]
    part = jnp.where(in_grp, g[:, 1:2] * o_s[0:N_HEADS] + g[:, 2:3] * o_w[0:N_HEADS], 0.0)

    @pl.when(k == 0)
    def _():
        o_ref[0] = g[:, 0:1] * oc_ref[0] + part

    @pl.when(k > 0)
    def _():
        o_ref[0] = o_ref[0] + part


def _attn_sample(q3, ck, cv, band_s, sel_t, pt_flat, kvs_rows, win_t, kvw_rows, gates_hm,
                 *, nb, n_pages, past, n_sel_blocks):
    n_chunk = ck.shape[2]
    nbp = band_s.shape[1]
    oc, pcs = pl.pallas_call(
        functools.partial(_attn_s1_kernel, n_chunk=n_chunk, past=past),
        grid=(nb,),
        in_specs=[pl.BlockSpec((1, N_HEADS, HD), lambda b: (b, 0, 0)),
                  pl.BlockSpec((1, N_KV, n_chunk, HD), lambda b: (b, 0, 0, 0)),
                  pl.BlockSpec((1, N_KV, n_chunk, HD), lambda b: (b, 0, 0, 0))],
        out_specs=[pl.BlockSpec((1, N_HEADS, HD), lambda b: (b, 0, 0)),
                   pl.BlockSpec((1, SUBLANES, n_chunk), lambda b: (b, 0, 0))],
        out_shape=[jax.ShapeDtypeStruct((nb, N_HEADS, HD), F32), jax.ShapeDtypeStruct((nb, SUBLANES, n_chunk), F32)],
        compiler_params=_params(1), name="attn_sample_cmp",
    )(q3, ck, cv)
    idx = pl.pallas_call(
        functools.partial(_topk_s_kernel, n_sel_blocks=n_sel_blocks, past=past),
        out_shape=jax.ShapeDtypeStruct((nb * N_KV, LANES), I32),
        compiler_params=pltpu.CompilerParams(vmem_limit_bytes=VMEM_LIMIT), name="topk_sample",
    )(pcs[:, 0:N_KV, :].reshape(nb * N_KV, n_chunk), band_s)
    idx_flat = idx.reshape(-1)

    def tile_map(c, j):
        def f(b, k, pt, ix):
            bidx = ix[(b * N_KV + k) * LANES + j]
            return (pt[b * n_pages + jnp.minimum(bidx // 2, n_pages - 1)], c, k, 0, 0)
        return f

    tile = lambda c, j: pl.BlockSpec((None, None, None, HD, PAGE), tile_map(c, j))
    in_specs = [tile(0, j) for j in range(N_SEL)] + [tile(1, j) for j in range(N_SEL)]
    wb = win_t.shape[-1]
    per_b = lambda shp: pl.BlockSpec(shp, lambda b, k, pt, ix: (b, 0, 0))
    in_specs += [per_b((1, N_HEADS, HD)), per_b((1, N_HEADS, HD)), per_b((1, SUBLANES, HD)),
                 pl.BlockSpec((None, None, None, HD, wb), lambda b, k, pt, ix: (b, 0, k, 0, 0)),
                 pl.BlockSpec((None, None, None, HD, wb), lambda b, k, pt, ix: (b, 1, k, 0, 0)),
                 per_b((1, SUBLANES, HD)), per_b((1, N_HEADS, LANES))]
    grid_spec = pltpu.PrefetchScalarGridSpec(
        num_scalar_prefetch=2, grid=(nb, N_KV), in_specs=in_specs,
        out_specs=per_b((1, N_HEADS, HD)),
        scratch_shapes=[pltpu.VMEM((HD, N_SEL * PAGE), BF16), pltpu.VMEM((HD, N_SEL * PAGE), BF16)])
    return pl.pallas_call(
        functools.partial(_attn_s2_kernel, n_pages=n_pages, past=past, n_sel_blocks=n_sel_blocks),
        grid_spec=grid_spec,
        out_shape=jax.ShapeDtypeStruct((nb, N_HEADS, HD), F32),
        compiler_params=_params(2), name="attn_sample_sel",
    )(pt_flat, idx_flat, *([sel_t] * (2 * N_SEL)), q3, oc, kvs_rows, win_t, win_t, kvw_rows, gates_hm)


TOK_ROWS = D_MODEL // LANES


def _store_token_tiles(ref, x):
    n = x.shape[0]
    for j in range(TOK_ROWS):
        ref[pl.ds(j, n, stride=TOK_ROWS), :] = x[:, j * LANES:(j + 1) * LANES]


def _load_token_tiles(ref, lead, n):
    return jnp.concatenate([ref[lead + (pl.ds(j, n, stride=TOK_ROWS), slice(None))] for j in range(TOK_ROWS)], axis=1)


def _outp_kernel(xp_ref, convp_ref, attnp_ref, ga1p_ref, sc2p_ref, sh2p_ref,
                 xs_ref, convs_ref, attns_ref, ga1s_ref, sc2s_ref, sh2s_ref,
                 gc_ref, ga_ref, w_ref, g2_ref, wr_ref, x1_ref, hp_ref, lg_ref, *, n_prompt_tiles):
    is_p = pl.program_id(0) < n_prompt_tiles
    pick = lambda a, b: jnp.where(is_p, a, b)
    cn = _rms(pick(convp_ref[...], convs_ref[...]), gc_ref[...])
    an = _rms(pick(attnp_ref[...], attns_ref[...]), ga_ref[...])
    cat = jnp.concatenate([cn, an], axis=1).astype(BF16)
    y = jnp.dot(cat, w_ref[...], preferred_element_type=F32)
    x1 = pick(xp_ref[...], xs_ref[...]) + pick(ga1p_ref[0], ga1s_ref[0]) * y
    x1_ref[...] = x1
    hp = _rms(x1, g2_ref[...]) * (1.0 + pick(sc2p_ref[0], sc2s_ref[0])) + pick(sh2p_ref[0], sh2s_ref[0])
    _store_token_tiles(hp_ref, hp)
    hp_hi = hp.astype(BF16)
    hp_lo = (hp - hp_hi.astype(F32)).astype(BF16)
    lg_ref[...] = (jnp.dot(hp_hi, wr_ref[0], preferred_element_type=F32)
                   + (jnp.dot(hp_hi, wr_ref[1], preferred_element_type=F32)
                      + jnp.dot(hp_lo, wr_ref[0], preferred_element_type=F32)))


TOKEN_TILE = 512


def _outp(prompt, sample, g_conv, g_attn, w_out_b, g2, w_route, *, tpb):
    tm = TOKEN_TILE
    n_p = prompt[0].shape[0] // tm
    total = (n_p + 1) * tm
    last = n_p - 1
    prow = lambda w: pl.BlockSpec((tm, w), lambda i: (jnp.minimum(i, last), 0))
    srow = lambda w: pl.BlockSpec((tm, w), lambda i: (0, 0))
    pmod = pl.BlockSpec((1, 1, D_MODEL), lambda i: (jnp.minimum(i, last) // tpb, 0, 0))
    smod = pl.BlockSpec((1, tm, D_MODEL), lambda i: (0, 0, 0))
    vec = lambda w: pl.BlockSpec((1, w), lambda i: (0, 0))
    row = lambda w: pl.BlockSpec((tm, w), lambda i: (i, 0))
    in_specs = [prow(D_MODEL), prow(CONV_W), prow(ATTN_W), pmod, pmod, pmod,
                srow(D_MODEL), srow(CONV_W), srow(ATTN_W), smod, smod, smod,
                vec(CONV_W), vec(ATTN_W), pl.BlockSpec((D_MODEL, D_MODEL), lambda i: (0, 0)), vec(D_MODEL),
                pl.BlockSpec((2, D_MODEL, LANES), lambda i: (0, 0, 0))]
    return pl.pallas_call(
        functools.partial(_outp_kernel, n_prompt_tiles=n_p),
        grid=(n_p + 1,), in_specs=in_specs,
        out_specs=[row(D_MODEL), pl.BlockSpec((tm * TOK_ROWS, LANES), lambda i: (i, 0)), row(LANES)],
        out_shape=[jax.ShapeDtypeStruct((total, D_MODEL), F32), jax.ShapeDtypeStruct((total * TOK_ROWS, LANES), F32),
                   jax.ShapeDtypeStruct((total, LANES), F32)],
        compiler_params=_params(1), name="outp",
    )(*prompt, *sample, g_conv.reshape(1, -1), g_attn.reshape(1, -1), w_out_b, g2.reshape(1, -1), w_route)


def _route_kernel(lg_ref, bias_ref, tri_ref, o_ref, cnt_ref, carry, *, tm, n_valid):
    i = pl.program_id(0)

    @pl.when(i == 0)
    def _():
        carry[...] = jnp.zeros_like(carry)

    lane = lax.broadcasted_iota(I32, (tm, LANES), 1)
    rowid = i * tm + lax.broadcasted_iota(I32, (tm, 1), 0)
    live = rowid < n_valid
    lg = lg_ref[...] + bias_ref[...]
    is_g = lane < N_GROUPS
    lgg = jnp.where(is_g, lg, NEG_INF)
    gmax = jnp.max(lgg, axis=-1, keepdims=True)
    grp = jnp.min(jnp.where(is_g & (lgg == gmax), lane, LANES), axis=-1, keepdims=True)
    p_grp = 1.0 / jnp.sum(jnp.where(is_g, jnp.exp(lgg - gmax), 0.0), axis=-1, keepdims=True)
    eid = lane - N_GROUPS
    in_grp = (eid >= grp * EPG) & (eid < (grp + 1) * EPG)
    le = jnp.where(in_grp, lg, NEG_INF)
    v1 = jnp.max(le, axis=-1, keepdims=True)
    e1 = jnp.min(jnp.where(in_grp & (le == v1), eid, LANES), axis=-1, keepdims=True)
    le2 = jnp.where(eid == e1, NEG_INF, le)
    v2 = jnp.max(le2, axis=-1, keepdims=True)
    e2 = jnp.min(jnp.where(in_grp & (eid != e1) & (le2 == v2), eid, LANES), axis=-1, keepdims=True)
    ex2 = jnp.exp(v2 - v1)
    w1 = p_grp * (1.0 / (1.0 + ex2))
    w2 = p_grp * (ex2 / (1.0 + ex2))
    oh1 = ((lane == e1) & live).astype(F32)
    oh2 = ((lane == e2) & live).astype(F32)
    both = oh1 + oh2
    before = jnp.dot(tri_ref[...], both.astype(BF16), preferred_element_type=F32) + carry[0:1, :]
    r1 = jnp.sum(oh1 * before, axis=-1, keepdims=True)
    r2 = jnp.sum(oh2 * before, axis=-1, keepdims=True)
    carry[0:1, :] = carry[0:1, :] + jnp.sum(both, axis=0, keepdims=True)
    out = jnp.where(lane == 0, e1.astype(F32), 0.0)
    out = jnp.where(lane == 1, e2.astype(F32), out)
    out = jnp.where(lane == 2, w1, out)
    out = jnp.where(lane == 3, w2, out)
    out = jnp.where(lane == 4, r1, out)
    out = jnp.where(lane == 5, r2, out)
    o_ref[...] = out
    cnt_ref[...] = carry[...]


def _route(logits, bias_row, n_valid):
    total = logits.shape[0]
    tm = TOKEN_TILE
    n_tiles = total // tm
    tri =(np.arange(tm)[:, None] > np.arange(tm)[None, :]).astype(np.float32)
    return pl.pallas_call(
        functools.partial(_route_kernel, tm=tm, n_valid=n_valid),
        grid=(n_tiles,),
        in_specs=[pl.BlockSpec((tm, LANES), lambda i: (i, 0)), pl.BlockSpec((1, LANES), lambda i: (0, 0)),
                  pl.BlockSpec((tm, tm), lambda i: (0, 0))],
        out_specs=[pl.BlockSpec((tm, LANES), lambda i: (i, 0)), pl.BlockSpec((SUBLANES, LANES), lambda i: (0, 0))],
        out_shape=[jax.ShapeDtypeStruct((total, LANES), F32), jax.ShapeDtypeStruct((SUBLANES, LANES), F32)],
        scratch_shapes=[pltpu.VMEM((SUBLANES, LANES), F32)],
        compiler_params=_params(1), name="route",
    )(logits, bias_row, jnp.asarray(tri, BF16))


EXPERT_ROWS = 256
DISPATCH_TILE = 256


def _tile_copy(src, src_row, dst, dst_row, sem):
    return pltpu.make_async_copy(src.at[pl.ds(pl.multiple_of(src_row * TOK_ROWS, TOK_ROWS), TOK_ROWS), :],
                                 dst.at[pl.ds(pl.multiple_of(dst_row * TOK_ROWS, TOK_ROWS), TOK_ROWS), :], sem)


def _dispatch_kernel(zstart_ref, zcnt_ref, dest_ref, x_ref, xb_hbm, stage, zeros, sem, zsem, *, n_tiles, n_blocks):
    i = pl.program_id(0)
    tm = DISPATCH_TILE
    slot = i % 2
    blk_rows = EXPERT_ROWS * TOK_ROWS

    def tail_copy(b):
        off = pl.multiple_of(b * blk_rows, blk_rows)
        return pltpu.make_async_copy(zeros, xb_hbm.at[pl.ds(off, blk_rows), :], zsem.at[1])

    @pl.when(i == 0)
    def _():
        zeros[...] = jnp.zeros_like(zeros)
        first_tail = zstart_ref[N_EXPERTS] // EXPERT_ROWS
        for e in range(N_EXPERTS):
            def fill(r, c, e=e):
                _tile_copy(zeros, 0, xb_hbm, zstart_ref[e] + r, zsem.at[0]).start()
                return c
            lax.fori_loop(0, zcnt_ref[e], fill, 0)
        lax.fori_loop(first_tail, n_blocks, lambda b, c: (tail_copy(b).start(), c)[1], 0)
        for e in range(N_EXPERTS):
            def drain(r, c):
                _tile_copy(zeros, 0, xb_hbm, 0, zsem.at[0]).wait()
                return c
            lax.fori_loop(0, zcnt_ref[e], drain, 0)
        lax.fori_loop(first_tail, n_blocks, lambda b, c: (tail_copy(b).wait(), c)[1], 0)

    def wait_rows(s):
        for _ in range(2 * tm):
            _tile_copy(stage.at[s], 0, xb_hbm, 0, sem.at[s]).wait()

    @pl.when(i >= 2)
    def _():
        wait_rows(slot)

    stage[slot] = x_ref[...]
    for r in range(tm):
        for k in range(2):
            _tile_copy(stage.at[slot], r, xb_hbm, dest_ref[0, 0, 2 * r + k], sem.at[slot]).start(priority=k)

    @pl.when(i == n_tiles - 1)
    def _():
        wait_rows(slot)
        if n_tiles > 1:
            wait_rows(1 - slot)


def _dispatch(dest_pad, zstart, zcnt, hp_all, n_rows, n_blocks):
    tm = DISPATCH_TILE
    n_tiles = dest_pad.shape[0] // tm
    grid_spec = pltpu.PrefetchScalarGridSpec(
        num_scalar_prefetch=2, grid=(n_tiles,),
        in_specs=[pl.BlockSpec((1, 1, 2 * tm), lambda i, zs, zc: (i, 0, 0), memory_space=pltpu.SMEM),
                  pl.BlockSpec((tm * TOK_ROWS, LANES), lambda i, zs, zc: (i, 0))],
        out_specs=pl.BlockSpec(memory_space=pl.ANY),
        scratch_shapes=[pltpu.VMEM((2, tm * TOK_ROWS, LANES), F32), pltpu.VMEM((EXPERT_ROWS * TOK_ROWS, LANES), F32),
                        pltpu.SemaphoreType.DMA((2,)), pltpu.SemaphoreType.DMA((2,))])
    return pl.pallas_call(
        functools.partial(_dispatch_kernel, n_tiles=n_tiles, n_blocks=n_blocks),
        grid_spec=grid_spec,
        out_shape=jax.ShapeDtypeStruct((n_rows * TOK_ROWS, LANES), F32),
        compiler_params=_params(1), name="dispatch",
    )(zstart, zcnt, dest_pad.reshape(n_tiles, 1, 2 * tm), hp_all)


def _experts_kernel(blk_e_ref, x_ref, wg_ref, wu_ref, wd_ref, o_ref, wg_b, wu_b, wd_b):
    i = pl.program_id(0)
    changed = jnp.logical_or(i == 0, blk_e_ref[i] != blk_e_ref[jnp.maximum(i - 1, 0)])

    @pl.when(changed)
    def _():
        wg_b[...] = wg_ref[0].astype(BF16)
        wu_b[...] = wu_ref[0].astype(BF16)
        wd_b[...] = wd_ref[0].astype(BF16)

    x = _load_token_tiles(x_ref, (), EXPERT_ROWS).astype(BF16)
    g = jnp.dot(x, wg_b[...], preferred_element_type=F32)
    u = jnp.dot(x, wu_b[...], preferred_element_type=F32)
    h = (g * jax.nn.sigmoid(g)) * u
    _store_token_tiles(o_ref, jnp.dot(h.astype(BF16), wd_b[...], preferred_element_type=F32))


def _experts(blk_e, xb, w_gate, w_up, w_down, n_blocks):
    blk = pl.BlockSpec((EXPERT_ROWS * TOK_ROWS, LANES), lambda i, be: (i, 0))
    grid_spec = pltpu.PrefetchScalarGridSpec(
        num_scalar_prefetch=1, grid=(n_blocks,),
        in_specs=[blk,
                  pl.BlockSpec((1, D_MODEL, D_EXPERT), lambda i, be: (be[i], 0, 0)),
                  pl.BlockSpec((1, D_MODEL, D_EXPERT), lambda i, be: (be[i], 0, 0)),
                  pl.BlockSpec((1, D_EXPERT, D_MODEL), lambda i, be: (be[i], 0, 0))],
        out_specs=blk,
        scratch_shapes=[pltpu.VMEM((D_MODEL, D_EXPERT), BF16), pltpu.VMEM((D_MODEL, D_EXPERT), BF16),
                        pltpu.VMEM((D_EXPERT, D_MODEL), BF16)])
    return pl.pallas_call(
        _experts_kernel,
        grid_spec=grid_spec,
        out_shape=jax.ShapeDtypeStruct((n_blocks * EXPERT_ROWS * TOK_ROWS, LANES), F32),
        compiler_params=_params(1), name="experts",
    )(blk_e, xb, w_gate, w_up, w_down)


def _final_kernel(dest_first_ref, dest_next_ref, yb_hbm, x1_ref, wt_ref, gate2_ref, gf_ref, o_ref, ybuf, sem,
                  *, tm, n_tiles):
    i = pl.program_id(0)
    slot = i % 2

    def issue(dest_ref, s):
        for r in range(tm):
            for k in range(2):
                d = dest_ref[0, 0, 2 * r + k]
                src = yb_hbm.at[pl.ds(pl.multiple_of(d * TOK_ROWS, TOK_ROWS), TOK_ROWS), :]
                pltpu.make_async_copy(src, ybuf.at[s, k, pl.ds(r * TOK_ROWS, TOK_ROWS), :],
                                      sem.at[s]).start(priority=k)

    @pl.when(i == 0)
    def _():
        issue(dest_first_ref, 0)

    @pl.when(i + 1 < n_tiles)
    def _():
        issue(dest_next_ref, 1 - slot)

    for r in range(tm):
        for k in range(2):
            pltpu.make_async_copy(yb_hbm.at[pl.ds(0, TOK_ROWS), :], ybuf.at[slot, k, pl.ds(r * TOK_ROWS, TOK_ROWS), :],
                                  sem.at[slot]).wait()
    wt = wt_ref[...]
    f = wt[:, 2:3] * _load_token_tiles(ybuf, (slot, 0), tm) + wt[:, 3:4] * _load_token_tiles(ybuf, (slot, 1), tm)
    x2 = x1_ref[...] + gate2_ref[0] * f
    o_ref[...] = _rms(x2, gf_ref[...])


def _final(dest_pad, yb, x1_all, route_rows, gate2, final_g, *, rows, tpb, per_row, row0):
    tm = min(256, rows)
    n_tiles = rows // tm
    blk0 = row0 // tm
    dest3 = dest_pad.reshape(-1, 1, 2 * tm)
    idx_blk = lambda f: pl.BlockSpec((1, 1, 2 * tm), f, memory_space=pltpu.SMEM)
    mod = (pl.BlockSpec((1, tm, D_MODEL), lambda i: (0, i, 0)) if per_row
           else pl.BlockSpec((1, 1, D_MODEL), lambda i: (i // tpb, 0, 0)))
    return pl.pallas_call(
        functools.partial(_final_kernel, tm=tm, n_tiles=n_tiles),
        grid=(n_tiles,),
        in_specs=[idx_blk(lambda i: (blk0, 0, 0)),
                  idx_blk(lambda i: (blk0 + jnp.minimum(i + 1, n_tiles - 1), 0, 0)),
                  pl.BlockSpec(memory_space=pl.ANY),
                  pl.BlockSpec((tm, D_MODEL), lambda i: (blk0 + i, 0)),
                  pl.BlockSpec((tm, LANES), lambda i: (blk0 + i, 0)),
                  mod, pl.BlockSpec((1, D_MODEL), lambda i: (0, 0))],
        out_specs=pl.BlockSpec((tm, D_MODEL), lambda i: (i, 0)),
        scratch_shapes=[pltpu.VMEM((2, 2, tm * TOK_ROWS, LANES), F32), pltpu.SemaphoreType.DMA((2,))],
        out_shape=jax.ShapeDtypeStruct((rows, D_MODEL), F32),
        compiler_params=_params(1), name="final_sample" if per_row else "final_prompt",
    )(dest3, dest3, yb, x1_all, route_rows, gate2, final_g.reshape(1, -1))


def _rope_tables(pos):
    inv = ROPE_THETA ** (-jnp.arange(HALF, dtype=F32) / HALF)
    ang = pos.astype(F32)[:, None] * inv[None, :]
    cos = jnp.tile(jnp.cos(ang), (1, LANES // HALF))
    sin = jnp.sin(ang)
    sin_s = jnp.tile(jnp.concatenate([-sin, sin], axis=1), (1, LANES // HD))
    return cos, sin_s


def _pack_w_in(w_in):
    gl = w_in[:, _C_G:_C_G + 3 * N_HEADS].reshape(D_MODEL, 3, N_KV, QPK)
    gcols = []
    for k in range(N_KV):
        gk = gl[:, :, k, :].reshape(D_MODEL, 3 * QPK)
        gcols.append(jnp.pad(gk, ((0, 0), (0, LANES - 3 * QPK))))
    return jnp.concatenate([w_in[:, :_C_G]] + gcols, axis=1).astype(BF16)


def _pack_cmp_weights(cmp_w1, cmp_w2, bias, cmp_b2):
    w1 = cmp_w1.reshape(2, 2, CMP_STRIDE, HD, CMP_HID)
    eye = jnp.eye(N_KV, dtype=F32)
    w1p = w1.transpose(0, 2, 3, 1, 4).reshape(2, CMP_STRIDE * HD, 2 * CMP_HID)
    w2p =jnp.einsum('chd,pk->cphkd', cmp_w2, eye).reshape(2, N_KV * CMP_HID, KV_W)
    b1p = jnp.tile(bias, (1, N_KV)).reshape(2, 1, N_KV * CMP_HID)
    b2p = jnp.tile(cmp_b2, (1, N_KV)).reshape(2, 1, KV_W)
    return w1p.astype(BF16), b1p, w2p.astype(BF16), b2p


def _band(n_cmp_pad, n_cmp, n_blk_pad, n_blk):
    n = np.arange(n_cmp_pad)[:, None]
    b = np.arange(n_blk_pad)[None, :]
    r = SEL_BLOCK // CMP_STRIDE
    m = (n >= r * b - 1) & (n <= r * b + r - 1) & (n < n_cmp) & (b < n_blk)
    return jnp.asarray(m.astype(np.float32))


def _expand(t, kc):
    n_chunks = t // kc
    key = np.arange(t).reshape(n_chunks, 1, kc)
    blk = np.arange(t // SEL_BLOCK).reshape(1, -1, 1)
    return jnp.asarray((key // SEL_BLOCK == blk).astype(np.float32), BF16)


def kernel(x_prompt, x_sample, c_prompt, c_sample, cache_cmp_kv, cache_sel_kv, cache_win_kv, state_conv, page_table,
           ln1_g, ln2_g, w_ada, b_ada, w_in, w_conv, cmp_pos, cmp_w1, cmp_b1, cmp_w2, cmp_b2, g_out_conv, g_out_attn,
           w_out, w_route_group, b_route_group, w_route_expert, b_route_expert, w_gate, w_up, w_down, final_g):
    depth = w_in.shape[0]
    assert depth == 1, "single-layer step"
    nb, t, _ = x_prompt.shape
    ns, ts, _ = x_sample.shape
    assert ts == 1 and t % 512 == 0 and t >= WINDOW + Q_BLOCK
    n_pool = cache_cmp_kv.shape[1]
    n_pages = page_table.shape[1]
    past = n_pages * PAGE
    wb = cache_win_kv.shape[2]
    assert wb == WINDOW
    l = 0

    n_c = nb + ns
    c_all = jnp.pad(jnp.concatenate([c_prompt, c_sample], axis=0), ((0, (-n_c) % SUBLANES), (0, 0)))
    mods = _ada(c_all, w_ada[l], b_ada[l])
    sh1, sc1, ga1, sh2, sc2, ga2 = [mods[:, j * D_MODEL:(j + 1) * D_MODEL] for j in range(6)]
    pr = lambda a: a[0:nb].reshape(nb, 1, D_MODEL)
    sr = lambda a: a[nb:nb + ns].reshape(1, ns, D_MODEL)

    w_pack = _pack_w_in(w_in[l])
    wconv8 = jnp.pad(w_conv[l], ((0, SUBLANES - CONV_K), (0, 0)))
    cos_p, sin_p = _rope_tables(jnp.arange(t, dtype=I32))
    cos_s, sin_s = _rope_tables(jnp.full((1,), past, I32))
    xp2 = x_prompt.reshape(nb * t, D_MODEL)
    xs2 = x_sample.reshape(ns, D_MODEL)
    (conv_p, cst_p, q_p, kvc_p, kvc_rows_p, kvs_rows_p, kvw_rows_p, ks_p, vs_p, kw_p, vw_p, gates_p) = _proj(
        xp2, ln1_g[l], pr(sc1), pr(sh1), w_pack, wconv8, cos_p, sin_p, nb=nb, t=t, sample=False)
    (conv_s, cst_s, q_s, _, kvc_rows_s, kvs_rows_s, kvw_rows_s, _, _, _, _, gates_s) = _proj(
        xs2, ln1_g[l], sr(sc1), sr(sh1), w_pack, wconv8, cos_s, sin_s, nb=ns, t=1, sample=True,
        prev=(state_conv[l][:, 0], state_conv[l][:, 1]))

    bias = _cmpbias(cmp_pos[l], cmp_w1[l], cmp_b1[l])
    w1p, b1p, w2p, b2p = _pack_cmp_weights(cmp_w1[l], cmp_w2[l], bias, cmp_b2[l])
    pp = t // PAGE
    cos_cp, sin_cp = _rope_tables((jnp.arange(t // CMP_STRIDE, dtype=I32) + 2) * CMP_STRIDE - 1)
    ck_p, cv_p = _cmp(kvc_p.reshape(nb * pp, SUBLANES, CHUNK_ROW), jnp.arange(nb * pp, dtype=I32), nb, pp,
                      w1p, b1p, w2p, b2p, cos_cp, sin_cp, "cmp_prompt", tiles=False)
    pt_flat = page_table.reshape(-1).astype(I32)
    cos_cs, sin_cs = _rope_tables((jnp.arange(past // CMP_STRIDE, dtype=I32) + 2) * CMP_STRIDE - 1)
    to_tiles = lambda a: a.transpose(0, 2, 3, 4, 1)
    ck_s, cv_s = _cmp(to_tiles(cache_cmp_kv[l]), pt_flat, ns, n_pages,
                      w1p, b1p, w2p, b2p, cos_cs, sin_cs, "cmp_sample", tiles=True)

    n_chunk_p = t // CMP_STRIDE
    n_blk_p = t // SEL_BLOCK
    band_p = _band(n_chunk_p, n_chunk_p - 1, n_blk_p, n_blk_p)
    attn_p = _attn_prompt(q_p, ck_p, cv_p, ks_p, vs_p, kw_p, vw_p, gates_p, band_p.T, _expand(t, ATTN_KEY_CHUNK), nb=nb, t=t)

    n_chunk_s = past // CMP_STRIDE
    n_sel_s = -(-(past + 1) // SEL_BLOCK)
    nbp = -(-n_sel_s // LANES) * LANES
    band_s = _band(n_chunk_s, (past + 1) // CMP_STRIDE - 1, nbp, n_sel_s)
    q3 = q_s.reshape(N_HEADS, ns, HD).transpose(1, 0, 2).astype(F32)
    gs = gates_s.reshape(ns, N_KV, LANES)[:, :, :3 * QPK].reshape(ns, N_KV, 3, QPK)
    gates_hm = jnp.pad(gs.transpose(0, 1, 3, 2).reshape(ns, N_HEADS, 3), ((0, 0), (0, 0), (0, LANES - 3)))
    rpt = 2 * N_KV
    new_rows = lambda a: jnp.pad(a.reshape(ns, rpt, HD), ((0, 0), (0, SUBLANES - rpt), (0, 0)))
    attn_s = _attn_sample(q3, ck_s, cv_s, band_s, to_tiles(cache_sel_kv[l]), pt_flat,
                          new_rows(kvs_rows_s), to_tiles(cache_win_kv[l]), new_rows(kvw_rows_s),
                          gates_hm, nb=ns, n_pages=n_pages, past=past, n_sel_blocks=n_sel_s).reshape(ns, ATTN_W)

    total = nb * t + ns
    w_out_b = w_out[l].astype(BF16)
    w_route = jnp.pad(jnp.concatenate([w_route_group[l], w_route_expert[l]], axis=1),
                      ((0, 0), (0, LANES - N_GROUPS - N_EXPERTS)))
    w_route_hi = w_route.astype(BF16)
    w_route = jnp.stack([w_route_hi, (w_route - w_route_hi.astype(F32)).astype(BF16)])
    b_route = jnp.pad(jnp.concatenate([b_route_group[l], b_route_expert[l]]), (0, LANES - N_GROUPS - N_EXPERTS))
    tile_pad = lambda a: jnp.pad(a, ((0, TOKEN_TILE - ns), (0, 0)))
    smod = lambda a: tile_pad(a[nb:nb + ns]).reshape(1, TOKEN_TILE, D_MODEL)
    x1_all, hp_all, lg_all = _outp(
        (xp2, conv_p, attn_p.reshape(nb * t, ATTN_W), pr(ga1), pr(sc2), pr(sh2)),
        (tile_pad(xs2), tile_pad(conv_s), tile_pad(attn_s), smod(ga1), smod(sc2), smod(sh2)),
        g_out_conv[l], g_out_attn[l], w_out_b, ln2_g[l], w_route, tpb=t // TOKEN_TILE)

    route, counts = _route(lg_all, b_route.reshape(1, LANES), total)
    e = route[:total, 0:2].astype(I32)
    rank = route[:total, 4:6].astype(I32)
    cnt = counts[0, :N_EXPERTS].astype(I32)
    padded = (cnt + EXPERT_ROWS - 1) // EXPERT_ROWS * EXPERT_ROWS
    pad_end = jnp.cumsum(padded)
    pad_start = pad_end - padded
    m_slots = total * 2
    n_blocks = -(-(m_slots + N_EXPERTS * (EXPERT_ROWS - 1)) // EXPERT_ROWS)
    n_slots = n_blocks * EXPERT_ROWS
    dest = jnp.clip(pad_start[e] + rank, 0, n_slots - 1)
    blk_start = jnp.arange(n_blocks, dtype=I32) * EXPERT_ROWS
    blk_e = jnp.minimum(jnp.sum((pad_end[None, :] <= blk_start[:, None]).astype(I32), axis=1), N_EXPERTS - 1)
    n_dump = 2 * (x1_all.shape[0] - total)
    dest_pad = jnp.concatenate([dest, n_slots + jnp.arange(n_dump, dtype=I32).reshape(-1, 2)], axis=0)

    zstart = jnp.concatenate([pad_start + cnt, pad_end[-1:]])
    zcnt = jnp.concatenate([padded - cnt, jnp.zeros((1,), I32)])
    xb = _dispatch(dest_pad, zstart, zcnt, hp_all, n_slots + n_dump, n_blocks)
    yb = _experts(blk_e, xb, w_gate[l], w_up[l], w_down[l], n_blocks)
    y_p = _final(dest_pad, yb, x1_all, route, pr(ga2), final_g, rows=nb * t, tpb=t // 256, per_row=False, row0=0)
    y_s = _final(dest_pad, yb, x1_all, route, sr(ga2), final_g, rows=ns, tpb=1, per_row=True, row0=nb * t)

    kv_shape = (2, N_KV, HD)
    y_prompt = y_p.reshape(nb, t, D_MODEL)
    y_sample = y_s.reshape(ns, 1, D_MODEL)
    new_cmp_prompt = kvc_rows_p.reshape((1, nb, t) + kv_shape)
    new_cmp_sample = kvc_rows_s.reshape((1, ns, 1) + kv_shape)
    new_sel_prompt = kvs_rows_p.reshape((1, nb, t) + kv_shape)
    new_sel_sample = kvs_rows_s.reshape((1, ns, 1) + kv_shape)
    new_win_prompt = kvw_rows_p.reshape((nb, t) + kv_shape)[:, t - WINDOW:][None]
    new_win_sample = jnp.concatenate([cache_win_kv[l][:, 1:], kvw_rows_s.reshape((ns, 1) + kv_shape)], axis=1)[None]
    new_conv_prompt = cst_p[:, SUBLANES - (CONV_K - 1):][None]
    new_conv_sample = jnp.stack([state_conv[l][:, 1], cst_s], axis=1)[None]
    return (y_prompt, y_sample, new_cmp_prompt, new_cmp_sample, new_sel_prompt, new_sel_sample,
            new_win_prompt, new_win_sample, new_conv_prompt, new_conv_sample)
```

```python
import functools

import numpy as np
import jax
import jax.numpy as jnp
from jax import lax
from jax.experimental import pallas as pl
from jax.experimental.pallas import tpu as pltpu

F32 = jnp.float32
BF16 = jnp.bfloat16
I32 = jnp.int32

D_MODEL = 1024
CONV_W = 512
ATTN_W = 512
HD = 64
HALF = HD // 2
N_HEADS = 8
N_KV = 2
QPK = 4
KV_W = N_KV * HD
CONV_K = 3
PAGE = 128
CMP_STRIDE = 16
CMP_HID = 128
SEL_BLOCK = 64
N_SEL = 16
WINDOW = 512
Q_BLOCK = 128
ROPE_THETA = 10000.0
N_GROUPS = 4
EPG = 8
N_EXPERTS = 32
D_EXPERT = 512
NORM_EPS = 1e-6
NEG_INF = -1e30
FORCE_SCORE = 1e4
LANES = 128
SUBLANES = 8
CHUNK_ROW = CMP_STRIDE * 2 * KV_W
VMEM_LIMIT = 56 * 1024 * 1024

_NT = (((1,), (1,)), ((), ()))
Q_SCALE = HD ** -0.5 * 1.4426950408889634


def _params(n_axes):
    return pltpu.CompilerParams(dimension_semantics=("arbitrary",) * n_axes,
                                vmem_limit_bytes=VMEM_LIMIT)


def _rms(x, g):
    return x * lax.rsqrt(jnp.mean(x * x, axis=-1, keepdims=True) + NORM_EPS) * g


def _rope128(x, cos, sin_signed, first_half):
    xr = jnp.where(first_half, pltpu.roll(x, LANES - HALF, 1), pltpu.roll(x, HALF, 1))
    return x * cos + xr * sin_signed


def _first_half_mask(rows):
    lane = lax.broadcasted_iota(I32, (rows, LANES), 1)
    return (lane % HD) < HALF


def _ada_kernel(c_ref, w_ref, b_ref, o_ref):
    c = c_ref[...]
    s = c * jax.nn.sigmoid(c)
    o_ref[...] = jnp.dot(s.astype(BF16), w_ref[...].astype(BF16), preferred_element_type=F32) + b_ref[...]


def _ada(c_all, w_ada, b_ada):
    m, d = c_all.shape
    n = w_ada.shape[1]
    tn = 1024
    return pl.pallas_call(
        _ada_kernel,
        grid=(n // tn,),
        in_specs=[pl.BlockSpec((m, d), lambda j: (0, 0)),
                  pl.BlockSpec((d, tn), lambda j: (0, j)),
                  pl.BlockSpec((1, tn), lambda j: (0, j))],
        out_specs=pl.BlockSpec((m, tn), lambda j: (0, j)),
        out_shape=jax.ShapeDtypeStruct((m, n), F32),
        compiler_params=_params(1),
        name="ada",
    )(c_all, w_ada, b_ada.reshape(1, n))


_C_B, _C_C, _C_U, _C_Q, _C_KVC, _C_KVS, _C_KVW, _C_G, _C_END = 0, 512, 1024, 1536, 2048, 2304, 2560, 2816, 3072


def _proj_kernel(*refs, tm, tpb, sample):
    if sample:
        (x_ref, g1_ref, sc_ref, sh_ref, w_ref, wc_ref, cos_ref, sin_ref, p0_ref, p1_ref,
         conv_ref, cst_ref, q_ref, kvc_ref, kvc_il_ref, kvs_ref, kvw_ref, ks_ref, vs_ref, kw_ref, vw_ref, gate_ref,
         ilbuf) = refs
        vbuf = None
    else:
        (x_ref, g1_ref, sc_ref, sh_ref, w_ref, wc_ref, cos_ref, sin_ref,
         conv_ref, cst_ref, q_ref, kvc_ref, kvc_il_ref, kvs_ref, kvw_ref, ks_ref, vs_ref, kw_ref, vw_ref, gate_ref,
         ilbuf, vbuf) = refs
    i = pl.program_id(0)
    x = x_ref[...]
    h = _rms(x, g1_ref[...]) * (1.0 + sc_ref[0]) + sh_ref[0]
    hb = h.astype(BF16)

    zc = jnp.dot(hb, w_ref[:, _C_B:_C_Q], preferred_element_type=F32)
    b_g = zc[:, 0:CONV_W]
    v = zc[:, CONV_W:2 * CONV_W] * zc[:, 2 * CONV_W:3 * CONV_W]
    wc = wc_ref[...]
    if sample:
        y = wc[0:1] * p0_ref[...] + wc[1:2] * p1_ref[...] + wc[2:3] * v
        cst_ref[...] = v
    else:
        @pl.when(i % tpb == 0)
        def _():
            vbuf[0:SUBLANES, :] = jnp.zeros((SUBLANES, CONV_W), F32)
        vbuf[SUBLANES:SUBLANES + tm, :] = v
        y = wc[0:1] * vbuf[pl.ds(SUBLANES - 2, tm), :] + wc[1:2] * vbuf[pl.ds(SUBLANES - 1, tm), :] + wc[2:3] * v
        tail = vbuf[tm:tm + SUBLANES, :]
        cst_ref[0] = tail
        vbuf[0:SUBLANES, :] = tail
    conv_ref[...] = b_g * y

    cos = cos_ref[...]
    sin_s = sin_ref[...]
    first = _first_half_mask(tm)

    zq = jnp.dot(hb, w_ref[:, _C_Q:_C_KVC], preferred_element_type=F32)
    for gq in range(ATTN_W // LANES):
        qr = _rope128(zq[:, gq * LANES:(gq + 1) * LANES], cos, sin_s, first) * Q_SCALE
        q_ref[0, 2 * gq] = qr[:, 0:HD].astype(BF16)
        q_ref[0, 2 * gq + 1] = qr[:, HD:LANES].astype(BF16)

    def store_rows(out_ref, halves):
        for j in range(2 * N_KV):
            piece = halves[j // N_KV]
            if j % N_KV == 1:
                piece = pltpu.roll(piece, HD, 1)
            ilbuf[pl.ds(j, tm, stride=2 * N_KV), :] = piece
        out_ref[...] = ilbuf[:, 0:HD]

    zkv = jnp.dot(hb, w_ref[:, _C_KVC:_C_G], preferred_element_type=F32)
    kvc_ref[...] = zkv[:, 0:2 * KV_W]
    store_rows(kvc_il_ref, (zkv[:, 0:KV_W], zkv[:, KV_W:2 * KV_W]))
    for base, kv_ref, kh_ref, vh_ref in ((2 * KV_W, kvs_ref, ks_ref, vs_ref), (4 * KV_W, kvw_ref, kw_ref, vw_ref)):
        kr = _rope128(zkv[:, base:base + KV_W], cos, sin_s, first)
        vv = zkv[:, base + KV_W:base + 2 * KV_W]
        store_rows(kv_ref, (kr, vv))
        lane = lax.broadcasted_iota(I32, (tm, LANES), 1)
        for k in range(N_KV):
            kh_ref[0, k] = kr[:, k * HD:(k + 1) * HD].astype(BF16)
            vk = vv if k == 0 else pltpu.roll(vv, HD, 1)
            vh_ref[0, k] = jnp.where(lane < HD, vk, jnp.where(lane == HD, 1.0, 0.0)).astype(BF16)

    zg = jnp.dot(hb, w_ref[:, _C_G:_C_END], preferred_element_type=F32)
    gate_ref[...] = jax.nn.sigmoid(zg)


def _proj(x2d, g1, sc, sh, w_pack, w_conv, cos_t, sin_t, *, nb, t, sample, prev=None):
    rows = nb * t
    tm = min(512, rows) if not sample else rows
    tpb = (t // tm) if not sample else 1
    n_tiles = rows // tm
    f = lambda a: jax.ShapeDtypeStruct(a, F32)
    b = lambda a: jax.ShapeDtypeStruct(a, BF16)
    if sample:
        mod_spec = pl.BlockSpec((1, tm, D_MODEL), lambda i: (0, 0, 0))
        tab_spec = pl.BlockSpec((1, LANES), lambda i: (0, 0))
        cst_shape, cst_spec = f((rows, CONV_W)), pl.BlockSpec((tm, CONV_W), lambda i: (0, 0))
        hm = lambda i: (0, 0, i, 0)
        hb_, ht_ = 1, rows
    else:
        mod_spec = pl.BlockSpec((1, 1, D_MODEL), lambda i: (i // tpb, 0, 0))
        tab_spec = pl.BlockSpec((tm, LANES), lambda i: (i % tpb, 0))
        cst_shape, cst_spec = f((nb, SUBLANES, CONV_W)), pl.BlockSpec((1, SUBLANES, CONV_W), lambda i: (i // tpb, 0, 0))
        hm = lambda i: (i // tpb, 0, i % tpb, 0)
        hb_, ht_ = nb, t
    row = lambda w: pl.BlockSpec((tm, w), lambda i: (i, 0))
    in_specs = [row(D_MODEL), pl.BlockSpec((1, D_MODEL), lambda i: (0, 0)), mod_spec, mod_spec,
                pl.BlockSpec((D_MODEL, _C_END), lambda i: (0, 0)),
                pl.BlockSpec((SUBLANES, CONV_W), lambda i: (0, 0)), tab_spec, tab_spec]
    args = [x2d, g1.reshape(1, D_MODEL), sc, sh, w_pack, w_conv, cos_t, sin_t]
    scratch = [pltpu.VMEM((2 * N_KV * tm, LANES), F32)]
    if sample:
        in_specs += [row(CONV_W), row(CONV_W)]
        args += [prev[0], prev[1]]
    else:
        scratch.append(pltpu.VMEM((tm + SUBLANES, CONV_W), F32))
    il_rows = 2 * N_KV * rows
    il = pl.BlockSpec((2 * N_KV * tm, HD), lambda i: (i, 0))
    out_shape = [f((rows, CONV_W)), cst_shape, b((hb_, N_HEADS, ht_, HD)),
                 f((rows, 2 * KV_W)), f((il_rows, HD)), f((il_rows, HD)), f((il_rows, HD)),
                 b((hb_, N_KV, ht_, HD)), b((hb_, N_KV, ht_, LANES)), b((hb_, N_KV, ht_, HD)), b((hb_, N_KV, ht_, LANES)),
                 f((rows, 2 * LANES))]
    out_specs = [row(CONV_W), cst_spec, pl.BlockSpec((1, N_HEADS, tm, HD), hm),
                 row(2 * KV_W), il, il, il,
                 pl.BlockSpec((1, N_KV, tm, HD), hm), pl.BlockSpec((1, N_KV, tm, LANES), hm),
                 pl.BlockSpec((1, N_KV, tm, HD), hm), pl.BlockSpec((1, N_KV, tm, LANES), hm),
                 row(2 * LANES)]
    return pl.pallas_call(
        functools.partial(_proj_kernel, tm=tm, tpb=tpb, sample=sample),
        grid=(n_tiles,), in_specs=in_specs, out_specs=out_specs, out_shape=out_shape,
        scratch_shapes=scratch, compiler_params=_params(1),
        name="proj_sample" if sample else "proj_prompt",
    )(*args)


def _cmpbias_kernel(pos_ref, w_ref, b1_ref, o_ref):
    for c in range(2):
        o_ref[c:c + 1, :] = jnp.sum(pos_ref[c] * w_ref[c], axis=0, keepdims=True) + b1_ref[c:c + 1, :]


def _cmpbias(cmp_pos, cmp_w1, cmp_b1):
    n = cmp_pos.shape[1] * cmp_pos.shape[2]
    return pl.pallas_call(
        _cmpbias_kernel,
        out_shape=jax.ShapeDtypeStruct((2, CMP_HID), F32),
        compiler_params=pltpu.CompilerParams(vmem_limit_bytes=VMEM_LIMIT),
        name="cmpbias",
    )(cmp_pos.reshape(2, n, 1), cmp_w1.reshape(2, n, CMP_HID), cmp_b1)


def _cmp_kernel(pt_ref, *refs, ppt, nsub, tiles):
    n_in = nsub * ppt + 1
    all_pages = refs[:n_in]
    if tiles:
        unfold_ref = refs[n_in]
        refs = refs[1:]
    w1_ref, b1_ref, w2_ref, b2_ref, cos_ref, sin_ref, ck_ref, cv_ref, lhs_all, pbuf = refs[n_in:]
    r = ppt * SUBLANES
    for u in range(nsub):
        _cmp_unfold(all_pages[u * ppt:u * ppt + ppt + 1], unfold_ref if tiles else None, lhs_all.at[u], ppt, tiles)
    for u in range(nsub):
        rows = slice(u * r, (u + 1) * r)
        _cmp_mlp(lhs_all.at[u], pbuf, w1_ref, b1_ref, w2_ref, b2_ref, cos_ref[rows, :], sin_ref[rows, :],
                 ck_ref, cv_ref, rows, r)


def _cmp_unfold(pages, unfold_ref, lhs, ppt, tiles):
    r = ppt * SUBLANES
    rk = r + SUBLANES
    low = lax.broadcasted_iota(I32, (SUBLANES, LANES), 1) < HD

    def tap_tile(j, c, s, y):
        if tiles:
            return y[s * SUBLANES:(s + 1) * SUBLANES, c * KV_W:(c + 1) * KV_W]
        return pages[j][0, :, s * 2 * KV_W + c * KV_W:s * 2 * KV_W + (c + 1) * KV_W]

    for j in range(ppt + 1):
        y = None
        if tiles:
            a = pages[j][...].reshape(2 * KV_W, PAGE).astype(BF16)
            y = lax.dot_general(unfold_ref[...], a, _NT, preferred_element_type=F32)
        for c in range(2):
            for sp in range(CMP_STRIDE // 2):
                t0 = tap_tile(j, c, 2 * sp, y)
                t1 = tap_tile(j, c, 2 * sp + 1, y)
                lhs[c, j * SUBLANES:(j + 1) * SUBLANES, sp * LANES:(sp + 1) * LANES] = (
                    jnp.where(low, t0, pltpu.roll(t1, HD, 1)))
                lhs[c, rk + j * SUBLANES:rk + (j + 1) * SUBLANES, sp * LANES:(sp + 1) * LANES] = (
                    jnp.where(low, pltpu.roll(t0, HD, 1), t1))


def _cmp_mlp(lhs, pbuf, w1_ref, b1_ref, w2_ref, b2_ref, cos, sin_s, ck_ref, cv_ref, rows, r):
    rk = r + SUBLANES
    first = _first_half_mask(r)
    for c in range(2):
        p = jnp.dot(lhs[c].astype(BF16), w1_ref[c], preferred_element_type=F32)
        hids = []
        for k in range(N_KV):
            pbuf[...] = p[k * rk:(k + 1) * rk, CMP_HID:2 * CMP_HID]
            hids.append(p[k * rk:k * rk + r, 0:CMP_HID] + pbuf[pl.ds(1, r), :])
        hid = jnp.concatenate(hids, axis=1) + b1_ref[c]
        act = jax.nn.gelu(hid)
        comp = jnp.dot(act.astype(BF16), w2_ref[c], preferred_element_type=F32) + b2_ref[c]
        if c == 0:
            comp = _rope128(comp, cos, sin_s, first)
            out = ck_ref
        else:
            out = cv_ref
        for k in range(N_KV):
            out[0, k, rows, :] = comp[:, k * HD:(k + 1) * HD]


def _cmp(pages, pt_flat, nb, n_pages, w1p, b1p, w2p, b2p, cos_c, sin_c, name, tiles):
    ppt = min(32, n_pages)
    nsub = 2 if n_pages % (2 * ppt) == 0 else 1
    pps = nsub * ppt
    n_tiles = n_pages // pps
    r = pps * SUBLANES
    n_chunk = n_pages * SUBLANES
    zeros = (0,) * (pages.ndim - 1)

    def page_map(j):
        return lambda b, t, pt: (pt[b * n_pages + t * pps + j],) + zeros

    def next_map(b, t, pt):
        return (pt[b * n_pages + jnp.minimum(t * pps + pps, n_pages - 1)],) + zeros

    page_blk = (None, 2, N_KV, HD, PAGE) if tiles else (1, SUBLANES, CHUNK_ROW)
    in_specs = [pl.BlockSpec(page_blk, page_map(j)) for j in range(pps)]
    in_specs.append(pl.BlockSpec(page_blk, next_map))
    const = lambda shp: pl.BlockSpec(shp, lambda b, t, pt: (0,) * len(shp))
    extra = []
    if tiles:
        row = np.arange(PAGE)
        tok = (row % SUBLANES) * CMP_STRIDE + row // SUBLANES
        extra = [jnp.asarray(tok[:, None] == np.arange(PAGE)[None, :], BF16)]
        in_specs.append(const((PAGE, PAGE)))
    in_specs += [const(w1p.shape), const(b1p.shape), const(w2p.shape), const(b2p.shape),
                 pl.BlockSpec((r, LANES), lambda b, t, pt: (t, 0)), pl.BlockSpec((r, LANES), lambda b, t, pt: (t, 0))]
    hm = pl.BlockSpec((1, N_KV, r, HD), lambda b, t, pt: (b, 0, t, 0))
    grid_spec = pltpu.PrefetchScalarGridSpec(
        num_scalar_prefetch=1, grid=(nb, n_tiles), in_specs=in_specs, out_specs=[hm, hm],
        scratch_shapes=[pltpu.VMEM((nsub, 2, N_KV * (ppt + 1) * SUBLANES, CMP_STRIDE * HD), F32),
                        pltpu.VMEM(((ppt + 1) * SUBLANES, CMP_HID), F32)])
    return pl.pallas_call(
        functools.partial(_cmp_kernel, ppt=ppt, nsub=nsub, tiles=tiles),
        grid_spec=grid_spec,
        out_shape=[jax.ShapeDtypeStruct((nb, N_KV, n_chunk, HD), F32)] * 2,
        compiler_params=_params(2), name=name,
    )(pt_flat, *([pages] * (pps + 1)), *extra, w1p, b1p, w2p, b2p, cos_c, sin_c)


def _softmax_rows(s, valid):
    s = jnp.where(valid, s, NEG_INF)
    m = jnp.max(s, axis=-1, keepdims=True)
    e = jnp.exp2(s - m)
    return e / jnp.sum(e, axis=-1, keepdims=True)


def _attn_p_kernel(q_ref, ck_ref, cv_ref, ks_ref, vs_ref, kw_ref, vw_ref, gate_ref, band_ref, exp_ref, o_ref,
                   *, n_cmp_pad, n_blk, kc, hg, wc):
    qb = pl.program_id(2)
    start = qb * Q_BLOCK
    tpos = start + lax.broadcasted_iota(I32, (Q_BLOCK, 1), 0)
    groups = range(QPK // hg)
    rows = hg * Q_BLOCK

    def q_of(g):
        return q_ref[0, g * hg:(g + 1) * hg].reshape(rows, HD)

    def biased(s, bias):
        width = s.shape[-1]
        return (s.reshape(hg, Q_BLOCK, width) + bias[None]).reshape(rows, width)

    ck = ck_ref[0, 0].astype(BF16)
    cv = cv_ref[0, 0].astype(BF16)
    cmp_end = (lax.broadcasted_iota(I32, (1, n_cmp_pad), 1) + 2) * CMP_STRIDE - 1
    bias_c = jnp.where(cmp_end <= tpos, 0.0, NEG_INF)
    o_c = []
    pcs = jnp.zeros((Q_BLOCK, n_cmp_pad), F32)
    for g in groups:
        s_c = biased(lax.dot_general(q_of(g), ck, _NT, preferred_element_type=F32), bias_c)
        m_c = jnp.maximum(jnp.max(s_c, axis=-1, keepdims=True), 0.5 * NEG_INF)
        e_c = jnp.exp2(s_c - m_c)
        l_c = jnp.sum(e_c, axis=-1, keepdims=True)
        p_c = e_c * (1.0 / jnp.where(l_c > 0.0, l_c, 1.0))
        o_c.append(jnp.dot(p_c.astype(BF16), cv, preferred_element_type=F32))
        for h in range(hg):
            pcs = pcs + p_c[h * Q_BLOCK:(h + 1) * Q_BLOCK]

    imp =lax.dot_general(band_ref[...], pcs, _NT, preferred_element_type=F32,
                          precision=lax.Precision.HIGHEST)
    blk = lax.broadcasted_iota(I32, (n_blk, Q_BLOCK), 0)
    tlane = start + lax.broadcasted_iota(I32, (1, Q_BLOCK), 1)
    cur = tlane // SEL_BLOCK
    causal = blk * SEL_BLOCK <= tlane
    forced = causal & ((blk == 0) | (blk == cur) | (blk == cur - 1))
    score = jnp.where(forced, FORCE_SCORE, jnp.where(causal, imp, -1.0))
    rank = jnp.zeros((n_blk, Q_BLOCK), F32)
    for bp in range(n_blk):
        other = score[bp:bp + 1, :]
        beats = (other > score) | ((other == score) & (bp < blk))
        rank = rank + beats.astype(F32)
    sel_t = (rank < float(min(N_SEL, n_blk))).astype(BF16)
    eye = (lax.broadcasted_iota(I32, (Q_BLOCK, Q_BLOCK), 0)
           == lax.broadcasted_iota(I32, (Q_BLOCK, Q_BLOCK), 1)).astype(BF16)
    sel = lax.dot_general(eye, sel_t, _NT, preferred_element_type=F32).astype(BF16)

    n_chunks = (start + Q_BLOCK + kc - 1) // kc

    def online(state, kj, vj, bias):
        out = []
        for g in groups:
            m_i, acc = state[g]
            s = biased(lax.dot_general(q_of(g), kj, _NT, preferred_element_type=F32), bias)
            m_new = jnp.maximum(m_i, jnp.max(s, axis=-1, keepdims=True))
            p = jnp.exp2(s - m_new).astype(BF16)
            out.append((m_new, jnp.exp2(m_i - m_new) * acc + jnp.dot(p, vj, preferred_element_type=F32)))
        return tuple(out)

    def step(j, state, causal_chunk):
        off = pl.multiple_of(j * kc, kc)
        mexp = jnp.dot(sel, exp_ref[j], preferred_element_type=F32)
        bias = mexp * (-NEG_INF) + NEG_INF
        if causal_chunk:
            keypos = off + lax.broadcasted_iota(I32, (1, kc), 1)
            bias = jnp.where(keypos <= tpos, bias, NEG_INF)
        return online(state, ks_ref[0, 0, pl.ds(off, kc), :], vs_ref[0, 0, pl.ds(off, kc), :], bias)

    init = tuple((jnp.full((rows, 1), NEG_INF, F32), jnp.zeros((rows, LANES), F32)) for _ in groups)
    state = lax.fori_loop(0, n_chunks - 1, lambda j, c: step(j, c, False), init)
    sel_state = step(n_chunks - 1, state, True)

    s0 = jnp.maximum(start - WINDOW, 0)
    win_state = init
    for c in range((WINDOW + Q_BLOCK) // wc):
        off = pl.multiple_of(s0 + c * wc, Q_BLOCK)
        dist = tpos - (off + lax.broadcasted_iota(I32, (1, wc), 1))
        bias_w = jnp.where((dist >= 0) & (dist <= WINDOW), 0.0, NEG_INF)
        win_state = online(win_state, kw_ref[0, 0, pl.ds(off, wc), :], vw_ref[0, 0, pl.ds(off, wc), :], bias_w)

    gt = gate_ref[...]
    for g in groups:
        acc_s = sel_state[g][1]
        acc_w = win_state[g][1]
        o_s = acc_s[:, 0:HD] * (1.0 / acc_s[:, HD:HD + 1])
        o_w = acc_w[:, 0:HD] * (1.0 / acc_w[:, HD:HD + 1])
        for hh in range(hg):
            h = g * hg + hh
            rs = slice(hh * Q_BLOCK, (hh + 1) * Q_BLOCK)
            o = (gt[:, h:h + 1] * o_c[g][rs] + gt[:, QPK + h:QPK + h + 1] * o_s[rs]
                 + gt[:, 2 * QPK + h:2 * QPK + h + 1] * o_w[rs])
            o_ref[0, :, h * HD:(h + 1) * HD] = o


ATTN_HEAD_GROUP = 4
ATTN_KEY_CHUNK = 512
ATTN_WIN_CHUNK = 640


def _attn_prompt(q_hm, ck, cv, ks, vs, kw, vw, gates, band, expand, *, nb, t):
    n_qb = t // Q_BLOCK
    n_cmp_pad = ck.shape[2]
    n_blk = band.shape[0]
    kc = expand.shape[2]
    kv_spec = lambda n, w=HD: pl.BlockSpec((1, 1, n, w), lambda b, k, i: (b, k, 0, 0))
    return pl.pallas_call(
        functools.partial(_attn_p_kernel, n_cmp_pad=n_cmp_pad, n_blk=n_blk, kc=kc, hg=ATTN_HEAD_GROUP, wc=ATTN_WIN_CHUNK),
        grid=(nb, N_KV, n_qb),
        in_specs=[pl.BlockSpec((1, QPK, Q_BLOCK, HD), lambda b, k, i: (b, k, i, 0)),
                  kv_spec(n_cmp_pad), kv_spec(n_cmp_pad), kv_spec(t), kv_spec(t, LANES), kv_spec(t), kv_spec(t, LANES),
                  pl.BlockSpec((Q_BLOCK, LANES), lambda b, k, i: (b * n_qb + i, k)),
                  pl.BlockSpec(band.shape, lambda b, k, i: (0, 0)),
                  pl.BlockSpec(expand.shape, lambda b, k, i: (0, 0, 0))],
        out_specs=pl.BlockSpec((1, Q_BLOCK, QPK * HD), lambda b, k, i: (b, i, k)),
        out_shape=jax.ShapeDtypeStruct((nb, t, ATTN_W), F32),
        compiler_params=_params(3), name="attn_prompt",
    )(q_hm, ck, cv, ks, vs, kw, vw, gates, band, expand)


def _attn_s1_kernel(q_ref, ck_ref, cv_ref, oc_ref, pcs_ref, *, n_chunk, past):
    q = q_ref[0]
    q16 = jnp.concatenate([q, jnp.zeros_like(q)], axis=0).astype(BF16)
    cmp_end = (lax.broadcasted_iota(I32, (1, n_chunk), 1) + 2) * CMP_STRIDE - 1
    valid = cmp_end <= past
    head = lax.broadcasted_iota(I32, (2 * N_HEADS, 1), 0)
    oc = jnp.zeros((2 * N_HEADS, HD), F32)
    pcs = []
    for k in range(N_KV):
        s = lax.dot_general(q16, ck_ref[0, k].astype(BF16), _NT, preferred_element_type=F32)
        p = _softmax_rows(s, valid) * valid.astype(F32)
        in_grp = (head >= k * QPK) & (head < (k + 1) * QPK)
        p = jnp.where(in_grp, p, 0.0)
        oc = oc + jnp.dot(p.astype(BF16), cv_ref[0, k].astype(BF16), preferred_element_type=F32)
        pcs.append(jnp.sum(p, axis=0, keepdims=True))
    oc_ref[0] = oc[0:N_HEADS]
    pcs_ref[0] = jnp.concatenate(pcs + [jnp.zeros((SUBLANES - N_KV, n_chunk), F32)], axis=0)


def _topk_s_kernel(pcs_ref, band_ref, idx_ref, *, n_sel_blocks, past):
    imp = jnp.dot(pcs_ref[...], band_ref[...], preferred_element_type=F32, precision=lax.Precision.HIGHEST)
    rows, nbp = imp.shape
    blk = lax.broadcasted_iota(I32, (rows, nbp), 1)
    cur = past // SEL_BLOCK
    causal = blk * SEL_BLOCK <= past
    forced = causal & ((blk == 0) | (blk == cur) | (blk == cur - 1))
    score = jnp.where(forced, FORCE_SCORE, jnp.where(causal, imp, -1.0))
    score = jnp.where(blk < n_sel_blocks, score, -2.0)
    lane = lax.broadcasted_iota(I32, (rows, LANES), 1)
    out = jnp.zeros((rows, LANES), I32)
    for r in range(min(N_SEL, n_sel_blocks)):
        m = jnp.max(score, axis=-1, keepdims=True)
        pick = jnp.min(jnp.where(score == m, blk, nbp), axis=-1, keepdims=True)
        out = jnp.where(lane == r, pick, out)
        score = jnp.where(blk == pick, -3.0, score)
    idx_ref[...] = out


def _attn_s2_kernel(pt_ref, idx_ref, *refs, n_pages, past, n_sel_blocks):
    ktiles, vtiles = refs[:N_SEL], refs[N_SEL:2 * N_SEL]
    q_ref, oc_ref, kvs_ref, wk_ref, wv_ref, kvw_ref, gate_ref, o_ref, kbuf, vbuf = refs[2 * N_SEL:]
    b = pl.program_id(0)
    k = pl.program_id(1)
    q = q_ref[0]
    q16f = jnp.concatenate([q, jnp.zeros_like(q)], axis=0)
    q16 = q16f.astype(BF16)
    head = lax.broadcasted_iota(I32, (N_HEADS, 1), 0)
    nk = N_SEL * PAGE
    lane = lax.broadcasted_iota(I32, (1, nk), 1)
    slot = lane // PAGE
    new_blk = n_sel_blocks - 1
    wb = wk_ref.shape[-1]
    wpos = past - wb + lax.broadcasted_iota(I32, (1, wb), 1)
    wdist = past - wpos
    valid_w = (wdist >= 0) & (wdist <= WINDOW) & (wpos >= 0)

    def attend(s, valid, v_t, k_new, v_new):
        s_new = jnp.sum(q16f * k_new, axis=-1, keepdims=True)
        s = jnp.where(valid, s, NEG_INF)
        m = jnp.maximum(jnp.max(s, axis=-1, keepdims=True), s_new)
        e = jnp.exp2(s - m)
        e_new = jnp.exp2(s_new - m)
        den = jnp.sum(e, axis=-1, keepdims=True) + e_new
        acc = lax.dot_general(e.astype(BF16), v_t, _NT, preferred_element_type=F32) + e_new * v_new
        return acc / den

    in_grp = (head >= k * QPK) & (head < (k + 1) * QPK)
    bvec = jnp.zeros((1, nk), I32)
    for j in range(N_SEL):
        kbuf[:, j * PAGE:(j + 1) * PAGE] = ktiles[j][...].astype(BF16)
        vbuf[:, j * PAGE:(j + 1) * PAGE] = vtiles[j][...].astype(BF16)
        bvec = jnp.where(slot == j, idx_ref[(b * N_KV + k) * LANES + j], bvec)
    tok = (bvec // 2) * PAGE + lane % PAGE
    valid = (tok // SEL_BLOCK == bvec) & (bvec < new_blk) & (tok <= past)
    s = jnp.dot(q16, kbuf[...], preferred_element_type=F32)
    o_s = attend(s, valid, vbuf[...], kvs_ref[0, pl.ds(k, 1), :], kvs_ref[0, pl.ds(N_KV + k, 1), :])
    sw = jnp.dot(q16, wk_ref[...].astype(BF16), preferred_element_type=F32)
    o_w = attend(sw, valid_w, wv_ref[...].astype(BF16), kvw_ref[0, pl.ds(k, 1), :], kvw_ref[0, pl.ds(N_KV + k, 1), :])
    g = gate_ref[0]
    part = jnp.where(in_grp, g[:, 1:2] * o_s[0:N_HEADS] + g[:, 2:3] * o_w[0:N_HEADS], 0.0)

    @pl.when(k == 0)
    def _():
        o_ref[0] = g[:, 0:1] * oc_ref[0] + part

    @pl.when(k > 0)
    def _():
        o_ref[0] = o_ref[0] + part


def _attn_sample(q3, ck, cv, band_s, sel_t, pt_flat, kvs_rows, win_t, kvw_rows, gates_hm,
                 *, nb, n_pages, past, n_sel_blocks):
    n_chunk = ck.shape[2]
    nbp = band_s.shape[1]
    oc, pcs = pl.pallas_call(
        functools.partial(_attn_s1_kernel, n_chunk=n_chunk, past=past),
        grid=(nb,),
        in_specs=[pl.BlockSpec((1, N_HEADS, HD), lambda b: (b, 0, 0)),
                  pl.BlockSpec((1, N_KV, n_chunk, HD), lambda b: (b, 0, 0, 0)),
                  pl.BlockSpec((1, N_KV, n_chunk, HD), lambda b: (b, 0, 0, 0))],
        out_specs=[pl.BlockSpec((1, N_HEADS, HD), lambda b: (b, 0, 0)),
                   pl.BlockSpec((1, SUBLANES, n_chunk), lambda b: (b, 0, 0))],
        out_shape=[jax.ShapeDtypeStruct((nb, N_HEADS, HD), F32), jax.ShapeDtypeStruct((nb, SUBLANES, n_chunk), F32)],
        compiler_params=_params(1), name="attn_sample_cmp",
    )(q3, ck, cv)
    idx = pl.pallas_call(
        functools.partial(_topk_s_kernel, n_sel_blocks=n_sel_blocks, past=past),
        out_shape=jax.ShapeDtypeStruct((nb * N_KV, LANES), I32),
        compiler_params=pltpu.CompilerParams(vmem_limit_bytes=VMEM_LIMIT), name="topk_sample",
    )(pcs[:, 0:N_KV, :].reshape(nb * N_KV, n_chunk), band_s)
    idx_flat = idx.reshape(-1)

    def tile_map(c, j):
        def f(b, k, pt, ix):
            bidx = ix[(b * N_KV + k) * LANES + j]
            return (pt[b * n_pages + jnp.minimum(bidx // 2, n_pages - 1)], c, k, 0, 0)
        return f

    tile = lambda c, j: pl.BlockSpec((None, None, None, HD, PAGE), tile_map(c, j))
    in_specs = [tile(0, j) for j in range(N_SEL)] + [tile(1, j) for j in range(N_SEL)]
    wb = win_t.shape[-1]
    per_b = lambda shp: pl.BlockSpec(shp, lambda b, k, pt, ix: (b, 0, 0))
    in_specs += [per_b((1, N_HEADS, HD)), per_b((1, N_HEADS, HD)), per_b((1, SUBLANES, HD)),
                 pl.BlockSpec((None, None, None, HD, wb), lambda b, k, pt, ix: (b, 0, k, 0, 0)),
                 pl.BlockSpec((None, None, None, HD, wb), lambda b, k, pt, ix: (b, 1, k, 0, 0)),
                 per_b((1, SUBLANES, HD)), per_b((1, N_HEADS, LANES))]
    grid_spec = pltpu.PrefetchScalarGridSpec(
        num_scalar_prefetch=2, grid=(nb, N_KV), in_specs=in_specs,
        out_specs=per_b((1, N_HEADS, HD)),
        scratch_shapes=[pltpu.VMEM((HD, N_SEL * PAGE), BF16), pltpu.VMEM((HD, N_SEL * PAGE), BF16)])
    return pl.pallas_call(
        functools.partial(_attn_s2_kernel, n_pages=n_pages, past=past, n_sel_blocks=n_sel_blocks),
        grid_spec=grid_spec,
        out_shape=jax.ShapeDtypeStruct((nb, N_HEADS, HD), F32),
        compiler_params=_params(2), name="attn_sample_sel",
    )(pt_flat, idx_flat, *([sel_t] * (2 * N_SEL)), q3, oc, kvs_rows, win_t, win_t, kvw_rows, gates_hm)


TOK_ROWS = D_MODEL // LANES


def _store_token_tiles(ref, x):
    n = x.shape[0]
    for j in range(TOK_ROWS):
        ref[pl.ds(j, n, stride=TOK_ROWS), :] = x[:, j * LANES:(j + 1) * LANES]


def _load_token_tiles(ref, lead, n):
    return jnp.concatenate([ref[lead + (pl.ds(j, n, stride=TOK_ROWS), slice(None))] for j in range(TOK_ROWS)], axis=1)


def _outp_kernel(xp_ref, convp_ref, attnp_ref, ga1p_ref, sc2p_ref, sh2p_ref,
                 xs_ref, convs_ref, attns_ref, ga1s_ref, sc2s_ref, sh2s_ref,
                 gc_ref, ga_ref, w_ref, g2_ref, wr_ref, x1_ref, hp_ref, lg_ref, *, n_prompt_tiles):
    is_p = pl.program_id(0) < n_prompt_tiles
    pick = lambda a, b: jnp.where(is_p, a, b)
    cn = _rms(pick(convp_ref[...], convs_ref[...]), gc_ref[...])
    an = _rms(pick(attnp_ref[...], attns_ref[...]), ga_ref[...])
    cat = jnp.concatenate([cn, an], axis=1).astype(BF16)
    y = jnp.dot(cat, w_ref[...], preferred_element_type=F32)
    x1 = pick(xp_ref[...], xs_ref[...]) + pick(ga1p_ref[0], ga1s_ref[0]) * y
    x1_ref[...] = x1
    hp = _rms(x1, g2_ref[...]) * (1.0 + pick(sc2p_ref[0], sc2s_ref[0])) + pick(sh2p_ref[0], sh2s_ref[0])
    _store_token_tiles(hp_ref, hp)
    hp_hi = hp.astype(BF16)
    hp_lo = (hp - hp_hi.astype(F32)).astype(BF16)
    lg_ref[...] = (jnp.dot(hp_hi, wr_ref[0], preferred_element_type=F32)
                   + (jnp.dot(hp_hi, wr_ref[1], preferred_element_type=F32)
                      + jnp.dot(hp_lo, wr_ref[0], preferred_element_type=F32)))


TOKEN_TILE = 512


def _outp(prompt, sample, g_conv, g_attn, w_out_b, g2, w_route, *, tpb):
    tm = TOKEN_TILE
    n_p = prompt[0].shape[0] // tm
    total = (n_p + 1) * tm
    last = n_p - 1
    prow = lambda w: pl.BlockSpec((tm, w), lambda i: (jnp.minimum(i, last), 0))
    srow = lambda w: pl.BlockSpec((tm, w), lambda i: (0, 0))
    pmod = pl.BlockSpec((1, 1, D_MODEL), lambda i: (jnp.minimum(i, last) // tpb, 0, 0))
    smod = pl.BlockSpec((1, tm, D_MODEL), lambda i: (0, 0, 0))
    vec = lambda w: pl.BlockSpec((1, w), lambda i: (0, 0))
    row = lambda w: pl.BlockSpec((tm, w), lambda i: (i, 0))
    in_specs = [prow(D_MODEL), prow(CONV_W), prow(ATTN_W), pmod, pmod, pmod,
                srow(D_MODEL), srow(CONV_W), srow(ATTN_W), smod, smod, smod,
                vec(CONV_W), vec(ATTN_W), pl.BlockSpec((D_MODEL, D_MODEL), lambda i: (0, 0)), vec(D_MODEL),
                pl.BlockSpec((2, D_MODEL, LANES), lambda i: (0, 0, 0))]
    return pl.pallas_call(
        functools.partial(_outp_kernel, n_prompt_tiles=n_p),
        grid=(n_p + 1,), in_specs=in_specs,
        out_specs=[row(D_MODEL), pl.BlockSpec((tm * TOK_ROWS, LANES), lambda i: (i, 0)), row(LANES)],
        out_shape=[jax.ShapeDtypeStruct((total, D_MODEL), F32), jax.ShapeDtypeStruct((total * TOK_ROWS, LANES), F32),
                   jax.ShapeDtypeStruct((total, LANES), F32)],
        compiler_params=_params(1), name="outp",
    )(*prompt, *sample, g_conv.reshape(1, -1), g_attn.reshape(1, -1), w_out_b, g2.reshape(1, -1), w_route)


def _route_kernel(lg_ref, bias_ref, tri_ref, o_ref, cnt_ref, carry, *, tm, n_valid):
    i = pl.program_id(0)

    @pl.when(i == 0)
    def _():
        carry[...] = jnp.zeros_like(carry)

    lane = lax.broadcasted_iota(I32, (tm, LANES), 1)
    rowid = i * tm + lax.broadcasted_iota(I32, (tm, 1), 0)
    live = rowid < n_valid
    lg = lg_ref[...] + bias_ref[...]
    is_g = lane < N_GROUPS
    lgg = jnp.where(is_g, lg, NEG_INF)
    gmax = jnp.max(lgg, axis=-1, keepdims=True)
    grp = jnp.min(jnp.where(is_g & (lgg == gmax), lane, LANES), axis=-1, keepdims=True)
    p_grp = 1.0 / jnp.sum(jnp.where(is_g, jnp.exp(lgg - gmax), 0.0), axis=-1, keepdims=True)
    eid = lane - N_GROUPS
    in_grp = (eid >= grp * EPG) & (eid < (grp + 1) * EPG)
    le = jnp.where(in_grp, lg, NEG_INF)
    v1 = jnp.max(le, axis=-1, keepdims=True)
    e1 = jnp.min(jnp.where(in_grp & (le == v1), eid, LANES), axis=-1, keepdims=True)
    le2 = jnp.where(eid == e1, NEG_INF, le)
    v2 = jnp.max(le2, axis=-1, keepdims=True)
    e2 = jnp.min(jnp.where(in_grp & (eid != e1) & (le2 == v2), eid, LANES), axis=-1, keepdims=True)
    ex2 = jnp.exp(v2 - v1)
    w1 = p_grp * (1.0 / (1.0 + ex2))
    w2 = p_grp * (ex2 / (1.0 + ex2))
    oh1 = ((lane == e1) & live).astype(F32)
    oh2 = ((lane == e2) & live).astype(F32)
    both = oh1 + oh2
    before = jnp.dot(tri_ref[...], both.astype(BF16), preferred_element_type=F32) + carry[0:1, :]
    r1 = jnp.sum(oh1 * before, axis=-1, keepdims=True)
    r2 = jnp.sum(oh2 * before, axis=-1, keepdims=True)
    carry[0:1, :] = carry[0:1, :] + jnp.sum(both, axis=0, keepdims=True)
    out = jnp.where(lane == 0, e1.astype(F32), 0.0)
    out = jnp.where(lane == 1, e2.astype(F32), out)
    out = jnp.where(lane == 2, w1, out)
    out = jnp.where(lane == 3, w2, out)
    out = jnp.where(lane == 4, r1, out)
    out = jnp.where(lane == 5, r2, out)
    o_ref[...] = out
    cnt_ref[...] = carry[...]


def _route(logits, bias_row, n_valid):
    total = logits.shape[0]
    tm = TOKEN_TILE
    n_tiles = total // tm
    tri =(np.arange(tm)[:, None] > np.arange(tm)[None, :]).astype(np.float32)
    return pl.pallas_call(
        functools.partial(_route_kernel, tm=tm, n_valid=n_valid),
        grid=(n_tiles,),
        in_specs=[pl.BlockSpec((tm, LANES), lambda i: (i, 0)), pl.BlockSpec((1, LANES), lambda i: (0, 0)),
                  pl.BlockSpec((tm, tm), lambda i: (0, 0))],
        out_specs=[pl.BlockSpec((tm, LANES), lambda i: (i, 0)), pl.BlockSpec((SUBLANES, LANES), lambda i: (0, 0))],
        out_shape=[jax.ShapeDtypeStruct((total, LANES), F32), jax.ShapeDtypeStruct((SUBLANES, LANES), F32)],
        scratch_shapes=[pltpu.VMEM((SUBLANES, LANES), F32)],
        compiler_params=_params(1), name="route",
    )(logits, bias_row, jnp.asarray(tri, BF16))


EXPERT_ROWS = 256
DISPATCH_TILE = 256


def _tile_copy(src, src_row, dst, dst_row, sem):
    return pltpu.make_async_copy(src.at[pl.ds(pl.multiple_of(src_row * TOK_ROWS, TOK_ROWS), TOK_ROWS), :],
                                 dst.at[pl.ds(pl.multiple_of(dst_row * TOK_ROWS, TOK_ROWS), TOK_ROWS), :], sem)


def _dispatch_kernel(zstart_ref, zcnt_ref, dest_ref, x_ref, xb_hbm, stage, zeros, sem, zsem, *, n_tiles, n_blocks):
    i = pl.program_id(0)
    tm = DISPATCH_TILE
    slot = i % 2
    blk_rows = EXPERT_ROWS * TOK_ROWS

    def tail_copy(b):
        off = pl.multiple_of(b * blk_rows, blk_rows)
        return pltpu.make_async_copy(zeros, xb_hbm.at[pl.ds(off, blk_rows), :], zsem.at[1])

    @pl.when(i == 0)
    def _():
        zeros[...] = jnp.zeros_like(zeros)
        first_tail = zstart_ref[N_EXPERTS] // EXPERT_ROWS
        for e in range(N_EXPERTS):
            def fill(r, c, e=e):
                _tile_copy(zeros, 0, xb_hbm, zstart_ref[e] + r, zsem.at[0]).start()
                return c
            lax.fori_loop(0, zcnt_ref[e], fill, 0)
        lax.fori_loop(first_tail, n_blocks, lambda b, c: (tail_copy(b).start(), c)[1], 0)
        for e in range(N_EXPERTS):
            def drain(r, c):
                _tile_copy(zeros, 0, xb_hbm, 0, zsem.at[0]).wait()
                return c
            lax.fori_loop(0, zcnt_ref[e], drain, 0)
        lax.fori_loop(first_tail, n_blocks, lambda b, c: (tail_copy(b).wait(), c)[1], 0)

    def wait_rows(s):
        for _ in range(2 * tm):
            _tile_copy(stage.at[s], 0, xb_hbm, 0, sem.at[s]).wait()

    @pl.when(i >= 2)
    def _():
        wait_rows(slot)

    stage[slot] = x_ref[...]
    for r in range(tm):
        for k in range(2):
            _tile_copy(stage.at[slot], r, xb_hbm, dest_ref[0, k, r], sem.at[slot]).start(priority=k)

    @pl.when(i == n_tiles - 1)
    def _():
        wait_rows(slot)
        if n_tiles > 1:
            wait_rows(1 - slot)


def _dispatch(dest_pad, zstart, zcnt, hp_all, n_rows, n_blocks):
    tm = DISPATCH_TILE
    n_tiles = dest_pad.shape[1] // tm
    grid_spec = pltpu.PrefetchScalarGridSpec(
        num_scalar_prefetch=2, grid=(n_tiles,),
        in_specs=[pl.BlockSpec((1, 2, tm), lambda i, zs, zc: (i, 0, 0), memory_space=pltpu.SMEM),
                  pl.BlockSpec((tm * TOK_ROWS, LANES), lambda i, zs, zc: (i, 0))],
        out_specs=pl.BlockSpec(memory_space=pl.ANY),
        scratch_shapes=[pltpu.VMEM((2, tm * TOK_ROWS, LANES), F32), pltpu.VMEM((EXPERT_ROWS * TOK_ROWS, LANES), F32),
                        pltpu.SemaphoreType.DMA((2,)), pltpu.SemaphoreType.DMA((2,))])
    return pl.pallas_call(
        functools.partial(_dispatch_kernel, n_tiles=n_tiles, n_blocks=n_blocks),
        grid_spec=grid_spec,
        out_shape=jax.ShapeDtypeStruct((n_rows * TOK_ROWS, LANES), F32),
        compiler_params=_params(1), name="dispatch",
    )(zstart, zcnt, dest_pad.reshape(2, n_tiles, tm).transpose(1, 0, 2), hp_all)


def _experts_kernel(blk_e_ref, x_ref, wg_ref, wu_ref, wd_ref, o_ref, wg_b, wu_b, wd_b):
    i = pl.program_id(0)
    changed = jnp.logical_or(i == 0, blk_e_ref[i] != blk_e_ref[jnp.maximum(i - 1, 0)])

    @pl.when(changed)
    def _():
        wg_b[...] = wg_ref[0].astype(BF16)
        wu_b[...] = wu_ref[0].astype(BF16)
        wd_b[...] = wd_ref[0].astype(BF16)

    x = _load_token_tiles(x_ref, (), EXPERT_ROWS).astype(BF16)
    g = jnp.dot(x, wg_b[...], preferred_element_type=F32)
    u = jnp.dot(x, wu_b[...], preferred_element_type=F32)
    h = (g * jax.nn.sigmoid(g)) * u
    _store_token_tiles(o_ref, jnp.dot(h.astype(BF16), wd_b[...], preferred_element_type=F32))


def _experts(blk_e, xb, w_gate, w_up, w_down, n_blocks):
    blk = pl.BlockSpec((EXPERT_ROWS * TOK_ROWS, LANES), lambda i, be: (i, 0))
    grid_spec = pltpu.PrefetchScalarGridSpec(
        num_scalar_prefetch=1, grid=(n_blocks,),
        in_specs=[blk,
                  pl.BlockSpec((1, D_MODEL, D_EXPERT), lambda i, be: (be[i], 0, 0)),
                  pl.BlockSpec((1, D_MODEL, D_EXPERT), lambda i, be: (be[i], 0, 0)),
                  pl.BlockSpec((1, D_EXPERT, D_MODEL), lambda i, be: (be[i], 0, 0))],
        out_specs=blk,
        scratch_shapes=[pltpu.VMEM((D_MODEL, D_EXPERT), BF16), pltpu.VMEM((D_MODEL, D_EXPERT), BF16),
                        pltpu.VMEM((D_EXPERT, D_MODEL), BF16)])
    return pl.pallas_call(
        _experts_kernel,
        grid_spec=grid_spec,
        out_shape=jax.ShapeDtypeStruct((n_blocks * EXPERT_ROWS * TOK_ROWS, LANES), F32),
        compiler_params=_params(1), name="experts",
    )(blk_e, xb, w_gate, w_up, w_down)


def _final_kernel(dest_first_ref, dest_next_ref, yb_hbm, x1_ref, wt_ref, gate2_ref, gf_ref, o_ref, ybuf, sem,
                  *, tm, n_tiles):
    i = pl.program_id(0)
    slot = i % 2

    def issue(dest_ref, s):
        for r in range(tm):
            for k in range(2):
                d = dest_ref[0, k, r]
                src = yb_hbm.at[pl.ds(pl.multiple_of(d * TOK_ROWS, TOK_ROWS), TOK_ROWS), :]
                pltpu.make_async_copy(src, ybuf.at[s, k, pl.ds(r * TOK_ROWS, TOK_ROWS), :],
                                      sem.at[s]).start(priority=k)

    @pl.when(i == 0)
    def _():
        issue(dest_first_ref, 0)

    @pl.when(i + 1 < n_tiles)
    def _():
        issue(dest_next_ref, 1 - slot)

    for r in range(tm):
        for k in range(2):
            pltpu.make_async_copy(yb_hbm.at[pl.ds(0, TOK_ROWS), :], ybuf.at[slot, k, pl.ds(r * TOK_ROWS, TOK_ROWS), :],
                                  sem.at[slot]).wait()
    wt = wt_ref[...]
    f = wt[:, 2:3] * _load_token_tiles(ybuf, (slot, 0), tm) + wt[:, 3:4] * _load_token_tiles(ybuf, (slot, 1), tm)
    x2 = x1_ref[...] + gate2_ref[0] * f
    o_ref[...] = _rms(x2, gf_ref[...])


def _final(dest_pad, yb, x1_all, route_rows, gate2, final_g, *, rows, tpb, per_row, row0):
    tm = min(256, rows)
    n_tiles = rows // tm
    blk0 = row0 // tm
    dest3 = dest_pad.reshape(2, -1, tm).transpose(1, 0, 2)
    idx_blk = lambda f: pl.BlockSpec((1, 2, tm), f, memory_space=pltpu.SMEM)
    mod = (pl.BlockSpec((1, tm, D_MODEL), lambda i: (0, i, 0)) if per_row
           else pl.BlockSpec((1, 1, D_MODEL), lambda i: (i // tpb, 0, 0)))
    return pl.pallas_call(
        functools.partial(_final_kernel, tm=tm, n_tiles=n_tiles),
        grid=(n_tiles,),
        in_specs=[idx_blk(lambda i: (blk0, 0, 0)),
                  idx_blk(lambda i: (blk0 + jnp.minimum(i + 1, n_tiles - 1), 0, 0)),
                  pl.BlockSpec(memory_space=pl.ANY),
                  pl.BlockSpec((tm, D_MODEL), lambda i: (blk0 + i, 0)),
                  pl.BlockSpec((tm, LANES), lambda i: (blk0 + i, 0)),
                  mod, pl.BlockSpec((1, D_MODEL), lambda i: (0, 0))],
        out_specs=pl.BlockSpec((tm, D_MODEL), lambda i: (i, 0)),
        scratch_shapes=[pltpu.VMEM((2, 2, tm * TOK_ROWS, LANES), F32), pltpu.SemaphoreType.DMA((2,))],
        out_shape=jax.ShapeDtypeStruct((rows, D_MODEL), F32),
        compiler_params=_params(1), name="final_sample" if per_row else "final_prompt",
    )(dest3, dest3, yb, x1_all, route_rows, gate2, final_g.reshape(1, -1))


def _rope_tables(pos):
    inv = ROPE_THETA ** (-jnp.arange(HALF, dtype=F32) / HALF)
    ang = pos.astype(F32)[:, None] * inv[None, :]
    cos = jnp.tile(jnp.cos(ang), (1, LANES // HALF))
    sin = jnp.sin(ang)
    sin_s = jnp.tile(jnp.concatenate([-sin, sin], axis=1), (1, LANES // HD))
    return cos, sin_s


def _pack_w_in(w_in):
    gl = w_in[:, _C_G:_C_G + 3 * N_HEADS].reshape(D_MODEL, 3, N_KV, QPK)
    gcols = []
    for k in range(N_KV):
        gk = gl[:, :, k, :].reshape(D_MODEL, 3 * QPK)
        gcols.append(jnp.pad(gk, ((0, 0), (0, LANES - 3 * QPK))))
    return jnp.concatenate([w_in[:, :_C_G]] + gcols, axis=1).astype(BF16)


def _pack_cmp_weights(cmp_w1, cmp_w2, bias, cmp_b2):
    w1 = cmp_w1.reshape(2, 2, CMP_STRIDE, HD, CMP_HID)
    eye = jnp.eye(N_KV, dtype=F32)
    w1p = w1.transpose(0, 2, 3, 1, 4).reshape(2, CMP_STRIDE * HD, 2 * CMP_HID)
    w2p =jnp.einsum('chd,pk->cphkd', cmp_w2, eye).reshape(2, N_KV * CMP_HID, KV_W)
    b1p = jnp.tile(bias, (1, N_KV)).reshape(2, 1, N_KV * CMP_HID)
    b2p = jnp.tile(cmp_b2, (1, N_KV)).reshape(2, 1, KV_W)
    return w1p.astype(BF16), b1p, w2p.astype(BF16), b2p


def _band(n_cmp_pad, n_cmp, n_blk_pad, n_blk):
    n = np.arange(n_cmp_pad)[:, None]
    b = np.arange(n_blk_pad)[None, :]
    r = SEL_BLOCK // CMP_STRIDE
    m = (n >= r * b - 1) & (n <= r * b + r - 1) & (n < n_cmp) & (b < n_blk)
    return jnp.asarray(m.astype(np.float32))


def _expand(t, kc):
    n_chunks = t // kc
    key = np.arange(t).reshape(n_chunks, 1, kc)
    blk = np.arange(t // SEL_BLOCK).reshape(1, -1, 1)
    return jnp.asarray((key // SEL_BLOCK == blk).astype(np.float32), BF16)


def kernel(x_prompt, x_sample, c_prompt, c_sample, cache_cmp_kv, cache_sel_kv, cache_win_kv, state_conv, page_table,
           ln1_g, ln2_g, w_ada, b_ada, w_in, w_conv, cmp_pos, cmp_w1, cmp_b1, cmp_w2, cmp_b2, g_out_conv, g_out_attn,
           w_out, w_route_group, b_route_group, w_route_expert, b_route_expert, w_gate, w_up, w_down, final_g):
    depth = w_in.shape[0]
    assert depth == 1, "single-layer step"
    nb, t, _ = x_prompt.shape
    ns, ts, _ = x_sample.shape
    assert ts == 1 and t % 512 == 0 and t >= WINDOW + Q_BLOCK
    n_pool = cache_cmp_kv.shape[1]
    n_pages = page_table.shape[1]
    past = n_pages * PAGE
    wb = cache_win_kv.shape[2]
    assert wb == WINDOW
    l = 0

    n_c = nb + ns
    c_all = jnp.pad(jnp.concatenate([c_prompt, c_sample], axis=0), ((0, (-n_c) % SUBLANES), (0, 0)))
    mods = _ada(c_all, w_ada[l], b_ada[l])
    sh1, sc1, ga1, sh2, sc2, ga2 = [mods[:, j * D_MODEL:(j + 1) * D_MODEL] for j in range(6)]
    pr = lambda a: a[0:nb].reshape(nb, 1, D_MODEL)
    sr = lambda a: a[nb:nb + ns].reshape(1, ns, D_MODEL)

    w_pack = _pack_w_in(w_in[l])
    wconv8 = jnp.pad(w_conv[l], ((0, SUBLANES - CONV_K), (0, 0)))
    cos_p, sin_p = _rope_tables(jnp.arange(t, dtype=I32))
    cos_s, sin_s = _rope_tables(jnp.full((1,), past, I32))
    xp2 = x_prompt.reshape(nb * t, D_MODEL)
    xs2 = x_sample.reshape(ns, D_MODEL)
    (conv_p, cst_p, q_p, kvc_p, kvc_rows_p, kvs_rows_p, kvw_rows_p, ks_p, vs_p, kw_p, vw_p, gates_p) = _proj(
        xp2, ln1_g[l], pr(sc1), pr(sh1), w_pack, wconv8, cos_p, sin_p, nb=nb, t=t, sample=False)
    (conv_s, cst_s, q_s, _, kvc_rows_s, kvs_rows_s, kvw_rows_s, _, _, _, _, gates_s) = _proj(
        xs2, ln1_g[l], sr(sc1), sr(sh1), w_pack, wconv8, cos_s, sin_s, nb=ns, t=1, sample=True,
        prev=(state_conv[l][:, 0], state_conv[l][:, 1]))

    bias = _cmpbias(cmp_pos[l], cmp_w1[l], cmp_b1[l])
    w1p, b1p, w2p, b2p = _pack_cmp_weights(cmp_w1[l], cmp_w2[l], bias, cmp_b2[l])
    pp = t // PAGE
    cos_cp, sin_cp = _rope_tables((jnp.arange(t // CMP_STRIDE, dtype=I32) + 2) * CMP_STRIDE - 1)
    ck_p, cv_p = _cmp(kvc_p.reshape(nb * pp, SUBLANES, CHUNK_ROW), jnp.arange(nb * pp, dtype=I32), nb, pp,
                      w1p, b1p, w2p, b2p, cos_cp, sin_cp, "cmp_prompt", tiles=False)
    pt_flat = page_table.reshape(-1).astype(I32)
    cos_cs, sin_cs = _rope_tables((jnp.arange(past // CMP_STRIDE, dtype=I32) + 2) * CMP_STRIDE - 1)
    to_tiles = lambda a: a.transpose(0, 2, 3, 4, 1)
    ck_s, cv_s = _cmp(to_tiles(cache_cmp_kv[l]), pt_flat, ns, n_pages,
                      w1p, b1p, w2p, b2p, cos_cs, sin_cs, "cmp_sample", tiles=True)

    n_chunk_p = t // CMP_STRIDE
    n_blk_p = t // SEL_BLOCK
    band_p = _band(n_chunk_p, n_chunk_p - 1, n_blk_p, n_blk_p)
    attn_p = _attn_prompt(q_p, ck_p, cv_p, ks_p, vs_p, kw_p, vw_p, gates_p, band_p.T, _expand(t, ATTN_KEY_CHUNK), nb=nb, t=t)

    n_chunk_s = past // CMP_STRIDE
    n_sel_s = -(-(past + 1) // SEL_BLOCK)
    nbp = -(-n_sel_s // LANES) * LANES
    band_s = _band(n_chunk_s, (past + 1) // CMP_STRIDE - 1, nbp, n_sel_s)
    q3 = q_s.reshape(N_HEADS, ns, HD).transpose(1, 0, 2).astype(F32)
    gs = gates_s.reshape(ns, N_KV, LANES)[:, :, :3 * QPK].reshape(ns, N_KV, 3, QPK)
    gates_hm = jnp.pad(gs.transpose(0, 1, 3, 2).reshape(ns, N_HEADS, 3), ((0, 0), (0, 0), (0, LANES - 3)))
    rpt = 2 * N_KV
    new_rows = lambda a: jnp.pad(a.reshape(ns, rpt, HD), ((0, 0), (0, SUBLANES - rpt), (0, 0)))
    attn_s = _attn_sample(q3, ck_s, cv_s, band_s, to_tiles(cache_sel_kv[l]), pt_flat,
                          new_rows(kvs_rows_s), to_tiles(cache_win_kv[l]), new_rows(kvw_rows_s),
                          gates_hm, nb=ns, n_pages=n_pages, past=past, n_sel_blocks=n_sel_s).reshape(ns, ATTN_W)

    total = nb * t + ns
    w_out_b = w_out[l].astype(BF16)
    w_route = jnp.pad(jnp.concatenate([w_route_group[l], w_route_expert[l]], axis=1),
                      ((0, 0), (0, LANES - N_GROUPS - N_EXPERTS)))
    w_route_hi = w_route.astype(BF16)
    w_route = jnp.stack([w_route_hi, (w_route - w_route_hi.astype(F32)).astype(BF16)])
    b_route = jnp.pad(jnp.concatenate([b_route_group[l], b_route_expert[l]]), (0, LANES - N_GROUPS - N_EXPERTS))
    tile_pad = lambda a: jnp.pad(a, ((0, TOKEN_TILE - ns), (0, 0)))
    smod = lambda a: tile_pad(a[nb:nb + ns]).reshape(1, TOKEN_TILE, D_MODEL)
    x1_all, hp_all, lg_all = _outp(
        (xp2, conv_p, attn_p.reshape(nb * t, ATTN_W), pr(ga1), pr(sc2), pr(sh2)),
        (tile_pad(xs2), tile_pad(conv_s), tile_pad(attn_s), smod(ga1), smod(sc2), smod(sh2)),
        g_out_conv[l], g_out_attn[l], w_out_b, ln2_g[l], w_route, tpb=t // TOKEN_TILE)

    route, counts = _route(lg_all, b_route.reshape(1, LANES), total)
    route_t = route[:total, 0:SUBLANES].T.astype(I32)
    e = route_t[0:2]
    rank = route_t[4:6]
    cnt = counts[0, :N_EXPERTS].astype(I32)
    padded = (cnt + EXPERT_ROWS - 1) // EXPERT_ROWS * EXPERT_ROWS
    pad_end = jnp.cumsum(padded)
    pad_start = pad_end - padded
    m_slots = total * 2
    n_blocks = -(-(m_slots + N_EXPERTS * (EXPERT_ROWS - 1)) // EXPERT_ROWS)
    n_slots = n_blocks * EXPERT_ROWS
    dest = jnp.clip(pad_start[e] + rank, 0, n_slots - 1)
    blk_start = jnp.arange(n_blocks, dtype=I32) * EXPERT_ROWS
    blk_e = jnp.minimum(jnp.sum((pad_end[None, :] <= blk_start[:, None]).astype(I32), axis=1), N_EXPERTS - 1)
    n_dump = 2 * (x1_all.shape[0] - total)
    dest_pad = jnp.concatenate([dest, n_slots + jnp.arange(n_dump, dtype=I32).reshape(2, -1)], axis=1)

    zstart = jnp.concatenate([pad_start + cnt, pad_end[-1:]])
    zcnt = jnp.concatenate([padded - cnt, jnp.zeros((1,), I32)])
    xb = _dispatch(dest_pad, zstart, zcnt, hp_all, n_slots + n_dump, n_blocks)
    yb = _experts(blk_e, xb, w_gate[l], w_up[l], w_down[l], n_blocks)
    y_p = _final(dest_pad, yb, x1_all, route, pr(ga2), final_g, rows=nb * t, tpb=t // 256, per_row=False, row0=0)
    y_s = _final(dest_pad, yb, x1_all, route, sr(ga2), final_g, rows=ns, tpb=1, per_row=True, row0=nb * t)

    kv_shape = (2, N_KV, HD)
    y_prompt = y_p.reshape(nb, t, D_MODEL)
    y_sample = y_s.reshape(ns, 1, D_MODEL)
    new_cmp_prompt = kvc_rows_p.reshape((1, nb, t) + kv_shape)
    new_cmp_sample = kvc_rows_s.reshape((1, ns, 1) + kv_shape)
    new_sel_prompt = kvs_rows_p.reshape((1, nb, t) + kv_shape)
    new_sel_sample = kvs_rows_s.reshape((1, ns, 1) + kv_shape)
    new_win_prompt = kvw_rows_p.reshape((nb, t) + kv_shape)[:, t - WINDOW:][None]
    new_win_sample = jnp.concatenate([cache_win_kv[l][:, 1:], kvw_rows_s.reshape((ns, 1) + kv_shape)], axis=1)[None]
    new_conv_prompt = cst_p[:, SUBLANES - (CONV_K - 1):][None]
    new_conv_sample = jnp.stack([state_conv[l][:, 1], cst_s], axis=1)[None]
    return (y_prompt, y_sample, new_cmp_prompt, new_cmp_sample, new_sel_prompt, new_sel_sample,
            new_win_prompt, new_win_sample, new_conv_prompt, new_conv_sample)
```

```python
import functools

import numpy as np
import jax
import jax.numpy as jnp
from jax import lax
from jax.experimental import pallas as pl
from jax.experimental.pallas import tpu as pltpu

F32 = jnp.float32
BF16 = jnp.bfloat16
I32 = jnp.int32

D_MODEL = 1024
CONV_W = 512
ATTN_W = 512
HD = 64
HALF = HD // 2
N_HEADS = 8
N_KV = 2
QPK = 4
KV_W = N_KV * HD
CONV_K = 3
PAGE = 128
CMP_STRIDE = 16
CMP_HID = 128
SEL_BLOCK = 64
N_SEL = 16
WINDOW = 512
Q_BLOCK = 128
ROPE_THETA = 10000.0
N_GROUPS = 4
EPG = 8
N_EXPERTS = 32
D_EXPERT = 512
NORM_EPS = 1e-6
NEG_INF = -1e30
FORCE_SCORE = 1e4
LANES = 128
SUBLANES = 8
CHUNK_ROW = CMP_STRIDE * 2 * KV_W
VMEM_LIMIT = 56 * 1024 * 1024

_NT = (((1,), (1,)), ((), ()))
Q_SCALE = HD ** -0.5 * 1.4426950408889634


def _params(n_axes):
    return pltpu.CompilerParams(dimension_semantics=("arbitrary",) * n_axes,
                                vmem_limit_bytes=VMEM_LIMIT)


def _rms(x, g):
    return x * lax.rsqrt(jnp.mean(x * x, axis=-1, keepdims=True) + NORM_EPS) * g


def _rope128(x, cos, sin_signed, first_half):
    xr = jnp.where(first_half, pltpu.roll(x, LANES - HALF, 1), pltpu.roll(x, HALF, 1))
    return x * cos + xr * sin_signed


def _first_half_mask(rows):
    lane = lax.broadcasted_iota(I32, (rows, LANES), 1)
    return (lane % HD) < HALF


def _ada_kernel(c_ref, w_ref, b_ref, o_ref):
    c = c_ref[...]
    s = c * jax.nn.sigmoid(c)
    o_ref[...] = jnp.dot(s.astype(BF16), w_ref[...].astype(BF16), preferred_element_type=F32) + b_ref[...]


def _ada(c_all, w_ada, b_ada):
    m, d = c_all.shape
    n = w_ada.shape[1]
    tn = 1024
    return pl.pallas_call(
        _ada_kernel,
        grid=(n // tn,),
        in_specs=[pl.BlockSpec((m, d), lambda j: (0, 0)),
                  pl.BlockSpec((d, tn), lambda j: (0, j)),
                  pl.BlockSpec((1, tn), lambda j: (0, j))],
        out_specs=pl.BlockSpec((m, tn), lambda j: (0, j)),
        out_shape=jax.ShapeDtypeStruct((m, n), F32),
        compiler_params=_params(1),
        name="ada",
    )(c_all, w_ada, b_ada.reshape(1, n))


_C_B, _C_C, _C_U, _C_Q, _C_KVC, _C_KVS, _C_KVW, _C_G, _C_END = 0, 512, 1024, 1536, 2048, 2304, 2560, 2816, 3072


def _proj_kernel(*refs, tm, tpb, sample):
    if sample:
        (x_ref, g1_ref, sc_ref, sh_ref, w_ref, wc_ref, cos_ref, sin_ref, p0_ref, p1_ref,
         conv_ref, cst_ref, q_ref, kvc_ref, kvc_il_ref, kvs_ref, kvw_ref, ks_ref, vs_ref, kw_ref, vw_ref, gate_ref,
         ilbuf) = refs
        vbuf = None
    else:
        (x_ref, g1_ref, sc_ref, sh_ref, w_ref, wc_ref, cos_ref, sin_ref,
         conv_ref, cst_ref, q_ref, kvc_ref, kvc_il_ref, kvs_ref, kvw_ref, ks_ref, vs_ref, kw_ref, vw_ref, gate_ref,
         ilbuf, vbuf) = refs
    i = pl.program_id(0)
    x = x_ref[...]
    h = _rms(x, g1_ref[...]) * (1.0 + sc_ref[0]) + sh_ref[0]
    hb = h.astype(BF16)

    zc = jnp.dot(hb, w_ref[:, _C_B:_C_Q], preferred_element_type=F32)
    b_g = zc[:, 0:CONV_W]
    v = zc[:, CONV_W:2 * CONV_W] * zc[:, 2 * CONV_W:3 * CONV_W]
    wc = wc_ref[...]
    if sample:
        y = wc[0:1] * p0_ref[...] + wc[1:2] * p1_ref[...] + wc[2:3] * v
        cst_ref[...] = v
    else:
        @pl.when(i % tpb == 0)
        def _():
            vbuf[0:SUBLANES, :] = jnp.zeros((SUBLANES, CONV_W), F32)
        vbuf[SUBLANES:SUBLANES + tm, :] = v
        y = wc[0:1] * vbuf[pl.ds(SUBLANES - 2, tm), :] + wc[1:2] * vbuf[pl.ds(SUBLANES - 1, tm), :] + wc[2:3] * v
        tail = vbuf[tm:tm + SUBLANES, :]
        cst_ref[0] = tail
        vbuf[0:SUBLANES, :] = tail
    conv_ref[...] = b_g * y

    cos = cos_ref[...]
    sin_s = sin_ref[...]
    first = _first_half_mask(tm)

    zq = jnp.dot(hb, w_ref[:, _C_Q:_C_KVC], preferred_element_type=F32)
    for gq in range(ATTN_W // LANES):
        qr = _rope128(zq[:, gq * LANES:(gq + 1) * LANES], cos, sin_s, first) * Q_SCALE
        q_ref[0, 2 * gq] = qr[:, 0:HD].astype(BF16)
        q_ref[0, 2 * gq + 1] = qr[:, HD:LANES].astype(BF16)

    def store_rows(out_ref, halves):
        for j in range(2 * N_KV):
            piece = halves[j // N_KV]
            if j % N_KV == 1:
                piece = pltpu.roll(piece, HD, 1)
            ilbuf[pl.ds(j, tm, stride=2 * N_KV), :] = piece
        out_ref[...] = ilbuf[:, 0:HD]

    zkv = jnp.dot(hb, w_ref[:, _C_KVC:_C_G], preferred_element_type=F32)
    kvc_ref[...] = zkv[:, 0:2 * KV_W]
    store_rows(kvc_il_ref, (zkv[:, 0:KV_W], zkv[:, KV_W:2 * KV_W]))
    for base, kv_ref, kh_ref, vh_ref in ((2 * KV_W, kvs_ref, ks_ref, vs_ref), (4 * KV_W, kvw_ref, kw_ref, vw_ref)):
        kr = _rope128(zkv[:, base:base + KV_W], cos, sin_s, first)
        vv = zkv[:, base + KV_W:base + 2 * KV_W]
        store_rows(kv_ref, (kr, vv))
        lane = lax.broadcasted_iota(I32, (tm, LANES), 1)
        for k in range(N_KV):
            kh_ref[0, k] = kr[:, k * HD:(k + 1) * HD].astype(BF16)
            vk = vv if k == 0 else pltpu.roll(vv, HD, 1)
            vh_ref[0, k] = jnp.where(lane < HD, vk, jnp.where(lane == HD, 1.0, 0.0)).astype(BF16)

    zg = jnp.dot(hb, w_ref[:, _C_G:_C_END], preferred_element_type=F32)
    gate_ref[...] = jax.nn.sigmoid(zg)


def _proj(x2d, g1, sc, sh, w_pack, w_conv, cos_t, sin_t, *, nb, t, sample, prev=None):
    rows = nb * t
    tm = min(512, rows) if not sample else rows
    tpb = (t // tm) if not sample else 1
    n_tiles = rows // tm
    f = lambda a: jax.ShapeDtypeStruct(a, F32)
    b = lambda a: jax.ShapeDtypeStruct(a, BF16)
    if sample:
        mod_spec = pl.BlockSpec((1, tm, D_MODEL), lambda i: (0, 0, 0))
        tab_spec = pl.BlockSpec((1, LANES), lambda i: (0, 0))
        cst_shape, cst_spec = f((rows, CONV_W)), pl.BlockSpec((tm, CONV_W), lambda i: (0, 0))
        hm = lambda i: (0, 0, i, 0)
        hb_, ht_ = 1, rows
    else:
        mod_spec = pl.BlockSpec((1, 1, D_MODEL), lambda i: (i // tpb, 0, 0))
        tab_spec = pl.BlockSpec((tm, LANES), lambda i: (i % tpb, 0))
        cst_shape, cst_spec = f((nb, SUBLANES, CONV_W)), pl.BlockSpec((1, SUBLANES, CONV_W), lambda i: (i // tpb, 0, 0))
        hm = lambda i: (i // tpb, 0, i % tpb, 0)
        hb_, ht_ = nb, t
    row = lambda w: pl.BlockSpec((tm, w), lambda i: (i, 0))
    in_specs = [row(D_MODEL), pl.BlockSpec((1, D_MODEL), lambda i: (0, 0)), mod_spec, mod_spec,
                pl.BlockSpec((D_MODEL, _C_END), lambda i: (0, 0)),
                pl.BlockSpec((SUBLANES, CONV_W), lambda i: (0, 0)), tab_spec, tab_spec]
    args = [x2d, g1.reshape(1, D_MODEL), sc, sh, w_pack, w_conv, cos_t, sin_t]
    scratch = [pltpu.VMEM((2 * N_KV * tm, LANES), F32)]
    if sample:
        in_specs += [row(CONV_W), row(CONV_W)]
        args += [prev[0], prev[1]]
    else:
        scratch.append(pltpu.VMEM((tm + SUBLANES, CONV_W), F32))
    il_rows = 2 * N_KV * rows
    il = pl.BlockSpec((2 * N_KV * tm, HD), lambda i: (i, 0))
    out_shape = [f((rows, CONV_W)), cst_shape, b((hb_, N_HEADS, ht_, HD)),
                 f((rows, 2 * KV_W)), f((il_rows, HD)), f((il_rows, HD)), f((il_rows, HD)),
                 b((hb_, N_KV, ht_, HD)), b((hb_, N_KV, ht_, LANES)), b((hb_, N_KV, ht_, HD)), b((hb_, N_KV, ht_, LANES)),
                 f((rows, 2 * LANES))]
    out_specs = [row(CONV_W), cst_spec, pl.BlockSpec((1, N_HEADS, tm, HD), hm),
                 row(2 * KV_W), il, il, il,
                 pl.BlockSpec((1, N_KV, tm, HD), hm), pl.BlockSpec((1, N_KV, tm, LANES), hm),
                 pl.BlockSpec((1, N_KV, tm, HD), hm), pl.BlockSpec((1, N_KV, tm, LANES), hm),
                 row(2 * LANES)]
    return pl.pallas_call(
        functools.partial(_proj_kernel, tm=tm, tpb=tpb, sample=sample),
        grid=(n_tiles,), in_specs=in_specs, out_specs=out_specs, out_shape=out_shape,
        scratch_shapes=scratch, compiler_params=_params(1),
        name="proj_sample" if sample else "proj_prompt",
    )(*args)


def _cmpbias_kernel(pos_ref, w_ref, b1_ref, o_ref):
    for c in range(2):
        o_ref[c:c + 1, :] = jnp.sum(pos_ref[c] * w_ref[c], axis=0, keepdims=True) + b1_ref[c:c + 1, :]


def _cmpbias(cmp_pos, cmp_w1, cmp_b1):
    n = cmp_pos.shape[1] * cmp_pos.shape[2]
    return pl.pallas_call(
        _cmpbias_kernel,
        out_shape=jax.ShapeDtypeStruct((2, CMP_HID), F32),
        compiler_params=pltpu.CompilerParams(vmem_limit_bytes=VMEM_LIMIT),
        name="cmpbias",
    )(cmp_pos.reshape(2, n, 1), cmp_w1.reshape(2, n, CMP_HID), cmp_b1)


def _cmp_kernel(pt_ref, *refs, ppt, nsub, tiles):
    n_in = nsub * ppt + 1
    all_pages = refs[:n_in]
    if tiles:
        unfold_ref = refs[n_in]
        refs = refs[1:]
    w1_ref, b1_ref, w2_ref, b2_ref, cos_ref, sin_ref, ck_ref, cv_ref, lhs_all, pbuf = refs[n_in:]
    r = ppt * SUBLANES
    for u in range(nsub):
        _cmp_unfold(all_pages[u * ppt:u * ppt + ppt + 1], unfold_ref if tiles else None, lhs_all.at[u], ppt, tiles)
    for u in range(nsub):
        rows = slice(u * r, (u + 1) * r)
        _cmp_mlp(lhs_all.at[u], pbuf, w1_ref, b1_ref, w2_ref, b2_ref, cos_ref[rows, :], sin_ref[rows, :],
                 ck_ref, cv_ref, rows, r)


def _cmp_unfold(pages, unfold_ref, lhs, ppt, tiles):
    r = ppt * SUBLANES
    rk = r + SUBLANES
    low = lax.broadcasted_iota(I32, (SUBLANES, LANES), 1) < HD

    def tap_tile(j, c, s, y):
        if tiles:
            return y[s * SUBLANES:(s + 1) * SUBLANES, c * KV_W:(c + 1) * KV_W]
        return pages[j][0, :, s * 2 * KV_W + c * KV_W:s * 2 * KV_W + (c + 1) * KV_W]

    for j in range(ppt + 1):
        y = None
        if tiles:
            a = pages[j][...].reshape(2 * KV_W, PAGE).astype(BF16)
            y = lax.dot_general(unfold_ref[...], a, _NT, preferred_element_type=F32)
        for c in range(2):
            for sp in range(CMP_STRIDE // 2):
                t0 = tap_tile(j, c, 2 * sp, y)
                t1 = tap_tile(j, c, 2 * sp + 1, y)
                lhs[c, j * SUBLANES:(j + 1) * SUBLANES, sp * LANES:(sp + 1) * LANES] = (
                    jnp.where(low, t0, pltpu.roll(t1, HD, 1)))
                lhs[c, rk + j * SUBLANES:rk + (j + 1) * SUBLANES, sp * LANES:(sp + 1) * LANES] = (
                    jnp.where(low, pltpu.roll(t0, HD, 1), t1))


def _cmp_mlp(lhs, pbuf, w1_ref, b1_ref, w2_ref, b2_ref, cos, sin_s, ck_ref, cv_ref, rows, r):
    rk = r + SUBLANES
    first = _first_half_mask(r)
    for c in range(2):
        p = jnp.dot(lhs[c].astype(BF16), w1_ref[c], preferred_element_type=F32)
        hids = []
        for k in range(N_KV):
            pbuf[...] = p[k * rk:(k + 1) * rk, CMP_HID:2 * CMP_HID]
            hids.append(p[k * rk:k * rk + r, 0:CMP_HID] + pbuf[pl.ds(1, r), :])
        hid = jnp.concatenate(hids, axis=1) + b1_ref[c]
        act = jax.nn.gelu(hid)
        comp = jnp.dot(act.astype(BF16), w2_ref[c], preferred_element_type=F32) + b2_ref[c]
        if c == 0:
            comp = _rope128(comp, cos, sin_s, first)
            out = ck_ref
        else:
            out = cv_ref
        for k in range(N_KV):
            out[0, k, rows, :] = comp[:, k * HD:(k + 1) * HD]


def _cmp(pages, pt_flat, nb, n_pages, w1p, b1p, w2p, b2p, cos_c, sin_c, name, tiles):
    ppt = min(32, n_pages)
    nsub = 2 if n_pages % (2 * ppt) == 0 else 1
    pps = nsub * ppt
    n_tiles = n_pages // pps
    r = pps * SUBLANES
    n_chunk = n_pages * SUBLANES
    zeros = (0,) * (pages.ndim - 1)

    def page_map(j):
        return lambda b, t, pt: (pt[b * n_pages + t * pps + j],) + zeros

    def next_map(b, t, pt):
        return (pt[b * n_pages + jnp.minimum(t * pps + pps, n_pages - 1)],) + zeros

    page_blk = (None, 2, N_KV, HD, PAGE) if tiles else (1, SUBLANES, CHUNK_ROW)
    in_specs = [pl.BlockSpec(page_blk, page_map(j)) for j in range(pps)]
    in_specs.append(pl.BlockSpec(page_blk, next_map))
    const = lambda shp: pl.BlockSpec(shp, lambda b, t, pt: (0,) * len(shp))
    extra = []
    if tiles:
        row = np.arange(PAGE)
        tok = (row % SUBLANES) * CMP_STRIDE + row // SUBLANES
        extra = [jnp.asarray(tok[:, None] == np.arange(PAGE)[None, :], BF16)]
        in_specs.append(const((PAGE, PAGE)))
    in_specs += [const(w1p.shape), const(b1p.shape), const(w2p.shape), const(b2p.shape),
                 pl.BlockSpec((r, LANES), lambda b, t, pt: (t, 0)), pl.BlockSpec((r, LANES), lambda b, t, pt: (t, 0))]
    hm = pl.BlockSpec((1, N_KV, r, HD), lambda b, t, pt: (b, 0, t, 0))
    grid_spec = pltpu.PrefetchScalarGridSpec(
        num_scalar_prefetch=1, grid=(nb, n_tiles), in_specs=in_specs, out_specs=[hm, hm],
        scratch_shapes=[pltpu.VMEM((nsub, 2, N_KV * (ppt + 1) * SUBLANES, CMP_STRIDE * HD), F32),
                        pltpu.VMEM(((ppt + 1) * SUBLANES, CMP_HID), F32)])
    return pl.pallas_call(
        functools.partial(_cmp_kernel, ppt=ppt, nsub=nsub, tiles=tiles),
        grid_spec=grid_spec,
        out_shape=[jax.ShapeDtypeStruct((nb, N_KV, n_chunk, HD), F32)] * 2,
        compiler_params=_params(2), name=name,
    )(pt_flat, *([pages] * (pps + 1)), *extra, w1p, b1p, w2p, b2p, cos_c, sin_c)


def _softmax_rows(s, valid):
    s = jnp.where(valid, s, NEG_INF)
    m = jnp.max(s, axis=-1, keepdims=True)
    e = jnp.exp2(s - m)
    return e / jnp.sum(e, axis=-1, keepdims=True)


def _attn_p_kernel(q_ref, ck_ref, cv_ref, ks_ref, vs_ref, kw_ref, vw_ref, gate_ref, band_ref, exp_ref, o_ref,
                   *, n_cmp_pad, n_blk, kc, hg, wc):
    qb = pl.program_id(2)
    start = qb * Q_BLOCK
    tpos = start + lax.broadcasted_iota(I32, (Q_BLOCK, 1), 0)
    groups = range(QPK // hg)
    rows = hg * Q_BLOCK

    def q_of(g):
        return q_ref[0, g * hg:(g + 1) * hg].reshape(rows, HD)

    def biased(s, bias):
        width = s.shape[-1]
        return (s.reshape(hg, Q_BLOCK, width) + bias[None]).reshape(rows, width)

    ck = ck_ref[0, 0].astype(BF16)
    cv = cv_ref[0, 0].astype(BF16)
    cmp_end = (lax.broadcasted_iota(I32, (1, n_cmp_pad), 1) + 2) * CMP_STRIDE - 1
    bias_c = jnp.where(cmp_end <= tpos, 0.0, NEG_INF)
    o_c = []
    pcs = jnp.zeros((Q_BLOCK, n_cmp_pad), F32)
    for g in groups:
        s_c = biased(lax.dot_general(q_of(g), ck, _NT, preferred_element_type=F32), bias_c)
        m_c = jnp.maximum(jnp.max(s_c, axis=-1, keepdims=True), 0.5 * NEG_INF)
        e_c = jnp.exp2(s_c - m_c)
        l_c = jnp.sum(e_c, axis=-1, keepdims=True)
        p_c = e_c * (1.0 / jnp.where(l_c > 0.0, l_c, 1.0))
        o_c.append(jnp.dot(p_c.astype(BF16), cv, preferred_element_type=F32))
        for h in range(hg):
            pcs = pcs + p_c[h * Q_BLOCK:(h + 1) * Q_BLOCK]

    imp =lax.dot_general(band_ref[...], pcs, _NT, preferred_element_type=F32,
                          precision=lax.Precision.HIGHEST)
    blk = lax.broadcasted_iota(I32, (n_blk, Q_BLOCK), 0)
    tlane = start + lax.broadcasted_iota(I32, (1, Q_BLOCK), 1)
    cur = tlane // SEL_BLOCK
    causal = blk * SEL_BLOCK <= tlane
    forced = causal & ((blk == 0) | (blk == cur) | (blk == cur - 1))
    score = jnp.where(forced, FORCE_SCORE, jnp.where(causal, imp, -1.0))
    rank = jnp.zeros((n_blk, Q_BLOCK), F32)
    for bp in range(n_blk):
        other = score[bp:bp + 1, :]
        beats = (other > score) | ((other == score) & (bp < blk))
        rank = rank + beats.astype(F32)
    sel_t = (rank < float(min(N_SEL, n_blk))).astype(BF16)
    eye = (lax.broadcasted_iota(I32, (Q_BLOCK, Q_BLOCK), 0)
           == lax.broadcasted_iota(I32, (Q_BLOCK, Q_BLOCK), 1)).astype(BF16)
    sel = lax.dot_general(eye, sel_t, _NT, preferred_element_type=F32).astype(BF16)

    n_chunks = (start + Q_BLOCK + kc - 1) // kc

    def online(state, kj, vj, bias):
        out = []
        for g in groups:
            m_i, acc = state[g]
            s = biased(lax.dot_general(q_of(g), kj, _NT, preferred_element_type=F32), bias)
            m_new = jnp.maximum(m_i, jnp.max(s, axis=-1, keepdims=True))
            p = jnp.exp2(s - m_new).astype(BF16)
            out.append((m_new, jnp.exp2(m_i - m_new) * acc + jnp.dot(p, vj, preferred_element_type=F32)))
        return tuple(out)

    def step(j, state, causal_chunk):
        off = pl.multiple_of(j * kc, kc)
        mexp = jnp.dot(sel, exp_ref[j], preferred_element_type=F32)
        bias = mexp * (-NEG_INF) + NEG_INF
        if causal_chunk:
            keypos = off + lax.broadcasted_iota(I32, (1, kc), 1)
            bias = jnp.where(keypos <= tpos, bias, NEG_INF)
        return online(state, ks_ref[0, 0, pl.ds(off, kc), :], vs_ref[0, 0, pl.ds(off, kc), :], bias)

    init = tuple((jnp.full((rows, 1), NEG_INF, F32), jnp.zeros((rows, LANES), F32)) for _ in groups)
    state = lax.fori_loop(0, n_chunks - 1, lambda j, c: step(j, c, False), init)
    sel_state = step(n_chunks - 1, state, True)

    s0 = jnp.maximum(start - WINDOW, 0)
    win_state = init
    for c in range((WINDOW + Q_BLOCK) // wc):
        off = pl.multiple_of(s0 + c * wc, Q_BLOCK)
        dist = tpos - (off + lax.broadcasted_iota(I32, (1, wc), 1))
        bias_w = jnp.where((dist >= 0) & (dist <= WINDOW), 0.0, NEG_INF)
        win_state = online(win_state, kw_ref[0, 0, pl.ds(off, wc), :], vw_ref[0, 0, pl.ds(off, wc), :], bias_w)

    gt = gate_ref[...]
    for g in groups:
        acc_s = sel_state[g][1]
        acc_w = win_state[g][1]
        o_s = acc_s[:, 0:HD] * (1.0 / acc_s[:, HD:HD + 1])
        o_w = acc_w[:, 0:HD] * (1.0 / acc_w[:, HD:HD + 1])
        for hh in range(hg):
            h = g * hg + hh
            rs = slice(hh * Q_BLOCK, (hh + 1) * Q_BLOCK)
            o = (gt[:, h:h + 1] * o_c[g][rs] + gt[:, QPK + h:QPK + h + 1] * o_s[rs]
                 + gt[:, 2 * QPK + h:2 * QPK + h + 1] * o_w[rs])
            o_ref[0, :, h * HD:(h + 1) * HD] = o


ATTN_HEAD_GROUP = 4
ATTN_KEY_CHUNK = 512
ATTN_WIN_CHUNK = 640


def _attn_prompt(q_hm, ck, cv, ks, vs, kw, vw, gates, band, expand, *, nb, t):
    n_qb = t // Q_BLOCK
    n_cmp_pad = ck.shape[2]
    n_blk = band.shape[0]
    kc = expand.shape[2]
    kv_spec = lambda n, w=HD: pl.BlockSpec((1, 1, n, w), lambda b, k, i: (b, k, 0, 0))
    return pl.pallas_call(
        functools.partial(_attn_p_kernel, n_cmp_pad=n_cmp_pad, n_blk=n_blk, kc=kc, hg=ATTN_HEAD_GROUP, wc=ATTN_WIN_CHUNK),
        grid=(nb, N_KV, n_qb),
        in_specs=[pl.BlockSpec((1, QPK, Q_BLOCK, HD), lambda b, k, i: (b, k, i, 0)),
                  kv_spec(n_cmp_pad), kv_spec(n_cmp_pad), kv_spec(t), kv_spec(t, LANES), kv_spec(t), kv_spec(t, LANES),
                  pl.BlockSpec((Q_BLOCK, LANES), lambda b, k, i: (b * n_qb + i, k)),
                  pl.BlockSpec(band.shape, lambda b, k, i: (0, 0)),
                  pl.BlockSpec(expand.shape, lambda b, k, i: (0, 0, 0))],
        out_specs=pl.BlockSpec((1, Q_BLOCK, QPK * HD), lambda b, k, i: (b, i, k)),
        out_shape=jax.ShapeDtypeStruct((nb, t, ATTN_W), F32),
        compiler_params=_params(3), name="attn_prompt",
    )(q_hm, ck, cv, ks, vs, kw, vw, gates, band, expand)


def _attn_s1_kernel(q_ref, ck_ref, cv_ref, oc_ref, pcs_ref, *, n_chunk, past):
    q = q_ref[0]
    q16 = jnp.concatenate([q, jnp.zeros_like(q)], axis=0).astype(BF16)
    cmp_end = (lax.broadcasted_iota(I32, (1, n_chunk), 1) + 2) * CMP_STRIDE - 1
    valid = cmp_end <= past
    head = lax.broadcasted_iota(I32, (2 * N_HEADS, 1), 0)
    oc = jnp.zeros((2 * N_HEADS, HD), F32)
    pcs = []
    for k in range(N_KV):
        s = lax.dot_general(q16, ck_ref[0, k].astype(BF16), _NT, preferred_element_type=F32)
        p = _softmax_rows(s, valid) * valid.astype(F32)
        in_grp = (head >= k * QPK) & (head < (k + 1) * QPK)
        p = jnp.where(in_grp, p, 0.0)
        oc = oc + jnp.dot(p.astype(BF16), cv_ref[0, k].astype(BF16), preferred_element_type=F32)
        pcs.append(jnp.sum(p, axis=0, keepdims=True))
    oc_ref[0] = oc[0:N_HEADS]
    pcs_ref[0] = jnp.concatenate(pcs + [jnp.zeros((SUBLANES - N_KV, n_chunk), F32)], axis=0)


def _topk_s_kernel(pcs_ref, band_ref, idx_ref, *, n_sel_blocks, past):
    imp = jnp.dot(pcs_ref[...], band_ref[...], preferred_element_type=F32, precision=lax.Precision.HIGHEST)
    rows, nbp = imp.shape
    blk = lax.broadcasted_iota(I32, (rows, nbp), 1)
    cur = past // SEL_BLOCK
    causal = blk * SEL_BLOCK <= past
    forced = causal & ((blk == 0) | (blk == cur) | (blk == cur - 1))
    score = jnp.where(forced, FORCE_SCORE, jnp.where(causal, imp, -1.0))
    score = jnp.where(blk < n_sel_blocks, score, -2.0)
    lane = lax.broadcasted_iota(I32, (rows, LANES), 1)
    out = jnp.zeros((rows, LANES), I32)
    for r in range(min(N_SEL, n_sel_blocks)):
        m = jnp.max(score, axis=-1, keepdims=True)
        pick = jnp.min(jnp.where(score == m, blk, nbp), axis=-1, keepdims=True)
        out = jnp.where(lane == r, pick, out)
        score = jnp.where(blk == pick, -3.0, score)
    idx_ref[...] = out


def _attn_s2_kernel(pt_ref, idx_ref, *refs, n_pages, past, n_sel_blocks):
    ktiles, vtiles = refs[:N_SEL], refs[N_SEL:2 * N_SEL]
    q_ref, oc_ref, kvs_ref, wk_ref, wv_ref, kvw_ref, gate_ref, o_ref, kbuf, vbuf = refs[2 * N_SEL:]
    b = pl.program_id(0)
    k = pl.program_id(1)
    q = q_ref[0]
    q16f = jnp.concatenate([q, jnp.zeros_like(q)], axis=0)
    q16 = q16f.astype(BF16)
    head = lax.broadcasted_iota(I32, (N_HEADS, 1), 0)
    nk = N_SEL * PAGE
    lane = lax.broadcasted_iota(I32, (1, nk), 1)
    slot = lane // PAGE
    new_blk = n_sel_blocks - 1
    wb = wk_ref.shape[-1]
    wpos = past - wb + lax.broadcasted_iota(I32, (1, wb), 1)
    wdist = past - wpos
    valid_w = (wdist >= 0) & (wdist <= WINDOW) & (wpos >= 0)

    def attend(s, valid, v_t, k_new, v_new):
        s_new = jnp.sum(q16f * k_new, axis=-1, keepdims=True)
        s = jnp.where(valid, s, NEG_INF)
        m = jnp.maximum(jnp.max(s, axis=-1, keepdims=True), s_new)
        e = jnp.exp2(s - m)
        e_new = jnp.exp2(s_new - m)
        den = jnp.sum(e, axis=-1, keepdims=True) + e_new
        acc = lax.dot_general(e.astype(BF16), v_t, _NT, preferred_element_type=F32) + e_new * v_new
        return acc / den

    in_grp = (head >= k * QPK) & (head < (k + 1) * QPK)
    bvec = jnp.zeros((1, nk), I32)
    for j in range(N_SEL):
        kbuf[:, j * PAGE:(j + 1) * PAGE] = ktiles[j][...].astype(BF16)
        vbuf[:, j * PAGE:(j + 1) * PAGE] = vtiles[j][...].astype(BF16)
        bvec = jnp.where(slot == j, idx_ref[(b * N_KV + k) * LANES + j], bvec)
    tok = (bvec // 2) * PAGE + lane % PAGE
    valid = (tok // SEL_BLOCK == bvec) & (bvec < new_blk) & (tok <= past)
    s = jnp.dot(q16, kbuf[...], preferred_element_type=F32)
    o_s = attend(s, valid, vbuf[...], kvs_ref[0, pl.ds(k, 1), :], kvs_ref[0, pl.ds(N_KV + k, 1), :])
    sw = jnp.dot(q16, wk_ref[...].astype(BF16), preferred_element_type=F32)
    o_w = attend(sw, valid_w, wv_ref[...].astype(BF16), kvw_ref[0, pl.ds(k, 1), :], kvw_ref[0, pl.ds(N_KV + k, 1), :])
    g = gate_ref[0]
    part = jnp.where(in_grp, g[:, 1:2] * o_s[0:N_HEADS] + g[:, 2:3] * o_w[0:N_HEADS], 0.0)

    @pl.when(k == 0)
    def _():
        o_ref[0] = g[:, 0:1] * oc_ref[0] + part

    @pl.when(k > 0)
    def _():
        o_ref[0] = o_ref[0] + part


def _attn_sample(q3, ck, cv, band_s, sel_t, pt_flat, kvs_rows, win_t, kvw_rows, gates_hm,
                 *, nb, n_pages, past, n_sel_blocks):
    n_chunk = ck.shape[2]
    nbp = band_s.shape[1]
    oc, pcs = pl.pallas_call(
        functools.partial(_attn_s1_kernel, n_chunk=n_chunk, past=past),
        grid=(nb,),
        in_specs=[pl.BlockSpec((1, N_HEADS, HD), lambda b: (b, 0, 0)),
                  pl.BlockSpec((1, N_KV, n_chunk, HD), lambda b: (b, 0, 0, 0)),
                  pl.BlockSpec((1, N_KV, n_chunk, HD), lambda b: (b, 0, 0, 0))],
        out_specs=[pl.BlockSpec((1, N_HEADS, HD), lambda b: (b, 0, 0)),
                   pl.BlockSpec((1, SUBLANES, n_chunk), lambda b: (b, 0, 0))],
        out_shape=[jax.ShapeDtypeStruct((nb, N_HEADS, HD), F32), jax.ShapeDtypeStruct((nb, SUBLANES, n_chunk), F32)],
        compiler_params=_params(1), name="attn_sample_cmp",
    )(q3, ck, cv)
    idx = pl.pallas_call(
        functools.partial(_topk_s_kernel, n_sel_blocks=n_sel_blocks, past=past),
        out_shape=jax.ShapeDtypeStruct((nb * N_KV, LANES), I32),
        compiler_params=pltpu.CompilerParams(vmem_limit_bytes=VMEM_LIMIT), name="topk_sample",
    )(pcs[:, 0:N_KV, :].reshape(nb * N_KV, n_chunk), band_s)
    idx_flat = idx.reshape(-1)

    def tile_map(c, j):
        def f(b, k, pt, ix):
            bidx = ix[(b * N_KV + k) * LANES + j]
            return (pt[b * n_pages + jnp.minimum(bidx // 2, n_pages - 1)], c, k, 0, 0)
        return f

    tile = lambda c, j: pl.BlockSpec((None, None, None, HD, PAGE), tile_map(c, j))
    in_specs = [tile(0, j) for j in range(N_SEL)] + [tile(1, j) for j in range(N_SEL)]
    wb = win_t.shape[-1]
    per_b = lambda shp: pl.BlockSpec(shp, lambda b, k, pt, ix: (b, 0, 0))
    in_specs += [per_b((1, N_HEADS, HD)), per_b((1, N_HEADS, HD)), per_b((1, SUBLANES, HD)),
                 pl.BlockSpec((None, None, None, HD, wb), lambda b, k, pt, ix: (b, 0, k, 0, 0)),
                 pl.BlockSpec((None, None, None, HD, wb), lambda b, k, pt, ix: (b, 1, k, 0, 0)),
                 per_b((1, SUBLANES, HD)), per_b((1, N_HEADS, LANES))]
    grid_spec = pltpu.PrefetchScalarGridSpec(
        num_scalar_prefetch=2, grid=(nb, N_KV), in_specs=in_specs,
        out_specs=per_b((1, N_HEADS, HD)),
        scratch_shapes=[pltpu.VMEM((HD, N_SEL * PAGE), BF16), pltpu.VMEM((HD, N_SEL * PAGE), BF16)])
    return pl.pallas_call(
        functools.partial(_attn_s2_kernel, n_pages=n_pages, past=past, n_sel_blocks=n_sel_blocks),
        grid_spec=grid_spec,
        out_shape=jax.ShapeDtypeStruct((nb, N_HEADS, HD), F32),
        compiler_params=_params(2), name="attn_sample_sel",
    )(pt_flat, idx_flat, *([sel_t] * (2 * N_SEL)), q3, oc, kvs_rows, win_t, win_t, kvw_rows, gates_hm)


TOK_ROWS = D_MODEL // LANES


def _store_token_tiles(ref, x):
    n = x.shape[0]
    for j in range(TOK_ROWS):
        ref[pl.ds(j, n, stride=TOK_ROWS), :] = x[:, j * LANES:(j + 1) * LANES]


def _load_token_tiles(ref, lead, n):
    return jnp.concatenate([ref[lead + (pl.ds(j, n, stride=TOK_ROWS), slice(None))] for j in range(TOK_ROWS)], axis=1)


def _outp_kernel(xp_ref, convp_ref, attnp_ref, ga1p_ref, sc2p_ref, sh2p_ref,
                 xs_ref, convs_ref, attns_ref, ga1s_ref, sc2s_ref, sh2s_ref,
                 gc_ref, ga_ref, w_ref, g2_ref, wr_ref, x1_ref, hp_ref, lg_ref, *, n_prompt_tiles):
    is_p = pl.program_id(0) < n_prompt_tiles
    pick = lambda a, b: jnp.where(is_p, a, b)
    cn = _rms(pick(convp_ref[...], convs_ref[...]), gc_ref[...])
    an = _rms(pick(attnp_ref[...], attns_ref[...]), ga_ref[...])
    cat = jnp.concatenate([cn, an], axis=1).astype(BF16)
    y = jnp.dot(cat, w_ref[...], preferred_element_type=F32)
    x1 = pick(xp_ref[...], xs_ref[...]) + pick(ga1p_ref[0], ga1s_ref[0]) * y
    x1_ref[...] = x1
    hp = _rms(x1, g2_ref[...]) * (1.0 + pick(sc2p_ref[0], sc2s_ref[0])) + pick(sh2p_ref[0], sh2s_ref[0])
    _store_token_tiles(hp_ref, hp)
    hp_hi = hp.astype(BF16)
    hp_lo = (hp - hp_hi.astype(F32)).astype(BF16)
    lg_ref[...] = (jnp.dot(hp_hi, wr_ref[0], preferred_element_type=F32)
                   + (jnp.dot(hp_hi, wr_ref[1], preferred_element_type=F32)
                      + jnp.dot(hp_lo, wr_ref[0], preferred_element_type=F32)))


TOKEN_TILE = 512


def _outp(prompt, sample, g_conv, g_attn, w_out_b, g2, w_route, *, tpb):
    tm = TOKEN_TILE
    n_p = prompt[0].shape[0] // tm
    total = (n_p + 1) * tm
    last = n_p - 1
    prow = lambda w: pl.BlockSpec((tm, w), lambda i: (jnp.minimum(i, last), 0))
    srow = lambda w: pl.BlockSpec((tm, w), lambda i: (0, 0))
    pmod = pl.BlockSpec((1, 1, D_MODEL), lambda i: (jnp.minimum(i, last) // tpb, 0, 0))
    smod = pl.BlockSpec((1, tm, D_MODEL), lambda i: (0, 0, 0))
    vec = lambda w: pl.BlockSpec((1, w), lambda i: (0, 0))
    row = lambda w: pl.BlockSpec((tm, w), lambda i: (i, 0))
    in_specs = [prow(D_MODEL), prow(CONV_W), prow(ATTN_W), pmod, pmod, pmod,
                srow(D_MODEL), srow(CONV_W), srow(ATTN_W), smod, smod, smod,
                vec(CONV_W), vec(ATTN_W), pl.BlockSpec((D_MODEL, D_MODEL), lambda i: (0, 0)), vec(D_MODEL),
                pl.BlockSpec((2, D_MODEL, LANES), lambda i: (0, 0, 0))]
    return pl.pallas_call(
        functools.partial(_outp_kernel, n_prompt_tiles=n_p),
        grid=(n_p + 1,), in_specs=in_specs,
        out_specs=[row(D_MODEL), pl.BlockSpec((tm * TOK_ROWS, LANES), lambda i: (i, 0)), row(LANES)],
        out_shape=[jax.ShapeDtypeStruct((total, D_MODEL), F32), jax.ShapeDtypeStruct((total * TOK_ROWS, LANES), F32),
                   jax.ShapeDtypeStruct((total, LANES), F32)],
        compiler_params=_params(1), name="outp",
    )(*prompt, *sample, g_conv.reshape(1, -1), g_attn.reshape(1, -1), w_out_b, g2.reshape(1, -1), w_route)


def _route_kernel(lg_ref, bias_ref, tri_ref, o_ref, cnt_ref, carry, *, tm, n_valid):
    i = pl.program_id(0)

    @pl.when(i == 0)
    def _():
        carry[...] = jnp.zeros_like(carry)

    lane = lax.broadcasted_iota(I32, (tm, LANES), 1)
    rowid = i * tm + lax.broadcasted_iota(I32, (tm, 1), 0)
    live = rowid < n_valid
    lg = lg_ref[...] + bias_ref[...]
    is_g = lane < N_GROUPS
    lgg = jnp.where(is_g, lg, NEG_INF)
    gmax = jnp.max(lgg, axis=-1, keepdims=True)
    grp = jnp.min(jnp.where(is_g & (lgg == gmax), lane, LANES), axis=-1, keepdims=True)
    p_grp = 1.0 / jnp.sum(jnp.where(is_g, jnp.exp(lgg - gmax), 0.0), axis=-1, keepdims=True)
    eid = lane - N_GROUPS
    in_grp = (eid >= grp * EPG) & (eid < (grp + 1) * EPG)
    le = jnp.where(in_grp, lg, NEG_INF)
    v1 = jnp.max(le, axis=-1, keepdims=True)
    e1 = jnp.min(jnp.where(in_grp & (le == v1), eid, LANES), axis=-1, keepdims=True)
    le2 = jnp.where(eid == e1, NEG_INF, le)
    v2 = jnp.max(le2, axis=-1, keepdims=True)
    e2 = jnp.min(jnp.where(in_grp & (eid != e1) & (le2 == v2), eid, LANES), axis=-1, keepdims=True)
    ex2 = jnp.exp(v2 - v1)
    w1 = p_grp * (1.0 / (1.0 + ex2))
    w2 = p_grp * (ex2 / (1.0 + ex2))
    oh1 = ((lane == e1) & live).astype(F32)
    oh2 = ((lane == e2) & live).astype(F32)
    both = oh1 + oh2
    before = jnp.dot(tri_ref[...], both.astype(BF16), preferred_element_type=F32) + carry[0:1, :]
    r1 = jnp.sum(oh1 * before, axis=-1, keepdims=True)
    r2 = jnp.sum(oh2 * before, axis=-1, keepdims=True)
    carry[0:1, :] = carry[0:1, :] + jnp.sum(both, axis=0, keepdims=True)
    out = jnp.where(lane == 0, e1.astype(F32), 0.0)
    out = jnp.where(lane == 1, e2.astype(F32), out)
    out = jnp.where(lane == 2, w1, out)
    out = jnp.where(lane == 3, w2, out)
    out = jnp.where(lane == 4, r1, out)
    out = jnp.where(lane == 5, r2, out)
    o_ref[...] = out
    cnt_ref[...] = carry[...]


def _route(logits, bias_row, n_valid):
    total = logits.shape[0]
    tm = TOKEN_TILE
    n_tiles = total // tm
    tri =(np.arange(tm)[:, None] > np.arange(tm)[None, :]).astype(np.float32)
    return pl.pallas_call(
        functools.partial(_route_kernel, tm=tm, n_valid=n_valid),
        grid=(n_tiles,),
        in_specs=[pl.BlockSpec((tm, LANES), lambda i: (i, 0)), pl.BlockSpec((1, LANES), lambda i: (0, 0)),
                  pl.BlockSpec((tm, tm), lambda i: (0, 0))],
        out_specs=[pl.BlockSpec((tm, LANES), lambda i: (i, 0)), pl.BlockSpec((SUBLANES, LANES), lambda i: (0, 0))],
        out_shape=[jax.ShapeDtypeStruct((total, LANES), F32), jax.ShapeDtypeStruct((SUBLANES, LANES), F32)],
        scratch_shapes=[pltpu.VMEM((SUBLANES, LANES), F32)],
        compiler_params=_params(1), name="route",
    )(logits, bias_row, jnp.asarray(tri, BF16))


EXPERT_ROWS = 256
DISPATCH_TILE = 256


def _tile_copy(src, src_row, dst, dst_row, sem):
    return pltpu.make_async_copy(src.at[pl.ds(pl.multiple_of(src_row * TOK_ROWS, TOK_ROWS), TOK_ROWS), :],
                                 dst.at[pl.ds(pl.multiple_of(dst_row * TOK_ROWS, TOK_ROWS), TOK_ROWS), :], sem)


def _dispatch_kernel(zstart_ref, zcnt_ref, dest_ref, x_ref, xb_hbm, stage, zeros, sem, zsem, *, n_tiles, n_blocks):
    i = pl.program_id(0)
    tm = DISPATCH_TILE
    slot = i % 2
    blk_rows = EXPERT_ROWS * TOK_ROWS

    def tail_copy(b):
        off = pl.multiple_of(b * blk_rows, blk_rows)
        return pltpu.make_async_copy(zeros, xb_hbm.at[pl.ds(off, blk_rows), :], zsem.at[1])

    @pl.when(i == 0)
    def _():
        zeros[...] = jnp.zeros_like(zeros)
        first_tail = zstart_ref[N_EXPERTS] // EXPERT_ROWS
        for e in range(N_EXPERTS):
            def fill(r, c, e=e):
                _tile_copy(zeros, 0, xb_hbm, zstart_ref[e] + r, zsem.at[0]).start()
                return c
            lax.fori_loop(0, zcnt_ref[e], fill, 0)
        lax.fori_loop(first_tail, n_blocks, lambda b, c: (tail_copy(b).start(), c)[1], 0)
        for e in range(N_EXPERTS):
            def drain(r, c):
                _tile_copy(zeros, 0, xb_hbm, 0, zsem.at[0]).wait()
                return c
            lax.fori_loop(0, zcnt_ref[e], drain, 0)
        lax.fori_loop(first_tail, n_blocks, lambda b, c: (tail_copy(b).wait(), c)[1], 0)

    def wait_rows(s):
        for _ in range(2 * tm):
            _tile_copy(stage.at[s], 0, xb_hbm, 0, sem.at[s]).wait()

    @pl.when(i >= 2)
    def _():
        wait_rows(slot)

    stage[slot] = x_ref[...]
    for r in range(tm):
        for k in range(2):
            _tile_copy(stage.at[slot], r, xb_hbm, dest_ref[0, k, r], sem.at[slot]).start(priority=k)

    @pl.when(i == n_tiles - 1)
    def _():
        wait_rows(slot)
        if n_tiles > 1:
            wait_rows(1 - slot)


def _dispatch(dest_pad, zstart, zcnt, hp_all, n_rows, n_blocks):
    tm = DISPATCH_TILE
    n_tiles = dest_pad.shape[1] // tm
    grid_spec = pltpu.PrefetchScalarGridSpec(
        num_scalar_prefetch=2, grid=(n_tiles,),
        in_specs=[pl.BlockSpec((1, 2, tm), lambda i, zs, zc: (i, 0, 0), memory_space=pltpu.SMEM),
                  pl.BlockSpec((tm * TOK_ROWS, LANES), lambda i, zs, zc: (i, 0))],
        out_specs=pl.BlockSpec(memory_space=pl.ANY),
        scratch_shapes=[pltpu.VMEM((2, tm * TOK_ROWS, LANES), F32), pltpu.VMEM((EXPERT_ROWS * TOK_ROWS, LANES), F32),
                        pltpu.SemaphoreType.DMA((2,)), pltpu.SemaphoreType.DMA((2,))])
    return pl.pallas_call(
        functools.partial(_dispatch_kernel, n_tiles=n_tiles, n_blocks=n_blocks),
        grid_spec=grid_spec,
        out_shape=jax.ShapeDtypeStruct((n_rows * TOK_ROWS, LANES), F32),
        compiler_params=_params(1), name="dispatch",
    )(zstart, zcnt, dest_pad.reshape(2, n_tiles, tm).transpose(1, 0, 2), hp_all)


def _experts_kernel(blk_e_ref, x_ref, wg_ref, wu_ref, wd_ref, o_ref, wg_b, wu_b, wd_b):
    i = pl.program_id(0)
    changed = jnp.logical_or(i == 0, blk_e_ref[i] != blk_e_ref[jnp.maximum(i - 1, 0)])

    @pl.when(changed)
    def _():
        wg_b[...] = wg_ref[0].astype(BF16)
        wu_b[...] = wu_ref[0].astype(BF16)
        wd_b[...] = wd_ref[0].astype(BF16)

    x = _load_token_tiles(x_ref, (), EXPERT_ROWS).astype(BF16)
    g = jnp.dot(x, wg_b[...], preferred_element_type=F32)
    u = jnp.dot(x, wu_b[...], preferred_element_type=F32)
    h = (g * jax.nn.sigmoid(g)) * u
    _store_token_tiles(o_ref, jnp.dot(h.astype(BF16), wd_b[...], preferred_element_type=F32))


def _experts(blk_e, xb, w_gate, w_up, w_down, n_blocks):
    blk = pl.BlockSpec((EXPERT_ROWS * TOK_ROWS, LANES), lambda i, be: (i, 0))
    grid_spec = pltpu.PrefetchScalarGridSpec(
        num_scalar_prefetch=1, grid=(n_blocks,),
        in_specs=[blk,
                  pl.BlockSpec((1, D_MODEL, D_EXPERT), lambda i, be: (be[i], 0, 0)),
                  pl.BlockSpec((1, D_MODEL, D_EXPERT), lambda i, be: (be[i], 0, 0)),
                  pl.BlockSpec((1, D_EXPERT, D_MODEL), lambda i, be: (be[i], 0, 0))],
        out_specs=blk,
        scratch_shapes=[pltpu.VMEM((D_MODEL, D_EXPERT), BF16), pltpu.VMEM((D_MODEL, D_EXPERT), BF16),
                        pltpu.VMEM((D_EXPERT, D_MODEL), BF16)])
    return pl.pallas_call(
        _experts_kernel,
        grid_spec=grid_spec,
        out_shape=jax.ShapeDtypeStruct((n_blocks * EXPERT_ROWS * TOK_ROWS, LANES), F32),
        compiler_params=_params(1), name="experts",
    )(blk_e, xb, w_gate, w_up, w_down)


def _final_kernel(dest_first_ref, dest_next_ref, yb_hbm, x1_ref, wt_ref, gate2_ref, gf_ref, o_ref, ybuf, sem,
                  *, tm, n_tiles):
    i = pl.program_id(0)
    slot = i % 2

    def issue(dest_ref, s):
        for r in range(tm):
            for k in range(2):
                d = dest_ref[0, k, r]
                src = yb_hbm.at[pl.ds(pl.multiple_of(d * TOK_ROWS, TOK_ROWS), TOK_ROWS), :]
                pltpu.make_async_copy(src, ybuf.at[s, k, pl.ds(r * TOK_ROWS, TOK_ROWS), :],
                                      sem.at[s]).start(priority=k)

    @pl.when(i == 0)
    def _():
        issue(dest_first_ref, 0)

    @pl.when(i + 1 < n_tiles)
    def _():
        issue(dest_next_ref, 1 - slot)

    for r in range(tm):
        for k in range(2):
            pltpu.make_async_copy(yb_hbm.at[pl.ds(0, TOK_ROWS), :], ybuf.at[slot, k, pl.ds(r * TOK_ROWS, TOK_ROWS), :],
                                  sem.at[slot]).wait()
    wt = wt_ref[...]
    f = wt[:, 2:3] * _load_token_tiles(ybuf, (slot, 0), tm) + wt[:, 3:4] * _load_token_tiles(ybuf, (slot, 1), tm)
    x2 = x1_ref[...] + gate2_ref[0] * f
    o_ref[...] = _rms(x2, gf_ref[...])


def _final(dest_pad, yb, x1_all, route_rows, gate2, final_g, *, rows, tpb, per_row, row0):
    tm = min(256, rows)
    n_tiles = rows // tm
    blk0 = row0 // tm
    dest3 = dest_pad.reshape(2, -1, tm).transpose(1, 0, 2)
    idx_blk = lambda f: pl.BlockSpec((1, 2, tm), f, memory_space=pltpu.SMEM)
    mod = (pl.BlockSpec((1, tm, D_MODEL), lambda i: (0, i, 0)) if per_row
           else pl.BlockSpec((1, 1, D_MODEL), lambda i: (i // tpb, 0, 0)))
    return pl.pallas_call(
        functools.partial(_final_kernel, tm=tm, n_tiles=n_tiles),
        grid=(n_tiles,),
        in_specs=[idx_blk(lambda i: (blk0, 0, 0)),
                  idx_blk(lambda i: (blk0 + jnp.minimum(i + 1, n_tiles - 1), 0, 0)),
                  pl.BlockSpec(memory_space=pl.ANY),
                  pl.BlockSpec((tm, D_MODEL), lambda i: (blk0 + i, 0)),
                  pl.BlockSpec((tm, LANES), lambda i: (blk0 + i, 0)),
                  mod, pl.BlockSpec((1, D_MODEL), lambda i: (0, 0))],
        out_specs=pl.BlockSpec((tm, D_MODEL), lambda i: (i, 0)),
        scratch_shapes=[pltpu.VMEM((2, 2, tm * TOK_ROWS, LANES), F32), pltpu.SemaphoreType.DMA((2,))],
        out_shape=jax.ShapeDtypeStruct((rows, D_MODEL), F32),
        compiler_params=_params(1), name="final_sample" if per_row else "final_prompt",
    )(dest3, dest3, yb, x1_all, route_rows, gate2, final_g.reshape(1, -1))


def _rope_tables(pos):
    inv = ROPE_THETA ** (-jnp.arange(HALF, dtype=F32) / HALF)
    ang = pos.astype(F32)[:, None] * inv[None, :]
    cos = jnp.tile(jnp.cos(ang), (1, LANES // HALF))
    sin = jnp.sin(ang)
    sin_s = jnp.tile(jnp.concatenate([-sin, sin], axis=1), (1, LANES // HD))
    return cos, sin_s


def _pack_w_in(w_in):
    gl = w_in[:, _C_G:_C_G + 3 * N_HEADS].reshape(D_MODEL, 3, N_KV, QPK)
    gcols = []
    for k in range(N_KV):
        gk = gl[:, :, k, :].reshape(D_MODEL, 3 * QPK)
        gcols.append(jnp.pad(gk, ((0, 0), (0, LANES - 3 * QPK))))
    return jnp.concatenate([w_in[:, :_C_G]] + gcols, axis=1).astype(BF16)


def _pack_cmp_weights(cmp_w1, cmp_w2, bias, cmp_b2):
    w1 = cmp_w1.reshape(2, 2, CMP_STRIDE, HD, CMP_HID)
    eye = jnp.eye(N_KV, dtype=F32)
    w1p = w1.transpose(0, 2, 3, 1, 4).reshape(2, CMP_STRIDE * HD, 2 * CMP_HID)
    w2p =jnp.einsum('chd,pk->cphkd', cmp_w2, eye).reshape(2, N_KV * CMP_HID, KV_W)
    b1p = jnp.tile(bias, (1, N_KV)).reshape(2, 1, N_KV * CMP_HID)
    b2p = jnp.tile(cmp_b2, (1, N_KV)).reshape(2, 1, KV_W)
    return w1p.astype(BF16), b1p, w2p.astype(BF16), b2p


def _band(n_cmp_pad, n_cmp, n_blk_pad, n_blk):
    n = np.arange(n_cmp_pad)[:, None]
    b = np.arange(n_blk_pad)[None, :]
    r = SEL_BLOCK // CMP_STRIDE
    m = (n >= r * b - 1) & (n <= r * b + r - 1) & (n < n_cmp) & (b < n_blk)
    return jnp.asarray(m.astype(np.float32))


def _expand(t, kc):
    n_chunks = t // kc
    key = np.arange(t).reshape(n_chunks, 1, kc)
    blk = np.arange(t // SEL_BLOCK).reshape(1, -1, 1)
    return jnp.asarray((key // SEL_BLOCK == blk).astype(np.float32), BF16)


def kernel(x_prompt, x_sample, c_prompt, c_sample, cache_cmp_kv, cache_sel_kv, cache_win_kv, state_conv, page_table,
           ln1_g, ln2_g, w_ada, b_ada, w_in, w_conv, cmp_pos, cmp_w1, cmp_b1, cmp_w2, cmp_b2, g_out_conv, g_out_attn,
           w_out, w_route_group, b_route_group, w_route_expert, b_route_expert, w_gate, w_up, w_down, final_g):
    depth = w_in.shape[0]
    assert depth == 1, "single-layer step"
    nb, t, _ = x_prompt.shape
    ns, ts, _ = x_sample.shape
    assert ts == 1 and t % 512 == 0 and t >= WINDOW + Q_BLOCK
    n_pool = cache_cmp_kv.shape[1]
    n_pages = page_table.shape[1]
    past = n_pages * PAGE
    wb = cache_win_kv.shape[2]
    assert wb == WINDOW
    l = 0

    n_c = nb + ns
    c_all = jnp.pad(jnp.concatenate([c_prompt, c_sample], axis=0), ((0, (-n_c) % SUBLANES), (0, 0)))
    mods = _ada(c_all, w_ada[l], b_ada[l])
    sh1, sc1, ga1, sh2, sc2, ga2 = [mods[:, j * D_MODEL:(j + 1) * D_MODEL] for j in range(6)]
    pr = lambda a: a[0:nb].reshape(nb, 1, D_MODEL)
    sr = lambda a: a[nb:nb + ns].reshape(1, ns, D_MODEL)

    w_pack = _pack_w_in(w_in[l])
    wconv8 = jnp.pad(w_conv[l], ((0, SUBLANES - CONV_K), (0, 0)))
    cos_p, sin_p = _rope_tables(jnp.arange(t, dtype=I32))
    cos_s, sin_s = _rope_tables(jnp.full((1,), past, I32))
    xp2 = x_prompt.reshape(nb * t, D_MODEL)
    xs2 = x_sample.reshape(ns, D_MODEL)
    (conv_p, cst_p, q_p, kvc_p, kvc_rows_p, kvs_rows_p, kvw_rows_p, ks_p, vs_p, kw_p, vw_p, gates_p) = _proj(
        xp2, ln1_g[l], pr(sc1), pr(sh1), w_pack, wconv8, cos_p, sin_p, nb=nb, t=t, sample=False)
    (conv_s, cst_s, q_s, _, kvc_rows_s, kvs_rows_s, kvw_rows_s, _, _, _, _, gates_s) = _proj(
        xs2, ln1_g[l], sr(sc1), sr(sh1), w_pack, wconv8, cos_s, sin_s, nb=ns, t=1, sample=True,
        prev=(state_conv[l][:, 0], state_conv[l][:, 1]))

    bias = _cmpbias(cmp_pos[l], cmp_w1[l], cmp_b1[l])
    w1p, b1p, w2p, b2p = _pack_cmp_weights(cmp_w1[l], cmp_w2[l], bias, cmp_b2[l])
    pp = t // PAGE
    cos_cp, sin_cp = _rope_tables((jnp.arange(t // CMP_STRIDE, dtype=I32) + 2) * CMP_STRIDE - 1)
    ck_p, cv_p = _cmp(kvc_p.reshape(nb * pp, SUBLANES, CHUNK_ROW), jnp.arange(nb * pp, dtype=I32), nb, pp,
                      w1p, b1p, w2p, b2p, cos_cp, sin_cp, "cmp_prompt", tiles=False)
    pt_flat = page_table.reshape(-1).astype(I32)
    cos_cs, sin_cs = _rope_tables((jnp.arange(past // CMP_STRIDE, dtype=I32) + 2) * CMP_STRIDE - 1)
    to_tiles = lambda a: a.transpose(0, 2, 3, 4, 1)
    ck_s, cv_s = _cmp(to_tiles(cache_cmp_kv[l]), pt_flat, ns, n_pages,
                      w1p, b1p, w2p, b2p, cos_cs, sin_cs, "cmp_sample", tiles=True)

    n_chunk_p = t // CMP_STRIDE
    n_blk_p = t // SEL_BLOCK
    band_p = _band(n_chunk_p, n_chunk_p - 1, n_blk_p, n_blk_p)
    attn_p = _attn_prompt(q_p, ck_p, cv_p, ks_p, vs_p, kw_p, vw_p, gates_p, band_p.T, _expand(t, ATTN_KEY_CHUNK), nb=nb, t=t)

    n_chunk_s = past // CMP_STRIDE
    n_sel_s = -(-(past + 1) // SEL_BLOCK)
    nbp = -(-n_sel_s // LANES) * LANES
    band_s = _band(n_chunk_s, (past + 1) // CMP_STRIDE - 1, nbp, n_sel_s)
    q3 = q_s.reshape(N_HEADS, ns, HD).transpose(1, 0, 2).astype(F32)
    gs = gates_s.reshape(ns, N_KV, LANES)[:, :, :3 * QPK].reshape(ns, N_KV, 3, QPK)
    gates_hm = jnp.pad(gs.transpose(0, 1, 3, 2).reshape(ns, N_HEADS, 3), ((0, 0), (0, 0), (0, LANES - 3)))
    rpt = 2 * N_KV
    new_rows = lambda a: jnp.pad(a.reshape(ns, rpt, HD), ((0, 0), (0, SUBLANES - rpt), (0, 0)))
    attn_s = _attn_sample(q3, ck_s, cv_s, band_s, to_tiles(cache_sel_kv[l]), pt_flat,
                          new_rows(kvs_rows_s), to_tiles(cache_win_kv[l]), new_rows(kvw_rows_s),
                          gates_hm, nb=ns, n_pages=n_pages, past=past, n_sel_blocks=n_sel_s).reshape(ns, ATTN_W)

    total = nb * t + ns
    w_out_b = w_out[l].astype(BF16)
    w_route = jnp.pad(jnp.concatenate([w_route_group[l], w_route_expert[l]], axis=1),
                      ((0, 0), (0, LANES - N_GROUPS - N_EXPERTS)))
    w_route_hi = w_route.astype(BF16)
    w_route = jnp.stack([w_route_hi, (w_route - w_route_hi.astype(F32)).astype(BF16)])
    b_route = jnp.pad(jnp.concatenate([b_route_group[l], b_route_expert[l]]), (0, LANES - N_GROUPS - N_EXPERTS))
    tile_pad = lambda a: jnp.pad(a, ((0, TOKEN_TILE - ns), (0, 0)))
    smod = lambda a: tile_pad(a[nb:nb + ns]).reshape(1, TOKEN_TILE, D_MODEL)
    x1_all, hp_all, lg_all = _outp(
        (xp2, conv_p, attn_p.reshape(nb * t, ATTN_W), pr(ga1), pr(sc2), pr(sh2)),
        (tile_pad(xs2), tile_pad(conv_s), tile_pad(attn_s), smod(ga1), smod(sc2), smod(sh2)),
        g_out_conv[l], g_out_attn[l], w_out_b, ln2_g[l], w_route, tpb=t // TOKEN_TILE)

    route, counts = _route(lg_all, b_route.reshape(1, LANES), total)
    route_t = route[:total, 0:SUBLANES].T.astype(I32)
    e = route_t[0:2]
    rank = route_t[4:6]
    cnt = counts[0, :N_EXPERTS].astype(I32)
    padded = (cnt + EXPERT_ROWS - 1) // EXPERT_ROWS * EXPERT_ROWS
    pad_end = jnp.cumsum(padded)
    pad_start = pad_end - padded
    m_slots = total * 2
    n_blocks = -(-(m_slots + N_EXPERTS * (EXPERT_ROWS - 1)) // EXPERT_ROWS)
    n_slots = n_blocks * EXPERT_ROWS
    first_slot = sum(jnp.where(e == j, pad_start[j], 0) for j in range(N_EXPERTS))
    dest = jnp.clip(first_slot + rank, 0, n_slots - 1)
    blk_start = jnp.arange(n_blocks, dtype=I32) * EXPERT_ROWS
    blk_e = jnp.minimum(jnp.sum((pad_end[None, :] <= blk_start[:, None]).astype(I32), axis=1), N_EXPERTS - 1)
    n_dump = 2 * (x1_all.shape[0] - total)
    dest_pad = jnp.concatenate([dest, n_slots + jnp.arange(n_dump, dtype=I32).reshape(2, -1)], axis=1)

    zstart = jnp.concatenate([pad_start + cnt, pad_end[-1:]])
    zcnt = jnp.concatenate([padded - cnt, jnp.zeros((1,), I32)])
    xb = _dispatch(dest_pad, zstart, zcnt, hp_all, n_slots + n_dump, n_blocks)
    yb = _experts(blk_e, xb, w_gate[l], w_up[l], w_down[l], n_blocks)
    y_p = _final(dest_pad, yb, x1_all, route, pr(ga2), final_g, rows=nb * t, tpb=t // 256, per_row=False, row0=0)
    y_s = _final(dest_pad, yb, x1_all, route, sr(ga2), final_g, rows=ns, tpb=1, per_row=True, row0=nb * t)

    kv_shape = (2, N_KV, HD)
    y_prompt = y_p.reshape(nb, t, D_MODEL)
    y_sample = y_s.reshape(ns, 1, D_MODEL)
    new_cmp_prompt = kvc_rows_p.reshape((1, nb, t) + kv_shape)
    new_cmp_sample = kvc_rows_s.reshape((1, ns, 1) + kv_shape)
    new_sel_prompt = kvs_rows_p.reshape((1, nb, t) + kv_shape)
    new_sel_sample = kvs_rows_s.reshape((1, ns, 1) + kv_shape)
    new_win_prompt = kvw_rows_p.reshape((nb, t) + kv_shape)[:, t - WINDOW:][None]
    new_win_sample = jnp.concatenate([cache_win_kv[l][:, 1:], kvw_rows_s.reshape((ns, 1) + kv_shape)], axis=1)[None]
    new_conv_prompt = cst_p[:, SUBLANES - (CONV_K - 1):][None]
    new_conv_sample = jnp.stack([state_conv[l][:, 1], cst_s], axis=1)[None]
    return (y_prompt, y_sample, new_cmp_prompt, new_cmp_sample, new_sel_prompt, new_sel_sample,
            new_win_prompt, new_win_sample, new_conv_prompt, new_conv_sample)
```

```python
import functools

import numpy as np
import jax
import jax.numpy as jnp
from jax import lax
from jax.experimental import pallas as pl
from jax.experimental.pallas import tpu as pltpu

F32 = jnp.float32
BF16 = jnp.bfloat16
I32 = jnp.int32

D_MODEL = 1024
CONV_W = 512
ATTN_W = 512
HD = 64
HALF = HD // 2
N_HEADS = 8
N_KV = 2
QPK = 4
KV_W = N_KV * HD
CONV_K = 3
PAGE = 128
CMP_STRIDE = 16
CMP_HID = 128
SEL_BLOCK = 64
N_SEL = 16
WINDOW = 512
Q_BLOCK = 128
ROPE_THETA = 10000.0
N_GROUPS = 4
EPG = 8
N_EXPERTS = 32
D_EXPERT = 512
NORM_EPS = 1e-6
NEG_INF = -1e30
FORCE_SCORE = 1e4
LANES = 128
SUBLANES = 8
CHUNK_ROW = CMP_STRIDE * 2 * KV_W
VMEM_LIMIT = 56 * 1024 * 1024

_NT = (((1,), (1,)), ((), ()))
Q_SCALE = HD ** -0.5 * 1.4426950408889634


def _params(n_axes):
    return pltpu.CompilerParams(dimension_semantics=("arbitrary",) * n_axes,
                                vmem_limit_bytes=VMEM_LIMIT)


def _rms(x, g):
    return x * lax.rsqrt(jnp.mean(x * x, axis=-1, keepdims=True) + NORM_EPS) * g


def _rope128(x, cos, sin_signed, first_half):
    xr = jnp.where(first_half, pltpu.roll(x, LANES - HALF, 1), pltpu.roll(x, HALF, 1))
    return x * cos + xr * sin_signed


def _first_half_mask(rows):
    lane = lax.broadcasted_iota(I32, (rows, LANES), 1)
    return (lane % HD) < HALF


def _ada_kernel(c_ref, w_ref, b_ref, o_ref):
    c = c_ref[...]
    s = c * jax.nn.sigmoid(c)
    o_ref[...] = jnp.dot(s.astype(BF16), w_ref[...].astype(BF16), preferred_element_type=F32) + b_ref[...]


def _ada(c_all, w_ada, b_ada):
    m, d = c_all.shape
    n = w_ada.shape[1]
    tn = 1024
    return pl.pallas_call(
        _ada_kernel,
        grid=(n // tn,),
        in_specs=[pl.BlockSpec((m, d), lambda j: (0, 0)),
                  pl.BlockSpec((d, tn), lambda j: (0, j)),
                  pl.BlockSpec((1, tn), lambda j: (0, j))],
        out_specs=pl.BlockSpec((m, tn), lambda j: (0, j)),
        out_shape=jax.ShapeDtypeStruct((m, n), F32),
        compiler_params=_params(1),
        name="ada",
    )(c_all, w_ada, b_ada.reshape(1, n))


_C_B, _C_C, _C_U, _C_Q, _C_KVC, _C_KVS, _C_KVW, _C_G, _C_END = 0, 512, 1024, 1536, 2048, 2304, 2560, 2816, 3072


def _proj_kernel(*refs, tm, tpb, sample):
    if sample:
        (x_ref, g1_ref, sc_ref, sh_ref, w_ref, wc_ref, cos_ref, sin_ref, p0_ref, p1_ref,
         conv_ref, cst_ref, q_ref, kvc_ref, kvc_il_ref, kvs_ref, kvw_ref, ks_ref, vs_ref, kw_ref, vw_ref, gate_ref,
         ilbuf) = refs
        vbuf = None
    else:
        (x_ref, g1_ref, sc_ref, sh_ref, w_ref, wc_ref, cos_ref, sin_ref,
         conv_ref, cst_ref, q_ref, kvc_ref, kvc_il_ref, kvs_ref, kvw_ref, ks_ref, vs_ref, kw_ref, vw_ref, gate_ref,
         ilbuf, vbuf) = refs
    i = pl.program_id(0)
    x = x_ref[...]
    h = _rms(x, g1_ref[...]) * (1.0 + sc_ref[0]) + sh_ref[0]
    hb = h.astype(BF16)

    zc = jnp.dot(hb, w_ref[:, _C_B:_C_Q], preferred_element_type=F32)
    b_g = zc[:, 0:CONV_W]
    v = zc[:, CONV_W:2 * CONV_W] * zc[:, 2 * CONV_W:3 * CONV_W]
    wc = wc_ref[...]
    if sample:
        y = wc[0:1] * p0_ref[...] + wc[1:2] * p1_ref[...] + wc[2:3] * v
        cst_ref[...] = v
    else:
        @pl.when(i % tpb == 0)
        def _():
            vbuf[0:SUBLANES, :] = jnp.zeros((SUBLANES, CONV_W), F32)
        vbuf[SUBLANES:SUBLANES + tm, :] = v
        y = wc[0:1] * vbuf[pl.ds(SUBLANES - 2, tm), :] + wc[1:2] * vbuf[pl.ds(SUBLANES - 1, tm), :] + wc[2:3] * v
        tail = vbuf[tm:tm + SUBLANES, :]
        cst_ref[0] = tail
        vbuf[0:SUBLANES, :] = tail
    conv_ref[...] = b_g * y

    cos = cos_ref[...]
    sin_s = sin_ref[...]
    first = _first_half_mask(tm)

    zq = jnp.dot(hb, w_ref[:, _C_Q:_C_KVC], preferred_element_type=F32)
    for gq in range(ATTN_W // LANES):
        qr = _rope128(zq[:, gq * LANES:(gq + 1) * LANES], cos, sin_s, first) * Q_SCALE
        q_ref[0, 2 * gq] = qr[:, 0:HD].astype(BF16)
        q_ref[0, 2 * gq + 1] = qr[:, HD:LANES].astype(BF16)

    def store_rows(out_ref, halves):
        for j in range(2 * N_KV):
            piece = halves[j // N_KV]
            if j % N_KV == 1:
                piece = pltpu.roll(piece, HD, 1)
            ilbuf[pl.ds(j, tm, stride=2 * N_KV), :] = piece
        out_ref[...] = ilbuf[:, 0:HD]

    zkv = jnp.dot(hb, w_ref[:, _C_KVC:_C_G], preferred_element_type=F32)
    kvc_ref[...] = zkv[:, 0:2 * KV_W]
    store_rows(kvc_il_ref, (zkv[:, 0:KV_W], zkv[:, KV_W:2 * KV_W]))
    for base, kv_ref, kh_ref, vh_ref in ((2 * KV_W, kvs_ref, ks_ref, vs_ref), (4 * KV_W, kvw_ref, kw_ref, vw_ref)):
        kr = _rope128(zkv[:, base:base + KV_W], cos, sin_s, first)
        vv = zkv[:, base + KV_W:base + 2 * KV_W]
        store_rows(kv_ref, (kr, vv))
        lane = lax.broadcasted_iota(I32, (tm, LANES), 1)
        for k in range(N_KV):
            kh_ref[0, k] = kr[:, k * HD:(k + 1) * HD].astype(BF16)
            vk = vv if k == 0 else pltpu.roll(vv, HD, 1)
            vh_ref[0, k] = jnp.where(lane < HD, vk, jnp.where(lane == HD, 1.0, 0.0)).astype(BF16)

    zg = jnp.dot(hb, w_ref[:, _C_G:_C_END], preferred_element_type=F32)
    gate_ref[...] = jax.nn.sigmoid(zg)


def _proj(x2d, g1, sc, sh, w_pack, w_conv, cos_t, sin_t, *, nb, t, sample, prev=None):
    rows = nb * t
    tm = min(512, rows) if not sample else rows
    tpb = (t // tm) if not sample else 1
    n_tiles = rows // tm
    f = lambda a: jax.ShapeDtypeStruct(a, F32)
    b = lambda a: jax.ShapeDtypeStruct(a, BF16)
    if sample:
        mod_spec = pl.BlockSpec((1, tm, D_MODEL), lambda i: (0, 0, 0))
        tab_spec = pl.BlockSpec((1, LANES), lambda i: (0, 0))
        cst_shape, cst_spec = f((rows, CONV_W)), pl.BlockSpec((tm, CONV_W), lambda i: (0, 0))
        hm = lambda i: (0, 0, i, 0)
        hb_, ht_ = 1, rows
    else:
        mod_spec = pl.BlockSpec((1, 1, D_MODEL), lambda i: (i // tpb, 0, 0))
        tab_spec = pl.BlockSpec((tm, LANES), lambda i: (i % tpb, 0))
        cst_shape, cst_spec = f((nb, SUBLANES, CONV_W)), pl.BlockSpec((1, SUBLANES, CONV_W), lambda i: (i // tpb, 0, 0))
        hm = lambda i: (i // tpb, 0, i % tpb, 0)
        hb_, ht_ = nb, t
    row = lambda w: pl.BlockSpec((tm, w), lambda i: (i, 0))
    in_specs = [row(D_MODEL), pl.BlockSpec((1, D_MODEL), lambda i: (0, 0)), mod_spec, mod_spec,
                pl.BlockSpec((D_MODEL, _C_END), lambda i: (0, 0)),
                pl.BlockSpec((SUBLANES, CONV_W), lambda i: (0, 0)), tab_spec, tab_spec]
    args = [x2d, g1.reshape(1, D_MODEL), sc, sh, w_pack, w_conv, cos_t, sin_t]
    scratch = [pltpu.VMEM((2 * N_KV * tm, LANES), F32)]
    if sample:
        in_specs += [row(CONV_W), row(CONV_W)]
        args += [prev[0], prev[1]]
    else:
        scratch.append(pltpu.VMEM((tm + SUBLANES, CONV_W), F32))
    il_rows = 2 * N_KV * rows
    il = pl.BlockSpec((2 * N_KV * tm, HD), lambda i: (i, 0))
    out_shape = [f((rows, CONV_W)), cst_shape, b((hb_, N_HEADS, ht_, HD)),
                 f((rows, 2 * KV_W)), f((il_rows, HD)), f((il_rows, HD)), f((il_rows, HD)),
                 b((hb_, N_KV, ht_, HD)), b((hb_, N_KV, ht_, LANES)), b((hb_, N_KV, ht_, HD)), b((hb_, N_KV, ht_, LANES)),
                 f((rows, 2 * LANES))]
    out_specs = [row(CONV_W), cst_spec, pl.BlockSpec((1, N_HEADS, tm, HD), hm),
                 row(2 * KV_W), il, il, il,
                 pl.BlockSpec((1, N_KV, tm, HD), hm), pl.BlockSpec((1, N_KV, tm, LANES), hm),
                 pl.BlockSpec((1, N_KV, tm, HD), hm), pl.BlockSpec((1, N_KV, tm, LANES), hm),
                 row(2 * LANES)]
    return pl.pallas_call(
        functools.partial(_proj_kernel, tm=tm, tpb=tpb, sample=sample),
        grid=(n_tiles,), in_specs=in_specs, out_specs=out_specs, out_shape=out_shape,
        scratch_shapes=scratch, compiler_params=_params(1),
        name="proj_sample" if sample else "proj_prompt",
    )(*args)


def _cmpbias_kernel(pos_ref, w_ref, b1_ref, o_ref):
    for c in range(2):
        o_ref[c:c + 1, :] = jnp.sum(pos_ref[c] * w_ref[c], axis=0, keepdims=True) + b1_ref[c:c + 1, :]


def _cmpbias(cmp_pos, cmp_w1, cmp_b1):
    n = cmp_pos.shape[1] * cmp_pos.shape[2]
    return pl.pallas_call(
        _cmpbias_kernel,
        out_shape=jax.ShapeDtypeStruct((2, CMP_HID), F32),
        compiler_params=pltpu.CompilerParams(vmem_limit_bytes=VMEM_LIMIT),
        name="cmpbias",
    )(cmp_pos.reshape(2, n, 1), cmp_w1.reshape(2, n, CMP_HID), cmp_b1)


def _cmp_kernel(pt_ref, *refs, ppt, nsub, tiles):
    n_in = nsub * ppt + 1
    all_pages = refs[:n_in]
    if tiles:
        unfold_ref = refs[n_in]
        refs = refs[1:]
    w1_ref, b1_ref, w2_ref, b2_ref, cos_ref, sin_ref, ck_ref, cv_ref, lhs_all, pbuf = refs[n_in:]
    r = ppt * SUBLANES
    for u in range(nsub):
        _cmp_unfold(all_pages[u * ppt:u * ppt + ppt + 1], unfold_ref if tiles else None, lhs_all.at[u], ppt, tiles)
    for u in range(nsub):
        rows = slice(u * r, (u + 1) * r)
        _cmp_mlp(lhs_all.at[u], pbuf, w1_ref, b1_ref, w2_ref, b2_ref, cos_ref[rows, :], sin_ref[rows, :],
                 ck_ref, cv_ref, rows, r)


def _cmp_unfold(pages, unfold_ref, lhs, ppt, tiles):
    r = ppt * SUBLANES
    rk = r + SUBLANES
    low = lax.broadcasted_iota(I32, (SUBLANES, LANES), 1) < HD

    def tap_tile(j, c, s, y):
        if tiles:
            return y[s * SUBLANES:(s + 1) * SUBLANES, c * KV_W:(c + 1) * KV_W]
        return pages[j][0, :, s * 2 * KV_W + c * KV_W:s * 2 * KV_W + (c + 1) * KV_W]

    for j in range(ppt + 1):
        y = None
        if tiles:
            a = pages[j][...].reshape(2 * KV_W, PAGE).astype(BF16)
            y = lax.dot_general(unfold_ref[...], a, _NT, preferred_element_type=F32)
        for c in range(2):
            for sp in range(CMP_STRIDE // 2):
                t0 = tap_tile(j, c, 2 * sp, y)
                t1 = tap_tile(j, c, 2 * sp + 1, y)
                lhs[c, j * SUBLANES:(j + 1) * SUBLANES, sp * LANES:(sp + 1) * LANES] = (
                    jnp.where(low, t0, pltpu.roll(t1, HD, 1)))
                lhs[c, rk + j * SUBLANES:rk + (j + 1) * SUBLANES, sp * LANES:(sp + 1) * LANES] = (
                    jnp.where(low, pltpu.roll(t0, HD, 1), t1))


def _cmp_mlp(lhs, pbuf, w1_ref, b1_ref, w2_ref, b2_ref, cos, sin_s, ck_ref, cv_ref, rows, r):
    rk = r + SUBLANES
    first = _first_half_mask(r)
    for c in range(2):
        p = jnp.dot(lhs[c].astype(BF16), w1_ref[c], preferred_element_type=F32)
        hids = []
        for k in range(N_KV):
            pbuf[...] = p[k * rk:(k + 1) * rk, CMP_HID:2 * CMP_HID]
            hids.append(p[k * rk:k * rk + r, 0:CMP_HID] + pbuf[pl.ds(1, r), :])
        hid = jnp.concatenate(hids, axis=1) + b1_ref[c]
        act = jax.nn.gelu(hid)
        comp = jnp.dot(act.astype(BF16), w2_ref[c], preferred_element_type=F32) + b2_ref[c]
        if c == 0:
            comp = _rope128(comp, cos, sin_s, first)
            out = ck_ref
        else:
            out = cv_ref
        for k in range(N_KV):
            out[0, k, rows, :] = comp[:, k * HD:(k + 1) * HD]


def _cmp(pages, pt_flat, nb, n_pages, w1p, b1p, w2p, b2p, cos_c, sin_c, name, tiles):
    ppt = min(32, n_pages)
    nsub = 2 if n_pages % (2 * ppt) == 0 else 1
    pps = nsub * ppt
    n_tiles = n_pages // pps
    r = pps * SUBLANES
    n_chunk = n_pages * SUBLANES
    zeros = (0,) * (pages.ndim - 1)

    def page_map(j):
        return lambda b, t, pt: (pt[b * n_pages + t * pps + j],) + zeros

    def next_map(b, t, pt):
        return (pt[b * n_pages + jnp.minimum(t * pps + pps, n_pages - 1)],) + zeros

    page_blk = (None, 2, N_KV, HD, PAGE) if tiles else (1, SUBLANES, CHUNK_ROW)
    in_specs = [pl.BlockSpec(page_blk, page_map(j)) for j in range(pps)]
    in_specs.append(pl.BlockSpec(page_blk, next_map))
    const = lambda shp: pl.BlockSpec(shp, lambda b, t, pt: (0,) * len(shp))
    extra = []
    if tiles:
        row = np.arange(PAGE)
        tok = (row % SUBLANES) * CMP_STRIDE + row // SUBLANES
        extra = [jnp.asarray(tok[:, None] == np.arange(PAGE)[None, :], BF16)]
        in_specs.append(const((PAGE, PAGE)))
    in_specs += [const(w1p.shape), const(b1p.shape), const(w2p.shape), const(b2p.shape),
                 pl.BlockSpec((r, LANES), lambda b, t, pt: (t, 0)), pl.BlockSpec((r, LANES), lambda b, t, pt: (t, 0))]
    hm = pl.BlockSpec((1, N_KV, r, HD), lambda b, t, pt: (b, 0, t, 0))
    grid_spec = pltpu.PrefetchScalarGridSpec(
        num_scalar_prefetch=1, grid=(nb, n_tiles), in_specs=in_specs, out_specs=[hm, hm],
        scratch_shapes=[pltpu.VMEM((nsub, 2, N_KV * (ppt + 1) * SUBLANES, CMP_STRIDE * HD), F32),
                        pltpu.VMEM(((ppt + 1) * SUBLANES, CMP_HID), F32)])
    return pl.pallas_call(
        functools.partial(_cmp_kernel, ppt=ppt, nsub=nsub, tiles=tiles),
        grid_spec=grid_spec,
        out_shape=[jax.ShapeDtypeStruct((nb, N_KV, n_chunk, HD), F32)] * 2,
        compiler_params=_params(2), name=name,
    )(pt_flat, *([pages] * (pps + 1)), *extra, w1p, b1p, w2p, b2p, cos_c, sin_c)


def _softmax_rows(s, valid):
    s = jnp.where(valid, s, NEG_INF)
    m = jnp.max(s, axis=-1, keepdims=True)
    e = jnp.exp2(s - m)
    return e / jnp.sum(e, axis=-1, keepdims=True)


def _attn_p_kernel(q_ref, ck_ref, cv_ref, ks_ref, vs_ref, kw_ref, vw_ref, gate_ref, band_ref, exp_ref, o_ref,
                   *, n_cmp_pad, n_blk, kc, hg, wc):
    qb = pl.program_id(2)
    start = qb * Q_BLOCK
    tpos = start + lax.broadcasted_iota(I32, (Q_BLOCK, 1), 0)
    groups = range(QPK // hg)
    rows = hg * Q_BLOCK

    def q_of(g):
        return q_ref[0, g * hg:(g + 1) * hg].reshape(rows, HD)

    def biased(s, bias):
        width = s.shape[-1]
        return (s.reshape(hg, Q_BLOCK, width) + bias[None]).reshape(rows, width)

    ck = ck_ref[0, 0].astype(BF16)
    cv = cv_ref[0, 0].astype(BF16)
    cmp_end = (lax.broadcasted_iota(I32, (1, n_cmp_pad), 1) + 2) * CMP_STRIDE - 1
    bias_c = jnp.where(cmp_end <= tpos, 0.0, NEG_INF)
    o_c = []
    pcs = jnp.zeros((Q_BLOCK, n_cmp_pad), F32)
    for g in groups:
        s_c = biased(lax.dot_general(q_of(g), ck, _NT, preferred_element_type=F32), bias_c)
        m_c = jnp.maximum(jnp.max(s_c, axis=-1, keepdims=True), 0.5 * NEG_INF)
        e_c = jnp.exp2(s_c - m_c)
        l_c = jnp.sum(e_c, axis=-1, keepdims=True)
        p_c = e_c * (1.0 / jnp.where(l_c > 0.0, l_c, 1.0))
        o_c.append(jnp.dot(p_c.astype(BF16), cv, preferred_element_type=F32))
        for h in range(hg):
            pcs = pcs + p_c[h * Q_BLOCK:(h + 1) * Q_BLOCK]

    imp =lax.dot_general(band_ref[...], pcs, _NT, preferred_element_type=F32,
                          precision=lax.Precision.HIGHEST)
    blk = lax.broadcasted_iota(I32, (n_blk, Q_BLOCK), 0)
    tlane = start + lax.broadcasted_iota(I32, (1, Q_BLOCK), 1)
    cur = tlane // SEL_BLOCK
    causal = blk * SEL_BLOCK <= tlane
    forced = causal & ((blk == 0) | (blk == cur) | (blk == cur - 1))
    score = jnp.where(forced, FORCE_SCORE, jnp.where(causal, imp, -1.0))
    rank = jnp.zeros((n_blk, Q_BLOCK), F32)
    for bp in range(n_blk):
        other = score[bp:bp + 1, :]
        beats = (other > score) | ((other == score) & (bp < blk))
        rank = rank + beats.astype(F32)
    sel_t = (rank < float(min(N_SEL, n_blk))).astype(BF16)
    eye = (lax.broadcasted_iota(I32, (Q_BLOCK, Q_BLOCK), 0)
           == lax.broadcasted_iota(I32, (Q_BLOCK, Q_BLOCK), 1)).astype(BF16)
    sel = lax.dot_general(eye, sel_t, _NT, preferred_element_type=F32).astype(BF16)

    n_chunks = (start + Q_BLOCK + kc - 1) // kc

    def online(state, kj, vj, bias):
        out = []
        for g in groups:
            m_i, acc = state[g]
            s = biased(lax.dot_general(q_of(g), kj, _NT, preferred_element_type=F32), bias)
            m_new = jnp.maximum(m_i, jnp.max(s, axis=-1, keepdims=True))
            p = jnp.exp2(s - m_new).astype(BF16)
            out.append((m_new, jnp.exp2(m_i - m_new) * acc + jnp.dot(p, vj, preferred_element_type=F32)))
        return tuple(out)

    def step(j, state, causal_chunk):
        off = pl.multiple_of(j * kc, kc)
        mexp = jnp.dot(sel, exp_ref[j], preferred_element_type=F32)
        bias = mexp * (-NEG_INF) + NEG_INF
        if causal_chunk:
            keypos = off + lax.broadcasted_iota(I32, (1, kc), 1)
            bias = jnp.where(keypos <= tpos, bias, NEG_INF)
        return online(state, ks_ref[0, 0, pl.ds(off, kc), :], vs_ref[0, 0, pl.ds(off, kc), :], bias)

    init = tuple((jnp.full((rows, 1), NEG_INF, F32), jnp.zeros((rows, LANES), F32)) for _ in groups)
    state = lax.fori_loop(0, n_chunks - 1, lambda j, c: step(j, c, False), init)
    sel_state = step(n_chunks - 1, state, True)

    s0 = jnp.maximum(start - WINDOW, 0)
    win_state = init
    for c in range((WINDOW + Q_BLOCK) // wc):
        off = pl.multiple_of(s0 + c * wc, Q_BLOCK)
        dist = tpos - (off + lax.broadcasted_iota(I32, (1, wc), 1))
        bias_w = jnp.where((dist >= 0) & (dist <= WINDOW), 0.0, NEG_INF)
        win_state = online(win_state, kw_ref[0, 0, pl.ds(off, wc), :], vw_ref[0, 0, pl.ds(off, wc), :], bias_w)

    gt = gate_ref[...]
    for g in groups:
        acc_s = sel_state[g][1]
        acc_w = win_state[g][1]
        o_s = acc_s[:, 0:HD] * (1.0 / acc_s[:, HD:HD + 1])
        o_w = acc_w[:, 0:HD] * (1.0 / acc_w[:, HD:HD + 1])
        for hh in range(hg):
            h = g * hg + hh
            rs = slice(hh * Q_BLOCK, (hh + 1) * Q_BLOCK)
            o = (gt[:, h:h + 1] * o_c[g][rs] + gt[:, QPK + h:QPK + h + 1] * o_s[rs]
                 + gt[:, 2 * QPK + h:2 * QPK + h + 1] * o_w[rs])
            o_ref[0, :, h * HD:(h + 1) * HD] = o


ATTN_HEAD_GROUP = 4
ATTN_KEY_CHUNK = 512
ATTN_WIN_CHUNK = 640


def _attn_prompt(q_hm, ck, cv, ks, vs, kw, vw, gates, band, expand, *, nb, t):
    n_qb = t // Q_BLOCK
    n_cmp_pad = ck.shape[2]
    n_blk = band.shape[0]
    kc = expand.shape[2]
    kv_spec = lambda n, w=HD: pl.BlockSpec((1, 1, n, w), lambda b, k, i: (b, k, 0, 0))
    return pl.pallas_call(
        functools.partial(_attn_p_kernel, n_cmp_pad=n_cmp_pad, n_blk=n_blk, kc=kc, hg=ATTN_HEAD_GROUP, wc=ATTN_WIN_CHUNK),
        grid=(nb, N_KV, n_qb),
        in_specs=[pl.BlockSpec((1, QPK, Q_BLOCK, HD), lambda b, k, i: (b, k, i, 0)),
                  kv_spec(n_cmp_pad), kv_spec(n_cmp_pad), kv_spec(t), kv_spec(t, LANES), kv_spec(t), kv_spec(t, LANES),
                  pl.BlockSpec((Q_BLOCK, LANES), lambda b, k, i: (b * n_qb + i, k)),
                  pl.BlockSpec(band.shape, lambda b, k, i: (0, 0)),
                  pl.BlockSpec(expand.shape, lambda b, k, i: (0, 0, 0))],
        out_specs=pl.BlockSpec((1, Q_BLOCK, QPK * HD), lambda b, k, i: (b, i, k)),
        out_shape=jax.ShapeDtypeStruct((nb, t, ATTN_W), F32),
        compiler_params=_params(3), name="attn_prompt",
    )(q_hm, ck, cv, ks, vs, kw, vw, gates, band, expand)


def _attn_s1_kernel(q_ref, ck_ref, cv_ref, oc_ref, pcs_ref, *, n_chunk, past):
    q = q_ref[0]
    q16 = jnp.concatenate([q, jnp.zeros_like(q)], axis=0).astype(BF16)
    cmp_end = (lax.broadcasted_iota(I32, (1, n_chunk), 1) + 2) * CMP_STRIDE - 1
    valid = cmp_end <= past
    head = lax.broadcasted_iota(I32, (2 * N_HEADS, 1), 0)
    oc = jnp.zeros((2 * N_HEADS, HD), F32)
    pcs = []
    for k in range(N_KV):
        s = lax.dot_general(q16, ck_ref[0, k].astype(BF16), _NT, preferred_element_type=F32)
        p = _softmax_rows(s, valid) * valid.astype(F32)
        in_grp = (head >= k * QPK) & (head < (k + 1) * QPK)
        p = jnp.where(in_grp, p, 0.0)
        oc = oc + jnp.dot(p.astype(BF16), cv_ref[0, k].astype(BF16), preferred_element_type=F32)
        pcs.append(jnp.sum(p, axis=0, keepdims=True))
    oc_ref[0] = oc[0:N_HEADS]
    pcs_ref[0] = jnp.concatenate(pcs + [jnp.zeros((SUBLANES - N_KV, n_chunk), F32)], axis=0)


def _topk_s_kernel(pcs_ref, band_ref, idx_ref, *, n_sel_blocks, past):
    imp = jnp.dot(pcs_ref[...], band_ref[...], preferred_element_type=F32, precision=lax.Precision.HIGHEST)
    rows, nbp = imp.shape
    blk = lax.broadcasted_iota(I32, (rows, nbp), 1)
    cur = past // SEL_BLOCK
    causal = blk * SEL_BLOCK <= past
    forced = causal & ((blk == 0) | (blk == cur) | (blk == cur - 1))
    score = jnp.where(forced, FORCE_SCORE, jnp.where(causal, imp, -1.0))
    score = jnp.where(blk < n_sel_blocks, score, -2.0)
    lane = lax.broadcasted_iota(I32, (rows, LANES), 1)
    out = jnp.zeros((rows, LANES), I32)
    for r in range(min(N_SEL, n_sel_blocks)):
        m = jnp.max(score, axis=-1, keepdims=True)
        pick = jnp.min(jnp.where(score == m, blk, nbp), axis=-1, keepdims=True)
        out = jnp.where(lane == r, pick, out)
        score = jnp.where(blk == pick, -3.0, score)
    idx_ref[...] = out


def _attn_s2_kernel(pt_ref, idx_ref, *refs, n_pages, past, n_sel_blocks):
    ktiles, vtiles = refs[:N_SEL], refs[N_SEL:2 * N_SEL]
    q_ref, oc_ref, kvs_ref, wk_ref, wv_ref, kvw_ref, gate_ref, o_ref, kbuf, vbuf = refs[2 * N_SEL:]
    b = pl.program_id(0)
    k = pl.program_id(1)
    q = q_ref[0]
    q16f = jnp.concatenate([q, jnp.zeros_like(q)], axis=0)
    q16 = q16f.astype(BF16)
    head = lax.broadcasted_iota(I32, (N_HEADS, 1), 0)
    nk = N_SEL * PAGE
    lane = lax.broadcasted_iota(I32, (1, nk), 1)
    slot = lane // PAGE
    new_blk = n_sel_blocks - 1
    wb = wk_ref.shape[-1]
    wpos = past - wb + lax.broadcasted_iota(I32, (1, wb), 1)
    wdist = past - wpos
    valid_w = (wdist >= 0) & (wdist <= WINDOW) & (wpos >= 0)

    def attend(s, valid, v_t, k_new, v_new):
        s_new = jnp.sum(q16f * k_new, axis=-1, keepdims=True)
        s = jnp.where(valid, s, NEG_INF)
        m = jnp.maximum(jnp.max(s, axis=-1, keepdims=True), s_new)
        e = jnp.exp2(s - m)
        e_new = jnp.exp2(s_new - m)
        den = jnp.sum(e, axis=-1, keepdims=True) + e_new
        acc = lax.dot_general(e.astype(BF16), v_t, _NT, preferred_element_type=F32) + e_new * v_new
        return acc / den

    in_grp = (head >= k * QPK) & (head < (k + 1) * QPK)
    bvec = jnp.zeros((1, nk), I32)
    for j in range(N_SEL):
        kbuf[:, j * PAGE:(j + 1) * PAGE] = ktiles[j][...].astype(BF16)
        vbuf[:, j * PAGE:(j + 1) * PAGE] = vtiles[j][...].astype(BF16)
        bvec = jnp.where(slot == j, idx_ref[(b * N_KV + k) * LANES + j], bvec)
    tok = (bvec // 2) * PAGE + lane % PAGE
    valid = (tok // SEL_BLOCK == bvec) & (bvec < new_blk) & (tok <= past)
    s = jnp.dot(q16, kbuf[...], preferred_element_type=F32)
    o_s = attend(s, valid, vbuf[...], kvs_ref[0, pl.ds(k, 1), :], kvs_ref[0, pl.ds(N_KV + k, 1), :])
    sw = jnp.dot(q16, wk_ref[...].astype(BF16), preferred_element_type=F32)
    o_w = attend(sw, valid_w, wv_ref[...].astype(BF16), kvw_ref[0, pl.ds(k, 1), :], kvw_ref[0, pl.ds(N_KV + k, 1), :])
    g = gate_ref[0]
    part = jnp.where(in_grp, g[:, 1:2] * o_s[0:N_HEADS] + g[:, 2:3] * o_w[0:N_HEADS], 0.0)

    @pl.when(k == 0)
    def _():
        o_ref[0] = g[:, 0:1] * oc_ref[0] + part

    @pl.when(k > 0)
    def _():
        o_ref[0] = o_ref[0] + part


def _attn_sample(q3, ck, cv, band_s, sel_t, pt_flat, kvs_rows, win_t, kvw_rows, gates_hm,
                 *, nb, n_pages, past, n_sel_blocks):
    n_chunk = ck.shape[2]
    nbp = band_s.shape[1]
    oc, pcs = pl.pallas_call(
        functools.partial(_attn_s1_kernel, n_chunk=n_chunk, past=past),
        grid=(nb,),
        in_specs=[pl.BlockSpec((1, N_HEADS, HD), lambda b: (b, 0, 0)),
                  pl.BlockSpec((1, N_KV, n_chunk, HD), lambda b: (b, 0, 0, 0)),
                  pl.BlockSpec((1, N_KV, n_chunk, HD), lambda b: (b, 0, 0, 0))],
        out_specs=[pl.BlockSpec((1, N_HEADS, HD), lambda b: (b, 0, 0)),
                   pl.BlockSpec((1, SUBLANES, n_chunk), lambda b: (b, 0, 0))],
        out_shape=[jax.ShapeDtypeStruct((nb, N_HEADS, HD), F32), jax.ShapeDtypeStruct((nb, SUBLANES, n_chunk), F32)],
        compiler_params=_params(1), name="attn_sample_cmp",
    )(q3, ck, cv)
    idx = pl.pallas_call(
        functools.partial(_topk_s_kernel, n_sel_blocks=n_sel_blocks, past=past),
        out_shape=jax.ShapeDtypeStruct((nb * N_KV, LANES), I32),
        compiler_params=pltpu.CompilerParams(vmem_limit_bytes=VMEM_LIMIT), name="topk_sample",
    )(pcs[:, 0:N_KV, :].reshape(nb * N_KV, n_chunk), band_s)
    idx_flat = idx.reshape(-1)
    seq_pages = jnp.minimum(idx[:, 0:N_SEL] // (PAGE // SEL_BLOCK), n_pages - 1).reshape(nb, N_KV * N_SEL)
    page_sel = jnp.take_along_axis(pt_flat.reshape(nb, n_pages), seq_pages, axis=1).reshape(-1)

    def tile_map(c, j):
        def f(b, k, ps, ix):
            return (ps[(b * N_KV + k) * N_SEL + j], c, k, 0, 0)
        return f

    tile = lambda c, j: pl.BlockSpec((None, None, None, HD, PAGE), tile_map(c, j))
    in_specs = [tile(0, j) for j in range(N_SEL)] + [tile(1, j) for j in range(N_SEL)]
    wb = win_t.shape[-1]
    per_b = lambda shp: pl.BlockSpec(shp, lambda b, k, pt, ix: (b, 0, 0))
    in_specs += [per_b((1, N_HEADS, HD)), per_b((1, N_HEADS, HD)), per_b((1, SUBLANES, HD)),
                 pl.BlockSpec((None, None, None, HD, wb), lambda b, k, pt, ix: (b, 0, k, 0, 0)),
                 pl.BlockSpec((None, None, None, HD, wb), lambda b, k, pt, ix: (b, 1, k, 0, 0)),
                 per_b((1, SUBLANES, HD)), per_b((1, N_HEADS, LANES))]
    grid_spec = pltpu.PrefetchScalarGridSpec(
        num_scalar_prefetch=2, grid=(nb, N_KV), in_specs=in_specs,
        out_specs=per_b((1, N_HEADS, HD)),
        scratch_shapes=[pltpu.VMEM((HD, N_SEL * PAGE), BF16), pltpu.VMEM((HD, N_SEL * PAGE), BF16)])
    return pl.pallas_call(
        functools.partial(_attn_s2_kernel, n_pages=n_pages, past=past, n_sel_blocks=n_sel_blocks),
        grid_spec=grid_spec,
        out_shape=jax.ShapeDtypeStruct((nb, N_HEADS, HD), F32),
        compiler_params=_params(2), name="attn_sample_sel",
    )(page_sel, idx_flat, *([sel_t] * (2 * N_SEL)), q3, oc, kvs_rows, win_t, win_t, kvw_rows, gates_hm)


TOK_ROWS = D_MODEL // LANES


def _store_token_tiles(ref, x):
    n = x.shape[0]
    for j in range(TOK_ROWS):
        ref[pl.ds(j, n, stride=TOK_ROWS), :] = x[:, j * LANES:(j + 1) * LANES]


def _load_token_tiles(ref, lead, n):
    return jnp.concatenate([ref[lead + (pl.ds(j, n, stride=TOK_ROWS), slice(None))] for j in range(TOK_ROWS)], axis=1)


def _outp_kernel(xp_ref, convp_ref, attnp_ref, ga1p_ref, sc2p_ref, sh2p_ref,
                 xs_ref, convs_ref, attns_ref, ga1s_ref, sc2s_ref, sh2s_ref,
                 gc_ref, ga_ref, w_ref, g2_ref, wr_ref, x1_ref, hp_ref, lg_ref, *, n_prompt_tiles):
    is_p = pl.program_id(0) < n_prompt_tiles
    pick = lambda a, b: jnp.where(is_p, a, b)
    cn = _rms(pick(convp_ref[...], convs_ref[...]), gc_ref[...])
    an = _rms(pick(attnp_ref[...], attns_ref[...]), ga_ref[...])
    cat = jnp.concatenate([cn, an], axis=1).astype(BF16)
    y = jnp.dot(cat, w_ref[...], preferred_element_type=F32)
    x1 = pick(xp_ref[...], xs_ref[...]) + pick(ga1p_ref[0], ga1s_ref[0]) * y
    x1_ref[...] = x1
    hp = _rms(x1, g2_ref[...]) * (1.0 + pick(sc2p_ref[0], sc2s_ref[0])) + pick(sh2p_ref[0], sh2s_ref[0])
    _store_token_tiles(hp_ref, hp)
    hp_hi = hp.astype(BF16)
    hp_lo = (hp - hp_hi.astype(F32)).astype(BF16)
    lg_ref[...] = (jnp.dot(hp_hi, wr_ref[0], preferred_element_type=F32)
                   + (jnp.dot(hp_hi, wr_ref[1], preferred_element_type=F32)
                      + jnp.dot(hp_lo, wr_ref[0], preferred_element_type=F32)))


TOKEN_TILE = 512


def _outp(prompt, sample, g_conv, g_attn, w_out_b, g2, w_route, *, tpb):
    tm = TOKEN_TILE
    n_p = prompt[0].shape[0] // tm
    total = (n_p + 1) * tm
    last = n_p - 1
    prow = lambda w: pl.BlockSpec((tm, w), lambda i: (jnp.minimum(i, last), 0))
    srow = lambda w: pl.BlockSpec((tm, w), lambda i: (0, 0))
    pmod = pl.BlockSpec((1, 1, D_MODEL), lambda i: (jnp.minimum(i, last) // tpb, 0, 0))
    smod = pl.BlockSpec((1, tm, D_MODEL), lambda i: (0, 0, 0))
    vec = lambda w: pl.BlockSpec((1, w), lambda i: (0, 0))
    row = lambda w: pl.BlockSpec((tm, w), lambda i: (i, 0))
    in_specs = [prow(D_MODEL), prow(CONV_W), prow(ATTN_W), pmod, pmod, pmod,
                srow(D_MODEL), srow(CONV_W), srow(ATTN_W), smod, smod, smod,
                vec(CONV_W), vec(ATTN_W), pl.BlockSpec((D_MODEL, D_MODEL), lambda i: (0, 0)), vec(D_MODEL),
                pl.BlockSpec((2, D_MODEL, LANES), lambda i: (0, 0, 0))]
    return pl.pallas_call(
        functools.partial(_outp_kernel, n_prompt_tiles=n_p),
        grid=(n_p + 1,), in_specs=in_specs,
        out_specs=[row(D_MODEL), pl.BlockSpec((tm * TOK_ROWS, LANES), lambda i: (i, 0)), row(LANES)],
        out_shape=[jax.ShapeDtypeStruct((total, D_MODEL), F32), jax.ShapeDtypeStruct((total * TOK_ROWS, LANES), F32),
                   jax.ShapeDtypeStruct((total, LANES), F32)],
        compiler_params=_params(1), name="outp",
    )(*prompt, *sample, g_conv.reshape(1, -1), g_attn.reshape(1, -1), w_out_b, g2.reshape(1, -1), w_route)


def _route_kernel(lg_ref, bias_ref, tri_ref, o_ref, cnt_ref, carry, *, tm, n_valid):
    i = pl.program_id(0)

    @pl.when(i == 0)
    def _():
        carry[...] = jnp.zeros_like(carry)

    lane = lax.broadcasted_iota(I32, (tm, LANES), 1)
    rowid = i * tm + lax.broadcasted_iota(I32, (tm, 1), 0)
    live = rowid < n_valid
    lg = lg_ref[...] + bias_ref[...]
    is_g = lane < N_GROUPS
    lgg = jnp.where(is_g, lg, NEG_INF)
    gmax = jnp.max(lgg, axis=-1, keepdims=True)
    grp = jnp.min(jnp.where(is_g & (lgg == gmax), lane, LANES), axis=-1, keepdims=True)
    p_grp = 1.0 / jnp.sum(jnp.where(is_g, jnp.exp(lgg - gmax), 0.0), axis=-1, keepdims=True)
    eid = lane - N_GROUPS
    in_grp = (eid >= grp * EPG) & (eid < (grp + 1) * EPG)
    le = jnp.where(in_grp, lg, NEG_INF)
    v1 = jnp.max(le, axis=-1, keepdims=True)
    e1 = jnp.min(jnp.where(in_grp & (le == v1), eid, LANES), axis=-1, keepdims=True)
    le2 = jnp.where(eid == e1, NEG_INF, le)
    v2 = jnp.max(le2, axis=-1, keepdims=True)
    e2 = jnp.min(jnp.where(in_grp & (eid != e1) & (le2 == v2), eid, LANES), axis=-1, keepdims=True)
    ex2 = jnp.exp(v2 - v1)
    w1 = p_grp * (1.0 / (1.0 + ex2))
    w2 = p_grp * (ex2 / (1.0 + ex2))
    oh1 = ((lane == e1) & live).astype(F32)
    oh2 = ((lane == e2) & live).astype(F32)
    both = oh1 + oh2
    before = jnp.dot(tri_ref[...], both.astype(BF16), preferred_element_type=F32) + carry[0:1, :]
    r1 = jnp.sum(oh1 * before, axis=-1, keepdims=True)
    r2 = jnp.sum(oh2 * before, axis=-1, keepdims=True)
    carry[0:1, :] = carry[0:1, :] + jnp.sum(both, axis=0, keepdims=True)
    out = jnp.where(lane == 0, e1.astype(F32), 0.0)
    out = jnp.where(lane == 1, e2.astype(F32), out)
    out = jnp.where(lane == 2, w1, out)
    out = jnp.where(lane == 3, w2, out)
    out = jnp.where(lane == 4, r1, out)
    out = jnp.where(lane == 5, r2, out)
    o_ref[...] = out
    cnt_ref[...] = carry[...]


def _route(logits, bias_row, n_valid):
    total = logits.shape[0]
    tm = TOKEN_TILE
    n_tiles = total // tm
    tri =(np.arange(tm)[:, None] > np.arange(tm)[None, :]).astype(np.float32)
    return pl.pallas_call(
        functools.partial(_route_kernel, tm=tm, n_valid=n_valid),
        grid=(n_tiles,),
        in_specs=[pl.BlockSpec((tm, LANES), lambda i: (i, 0)), pl.BlockSpec((1, LANES), lambda i: (0, 0)),
                  pl.BlockSpec((tm, tm), lambda i: (0, 0))],
        out_specs=[pl.BlockSpec((tm, LANES), lambda i: (i, 0)), pl.BlockSpec((SUBLANES, LANES), lambda i: (0, 0))],
        out_shape=[jax.ShapeDtypeStruct((total, LANES), F32), jax.ShapeDtypeStruct((SUBLANES, LANES), F32)],
        scratch_shapes=[pltpu.VMEM((SUBLANES, LANES), F32)],
        compiler_params=_params(1), name="route",
    )(logits, bias_row, jnp.asarray(tri, BF16))


EXPERT_ROWS = 256
DISPATCH_TILE = 256


def _tile_copy(src, src_row, dst, dst_row, sem):
    return pltpu.make_async_copy(src.at[pl.ds(pl.multiple_of(src_row * TOK_ROWS, TOK_ROWS), TOK_ROWS), :],
                                 dst.at[pl.ds(pl.multiple_of(dst_row * TOK_ROWS, TOK_ROWS), TOK_ROWS), :], sem)


def _dispatch_kernel(zstart_ref, zcnt_ref, dest_ref, x_ref, xb_hbm, stage, zeros, sem, zsem, *, n_tiles, n_blocks):
    i = pl.program_id(0)
    tm = DISPATCH_TILE
    slot = i % 2
    blk_rows = EXPERT_ROWS * TOK_ROWS

    def tail_copy(b):
        off = pl.multiple_of(b * blk_rows, blk_rows)
        return pltpu.make_async_copy(zeros, xb_hbm.at[pl.ds(off, blk_rows), :], zsem.at[1])

    @pl.when(i == 0)
    def _():
        zeros[...] = jnp.zeros_like(zeros)
        first_tail = zstart_ref[N_EXPERTS] // EXPERT_ROWS
        for e in range(N_EXPERTS):
            def fill(r, c, e=e):
                _tile_copy(zeros, 0, xb_hbm, zstart_ref[e] + r, zsem.at[0]).start()
                return c
            lax.fori_loop(0, zcnt_ref[e], fill, 0)
        lax.fori_loop(first_tail, n_blocks, lambda b, c: (tail_copy(b).start(), c)[1], 0)
        for e in range(N_EXPERTS):
            def drain(r, c):
                _tile_copy(zeros, 0, xb_hbm, 0, zsem.at[0]).wait()
                return c
            lax.fori_loop(0, zcnt_ref[e], drain, 0)
        lax.fori_loop(first_tail, n_blocks, lambda b, c: (tail_copy(b).wait(), c)[1], 0)

    def wait_rows(s):
        for _ in range(2 * tm):
            _tile_copy(stage.at[s], 0, xb_hbm, 0, sem.at[s]).wait()

    @pl.when(i >= 2)
    def _():
        wait_rows(slot)

    stage[slot] = x_ref[...]
    for r in range(tm):
        for k in range(2):
            _tile_copy(stage.at[slot], r, xb_hbm, dest_ref[0, k, r], sem.at[slot]).start(priority=k)

    @pl.when(i == n_tiles - 1)
    def _():
        wait_rows(slot)
        if n_tiles > 1:
            wait_rows(1 - slot)


def _dispatch(dest_pad, zstart, zcnt, hp_all, n_rows, n_blocks):
    tm = DISPATCH_TILE
    n_tiles = dest_pad.shape[1] // tm
    grid_spec = pltpu.PrefetchScalarGridSpec(
        num_scalar_prefetch=2, grid=(n_tiles,),
        in_specs=[pl.BlockSpec((1, 2, tm), lambda i, zs, zc: (i, 0, 0), memory_space=pltpu.SMEM),
                  pl.BlockSpec((tm * TOK_ROWS, LANES), lambda i, zs, zc: (i, 0))],
        out_specs=pl.BlockSpec(memory_space=pl.ANY),
        scratch_shapes=[pltpu.VMEM((2, tm * TOK_ROWS, LANES), F32), pltpu.VMEM((EXPERT_ROWS * TOK_ROWS, LANES), F32),
                        pltpu.SemaphoreType.DMA((2,)), pltpu.SemaphoreType.DMA((2,))])
    return pl.pallas_call(
        functools.partial(_dispatch_kernel, n_tiles=n_tiles, n_blocks=n_blocks),
        grid_spec=grid_spec,
        out_shape=jax.ShapeDtypeStruct((n_rows * TOK_ROWS, LANES), F32),
        compiler_params=_params(1), name="dispatch",
    )(zstart, zcnt, dest_pad.reshape(2, n_tiles, tm).transpose(1, 0, 2), hp_all)


def _experts_kernel(blk_e_ref, x_ref, wg_ref, wu_ref, wd_ref, o_ref, wg_b, wu_b, wd_b):
    i = pl.program_id(0)
    changed = jnp.logical_or(i == 0, blk_e_ref[i] != blk_e_ref[jnp.maximum(i - 1, 0)])

    @pl.when(changed)
    def _():
        wg_b[...] = wg_ref[0].astype(BF16)
        wu_b[...] = wu_ref[0].astype(BF16)
        wd_b[...] = wd_ref[0].astype(BF16)

    x = _load_token_tiles(x_ref, (), EXPERT_ROWS).astype(BF16)
    g = jnp.dot(x, wg_b[...], preferred_element_type=F32)
    u = jnp.dot(x, wu_b[...], preferred_element_type=F32)
    h = (g * jax.nn.sigmoid(g)) * u
    _store_token_tiles(o_ref, jnp.dot(h.astype(BF16), wd_b[...], preferred_element_type=F32))


def _experts(blk_e, xb, w_gate, w_up, w_down, n_blocks):
    blk = pl.BlockSpec((EXPERT_ROWS * TOK_ROWS, LANES), lambda i, be: (i, 0))
    grid_spec = pltpu.PrefetchScalarGridSpec(
        num_scalar_prefetch=1, grid=(n_blocks,),
        in_specs=[blk,
                  pl.BlockSpec((1, D_MODEL, D_EXPERT), lambda i, be: (be[i], 0, 0)),
                  pl.BlockSpec((1, D_MODEL, D_EXPERT), lambda i, be: (be[i], 0, 0)),
                  pl.BlockSpec((1, D_EXPERT, D_MODEL), lambda i, be: (be[i], 0, 0))],
        out_specs=blk,
        scratch_shapes=[pltpu.VMEM((D_MODEL, D_EXPERT), BF16), pltpu.VMEM((D_MODEL, D_EXPERT), BF16),
                        pltpu.VMEM((D_EXPERT, D_MODEL), BF16)])
    return pl.pallas_call(
        _experts_kernel,
        grid_spec=grid_spec,
        out_shape=jax.ShapeDtypeStruct((n_blocks * EXPERT_ROWS * TOK_ROWS, LANES), F32),
        compiler_params=_params(1), name="experts",
    )(blk_e, xb, w_gate, w_up, w_down)


def _final_kernel(dest_first_ref, dest_next_ref, yb_hbm, x1_ref, wt_ref, gate2_ref, gf_ref, o_ref, ybuf, sem,
                  *, tm, n_tiles):
    i = pl.program_id(0)
    slot = i % 2

    def issue(dest_ref, s):
        for r in range(tm):
            for k in range(2):
                d = dest_ref[0, k, r]
                src = yb_hbm.at[pl.ds(pl.multiple_of(d * TOK_ROWS, TOK_ROWS), TOK_ROWS), :]
                pltpu.make_async_copy(src, ybuf.at[s, k, pl.ds(r * TOK_ROWS, TOK_ROWS), :],
                                      sem.at[s]).start(priority=k)

    @pl.when(i == 0)
    def _():
        issue(dest_first_ref, 0)

    @pl.when(i + 1 < n_tiles)
    def _():
        issue(dest_next_ref, 1 - slot)

    for r in range(tm):
        for k in range(2):
            pltpu.make_async_copy(yb_hbm.at[pl.ds(0, TOK_ROWS), :], ybuf.at[slot, k, pl.ds(r * TOK_ROWS, TOK_ROWS), :],
                                  sem.at[slot]).wait()
    wt = wt_ref[...]
    f = wt[:, 2:3] * _load_token_tiles(ybuf, (slot, 0), tm) + wt[:, 3:4] * _load_token_tiles(ybuf, (slot, 1), tm)
    x2 = x1_ref[...] + gate2_ref[0] * f
    o_ref[...] = _rms(x2, gf_ref[...])


def _final(dest_pad, yb, x1_all, route_rows, gate2, final_g, *, rows, tpb, per_row, row0):
    tm = min(256, rows)
    n_tiles = rows // tm
    blk0 = row0 // tm
    dest3 = dest_pad.reshape(2, -1, tm).transpose(1, 0, 2)
    idx_blk = lambda f: pl.BlockSpec((1, 2, tm), f, memory_space=pltpu.SMEM)
    mod = (pl.BlockSpec((1, tm, D_MODEL), lambda i: (0, i, 0)) if per_row
           else pl.BlockSpec((1, 1, D_MODEL), lambda i: (i // tpb, 0, 0)))
    return pl.pallas_call(
        functools.partial(_final_kernel, tm=tm, n_tiles=n_tiles),
        grid=(n_tiles,),
        in_specs=[idx_blk(lambda i: (blk0, 0, 0)),
                  idx_blk(lambda i: (blk0 + jnp.minimum(i + 1, n_tiles - 1), 0, 0)),
                  pl.BlockSpec(memory_space=pl.ANY),
                  pl.BlockSpec((tm, D_MODEL), lambda i: (blk0 + i, 0)),
                  pl.BlockSpec((tm, LANES), lambda i: (blk0 + i, 0)),
                  mod, pl.BlockSpec((1, D_MODEL), lambda i: (0, 0))],
        out_specs=pl.BlockSpec((tm, D_MODEL), lambda i: (i, 0)),
        scratch_shapes=[pltpu.VMEM((2, 2, tm * TOK_ROWS, LANES), F32), pltpu.SemaphoreType.DMA((2,))],
        out_shape=jax.ShapeDtypeStruct((rows, D_MODEL), F32),
        compiler_params=_params(1), name="final_sample" if per_row else "final_prompt",
    )(dest3, dest3, yb, x1_all, route_rows, gate2, final_g.reshape(1, -1))


def _rope_tables(pos):
    inv = ROPE_THETA ** (-jnp.arange(HALF, dtype=F32) / HALF)
    ang = pos.astype(F32)[:, None] * inv[None, :]
    cos = jnp.tile(jnp.cos(ang), (1, LANES // HALF))
    sin = jnp.sin(ang)
    sin_s = jnp.tile(jnp.concatenate([-sin, sin], axis=1), (1, LANES // HD))
    return cos, sin_s


def _pack_w_in(w_in):
    gl = w_in[:, _C_G:_C_G + 3 * N_HEADS].reshape(D_MODEL, 3, N_KV, QPK)
    gcols = []
    for k in range(N_KV):
        gk = gl[:, :, k, :].reshape(D_MODEL, 3 * QPK)
        gcols.append(jnp.pad(gk, ((0, 0), (0, LANES - 3 * QPK))))
    return jnp.concatenate([w_in[:, :_C_G]] + gcols, axis=1).astype(BF16)


def _pack_cmp_weights(cmp_w1, cmp_w2, bias, cmp_b2):
    w1 = cmp_w1.reshape(2, 2, CMP_STRIDE, HD, CMP_HID)
    eye = jnp.eye(N_KV, dtype=F32)
    w1p = w1.transpose(0, 2, 3, 1, 4).reshape(2, CMP_STRIDE * HD, 2 * CMP_HID)
    w2p =jnp.einsum('chd,pk->cphkd', cmp_w2, eye).reshape(2, N_KV * CMP_HID, KV_W)
    b1p = jnp.tile(bias, (1, N_KV)).reshape(2, 1, N_KV * CMP_HID)
    b2p = jnp.tile(cmp_b2, (1, N_KV)).reshape(2, 1, KV_W)
    return w1p.astype(BF16), b1p, w2p.astype(BF16), b2p


def _band(n_cmp_pad, n_cmp, n_blk_pad, n_blk):
    n = np.arange(n_cmp_pad)[:, None]
    b = np.arange(n_blk_pad)[None, :]
    r = SEL_BLOCK // CMP_STRIDE
    m = (n >= r * b - 1) & (n <= r * b + r - 1) & (n < n_cmp) & (b < n_blk)
    return jnp.asarray(m.astype(np.float32))


def _expand(t, kc):
    n_chunks = t // kc
    key = np.arange(t).reshape(n_chunks, 1, kc)
    blk = np.arange(t // SEL_BLOCK).reshape(1, -1, 1)
    return jnp.asarray((key // SEL_BLOCK == blk).astype(np.float32), BF16)


def kernel(x_prompt, x_sample, c_prompt, c_sample, cache_cmp_kv, cache_sel_kv, cache_win_kv, state_conv, page_table,
           ln1_g, ln2_g, w_ada, b_ada, w_in, w_conv, cmp_pos, cmp_w1, cmp_b1, cmp_w2, cmp_b2, g_out_conv, g_out_attn,
           w_out, w_route_group, b_route_group, w_route_expert, b_route_expert, w_gate, w_up, w_down, final_g):
    depth = w_in.shape[0]
    assert depth == 1, "single-layer step"
    nb, t, _ = x_prompt.shape
    ns, ts, _ = x_sample.shape
    assert ts == 1 and t % 512 == 0 and t >= WINDOW + Q_BLOCK
    n_pool = cache_cmp_kv.shape[1]
    n_pages = page_table.shape[1]
    past = n_pages * PAGE
    wb = cache_win_kv.shape[2]
    assert wb == WINDOW
    l = 0

    n_c = nb + ns
    c_all = jnp.pad(jnp.concatenate([c_prompt, c_sample], axis=0), ((0, (-n_c) % SUBLANES), (0, 0)))
    mods = _ada(c_all, w_ada[l], b_ada[l])
    sh1, sc1, ga1, sh2, sc2, ga2 = [mods[:, j * D_MODEL:(j + 1) * D_MODEL] for j in range(6)]
    pr = lambda a: a[0:nb].reshape(nb, 1, D_MODEL)
    sr = lambda a: a[nb:nb + ns].reshape(1, ns, D_MODEL)

    w_pack = _pack_w_in(w_in[l])
    wconv8 = jnp.pad(w_conv[l], ((0, SUBLANES - CONV_K), (0, 0)))
    cos_p, sin_p = _rope_tables(jnp.arange(t, dtype=I32))
    cos_s, sin_s = _rope_tables(jnp.full((1,), past, I32))
    xp2 = x_prompt.reshape(nb * t, D_MODEL)
    xs2 = x_sample.reshape(ns, D_MODEL)
    (conv_p, cst_p, q_p, kvc_p, kvc_rows_p, kvs_rows_p, kvw_rows_p, ks_p, vs_p, kw_p, vw_p, gates_p) = _proj(
        xp2, ln1_g[l], pr(sc1), pr(sh1), w_pack, wconv8, cos_p, sin_p, nb=nb, t=t, sample=False)
    (conv_s, cst_s, q_s, _, kvc_rows_s, kvs_rows_s, kvw_rows_s, _, _, _, _, gates_s) = _proj(
        xs2, ln1_g[l], sr(sc1), sr(sh1), w_pack, wconv8, cos_s, sin_s, nb=ns, t=1, sample=True,
        prev=(state_conv[l][:, 0], state_conv[l][:, 1]))

    bias = _cmpbias(cmp_pos[l], cmp_w1[l], cmp_b1[l])
    w1p, b1p, w2p, b2p = _pack_cmp_weights(cmp_w1[l], cmp_w2[l], bias, cmp_b2[l])
    pp = t // PAGE
    cos_cp, sin_cp = _rope_tables((jnp.arange(t // CMP_STRIDE, dtype=I32) + 2) * CMP_STRIDE - 1)
    ck_p, cv_p = _cmp(kvc_p.reshape(nb * pp, SUBLANES, CHUNK_ROW), jnp.arange(nb * pp, dtype=I32), nb, pp,
                      w1p, b1p, w2p, b2p, cos_cp, sin_cp, "cmp_prompt", tiles=False)
    pt_flat = page_table.reshape(-1).astype(I32)
    cos_cs, sin_cs = _rope_tables((jnp.arange(past // CMP_STRIDE, dtype=I32) + 2) * CMP_STRIDE - 1)
    to_tiles = lambda a: a.transpose(0, 2, 3, 4, 1)
    ck_s, cv_s = _cmp(to_tiles(cache_cmp_kv[l]), pt_flat, ns, n_pages,
                      w1p, b1p, w2p, b2p, cos_cs, sin_cs, "cmp_sample", tiles=True)

    n_chunk_p = t // CMP_STRIDE
    n_blk_p = t // SEL_BLOCK
    band_p = _band(n_chunk_p, n_chunk_p - 1, n_blk_p, n_blk_p)
    attn_p = _attn_prompt(q_p, ck_p, cv_p, ks_p, vs_p, kw_p, vw_p, gates_p, band_p.T, _expand(t, ATTN_KEY_CHUNK), nb=nb, t=t)

    n_chunk_s = past // CMP_STRIDE
    n_sel_s = -(-(past + 1) // SEL_BLOCK)
    nbp = -(-n_sel_s // LANES) * LANES
    band_s = _band(n_chunk_s, (past + 1) // CMP_STRIDE - 1, nbp, n_sel_s)
    q3 = q_s.reshape(N_HEADS, ns, HD).transpose(1, 0, 2).astype(F32)
    gs = gates_s.reshape(ns, N_KV, LANES)[:, :, :3 * QPK].reshape(ns, N_KV, 3, QPK)
    gates_hm = jnp.pad(gs.transpose(0, 1, 3, 2).reshape(ns, N_HEADS, 3), ((0, 0), (0, 0), (0, LANES - 3)))
    rpt = 2 * N_KV
    new_rows = lambda a: jnp.pad(a.reshape(ns, rpt, HD), ((0, 0), (0, SUBLANES - rpt), (0, 0)))
    attn_s = _attn_sample(q3, ck_s, cv_s, band_s, to_tiles(cache_sel_kv[l]), pt_flat,
                          new_rows(kvs_rows_s), to_tiles(cache_win_kv[l]), new_rows(kvw_rows_s),
                          gates_hm, nb=ns, n_pages=n_pages, past=past, n_sel_blocks=n_sel_s).reshape(ns, ATTN_W)

    total = nb * t + ns
    w_out_b = w_out[l].astype(BF16)
    w_route = jnp.pad(jnp.concatenate([w_route_group[l], w_route_expert[l]], axis=1),
                      ((0, 0), (0, LANES - N_GROUPS - N_EXPERTS)))
    w_route_hi = w_route.astype(BF16)
    w_route = jnp.stack([w_route_hi, (w_route - w_route_hi.astype(F32)).astype(BF16)])
    b_route = jnp.pad(jnp.concatenate([b_route_group[l], b_route_expert[l]]), (0, LANES - N_GROUPS - N_EXPERTS))
    tile_pad = lambda a: jnp.pad(a, ((0, TOKEN_TILE - ns), (0, 0)))
    smod = lambda a: tile_pad(a[nb:nb + ns]).reshape(1, TOKEN_TILE, D_MODEL)
    x1_all, hp_all, lg_all = _outp(
        (xp2, conv_p, attn_p.reshape(nb * t, ATTN_W), pr(ga1), pr(sc2), pr(sh2)),
        (tile_pad(xs2), tile_pad(conv_s), tile_pad(attn_s), smod(ga1), smod(sc2), smod(sh2)),
        g_out_conv[l], g_out_attn[l], w_out_b, ln2_g[l], w_route, tpb=t // TOKEN_TILE)

    route, counts = _route(lg_all, b_route.reshape(1, LANES), total)
    route_t = route[:total, 0:SUBLANES].T.astype(I32)
    e = route_t[0:2]
    rank = route_t[4:6]
    cnt = counts[0, :N_EXPERTS].astype(I32)
    padded = (cnt + EXPERT_ROWS - 1) // EXPERT_ROWS * EXPERT_ROWS
    pad_end = jnp.cumsum(padded)
    pad_start = pad_end - padded
    m_slots = total * 2
    n_blocks = -(-(m_slots + N_EXPERTS * (EXPERT_ROWS - 1)) // EXPERT_ROWS)
    n_slots = n_blocks * EXPERT_ROWS
    first_slot = sum(jnp.where(e == j, pad_start[j], 0) for j in range(N_EXPERTS))
    dest = jnp.clip(first_slot + rank, 0, n_slots - 1)
    blk_start = jnp.arange(n_blocks, dtype=I32) * EXPERT_ROWS
    blk_e = jnp.minimum(jnp.sum((pad_end[None, :] <= blk_start[:, None]).astype(I32), axis=1), N_EXPERTS - 1)
    n_dump = 2 * (x1_all.shape[0] - total)
    dest_pad = jnp.concatenate([dest, n_slots + jnp.arange(n_dump, dtype=I32).reshape(2, -1)], axis=1)

    zstart = jnp.concatenate([pad_start + cnt, pad_end[-1:]])
    zcnt = jnp.concatenate([padded - cnt, jnp.zeros((1,), I32)])
    xb = _dispatch(dest_pad, zstart, zcnt, hp_all, n_slots + n_dump, n_blocks)
    yb = _experts(blk_e, xb, w_gate[l], w_up[l], w_down[l], n_blocks)
    y_p = _final(dest_pad, yb, x1_all, route, pr(ga2), final_g, rows=nb * t, tpb=t // 256, per_row=False, row0=0)
    y_s = _final(dest_pad, yb, x1_all, route, sr(ga2), final_g, rows=ns, tpb=1, per_row=True, row0=nb * t)

    kv_shape = (2, N_KV, HD)
    y_prompt = y_p.reshape(nb, t, D_MODEL)
    y_sample = y_s.reshape(ns, 1, D_MODEL)
    new_cmp_prompt = kvc_rows_p.reshape((1, nb, t) + kv_shape)
    new_cmp_sample = kvc_rows_s.reshape((1, ns, 1) + kv_shape)
    new_sel_prompt = kvs_rows_p.reshape((1, nb, t) + kv_shape)
    new_sel_sample = kvs_rows_s.reshape((1, ns, 1) + kv_shape)
    new_win_prompt = kvw_rows_p.reshape((nb, t) + kv_shape)[:, t - WINDOW:][None]
    new_win_sample = jnp.concatenate([cache_win_kv[l][:, 1:], kvw_rows_s.reshape((ns, 1) + kv_shape)], axis=1)[None]
    new_conv_prompt = cst_p[:, SUBLANES - (CONV_K - 1):][None]
    new_conv_sample = jnp.stack([state_conv[l][:, 1], cst_s], axis=1)[None]
    return (y_prompt, y_sample, new_cmp_prompt, new_cmp_sample, new_sel_prompt, new_sel_sample,
            new_win_prompt, new_win_sample, new_conv_prompt, new_conv_sample)
```

```python
import functools

import numpy as np
import jax
import jax.numpy as jnp
from jax import lax
from jax.experimental import pallas as pl
from jax.experimental.pallas import tpu as pltpu

F32 = jnp.float32
BF16 = jnp.bfloat16
I32 = jnp.int32

D_MODEL = 1024
CONV_W = 512
ATTN_W = 512
HD = 64
HALF = HD // 2
N_HEADS = 8
N_KV = 2
QPK = 4
KV_W = N_KV * HD
CONV_K = 3
PAGE = 128
CMP_STRIDE = 16
CMP_HID = 128
SEL_BLOCK = 64
N_SEL = 16
WINDOW = 512
Q_BLOCK = 128
ROPE_THETA = 10000.0
N_GROUPS = 4
EPG = 8
N_EXPERTS = 32
D_EXPERT = 512
NORM_EPS = 1e-6
NEG_INF = -1e30
FORCE_SCORE = 1e4
LANES = 128
SUBLANES = 8
CHUNK_ROW = CMP_STRIDE * 2 * KV_W
VMEM_LIMIT = 56 * 1024 * 1024

_NT = (((1,), (1,)), ((), ()))
Q_SCALE = HD ** -0.5 * 1.4426950408889634


def _params(n_axes):
    return pltpu.CompilerParams(dimension_semantics=("arbitrary",) * n_axes,
                                vmem_limit_bytes=VMEM_LIMIT)


def _rms(x, g):
    return x * lax.rsqrt(jnp.mean(x * x, axis=-1, keepdims=True) + NORM_EPS) * g


def _rope128(x, cos, sin_signed, first_half):
    xr = jnp.where(first_half, pltpu.roll(x, LANES - HALF, 1), pltpu.roll(x, HALF, 1))
    return x * cos + xr * sin_signed


def _first_half_mask(rows):
    lane = lax.broadcasted_iota(I32, (rows, LANES), 1)
    return (lane % HD) < HALF


def _ada_kernel(c_ref, w_ref, b_ref, o_ref):
    c = c_ref[...]
    s = c * jax.nn.sigmoid(c)
    o_ref[...] = jnp.dot(s.astype(BF16), w_ref[...].astype(BF16), preferred_element_type=F32) + b_ref[...]


def _ada(c_all, w_ada, b_ada):
    m, d = c_all.shape
    n = w_ada.shape[1]
    tn = 1024
    return pl.pallas_call(
        _ada_kernel,
        grid=(n // tn,),
        in_specs=[pl.BlockSpec((m, d), lambda j: (0, 0)),
                  pl.BlockSpec((d, tn), lambda j: (0, j)),
                  pl.BlockSpec((1, tn), lambda j: (0, j))],
        out_specs=pl.BlockSpec((m, tn), lambda j: (0, j)),
        out_shape=jax.ShapeDtypeStruct((m, n), F32),
        compiler_params=_params(1),
        name="ada",
    )(c_all, w_ada, b_ada.reshape(1, n))


_C_B, _C_C, _C_U, _C_Q, _C_KVC, _C_KVS, _C_KVW, _C_G, _C_END = 0, 512, 1024, 1536, 2048, 2304, 2560, 2816, 3072


def _proj_kernel(*refs, tm, tpb, sample):
    if sample:
        (x_ref, g1_ref, sc_ref, sh_ref, w_ref, wc_ref, cos_ref, sin_ref, p0_ref, p1_ref,
         conv_ref, cst_ref, q_ref, kvc_ref, kvc_il_ref, kvs_ref, kvw_ref, ks_ref, vs_ref, kw_ref, vw_ref, gate_ref,
         ilbuf) = refs
        vbuf = None
    else:
        (x_ref, g1_ref, sc_ref, sh_ref, w_ref, wc_ref, cos_ref, sin_ref,
         conv_ref, cst_ref, q_ref, kvc_ref, kvc_il_ref, kvs_ref, kvw_ref, ks_ref, vs_ref, kw_ref, vw_ref, gate_ref,
         ilbuf, vbuf) = refs
    i = pl.program_id(0)
    x = x_ref[...]
    h = _rms(x, g1_ref[...]) * (1.0 + sc_ref[0]) + sh_ref[0]
    hb = h.astype(BF16)

    zc = jnp.dot(hb, w_ref[:, _C_B:_C_Q], preferred_element_type=F32)
    b_g = zc[:, 0:CONV_W]
    v = zc[:, CONV_W:2 * CONV_W] * zc[:, 2 * CONV_W:3 * CONV_W]
    wc = wc_ref[...]
    if sample:
        y = wc[0:1] * p0_ref[...] + wc[1:2] * p1_ref[...] + wc[2:3] * v
        cst_ref[...] = v
    else:
        @pl.when(i % tpb == 0)
        def _():
            vbuf[0:SUBLANES, :] = jnp.zeros((SUBLANES, CONV_W), F32)
        vbuf[SUBLANES:SUBLANES + tm, :] = v
        y = wc[0:1] * vbuf[pl.ds(SUBLANES - 2, tm), :] + wc[1:2] * vbuf[pl.ds(SUBLANES - 1, tm), :] + wc[2:3] * v
        tail = vbuf[tm:tm + SUBLANES, :]
        cst_ref[0] = tail
        vbuf[0:SUBLANES, :] = tail
    conv_ref[...] = b_g * y

    cos = cos_ref[...]
    sin_s = sin_ref[...]
    first = _first_half_mask(tm)

    zq = jnp.dot(hb, w_ref[:, _C_Q:_C_KVC], preferred_element_type=F32)
    for gq in range(ATTN_W // LANES):
        qr = _rope128(zq[:, gq * LANES:(gq + 1) * LANES], cos, sin_s, first) * Q_SCALE
        q_ref[0, 2 * gq] = qr[:, 0:HD].astype(BF16)
        q_ref[0, 2 * gq + 1] = qr[:, HD:LANES].astype(BF16)

    def store_rows(out_ref, halves):
        for j in range(2 * N_KV):
            piece = halves[j // N_KV]
            if j % N_KV == 1:
                piece = pltpu.roll(piece, HD, 1)
            ilbuf[pl.ds(j, tm, stride=2 * N_KV), :] = piece
        out_ref[...] = ilbuf[:, 0:HD]

    zkv = jnp.dot(hb, w_ref[:, _C_KVC:_C_G], preferred_element_type=F32)
    kvc_ref[...] = zkv[:, 0:2 * KV_W]
    store_rows(kvc_il_ref, (zkv[:, 0:KV_W], zkv[:, KV_W:2 * KV_W]))
    for base, kv_ref, kh_ref, vh_ref in ((2 * KV_W, kvs_ref, ks_ref, vs_ref), (4 * KV_W, kvw_ref, kw_ref, vw_ref)):
        kr = _rope128(zkv[:, base:base + KV_W], cos, sin_s, first)
        vv = zkv[:, base + KV_W:base + 2 * KV_W]
        store_rows(kv_ref, (kr, vv))
        lane = lax.broadcasted_iota(I32, (tm, LANES), 1)
        for k in range(N_KV):
            kh_ref[0, k] = kr[:, k * HD:(k + 1) * HD].astype(BF16)
            vk = vv if k == 0 else pltpu.roll(vv, HD, 1)
            vh_ref[0, k] = jnp.where(lane < HD, vk, jnp.where(lane == HD, 1.0, 0.0)).astype(BF16)

    zg = jnp.dot(hb, w_ref[:, _C_G:_C_END], preferred_element_type=F32)
    gate_ref[...] = jax.nn.sigmoid(zg)


def _proj(x2d, g1, sc, sh, w_pack, w_conv, cos_t, sin_t, *, nb, t, sample, prev=None):
    rows = nb * t
    tm = min(512, rows) if not sample else rows
    tpb = (t // tm) if not sample else 1
    n_tiles = rows // tm
    f = lambda a: jax.ShapeDtypeStruct(a, F32)
    b = lambda a: jax.ShapeDtypeStruct(a, BF16)
    if sample:
        mod_spec = pl.BlockSpec((1, tm, D_MODEL), lambda i: (0, 0, 0))
        tab_spec = pl.BlockSpec((1, LANES), lambda i: (0, 0))
        cst_shape, cst_spec = f((rows, CONV_W)), pl.BlockSpec((tm, CONV_W), lambda i: (0, 0))
        hm = lambda i: (0, 0, i, 0)
        hb_, ht_ = 1, rows
    else:
        mod_spec = pl.BlockSpec((1, 1, D_MODEL), lambda i: (i // tpb, 0, 0))
        tab_spec = pl.BlockSpec((tm, LANES), lambda i: (i % tpb, 0))
        cst_shape, cst_spec = f((nb, SUBLANES, CONV_W)), pl.BlockSpec((1, SUBLANES, CONV_W), lambda i: (i // tpb, 0, 0))
        hm = lambda i: (i // tpb, 0, i % tpb, 0)
        hb_, ht_ = nb, t
    row = lambda w: pl.BlockSpec((tm, w), lambda i: (i, 0))
    in_specs = [row(D_MODEL), pl.BlockSpec((1, D_MODEL), lambda i: (0, 0)), mod_spec, mod_spec,
                pl.BlockSpec((D_MODEL, _C_END), lambda i: (0, 0)),
                pl.BlockSpec((SUBLANES, CONV_W), lambda i: (0, 0)), tab_spec, tab_spec]
    args = [x2d, g1.reshape(1, D_MODEL), sc, sh, w_pack, w_conv, cos_t, sin_t]
    scratch = [pltpu.VMEM((2 * N_KV * tm, LANES), F32)]
    if sample:
        in_specs += [row(CONV_W), row(CONV_W)]
        args += [prev[0], prev[1]]
    else:
        scratch.append(pltpu.VMEM((tm + SUBLANES, CONV_W), F32))
    il_rows = 2 * N_KV * rows
    il = pl.BlockSpec((2 * N_KV * tm, HD), lambda i: (i, 0))
    out_shape = [f((rows, CONV_W)), cst_shape, b((hb_, N_HEADS, ht_, HD)),
                 f((rows, 2 * KV_W)), f((il_rows, HD)), f((il_rows, HD)), f((il_rows, HD)),
                 b((hb_, N_KV, ht_, HD)), b((hb_, N_KV, ht_, LANES)), b((hb_, N_KV, ht_, HD)), b((hb_, N_KV, ht_, LANES)),
                 f((rows, 2 * LANES))]
    out_specs = [row(CONV_W), cst_spec, pl.BlockSpec((1, N_HEADS, tm, HD), hm),
                 row(2 * KV_W), il, il, il,
                 pl.BlockSpec((1, N_KV, tm, HD), hm), pl.BlockSpec((1, N_KV, tm, LANES), hm),
                 pl.BlockSpec((1, N_KV, tm, HD), hm), pl.BlockSpec((1, N_KV, tm, LANES), hm),
                 row(2 * LANES)]
    return pl.pallas_call(
        functools.partial(_proj_kernel, tm=tm, tpb=tpb, sample=sample),
        grid=(n_tiles,), in_specs=in_specs, out_specs=out_specs, out_shape=out_shape,
        scratch_shapes=scratch, compiler_params=_params(1),
        name="proj_sample" if sample else "proj_prompt",
    )(*args)


def _cmpbias_kernel(pos_ref, w_ref, b1_ref, o_ref):
    for c in range(2):
        o_ref[c:c + 1, :] = jnp.sum(pos_ref[c] * w_ref[c], axis=0, keepdims=True) + b1_ref[c:c + 1, :]


def _cmpbias(cmp_pos, cmp_w1, cmp_b1):
    n = cmp_pos.shape[1] * cmp_pos.shape[2]
    return pl.pallas_call(
        _cmpbias_kernel,
        out_shape=jax.ShapeDtypeStruct((2, CMP_HID), F32),
        compiler_params=pltpu.CompilerParams(vmem_limit_bytes=VMEM_LIMIT),
        name="cmpbias",
    )(cmp_pos.reshape(2, n, 1), cmp_w1.reshape(2, n, CMP_HID), cmp_b1)


def _cmp_kernel(pt_ref, *refs, ppt, nsub, tiles):
    n_in = nsub * ppt + 1
    all_pages = refs[:n_in]
    if tiles:
        unfold_ref = refs[n_in]
        refs = refs[1:]
    w1_ref, b1_ref, w2_ref, b2_ref, cos_ref, sin_ref, ck_ref, cv_ref, lhs_all, pbuf = refs[n_in:]
    r = ppt * SUBLANES
    for u in range(nsub):
        _cmp_unfold(all_pages[u * ppt:u * ppt + ppt + 1], unfold_ref if tiles else None, lhs_all.at[u], ppt, tiles)
    for u in range(nsub):
        rows = slice(u * r, (u + 1) * r)
        _cmp_mlp(lhs_all.at[u], pbuf, w1_ref, b1_ref, w2_ref, b2_ref, cos_ref[rows, :], sin_ref[rows, :],
                 ck_ref, cv_ref, rows, r)


def _cmp_unfold(pages, unfold_ref, lhs, ppt, tiles):
    r = ppt * SUBLANES
    rk = r + SUBLANES
    low = lax.broadcasted_iota(I32, (SUBLANES, LANES), 1) < HD

    def tap_tile(j, c, s, y):
        if tiles:
            return y[s * SUBLANES:(s + 1) * SUBLANES, c * KV_W:(c + 1) * KV_W]
        return pages[j][0, :, s * 2 * KV_W + c * KV_W:s * 2 * KV_W + (c + 1) * KV_W]

    for j in range(ppt + 1):
        y = None
        if tiles:
            a = pages[j][...].reshape(2 * KV_W, PAGE).astype(BF16)
            y = lax.dot_general(unfold_ref[...], a, _NT, preferred_element_type=F32)
        for c in range(2):
            for sp in range(CMP_STRIDE // 2):
                t0 = tap_tile(j, c, 2 * sp, y)
                t1 = tap_tile(j, c, 2 * sp + 1, y)
                lhs[c, j * SUBLANES:(j + 1) * SUBLANES, sp * LANES:(sp + 1) * LANES] = (
                    jnp.where(low, t0, pltpu.roll(t1, HD, 1)))
                lhs[c, rk + j * SUBLANES:rk + (j + 1) * SUBLANES, sp * LANES:(sp + 1) * LANES] = (
                    jnp.where(low, pltpu.roll(t0, HD, 1), t1))


def _cmp_mlp(lhs, pbuf, w1_ref, b1_ref, w2_ref, b2_ref, cos, sin_s, ck_ref, cv_ref, rows, r):
    rk = r + SUBLANES
    first = _first_half_mask(r)
    for c in range(2):
        p = jnp.dot(lhs[c].astype(BF16), w1_ref[c], preferred_element_type=F32)
        hids = []
        for k in range(N_KV):
            pbuf[...] = p[k * rk:(k + 1) * rk, CMP_HID:2 * CMP_HID]
            hids.append(p[k * rk:k * rk + r, 0:CMP_HID] + pbuf[pl.ds(1, r), :])
        hid = jnp.concatenate(hids, axis=1) + b1_ref[c]
        act = jax.nn.gelu(hid)
        comp = jnp.dot(act.astype(BF16), w2_ref[c], preferred_element_type=F32) + b2_ref[c]
        if c == 0:
            comp = _rope128(comp, cos, sin_s, first)
            out = ck_ref
        else:
            out = cv_ref
        for k in range(N_KV):
            out[0, k, rows, :] = comp[:, k * HD:(k + 1) * HD]


def _cmp(pages, pt_flat, nb, n_pages, w1p, b1p, w2p, b2p, cos_c, sin_c, name, tiles):
    ppt = min(32, n_pages)
    nsub = 2 if n_pages % (2 * ppt) == 0 else 1
    pps = nsub * ppt
    n_tiles = n_pages // pps
    r = pps * SUBLANES
    n_chunk = n_pages * SUBLANES
    zeros = (0,) * (pages.ndim - 1)

    def page_map(j):
        return lambda b, t, pt: (pt[b * n_pages + t * pps + j],) + zeros

    def next_map(b, t, pt):
        return (pt[b * n_pages + jnp.minimum(t * pps + pps, n_pages - 1)],) + zeros

    page_blk = (None, 2, N_KV, HD, PAGE) if tiles else (1, SUBLANES, CHUNK_ROW)
    in_specs = [pl.BlockSpec(page_blk, page_map(j)) for j in range(pps)]
    in_specs.append(pl.BlockSpec(page_blk, next_map))
    const = lambda shp: pl.BlockSpec(shp, lambda b, t, pt: (0,) * len(shp))
    extra = []
    if tiles:
        row = np.arange(PAGE)
        tok = (row % SUBLANES) * CMP_STRIDE + row // SUBLANES
        extra = [jnp.asarray(tok[:, None] == np.arange(PAGE)[None, :], BF16)]
        in_specs.append(const((PAGE, PAGE)))
    in_specs += [const(w1p.shape), const(b1p.shape), const(w2p.shape), const(b2p.shape),
                 pl.BlockSpec((r, LANES), lambda b, t, pt: (t, 0)), pl.BlockSpec((r, LANES), lambda b, t, pt: (t, 0))]
    hm = pl.BlockSpec((1, N_KV, r, HD), lambda b, t, pt: (b, 0, t, 0))
    grid_spec = pltpu.PrefetchScalarGridSpec(
        num_scalar_prefetch=1, grid=(nb, n_tiles), in_specs=in_specs, out_specs=[hm, hm],
        scratch_shapes=[pltpu.VMEM((nsub, 2, N_KV * (ppt + 1) * SUBLANES, CMP_STRIDE * HD), F32),
                        pltpu.VMEM(((ppt + 1) * SUBLANES, CMP_HID), F32)])
    return pl.pallas_call(
        functools.partial(_cmp_kernel, ppt=ppt, nsub=nsub, tiles=tiles),
        grid_spec=grid_spec,
        out_shape=[jax.ShapeDtypeStruct((nb, N_KV, n_chunk, HD), F32)] * 2,
        compiler_params=_params(2), name=name,
    )(pt_flat, *([pages] * (pps + 1)), *extra, w1p, b1p, w2p, b2p, cos_c, sin_c)


def _softmax_rows(s, valid):
    s = jnp.where(valid, s, NEG_INF)
    m = jnp.max(s, axis=-1, keepdims=True)
    e = jnp.exp2(s - m)
    return e / jnp.sum(e, axis=-1, keepdims=True)


def _attn_p_kernel(q_ref, ck_ref, cv_ref, ks_ref, vs_ref, kw_ref, vw_ref, gate_ref, band_ref, exp_ref, o_ref,
                   *, n_cmp_pad, n_blk, kc, hg, wc):
    qb = pl.program_id(2)
    start = qb * Q_BLOCK
    tpos = start + lax.broadcasted_iota(I32, (Q_BLOCK, 1), 0)
    groups = range(QPK // hg)
    rows = hg * Q_BLOCK

    def q_of(g):
        return q_ref[0, g * hg:(g + 1) * hg].reshape(rows, HD)

    def biased(s, bias):
        width = s.shape[-1]
        return (s.reshape(hg, Q_BLOCK, width) + bias[None]).reshape(rows, width)

    ck = ck_ref[0, 0].astype(BF16)
    cv = cv_ref[0, 0].astype(BF16)
    cmp_end = (lax.broadcasted_iota(I32, (1, n_cmp_pad), 1) + 2) * CMP_STRIDE - 1
    bias_c = jnp.where(cmp_end <= tpos, 0.0, NEG_INF)
    o_c = []
    pcs = jnp.zeros((Q_BLOCK, n_cmp_pad), F32)
    for g in groups:
        s_c = biased(lax.dot_general(q_of(g), ck, _NT, preferred_element_type=F32), bias_c)
        m_c = jnp.maximum(jnp.max(s_c, axis=-1, keepdims=True), 0.5 * NEG_INF)
        e_c = jnp.exp2(s_c - m_c)
        l_c = jnp.sum(e_c, axis=-1, keepdims=True)
        p_c = e_c * (1.0 / jnp.where(l_c > 0.0, l_c, 1.0))
        o_c.append(jnp.dot(p_c.astype(BF16), cv, preferred_element_type=F32))
        for h in range(hg):
            pcs = pcs + p_c[h * Q_BLOCK:(h + 1) * Q_BLOCK]

    imp =lax.dot_general(band_ref[...], pcs, _NT, preferred_element_type=F32,
                          precision=lax.Precision.HIGHEST)
    blk = lax.broadcasted_iota(I32, (n_blk, Q_BLOCK), 0)
    tlane = start + lax.broadcasted_iota(I32, (1, Q_BLOCK), 1)
    cur = tlane // SEL_BLOCK
    causal = blk * SEL_BLOCK <= tlane
    forced = causal & ((blk == 0) | (blk == cur) | (blk == cur - 1))
    score = jnp.where(forced, FORCE_SCORE, jnp.where(causal, imp, -1.0))
    rank = jnp.zeros((n_blk, Q_BLOCK), F32)
    for bp in range(n_blk):
        other = score[bp:bp + 1, :]
        beats = (other > score) | ((other == score) & (bp < blk))
        rank = rank + beats.astype(F32)
    sel_t = (rank < float(min(N_SEL, n_blk))).astype(BF16)
    eye = (lax.broadcasted_iota(I32, (Q_BLOCK, Q_BLOCK), 0)
           == lax.broadcasted_iota(I32, (Q_BLOCK, Q_BLOCK), 1)).astype(BF16)
    sel = lax.dot_general(eye, sel_t, _NT, preferred_element_type=F32).astype(BF16)

    n_chunks = (start + Q_BLOCK + kc - 1) // kc

    def online(state, kj, vj, bias):
        out = []
        for g in groups:
            m_i, acc = state[g]
            s = biased(lax.dot_general(q_of(g), kj, _NT, preferred_element_type=F32), bias)
            m_new = jnp.maximum(m_i, jnp.max(s, axis=-1, keepdims=True))
            p = jnp.exp2(s - m_new).astype(BF16)
            out.append((m_new, jnp.exp2(m_i - m_new) * acc + jnp.dot(p, vj, preferred_element_type=F32)))
        return tuple(out)

    def step(j, state, causal_chunk):
        off = pl.multiple_of(j * kc, kc)
        mexp = jnp.dot(sel, exp_ref[j], preferred_element_type=F32)
        bias = mexp * (-NEG_INF) + NEG_INF
        if causal_chunk:
            keypos = off + lax.broadcasted_iota(I32, (1, kc), 1)
            bias = jnp.where(keypos <= tpos, bias, NEG_INF)
        return online(state, ks_ref[0, 0, pl.ds(off, kc), :], vs_ref[0, 0, pl.ds(off, kc), :], bias)

    init = tuple((jnp.full((rows, 1), NEG_INF, F32), jnp.zeros((rows, LANES), F32)) for _ in groups)
    state = lax.fori_loop(0, n_chunks - 1, lambda j, c: step(j, c, False), init)
    sel_state = step(n_chunks - 1, state, True)

    s0 = jnp.maximum(start - WINDOW, 0)
    win_state = init
    for c in range((WINDOW + Q_BLOCK) // wc):
        off = pl.multiple_of(s0 + c * wc, Q_BLOCK)
        dist = tpos - (off + lax.broadcasted_iota(I32, (1, wc), 1))
        bias_w = jnp.where((dist >= 0) & (dist <= WINDOW), 0.0, NEG_INF)
        win_state = online(win_state, kw_ref[0, 0, pl.ds(off, wc), :], vw_ref[0, 0, pl.ds(off, wc), :], bias_w)

    gt = gate_ref[...]
    for g in groups:
        acc_s = sel_state[g][1]
        acc_w = win_state[g][1]
        o_s = acc_s[:, 0:HD] * (1.0 / acc_s[:, HD:HD + 1])
        o_w = acc_w[:, 0:HD] * (1.0 / acc_w[:, HD:HD + 1])
        for hh in range(hg):
            h = g * hg + hh
            rs = slice(hh * Q_BLOCK, (hh + 1) * Q_BLOCK)
            o = (gt[:, h:h + 1] * o_c[g][rs] + gt[:, QPK + h:QPK + h + 1] * o_s[rs]
                 + gt[:, 2 * QPK + h:2 * QPK + h + 1] * o_w[rs])
            o_ref[0, :, h * HD:(h + 1) * HD] = o


ATTN_HEAD_GROUP = 4
ATTN_KEY_CHUNK = 512
ATTN_WIN_CHUNK = 640


def _attn_prompt(q_hm, ck, cv, ks, vs, kw, vw, gates, band, expand, *, nb, t):
    n_qb = t // Q_BLOCK
    n_cmp_pad = ck.shape[2]
    n_blk = band.shape[0]
    kc = expand.shape[2]
    kv_spec = lambda n, w=HD: pl.BlockSpec((1, 1, n, w), lambda b, k, i: (b, k, 0, 0))
    return pl.pallas_call(
        functools.partial(_attn_p_kernel, n_cmp_pad=n_cmp_pad, n_blk=n_blk, kc=kc, hg=ATTN_HEAD_GROUP, wc=ATTN_WIN_CHUNK),
        grid=(nb, N_KV, n_qb),
        in_specs=[pl.BlockSpec((1, QPK, Q_BLOCK, HD), lambda b, k, i: (b, k, i, 0)),
                  kv_spec(n_cmp_pad), kv_spec(n_cmp_pad), kv_spec(t), kv_spec(t, LANES), kv_spec(t), kv_spec(t, LANES),
                  pl.BlockSpec((Q_BLOCK, LANES), lambda b, k, i: (b * n_qb + i, k)),
                  pl.BlockSpec(band.shape, lambda b, k, i: (0, 0)),
                  pl.BlockSpec(expand.shape, lambda b, k, i: (0, 0, 0))],
        out_specs=pl.BlockSpec((1, Q_BLOCK, QPK * HD), lambda b, k, i: (b, i, k)),
        out_shape=jax.ShapeDtypeStruct((nb, t, ATTN_W), F32),
        compiler_params=_params(3), name="attn_prompt",
    )(q_hm, ck, cv, ks, vs, kw, vw, gates, band, expand)


def _attn_s1_kernel(q_ref, ck_ref, cv_ref, oc_ref, pcs_ref, *, n_chunk, past):
    q = q_ref[0]
    q16 = jnp.concatenate([q, jnp.zeros_like(q)], axis=0).astype(BF16)
    cmp_end = (lax.broadcasted_iota(I32, (1, n_chunk), 1) + 2) * CMP_STRIDE - 1
    valid = cmp_end <= past
    head = lax.broadcasted_iota(I32, (2 * N_HEADS, 1), 0)
    oc = jnp.zeros((2 * N_HEADS, HD), F32)
    pcs = []
    for k in range(N_KV):
        s = lax.dot_general(q16, ck_ref[0, k].astype(BF16), _NT, preferred_element_type=F32)
        p = _softmax_rows(s, valid) * valid.astype(F32)
        in_grp = (head >= k * QPK) & (head < (k + 1) * QPK)
        p = jnp.where(in_grp, p, 0.0)
        oc = oc + jnp.dot(p.astype(BF16), cv_ref[0, k].astype(BF16), preferred_element_type=F32)
        pcs.append(jnp.sum(p, axis=0, keepdims=True))
    oc_ref[0] = oc[0:N_HEADS]
    pcs_ref[0] = jnp.concatenate(pcs + [jnp.zeros((SUBLANES - N_KV, n_chunk), F32)], axis=0)


def _topk_s_kernel(pcs_ref, band_ref, idx_ref, *, n_sel_blocks, past):
    imp = jnp.dot(pcs_ref[...], band_ref[...], preferred_element_type=F32, precision=lax.Precision.HIGHEST)
    rows, nbp = imp.shape
    blk = lax.broadcasted_iota(I32, (rows, nbp), 1)
    cur = past // SEL_BLOCK
    causal = blk * SEL_BLOCK <= past
    forced = causal & ((blk == 0) | (blk == cur) | (blk == cur - 1))
    score = jnp.where(forced, FORCE_SCORE, jnp.where(causal, imp, -1.0))
    score = jnp.where(blk < n_sel_blocks, score, -2.0)
    lane = lax.broadcasted_iota(I32, (rows, LANES), 1)
    out = jnp.zeros((rows, LANES), I32)
    for r in range(min(N_SEL, n_sel_blocks)):
        m = jnp.max(score, axis=-1, keepdims=True)
        pick = jnp.min(jnp.where(score == m, blk, nbp), axis=-1, keepdims=True)
        out = jnp.where(lane == r, pick, out)
        score = jnp.where(blk == pick, -3.0, score)
    idx_ref[...] = out


def _attn_s2_kernel(pt_ref, idx_ref, *refs, n_pages, past, n_sel_blocks):
    ktiles, vtiles = refs[:N_SEL], refs[N_SEL:2 * N_SEL]
    q_ref, oc_ref, kvs_ref, wk_ref, wv_ref, kvw_ref, gate_ref, o_ref, kbuf, vbuf = refs[2 * N_SEL:]
    b = pl.program_id(0)
    k = pl.program_id(1)
    q = q_ref[0]
    q16f = jnp.concatenate([q, jnp.zeros_like(q)], axis=0)
    q16 = q16f.astype(BF16)
    head = lax.broadcasted_iota(I32, (N_HEADS, 1), 0)
    nk = N_SEL * PAGE
    lane = lax.broadcasted_iota(I32, (1, nk), 1)
    slot = lane // PAGE
    new_blk = n_sel_blocks - 1
    wb = wk_ref.shape[-1]
    wpos = past - wb + lax.broadcasted_iota(I32, (1, wb), 1)
    wdist = past - wpos
    valid_w = (wdist >= 0) & (wdist <= WINDOW) & (wpos >= 0)

    def attend(s, valid, v_t, k_new, v_new):
        s_new = jnp.sum(q16f * k_new, axis=-1, keepdims=True)
        s = jnp.where(valid, s, NEG_INF)
        m = jnp.maximum(jnp.max(s, axis=-1, keepdims=True), s_new)
        e = jnp.exp2(s - m)
        e_new = jnp.exp2(s_new - m)
        den = jnp.sum(e, axis=-1, keepdims=True) + e_new
        acc = lax.dot_general(e.astype(BF16), v_t, _NT, preferred_element_type=F32) + e_new * v_new
        return acc / den

    in_grp = (head >= k * QPK) & (head < (k + 1) * QPK)
    bvec = jnp.zeros((1, nk), I32)
    for j in range(N_SEL):
        kbuf[:, j * PAGE:(j + 1) * PAGE] = ktiles[j][...].astype(BF16)
        vbuf[:, j * PAGE:(j + 1) * PAGE] = vtiles[j][...].astype(BF16)
        bvec = jnp.where(slot == j, idx_ref[(b * N_KV + k) * LANES + j], bvec)
    tok = (bvec // 2) * PAGE + lane % PAGE
    valid = (tok // SEL_BLOCK == bvec) & (bvec < new_blk) & (tok <= past)
    s = jnp.dot(q16, kbuf[...], preferred_element_type=F32)
    o_s = attend(s, valid, vbuf[...], kvs_ref[0, pl.ds(k, 1), :], kvs_ref[0, pl.ds(N_KV + k, 1), :])
    sw = jnp.dot(q16, wk_ref[...].astype(BF16), preferred_element_type=F32)
    o_w = attend(sw, valid_w, wv_ref[...].astype(BF16), kvw_ref[0, pl.ds(k, 1), :], kvw_ref[0, pl.ds(N_KV + k, 1), :])
    g = gate_ref[0]
    part = jnp.where(in_grp, g[:, 1:2] * o_s[0:N_HEADS] + g[:, 2:3] * o_w[0:N_HEADS], 0.0)

    @pl.when(k == 0)
    def _():
        o_ref[0] = g[:, 0:1] * oc_ref[0] + part

    @pl.when(k > 0)
    def _():
        o_ref[0] = o_ref[0] + part


def _attn_sample(q3, ck, cv, band_s, sel_t, pt_flat, kvs_rows, win_t, kvw_rows, gates_hm,
                 *, nb, n_pages, past, n_sel_blocks):
    n_chunk = ck.shape[2]
    nbp = band_s.shape[1]
    oc, pcs = pl.pallas_call(
        functools.partial(_attn_s1_kernel, n_chunk=n_chunk, past=past),
        grid=(nb,),
        in_specs=[pl.BlockSpec((1, N_HEADS, HD), lambda b: (b, 0, 0)),
                  pl.BlockSpec((1, N_KV, n_chunk, HD), lambda b: (b, 0, 0, 0)),
                  pl.BlockSpec((1, N_KV, n_chunk, HD), lambda b: (b, 0, 0, 0))],
        out_specs=[pl.BlockSpec((1, N_HEADS, HD), lambda b: (b, 0, 0)),
                   pl.BlockSpec((1, SUBLANES, n_chunk), lambda b: (b, 0, 0))],
        out_shape=[jax.ShapeDtypeStruct((nb, N_HEADS, HD), F32), jax.ShapeDtypeStruct((nb, SUBLANES, n_chunk), F32)],
        compiler_params=_params(1), name="attn_sample_cmp",
    )(q3, ck, cv)
    idx = pl.pallas_call(
        functools.partial(_topk_s_kernel, n_sel_blocks=n_sel_blocks, past=past),
        out_shape=jax.ShapeDtypeStruct((nb * N_KV, LANES), I32),
        compiler_params=pltpu.CompilerParams(vmem_limit_bytes=VMEM_LIMIT), name="topk_sample",
    )(pcs[:, 0:N_KV, :].reshape(nb * N_KV, n_chunk), band_s)
    idx_flat = idx.reshape(-1)
    seq_pages = jnp.minimum(idx[:, 0:N_SEL] // (PAGE // SEL_BLOCK), n_pages - 1).reshape(nb, N_KV * N_SEL)
    page_sel = jnp.take_along_axis(pt_flat.reshape(nb, n_pages), seq_pages, axis=1).reshape(-1)

    def tile_map(c, j):
        def f(b, k, ps, ix):
            return (ps[(b * N_KV + k) * N_SEL + j], c, k, 0, 0)
        return f

    tile = lambda c, j: pl.BlockSpec((None, None, None, HD, PAGE), tile_map(c, j))
    in_specs = [tile(0, j) for j in range(N_SEL)] + [tile(1, j) for j in range(N_SEL)]
    wb = win_t.shape[-1]
    per_b = lambda shp: pl.BlockSpec(shp, lambda b, k, pt, ix: (b, 0, 0))
    in_specs += [per_b((1, N_HEADS, HD)), per_b((1, N_HEADS, HD)), per_b((1, SUBLANES, HD)),
                 pl.BlockSpec((None, None, None, HD, wb), lambda b, k, pt, ix: (b, 0, k, 0, 0)),
                 pl.BlockSpec((None, None, None, HD, wb), lambda b, k, pt, ix: (b, 1, k, 0, 0)),
                 per_b((1, SUBLANES, HD)), per_b((1, N_HEADS, LANES))]
    grid_spec = pltpu.PrefetchScalarGridSpec(
        num_scalar_prefetch=2, grid=(nb, N_KV), in_specs=in_specs,
        out_specs=per_b((1, N_HEADS, HD)),
        scratch_shapes=[pltpu.VMEM((HD, N_SEL * PAGE), BF16), pltpu.VMEM((HD, N_SEL * PAGE), BF16)])
    return pl.pallas_call(
        functools.partial(_attn_s2_kernel, n_pages=n_pages, past=past, n_sel_blocks=n_sel_blocks),
        grid_spec=grid_spec,
        out_shape=jax.ShapeDtypeStruct((nb, N_HEADS, HD), F32),
        compiler_params=_params(2), name="attn_sample_sel",
    )(page_sel, idx_flat, *([sel_t] * (2 * N_SEL)), q3, oc, kvs_rows, win_t, win_t, kvw_rows, gates_hm)


TOK_ROWS = D_MODEL // LANES


def _store_token_tiles(ref, x):
    n = x.shape[0]
    for j in range(TOK_ROWS):
        ref[pl.ds(j, n, stride=TOK_ROWS), :] = x[:, j * LANES:(j + 1) * LANES]


def _load_token_tiles(ref, lead, n):
    return jnp.concatenate([ref[lead + (pl.ds(j, n, stride=TOK_ROWS), slice(None))] for j in range(TOK_ROWS)], axis=1)


def _outp_kernel(xp_ref, convp_ref, attnp_ref, ga1p_ref, sc2p_ref, sh2p_ref,
                 xs_ref, convs_ref, attns_ref, ga1s_ref, sc2s_ref, sh2s_ref,
                 gc_ref, ga_ref, w_ref, g2_ref, wr_ref, x1_ref, hp_ref, lg_ref, *, n_prompt_tiles):
    is_p = pl.program_id(0) < n_prompt_tiles
    pick = lambda a, b: jnp.where(is_p, a, b)
    cn = _rms(pick(convp_ref[...], convs_ref[...]), gc_ref[...])
    an = _rms(pick(attnp_ref[...], attns_ref[...]), ga_ref[...])
    cat = jnp.concatenate([cn, an], axis=1).astype(BF16)
    y = jnp.dot(cat, w_ref[...], preferred_element_type=F32)
    x1 = pick(xp_ref[...], xs_ref[...]) + pick(ga1p_ref[0], ga1s_ref[0]) * y
    x1_ref[...] = x1
    hp = _rms(x1, g2_ref[...]) * (1.0 + pick(sc2p_ref[0], sc2s_ref[0])) + pick(sh2p_ref[0], sh2s_ref[0])
    _store_token_tiles(hp_ref, hp)
    hp_hi = hp.astype(BF16)
    hp_lo = (hp - hp_hi.astype(F32)).astype(BF16)
    lg_ref[...] = (jnp.dot(hp_hi, wr_ref[0], preferred_element_type=F32)
                   + (jnp.dot(hp_hi, wr_ref[1], preferred_element_type=F32)
                      + jnp.dot(hp_lo, wr_ref[0], preferred_element_type=F32)))


TOKEN_TILE = 512


def _outp(prompt, sample, g_conv, g_attn, w_out_b, g2, w_route, *, tpb):
    tm = TOKEN_TILE
    n_p = prompt[0].shape[0] // tm
    total = (n_p + 1) * tm
    last = n_p - 1
    prow = lambda w: pl.BlockSpec((tm, w), lambda i: (jnp.minimum(i, last), 0))
    srow = lambda w: pl.BlockSpec((tm, w), lambda i: (0, 0))
    pmod = pl.BlockSpec((1, 1, D_MODEL), lambda i: (jnp.minimum(i, last) // tpb, 0, 0))
    smod = pl.BlockSpec((1, tm, D_MODEL), lambda i: (0, 0, 0))
    vec = lambda w: pl.BlockSpec((1, w), lambda i: (0, 0))
    row = lambda w: pl.BlockSpec((tm, w), lambda i: (i, 0))
    in_specs = [prow(D_MODEL), prow(CONV_W), prow(ATTN_W), pmod, pmod, pmod,
                srow(D_MODEL), srow(CONV_W), srow(ATTN_W), smod, smod, smod,
                vec(CONV_W), vec(ATTN_W), pl.BlockSpec((D_MODEL, D_MODEL), lambda i: (0, 0)), vec(D_MODEL),
                pl.BlockSpec((2, D_MODEL, LANES), lambda i: (0, 0, 0))]
    return pl.pallas_call(
        functools.partial(_outp_kernel, n_prompt_tiles=n_p),
        grid=(n_p + 1,), in_specs=in_specs,
        out_specs=[row(D_MODEL), pl.BlockSpec((tm * TOK_ROWS, LANES), lambda i: (i, 0)), row(LANES)],
        out_shape=[jax.ShapeDtypeStruct((total, D_MODEL), F32), jax.ShapeDtypeStruct((total * TOK_ROWS, LANES), F32),
                   jax.ShapeDtypeStruct((total, LANES), F32)],
        compiler_params=_params(1), name="outp",
    )(*prompt, *sample, g_conv.reshape(1, -1), g_attn.reshape(1, -1), w_out_b, g2.reshape(1, -1), w_route)


def _route_kernel(lg_ref, bias_ref, tri_ref, o_ref, cnt_ref, carry, *, tm, n_valid):
    i = pl.program_id(0)

    @pl.when(i == 0)
    def _():
        carry[...] = jnp.zeros_like(carry)

    lane = lax.broadcasted_iota(I32, (tm, LANES), 1)
    rowid = i * tm + lax.broadcasted_iota(I32, (tm, 1), 0)
    live = rowid < n_valid
    lg = lg_ref[...] + bias_ref[...]
    is_g = lane < N_GROUPS
    lgg = jnp.where(is_g, lg, NEG_INF)
    gmax = jnp.max(lgg, axis=-1, keepdims=True)
    grp = jnp.min(jnp.where(is_g & (lgg == gmax), lane, LANES), axis=-1, keepdims=True)
    p_grp = 1.0 / jnp.sum(jnp.where(is_g, jnp.exp(lgg - gmax), 0.0), axis=-1, keepdims=True)
    eid = lane - N_GROUPS
    in_grp = (eid >= grp * EPG) & (eid < (grp + 1) * EPG)
    le = jnp.where(in_grp, lg, NEG_INF)
    v1 = jnp.max(le, axis=-1, keepdims=True)
    e1 = jnp.min(jnp.where(in_grp & (le == v1), eid, LANES), axis=-1, keepdims=True)
    le2 = jnp.where(eid == e1, NEG_INF, le)
    v2 = jnp.max(le2, axis=-1, keepdims=True)
    e2 = jnp.min(jnp.where(in_grp & (eid != e1) & (le2 == v2), eid, LANES), axis=-1, keepdims=True)
    ex2 = jnp.exp(v2 - v1)
    w1 = p_grp * (1.0 / (1.0 + ex2))
    w2 = p_grp * (ex2 / (1.0 + ex2))
    oh1 = ((lane == e1) & live).astype(F32)
    oh2 = ((lane == e2) & live).astype(F32)
    both = oh1 + oh2
    before = jnp.dot(tri_ref[...], both.astype(BF16), preferred_element_type=F32) + carry[0:1, :]
    r1 = jnp.sum(oh1 * before, axis=-1, keepdims=True)
    r2 = jnp.sum(oh2 * before, axis=-1, keepdims=True)
    carry[0:1, :] = carry[0:1, :] + jnp.sum(both, axis=0, keepdims=True)
    out = jnp.where(lane == 0, e1.astype(F32), 0.0)
    out = jnp.where(lane == 1, e2.astype(F32), out)
    out = jnp.where(lane == 2, w1, out)
    out = jnp.where(lane == 3, w2, out)
    out = jnp.where(lane == 4, r1, out)
    out = jnp.where(lane == 5, r2, out)
    o_ref[...] = out
    cnt_ref[...] = carry[...]


def _route(logits, bias_row, n_valid):
    total = logits.shape[0]
    tm = TOKEN_TILE
    n_tiles = total // tm
    tri =(np.arange(tm)[:, None] > np.arange(tm)[None, :]).astype(np.float32)
    return pl.pallas_call(
        functools.partial(_route_kernel, tm=tm, n_valid=n_valid),
        grid=(n_tiles,),
        in_specs=[pl.BlockSpec((tm, LANES), lambda i: (i, 0)), pl.BlockSpec((1, LANES), lambda i: (0, 0)),
                  pl.BlockSpec((tm, tm), lambda i: (0, 0))],
        out_specs=[pl.BlockSpec((tm, LANES), lambda i: (i, 0)), pl.BlockSpec((SUBLANES, LANES), lambda i: (0, 0))],
        out_shape=[jax.ShapeDtypeStruct((total, LANES), F32), jax.ShapeDtypeStruct((SUBLANES, LANES), F32)],
        scratch_shapes=[pltpu.VMEM((SUBLANES, LANES), F32)],
        compiler_params=_params(1), name="route",
    )(logits, bias_row, jnp.asarray(tri, BF16))


EXPERT_ROWS = 256
DISPATCH_TILE = 256


def _tile_copy(src, src_row, dst, dst_row, sem):
    return pltpu.make_async_copy(src.at[pl.ds(pl.multiple_of(src_row * TOK_ROWS, TOK_ROWS), TOK_ROWS), :],
                                 dst.at[pl.ds(pl.multiple_of(dst_row * TOK_ROWS, TOK_ROWS), TOK_ROWS), :], sem)


def _dispatch_kernel(zstart_ref, zcnt_ref, dest_ref, x_ref, xb_hbm, stage, zeros, sem, zsem, *, n_tiles, n_blocks):
    i = pl.program_id(0)
    tm = DISPATCH_TILE
    slot = i % 2
    blk_rows = EXPERT_ROWS * TOK_ROWS

    def tail_copy(b):
        off = pl.multiple_of(b * blk_rows, blk_rows)
        return pltpu.make_async_copy(zeros, xb_hbm.at[pl.ds(off, blk_rows), :], zsem.at[1])

    @pl.when(i == 0)
    def _():
        zeros[...] = jnp.zeros_like(zeros)
        first_tail = zstart_ref[N_EXPERTS] // EXPERT_ROWS
        for e in range(N_EXPERTS):
            def fill(r, c, e=e):
                _tile_copy(zeros, 0, xb_hbm, zstart_ref[e] + r, zsem.at[0]).start()
                return c
            lax.fori_loop(0, zcnt_ref[e], fill, 0)
        lax.fori_loop(first_tail, n_blocks, lambda b, c: (tail_copy(b).start(), c)[1], 0)
        for e in range(N_EXPERTS):
            def drain(r, c):
                _tile_copy(zeros, 0, xb_hbm, 0, zsem.at[0]).wait()
                return c
            lax.fori_loop(0, zcnt_ref[e], drain, 0)
        lax.fori_loop(first_tail, n_blocks, lambda b, c: (tail_copy(b).wait(), c)[1], 0)

    def wait_rows(s):
        for _ in range(2 * tm):
            _tile_copy(stage.at[s], 0, xb_hbm, 0, sem.at[s]).wait()

    @pl.when(i >= 2)
    def _():
        wait_rows(slot)

    stage[slot] = x_ref[...]
    for r in range(tm):
        for k in range(2):
            dst = xb_hbm.at[pl.ds(pl.multiple_of(dest_ref[0, k, r], TOK_ROWS), TOK_ROWS), :]
            pltpu.make_async_copy(stage.at[slot, pl.ds(r * TOK_ROWS, TOK_ROWS), :], dst,
                                  sem.at[slot]).start(priority=k)

    @pl.when(i == n_tiles - 1)
    def _():
        wait_rows(slot)
        if n_tiles > 1:
            wait_rows(1 - slot)


def _dispatch(dest_pad, zstart, zcnt, hp_all, n_rows, n_blocks):
    tm = DISPATCH_TILE
    n_tiles = dest_pad.shape[1] // tm
    grid_spec = pltpu.PrefetchScalarGridSpec(
        num_scalar_prefetch=2, grid=(n_tiles,),
        in_specs=[pl.BlockSpec((1, 2, tm), lambda i, zs, zc: (i, 0, 0), memory_space=pltpu.SMEM),
                  pl.BlockSpec((tm * TOK_ROWS, LANES), lambda i, zs, zc: (i, 0))],
        out_specs=pl.BlockSpec(memory_space=pl.ANY),
        scratch_shapes=[pltpu.VMEM((2, tm * TOK_ROWS, LANES), F32), pltpu.VMEM((EXPERT_ROWS * TOK_ROWS, LANES), F32),
                        pltpu.SemaphoreType.DMA((2,)), pltpu.SemaphoreType.DMA((2,))])
    return pl.pallas_call(
        functools.partial(_dispatch_kernel, n_tiles=n_tiles, n_blocks=n_blocks),
        grid_spec=grid_spec,
        out_shape=jax.ShapeDtypeStruct((n_rows * TOK_ROWS, LANES), F32),
        compiler_params=_params(1), name="dispatch",
    )(zstart, zcnt, dest_pad.reshape(2, n_tiles, tm).transpose(1, 0, 2), hp_all)


def _experts_kernel(blk_e_ref, x_ref, wg_ref, wu_ref, wd_ref, o_ref, wg_b, wu_b, wd_b):
    i = pl.program_id(0)
    changed = jnp.logical_or(i == 0, blk_e_ref[i] != blk_e_ref[jnp.maximum(i - 1, 0)])

    @pl.when(changed)
    def _():
        wg_b[...] = wg_ref[0].astype(BF16)
        wu_b[...] = wu_ref[0].astype(BF16)
        wd_b[...] = wd_ref[0].astype(BF16)

    x = _load_token_tiles(x_ref, (), EXPERT_ROWS).astype(BF16)
    g = jnp.dot(x, wg_b[...], preferred_element_type=F32)
    u = jnp.dot(x, wu_b[...], preferred_element_type=F32)
    h = (g * jax.nn.sigmoid(g)) * u
    _store_token_tiles(o_ref, jnp.dot(h.astype(BF16), wd_b[...], preferred_element_type=F32))


def _experts(blk_e, xb, w_gate, w_up, w_down, n_blocks):
    blk = pl.BlockSpec((EXPERT_ROWS * TOK_ROWS, LANES), lambda i, be: (i, 0))
    grid_spec = pltpu.PrefetchScalarGridSpec(
        num_scalar_prefetch=1, grid=(n_blocks,),
        in_specs=[blk,
                  pl.BlockSpec((1, D_MODEL, D_EXPERT), lambda i, be: (be[i], 0, 0)),
                  pl.BlockSpec((1, D_MODEL, D_EXPERT), lambda i, be: (be[i], 0, 0)),
                  pl.BlockSpec((1, D_EXPERT, D_MODEL), lambda i, be: (be[i], 0, 0))],
        out_specs=blk,
        scratch_shapes=[pltpu.VMEM((D_MODEL, D_EXPERT), BF16), pltpu.VMEM((D_MODEL, D_EXPERT), BF16),
                        pltpu.VMEM((D_EXPERT, D_MODEL), BF16)])
    return pl.pallas_call(
        _experts_kernel,
        grid_spec=grid_spec,
        out_shape=jax.ShapeDtypeStruct((n_blocks * EXPERT_ROWS * TOK_ROWS, LANES), F32),
        compiler_params=_params(1), name="experts",
    )(blk_e, xb, w_gate, w_up, w_down)


def _final_kernel(dest_first_ref, dest_next_ref, yb_hbm, x1_ref, wt_ref, gate2_ref, gf_ref, o_ref, ybuf, sem,
                  *, tm, n_tiles):
    i = pl.program_id(0)
    slot = i % 2

    def issue(dest_ref, s):
        for r in range(tm):
            for k in range(2):
                d = dest_ref[0, k, r]
                src = yb_hbm.at[pl.ds(pl.multiple_of(d, TOK_ROWS), TOK_ROWS), :]
                pltpu.make_async_copy(src, ybuf.at[s, k, pl.ds(r * TOK_ROWS, TOK_ROWS), :],
                                      sem.at[s]).start(priority=k)

    @pl.when(i == 0)
    def _():
        issue(dest_first_ref, 0)

    @pl.when(i + 1 < n_tiles)
    def _():
        issue(dest_next_ref, 1 - slot)

    for r in range(tm):
        for k in range(2):
            pltpu.make_async_copy(yb_hbm.at[pl.ds(0, TOK_ROWS), :], ybuf.at[slot, k, pl.ds(r * TOK_ROWS, TOK_ROWS), :],
                                  sem.at[slot]).wait()
    wt = wt_ref[...]
    f = wt[:, 2:3] * _load_token_tiles(ybuf, (slot, 0), tm) + wt[:, 3:4] * _load_token_tiles(ybuf, (slot, 1), tm)
    x2 = x1_ref[...] + gate2_ref[0] * f
    o_ref[...] = _rms(x2, gf_ref[...])


def _final(dest_pad, yb, x1_all, route_rows, gate2, final_g, *, rows, tpb, per_row, row0):
    tm = min(256, rows)
    n_tiles = rows // tm
    blk0 = row0 // tm
    dest3 = dest_pad.reshape(2, -1, tm).transpose(1, 0, 2)
    idx_blk = lambda f: pl.BlockSpec((1, 2, tm), f, memory_space=pltpu.SMEM)
    mod = (pl.BlockSpec((1, tm, D_MODEL), lambda i: (0, i, 0)) if per_row
           else pl.BlockSpec((1, 1, D_MODEL), lambda i: (i // tpb, 0, 0)))
    return pl.pallas_call(
        functools.partial(_final_kernel, tm=tm, n_tiles=n_tiles),
        grid=(n_tiles,),
        in_specs=[idx_blk(lambda i: (blk0, 0, 0)),
                  idx_blk(lambda i: (blk0 + jnp.minimum(i + 1, n_tiles - 1), 0, 0)),
                  pl.BlockSpec(memory_space=pl.ANY),
                  pl.BlockSpec((tm, D_MODEL), lambda i: (blk0 + i, 0)),
                  pl.BlockSpec((tm, LANES), lambda i: (blk0 + i, 0)),
                  mod, pl.BlockSpec((1, D_MODEL), lambda i: (0, 0))],
        out_specs=pl.BlockSpec((tm, D_MODEL), lambda i: (i, 0)),
        scratch_shapes=[pltpu.VMEM((2, 2, tm * TOK_ROWS, LANES), F32), pltpu.SemaphoreType.DMA((2,))],
        out_shape=jax.ShapeDtypeStruct((rows, D_MODEL), F32),
        compiler_params=_params(1), name="final_sample" if per_row else "final_prompt",
    )(dest3, dest3, yb, x1_all, route_rows, gate2, final_g.reshape(1, -1))


def _rope_tables(pos):
    inv = ROPE_THETA ** (-jnp.arange(HALF, dtype=F32) / HALF)
    ang = pos.astype(F32)[:, None] * inv[None, :]
    cos = jnp.tile(jnp.cos(ang), (1, LANES // HALF))
    sin = jnp.sin(ang)
    sin_s = jnp.tile(jnp.concatenate([-sin, sin], axis=1), (1, LANES // HD))
    return cos, sin_s


def _pack_w_in(w_in):
    gl = w_in[:, _C_G:_C_G + 3 * N_HEADS].reshape(D_MODEL, 3, N_KV, QPK)
    gcols = []
    for k in range(N_KV):
        gk = gl[:, :, k, :].reshape(D_MODEL, 3 * QPK)
        gcols.append(jnp.pad(gk, ((0, 0), (0, LANES - 3 * QPK))))
    return jnp.concatenate([w_in[:, :_C_G]] + gcols, axis=1).astype(BF16)


def _pack_cmp_weights(cmp_w1, cmp_w2, bias, cmp_b2):
    w1 = cmp_w1.reshape(2, 2, CMP_STRIDE, HD, CMP_HID)
    eye = jnp.eye(N_KV, dtype=F32)
    w1p = w1.transpose(0, 2, 3, 1, 4).reshape(2, CMP_STRIDE * HD, 2 * CMP_HID)
    w2p =jnp.einsum('chd,pk->cphkd', cmp_w2, eye).reshape(2, N_KV * CMP_HID, KV_W)
    b1p = jnp.tile(bias, (1, N_KV)).reshape(2, 1, N_KV * CMP_HID)
    b2p = jnp.tile(cmp_b2, (1, N_KV)).reshape(2, 1, KV_W)
    return w1p.astype(BF16), b1p, w2p.astype(BF16), b2p


def _band(n_cmp_pad, n_cmp, n_blk_pad, n_blk):
    n = np.arange(n_cmp_pad)[:, None]
    b = np.arange(n_blk_pad)[None, :]
    r = SEL_BLOCK // CMP_STRIDE
    m = (n >= r * b - 1) & (n <= r * b + r - 1) & (n < n_cmp) & (b < n_blk)
    return jnp.asarray(m.astype(np.float32))


def _expand(t, kc):
    n_chunks = t // kc
    key = np.arange(t).reshape(n_chunks, 1, kc)
    blk = np.arange(t // SEL_BLOCK).reshape(1, -1, 1)
    return jnp.asarray((key // SEL_BLOCK == blk).astype(np.float32), BF16)


def kernel(x_prompt, x_sample, c_prompt, c_sample, cache_cmp_kv, cache_sel_kv, cache_win_kv, state_conv, page_table,
           ln1_g, ln2_g, w_ada, b_ada, w_in, w_conv, cmp_pos, cmp_w1, cmp_b1, cmp_w2, cmp_b2, g_out_conv, g_out_attn,
           w_out, w_route_group, b_route_group, w_route_expert, b_route_expert, w_gate, w_up, w_down, final_g):
    depth = w_in.shape[0]
    assert depth == 1, "single-layer step"
    nb, t, _ = x_prompt.shape
    ns, ts, _ = x_sample.shape
    assert ts == 1 and t % 512 == 0 and t >= WINDOW + Q_BLOCK
    n_pool = cache_cmp_kv.shape[1]
    n_pages = page_table.shape[1]
    past = n_pages * PAGE
    wb = cache_win_kv.shape[2]
    assert wb == WINDOW
    l = 0

    n_c = nb + ns
    c_all = jnp.pad(jnp.concatenate([c_prompt, c_sample], axis=0), ((0, (-n_c) % SUBLANES), (0, 0)))
    mods = _ada(c_all, w_ada[l], b_ada[l])
    sh1, sc1, ga1, sh2, sc2, ga2 = [mods[:, j * D_MODEL:(j + 1) * D_MODEL] for j in range(6)]
    pr = lambda a: a[0:nb].reshape(nb, 1, D_MODEL)
    sr = lambda a: a[nb:nb + ns].reshape(1, ns, D_MODEL)

    w_pack = _pack_w_in(w_in[l])
    wconv8 = jnp.pad(w_conv[l], ((0, SUBLANES - CONV_K), (0, 0)))
    cos_p, sin_p = _rope_tables(jnp.arange(t, dtype=I32))
    cos_s, sin_s = _rope_tables(jnp.full((1,), past, I32))
    xp2 = x_prompt.reshape(nb * t, D_MODEL)
    xs2 = x_sample.reshape(ns, D_MODEL)
    (conv_p, cst_p, q_p, kvc_p, kvc_rows_p, kvs_rows_p, kvw_rows_p, ks_p, vs_p, kw_p, vw_p, gates_p) = _proj(
        xp2, ln1_g[l], pr(sc1), pr(sh1), w_pack, wconv8, cos_p, sin_p, nb=nb, t=t, sample=False)
    (conv_s, cst_s, q_s, _, kvc_rows_s, kvs_rows_s, kvw_rows_s, _, _, _, _, gates_s) = _proj(
        xs2, ln1_g[l], sr(sc1), sr(sh1), w_pack, wconv8, cos_s, sin_s, nb=ns, t=1, sample=True,
        prev=(state_conv[l][:, 0], state_conv[l][:, 1]))

    bias = _cmpbias(cmp_pos[l], cmp_w1[l], cmp_b1[l])
    w1p, b1p, w2p, b2p = _pack_cmp_weights(cmp_w1[l], cmp_w2[l], bias, cmp_b2[l])
    pp = t // PAGE
    cos_cp, sin_cp = _rope_tables((jnp.arange(t // CMP_STRIDE, dtype=I32) + 2) * CMP_STRIDE - 1)
    ck_p, cv_p = _cmp(kvc_p.reshape(nb * pp, SUBLANES, CHUNK_ROW), jnp.arange(nb * pp, dtype=I32), nb, pp,
                      w1p, b1p, w2p, b2p, cos_cp, sin_cp, "cmp_prompt", tiles=False)
    pt_flat = page_table.reshape(-1).astype(I32)
    cos_cs, sin_cs = _rope_tables((jnp.arange(past // CMP_STRIDE, dtype=I32) + 2) * CMP_STRIDE - 1)
    to_tiles = lambda a: a.transpose(0, 2, 3, 4, 1)
    ck_s, cv_s = _cmp(to_tiles(cache_cmp_kv[l]), pt_flat, ns, n_pages,
                      w1p, b1p, w2p, b2p, cos_cs, sin_cs, "cmp_sample", tiles=True)

    n_chunk_p = t // CMP_STRIDE
    n_blk_p = t // SEL_BLOCK
    band_p = _band(n_chunk_p, n_chunk_p - 1, n_blk_p, n_blk_p)
    attn_p = _attn_prompt(q_p, ck_p, cv_p, ks_p, vs_p, kw_p, vw_p, gates_p, band_p.T, _expand(t, ATTN_KEY_CHUNK), nb=nb, t=t)

    n_chunk_s = past // CMP_STRIDE
    n_sel_s = -(-(past + 1) // SEL_BLOCK)
    nbp = -(-n_sel_s // LANES) * LANES
    band_s = _band(n_chunk_s, (past + 1) // CMP_STRIDE - 1, nbp, n_sel_s)
    q3 = q_s.reshape(N_HEADS, ns, HD).transpose(1, 0, 2).astype(F32)
    gs = gates_s.reshape(ns, N_KV, LANES)[:, :, :3 * QPK].reshape(ns, N_KV, 3, QPK)
    gates_hm = jnp.pad(gs.transpose(0, 1, 3, 2).reshape(ns, N_HEADS, 3), ((0, 0), (0, 0), (0, LANES - 3)))
    rpt = 2 * N_KV
    new_rows = lambda a: jnp.pad(a.reshape(ns, rpt, HD), ((0, 0), (0, SUBLANES - rpt), (0, 0)))
    attn_s = _attn_sample(q3, ck_s, cv_s, band_s, to_tiles(cache_sel_kv[l]), pt_flat,
                          new_rows(kvs_rows_s), to_tiles(cache_win_kv[l]), new_rows(kvw_rows_s),
                          gates_hm, nb=ns, n_pages=n_pages, past=past, n_sel_blocks=n_sel_s).reshape(ns, ATTN_W)

    total = nb * t + ns
    w_out_b = w_out[l].astype(BF16)
    w_route = jnp.pad(jnp.concatenate([w_route_group[l], w_route_expert[l]], axis=1),
                      ((0, 0), (0, LANES - N_GROUPS - N_EXPERTS)))
    w_route_hi = w_route.astype(BF16)
    w_route = jnp.stack([w_route_hi, (w_route - w_route_hi.astype(F32)).astype(BF16)])
    b_route = jnp.pad(jnp.concatenate([b_route_group[l], b_route_expert[l]]), (0, LANES - N_GROUPS - N_EXPERTS))
    tile_pad = lambda a: jnp.pad(a, ((0, TOKEN_TILE - ns), (0, 0)))
    smod = lambda a: tile_pad(a[nb:nb + ns]).reshape(1, TOKEN_TILE, D_MODEL)
    x1_all, hp_all, lg_all = _outp(
        (xp2, conv_p, attn_p.reshape(nb * t, ATTN_W), pr(ga1), pr(sc2), pr(sh2)),
        (tile_pad(xs2), tile_pad(conv_s), tile_pad(attn_s), smod(ga1), smod(sc2), smod(sh2)),
        g_out_conv[l], g_out_attn[l], w_out_b, ln2_g[l], w_route, tpb=t // TOKEN_TILE)

    route, counts = _route(lg_all, b_route.reshape(1, LANES), total)
    route_t = route[:total, 0:SUBLANES].T.astype(I32)
    e = route_t[0:2]
    rank = route_t[4:6]
    cnt = counts[0, :N_EXPERTS].astype(I32)
    padded = (cnt + EXPERT_ROWS - 1) // EXPERT_ROWS * EXPERT_ROWS
    pad_end = jnp.cumsum(padded)
    pad_start = pad_end - padded
    m_slots = total * 2
    n_blocks = -(-(m_slots + N_EXPERTS * (EXPERT_ROWS - 1)) // EXPERT_ROWS)
    n_slots = n_blocks * EXPERT_ROWS
    first_slot = sum(jnp.where(e == j, pad_start[j], 0) for j in range(N_EXPERTS))
    dest = jnp.clip(first_slot + rank, 0, n_slots - 1)
    blk_start = jnp.arange(n_blocks, dtype=I32) * EXPERT_ROWS
    blk_e = jnp.minimum(jnp.sum((pad_end[None, :] <= blk_start[:, None]).astype(I32), axis=1), N_EXPERTS - 1)
    n_dump = 2 * (x1_all.shape[0] - total)
    dest_pad = jnp.concatenate([dest, n_slots + jnp.arange(n_dump, dtype=I32).reshape(2, -1)], axis=1)

    zstart = jnp.concatenate([pad_start + cnt, pad_end[-1:]])
    zcnt = jnp.concatenate([padded - cnt, jnp.zeros((1,), I32)])
    dest_pad = dest_pad * TOK_ROWS
    xb = _dispatch(dest_pad, zstart, zcnt, hp_all, n_slots + n_dump, n_blocks)
    yb = _experts(blk_e, xb, w_gate[l], w_up[l], w_down[l], n_blocks)
    y_p = _final(dest_pad, yb, x1_all, route, pr(ga2), final_g, rows=nb * t, tpb=t // 256, per_row=False, row0=0)
    y_s = _final(dest_pad, yb, x1_all, route, sr(ga2), final_g, rows=ns, tpb=1, per_row=True, row0=nb * t)

    kv_shape = (2, N_KV, HD)
    y_prompt = y_p.reshape(nb, t, D_MODEL)
    y_sample = y_s.reshape(ns, 1, D_MODEL)
    new_cmp_prompt = kvc_rows_p.reshape((1, nb, t) + kv_shape)
    new_cmp_sample = kvc_rows_s.reshape((1, ns, 1) + kv_shape)
    new_sel_prompt = kvs_rows_p.reshape((1, nb, t) + kv_shape)
    new_sel_sample = kvs_rows_s.reshape((1, ns, 1) + kv_shape)
    new_win_prompt = kvw_rows_p.reshape((nb, t) + kv_shape)[:, t - WINDOW:][None]
    new_win_sample = jnp.concatenate([cache_win_kv[l][:, 1:], kvw_rows_s.reshape((ns, 1) + kv_shape)], axis=1)[None]
    new_conv_prompt = cst_p[:, SUBLANES - (CONV_K - 1):][None]
    new_conv_sample = jnp.stack([state_conv[l][:, 1], cst_s], axis=1)[None]
    return (y_prompt, y_sample, new_cmp_prompt, new_cmp_sample, new_sel_prompt, new_sel_sample,
            new_win_prompt, new_win_sample, new_conv_prompt, new_conv_sample)
```
